```python
import jax, jax.numpy as jnp
from jax import lax
import numpy as np

D_MODEL = 1024
BATCH = 32
SEQ = 2048
DEPTH = 1

CTX_LEN = 256
GRID_W = 64

A_HEADS = 4
A_DK = 128
A_DV = 128
A_KW = A_HEADS * A_DK
A_VW = A_HEADS * A_DV
A_CHUNK = 64

B_GROUPS = 4
B_GW = 128
B_W = B_GROUPS * B_GW
B_CHUNK = 2 * GRID_W

D_FF = -(-8 * D_MODEL // (3 * 256)) * 256

N_MOD = 6
EPS = 1e-6

CTX_STATE_COLS = 2 * A_KW + A_VW
SPLIT_IDX = (A_KW, 2 * A_KW, 2 * A_KW + A_VW, 3 * A_KW + A_VW, 3 * A_KW + 2 * A_VW,
             3 * A_KW + 2 * A_VW + B_W, 3 * A_KW + 2 * A_VW + 2 * B_W,
             3 * A_KW + 2 * A_VW + 2 * B_W + D_MODEL)
IN_COLS = 3 * A_KW + 2 * A_VW + 2 * B_W + 2 * D_MODEL

kernel_name = 'hybrid_hgrn2_chunkmlp_dit_block'


def rmsnorm(x, g):
    xf = x.astype(jnp.float32)
    y = xf * lax.rsqrt(jnp.mean(xf * xf, axis=-1, keepdims=True) + EPS)
    return (y * g.astype(jnp.float32)).astype(x.dtype)


def layernorm(x, g, b):
    xf = x.astype(jnp.float32)
    mu = jnp.mean(xf, axis=-1, keepdims=True)
    var = jnp.mean(jnp.square(xf - mu), axis=-1, keepdims=True)
    y = (xf - mu) * lax.rsqrt(var + EPS) * g.astype(jnp.float32) + b.astype(jnp.float32)
    return y.astype(x.dtype)


def modulate(h, shift, scale):
    return h * (1 + scale) + shift


def split_heads(t):
    return t.reshape(t.shape[0], t.shape[1], A_HEADS, -1).astype(jnp.float32)


def flip_seq(t):
    return jnp.flip(t, axis=1)


def forget_gate(f_logit, lb):
    f = lb + (1 - lb) * jax.nn.sigmoid(f_logit.astype(jnp.float32))
    return jnp.log(f), 1 - f


def hgrn2_chunked(q, k, logf, v, s0):
    b_, L = q.shape[:2]
    n = L // A_CHUNK
    rs = lambda t: t.reshape(b_, n, A_CHUNK, *t.shape[2:])
    q, k, logf, v = rs(q), rs(k), rs(logf), rs(v)
    bcum = jnp.cumsum(logf, axis=2)
    blast = bcum[:, :, -1:]
    mid = 0.5 * blast
    q_in = q * jnp.exp(bcum - mid)
    k_in = k * jnp.exp(mid - bcum)
    scores = jnp.einsum('bnthd,bnshd->bnhts', q_in, k_in)
    mask = jnp.tril(jnp.ones((A_CHUNK, A_CHUNK), dtype=bool))
    scores = jnp.where(mask, scores, 0.0)
    o_intra = jnp.einsum('bnhts,bnshe->bnthe', scores, v)
    d_state = jnp.einsum('bnshd,bnshe->bnhde', k * jnp.exp(blast - bcum), v)
    decay = jnp.exp(blast[:, :, 0])

    def step(s, inp):
        a, d = inp
        return a[..., None] * s + d, s

    s_final, s_in = lax.scan(step, s0, (jnp.moveaxis(decay, 1, 0), jnp.moveaxis(d_state, 1, 0)))
    s_in = jnp.moveaxis(s_in, 0, 1)
    o_inter = jnp.einsum('bnthd,bnhde->bnthe', q * jnp.exp(bcum), s_in)
    o = (o_intra + o_inter).reshape(b_, L, A_HEADS, A_DV)
    return o, s_final


def hgrn2_final_state(k, logf, v):
    bcum = jnp.cumsum(logf, axis=1)
    w = k * jnp.exp(bcum[:, -1:] - bcum)
    return jnp.einsum('blhd,blhe->bhde', w, v)


def chunk_sgu(u, v, ln_g, ln_b, w_s, b_s, n_chunks):
    b_, L = u.shape[:2]
    v = layernorm(v, ln_g, ln_b)
    vc = v.reshape(b_, n_chunks, B_CHUNK, B_GROUPS, B_GW)
    mixed = jnp.einsum('gts,bnsgc->bntgc', w_s, vc) + jnp.transpose(b_s)[:, :, None]
    return u * mixed.reshape(b_, L, B_W)


def token_mixers(p, lb, g_norm_a, ln_v_g, ln_v_b, w_s, b_s, w_pa, w_pb, w_o, s0_f, s0_b, n_chunks_b):
    f_f, f_b, i_in, q, og, u, v, ga, gb = jnp.split(p, SPLIT_IDX, axis=-1)
    b_, L = p.shape[:2]
    logf_f, k_f = forget_gate(f_f, lb[0])
    logf_b, k_b = forget_gate(f_b, lb[1])
    qh, vh = split_heads(q), split_heads(i_in)
    o_f, s_f = hgrn2_chunked(qh, split_heads(k_f), split_heads(logf_f), vh, s0_f)
    o_b, s_b = hgrn2_chunked(flip_seq(qh), flip_seq(split_heads(k_b)), flip_seq(split_heads(logf_b)),
                             flip_seq(vh), s0_b)
    o_a = rmsnorm(o_f + flip_seq(o_b), g_norm_a).reshape(b_, L, A_VW).astype(p.dtype)
    o_a = o_a * jax.nn.silu(og)
    o_b_mlp = chunk_sgu(jax.nn.gelu(u), jax.nn.gelu(v), ln_v_g, ln_v_b, w_s, b_s, n_chunks_b)
    merged = jax.nn.sigmoid(ga) * (o_a @ w_pa) + jax.nn.sigmoid(gb) * (o_b_mlp @ w_pb)
    return merged @ w_o, s_f, s_b


def swiglu(h, w_up, w_down):
    a, b = jnp.split(h @ w_up, 2, axis=-1)
    return (jax.nn.silu(a) * b) @ w_down


def _fwd_setup_inputs(seed: int = 0) -> dict:
    key = jax.random.key(seed)
    ks = jax.random.split(key, 21)
    nrm = lambda k, shape, s: jax.random.normal(k, shape, jnp.float32) * s
    return {
        'x': nrm(ks[0], (BATCH, SEQ, D_MODEL), 1.0),
        'c': nrm(ks[1], (BATCH, D_MODEL), 1.0),
        'ctx': nrm(ks[2], (BATCH, CTX_LEN, D_MODEL), 1.0),
        'c_ctx': nrm(ks[3], (D_MODEL,), 1.0),
        'w_mod': nrm(ks[4], (DEPTH, D_MODEL, N_MOD * D_MODEL), 0.5 * D_MODEL ** -0.5),
        'b_mod': nrm(ks[5], (DEPTH, N_MOD * D_MODEL), 0.01),
        'g_mix': 1.0 + nrm(ks[6], (DEPTH, D_MODEL), 0.1),
        'g_ffn': 1.0 + nrm(ks[7], (DEPTH, D_MODEL), 0.1),
        'w_in': nrm(ks[8], (DEPTH, D_MODEL, IN_COLS), D_MODEL ** -0.5),
        'lb_gamma': nrm(ks[9], (DEPTH + 1, 2, A_KW), 0.5),
        'g_norm_a': 1.0 + nrm(ks[10], (DEPTH, A_DV), 0.1),
        'ln_v_g': 1.0 + nrm(ks[11], (DEPTH, B_W), 0.1),
        'ln_v_b': nrm(ks[12], (DEPTH, B_W), 0.02),
        'w_s': nrm(ks[13], (DEPTH, B_GROUPS, B_CHUNK, B_CHUNK), 0.5 * B_CHUNK ** -0.5),
        'b_s': 1.0 + nrm(ks[14], (DEPTH, B_GROUPS, B_CHUNK), 0.1),
        'w_pa': nrm(ks[15], (DEPTH, A_VW, D_MODEL), A_VW ** -0.5),
        'w_pb': nrm(ks[16], (DEPTH, B_W, D_MODEL), B_W ** -0.5),
        'w_o': nrm(ks[17], (DEPTH, D_MODEL, D_MODEL), D_MODEL ** -0.5),
        'w_up': nrm(ks[18], (DEPTH, D_MODEL, 2 * D_FF), D_MODEL ** -0.5),
        'w_down': nrm(ks[19], (DEPTH, D_FF, D_MODEL), D_FF ** -0.5),
        'g_final': 1.0 + nrm(ks[20], (D_MODEL,), 0.1),
    }


def _fwd_reference(x, c, ctx, c_ctx, w_mod, b_mod, g_mix, g_ffn, w_in, lb_gamma, g_norm_a,
              ln_v_g, ln_v_b, w_s, b_s, w_pa, w_pb, w_o, w_up, w_down, g_final):
    bsz, L = x.shape[0], x.shape[1]
    rows = L // GRID_W
    n_chunks_lat = rows // 2
    n_chunks_ctx = ctx.shape[1] // B_CHUNK
    lb_all = jnp.cumsum(jax.nn.softmax(lb_gamma.astype(jnp.float32), axis=0), axis=0)
    for l in range(DEPTH):
        last = l == DEPTH - 1
        lb = lb_all[l]
        mod = (jax.nn.silu(c) @ w_mod[l] + b_mod[l]).reshape(bsz, N_MOD, D_MODEL)
        mc = (jax.nn.silu(c_ctx) @ w_mod[l] + b_mod[l]).reshape(N_MOD, D_MODEL)
        hc = modulate(rmsnorm(ctx, g_mix[l]), mc[0], mc[1])
        if last:
            pc = hc @ w_in[l][:, :CTX_STATE_COLS]
            f_f, f_b, i_c = jnp.split(pc, (A_KW, 2 * A_KW), axis=-1)
            logf_f, k_f = forget_gate(f_f, lb[0])
            logf_b, k_b = forget_gate(f_b, lb[1])
            vh = split_heads(i_c)
            s_ctx_f = hgrn2_final_state(split_heads(k_f), split_heads(logf_f), vh)
            s_ctx_b = hgrn2_final_state(flip_seq(split_heads(k_b)), flip_seq(split_heads(logf_b)), flip_seq(vh))
        else:
            zeros = jnp.zeros((bsz, A_HEADS, A_DK, A_DV), jnp.float32)
            mix_c, s_ctx_f, s_ctx_b = token_mixers(hc @ w_in[l], lb, g_norm_a[l], ln_v_g[l], ln_v_b[l],
                                                   w_s[l], b_s[l], w_pa[l], w_pb[l], w_o[l],
                                                   zeros, zeros, n_chunks_ctx)
            ctx = ctx + mc[2] * mix_c
            hc2 = modulate(rmsnorm(ctx, g_ffn[l]), mc[3], mc[4])
            ctx = ctx + mc[5] * swiglu(hc2, w_up[l], w_down[l])
        h = modulate(rmsnorm(x, g_mix[l]), mod[:, 0, None], mod[:, 1, None])
        mix_x, _, _ = token_mixers(h @ w_in[l], lb, g_norm_a[l], ln_v_g[l], ln_v_b[l],
                                   w_s[l], b_s[l], w_pa[l], w_pb[l], w_o[l],
                                   s_ctx_f, s_ctx_b, n_chunks_lat)
        x = x + mod[:, 2, None] * mix_x
        h2 = modulate(rmsnorm(x, g_ffn[l]), mod[:, 3, None], mod[:, 4, None])
        x = x + mod[:, 5, None] * swiglu(h2, w_up[l], w_down[l])
    return rmsnorm(x, g_final)


import jax as _jax
import jax.numpy as _jnp

TWIN_FORMAT = 'train_step'
FWD_PARAMS = ['x', 'c', 'ctx', 'c_ctx', 'w_mod', 'b_mod', 'g_mix', 'g_ffn', 'w_in', 'lb_gamma', 'g_norm_a', 'ln_v_g', 'ln_v_b', 'w_s', 'b_s', 'w_pa', 'w_pb', 'w_o', 'w_up', 'w_down', 'g_final']
TWIN_WEIGHTS = ['c_ctx', 'w_mod', 'b_mod', 'g_mix', 'g_ffn', 'w_in', 'lb_gamma', 'g_norm_a', 'ln_v_g', 'ln_v_b', 'w_s', 'b_s', 'w_pa', 'w_pb', 'w_o', 'w_up', 'w_down', 'g_final']
TWIN_DIFF_INPUT = 'x'
TWIN_INPUTS = ['x', 'c', 'ctx', 'c_ctx', 'w_mod', 'b_mod', 'g_mix', 'g_ffn', 'w_in', 'lb_gamma', 'g_norm_a', 'ln_v_g', 'ln_v_b', 'w_s', 'b_s', 'w_pa', 'w_pb', 'w_o', 'w_up', 'w_down', 'g_final', 'loss_target', 'm_c_ctx', 'm_w_mod', 'm_b_mod', 'm_g_mix', 'm_g_ffn', 'm_w_in', 'm_lb_gamma', 'm_g_norm_a', 'm_ln_v_g', 'm_ln_v_b', 'm_w_s', 'm_b_s', 'm_w_pa', 'm_w_pb', 'm_w_o', 'm_w_up', 'm_w_down', 'm_g_final', 'v_c_ctx', 'v_w_mod', 'v_b_mod', 'v_g_mix', 'v_g_ffn', 'v_w_in', 'v_lb_gamma', 'v_g_norm_a', 'v_ln_v_g', 'v_ln_v_b', 'v_w_s', 'v_b_s', 'v_w_pa', 'v_w_pb', 'v_w_o', 'v_w_up', 'v_w_down', 'v_g_final']
TWIN_OUTPUTS = ['loss', 'grad_x', 'grad_c_ctx', 'grad_w_mod', 'grad_b_mod', 'grad_g_mix', 'grad_g_ffn', 'grad_w_in', 'grad_lb_gamma', 'grad_g_norm_a', 'grad_ln_v_g', 'grad_ln_v_b', 'grad_w_s', 'grad_b_s', 'grad_w_pa', 'grad_w_pb', 'grad_w_o', 'grad_w_up', 'grad_w_down', 'grad_g_final', 'delta_c_ctx', 'delta_w_mod', 'delta_b_mod', 'delta_g_mix', 'delta_g_ffn', 'delta_w_in', 'delta_lb_gamma', 'delta_g_norm_a', 'delta_ln_v_g', 'delta_ln_v_b', 'delta_w_s', 'delta_b_s', 'delta_w_pa', 'delta_w_pb', 'delta_w_o', 'delta_w_up', 'delta_w_down', 'delta_g_final', 'new_m_c_ctx', 'new_m_w_mod', 'new_m_b_mod', 'new_m_g_mix', 'new_m_g_ffn', 'new_m_w_in', 'new_m_lb_gamma', 'new_m_g_norm_a', 'new_m_ln_v_g', 'new_m_ln_v_b', 'new_m_w_s', 'new_m_b_s', 'new_m_w_pa', 'new_m_w_pb', 'new_m_w_o', 'new_m_w_up', 'new_m_w_down', 'new_m_g_final', 'new_v_c_ctx', 'new_v_w_mod', 'new_v_b_mod', 'new_v_g_mix', 'new_v_g_ffn', 'new_v_w_in', 'new_v_lb_gamma', 'new_v_g_norm_a', 'new_v_ln_v_g', 'new_v_ln_v_b', 'new_v_w_s', 'new_v_b_s', 'new_v_w_pa', 'new_v_w_pb', 'new_v_w_o', 'new_v_w_up', 'new_v_w_down', 'new_v_g_final']
TWIN_LEAF_KINDS = {'loss': 'loss', 'grad_x': 'grad_x', 'grad_c_ctx': 'grad_w', 'grad_w_mod': 'grad_w', 'grad_b_mod': 'grad_w', 'grad_g_mix': 'grad_w', 'grad_g_ffn': 'grad_w', 'grad_w_in': 'grad_w', 'grad_lb_gamma': 'grad_w', 'grad_g_norm_a': 'grad_w', 'grad_ln_v_g': 'grad_w', 'grad_ln_v_b': 'grad_w', 'grad_w_s': 'grad_w', 'grad_b_s': 'grad_w', 'grad_w_pa': 'grad_w', 'grad_w_pb': 'grad_w', 'grad_w_o': 'grad_w', 'grad_w_up': 'grad_w', 'grad_w_down': 'grad_w', 'grad_g_final': 'grad_w', 'delta_c_ctx': 'delta_w', 'delta_w_mod': 'delta_w', 'delta_b_mod': 'delta_w', 'delta_g_mix': 'delta_w', 'delta_g_ffn': 'delta_w', 'delta_w_in': 'delta_w', 'delta_lb_gamma': 'delta_w', 'delta_g_norm_a': 'delta_w', 'delta_ln_v_g': 'delta_w', 'delta_ln_v_b': 'delta_w', 'delta_w_s': 'delta_w', 'delta_b_s': 'delta_w', 'delta_w_pa': 'delta_w', 'delta_w_pb': 'delta_w', 'delta_w_o': 'delta_w', 'delta_w_up': 'delta_w', 'delta_w_down': 'delta_w', 'delta_g_final': 'delta_w', 'new_m_c_ctx': 'new_m', 'new_m_w_mod': 'new_m', 'new_m_b_mod': 'new_m', 'new_m_g_mix': 'new_m', 'new_m_g_ffn': 'new_m', 'new_m_w_in': 'new_m', 'new_m_lb_gamma': 'new_m', 'new_m_g_norm_a': 'new_m', 'new_m_ln_v_g': 'new_m', 'new_m_ln_v_b': 'new_m', 'new_m_w_s': 'new_m', 'new_m_b_s': 'new_m', 'new_m_w_pa': 'new_m', 'new_m_w_pb': 'new_m', 'new_m_w_o': 'new_m', 'new_m_w_up': 'new_m', 'new_m_w_down': 'new_m', 'new_m_g_final': 'new_m', 'new_v_c_ctx': 'new_v', 'new_v_w_mod': 'new_v', 'new_v_b_mod': 'new_v', 'new_v_g_mix': 'new_v', 'new_v_g_ffn': 'new_v', 'new_v_w_in': 'new_v', 'new_v_lb_gamma': 'new_v', 'new_v_g_norm_a': 'new_v', 'new_v_ln_v_g': 'new_v', 'new_v_ln_v_b': 'new_v', 'new_v_w_s': 'new_v', 'new_v_b_s': 'new_v', 'new_v_w_pa': 'new_v', 'new_v_w_pb': 'new_v', 'new_v_w_o': 'new_v', 'new_v_w_up': 'new_v', 'new_v_w_down': 'new_v', 'new_v_g_final': 'new_v'}


def _forward(args):
    return _fwd_reference(*[args[k] for k in FWD_PARAMS])


def _output_shape():
    out = _jax.eval_shape(lambda: _forward(_fwd_setup_inputs(0)))
    return out.shape, out.dtype

N_MICROBATCH = 1
ADAM_LR = 0.001
ADAM_B1 = 0.9
ADAM_B2 = 0.999
ADAM_EPS = 1e-08
ADAM_WD = 0.01
ADAM_STEP = 10
PER_EXAMPLE_BATCH_AXIS = {'x': 0, 'c': 0, 'ctx': 0, 'loss_target': 0}
SHARED_INPUTS = []
_WEIGHT_DTYPES = {'c_ctx': _jnp.float32, 'w_mod': _jnp.float32, 'b_mod': _jnp.float32, 'g_mix': _jnp.float32, 'g_ffn': _jnp.float32, 'w_in': _jnp.float32, 'lb_gamma': _jnp.float32, 'g_norm_a': _jnp.float32, 'ln_v_g': _jnp.float32, 'ln_v_b': _jnp.float32, 'w_s': _jnp.float32, 'b_s': _jnp.float32, 'w_pa': _jnp.float32, 'w_pb': _jnp.float32, 'w_o': _jnp.float32, 'w_up': _jnp.float32, 'w_down': _jnp.float32, 'g_final': _jnp.float32}
MOMENT_SCALE = {'c_ctx': 1.576776e-03, 'w_mod': 1.575424e-01, 'b_mod': 2.956907e-01, 'g_mix': 8.024449e-02, 'g_ffn': 8.076714e-02, 'w_in': 3.521791e-02, 'lb_gamma': 1.315982e-02, 'g_norm_a': 8.078717e-02, 'ln_v_g': 2.223757e-02, 'ln_v_b': 2.186464e-02, 'w_s': 4.508406e-02, 'b_s': 4.530497e-02, 'w_pa': 2.854292e-02, 'w_pb': 3.773067e-02, 'w_o': 4.807577e-02, 'w_up': 3.482433e-02, 'w_down': 5.757442e-02, 'g_final': 6.447827e+01}


def _to_microbatches(a, axis):
    t = _jnp.moveaxis(a, axis, 0)
    t = t.reshape((N_MICROBATCH, t.shape[0] // N_MICROBATCH) + t.shape[1:])
    return _jnp.moveaxis(t, 1, axis + 1)


def setup_inputs(seed: int = 0) -> dict:
    inp = _fwd_setup_inputs(seed)
    key = _jax.random.fold_in(_jax.random.key(seed), 7919)
    shape, _ = _output_shape()
    out = dict(inp)
    out["loss_target"] = _jax.random.normal(_jax.random.fold_in(key, 0), shape, _jnp.float32)
    for i, name in enumerate(TWIN_WEIGHTS):
        w = inp[name].astype(_jnp.float32)
        if MOMENT_SCALE is None:
            s = _jnp.sqrt(_jnp.mean(_jnp.square(w)) + 1e-30)
        else:
            s = MOMENT_SCALE[name]
        km, kv = _jax.random.split(_jax.random.fold_in(key, i + 1))
        out[name] = w
        out["m_" + name] = s * _jax.random.normal(km, w.shape, _jnp.float32)
        out["v_" + name] = (s * s) * _jax.random.uniform(kv, w.shape, _jnp.float32, 0.5, 1.5)
    if N_MICROBATCH > 1:
        for name, axis in PER_EXAMPLE_BATCH_AXIS.items():
            out[name] = _to_microbatches(out[name], axis)
    return {'x': out['x'], 'c': out['c'], 'ctx': out['ctx'], 'c_ctx': out['c_ctx'], 'w_mod': out['w_mod'], 'b_mod': out['b_mod'], 'g_mix': out['g_mix'], 'g_ffn': out['g_ffn'], 'w_in': out['w_in'], 'lb_gamma': out['lb_gamma'], 'g_norm_a': out['g_norm_a'], 'ln_v_g': out['ln_v_g'], 'ln_v_b': out['ln_v_b'], 'w_s': out['w_s'], 'b_s': out['b_s'], 'w_pa': out['w_pa'], 'w_pb': out['w_pb'], 'w_o': out['w_o'], 'w_up': out['w_up'], 'w_down': out['w_down'], 'g_final': out['g_final'], 'loss_target': out['loss_target'], 'm_c_ctx': out['m_c_ctx'], 'm_w_mod': out['m_w_mod'], 'm_b_mod': out['m_b_mod'], 'm_g_mix': out['m_g_mix'], 'm_g_ffn': out['m_g_ffn'], 'm_w_in': out['m_w_in'], 'm_lb_gamma': out['m_lb_gamma'], 'm_g_norm_a': out['m_g_norm_a'], 'm_ln_v_g': out['m_ln_v_g'], 'm_ln_v_b': out['m_ln_v_b'], 'm_w_s': out['m_w_s'], 'm_b_s': out['m_b_s'], 'm_w_pa': out['m_w_pa'], 'm_w_pb': out['m_w_pb'], 'm_w_o': out['m_w_o'], 'm_w_up': out['m_w_up'], 'm_w_down': out['m_w_down'], 'm_g_final': out['m_g_final'], 'v_c_ctx': out['v_c_ctx'], 'v_w_mod': out['v_w_mod'], 'v_b_mod': out['v_b_mod'], 'v_g_mix': out['v_g_mix'], 'v_g_ffn': out['v_g_ffn'], 'v_w_in': out['v_w_in'], 'v_lb_gamma': out['v_lb_gamma'], 'v_g_norm_a': out['v_g_norm_a'], 'v_ln_v_g': out['v_ln_v_g'], 'v_ln_v_b': out['v_ln_v_b'], 'v_w_s': out['v_w_s'], 'v_b_s': out['v_b_s'], 'v_w_pa': out['v_w_pa'], 'v_w_pb': out['v_w_pb'], 'v_w_o': out['v_w_o'], 'v_w_up': out['v_w_up'], 'v_w_down': out['v_w_down'], 'v_g_final': out['v_g_final']}


def _loss(weights, diff, rest, loss_target):
    with _jax.named_scope("forward"):
        args = {**rest, TWIN_DIFF_INPUT: diff, **{k: w.astype(_WEIGHT_DTYPES[k]) for k, w in weights.items()}}
        y = _forward(args)
    with _jax.named_scope("loss_head"):
        err = _jnp.square(y.astype(_jnp.float32) - loss_target)
        return 0.5 * _jnp.sum(_jnp.mean(err, axis=-1)) if err.ndim else 0.5 * err


def _adamw(w, g, m, v):
    m = ADAM_B1 * m + (1.0 - ADAM_B1) * g
    v = ADAM_B2 * v + (1.0 - ADAM_B2) * _jnp.square(g)
    m_hat = m / (1.0 - ADAM_B1 ** ADAM_STEP)
    v_hat = v / (1.0 - ADAM_B2 ** ADAM_STEP)
    delta = -ADAM_LR * (m_hat / (_jnp.sqrt(v_hat) + ADAM_EPS) + ADAM_WD * w)
    return delta, m, v


def reference(x, c, ctx, c_ctx, w_mod, b_mod, g_mix, g_ffn, w_in, lb_gamma, g_norm_a, ln_v_g, ln_v_b, w_s, b_s, w_pa, w_pb, w_o, w_up, w_down, g_final, loss_target, m_c_ctx, m_w_mod, m_b_mod, m_g_mix, m_g_ffn, m_w_in, m_lb_gamma, m_g_norm_a, m_ln_v_g, m_ln_v_b, m_w_s, m_b_s, m_w_pa, m_w_pb, m_w_o, m_w_up, m_w_down, m_g_final, v_c_ctx, v_w_mod, v_b_mod, v_g_mix, v_g_ffn, v_w_in, v_lb_gamma, v_g_norm_a, v_ln_v_g, v_ln_v_b, v_w_s, v_b_s, v_w_pa, v_w_pb, v_w_o, v_w_up, v_w_down, v_g_final):
    given = dict(x=x, c=c, ctx=ctx, c_ctx=c_ctx, w_mod=w_mod, b_mod=b_mod, g_mix=g_mix, g_ffn=g_ffn, w_in=w_in, lb_gamma=lb_gamma, g_norm_a=g_norm_a, ln_v_g=ln_v_g, ln_v_b=ln_v_b, w_s=w_s, b_s=b_s, w_pa=w_pa, w_pb=w_pb, w_o=w_o, w_up=w_up, w_down=w_down, g_final=g_final, loss_target=loss_target, m_c_ctx=m_c_ctx, m_w_mod=m_w_mod, m_b_mod=m_b_mod, m_g_mix=m_g_mix, m_g_ffn=m_g_ffn, m_w_in=m_w_in, m_lb_gamma=m_lb_gamma, m_g_norm_a=m_g_norm_a, m_ln_v_g=m_ln_v_g, m_ln_v_b=m_ln_v_b, m_w_s=m_w_s, m_b_s=m_b_s, m_w_pa=m_w_pa, m_w_pb=m_w_pb, m_w_o=m_w_o, m_w_up=m_w_up, m_w_down=m_w_down, m_g_final=m_g_final, v_c_ctx=v_c_ctx, v_w_mod=v_w_mod, v_b_mod=v_b_mod, v_g_mix=v_g_mix, v_g_ffn=v_g_ffn, v_w_in=v_w_in, v_lb_gamma=v_lb_gamma, v_g_norm_a=v_g_norm_a, v_ln_v_g=v_ln_v_g, v_ln_v_b=v_ln_v_b, v_w_s=v_w_s, v_b_s=v_b_s, v_w_pa=v_w_pa, v_w_pb=v_w_pb, v_w_o=v_w_o, v_w_up=v_w_up, v_w_down=v_w_down, v_g_final=v_g_final)
    weights = {n: given[n] for n in TWIN_WEIGHTS}
    shared = {n: given[n] for n in SHARED_INPUTS}
    per_example = {n: given[n] for n in ['x', 'c', 'ctx']}
    grad_fn = _jax.value_and_grad(_loss, argnums=(0, 1))

    def one_microbatch(ex, loss_target):
        ex = dict(ex)
        diff = ex.pop(TWIN_DIFF_INPUT)
        return grad_fn(weights, diff, {**shared, **ex}, loss_target)

    if N_MICROBATCH == 1:
        loss, (grad_w, grad_x) = one_microbatch(per_example, given["loss_target"])
    else:
        def body(carry, xs):
            loss_sum, grad_sum = carry
            l_k, (gw_k, gx_k) = one_microbatch(xs[0], xs[1])
            with _jax.named_scope("update"):
                return (loss_sum + l_k, _jax.tree.map(_jnp.add, grad_sum, gw_k)), gx_k

        init = (_jnp.zeros((), _jnp.float32), _jax.tree.map(_jnp.zeros_like, weights))
        (loss, grad_w), grad_x = _jax.lax.scan(body, init, (per_example, given["loss_target"]))
    with _jax.named_scope("update"):
        delta_w, new_m, new_v = {}, {}, {}
        for n in TWIN_WEIGHTS:
            delta_w[n], new_m[n], new_v[n] = _adamw(weights[n], grad_w[n], given["m_" + n], given["v_" + n])
    return (loss, grad_x, *[grad_w[n] for n in TWIN_WEIGHTS], *[delta_w[n] for n in TWIN_WEIGHTS],
            *[new_m[n] for n in TWIN_WEIGHTS], *[new_v[n] for n in TWIN_WEIGHTS])
```

```python
import functools
import math

import jax
import jax.numpy as jnp
from jax import lax
from jax.experimental import pallas as pl
from jax.experimental.pallas import tpu as pltpu

F32 = jnp.float32
BF16 = jnp.bfloat16
SDS = jax.ShapeDtypeStruct
MESH = pl.DeviceIdType.MESH

EPS = 1e-6
D_MODEL = 1024
N_HEADS = 4
HEAD_DIM = 128
KW = N_HEADS * HEAD_DIM
IN_COLS = 11 * KW
D_FF = 2816
HGRN_CHUNK = 64
SGU_CHUNK = 128
ROW_BLOCK = 256
N_CHIPS = 4
N_DEV = 8
V7X_VMEM_BYTES = 64 * 1024 * 1024
VMEM_LIMIT = V7X_VMEM_BYTES - 6 * 1024 * 1024

ADAM_LR, ADAM_B1, ADAM_B2, ADAM_EPS, ADAM_WD, ADAM_STEP = 0.001, 0.9, 0.999, 1e-08, 0.01, 10
GELU_C0 = math.sqrt(2.0 / math.pi)
GELU_C1 = 0.044715

VMEM_SPEC = pl.BlockSpec(memory_space=pltpu.VMEM)
ANY_SPEC = pl.BlockSpec(memory_space=pl.ANY)


def _params(n_grid):
    return pltpu.CompilerParams(dimension_semantics=("arbitrary",) * n_grid, vmem_limit_bytes=VMEM_LIMIT)


def _sig(x):
    return 1.0 / (1.0 + jnp.exp(-x))


def _gelu(x):
    t = jnp.tanh(GELU_C0 * (x + GELU_C1 * x * x * x))
    return 0.5 * x * (1.0 + t), t


def _dgelu(x, t):
    return 0.5 * (1.0 + t) + 0.5 * x * (1.0 - t * t) * GELU_C0 * (1.0 + 3.0 * GELU_C1 * x * x)


def _dot(a, b):
    return jnp.dot(a.astype(BF16), b.astype(BF16), preferred_element_type=F32)


def _dot_nt(a, b):
    return lax.dot_general(a.astype(BF16), b.astype(BF16), (((1,), (1,)), ((), ())), preferred_element_type=F32)


def _dot_tn(a, b):
    return lax.dot_general(a.astype(BF16), b.astype(BF16), (((0,), (0,)), ((), ())), preferred_element_type=F32)


def _dot_f32(a, b, dims=(((1,), (0,)), ((), ()))):
    return lax.dot_general(a, b, dims, precision=lax.Precision.HIGHEST, preferred_element_type=F32)


def _rms(x):
    r = lax.rsqrt(jnp.mean(x * x, axis=-1, keepdims=True) + EPS)
    return x * r, r


def _rms_bwd(dxn, xn, r):
    return r * (dxn - xn * jnp.mean(dxn * xn, axis=-1, keepdims=True))


def _colsum(a):
    return jnp.sum(a, axis=0, keepdims=True)


def _tri(n, upper):
    t = lax.broadcasted_iota(jnp.int32, (n, n), 0)
    s = lax.broadcasted_iota(jnp.int32, (n, n), 1)
    return (s >= t) if upper else (s <= t)


def _all_gather8(x_shard, name):
    m_per, n = x_shard.shape

    def body(x_ref, out_ref, send_sems, recv_sems, local_sem):
        x, y, c = lax.axis_index("x"), lax.axis_index("y"), lax.axis_index("c")
        me, sibling = (x, y, c), (x, y, 1 - c)
        chips = [(1 - x, y), (x, 1 - y), (1 - x, 1 - y)]

        def rows(px, py, pc):
            return out_ref.at[pl.ds((4 * px + 2 * py + pc) * m_per, m_per), :]

        def copy(k, block, to, src=None):
            return pltpu.make_async_remote_copy(
                src_ref=rows(*block) if src is None else src, dst_ref=rows(*block),
                send_sem=send_sems.at[k], recv_sem=recv_sems.at[k], device_id=to, device_id_type=MESH)

        mine = pltpu.make_async_copy(x_ref, rows(*me), local_sem)
        mine.start()
        first = [copy(0, me, sibling, src=x_ref)]
        first += [copy(1 + j, me, (*chip, c), src=x_ref) for j, chip in enumerate(chips)]
        for cp in first:
            cp.start()
        passed = [copy(4 + j, (*chip, c), sibling) for j, chip in enumerate(chips)]
        for j, chip in enumerate(chips):
            copy(1 + j, (*chip, c), me).wait_recv()
            passed[j].start()
        copy(0, sibling, me).wait_recv()
        for j, chip in enumerate(chips):
            copy(4 + j, (*chip, 1 - c), me).wait_recv()
        for cp in first + passed:
            cp.wait_send()
        mine.wait()

    return pl.pallas_call(
        body, name=name, out_shape=SDS((N_DEV * m_per, n), x_shard.dtype),
        in_specs=[VMEM_SPEC], out_specs=VMEM_SPEC,
        scratch_shapes=[pltpu.SemaphoreType.DMA((7,)), pltpu.SemaphoreType.DMA((7,)), pltpu.SemaphoreType.DMA],
    )(x_shard)


def _mesh_pos():
    x, y, c = lax.axis_index("x"), lax.axis_index("y"), lax.axis_index("c")
    chips = [(1 - x, y), (x, 1 - y), (1 - x, 1 - y)]
    return x, y, c, 2 * x + y, (x, y, 1 - c), chips


def _half_rows(c, rh):
    return pl.ds(pl.multiple_of(c * rh, 16), rh)


def _gather_weights(shards):
    n = len(shards)

    def body(*refs):
        ins, outs = refs[:n], refs[n:2 * n]
        send_sems, recv_sems, local_sems = refs[2 * n:]
        x, y, c, kc, sibling, chips = _mesh_pos()
        firsts, passed, locs = [], [], []
        for wi in range(n):
            rh = ins[wi].shape[0] // 2
            lc = pltpu.make_async_copy(ins[wi], outs[wi].at[kc], local_sems.at[wi])
            lc.start()
            locs.append(lc)
            for jj, chip in enumerate(chips):
                cp = pltpu.make_async_remote_copy(
                    src_ref=ins[wi].at[_half_rows(c, rh), :], dst_ref=outs[wi].at[kc, _half_rows(c, rh), :],
                    send_sem=send_sems.at[wi, jj], recv_sem=recv_sems.at[wi, jj],
                    device_id=(*chip, c), device_id_type=MESH)
                cp.start()
                firsts.append(cp)
        for wi in range(n):
            rh = ins[wi].shape[0] // 2
            for jj, chip in enumerate(chips):
                blk = outs[wi].at[2 * chip[0] + chip[1], _half_rows(c, rh), :]
                pltpu.make_async_remote_copy(
                    src_ref=blk, dst_ref=blk, send_sem=send_sems.at[wi, jj], recv_sem=recv_sems.at[wi, jj],
                    device_id=(*chip, c), device_id_type=MESH).wait_recv()
                fw = pltpu.make_async_remote_copy(
                    src_ref=blk, dst_ref=blk, send_sem=send_sems.at[wi, 3 + jj], recv_sem=recv_sems.at[wi, 3 + jj],
                    device_id=sibling, device_id_type=MESH)
                fw.start()
                passed.append(fw)
        for wi in range(n):
            rh = ins[wi].shape[0] // 2
            for jj, chip in enumerate(chips):
                blk = outs[wi].at[2 * chip[0] + chip[1], _half_rows(1 - c, rh), :]
                pltpu.make_async_remote_copy(
                    src_ref=blk, dst_ref=blk, send_sem=send_sems.at[wi, 3 + jj], recv_sem=recv_sems.at[wi, 3 + jj],
                    device_id=sibling, device_id_type=MESH).wait_recv()
        for cp in firsts + passed:
            cp.wait_send()
        for lc in locs:
            lc.wait()

    return pl.pallas_call(
        body, name="gather_weights",
        out_shape=[SDS((N_CHIPS,) + s.shape, s.dtype) for s in shards],
        in_specs=[ANY_SPEC] * n, out_specs=[ANY_SPEC] * n,
        scratch_shapes=[pltpu.SemaphoreType.DMA((n, 6)), pltpu.SemaphoreType.DMA((n, 6)),
                        pltpu.SemaphoreType.DMA((n,))],
    )(*shards)


def _rs_sibling_halves(grads):
    n = len(grads)

    def body(*refs):
        ins, outs = refs[:n], refs[n:2 * n]
        send_sems, recv_sems = refs[2 * n:]
        x, y, c, kc, sibling, chips = _mesh_pos()
        cps = []
        for wi in range(n):
            rh = ins[wi].shape[1] // 2
            cp = pltpu.make_async_remote_copy(
                src_ref=ins[wi].at[:, _half_rows(1 - c, rh), :], dst_ref=outs[wi],
                send_sem=send_sems.at[wi], recv_sem=recv_sems.at[wi], device_id=sibling, device_id_type=MESH)
            cp.start()
            cps.append(cp)
        for cp in cps:
            cp.wait()

    return pl.pallas_call(
        body, name="rs_sibling_halves",
        out_shape=[SDS((N_CHIPS, g.shape[1] // 2, g.shape[2]), F32) for g in grads],
        in_specs=[ANY_SPEC] * n, out_specs=[ANY_SPEC] * n,
        scratch_shapes=[pltpu.SemaphoreType.DMA((n,)), pltpu.SemaphoreType.DMA((n,))],
    )(*grads)


def _rs_to_owner(cpbfs):
    n = len(cpbfs)

    def body(*refs):
        ins, outs = refs[:n], refs[n:2 * n]
        send_sems, recv_sems = refs[2 * n:]
        x, y, c, kc, sibling, chips = _mesh_pos()
        cps = []
        for wi in range(n):
            for jj, chip in enumerate(chips):
                cp = pltpu.make_async_remote_copy(
                    src_ref=ins[wi].at[2 * chip[0] + chip[1]], dst_ref=outs[wi].at[kc],
                    send_sem=send_sems.at[wi, jj], recv_sem=recv_sems.at[wi, jj],
                    device_id=(*chip, c), device_id_type=MESH)
                cp.start()
                cps.append(cp)
        for wi in range(n):
            for jj, chip in enumerate(chips):
                slot = outs[wi].at[2 * chip[0] + chip[1]]
                pltpu.make_async_remote_copy(
                    src_ref=slot, dst_ref=slot, send_sem=send_sems.at[wi, jj], recv_sem=recv_sems.at[wi, jj],
                    device_id=(*chip, c), device_id_type=MESH).wait_recv()
        for cp in cps:
            cp.wait_send()

    return pl.pallas_call(
        body, name="rs_to_owner",
        out_shape=[SDS(g.shape, BF16) for g in cpbfs],
        in_specs=[ANY_SPEC] * n, out_specs=[ANY_SPEC] * n,
        scratch_shapes=[pltpu.SemaphoreType.DMA((n, 3)), pltpu.SemaphoreType.DMA((n, 3))],
    )(*cpbfs)


def _rs_join_halves(gtots):
    n = len(gtots)

    def body(*refs):
        ins, outs = refs[:n], refs[n:2 * n]
        send_sems, recv_sems, local_sems = refs[2 * n:]
        x, y, c, kc, sibling, chips = _mesh_pos()
        cps, locs = [], []
        for wi in range(n):
            rh = ins[wi].shape[0]
            lc = pltpu.make_async_copy(ins[wi], outs[wi].at[_half_rows(c, rh), :], local_sems.at[wi])
            lc.start()
            locs.append(lc)
            cp = pltpu.make_async_remote_copy(
                src_ref=ins[wi], dst_ref=outs[wi].at[_half_rows(c, rh), :],
                send_sem=send_sems.at[wi], recv_sem=recv_sems.at[wi], device_id=sibling, device_id_type=MESH)
            cp.start()
            cps.append(cp)
        for cp in cps:
            cp.wait()
        for lc in locs:
            lc.wait()

    return pl.pallas_call(
        body, name="rs_join_halves",
        out_shape=[SDS((2 * g.shape[0], g.shape[1]), F32) for g in gtots],
        in_specs=[ANY_SPEC] * n, out_specs=[ANY_SPEC] * n,
        scratch_shapes=[pltpu.SemaphoreType.DMA((n,)), pltpu.SemaphoreType.DMA((n,)),
                        pltpu.SemaphoreType.DMA((n,))],
    )(*gtots)


def _rs_add_halves(pos, grad, recv, name):
    _, rs, cs = grad.shape
    rh = rs // 2
    rb = rh // 2

    def body(pos_ref, g_ref, r_ref, o32_ref, obf_ref):
        s = g_ref[...] + r_ref[...]
        o32_ref[...] = s
        obf_ref[...] = s.astype(BF16)

    blk = (1, rb, cs)
    return pl.pallas_call(
        body, name=name,
        grid_spec=pltpu.PrefetchScalarGridSpec(
            num_scalar_prefetch=1, grid=(N_CHIPS, 2),
            in_specs=[pl.BlockSpec(blk, lambda k, i, p: (k, p[1] * 2 + i, 0)),
                      pl.BlockSpec(blk, lambda k, i, p: (k, i, 0))],
            out_specs=[pl.BlockSpec(blk, lambda k, i, p: (k, i, 0)), pl.BlockSpec(blk, lambda k, i, p: (k, i, 0))]),
        out_shape=[SDS((N_CHIPS, rh, cs), F32), SDS((N_CHIPS, rh, cs), BF16)],
        compiler_params=_params(2),
    )(pos, grad, recv)


def _rs_sum_owner(pos, cp32, recv3, name):
    _, rh, cs = cp32.shape
    rb = rh // 2

    def body(pos_ref, own_ref, r1_ref, r2_ref, r3_ref, o_ref):
        o_ref[...] = ((own_ref[0] + r1_ref[0].astype(F32)) + r2_ref[0].astype(F32)) + r3_ref[0].astype(F32)

    blk = (1, rb, cs)

    def slot(d):
        return pl.BlockSpec(blk, lambda i, p: ((p[0] + d) % N_CHIPS, i, 0))

    return pl.pallas_call(
        body, name=name,
        grid_spec=pltpu.PrefetchScalarGridSpec(
            num_scalar_prefetch=1, grid=(2,),
            in_specs=[slot(0), slot(1), slot(2), slot(3)],
            out_specs=pl.BlockSpec((rb, cs), lambda i, p: (i, 0))),
        out_shape=SDS((rh, cs), F32),
        compiler_params=_params(1),
    )(pos, cp32, recv3, recv3, recv3)


def _cast_bf16(arrs):
    n = len(arrs)

    def body(*refs):
        for i in range(n):
            refs[n + i][...] = refs[i][...].astype(BF16)

    return pl.pallas_call(
        body, name="cast_bf16", out_shape=[SDS(a.shape, BF16) for a in arrs],
        in_specs=[VMEM_SPEC] * n, out_specs=[VMEM_SPEC] * n,
        compiler_params=pltpu.CompilerParams(vmem_limit_bytes=VMEM_LIMIT),
    )(*arrs)


def _adamw_vals(w, g, m, v):
    m2 = ADAM_B1 * m + (1.0 - ADAM_B1) * g
    v2 = ADAM_B2 * v + (1.0 - ADAM_B2) * (g * g)
    m_hat = m2 / (1.0 - ADAM_B1 ** ADAM_STEP)
    v_hat = v2 / (1.0 - ADAM_B2 ** ADAM_STEP)
    delta = -ADAM_LR * (m_hat / (jnp.sqrt(v_hat) + ADAM_EPS) + ADAM_WD * w)
    return delta, m2, v2


def _adamw_big(w, g, m, v, name):
    rows, cols = w.shape
    rb = rows // 4

    def body(w_ref, g_ref, m_ref, v_ref, d_ref, m2_ref, v2_ref):
        d, m2, v2 = _adamw_vals(w_ref[...], g_ref[...], m_ref[...], v_ref[...])
        d_ref[...] = d
        m2_ref[...] = m2
        v2_ref[...] = v2

    spec = pl.BlockSpec((rb, cols), lambda i: (i, 0))
    return pl.pallas_call(
        body, name=name, grid=(4,), in_specs=[spec] * 4, out_specs=[spec] * 3,
        out_shape=[SDS(w.shape, F32)] * 3, compiler_params=_params(1),
    )(w, g, m, v)


def _adamw_small(ws, gs, ms, vs):
    n = len(ws)

    def body(*refs):
        for i in range(n):
            d, m2, v2 = _adamw_vals(refs[i][...], refs[n + i][...], refs[2 * n + i][...], refs[3 * n + i][...])
            refs[4 * n + i][...] = d
            refs[5 * n + i][...] = m2
            refs[6 * n + i][...] = v2

    shapes = [SDS(w.shape, F32) for w in ws]
    outs = pl.pallas_call(
        body, name="adamw_small", out_shape=shapes * 3,
        in_specs=[VMEM_SPEC] * (4 * n), out_specs=[VMEM_SPEC] * (3 * n),
    )(*ws, *gs, *ms, *vs)
    return outs[:n], outs[n:2 * n], outs[2 * n:]


def _mod_fwd(cond64, w_mod_s, b_mod_s):
    def body(c_ref, w_ref, b_ref, o_ref):
        cc = c_ref[...]
        o_ref[...] = _dot_f32(cc * _sig(cc), w_ref[...]) + b_ref[...]

    return pl.pallas_call(
        body, name="mod_fwd", out_shape=SDS((cond64.shape[0], w_mod_s.shape[1]), F32),
        in_specs=[VMEM_SPEC] * 3, out_specs=VMEM_SPEC,
        compiler_params=pltpu.CompilerParams(vmem_limit_bytes=VMEM_LIMIT),
    )(cond64, w_mod_s, b_mod_s)


def _mod_bwd(cond64, dmod64, dmod64_my, w_mod_s, c_ctx):
    def body(c_ref, g_ref, gm_ref, w_ref, cc_ref, gw_ref, gb_ref, gcc_ref):
        cc = c_ref[...]
        act = cc * _sig(cc)
        gm = gm_ref[...]
        gw_ref[...] = _dot_f32(act, gm, (((0,), (0,)), ((), ())))
        gb_ref[...] = _colsum(g_ref[...])
        dact = _dot_f32(gm, w_ref[...], (((1,), (1,)), ((), ())))
        tot = dact[4:5, :]
        for dev in range(1, N_DEV):
            tot = tot + dact[8 * dev + 4:8 * dev + 5, :]
        c0 = cc_ref[...]
        s0 = _sig(c0)
        gcc_ref[...] = tot * (s0 * (1.0 + c0 * (1.0 - s0)))

    return pl.pallas_call(
        body, name="mod_bwd",
        out_shape=[SDS(w_mod_s.shape, F32), SDS((1, dmod64.shape[1]), F32), SDS((1, D_MODEL), F32)],
        in_specs=[VMEM_SPEC] * 5, out_specs=[VMEM_SPEC] * 3,
        compiler_params=pltpu.CompilerParams(vmem_limit_bytes=VMEM_LIMIT),
    )(cond64, dmod64, dmod64_my, w_mod_s, c_ctx)


def _in_fwd(x, ctx, g_mix, mod_a, w_in):
    bs, seq, _ = x.shape
    nb = seq // ROW_BLOCK + 1

    def body(x_ref, ctx_ref, g_ref, mod_ref, w_ref, p_ref, h_ref):
        is_ctx = pl.program_id(1) == 0
        xin = jnp.where(is_ctx, ctx_ref[0], x_ref[0])
        shift = jnp.where(is_ctx, mod_ref[0, 2:3, :], mod_ref[0, 0:1, :])
        scale = jnp.where(is_ctx, mod_ref[0, 3:4, :], mod_ref[0, 1:2, :])
        xn, _ = _rms(xin)
        hb = ((xn * g_ref[...]) * (1.0 + scale) + shift).astype(BF16)
        h_ref[0] = hb
        p_ref[0] = jnp.dot(hb, w_ref[...], preferred_element_type=F32)

    return pl.pallas_call(
        body, name="in_fwd", grid=(bs, nb),
        in_specs=[pl.BlockSpec((1, ROW_BLOCK, D_MODEL), lambda b, j: (b, jnp.maximum(j - 1, 0), 0)),
                  pl.BlockSpec((1, ROW_BLOCK, D_MODEL), lambda b, j: (b, 0, 0)),
                  pl.BlockSpec((1, D_MODEL), lambda b, j: (0, 0)),
                  pl.BlockSpec((1, 8, D_MODEL), lambda b, j: (b, 0, 0)),
                  pl.BlockSpec((D_MODEL, IN_COLS), lambda b, j: (0, 0))],
        out_specs=[pl.BlockSpec((1, ROW_BLOCK, IN_COLS), lambda b, j: (b, j, 0)),
                   pl.BlockSpec((1, ROW_BLOCK, D_MODEL), lambda b, j: (b, j, 0))],
        out_shape=[SDS((bs, nb * ROW_BLOCK, IN_COLS), F32), SDS((bs, nb * ROW_BLOCK, D_MODEL), BF16)],
        compiler_params=_params(2),
    )(x, ctx, g_mix, mod_a, w_in)


def _in_bwd(x, ctx, dx1, g_mix, mod_a, w_in, df_f, df_b, di, dq, dpc):
    bs, seq, _ = x.shape
    nb = seq // ROW_BLOCK + 1

    def body(x_ref, ctx_ref, dx1_ref, g_ref, mod_ref, w_ref, dff_ref, dfb_ref, di_ref, dq_ref, dpc_ref,
             gx_ref, dp_ref, dg_ref, dmod_ref):
        b, j = pl.program_id(0), pl.program_id(1)
        is_ctx = j == 0

        @pl.when((b == 0) & (j == 0))
        def _():
            dg_ref[...] = jnp.zeros_like(dg_ref)

        @pl.when(j == 0)
        def _():
            dmod_ref[...] = jnp.zeros_like(dmod_ref)

        dp = jnp.concatenate([dff_ref[0], dfb_ref[0], di_ref[0], dq_ref[0], dpc_ref[0]], axis=1)
        dp_ref[0] = dp
        dh = lax.dot_general(dp, w_ref[...], (((1,), (1,)), ((), ())), preferred_element_type=F32)
        xin = jnp.where(is_ctx, ctx_ref[0], x_ref[0])
        scale = jnp.where(is_ctx, mod_ref[0, 3:4, :], mod_ref[0, 1:2, :])
        xn, r = _rms(xin)
        g = g_ref[...]
        hn = xn * g
        d_shift = _colsum(dh)
        d_scale = _colsum(dh * hn)
        dhn = dh * (1.0 + scale)
        dg_ref[...] += _colsum(dhn * xn)
        dx = _rms_bwd(dhn * g, xn, r)

        @pl.when(is_ctx)
        def _():
            dmod_ref[0, 2:3, :] += d_shift
            dmod_ref[0, 3:4, :] += d_scale

        @pl.when(jnp.logical_not(is_ctx))
        def _():
            dmod_ref[0, 0:1, :] += d_shift
            dmod_ref[0, 1:2, :] += d_scale
            gx_ref[0] = dx + dx1_ref[0]

    def rows(w):
        return pl.BlockSpec((1, ROW_BLOCK, w), lambda b, j: (b, j, 0))

    lat = pl.BlockSpec((1, ROW_BLOCK, D_MODEL), lambda b, j: (b, jnp.maximum(j - 1, 0), 0))
    return pl.pallas_call(
        body, name="in_bwd", grid=(bs, nb),
        in_specs=[lat, pl.BlockSpec((1, ROW_BLOCK, D_MODEL), lambda b, j: (b, 0, 0)), lat,
                  pl.BlockSpec((1, D_MODEL), lambda b, j: (0, 0)),
                  pl.BlockSpec((1, 8, D_MODEL), lambda b, j: (b, 0, 0)),
                  pl.BlockSpec((D_MODEL, IN_COLS), lambda b, j: (0, 0)),
                  rows(KW), rows(KW), rows(KW), rows(KW), rows(7 * KW)],
        out_specs=[lat, rows(IN_COLS), pl.BlockSpec((1, D_MODEL), lambda b, j: (0, 0)),
                   pl.BlockSpec((1, 8, D_MODEL), lambda b, j: (b, 0, 0))],
        out_shape=[SDS(x.shape, F32), SDS((bs, nb * ROW_BLOCK, IN_COLS), BF16), SDS((1, D_MODEL), F32),
                   SDS((bs, 8, D_MODEL), F32)],
        compiler_params=_params(2),
    )(x, ctx, dx1, g_mix, mod_a, w_in, df_f, df_b, di, dq, dpc)


def _lower_bound(lbg_ref, direction):
    return _sig(lbg_ref[0, direction:direction + 1, :] - lbg_ref[1, direction:direction + 1, :])


def _chunk_gates(fl, lb, tri_f32, upper):
    sg = _sig(fl)
    f = lb + (1.0 - lb) * sg
    g = jnp.log(f)
    k = 1.0 - f
    bcum = _dot_f32(tri_f32, g)
    bl = bcum[0:1] if upper else bcum[HGRN_CHUNK - 1:HGRN_CHUNK]
    return sg, f, k, bcum, bl


def _hgrn_block_order(direction, nb):
    if direction == 0:
        return lambda j: j
    return lambda j: jnp.where(j == 0, 0, nb - j)


def _hgrn_fwd(p, lbg, direction):
    bs, rows, _ = p.shape
    nb = rows // ROW_BLOCK
    ncb = ROW_BLOCK // HGRN_CHUNK
    upper = direction == 1
    order = _hgrn_block_order(direction, nb)

    def body(f_ref, i_ref, q_ref, lbg_ref, o_ref, s_ref, st):
        @pl.when(pl.program_id(1) == 0)
        def _():
            st[...] = jnp.zeros_like(st)

        lb = _lower_bound(lbg_ref, direction)
        mask = _tri(HGRN_CHUNK, upper)
        tri_f32 = mask.astype(F32)
        for ci in (reversed(range(ncb)) if upper else range(ncb)):
            rs = slice(ci * HGRN_CHUNK, (ci + 1) * HGRN_CHUNK)
            _, _, k, bcum, bl = _chunk_gates(f_ref[0, rs, :], lb, tri_f32, upper)
            q = q_ref[0, rs, :]
            v = i_ref[0, rs, :]
            qi = q * jnp.exp(bcum - 0.5 * bl)
            ki = k * jnp.exp(0.5 * bl - bcum)
            kd = k * jnp.exp(bl - bcum)
            qe = q * jnp.exp(bcum)
            dec = jnp.exp(bl)
            outs = []
            for h in range(N_HEADS):
                ls = slice(h * HEAD_DIM, (h + 1) * HEAD_DIM)
                s_in = st[h]
                s_ref[0, 0, ci, h] = s_in
                a = jnp.where(mask, _dot_nt(qi[:, ls], ki[:, ls]), 0.0)
                outs.append(_dot(a, v[:, ls]) + _dot_nt(qe[:, ls], s_in))
                st[h] = s_in * dec[:, ls] + _dot_tn(v[:, ls], kd[:, ls])
            o_ref[0, rs, :] = jnp.concatenate(outs, axis=1)

    def col(cb):
        return pl.BlockSpec((1, ROW_BLOCK, KW), lambda b, j: (b, order(j), cb))

    return pl.pallas_call(
        body, name=f"hgrn_fwd{direction}", grid=(bs, nb),
        in_specs=[col(direction), col(2), col(3), pl.BlockSpec((2, 2, KW), lambda b, j: (0, 0, 0))],
        out_specs=[pl.BlockSpec((1, ROW_BLOCK, KW), lambda b, j: (b, order(j), 0)),
                   pl.BlockSpec((1, 1, ncb, N_HEADS, HEAD_DIM, HEAD_DIM), lambda b, j: (b, order(j), 0, 0, 0, 0))],
        out_shape=[SDS((bs, rows, KW), F32), SDS((bs, nb, ncb, N_HEADS, HEAD_DIM, HEAD_DIM), F32)],
        scratch_shapes=[pltpu.VMEM((N_HEADS, HEAD_DIM, HEAD_DIM), F32)],
        compiler_params=_params(2),
    )(p, p, p, lbg)


def _hgrn_bwd(p, lbg, s_saved, do_raw, direction, dq_prev=None, dv_prev=None):
    bs, rows, _ = p.shape
    nb = rows // ROW_BLOCK
    ncb = ROW_BLOCK // HGRN_CHUNK
    upper = direction == 1
    fwd_order = _hgrn_block_order(direction, nb)
    order = lambda j: fwd_order(nb - 1 - j)
    last = dq_prev is not None
    out_dt = BF16 if last else F32

    def body(*refs):
        if last:
            f_ref, i_ref, q_ref, lbg_ref, s_ref, do_ref, dqp_ref, dvp_ref, df_ref, dq_ref, dv_ref, dlb_ref, dst = refs
        else:
            f_ref, i_ref, q_ref, lbg_ref, s_ref, do_ref, df_ref, dq_ref, dv_ref, dlb_ref, dst = refs
        b, j = pl.program_id(0), pl.program_id(1)

        @pl.when((b == 0) & (j == 0))
        def _():
            dlb_ref[...] = jnp.zeros_like(dlb_ref)

        @pl.when(j == 0)
        def _():
            dst[...] = jnp.zeros_like(dst)

        lb = _lower_bound(lbg_ref, direction)
        mask = _tri(HGRN_CHUNK, upper)
        mask_t = _tri(HGRN_CHUNK, not upper)
        tri_f32 = mask.astype(F32)
        tri_t_f32 = mask_t.astype(F32)
        dlb_acc = jnp.zeros((1, KW), F32)
        for ci in (range(ncb) if upper else reversed(range(ncb))):
            rs = slice(ci * HGRN_CHUNK, (ci + 1) * HGRN_CHUNK)
            sg, f, k, bcum, bl = _chunk_gates(f_ref[0, rs, :], lb, tri_f32, upper)
            q = q_ref[0, rs, :]
            v = i_ref[0, rs, :]
            do = do_ref[0, rs, :]
            eqm = jnp.exp(bcum - 0.5 * bl)
            ekm = jnp.exp(0.5 * bl - bcum)
            ekd = jnp.exp(bl - bcum)
            eb = jnp.exp(bcum)
            dec = jnp.exp(bl)
            qi, ki, kd, qe = q * eqm, k * ekm, k * ekd, q * eb
            dqi_l, dki_l, dkd_l, dqe_l, dv_l, ddec_l = [], [], [], [], [], []
            for h in range(N_HEADS):
                ls = slice(h * HEAD_DIM, (h + 1) * HEAD_DIM)
                s_in = s_ref[0, 0, ci, h]
                ds_out = dst[h]
                do_h, v_h = do[:, ls], v[:, ls]
                a_t = jnp.where(mask_t, _dot_nt(ki[:, ls], qi[:, ls]), 0.0)
                da = jnp.where(mask, _dot_nt(do_h, v_h), 0.0)
                da_t = jnp.where(mask_t, _dot_nt(v_h, do_h), 0.0)
                dqe_l.append(_dot(do_h, s_in))
                dkd_l.append(_dot(v_h, ds_out))
                dv_l.append(_dot(a_t, do_h) + _dot_nt(kd[:, ls], ds_out))
                dqi_l.append(_dot(da, ki[:, ls]))
                dki_l.append(_dot(da_t, qi[:, ls]))
                ddec_l.append(_colsum(ds_out * s_in))
                dst[h] = ds_out * dec[:, ls] + _dot_tn(do_h, qe[:, ls])
            dqi, dki, dkd, dqe = (jnp.concatenate(t, axis=1) for t in (dqi_l, dki_l, dkd_l, dqe_l))
            dv = jnp.concatenate(dv_l, axis=1)
            ddec = jnp.concatenate(ddec_l, axis=1)
            dq = dqi * eqm + dqe * eb
            dk = dki * ekm + dkd * ekd
            db = dqi * qi - dki * ki - dkd * kd + dqe * qe
            dbl = _colsum(dkd * kd) + ddec * dec
            dg = _dot_f32(tri_t_f32, db) + dbl
            df = dg / f - dk
            dlb_acc = dlb_acc + _colsum(df * (1.0 - sg))
            dfl = df * (1.0 - lb) * sg * (1.0 - sg)
            df_ref[0, rs, :] = dfl.astype(BF16)
            if last:
                dq_ref[0, rs, :] = (dq + dqp_ref[0, rs, :]).astype(out_dt)
                dv_ref[0, rs, :] = (dv + dvp_ref[0, rs, :]).astype(out_dt)
            else:
                dq_ref[0, rs, :] = dq
                dv_ref[0, rs, :] = dv
        dlb_ref[...] += dlb_acc

    def col(cb):
        return pl.BlockSpec((1, ROW_BLOCK, KW), lambda b, j: (b, order(j), cb))

    row = pl.BlockSpec((1, ROW_BLOCK, KW), lambda b, j: (b, order(j), 0))
    in_specs = [col(direction), col(2), col(3), pl.BlockSpec((2, 2, KW), lambda b, j: (0, 0, 0)),
                pl.BlockSpec((1, 1, ncb, N_HEADS, HEAD_DIM, HEAD_DIM), lambda b, j: (b, order(j), 0, 0, 0, 0)), row]
    args = [p, p, p, lbg, s_saved, do_raw]
    if last:
        in_specs += [row, row]
        args += [dq_prev, dv_prev]
    return pl.pallas_call(
        body, name=f"hgrn_bwd{direction}", grid=(bs, nb), in_specs=in_specs,
        out_specs=[row, row, row, pl.BlockSpec((1, KW), lambda b, j: (0, 0))],
        out_shape=[SDS((bs, rows, KW), BF16), SDS((bs, rows, KW), out_dt), SDS((bs, rows, KW), out_dt),
                   SDS((1, KW), F32)],
        scratch_shapes=[pltpu.VMEM((N_HEADS, HEAD_DIM, HEAD_DIM), F32)],
        compiler_params=_params(2),
    )(*args)


def _mix_values(og, u, v, ga, gb, o_raw, gna, lng, lnb, ws_ref, bst, wpa, wpb, wo):
    t = {}
    sog = _sig(og)
    t["sog"], t["silu_og"] = sog, og * sog
    xh_l, r_l = [], []
    for h in range(N_HEADS):
        xh, r = _rms(o_raw[:, h * HEAD_DIM:(h + 1) * HEAD_DIM])
        xh_l.append(xh)
        r_l.append(r)
    t["xh"], t["r"] = jnp.concatenate(xh_l, axis=1), r_l
    gna4 = jnp.concatenate([gna] * N_HEADS, axis=1)
    t["gna4"] = gna4
    t["o_n"] = t["xh"] * gna4
    t["o_a"] = t["o_n"] * t["silu_og"]
    t["gu"], t["tu"] = _gelu(u)
    gv, t["tv"] = _gelu(v)
    mu = jnp.mean(gv, axis=-1, keepdims=True)
    cen = gv - mu
    t["rstd"] = lax.rsqrt(jnp.mean(cen * cen, axis=-1, keepdims=True) + EPS)
    t["xhat"] = cen * t["rstd"]
    vn = t["xhat"] * lng + lnb
    t["vn"] = vn
    chunks = []
    for n in range(ROW_BLOCK // SGU_CHUNK):
        rs = slice(n * SGU_CHUNK, (n + 1) * SGU_CHUNK)
        groups = []
        for g in range(N_HEADS):
            ls = slice(g * HEAD_DIM, (g + 1) * HEAD_DIM)
            groups.append(_dot(ws_ref[g], vn[rs, ls]) + bst[:, g:g + 1])
        chunks.append(jnp.concatenate(groups, axis=1))
    t["mixed"] = jnp.concatenate(chunks, axis=0)
    t["o_bm"] = t["gu"] * t["mixed"]
    t["ya"] = _dot(t["o_a"], wpa)
    t["yb"] = _dot(t["o_bm"], wpb)
    t["sa"], t["sb"] = _sig(ga), _sig(gb)
    t["merged"] = t["sa"] * t["ya"] + t["sb"] * t["yb"]
    t["mix"] = _dot(t["merged"], wo)
    return t


def _mix_in_specs(row_of):
    def col(cb):
        return pl.BlockSpec((1, ROW_BLOCK, KW), lambda b, j: (b, row_of(j), cb))
    return [col(cb) for cb in range(4, 11)]


def _mix_param_specs():
    full2 = lambda r, c: pl.BlockSpec((r, c), lambda b, j: (0, 0))
    return [full2(1, HEAD_DIM), full2(1, KW), full2(1, KW),
            pl.BlockSpec((N_HEADS, SGU_CHUNK, SGU_CHUNK), lambda b, j: (0, 0, 0)),
            full2(SGU_CHUNK, N_HEADS), full2(KW, D_MODEL), full2(KW, D_MODEL), full2(D_MODEL, D_MODEL)]


def _mix_fwd(p, o_f, o_b, x, mod_c, gna, lng, lnb, w_s, bst, wpa, wpb, wo):
    bs, seq, _ = x.shape
    nbl = seq // ROW_BLOCK

    def body(og_r, u_r, v_r, ga0_r, ga1_r, gb0_r, gb1_r, of_r, ob_r, x_r, mod_r,
             gna_r, lng_r, lnb_r, ws_r, bst_r, wpa_r, wpb_r, wo_r, x1_r):
        ga = jnp.concatenate([ga0_r[0], ga1_r[0]], axis=1)
        gb = jnp.concatenate([gb0_r[0], gb1_r[0]], axis=1)
        t = _mix_values(og_r[0], u_r[0], v_r[0], ga, gb, of_r[0] + ob_r[0], gna_r[...], lng_r[...], lnb_r[...],
                        ws_r, bst_r[...], wpa_r[...], wpb_r[...], wo_r[...])
        x1_r[0] = x_r[0] + mod_r[0, 0:1, :] * t["mix"]

    row = lambda w: pl.BlockSpec((1, ROW_BLOCK, w), lambda b, j: (b, j + 1, 0))
    lat = pl.BlockSpec((1, ROW_BLOCK, D_MODEL), lambda b, j: (b, j, 0))
    return pl.pallas_call(
        body, name="mix_fwd", grid=(bs, nbl),
        in_specs=_mix_in_specs(lambda j: j + 1) + [row(KW), row(KW), lat,
                                                    pl.BlockSpec((1, 8, D_MODEL), lambda b, j: (b, 0, 0))]
        + _mix_param_specs(),
        out_specs=lat, out_shape=SDS(x.shape, F32), compiler_params=_params(2),
    )(p, p, p, p, p, p, p, o_f, o_b, x, mod_c, gna, lng, lnb, w_s, bst, wpa, wpb, wo)


def _mix_bwd(p, o_f, o_b, dx1, mod_c, gna, lng, lnb, w_s, w_s_t, bst, wpa, wpb, wo):
    bs, rows, _ = p.shape
    nb = rows // ROW_BLOCK

    def body(og_r, u_r, v_r, ga0_r, ga1_r, gb0_r, gb1_r, of_r, ob_r, dx1_r, mod_r,
             gna_r, lng_r, lnb_r, ws_r, bst_r, wpa_r, wpb_r, wo_r, wst_r,
             dor_r, dpc_r, dwpa_r, dwpb_r, dwo_r, dgna_r, dlng_r, dlnb_r, dws_r, dbst_r, dmod_r):
        b, j = pl.program_id(0), pl.program_id(1)

        @pl.when((b == 0) & (j == 0))
        def _():
            for r in (dwpa_r, dwpb_r, dwo_r, dgna_r, dlng_r, dlnb_r, dws_r, dbst_r):
                r[...] = jnp.zeros_like(r)

        @pl.when(j == 0)
        def _():
            dmod_r[...] = jnp.zeros_like(dmod_r)
            dor_r[...] = jnp.zeros_like(dor_r)
            dpc_r[...] = jnp.zeros_like(dpc_r)

        @pl.when(j > 0)
        def _():
            og, u, v = og_r[0], u_r[0], v_r[0]
            ga = jnp.concatenate([ga0_r[0], ga1_r[0]], axis=1)
            gb = jnp.concatenate([gb0_r[0], gb1_r[0]], axis=1)
            gna, lng = gna_r[...], lng_r[...]
            wpa, wpb, wo = wpa_r[...], wpb_r[...], wo_r[...]
            t = _mix_values(og, u, v, ga, gb, of_r[0] + ob_r[0], gna, lng, lnb_r[...],
                            ws_r, bst_r[...], wpa, wpb, wo)
            dx1 = dx1_r[0]
            dmod_r[0, 0:1, :] += _colsum(dx1 * t["mix"])
            dmix = mod_r[0, 0:1, :] * dx1
            dmerged = _dot_nt(dmix, wo)
            dwo_r[...] += _dot_tn(t["merged"], dmix)
            sa, sb = t["sa"], t["sb"]
            dya, dyb = sa * dmerged, sb * dmerged
            dga = dmerged * t["ya"] * sa * (1.0 - sa)
            dgb = dmerged * t["yb"] * sb * (1.0 - sb)
            do_a = _dot_nt(dya, wpa)
            dwpa_r[...] += _dot_tn(t["o_a"], dya)
            do_bm = _dot_nt(dyb, wpb)
            dwpb_r[...] += _dot_tn(t["o_bm"], dyb)
            sog = t["sog"]
            dog = do_a * t["o_n"] * (sog * (1.0 + og * (1.0 - sog)))
            do_n = do_a * t["silu_og"]
            dxh = do_n * t["gna4"]
            prod = do_n * t["xh"]
            dgna = jnp.zeros((1, HEAD_DIM), F32)
            dor_l = []
            for h in range(N_HEADS):
                ls = slice(h * HEAD_DIM, (h + 1) * HEAD_DIM)
                dgna = dgna + _colsum(prod[:, ls])
                dor_l.append(_rms_bwd(dxh[:, ls], t["xh"][:, ls], t["r"][h]))
            dgna_r[...] += dgna
            dor_r[0] = jnp.concatenate(dor_l, axis=1)
            du = do_bm * t["mixed"] * _dgelu(u, t["tu"])
            dmixed = do_bm * t["gu"]
            vn = t["vn"]
            dvn_chunks = []
            for n in range(ROW_BLOCK // SGU_CHUNK):
                rs = slice(n * SGU_CHUNK, (n + 1) * SGU_CHUNK)
                groups = []
                for g in range(N_HEADS):
                    ls = slice(g * HEAD_DIM, (g + 1) * HEAD_DIM)
                    dm = dmixed[rs, ls]
                    dws_r[g] += _dot_nt(dm, vn[rs, ls])
                    dbst_r[:, g:g + 1] += jnp.sum(dm, axis=1, keepdims=True)
                    groups.append(_dot(wst_r[g], dm))
                dvn_chunks.append(jnp.concatenate(groups, axis=1))
            dvn = jnp.concatenate(dvn_chunks, axis=0)
            xhat = t["xhat"]
            dlng_r[...] += _colsum(dvn * xhat)
            dlnb_r[...] += _colsum(dvn)
            dxhat = dvn * lng
            dgv = t["rstd"] * (dxhat - jnp.mean(dxhat, axis=-1, keepdims=True)
                               - xhat * jnp.mean(dxhat * xhat, axis=-1, keepdims=True))
            dv = dgv * _dgelu(v, t["tv"])
            dpc_r[0] = jnp.concatenate([dog, du, dv, dga, dgb], axis=1).astype(BF16)

    row = lambda w: pl.BlockSpec((1, ROW_BLOCK, w), lambda b, j: (b, j, 0))
    lat = pl.BlockSpec((1, ROW_BLOCK, D_MODEL), lambda b, j: (b, jnp.maximum(j - 1, 0), 0))
    full2 = lambda r, c: pl.BlockSpec((r, c), lambda b, j: (0, 0))
    ws_spec = pl.BlockSpec((N_HEADS, SGU_CHUNK, SGU_CHUNK), lambda b, j: (0, 0, 0))
    return pl.pallas_call(
        body, name="mix_bwd", grid=(bs, nb),
        in_specs=_mix_in_specs(lambda j: j) + [row(KW), row(KW), lat,
                                                pl.BlockSpec((1, 8, D_MODEL), lambda b, j: (b, 0, 0))]
        + _mix_param_specs() + [ws_spec],
        out_specs=[row(KW), row(7 * KW), full2(KW, D_MODEL), full2(KW, D_MODEL), full2(D_MODEL, D_MODEL),
                   full2(1, HEAD_DIM), full2(1, KW), full2(1, KW), ws_spec, full2(SGU_CHUNK, N_HEADS),
                   pl.BlockSpec((1, 8, D_MODEL), lambda b, j: (b, 0, 0))],
        out_shape=[SDS((bs, rows, KW), F32), SDS((bs, rows, 7 * KW), BF16), SDS((KW, D_MODEL), F32),
                   SDS((KW, D_MODEL), F32), SDS((D_MODEL, D_MODEL), F32), SDS((1, HEAD_DIM), F32),
                   SDS((1, KW), F32), SDS((1, KW), F32), SDS((N_HEADS, SGU_CHUNK, SGU_CHUNK), F32),
                   SDS((SGU_CHUNK, N_HEADS), F32), SDS((bs, 8, D_MODEL), F32)],
        compiler_params=_params(2),
    )(p, p, p, p, p, p, p, o_f, o_b, dx1, mod_c, gna, lng, lnb, w_s, bst, wpa, wpb, wo, w_s_t)


def _ffn(x1, target, mod_c, g_ffn, g_final, w_up, w_down):
    bs, seq, _ = x1.shape
    nbl = seq // ROW_BLOCK

    def body(x1_r, tg_r, mod_r, gf_r, gl_r, wu_r, wd_r,
             dx1_r, h2_r, dab_r, hid_r, dffn_r, loss_r, dgl_r, dgf_r, dmod_r):
        b, j = pl.program_id(0), pl.program_id(1)

        @pl.when((b == 0) & (j == 0))
        def _():
            for r in (loss_r, dgl_r, dgf_r):
                r[...] = jnp.zeros_like(r)

        @pl.when(j == 0)
        def _():
            dmod_r[...] = jnp.zeros_like(dmod_r)

        x1 = x1_r[0]
        shift, scale, gate = mod_r[0, 1:2, :], mod_r[0, 2:3, :], mod_r[0, 3:4, :]
        gf, gl = gf_r[...], gl_r[...]
        xn2, r2 = _rms(x1)
        hn2 = xn2 * gf
        h2 = (hn2 * (1.0 + scale) + shift).astype(BF16)
        h2_r[0] = h2
        ab = jnp.dot(h2, wu_r[...], preferred_element_type=F32)
        a, bb = ab[:, :D_FF], ab[:, D_FF:]
        sa = _sig(a)
        silu_a = a * sa
        hid = (silu_a * bb).astype(BF16)
        hid_r[0] = hid
        ffn = jnp.dot(hid, wd_r[...], preferred_element_type=F32)
        x2 = x1 + gate * ffn
        xn3, r3 = _rms(x2)
        err = xn3 * gl - tg_r[0]
        loss_r[...] += 0.5 * jnp.sum(jnp.mean(err * err, axis=-1, keepdims=True), axis=0, keepdims=True)
        dy = err * (1.0 / D_MODEL)
        dgl_r[...] += _colsum(dy * xn3)
        dx2 = _rms_bwd(dy * gl, xn3, r3)
        dmod_r[0, 3:4, :] += _colsum(dx2 * ffn)
        dffn = (gate * dx2).astype(BF16)
        dffn_r[0] = dffn
        dhid = lax.dot_general(dffn, wd_r[...], (((1,), (1,)), ((), ())), preferred_element_type=F32)
        da = dhid * bb * (sa * (1.0 + a * (1.0 - sa)))
        db = dhid * silu_a
        dab = jnp.concatenate([da, db], axis=1).astype(BF16)
        dab_r[0] = dab
        dh2 = lax.dot_general(dab, wu_r[...], (((1,), (1,)), ((), ())), preferred_element_type=F32)
        dmod_r[0, 1:2, :] += _colsum(dh2)
        dmod_r[0, 2:3, :] += _colsum(dh2 * hn2)
        dhn2 = dh2 * (1.0 + scale)
        dgf_r[...] += _colsum(dhn2 * xn2)
        dx1_r[0] = dx2 + _rms_bwd(dhn2 * gf, xn2, r2)

    lat = lambda w: pl.BlockSpec((1, ROW_BLOCK, w), lambda b, j: (b, j, 0))
    full2 = lambda r, c: pl.BlockSpec((r, c), lambda b, j: (0, 0))
    mod_spec = pl.BlockSpec((1, 8, D_MODEL), lambda b, j: (b, 0, 0))
    return pl.pallas_call(
        body, name="ffn", grid=(bs, nbl),
        in_specs=[lat(D_MODEL), lat(D_MODEL), mod_spec, full2(1, D_MODEL), full2(1, D_MODEL),
                  full2(D_MODEL, 2 * D_FF), full2(D_FF, D_MODEL)],
        out_specs=[lat(D_MODEL), lat(D_MODEL), lat(2 * D_FF), lat(D_FF), lat(D_MODEL),
                   full2(1, 1), full2(1, D_MODEL), full2(1, D_MODEL), mod_spec],
        out_shape=[SDS(x1.shape, F32), SDS(x1.shape, BF16), SDS((bs, seq, 2 * D_FF), BF16),
                   SDS((bs, seq, D_FF), BF16), SDS(x1.shape, BF16), SDS((1, 1), F32),
                   SDS((1, D_MODEL), F32), SDS((1, D_MODEL), F32), SDS((bs, 8, D_MODEL), F32)],
        compiler_params=_params(2),
    )(x1, target, mod_c, g_ffn, g_final, w_up, w_down)


def _matmul_tn(a, b, n_blocks, tk, name):
    t, m = a.shape
    n = b.shape[1]
    tn = n // n_blocks

    def body(a_ref, b_ref, o_ref):
        @pl.when(pl.program_id(1) == 0)
        def _():
            o_ref[...] = jnp.zeros_like(o_ref)
        o_ref[0] += _dot_tn(a_ref[...], b_ref[...])

    return pl.pallas_call(
        body, name=name, grid=(n_blocks, t // tk),
        in_specs=[pl.BlockSpec((tk, m), lambda i, k: (k, 0)), pl.BlockSpec((tk, tn), lambda i, k: (k, i))],
        out_specs=pl.BlockSpec((1, m, tn), lambda i, k: (i, 0, 0)),
        out_shape=SDS((n_blocks, m, tn), F32), compiler_params=_params(2),
    )(a, b)


SMALL_ROWS = 80
ROW_CCTX = 3


def _small_reduce(gathered, lbg):
    def body(g_ref, lbg_ref, s_ref, dgam_ref):
        tot = g_ref[0:SMALL_ROWS, :]
        for dev in range(1, N_DEV):
            tot = tot + g_ref[dev * SMALL_ROWS:(dev + 1) * SMALL_ROWS, :]
        s_ref[...] = tot
        cc = g_ref[ROW_CCTX:ROW_CCTX + 1, :]
        for dev in range(2, N_DEV, 2):
            cc = cc + g_ref[dev * SMALL_ROWS + ROW_CCTX:dev * SMALL_ROWS + ROW_CCTX + 1, :]
        s_ref[ROW_CCTX:ROW_CCTX + 1, :] = cc
        dlb = tot[7:8, :]
        for d in range(2):
            s0 = _sig(lbg_ref[0, d:d + 1, :] - lbg_ref[1, d:d + 1, :])
            dgam_ref[d:d + 1, :] = dlb[:, d * KW:(d + 1) * KW] * s0 * (1.0 - s0)

    return pl.pallas_call(
        body, name="small_reduce", out_shape=[SDS((SMALL_ROWS, D_MODEL), F32), SDS((2, KW), F32)],
        in_specs=[VMEM_SPEC] * 2, out_specs=[VMEM_SPEC] * 2,
    )(gathered, lbg)


def _pad_cols(a, width):
    return jnp.pad(a, ((0, 0), (0, width - a.shape[1])))


def kernel(x, c, ctx, c_ctx, w_mod, b_mod, g_mix, g_ffn, w_in, lb_gamma, g_norm_a, ln_v_g, ln_v_b, w_s, b_s, w_pa, w_pb, w_o, w_up, w_down, g_final, loss_target, m_c_ctx, m_w_mod, m_b_mod, m_g_mix, m_g_ffn, m_w_in, m_lb_gamma, m_g_norm_a, m_ln_v_g, m_ln_v_b, m_w_s, m_b_s, m_w_pa, m_w_pb, m_w_o, m_w_up, m_w_down, m_g_final, v_c_ctx, v_w_mod, v_b_mod, v_g_mix, v_g_ffn, v_w_in, v_lb_gamma, v_g_norm_a, v_ln_v_g, v_ln_v_b, v_w_s, v_b_s, v_w_pa, v_w_pb, v_w_o, v_w_up, v_w_down, v_g_final):
    ax, ay, ac = lax.axis_index("x"), lax.axis_index("y"), lax.axis_index("c")
    kc = 2 * ax + ay
    dev = 2 * kc + ac
    pos = jnp.stack([kc, ac]).astype(jnp.int32)
    bs, seq, _ = x.shape
    assert bs <= 4 and ctx.shape[1] == ROW_BLOCK and seq % ROW_BLOCK == 0
    mod_cols = w_mod.shape[2]

    lbg_row = _pad_cols(lb_gamma.reshape(1, -1), D_MODEL)
    pay1 = jnp.concatenate([c, jnp.zeros((4 - bs, D_MODEL), F32), c_ctx[None, :], lbg_row,
                            jnp.zeros((2, D_MODEL), F32)], axis=0)
    cond64 = _all_gather8(pay1, "gather_cond")
    lbg_full = cond64.reshape(N_DEV, 8, D_MODEL)[0::2, 5, :KW].reshape(N_CHIPS, 2, 2, HEAD_DIM)
    lbg_full = jnp.transpose(lbg_full, (1, 2, 0, 3)).reshape(2, 2, KW)

    b_mod_s = lax.dynamic_slice(b_mod, (0, kc * mod_cols), (1, mod_cols))
    mod_s = _mod_fwd(cond64, w_mod[0], b_mod_s)
    mod_g = _all_gather8(mod_s, "gather_mod").reshape(N_DEV, 64, mod_cols)[0::2]
    mod_full = jnp.transpose(mod_g, (1, 0, 2)).reshape(64, N_CHIPS * mod_cols)
    mod_mine = lax.dynamic_slice(mod_full, (dev * 8, 0), (8, 6 * D_MODEL)).reshape(8, 6, D_MODEL)
    mod, mc = mod_mine[:bs], mod_mine[4]
    zeros4 = jnp.zeros((bs, 4, D_MODEL), F32)
    mod_a = jnp.concatenate([mod[:, 0:2], jnp.broadcast_to(mc[None, 0:2], (bs, 2, D_MODEL)), zeros4], axis=1)
    mod_c = jnp.concatenate([mod[:, 2:6], zeros4], axis=1)

    shards = [w_in[0], w_up[0], w_pa[0], w_pb[0], w_o[0], w_down[0]]
    gathered = _gather_weights(_cast_bf16(shards))

    def cols_major(a):
        return jnp.transpose(a, (1, 0, 2)).reshape(a.shape[1], -1)

    w_in_f, w_up_f, w_pa_f, w_pb_f = (cols_major(a) for a in gathered[:4])
    w_o_f = gathered[4].reshape(-1, D_MODEL)
    w_down_f = gathered[5].reshape(-1, D_MODEL)

    gna, lng, lnb = g_norm_a, ln_v_g, ln_v_b
    ws3 = w_s[0]
    ws3_t = jnp.transpose(ws3, (0, 2, 1))
    bst = jnp.transpose(b_s[0])

    p, h_all = _in_fwd(x, ctx, g_mix, mod_a, w_in_f)
    o_f, s_f = _hgrn_fwd(p, lbg_full, 0)
    o_b, s_b = _hgrn_fwd(p, lbg_full, 1)
    x1 = _mix_fwd(p, o_f, o_b, x, mod_c, gna, lng, lnb, ws3, bst, w_pa_f, w_pb_f, w_o_f)
    dx1, h2, dab, hid, dffn, loss_part, dg_final, dg_ffn, dmod_ffn = _ffn(
        x1, loss_target, mod_c, g_ffn, g_final[None, :], w_up_f, w_down_f)
    (do_raw, dpc, dw_pa, dw_pb, dw_o, dgna, dlng, dlnb, dws, dbst, dmod_mix) = _mix_bwd(
        p, o_f, o_b, dx1, mod_c, gna, lng, lnb, ws3, ws3_t, bst, w_pa_f, w_pb_f, w_o_f)
    df_f, dq0, dv0, dlb0 = _hgrn_bwd(p, lbg_full, s_f, do_raw, 0)
    df_b, dq, di, dlb1 = _hgrn_bwd(p, lbg_full, s_b, do_raw, 1, dq0, dv0)
    grad_x, dp, dg_mix, dmod_in = _in_bwd(x, ctx, dx1, g_mix, mod_a, w_in_f, df_f, df_b, di, dq, dpc)

    rows_all = dp.shape[0] * dp.shape[1]
    rows_lat = bs * seq
    tk_all = 2 * ROW_BLOCK if rows_all % (2 * ROW_BLOCK) == 0 else ROW_BLOCK
    tk_lat = 2 * ROW_BLOCK if rows_lat % (2 * ROW_BLOCK) == 0 else ROW_BLOCK
    dw_in = _matmul_tn(h_all.reshape(rows_all, D_MODEL), dp.reshape(rows_all, IN_COLS), N_CHIPS, tk_all, "dw_in")
    dw_up = _matmul_tn(h2.reshape(rows_lat, D_MODEL), dab.reshape(rows_lat, 2 * D_FF), N_CHIPS, tk_lat, "dw_up")
    dw_down = _matmul_tn(hid.reshape(rows_lat, D_FF), dffn.reshape(rows_lat, D_MODEL), 1, tk_lat, "dw_down")

    def shard_major(a):
        return jnp.transpose(a.reshape(a.shape[0], N_CHIPS, -1), (1, 0, 2))

    partials = [dw_in, dw_up, shard_major(dw_pa), shard_major(dw_pb),
                dw_o.reshape(N_CHIPS, -1, D_MODEL), dw_down.reshape(N_CHIPS, -1, D_MODEL)]

    big_names = ["w_in", "w_up", "w_pa", "w_pb", "w_o", "w_down"]
    recv_sib = _rs_sibling_halves(partials)
    chip_sums = [_rs_add_halves(pos, g, r, "rs_add_" + nm) for g, r, nm in zip(partials, recv_sib, big_names)]
    recv_own = _rs_to_owner([cs[1] for cs in chip_sums])
    half_tot = [_rs_sum_owner(pos, cs[0], r, "rs_sum_" + nm) for cs, r, nm in zip(chip_sums, recv_own, big_names)]
    g_big = _rs_join_halves(half_tot)
    big_m = [m_w_in, m_w_up, m_w_pa, m_w_pb, m_w_o, m_w_down]
    big_v = [v_w_in, v_w_up, v_w_pa, v_w_pb, v_w_o, v_w_down]
    res = {}
    for name, w, g, m, v in zip(big_names, shards, g_big, big_m, big_v):
        d, m2, v2 = _adamw_big(w, g, m[0], v[0], "adamw_" + name)
        res[name] = (g[None], d[None], m2[None], v2[None])

    dmod_mine = jnp.concatenate([dmod_in[:, 0], dmod_in[:, 1], dmod_mix[:, 0], dmod_ffn[:, 1], dmod_ffn[:, 2],
                                 dmod_ffn[:, 3]], axis=1)
    dmc = jnp.concatenate([jnp.sum(dmod_in[:, 2], axis=0), jnp.sum(dmod_in[:, 3], axis=0),
                           jnp.zeros((4 * D_MODEL,), F32)])[None, :]
    pay3 = jnp.concatenate([dmod_mine, jnp.zeros((4 - bs, 6 * D_MODEL), F32), dmc,
                            jnp.zeros((3, 6 * D_MODEL), F32)], axis=0)
    dmod64 = _all_gather8(pay3, "gather_dmod")
    dmod64_my = lax.dynamic_slice(dmod64, (0, kc * mod_cols), (64, mod_cols))
    g_w_mod, g_b_mod, g_cctx_part = _mod_bwd(cond64, dmod64, dmod64_my, w_mod[0], c_ctx[None, :])
    d, m2, v2 = _adamw_big(w_mod[0], g_w_mod, m_w_mod[0], v_w_mod[0], "adamw_w_mod")
    res["w_mod"] = (g_w_mod[None], d[None], m2[None], v2[None])

    def row(*parts):
        return _pad_cols(jnp.concatenate([q.reshape(1, -1) for q in parts], axis=1), D_MODEL)

    small_rows = [dg_mix, dg_ffn, dg_final, g_cctx_part, row(dgna), row(dlng, dlnb), row(jnp.transpose(dbst)),
                  row(dlb0, dlb1), row(loss_part), jnp.zeros((7, D_MODEL), F32), dws.reshape(64, D_MODEL)]
    pay4 = jnp.concatenate(small_rows, axis=0)
    tot, dgam0 = _small_reduce(_all_gather8(pay4, "gather_small"), lbg_full)
    loss = tot[8, 0]
    dgam_full = jnp.stack([dgam0, -dgam0])
    g_lbg = lax.dynamic_slice(dgam_full, (0, 0, kc * HEAD_DIM), (2, 2, HEAD_DIM))

    small = [
        ("c_ctx", c_ctx[None, :], tot[3:4], m_c_ctx, v_c_ctx),
        ("b_mod", b_mod, g_b_mod, m_b_mod, v_b_mod),
        ("g_mix", g_mix, tot[0:1], m_g_mix, v_g_mix),
        ("g_ffn", g_ffn, tot[1:2], m_g_ffn, v_g_ffn),
        ("lb_gamma", lb_gamma.reshape(4, HEAD_DIM), g_lbg.reshape(4, HEAD_DIM), m_lb_gamma, v_lb_gamma),
        ("g_norm_a", g_norm_a, tot[4:5, :HEAD_DIM], m_g_norm_a, v_g_norm_a),
        ("ln_v_g", ln_v_g, tot[5:6, :KW], m_ln_v_g, v_ln_v_g),
        ("ln_v_b", ln_v_b, tot[5:6, KW:], m_ln_v_b, v_ln_v_b),
        ("w_s", w_s.reshape(N_HEADS * SGU_CHUNK, SGU_CHUNK), tot[16:80].reshape(N_HEADS * SGU_CHUNK, SGU_CHUNK),
         m_w_s, v_w_s),
        ("b_s", b_s[0], tot[6:7, :KW].reshape(N_HEADS, SGU_CHUNK), m_b_s, v_b_s),
        ("g_final", g_final[None, :], tot[2:3], m_g_final, v_g_final),
    ]
    ws_, gs_ = [s[1] for s in small], [s[2] for s in small]
    ms_ = [s[3].reshape(s[1].shape) for s in small]
    vs_ = [s[4].reshape(s[1].shape) for s in small]
    ds_, m2s_, v2s_ = _adamw_small(ws_, gs_, ms_, vs_)
    for (name, _, g, m, _), d, m2, v2 in zip(small, ds_, m2s_, v2s_):
        res[name] = tuple(t.reshape(m.shape) for t in (g, d, m2, v2))

    order = ["c_ctx", "w_mod", "b_mod", "g_mix", "g_ffn", "w_in", "lb_gamma", "g_norm_a", "ln_v_g", "ln_v_b",
             "w_s", "b_s", "w_pa", "w_pb", "w_o", "w_up", "w_down", "g_final"]
    outs = [loss, grad_x]
    for part in range(4):
        outs += [res[n][part] for n in order]
    return tuple(outs)
```

```python
import functools
import math

import jax
import jax.numpy as jnp
from jax import lax
from jax.experimental import pallas as pl
from jax.experimental.pallas import tpu as pltpu

F32 = jnp.float32
BF16 = jnp.bfloat16
SDS = jax.ShapeDtypeStruct
MESH = pl.DeviceIdType.MESH

EPS = 1e-6
D_MODEL = 1024
N_HEADS = 4
HEAD_DIM = 128
KW = N_HEADS * HEAD_DIM
IN_COLS = 11 * KW
D_FF = 2816
HGRN_CHUNK = 64
SGU_CHUNK = 128
ROW_BLOCK = 256
N_CHIPS = 4
N_DEV = 8
V7X_VMEM_BYTES = 64 * 1024 * 1024
VMEM_LIMIT = V7X_VMEM_BYTES - 6 * 1024 * 1024

ADAM_LR, ADAM_B1, ADAM_B2, ADAM_EPS, ADAM_WD, ADAM_STEP = 0.001, 0.9, 0.999, 1e-08, 0.01, 10
GELU_C0 = math.sqrt(2.0 / math.pi)
GELU_C1 = 0.044715

VMEM_SPEC = pl.BlockSpec(memory_space=pltpu.VMEM)
ANY_SPEC = pl.BlockSpec(memory_space=pl.ANY)


def _params(n_grid):
    return pltpu.CompilerParams(dimension_semantics=("arbitrary",) * n_grid, vmem_limit_bytes=VMEM_LIMIT)


def _sig(x):
    return 1.0 / (1.0 + jnp.exp(-x))


def _gelu(x):
    t = jnp.tanh(GELU_C0 * (x + GELU_C1 * x * x * x))
    return 0.5 * x * (1.0 + t), t


def _dgelu(x, t):
    return 0.5 * (1.0 + t) + 0.5 * x * (1.0 - t * t) * GELU_C0 * (1.0 + 3.0 * GELU_C1 * x * x)


def _dot(a, b):
    return jnp.dot(a.astype(BF16), b.astype(BF16), preferred_element_type=F32)


def _dot_nt(a, b):
    return lax.dot_general(a.astype(BF16), b.astype(BF16), (((1,), (1,)), ((), ())), preferred_element_type=F32)


def _dot_tn(a, b):
    return lax.dot_general(a.astype(BF16), b.astype(BF16), (((0,), (0,)), ((), ())), preferred_element_type=F32)


def _dot_f32(a, b, dims=(((1,), (0,)), ((), ()))):
    return lax.dot_general(a, b, dims, precision=lax.Precision.HIGHEST, preferred_element_type=F32)


def _rms(x):
    r = lax.rsqrt(jnp.mean(x * x, axis=-1, keepdims=True) + EPS)
    return x * r, r


def _rms_bwd(dxn, xn, r):
    return r * (dxn - xn * jnp.mean(dxn * xn, axis=-1, keepdims=True))


def _colsum(a):
    return jnp.sum(a, axis=0, keepdims=True)


def _tri(n, upper):
    t = lax.broadcasted_iota(jnp.int32, (n, n), 0)
    s = lax.broadcasted_iota(jnp.int32, (n, n), 1)
    return (s >= t) if upper else (s <= t)


def _all_gather8(x_shard, name):
    m_per, n = x_shard.shape

    def body(x_ref, out_ref, send_sems, recv_sems, local_sem):
        x, y, c = lax.axis_index("x"), lax.axis_index("y"), lax.axis_index("c")
        me, sibling = (x, y, c), (x, y, 1 - c)
        chips = [(1 - x, y), (x, 1 - y), (1 - x, 1 - y)]

        def rows(px, py, pc):
            return out_ref.at[pl.ds((4 * px + 2 * py + pc) * m_per, m_per), :]

        def copy(k, block, to, src=None):
            return pltpu.make_async_remote_copy(
                src_ref=rows(*block) if src is None else src, dst_ref=rows(*block),
                send_sem=send_sems.at[k], recv_sem=recv_sems.at[k], device_id=to, device_id_type=MESH)

        mine = pltpu.make_async_copy(x_ref, rows(*me), local_sem)
        mine.start()
        first = [copy(0, me, sibling, src=x_ref)]
        first += [copy(1 + j, me, (*chip, c), src=x_ref) for j, chip in enumerate(chips)]
        for cp in first:
            cp.start()
        passed = [copy(4 + j, (*chip, c), sibling) for j, chip in enumerate(chips)]
        for j, chip in enumerate(chips):
            copy(1 + j, (*chip, c), me).wait_recv()
            passed[j].start()
        copy(0, sibling, me).wait_recv()
        for j, chip in enumerate(chips):
            copy(4 + j, (*chip, 1 - c), me).wait_recv()
        for cp in first + passed:
            cp.wait_send()
        mine.wait()

    return pl.pallas_call(
        body, name=name, out_shape=SDS((N_DEV * m_per, n), x_shard.dtype),
        in_specs=[VMEM_SPEC], out_specs=VMEM_SPEC,
        scratch_shapes=[pltpu.SemaphoreType.DMA((7,)), pltpu.SemaphoreType.DMA((7,)), pltpu.SemaphoreType.DMA],
    )(x_shard)


def _mesh_pos():
    x, y, c = lax.axis_index("x"), lax.axis_index("y"), lax.axis_index("c")
    chips = [(1 - x, y), (x, 1 - y), (1 - x, 1 - y)]
    return x, y, c, 2 * x + y, (x, y, 1 - c), chips


def _half_rows(c, rh):
    return pl.ds(pl.multiple_of(c * rh, 16), rh)


def _gather_weights(bufs):
    n = len(bufs)

    def body(*refs):
        outs = refs[n:2 * n]
        send_sems, recv_sems = refs[2 * n:]
        x, y, c, kc, sibling, chips = _mesh_pos()
        firsts, passed = [], []
        for wi in range(n):
            rh = outs[wi].shape[1] // 2
            for jj, chip in enumerate(chips):
                mine = outs[wi].at[kc, _half_rows(c, rh), :]
                cp = pltpu.make_async_remote_copy(
                    src_ref=mine, dst_ref=mine, send_sem=send_sems.at[wi, jj], recv_sem=recv_sems.at[wi, jj],
                    device_id=(*chip, c), device_id_type=MESH)
                cp.start()
                firsts.append(cp)
        for wi in range(n):
            rh = outs[wi].shape[1] // 2
            for jj, chip in enumerate(chips):
                blk = outs[wi].at[2 * chip[0] + chip[1], _half_rows(c, rh), :]
                pltpu.make_async_remote_copy(
                    src_ref=blk, dst_ref=blk, send_sem=send_sems.at[wi, jj], recv_sem=recv_sems.at[wi, jj],
                    device_id=(*chip, c), device_id_type=MESH).wait_recv()
                fw = pltpu.make_async_remote_copy(
                    src_ref=blk, dst_ref=blk, send_sem=send_sems.at[wi, 3 + jj], recv_sem=recv_sems.at[wi, 3 + jj],
                    device_id=sibling, device_id_type=MESH)
                fw.start()
                passed.append(fw)
        for wi in range(n):
            rh = outs[wi].shape[1] // 2
            for jj, chip in enumerate(chips):
                blk = outs[wi].at[2 * chip[0] + chip[1], _half_rows(1 - c, rh), :]
                pltpu.make_async_remote_copy(
                    src_ref=blk, dst_ref=blk, send_sem=send_sems.at[wi, 3 + jj], recv_sem=recv_sems.at[wi, 3 + jj],
                    device_id=sibling, device_id_type=MESH).wait_recv()
        for cp in firsts + passed:
            cp.wait_send()

    return pl.pallas_call(
        body, name="gather_weights",
        out_shape=[SDS(b.shape, b.dtype) for b in bufs],
        in_specs=[ANY_SPEC] * n, out_specs=[ANY_SPEC] * n,
        input_output_aliases={i: i for i in range(n)},
        scratch_shapes=[pltpu.SemaphoreType.DMA((n, 6)), pltpu.SemaphoreType.DMA((n, 6))],
    )(*bufs)


def _rs_sibling_halves(grads):
    n = len(grads)

    def body(*refs):
        ins, outs = refs[:n], refs[n:2 * n]
        send_sems, recv_sems = refs[2 * n:]
        x, y, c, kc, sibling, chips = _mesh_pos()
        cps = []
        for wi in range(n):
            rh = ins[wi].shape[1] // 2
            cp = pltpu.make_async_remote_copy(
                src_ref=ins[wi].at[:, _half_rows(1 - c, rh), :], dst_ref=outs[wi],
                send_sem=send_sems.at[wi], recv_sem=recv_sems.at[wi], device_id=sibling, device_id_type=MESH)
            cp.start()
            cps.append(cp)
        for cp in cps:
            cp.wait()

    return pl.pallas_call(
        body, name="rs_sibling_halves",
        out_shape=[SDS((N_CHIPS, g.shape[1] // 2, g.shape[2]), F32) for g in grads],
        in_specs=[ANY_SPEC] * n, out_specs=[ANY_SPEC] * n,
        scratch_shapes=[pltpu.SemaphoreType.DMA((n,)), pltpu.SemaphoreType.DMA((n,))],
    )(*grads)


def _rs_to_owner(cpbfs):
    n = len(cpbfs)

    def body(*refs):
        ins, outs = refs[:n], refs[n:2 * n]
        send_sems, recv_sems = refs[2 * n:]
        x, y, c, kc, sibling, chips = _mesh_pos()
        cps = []
        for wi in range(n):
            for jj, chip in enumerate(chips):
                cp = pltpu.make_async_remote_copy(
                    src_ref=ins[wi].at[2 * chip[0] + chip[1]], dst_ref=outs[wi].at[kc],
                    send_sem=send_sems.at[wi, jj], recv_sem=recv_sems.at[wi, jj],
                    device_id=(*chip, c), device_id_type=MESH)
                cp.start()
                cps.append(cp)
        for wi in range(n):
            for jj, chip in enumerate(chips):
                slot = outs[wi].at[2 * chip[0] + chip[1]]
                pltpu.make_async_remote_copy(
                    src_ref=slot, dst_ref=slot, send_sem=send_sems.at[wi, jj], recv_sem=recv_sems.at[wi, jj],
                    device_id=(*chip, c), device_id_type=MESH).wait_recv()
        for cp in cps:
            cp.wait_send()

    return pl.pallas_call(
        body, name="rs_to_owner",
        out_shape=[SDS(g.shape, BF16) for g in cpbfs],
        in_specs=[ANY_SPEC] * n, out_specs=[ANY_SPEC] * n,
        scratch_shapes=[pltpu.SemaphoreType.DMA((n, 3)), pltpu.SemaphoreType.DMA((n, 3))],
    )(*cpbfs)


def _rs_join_halves(bufs):
    n = len(bufs)

    def body(*refs):
        outs = refs[n:2 * n]
        send_sems, recv_sems = refs[2 * n:]
        x, y, c, kc, sibling, chips = _mesh_pos()
        cps = []
        for wi in range(n):
            mine = outs[wi].at[_half_rows(c, outs[wi].shape[0] // 2), :]
            cp = pltpu.make_async_remote_copy(
                src_ref=mine, dst_ref=mine, send_sem=send_sems.at[wi], recv_sem=recv_sems.at[wi],
                device_id=sibling, device_id_type=MESH)
            cp.start()
            cps.append(cp)
        for cp in cps:
            cp.wait()

    return pl.pallas_call(
        body, name="rs_join_halves",
        out_shape=[SDS(b.shape, F32) for b in bufs],
        in_specs=[ANY_SPEC] * n, out_specs=[ANY_SPEC] * n,
        input_output_aliases={i: i for i in range(n)},
        scratch_shapes=[pltpu.SemaphoreType.DMA((n,)), pltpu.SemaphoreType.DMA((n,))],
    )(*bufs)


def _rs_add_halves(pos, grad, recv, name):
    _, rs, cs = grad.shape
    rh = rs // 2
    rb = rh // 2

    def body(pos_ref, g_ref, r_ref, o32_ref, obf_ref):
        s = g_ref[...] + r_ref[...]
        o32_ref[...] = s
        obf_ref[...] = s.astype(BF16)

    blk = (1, rb, cs)
    return pl.pallas_call(
        body, name=name,
        grid_spec=pltpu.PrefetchScalarGridSpec(
            num_scalar_prefetch=1, grid=(N_CHIPS, 2),
            in_specs=[pl.BlockSpec(blk, lambda k, i, p: (k, p[1] * 2 + i, 0)),
                      pl.BlockSpec(blk, lambda k, i, p: (k, i, 0))],
            out_specs=[pl.BlockSpec(blk, lambda k, i, p: (k, i, 0)), pl.BlockSpec(blk, lambda k, i, p: (k, i, 0))]),
        out_shape=[SDS((N_CHIPS, rh, cs), F32), SDS((N_CHIPS, rh, cs), BF16)],
        compiler_params=_params(2),
    )(pos, grad, recv)


def _rs_sum_owner(pos, cp32, recv3, name):
    _, rh, cs = cp32.shape
    rb = rh // 2

    def body(pos_ref, own_ref, r1_ref, r2_ref, r3_ref, o_ref):
        o_ref[...] = ((own_ref[0] + r1_ref[0].astype(F32)) + r2_ref[0].astype(F32)) + r3_ref[0].astype(F32)

    blk = (1, rb, cs)

    def slot(d):
        return pl.BlockSpec(blk, lambda i, p: ((p[0] + d) % N_CHIPS, i, 0))

    return pl.pallas_call(
        body, name=name,
        grid_spec=pltpu.PrefetchScalarGridSpec(
            num_scalar_prefetch=1, grid=(2,),
            in_specs=[slot(0), slot(1), slot(2), slot(3)],
            out_specs=pl.BlockSpec((rb, cs), lambda i, p: (p[1] * 2 + i, 0))),
        out_shape=SDS((2 * rh, cs), F32),
        compiler_params=_params(1),
    )(pos, cp32, recv3, recv3, recv3)


def _cast_bf16(pos, arrs):
    n = len(arrs)

    def body(pos_ref, *refs):
        for i in range(n):
            refs[n + i][0] = refs[i][...].astype(BF16)

    return pl.pallas_call(
        body, name="cast_bf16",
        grid_spec=pltpu.PrefetchScalarGridSpec(
            num_scalar_prefetch=1, grid=(2,),
            in_specs=[pl.BlockSpec((a.shape[0] // 2, a.shape[1]), lambda i, p: (i, 0)) for a in arrs],
            out_specs=[pl.BlockSpec((1, a.shape[0] // 2, a.shape[1]), lambda i, p: (p[0], i, 0)) for a in arrs]),
        out_shape=[SDS((N_CHIPS,) + a.shape, BF16) for a in arrs],
        compiler_params=_params(1),
    )(pos, *arrs)


def _adamw_vals(w, g, m, v):
    m2 = ADAM_B1 * m + (1.0 - ADAM_B1) * g
    v2 = ADAM_B2 * v + (1.0 - ADAM_B2) * (g * g)
    m_hat = m2 / (1.0 - ADAM_B1 ** ADAM_STEP)
    v_hat = v2 / (1.0 - ADAM_B2 ** ADAM_STEP)
    delta = -ADAM_LR * (m_hat / (jnp.sqrt(v_hat) + ADAM_EPS) + ADAM_WD * w)
    return delta, m2, v2


def _adamw_big(w, g, m, v, name):
    rows, cols = w.shape
    rb = rows // 4

    def body(w_ref, g_ref, m_ref, v_ref, d_ref, m2_ref, v2_ref):
        d, m2, v2 = _adamw_vals(w_ref[...], g_ref[...], m_ref[...], v_ref[...])
        d_ref[...] = d
        m2_ref[...] = m2
        v2_ref[...] = v2

    spec = pl.BlockSpec((rb, cols), lambda i: (i, 0))
    return pl.pallas_call(
        body, name=name, grid=(4,), in_specs=[spec] * 4, out_specs=[spec] * 3,
        out_shape=[SDS(w.shape, F32)] * 3, compiler_params=_params(1),
    )(w, g, m, v)


def _adamw_small(ws, gs, ms, vs):
    n = len(ws)

    def body(*refs):
        for i in range(n):
            d, m2, v2 = _adamw_vals(refs[i][...], refs[n + i][...], refs[2 * n + i][...], refs[3 * n + i][...])
            refs[4 * n + i][...] = d
            refs[5 * n + i][...] = m2
            refs[6 * n + i][...] = v2

    shapes = [SDS(w.shape, F32) for w in ws]
    outs = pl.pallas_call(
        body, name="adamw_small", out_shape=shapes * 3,
        in_specs=[VMEM_SPEC] * (4 * n), out_specs=[VMEM_SPEC] * (3 * n),
    )(*ws, *gs, *ms, *vs)
    return outs[:n], outs[n:2 * n], outs[2 * n:]


def _mod_fwd(cond64, w_mod_s, b_mod_s):
    def body(c_ref, w_ref, b_ref, o_ref):
        cc = c_ref[...]
        o_ref[...] = _dot_f32(cc * _sig(cc), w_ref[...]) + b_ref[...]

    return pl.pallas_call(
        body, name="mod_fwd", out_shape=SDS((cond64.shape[0], w_mod_s.shape[1]), F32),
        in_specs=[VMEM_SPEC] * 3, out_specs=VMEM_SPEC,
        compiler_params=pltpu.CompilerParams(vmem_limit_bytes=VMEM_LIMIT),
    )(cond64, w_mod_s, b_mod_s)


def _mod_bwd(cond64, dmod64, dmod64_my, w_mod_s, c_ctx):
    def body(c_ref, g_ref, gm_ref, w_ref, cc_ref, gw_ref, gb_ref, gcc_ref):
        cc = c_ref[...]
        act = cc * _sig(cc)
        gm = gm_ref[...]
        gw_ref[...] = _dot_f32(act, gm, (((0,), (0,)), ((), ())))
        gb_ref[...] = _colsum(g_ref[...])
        dact = _dot_f32(gm, w_ref[...], (((1,), (1,)), ((), ())))
        tot = dact[4:5, :]
        for dev in range(1, N_DEV):
            tot = tot + dact[8 * dev + 4:8 * dev + 5, :]
        c0 = cc_ref[...]
        s0 = _sig(c0)
        gcc_ref[...] = tot * (s0 * (1.0 + c0 * (1.0 - s0)))

    return pl.pallas_call(
        body, name="mod_bwd",
        out_shape=[SDS(w_mod_s.shape, F32), SDS((1, dmod64.shape[1]), F32), SDS((1, D_MODEL), F32)],
        in_specs=[VMEM_SPEC] * 5, out_specs=[VMEM_SPEC] * 3,
        compiler_params=pltpu.CompilerParams(vmem_limit_bytes=VMEM_LIMIT),
    )(cond64, dmod64, dmod64_my, w_mod_s, c_ctx)


def _in_fwd(x, ctx, g_mix, mod_a, w_in):
    bs, seq, _ = x.shape
    nb = seq // ROW_BLOCK + 1

    def body(x_ref, ctx_ref, g_ref, mod_ref, w_ref, p_ref, h_ref):
        is_ctx = pl.program_id(1) == 0
        xin = jnp.where(is_ctx, ctx_ref[0], x_ref[0])
        shift = jnp.where(is_ctx, mod_ref[0, 2:3, :], mod_ref[0, 0:1, :])
        scale = jnp.where(is_ctx, mod_ref[0, 3:4, :], mod_ref[0, 1:2, :])
        xn, _ = _rms(xin)
        hb = ((xn * g_ref[...]) * (1.0 + scale) + shift).astype(BF16)
        h_ref[0] = hb
        p_ref[0] = jnp.dot(hb, w_ref[...], preferred_element_type=F32)

    return pl.pallas_call(
        body, name="in_fwd", grid=(bs, nb),
        in_specs=[pl.BlockSpec((1, ROW_BLOCK, D_MODEL), lambda b, j: (b, jnp.maximum(j - 1, 0), 0)),
                  pl.BlockSpec((1, ROW_BLOCK, D_MODEL), lambda b, j: (b, 0, 0)),
                  pl.BlockSpec((1, D_MODEL), lambda b, j: (0, 0)),
                  pl.BlockSpec((1, 8, D_MODEL), lambda b, j: (b, 0, 0)),
                  pl.BlockSpec((D_MODEL, IN_COLS), lambda b, j: (0, 0))],
        out_specs=[pl.BlockSpec((1, ROW_BLOCK, IN_COLS), lambda b, j: (b, j, 0)),
                   pl.BlockSpec((1, ROW_BLOCK, D_MODEL), lambda b, j: (b, j, 0))],
        out_shape=[SDS((bs, nb * ROW_BLOCK, IN_COLS), F32), SDS((bs, nb * ROW_BLOCK, D_MODEL), BF16)],
        compiler_params=_params(2),
    )(x, ctx, g_mix, mod_a, w_in)


def _in_bwd(x, ctx, dx1, g_mix, mod_a, w_in, df_f, df_b, di, dq, dpc):
    bs, seq, _ = x.shape
    nb = seq // ROW_BLOCK + 1

    def body(x_ref, ctx_ref, dx1_ref, g_ref, mod_ref, w_ref, dff_ref, dfb_ref, di_ref, dq_ref, dpc_ref,
             gx_ref, dp_ref, dg_ref, dmod_ref):
        b, j = pl.program_id(0), pl.program_id(1)
        is_ctx = j == 0

        @pl.when((b == 0) & (j == 0))
        def _():
            dg_ref[...] = jnp.zeros_like(dg_ref)

        @pl.when(j == 0)
        def _():
            dmod_ref[...] = jnp.zeros_like(dmod_ref)

        dp = jnp.concatenate([dff_ref[0], dfb_ref[0], di_ref[0], dq_ref[0], dpc_ref[0]], axis=1)
        dp_ref[0] = dp
        dh = lax.dot_general(dp, w_ref[...], (((1,), (1,)), ((), ())), preferred_element_type=F32)
        xin = jnp.where(is_ctx, ctx_ref[0], x_ref[0])
        scale = jnp.where(is_ctx, mod_ref[0, 3:4, :], mod_ref[0, 1:2, :])
        xn, r = _rms(xin)
        g = g_ref[...]
        hn = xn * g
        d_shift = _colsum(dh)
        d_scale = _colsum(dh * hn)
        dhn = dh * (1.0 + scale)
        dg_ref[...] += _colsum(dhn * xn)
        dx = _rms_bwd(dhn * g, xn, r)

        @pl.when(is_ctx)
        def _():
            dmod_ref[0, 2:3, :] += d_shift
            dmod_ref[0, 3:4, :] += d_scale

        @pl.when(jnp.logical_not(is_ctx))
        def _():
            dmod_ref[0, 0:1, :] += d_shift
            dmod_ref[0, 1:2, :] += d_scale
            gx_ref[0] = dx + dx1_ref[0]

    def rows(w):
        return pl.BlockSpec((1, ROW_BLOCK, w), lambda b, j: (b, j, 0))

    lat = pl.BlockSpec((1, ROW_BLOCK, D_MODEL), lambda b, j: (b, jnp.maximum(j - 1, 0), 0))
    return pl.pallas_call(
        body, name="in_bwd", grid=(bs, nb),
        in_specs=[lat, pl.BlockSpec((1, ROW_BLOCK, D_MODEL), lambda b, j: (b, 0, 0)), lat,
                  pl.BlockSpec((1, D_MODEL), lambda b, j: (0, 0)),
                  pl.BlockSpec((1, 8, D_MODEL), lambda b, j: (b, 0, 0)),
                  pl.BlockSpec((D_MODEL, IN_COLS), lambda b, j: (0, 0)),
                  rows(KW), rows(KW), rows(KW), rows(KW), rows(7 * KW)],
        out_specs=[lat, rows(IN_COLS), pl.BlockSpec((1, D_MODEL), lambda b, j: (0, 0)),
                   pl.BlockSpec((1, 8, D_MODEL), lambda b, j: (b, 0, 0))],
        out_shape=[SDS(x.shape, F32), SDS((bs, nb * ROW_BLOCK, IN_COLS), BF16), SDS((1, D_MODEL), F32),
                   SDS((bs, 8, D_MODEL), F32)],
        compiler_params=_params(2),
    )(x, ctx, dx1, g_mix, mod_a, w_in, df_f, df_b, di, dq, dpc)


def _lower_bound(lbg_ref, direction):
    return _sig(lbg_ref[0, direction:direction + 1, :] - lbg_ref[1, direction:direction + 1, :])


def _chunk_gates(fl, lb, tri_f32, upper):
    sg = _sig(fl)
    f = lb + (1.0 - lb) * sg
    g = jnp.log(f)
    k = 1.0 - f
    bcum = _dot_f32(tri_f32, g)
    bl = bcum[0:1] if upper else bcum[HGRN_CHUNK - 1:HGRN_CHUNK]
    return sg, f, k, bcum, bl


def _hgrn_block_order(direction, nb):
    if direction == 0:
        return lambda j: j
    return lambda j: jnp.where(j == 0, 0, nb - j)


def _hgrn_fwd(p, lbg, direction):
    bs, rows, _ = p.shape
    nb = rows // ROW_BLOCK
    ncb = ROW_BLOCK // HGRN_CHUNK
    upper = direction == 1
    order = _hgrn_block_order(direction, nb)

    def body(f_ref, i_ref, q_ref, lbg_ref, o_ref, s_ref, st):
        @pl.when(pl.program_id(1) == 0)
        def _():
            st[...] = jnp.zeros_like(st)

        lb = _lower_bound(lbg_ref, direction)
        mask = _tri(HGRN_CHUNK, upper)
        tri_f32 = mask.astype(F32)
        for ci in (reversed(range(ncb)) if upper else range(ncb)):
            rs = slice(ci * HGRN_CHUNK, (ci + 1) * HGRN_CHUNK)
            _, _, k, bcum, bl = _chunk_gates(f_ref[0, rs, :], lb, tri_f32, upper)
            q = q_ref[0, rs, :]
            v = i_ref[0, rs, :]
            qi = q * jnp.exp(bcum - 0.5 * bl)
            ki = k * jnp.exp(0.5 * bl - bcum)
            kd = k * jnp.exp(bl - bcum)
            qe = q * jnp.exp(bcum)
            dec = jnp.exp(bl)
            outs = []
            for h in range(N_HEADS):
                ls = slice(h * HEAD_DIM, (h + 1) * HEAD_DIM)
                s_in = st[h]
                s_ref[0, 0, ci, h] = s_in
                a = jnp.where(mask, _dot_nt(qi[:, ls], ki[:, ls]), 0.0)
                outs.append(_dot(a, v[:, ls]) + _dot_nt(qe[:, ls], s_in))
                st[h] = s_in * dec[:, ls] + _dot_tn(v[:, ls], kd[:, ls])
            o_ref[0, rs, :] = jnp.concatenate(outs, axis=1)

    def col(cb):
        return pl.BlockSpec((1, ROW_BLOCK, KW), lambda b, j: (b, order(j), cb))

    return pl.pallas_call(
        body, name=f"hgrn_fwd{direction}", grid=(bs, nb),
        in_specs=[col(direction), col(2), col(3), pl.BlockSpec((2, 2, KW), lambda b, j: (0, 0, 0))],
        out_specs=[pl.BlockSpec((1, ROW_BLOCK, KW), lambda b, j: (b, order(j), 0)),
                   pl.BlockSpec((1, 1, ncb, N_HEADS, HEAD_DIM, HEAD_DIM), lambda b, j: (b, order(j), 0, 0, 0, 0))],
        out_shape=[SDS((bs, rows, KW), F32), SDS((bs, nb, ncb, N_HEADS, HEAD_DIM, HEAD_DIM), F32)],
        scratch_shapes=[pltpu.VMEM((N_HEADS, HEAD_DIM, HEAD_DIM), F32)],
        compiler_params=_params(2),
    )(p, p, p, lbg)


def _hgrn_bwd(p, lbg, s_saved, do_raw, direction, dq_prev=None, dv_prev=None):
    bs, rows, _ = p.shape
    nb = rows // ROW_BLOCK
    ncb = ROW_BLOCK // HGRN_CHUNK
    upper = direction == 1
    fwd_order = _hgrn_block_order(direction, nb)
    order = lambda j: fwd_order(nb - 1 - j)
    last = dq_prev is not None
    out_dt = BF16 if last else F32

    def body(*refs):
        if last:
            f_ref, i_ref, q_ref, lbg_ref, s_ref, do_ref, dqp_ref, dvp_ref, df_ref, dq_ref, dv_ref, dlb_ref, dst = refs
        else:
            f_ref, i_ref, q_ref, lbg_ref, s_ref, do_ref, df_ref, dq_ref, dv_ref, dlb_ref, dst = refs
        b, j = pl.program_id(0), pl.program_id(1)

        @pl.when((b == 0) & (j == 0))
        def _():
            dlb_ref[...] = jnp.zeros_like(dlb_ref)

        @pl.when(j == 0)
        def _():
            dst[...] = jnp.zeros_like(dst)

        lb = _lower_bound(lbg_ref, direction)
        mask = _tri(HGRN_CHUNK, upper)
        mask_t = _tri(HGRN_CHUNK, not upper)
        tri_f32 = mask.astype(F32)
        tri_t_f32 = mask_t.astype(F32)
        dlb_acc = jnp.zeros((1, KW), F32)
        for ci in (range(ncb) if upper else reversed(range(ncb))):
            rs = slice(ci * HGRN_CHUNK, (ci + 1) * HGRN_CHUNK)
            sg, f, k, bcum, bl = _chunk_gates(f_ref[0, rs, :], lb, tri_f32, upper)
            q = q_ref[0, rs, :]
            v = i_ref[0, rs, :]
            do = do_ref[0, rs, :]
            eqm = jnp.exp(bcum - 0.5 * bl)
            ekm = jnp.exp(0.5 * bl - bcum)
            ekd = jnp.exp(bl - bcum)
            eb = jnp.exp(bcum)
            dec = jnp.exp(bl)
            qi, ki, kd, qe = q * eqm, k * ekm, k * ekd, q * eb
            dqi_l, dki_l, dkd_l, dqe_l, dv_l, ddec_l = [], [], [], [], [], []
            for h in range(N_HEADS):
                ls = slice(h * HEAD_DIM, (h + 1) * HEAD_DIM)
                s_in = s_ref[0, 0, ci, h]
                ds_out = dst[h]
                do_h, v_h = do[:, ls], v[:, ls]
                a_t = jnp.where(mask_t, _dot_nt(ki[:, ls], qi[:, ls]), 0.0)
                da = jnp.where(mask, _dot_nt(do_h, v_h), 0.0)
                da_t = jnp.where(mask_t, _dot_nt(v_h, do_h), 0.0)
                dqe_l.append(_dot(do_h, s_in))
                dkd_l.append(_dot(v_h, ds_out))
                dv_l.append(_dot(a_t, do_h) + _dot_nt(kd[:, ls], ds_out))
                dqi_l.append(_dot(da, ki[:, ls]))
                dki_l.append(_dot(da_t, qi[:, ls]))
                ddec_l.append(_colsum(ds_out * s_in))
                dst[h] = ds_out * dec[:, ls] + _dot_tn(do_h, qe[:, ls])
            dqi, dki, dkd, dqe = (jnp.concatenate(t, axis=1) for t in (dqi_l, dki_l, dkd_l, dqe_l))
            dv = jnp.concatenate(dv_l, axis=1)
            ddec = jnp.concatenate(ddec_l, axis=1)
            dq = dqi * eqm + dqe * eb
            dk = dki * ekm + dkd * ekd
            db = dqi * qi - dki * ki - dkd * kd + dqe * qe
            dbl = _colsum(dkd * kd) + ddec * dec
            dg = _dot_f32(tri_t_f32, db) + dbl
            df = dg / f - dk
            dlb_acc = dlb_acc + _colsum(df * (1.0 - sg))
            dfl = df * (1.0 - lb) * sg * (1.0 - sg)
            df_ref[0, rs, :] = dfl.astype(BF16)
            if last:
                dq_ref[0, rs, :] = (dq + dqp_ref[0, rs, :]).astype(out_dt)
                dv_ref[0, rs, :] = (dv + dvp_ref[0, rs, :]).astype(out_dt)
            else:
                dq_ref[0, rs, :] = dq
                dv_ref[0, rs, :] = dv
        dlb_ref[...] += dlb_acc

    def col(cb):
        return pl.BlockSpec((1, ROW_BLOCK, KW), lambda b, j: (b, order(j), cb))

    row = pl.BlockSpec((1, ROW_BLOCK, KW), lambda b, j: (b, order(j), 0))
    in_specs = [col(direction), col(2), col(3), pl.BlockSpec((2, 2, KW), lambda b, j: (0, 0, 0)),
                pl.BlockSpec((1, 1, ncb, N_HEADS, HEAD_DIM, HEAD_DIM), lambda b, j: (b, order(j), 0, 0, 0, 0)), row]
    args = [p, p, p, lbg, s_saved, do_raw]
    if last:
        in_specs += [row, row]
        args += [dq_prev, dv_prev]
    return pl.pallas_call(
        body, name=f"hgrn_bwd{direction}", grid=(bs, nb), in_specs=in_specs,
        out_specs=[row, row, row, pl.BlockSpec((1, KW), lambda b, j: (0, 0))],
        out_shape=[SDS((bs, rows, KW), BF16), SDS((bs, rows, KW), out_dt), SDS((bs, rows, KW), out_dt),
                   SDS((1, KW), F32)],
        scratch_shapes=[pltpu.VMEM((N_HEADS, HEAD_DIM, HEAD_DIM), F32)],
        compiler_params=_params(2),
    )(*args)


def _mix_values(og, u, v, ga, gb, o_raw, gna, lng, lnb, ws_ref, bst, wpa, wpb, wo):
    t = {}
    sog = _sig(og)
    t["sog"], t["silu_og"] = sog, og * sog
    xh_l, r_l = [], []
    for h in range(N_HEADS):
        xh, r = _rms(o_raw[:, h * HEAD_DIM:(h + 1) * HEAD_DIM])
        xh_l.append(xh)
        r_l.append(r)
    t["xh"], t["r"] = jnp.concatenate(xh_l, axis=1), r_l
    gna4 = jnp.concatenate([gna] * N_HEADS, axis=1)
    t["gna4"] = gna4
    t["o_n"] = t["xh"] * gna4
    t["o_a"] = t["o_n"] * t["silu_og"]
    t["gu"], t["tu"] = _gelu(u)
    gv, t["tv"] = _gelu(v)
    mu = jnp.mean(gv, axis=-1, keepdims=True)
    cen = gv - mu
    t["rstd"] = lax.rsqrt(jnp.mean(cen * cen, axis=-1, keepdims=True) + EPS)
    t["xhat"] = cen * t["rstd"]
    vn = t["xhat"] * lng + lnb
    t["vn"] = vn
    chunks = []
    for n in range(ROW_BLOCK // SGU_CHUNK):
        rs = slice(n * SGU_CHUNK, (n + 1) * SGU_CHUNK)
        groups = []
        for g in range(N_HEADS):
            ls = slice(g * HEAD_DIM, (g + 1) * HEAD_DIM)
            groups.append(_dot(ws_ref[g], vn[rs, ls]) + bst[:, g:g + 1])
        chunks.append(jnp.concatenate(groups, axis=1))
    t["mixed"] = jnp.concatenate(chunks, axis=0)
    t["o_bm"] = t["gu"] * t["mixed"]
    t["ya"] = _dot(t["o_a"], wpa)
    t["yb"] = _dot(t["o_bm"], wpb)
    t["sa"], t["sb"] = _sig(ga), _sig(gb)
    t["merged"] = t["sa"] * t["ya"] + t["sb"] * t["yb"]
    t["mix"] = _dot(t["merged"], wo)
    return t


def _mix_in_specs(row_of):
    def col(cb):
        return pl.BlockSpec((1, ROW_BLOCK, KW), lambda b, j: (b, row_of(j), cb))
    return [col(cb) for cb in range(4, 11)]


def _mix_param_specs():
    full2 = lambda r, c: pl.BlockSpec((r, c), lambda b, j: (0, 0))
    return [full2(1, HEAD_DIM), full2(1, KW), full2(1, KW),
            pl.BlockSpec((N_HEADS, SGU_CHUNK, SGU_CHUNK), lambda b, j: (0, 0, 0)),
            full2(SGU_CHUNK, N_HEADS), full2(KW, D_MODEL), full2(KW, D_MODEL), full2(D_MODEL, D_MODEL)]


def _mix_fwd(p, o_f, o_b, x, mod_c, gna, lng, lnb, w_s, bst, wpa, wpb, wo):
    bs, seq, _ = x.shape
    nbl = seq // ROW_BLOCK

    def body(og_r, u_r, v_r, ga0_r, ga1_r, gb0_r, gb1_r, of_r, ob_r, x_r, mod_r,
             gna_r, lng_r, lnb_r, ws_r, bst_r, wpa_r, wpb_r, wo_r, x1_r):
        ga = jnp.concatenate([ga0_r[0], ga1_r[0]], axis=1)
        gb = jnp.concatenate([gb0_r[0], gb1_r[0]], axis=1)
        t = _mix_values(og_r[0], u_r[0], v_r[0], ga, gb, of_r[0] + ob_r[0], gna_r[...], lng_r[...], lnb_r[...],
                        ws_r, bst_r[...], wpa_r[...], wpb_r[...], wo_r[...])
        x1_r[0] = x_r[0] + mod_r[0, 0:1, :] * t["mix"]

    row = lambda w: pl.BlockSpec((1, ROW_BLOCK, w), lambda b, j: (b, j + 1, 0))
    lat = pl.BlockSpec((1, ROW_BLOCK, D_MODEL), lambda b, j: (b, j, 0))
    return pl.pallas_call(
        body, name="mix_fwd", grid=(bs, nbl),
        in_specs=_mix_in_specs(lambda j: j + 1) + [row(KW), row(KW), lat,
                                                    pl.BlockSpec((1, 8, D_MODEL), lambda b, j: (b, 0, 0))]
        + _mix_param_specs(),
        out_specs=lat, out_shape=SDS(x.shape, F32), compiler_params=_params(2),
    )(p, p, p, p, p, p, p, o_f, o_b, x, mod_c, gna, lng, lnb, w_s, bst, wpa, wpb, wo)


def _mix_bwd(p, o_f, o_b, dx1, mod_c, gna, lng, lnb, w_s, w_s_t, bst, wpa, wpb, wo):
    bs, rows, _ = p.shape
    nb = rows // ROW_BLOCK

    def body(og_r, u_r, v_r, ga0_r, ga1_r, gb0_r, gb1_r, of_r, ob_r, dx1_r, mod_r,
             gna_r, lng_r, lnb_r, ws_r, bst_r, wpa_r, wpb_r, wo_r, wst_r,
             dor_r, dpc_r, dwpa_r, dwpb_r, dwo_r, dgna_r, dlng_r, dlnb_r, dws_r, dbst_r, dmod_r):
        b, j = pl.program_id(0), pl.program_id(1)

        @pl.when((b == 0) & (j == 0))
        def _():
            for r in (dwpa_r, dwpb_r, dwo_r, dgna_r, dlng_r, dlnb_r, dws_r, dbst_r):
                r[...] = jnp.zeros_like(r)

        @pl.when(j == 0)
        def _():
            dmod_r[...] = jnp.zeros_like(dmod_r)
            dor_r[...] = jnp.zeros_like(dor_r)
            dpc_r[...] = jnp.zeros_like(dpc_r)

        @pl.when(j > 0)
        def _():
            og, u, v = og_r[0], u_r[0], v_r[0]
            ga = jnp.concatenate([ga0_r[0], ga1_r[0]], axis=1)
            gb = jnp.concatenate([gb0_r[0], gb1_r[0]], axis=1)
            gna, lng = gna_r[...], lng_r[...]
            wpa, wpb, wo = wpa_r[...], wpb_r[...], wo_r[...]
            t = _mix_values(og, u, v, ga, gb, of_r[0] + ob_r[0], gna, lng, lnb_r[...],
                            ws_r, bst_r[...], wpa, wpb, wo)
            dx1 = dx1_r[0]
            dmod_r[0, 0:1, :] += _colsum(dx1 * t["mix"])
            dmix = mod_r[0, 0:1, :] * dx1
            dmerged = _dot_nt(dmix, wo)
            dwo_r[...] += _dot_tn(t["merged"], dmix)
            sa, sb = t["sa"], t["sb"]
            dya, dyb = sa * dmerged, sb * dmerged
            dga = dmerged * t["ya"] * sa * (1.0 - sa)
            dgb = dmerged * t["yb"] * sb * (1.0 - sb)
            do_a = _dot_nt(dya, wpa)
            dwpa_r[...] += _dot_tn(t["o_a"], dya)
            do_bm = _dot_nt(dyb, wpb)
            dwpb_r[...] += _dot_tn(t["o_bm"], dyb)
            sog = t["sog"]
            dog = do_a * t["o_n"] * (sog * (1.0 + og * (1.0 - sog)))
            do_n = do_a * t["silu_og"]
            dxh = do_n * t["gna4"]
            prod = do_n * t["xh"]
            dgna = jnp.zeros((1, HEAD_DIM), F32)
            dor_l = []
            for h in range(N_HEADS):
                ls = slice(h * HEAD_DIM, (h + 1) * HEAD_DIM)
                dgna = dgna + _colsum(prod[:, ls])
                dor_l.append(_rms_bwd(dxh[:, ls], t["xh"][:, ls], t["r"][h]))
            dgna_r[...] += dgna
            dor_r[0] = jnp.concatenate(dor_l, axis=1)
            du = do_bm * t["mixed"] * _dgelu(u, t["tu"])
            dmixed = do_bm * t["gu"]
            vn = t["vn"]
            dvn_chunks = []
            for n in range(ROW_BLOCK // SGU_CHUNK):
                rs = slice(n * SGU_CHUNK, (n + 1) * SGU_CHUNK)
                groups = []
                for g in range(N_HEADS):
                    ls = slice(g * HEAD_DIM, (g + 1) * HEAD_DIM)
                    dm = dmixed[rs, ls]
                    dws_r[g] += _dot_nt(dm, vn[rs, ls])
                    dbst_r[:, g:g + 1] += jnp.sum(dm, axis=1, keepdims=True)
                    groups.append(_dot(wst_r[g], dm))
                dvn_chunks.append(jnp.concatenate(groups, axis=1))
            dvn = jnp.concatenate(dvn_chunks, axis=0)
            xhat = t["xhat"]
            dlng_r[...] += _colsum(dvn * xhat)
            dlnb_r[...] += _colsum(dvn)
            dxhat = dvn * lng
            dgv = t["rstd"] * (dxhat - jnp.mean(dxhat, axis=-1, keepdims=True)
                               - xhat * jnp.mean(dxhat * xhat, axis=-1, keepdims=True))
            dv = dgv * _dgelu(v, t["tv"])
            dpc_r[0] = jnp.concatenate([dog, du, dv, dga, dgb], axis=1).astype(BF16)

    row = lambda w: pl.BlockSpec((1, ROW_BLOCK, w), lambda b, j: (b, j, 0))
    lat = pl.BlockSpec((1, ROW_BLOCK, D_MODEL), lambda b, j: (b, jnp.maximum(j - 1, 0), 0))
    full2 = lambda r, c: pl.BlockSpec((r, c), lambda b, j: (0, 0))
    ws_spec = pl.BlockSpec((N_HEADS, SGU_CHUNK, SGU_CHUNK), lambda b, j: (0, 0, 0))
    return pl.pallas_call(
        body, name="mix_bwd", grid=(bs, nb),
        in_specs=_mix_in_specs(lambda j: j) + [row(KW), row(KW), lat,
                                                pl.BlockSpec((1, 8, D_MODEL), lambda b, j: (b, 0, 0))]
        + _mix_param_specs() + [ws_spec],
        out_specs=[row(KW), row(7 * KW), full2(KW, D_MODEL), full2(KW, D_MODEL), full2(D_MODEL, D_MODEL),
                   full2(1, HEAD_DIM), full2(1, KW), full2(1, KW), ws_spec, full2(SGU_CHUNK, N_HEADS),
                   pl.BlockSpec((1, 8, D_MODEL), lambda b, j: (b, 0, 0))],
        out_shape=[SDS((bs, rows, KW), F32), SDS((bs, rows, 7 * KW), BF16), SDS((KW, D_MODEL), F32),
                   SDS((KW, D_MODEL), F32), SDS((D_MODEL, D_MODEL), F32), SDS((1, HEAD_DIM), F32),
                   SDS((1, KW), F32), SDS((1, KW), F32), SDS((N_HEADS, SGU_CHUNK, SGU_CHUNK), F32),
                   SDS((SGU_CHUNK, N_HEADS), F32), SDS((bs, 8, D_MODEL), F32)],
        compiler_params=_params(2),
    )(p, p, p, p, p, p, p, o_f, o_b, dx1, mod_c, gna, lng, lnb, w_s, bst, wpa, wpb, wo, w_s_t)


def _ffn(x1, target, mod_c, g_ffn, g_final, w_up, w_down):
    bs, seq, _ = x1.shape
    nbl = seq // ROW_BLOCK

    def body(x1_r, tg_r, mod_r, gf_r, gl_r, wu_r, wd_r,
             dx1_r, h2_r, dab_r, hid_r, dffn_r, loss_r, dgl_r, dgf_r, dmod_r):
        b, j = pl.program_id(0), pl.program_id(1)

        @pl.when((b == 0) & (j == 0))
        def _():
            for r in (loss_r, dgl_r, dgf_r):
                r[...] = jnp.zeros_like(r)

        @pl.when(j == 0)
        def _():
            dmod_r[...] = jnp.zeros_like(dmod_r)

        x1 = x1_r[0]
        shift, scale, gate = mod_r[0, 1:2, :], mod_r[0, 2:3, :], mod_r[0, 3:4, :]
        gf, gl = gf_r[...], gl_r[...]
        xn2, r2 = _rms(x1)
        hn2 = xn2 * gf
        h2 = (hn2 * (1.0 + scale) + shift).astype(BF16)
        h2_r[0] = h2
        ab = jnp.dot(h2, wu_r[...], preferred_element_type=F32)
        a, bb = ab[:, :D_FF], ab[:, D_FF:]
        sa = _sig(a)
        silu_a = a * sa
        hid = (silu_a * bb).astype(BF16)
        hid_r[0] = hid
        ffn = jnp.dot(hid, wd_r[...], preferred_element_type=F32)
        x2 = x1 + gate * ffn
        xn3, r3 = _rms(x2)
        err = xn3 * gl - tg_r[0]
        loss_r[...] += 0.5 * jnp.sum(jnp.mean(err * err, axis=-1, keepdims=True), axis=0, keepdims=True)
        dy = err * (1.0 / D_MODEL)
        dgl_r[...] += _colsum(dy * xn3)
        dx2 = _rms_bwd(dy * gl, xn3, r3)
        dmod_r[0, 3:4, :] += _colsum(dx2 * ffn)
        dffn = (gate * dx2).astype(BF16)
        dffn_r[0] = dffn
        dhid = lax.dot_general(dffn, wd_r[...], (((1,), (1,)), ((), ())), preferred_element_type=F32)
        da = dhid * bb * (sa * (1.0 + a * (1.0 - sa)))
        db = dhid * silu_a
        dab = jnp.concatenate([da, db], axis=1).astype(BF16)
        dab_r[0] = dab
        dh2 = lax.dot_general(dab, wu_r[...], (((1,), (1,)), ((), ())), preferred_element_type=F32)
        dmod_r[0, 1:2, :] += _colsum(dh2)
        dmod_r[0, 2:3, :] += _colsum(dh2 * hn2)
        dhn2 = dh2 * (1.0 + scale)
        dgf_r[...] += _colsum(dhn2 * xn2)
        dx1_r[0] = dx2 + _rms_bwd(dhn2 * gf, xn2, r2)

    lat = lambda w: pl.BlockSpec((1, ROW_BLOCK, w), lambda b, j: (b, j, 0))
    full2 = lambda r, c: pl.BlockSpec((r, c), lambda b, j: (0, 0))
    mod_spec = pl.BlockSpec((1, 8, D_MODEL), lambda b, j: (b, 0, 0))
    return pl.pallas_call(
        body, name="ffn", grid=(bs, nbl),
        in_specs=[lat(D_MODEL), lat(D_MODEL), mod_spec, full2(1, D_MODEL), full2(1, D_MODEL),
                  full2(D_MODEL, 2 * D_FF), full2(D_FF, D_MODEL)],
        out_specs=[lat(D_MODEL), lat(D_MODEL), lat(2 * D_FF), lat(D_FF), lat(D_MODEL),
                   full2(1, 1), full2(1, D_MODEL), full2(1, D_MODEL), mod_spec],
        out_shape=[SDS(x1.shape, F32), SDS(x1.shape, BF16), SDS((bs, seq, 2 * D_FF), BF16),
                   SDS((bs, seq, D_FF), BF16), SDS(x1.shape, BF16), SDS((1, 1), F32),
                   SDS((1, D_MODEL), F32), SDS((1, D_MODEL), F32), SDS((bs, 8, D_MODEL), F32)],
        compiler_params=_params(2),
    )(x1, target, mod_c, g_ffn, g_final, w_up, w_down)


def _matmul_tn(a, b, n_blocks, tk, name):
    t, m = a.shape
    n = b.shape[1]
    tn = n // n_blocks

    def body(a_ref, b_ref, o_ref):
        @pl.when(pl.program_id(1) == 0)
        def _():
            o_ref[...] = jnp.zeros_like(o_ref)
        o_ref[0] += _dot_tn(a_ref[...], b_ref[...])

    return pl.pallas_call(
        body, name=name, grid=(n_blocks, t // tk),
        in_specs=[pl.BlockSpec((tk, m), lambda i, k: (k, 0)), pl.BlockSpec((tk, tn), lambda i, k: (k, i))],
        out_specs=pl.BlockSpec((1, m, tn), lambda i, k: (i, 0, 0)),
        out_shape=SDS((n_blocks, m, tn), F32), compiler_params=_params(2),
    )(a, b)


SMALL_ROWS = 80
ROW_CCTX = 3


def _small_reduce(gathered, lbg):
    def body(g_ref, lbg_ref, s_ref, dgam_ref):
        tot = g_ref[0:SMALL_ROWS, :]
        for dev in range(1, N_DEV):
            tot = tot + g_ref[dev * SMALL_ROWS:(dev + 1) * SMALL_ROWS, :]
        s_ref[...] = tot
        cc = g_ref[ROW_CCTX:ROW_CCTX + 1, :]
        for dev in range(2, N_DEV, 2):
            cc = cc + g_ref[dev * SMALL_ROWS + ROW_CCTX:dev * SMALL_ROWS + ROW_CCTX + 1, :]
        s_ref[ROW_CCTX:ROW_CCTX + 1, :] = cc
        dlb = tot[7:8, :]
        for d in range(2):
            s0 = _sig(lbg_ref[0, d:d + 1, :] - lbg_ref[1, d:d + 1, :])
            dgam_ref[d:d + 1, :] = dlb[:, d * KW:(d + 1) * KW] * s0 * (1.0 - s0)

    return pl.pallas_call(
        body, name="small_reduce", out_shape=[SDS((SMALL_ROWS, D_MODEL), F32), SDS((2, KW), F32)],
        in_specs=[VMEM_SPEC] * 2, out_specs=[VMEM_SPEC] * 2,
    )(gathered, lbg)


def _pad_cols(a, width):
    return jnp.pad(a, ((0, 0), (0, width - a.shape[1])))


def kernel(x, c, ctx, c_ctx, w_mod, b_mod, g_mix, g_ffn, w_in, lb_gamma, g_norm_a, ln_v_g, ln_v_b, w_s, b_s, w_pa, w_pb, w_o, w_up, w_down, g_final, loss_target, m_c_ctx, m_w_mod, m_b_mod, m_g_mix, m_g_ffn, m_w_in, m_lb_gamma, m_g_norm_a, m_ln_v_g, m_ln_v_b, m_w_s, m_b_s, m_w_pa, m_w_pb, m_w_o, m_w_up, m_w_down, m_g_final, v_c_ctx, v_w_mod, v_b_mod, v_g_mix, v_g_ffn, v_w_in, v_lb_gamma, v_g_norm_a, v_ln_v_g, v_ln_v_b, v_w_s, v_b_s, v_w_pa, v_w_pb, v_w_o, v_w_up, v_w_down, v_g_final):
    ax, ay, ac = lax.axis_index("x"), lax.axis_index("y"), lax.axis_index("c")
    kc = 2 * ax + ay
    dev = 2 * kc + ac
    pos = jnp.stack([kc, ac]).astype(jnp.int32)
    bs, seq, _ = x.shape
    assert bs <= 4 and ctx.shape[1] == ROW_BLOCK and seq % ROW_BLOCK == 0
    mod_cols = w_mod.shape[2]

    lbg_row = _pad_cols(lb_gamma.reshape(1, -1), D_MODEL)
    pay1 = jnp.concatenate([c, jnp.zeros((4 - bs, D_MODEL), F32), c_ctx[None, :], lbg_row,
                            jnp.zeros((2, D_MODEL), F32)], axis=0)
    cond64 = _all_gather8(pay1, "gather_cond")
    lbg_full = cond64.reshape(N_DEV, 8, D_MODEL)[0::2, 5, :KW].reshape(N_CHIPS, 2, 2, HEAD_DIM)
    lbg_full = jnp.transpose(lbg_full, (1, 2, 0, 3)).reshape(2, 2, KW)

    b_mod_s = lax.dynamic_slice(b_mod, (0, kc * mod_cols), (1, mod_cols))
    mod_s = _mod_fwd(cond64, w_mod[0], b_mod_s)
    mod_g = _all_gather8(mod_s, "gather_mod").reshape(N_DEV, 64, mod_cols)[0::2]
    mod_full = jnp.transpose(mod_g, (1, 0, 2)).reshape(64, N_CHIPS * mod_cols)
    mod_mine = lax.dynamic_slice(mod_full, (dev * 8, 0), (8, 6 * D_MODEL)).reshape(8, 6, D_MODEL)
    mod, mc = mod_mine[:bs], mod_mine[4]
    zeros4 = jnp.zeros((bs, 4, D_MODEL), F32)
    mod_a = jnp.concatenate([mod[:, 0:2], jnp.broadcast_to(mc[None, 0:2], (bs, 2, D_MODEL)), zeros4], axis=1)
    mod_c = jnp.concatenate([mod[:, 2:6], zeros4], axis=1)

    shards = [w_in[0], w_up[0], w_pa[0], w_pb[0], w_o[0], w_down[0]]
    gathered = _gather_weights(_cast_bf16(pos, shards))

    def cols_major(a):
        return jnp.transpose(a, (1, 0, 2)).reshape(a.shape[1], -1)

    w_in_f, w_up_f, w_pa_f, w_pb_f = (cols_major(a) for a in gathered[:4])
    w_o_f = gathered[4].reshape(-1, D_MODEL)
    w_down_f = gathered[5].reshape(-1, D_MODEL)

    gna, lng, lnb = g_norm_a, ln_v_g, ln_v_b
    ws3 = w_s[0]
    ws3_t = jnp.transpose(ws3, (0, 2, 1))
    bst = jnp.transpose(b_s[0])

    p, h_all = _in_fwd(x, ctx, g_mix, mod_a, w_in_f)
    o_f, s_f = _hgrn_fwd(p, lbg_full, 0)
    o_b, s_b = _hgrn_fwd(p, lbg_full, 1)
    x1 = _mix_fwd(p, o_f, o_b, x, mod_c, gna, lng, lnb, ws3, bst, w_pa_f, w_pb_f, w_o_f)
    dx1, h2, dab, hid, dffn, loss_part, dg_final, dg_ffn, dmod_ffn = _ffn(
        x1, loss_target, mod_c, g_ffn, g_final[None, :], w_up_f, w_down_f)
    (do_raw, dpc, dw_pa, dw_pb, dw_o, dgna, dlng, dlnb, dws, dbst, dmod_mix) = _mix_bwd(
        p, o_f, o_b, dx1, mod_c, gna, lng, lnb, ws3, ws3_t, bst, w_pa_f, w_pb_f, w_o_f)
    df_f, dq0, dv0, dlb0 = _hgrn_bwd(p, lbg_full, s_f, do_raw, 0)
    df_b, dq, di, dlb1 = _hgrn_bwd(p, lbg_full, s_b, do_raw, 1, dq0, dv0)
    grad_x, dp, dg_mix, dmod_in = _in_bwd(x, ctx, dx1, g_mix, mod_a, w_in_f, df_f, df_b, di, dq, dpc)

    rows_all = dp.shape[0] * dp.shape[1]
    rows_lat = bs * seq
    tk_all = 2 * ROW_BLOCK if rows_all % (2 * ROW_BLOCK) == 0 else ROW_BLOCK
    tk_lat = 2 * ROW_BLOCK if rows_lat % (2 * ROW_BLOCK) == 0 else ROW_BLOCK
    dw_in = _matmul_tn(h_all.reshape(rows_all, D_MODEL), dp.reshape(rows_all, IN_COLS), N_CHIPS, tk_all, "dw_in")
    dw_up = _matmul_tn(h2.reshape(rows_lat, D_MODEL), dab.reshape(rows_lat, 2 * D_FF), N_CHIPS, tk_lat, "dw_up")
    dw_down = _matmul_tn(hid.reshape(rows_lat, D_FF), dffn.reshape(rows_lat, D_MODEL), 1, tk_lat, "dw_down")

    def shard_major(a):
        return jnp.transpose(a.reshape(a.shape[0], N_CHIPS, -1), (1, 0, 2))

    partials = [dw_in, dw_up, shard_major(dw_pa), shard_major(dw_pb),
                dw_o.reshape(N_CHIPS, -1, D_MODEL), dw_down.reshape(N_CHIPS, -1, D_MODEL)]

    big_names = ["w_in", "w_up", "w_pa", "w_pb", "w_o", "w_down"]
    recv_sib = _rs_sibling_halves(partials)
    chip_sums = [_rs_add_halves(pos, g, r, "rs_add_" + nm) for g, r, nm in zip(partials, recv_sib, big_names)]
    recv_own = _rs_to_owner([cs[1] for cs in chip_sums])
    half_tot = [_rs_sum_owner(pos, cs[0], r, "rs_sum_" + nm) for cs, r, nm in zip(chip_sums, recv_own, big_names)]
    g_big = _rs_join_halves(half_tot)
    big_m = [m_w_in, m_w_up, m_w_pa, m_w_pb, m_w_o, m_w_down]
    big_v = [v_w_in, v_w_up, v_w_pa, v_w_pb, v_w_o, v_w_down]
    res = {}
    for name, w, g, m, v in zip(big_names, shards, g_big, big_m, big_v):
        d, m2, v2 = _adamw_big(w, g, m[0], v[0], "adamw_" + name)
        res[name] = (g[None], d[None], m2[None], v2[None])

    dmod_mine = jnp.concatenate([dmod_in[:, 0], dmod_in[:, 1], dmod_mix[:, 0], dmod_ffn[:, 1], dmod_ffn[:, 2],
                                 dmod_ffn[:, 3]], axis=1)
    dmc = jnp.concatenate([jnp.sum(dmod_in[:, 2], axis=0), jnp.sum(dmod_in[:, 3], axis=0),
                           jnp.zeros((4 * D_MODEL,), F32)])[None, :]
    pay3 = jnp.concatenate([dmod_mine, jnp.zeros((4 - bs, 6 * D_MODEL), F32), dmc,
                            jnp.zeros((3, 6 * D_MODEL), F32)], axis=0)
    dmod64 = _all_gather8(pay3, "gather_dmod")
    dmod64_my = lax.dynamic_slice(dmod64, (0, kc * mod_cols), (64, mod_cols))
    g_w_mod, g_b_mod, g_cctx_part = _mod_bwd(cond64, dmod64, dmod64_my, w_mod[0], c_ctx[None, :])
    d, m2, v2 = _adamw_big(w_mod[0], g_w_mod, m_w_mod[0], v_w_mod[0], "adamw_w_mod")
    res["w_mod"] = (g_w_mod[None], d[None], m2[None], v2[None])

    def row(*parts):
        return _pad_cols(jnp.concatenate([q.reshape(1, -1) for q in parts], axis=1), D_MODEL)

    small_rows = [dg_mix, dg_ffn, dg_final, g_cctx_part, row(dgna), row(dlng, dlnb), row(jnp.transpose(dbst)),
                  row(dlb0, dlb1), row(loss_part), jnp.zeros((7, D_MODEL), F32), dws.reshape(64, D_MODEL)]
    pay4 = jnp.concatenate(small_rows, axis=0)
    tot, dgam0 = _small_reduce(_all_gather8(pay4, "gather_small"), lbg_full)
    loss = tot[8, 0]
    dgam_full = jnp.stack([dgam0, -dgam0])
    g_lbg = lax.dynamic_slice(dgam_full, (0, 0, kc * HEAD_DIM), (2, 2, HEAD_DIM))

    small = [
        ("c_ctx", c_ctx[None, :], tot[3:4], m_c_ctx, v_c_ctx),
        ("b_mod", b_mod, g_b_mod, m_b_mod, v_b_mod),
        ("g_mix", g_mix, tot[0:1], m_g_mix, v_g_mix),
        ("g_ffn", g_ffn, tot[1:2], m_g_ffn, v_g_ffn),
        ("lb_gamma", lb_gamma.reshape(4, HEAD_DIM), g_lbg.reshape(4, HEAD_DIM), m_lb_gamma, v_lb_gamma),
        ("g_norm_a", g_norm_a, tot[4:5, :HEAD_DIM], m_g_norm_a, v_g_norm_a),
        ("ln_v_g", ln_v_g, tot[5:6, :KW], m_ln_v_g, v_ln_v_g),
        ("ln_v_b", ln_v_b, tot[5:6, KW:], m_ln_v_b, v_ln_v_b),
        ("w_s", w_s.reshape(N_HEADS * SGU_CHUNK, SGU_CHUNK), tot[16:80].reshape(N_HEADS * SGU_CHUNK, SGU_CHUNK),
         m_w_s, v_w_s),
        ("b_s", b_s[0], tot[6:7, :KW].reshape(N_HEADS, SGU_CHUNK), m_b_s, v_b_s),
        ("g_final", g_final[None, :], tot[2:3], m_g_final, v_g_final),
    ]
    ws_, gs_ = [s[1] for s in small], [s[2] for s in small]
    ms_ = [s[3].reshape(s[1].shape) for s in small]
    vs_ = [s[4].reshape(s[1].shape) for s in small]
    ds_, m2s_, v2s_ = _adamw_small(ws_, gs_, ms_, vs_)
    for (name, _, g, m, _), d, m2, v2 in zip(small, ds_, m2s_, v2s_):
        res[name] = tuple(t.reshape(m.shape) for t in (g, d, m2, v2))

    order = ["c_ctx", "w_mod", "b_mod", "g_mix", "g_ffn", "w_in", "lb_gamma", "g_norm_a", "ln_v_g", "ln_v_b",
             "w_s", "b_s", "w_pa", "w_pb", "w_o", "w_up", "w_down", "g_final"]
    outs = [loss, grad_x]
    for part in range(4):
        outs += [res[n][part] for n in order]
    return tuple(outs)
```

```python
import functools
import math

import jax
import jax.numpy as jnp
from jax import lax
from jax.experimental import pallas as pl
from jax.experimental.pallas import tpu as pltpu

F32 = jnp.float32
BF16 = jnp.bfloat16
SDS = jax.ShapeDtypeStruct
MESH = pl.DeviceIdType.MESH

EPS = 1e-6
D_MODEL = 1024
N_HEADS = 4
HEAD_DIM = 128
KW = N_HEADS * HEAD_DIM
IN_COLS = 11 * KW
D_FF = 2816
HGRN_CHUNK = 64
SGU_CHUNK = 128
ROW_BLOCK = 256
N_CHIPS = 4
N_DEV = 8
V7X_VMEM_BYTES = 64 * 1024 * 1024
VMEM_LIMIT = V7X_VMEM_BYTES - 6 * 1024 * 1024

ADAM_LR, ADAM_B1, ADAM_B2, ADAM_EPS, ADAM_WD, ADAM_STEP = 0.001, 0.9, 0.999, 1e-08, 0.01, 10
GELU_C0 = math.sqrt(2.0 / math.pi)
GELU_C1 = 0.044715

VMEM_SPEC = pl.BlockSpec(memory_space=pltpu.VMEM)
ANY_SPEC = pl.BlockSpec(memory_space=pl.ANY)


def _params(n_grid):
    return pltpu.CompilerParams(dimension_semantics=("arbitrary",) * n_grid, vmem_limit_bytes=VMEM_LIMIT)


def _sig(x):
    return 1.0 / (1.0 + jnp.exp(-x))


def _gelu(x):
    t = jnp.tanh(GELU_C0 * (x + GELU_C1 * x * x * x))
    return 0.5 * x * (1.0 + t), t


def _dgelu(x, t):
    return 0.5 * (1.0 + t) + 0.5 * x * (1.0 - t * t) * GELU_C0 * (1.0 + 3.0 * GELU_C1 * x * x)


def _dot(a, b):
    return jnp.dot(a.astype(BF16), b.astype(BF16), preferred_element_type=F32)


def _dot_nt(a, b):
    return lax.dot_general(a.astype(BF16), b.astype(BF16), (((1,), (1,)), ((), ())), preferred_element_type=F32)


def _dot_tn(a, b):
    return lax.dot_general(a.astype(BF16), b.astype(BF16), (((0,), (0,)), ((), ())), preferred_element_type=F32)


def _dot_f32(a, b, dims=(((1,), (0,)), ((), ()))):
    return lax.dot_general(a, b, dims, precision=lax.Precision.HIGHEST, preferred_element_type=F32)


def _rms(x):
    r = lax.rsqrt(jnp.mean(x * x, axis=-1, keepdims=True) + EPS)
    return x * r, r


def _rms_bwd(dxn, xn, r):
    return r * (dxn - xn * jnp.mean(dxn * xn, axis=-1, keepdims=True))


def _colsum(a):
    return jnp.sum(a, axis=0, keepdims=True)


def _tri(n, upper):
    t = lax.broadcasted_iota(jnp.int32, (n, n), 0)
    s = lax.broadcasted_iota(jnp.int32, (n, n), 1)
    return (s >= t) if upper else (s <= t)


def _all_gather8(x_shard, name):
    m_per, n = x_shard.shape

    def body(x_ref, out_ref, send_sems, recv_sems, local_sem):
        x, y, c = lax.axis_index("x"), lax.axis_index("y"), lax.axis_index("c")
        me, sibling = (x, y, c), (x, y, 1 - c)
        chips = [(1 - x, y), (x, 1 - y), (1 - x, 1 - y)]

        def rows(px, py, pc):
            return out_ref.at[pl.ds((4 * px + 2 * py + pc) * m_per, m_per), :]

        def copy(k, block, to, src=None):
            return pltpu.make_async_remote_copy(
                src_ref=rows(*block) if src is None else src, dst_ref=rows(*block),
                send_sem=send_sems.at[k], recv_sem=recv_sems.at[k], device_id=to, device_id_type=MESH)

        mine = pltpu.make_async_copy(x_ref, rows(*me), local_sem)
        mine.start()
        first = [copy(0, me, sibling, src=x_ref)]
        first += [copy(1 + j, me, (*chip, c), src=x_ref) for j, chip in enumerate(chips)]
        for cp in first:
            cp.start()
        passed = [copy(4 + j, (*chip, c), sibling) for j, chip in enumerate(chips)]
        for j, chip in enumerate(chips):
            copy(1 + j, (*chip, c), me).wait_recv()
            passed[j].start()
        copy(0, sibling, me).wait_recv()
        for j, chip in enumerate(chips):
            copy(4 + j, (*chip, 1 - c), me).wait_recv()
        for cp in first + passed:
            cp.wait_send()
        mine.wait()

    return pl.pallas_call(
        body, name=name, out_shape=SDS((N_DEV * m_per, n), x_shard.dtype),
        in_specs=[VMEM_SPEC], out_specs=VMEM_SPEC,
        scratch_shapes=[pltpu.SemaphoreType.DMA((7,)), pltpu.SemaphoreType.DMA((7,)), pltpu.SemaphoreType.DMA],
    )(x_shard)


def _mesh_pos():
    x, y, c = lax.axis_index("x"), lax.axis_index("y"), lax.axis_index("c")
    chips = [(1 - x, y), (x, 1 - y), (1 - x, 1 - y)]
    return x, y, c, 2 * x + y, (x, y, 1 - c), chips


def _half_rows(c, rh):
    return pl.ds(pl.multiple_of(c * rh, 16), rh)


def _gather_weights(bufs):
    n = len(bufs)

    def body(*refs):
        outs = refs[n:2 * n]
        send_sems, recv_sems = refs[2 * n:]
        x, y, c, kc, sibling, chips = _mesh_pos()
        firsts, passed = [], []
        for wi in range(n):
            rh = outs[wi].shape[1] // 2
            for jj, chip in enumerate(chips):
                mine = outs[wi].at[kc, _half_rows(c, rh), :]
                cp = pltpu.make_async_remote_copy(
                    src_ref=mine, dst_ref=mine, send_sem=send_sems.at[wi, jj], recv_sem=recv_sems.at[wi, jj],
                    device_id=(*chip, c), device_id_type=MESH)
                cp.start()
                firsts.append(cp)
        for wi in range(n):
            rh = outs[wi].shape[1] // 2
            for jj, chip in enumerate(chips):
                blk = outs[wi].at[2 * chip[0] + chip[1], _half_rows(c, rh), :]
                pltpu.make_async_remote_copy(
                    src_ref=blk, dst_ref=blk, send_sem=send_sems.at[wi, jj], recv_sem=recv_sems.at[wi, jj],
                    device_id=(*chip, c), device_id_type=MESH).wait_recv()
                fw = pltpu.make_async_remote_copy(
                    src_ref=blk, dst_ref=blk, send_sem=send_sems.at[wi, 3 + jj], recv_sem=recv_sems.at[wi, 3 + jj],
                    device_id=sibling, device_id_type=MESH)
                fw.start()
                passed.append(fw)
        for wi in range(n):
            rh = outs[wi].shape[1] // 2
            for jj, chip in enumerate(chips):
                blk = outs[wi].at[2 * chip[0] + chip[1], _half_rows(1 - c, rh), :]
                pltpu.make_async_remote_copy(
                    src_ref=blk, dst_ref=blk, send_sem=send_sems.at[wi, 3 + jj], recv_sem=recv_sems.at[wi, 3 + jj],
                    device_id=sibling, device_id_type=MESH).wait_recv()
        for cp in firsts + passed:
            cp.wait_send()

    return pl.pallas_call(
        body, name="gather_weights",
        out_shape=[SDS(b.shape, b.dtype) for b in bufs],
        in_specs=[ANY_SPEC] * n, out_specs=[ANY_SPEC] * n,
        input_output_aliases={i: i for i in range(n)},
        scratch_shapes=[pltpu.SemaphoreType.DMA((n, 6)), pltpu.SemaphoreType.DMA((n, 6))],
    )(*bufs)


class _Carry:
    def __init__(self, ins, outs, alias, sems, copies):
        self.ins, self.outs, self.alias, self.sems, self.copies = list(ins), list(outs), dict(alias), list(sems), copies


def _remote(src, dst, send, recv, to):
    return functools.partial(pltpu.make_async_remote_copy, src_ref=src, dst_ref=dst, send_sem=send, recv_sem=recv,
                             device_id=to, device_id_type=MESH)


def _carry_gather_send(bufs):
    n = len(bufs)

    def copies(ins, outs, sems):
        x, y, c, kc, sibling, chips = _mesh_pos()
        starts, waits = [], []
        for wi in range(n):
            rh = outs[wi].shape[1] // 2
            for jj, chip in enumerate(chips):
                mine = outs[wi].at[kc, _half_rows(c, rh), :]
                cp = _remote(mine, mine, sems[0].at[wi, jj], sems[1].at[wi, jj], (*chip, c))
                starts.append(cp)
                waits.append((cp, "send"))
                theirs = outs[wi].at[2 * chip[0] + chip[1], _half_rows(c, rh), :]
                waits.append((_remote(theirs, theirs, sems[0].at[wi, jj], sems[1].at[wi, jj], (*chip, c)), "recv"))
        return starts, waits

    return _Carry(bufs, [SDS(b.shape, b.dtype) for b in bufs], {i: i for i in range(n)},
                  [pltpu.SemaphoreType.DMA((n, 3)), pltpu.SemaphoreType.DMA((n, 3))], copies)


def _carry_gather_forward(bufs):
    n = len(bufs)

    def copies(ins, outs, sems):
        x, y, c, kc, sibling, chips = _mesh_pos()
        starts, waits = [], []
        for wi in range(n):
            rh = outs[wi].shape[1] // 2
            for jj, chip in enumerate(chips):
                got = outs[wi].at[2 * chip[0] + chip[1], _half_rows(c, rh), :]
                cp = _remote(got, got, sems[0].at[wi, jj], sems[1].at[wi, jj], sibling)
                starts.append(cp)
                waits.append((cp, "send"))
                other = outs[wi].at[2 * chip[0] + chip[1], _half_rows(1 - c, rh), :]
                waits.append((_remote(other, other, sems[0].at[wi, jj], sems[1].at[wi, jj], sibling), "recv"))
        return starts, waits

    return _Carry(bufs, [SDS(b.shape, b.dtype) for b in bufs], {i: i for i in range(n)},
                  [pltpu.SemaphoreType.DMA((n, 3)), pltpu.SemaphoreType.DMA((n, 3))], copies)


def _carry_sibling_halves(grads):
    n = len(grads)

    def copies(ins, outs, sems):
        x, y, c, kc, sibling, chips = _mesh_pos()
        cps = [_remote(ins[wi].at[:, _half_rows(1 - c, ins[wi].shape[1] // 2), :], outs[wi],
                       sems[0].at[wi], sems[1].at[wi], sibling) for wi in range(n)]
        return cps, [(cp, "both") for cp in cps]

    return _Carry(grads, [SDS((N_CHIPS, g.shape[1] // 2, g.shape[2]), F32) for g in grads], {},
                  [pltpu.SemaphoreType.DMA((n,)), pltpu.SemaphoreType.DMA((n,))], copies)


def _carry_to_owner(cpbfs):
    n = len(cpbfs)

    def copies(ins, outs, sems):
        x, y, c, kc, sibling, chips = _mesh_pos()
        starts, waits = [], []
        for wi in range(n):
            for jj, chip in enumerate(chips):
                cp = _remote(ins[wi].at[2 * chip[0] + chip[1]], outs[wi].at[kc],
                             sems[0].at[wi, jj], sems[1].at[wi, jj], (*chip, c))
                starts.append(cp)
                waits.append((cp, "send"))
                slot = outs[wi].at[2 * chip[0] + chip[1]]
                waits.append((_remote(slot, slot, sems[0].at[wi, jj], sems[1].at[wi, jj], (*chip, c)), "recv"))
        return starts, waits

    return _Carry(cpbfs, [SDS(g.shape, BF16) for g in cpbfs], {},
                  [pltpu.SemaphoreType.DMA((n, 3)), pltpu.SemaphoreType.DMA((n, 3))], copies)


def _carry_join_halves(bufs):
    n = len(bufs)

    def copies(ins, outs, sems):
        x, y, c, kc, sibling, chips = _mesh_pos()
        cps = []
        for wi in range(n):
            mine = outs[wi].at[_half_rows(c, outs[wi].shape[0] // 2), :]
            cps.append(_remote(mine, mine, sems[0].at[wi], sems[1].at[wi], sibling))
        return cps, [(cp, "both") for cp in cps]

    return _Carry(bufs, [SDS(b.shape, F32) for b in bufs], {i: i for i in range(n)},
                  [pltpu.SemaphoreType.DMA((n,)), pltpu.SemaphoreType.DMA((n,))], copies)


def _merge_carries(*carries):
    ins, outs, alias, sems, parts = [], [], {}, [], []
    for cy in carries:
        parts.append((len(ins), len(cy.ins), len(outs), len(cy.outs), len(sems), len(cy.sems), cy.copies))
        alias.update({len(ins) + i: len(outs) + o for i, o in cy.alias.items()})
        ins += cy.ins
        outs += cy.outs
        sems += cy.sems

    def copies(i, o, s):
        starts, waits = [], []
        for i0, ni, o0, no, s0, ns, fn in parts:
            st, wt = fn(i[i0:i0 + ni], o[o0:o0 + no], s[s0:s0 + ns])
            starts += st
            waits += wt
        return starts, waits

    return _Carry(ins, outs, alias, sems, copies)


def _start_all(starts):
    for cp in starts:
        cp().start()


def _wait_all(waits):
    for cp, which in waits:
        if which == "send":
            cp().wait_send()
        elif which == "recv":
            cp().wait_recv()
        else:
            cp().wait()


def _comm_call(name, carry):
    n_i, n_o = len(carry.ins), len(carry.outs)

    def body(*refs):
        ins, outs, sems = refs[:n_i], refs[n_i:n_i + n_o], refs[n_i + n_o:]
        _start_all(carry.copies(ins, outs, sems)[0])
        _wait_all(carry.copies(ins, outs, sems)[1])

    return pl.pallas_call(
        body, name=name, out_shape=carry.outs, in_specs=[ANY_SPEC] * n_i, out_specs=[ANY_SPEC] * n_o,
        input_output_aliases=carry.alias, scratch_shapes=carry.sems,
    )(*carry.ins)


def _host_call(body, *, name, grid, in_specs, out_specs, out_shape, args, scratch_shapes=(), carry=None):
    n_in, n_out, n_scr = len(in_specs), len(out_specs), len(scratch_shapes)
    if carry is None:
        res = pl.pallas_call(body, name=name, grid=grid, in_specs=in_specs, out_specs=out_specs, out_shape=out_shape,
                             scratch_shapes=list(scratch_shapes), compiler_params=_params(len(grid)))(*args)
        return list(res), []
    n_ci, n_co = len(carry.ins), len(carry.outs)

    def wrapped(*refs):
        ins, cins = refs[:n_in], refs[n_in:n_in + n_ci]
        o0 = n_in + n_ci
        outs, couts = refs[o0:o0 + n_out], refs[o0 + n_out:o0 + n_out + n_co]
        s0 = o0 + n_out + n_co
        scr, sems = refs[s0:s0 + n_scr], refs[s0 + n_scr:]
        idx = [pl.program_id(a) for a in range(len(grid))]
        first = functools.reduce(jnp.logical_and, [i == 0 for i in idx])
        last = functools.reduce(jnp.logical_and, [i == g - 1 for i, g in zip(idx, grid)])

        @pl.when(first)
        def _():
            _start_all(carry.copies(cins, couts, sems)[0])

        body(*ins, *outs, *scr)

        @pl.when(last)
        def _():
            _wait_all(carry.copies(cins, couts, sems)[1])

    res = pl.pallas_call(
        wrapped, name=name, grid=grid, in_specs=list(in_specs) + [ANY_SPEC] * n_ci,
        out_specs=list(out_specs) + [ANY_SPEC] * n_co, out_shape=list(out_shape) + carry.outs,
        scratch_shapes=list(scratch_shapes) + carry.sems,
        input_output_aliases={n_in + i: n_out + o for i, o in carry.alias.items()},
        compiler_params=_params(len(grid)),
    )(*args, *carry.ins)
    return list(res[:n_out]), list(res[n_out:])


def _rs_add_halves(pos, grad, recv, name):
    _, rs, cs = grad.shape
    rh = rs // 2
    rb = rh // 2

    def body(pos_ref, g_ref, r_ref, o32_ref, obf_ref):
        s = g_ref[...] + r_ref[...]
        o32_ref[...] = s
        obf_ref[...] = s.astype(BF16)

    blk = (1, rb, cs)
    return pl.pallas_call(
        body, name=name,
        grid_spec=pltpu.PrefetchScalarGridSpec(
            num_scalar_prefetch=1, grid=(N_CHIPS, 2),
            in_specs=[pl.BlockSpec(blk, lambda k, i, p: (k, p[1] * 2 + i, 0)),
                      pl.BlockSpec(blk, lambda k, i, p: (k, i, 0))],
            out_specs=[pl.BlockSpec(blk, lambda k, i, p: (k, i, 0)), pl.BlockSpec(blk, lambda k, i, p: (k, i, 0))]),
        out_shape=[SDS((N_CHIPS, rh, cs), F32), SDS((N_CHIPS, rh, cs), BF16)],
        compiler_params=_params(2),
    )(pos, grad, recv)


def _rs_sum_owner(pos, cp32, recv3, name):
    _, rh, cs = cp32.shape
    rb = rh // 2

    def body(pos_ref, own_ref, r1_ref, r2_ref, r3_ref, o_ref):
        o_ref[...] = ((own_ref[0] + r1_ref[0].astype(F32)) + r2_ref[0].astype(F32)) + r3_ref[0].astype(F32)

    blk = (1, rb, cs)

    def slot(d):
        return pl.BlockSpec(blk, lambda i, p: ((p[0] + d) % N_CHIPS, i, 0))

    return pl.pallas_call(
        body, name=name,
        grid_spec=pltpu.PrefetchScalarGridSpec(
            num_scalar_prefetch=1, grid=(2,),
            in_specs=[slot(0), slot(1), slot(2), slot(3)],
            out_specs=pl.BlockSpec((rb, cs), lambda i, p: (p[1] * 2 + i, 0))),
        out_shape=SDS((2 * rh, cs), F32),
        compiler_params=_params(1),
    )(pos, cp32, recv3, recv3, recv3)


def _cast_bf16(pos, arrs):
    n = len(arrs)

    def body(pos_ref, *refs):
        for i in range(n):
            refs[n + i][0] = refs[i][...].astype(BF16)

    return pl.pallas_call(
        body, name="cast_bf16",
        grid_spec=pltpu.PrefetchScalarGridSpec(
            num_scalar_prefetch=1, grid=(2,),
            in_specs=[pl.BlockSpec((a.shape[0] // 2, a.shape[1]), lambda i, p: (i, 0)) for a in arrs],
            out_specs=[pl.BlockSpec((1, a.shape[0] // 2, a.shape[1]), lambda i, p: (p[0], i, 0)) for a in arrs]),
        out_shape=[SDS((N_CHIPS,) + a.shape, BF16) for a in arrs],
        compiler_params=_params(1),
    )(pos, *arrs)


def _adamw_vals(w, g, m, v):
    m2 = ADAM_B1 * m + (1.0 - ADAM_B1) * g
    v2 = ADAM_B2 * v + (1.0 - ADAM_B2) * (g * g)
    m_hat = m2 / (1.0 - ADAM_B1 ** ADAM_STEP)
    v_hat = v2 / (1.0 - ADAM_B2 ** ADAM_STEP)
    delta = -ADAM_LR * (m_hat / (jnp.sqrt(v_hat) + ADAM_EPS) + ADAM_WD * w)
    return delta, m2, v2


def _adamw_big(w, g, m, v, name):
    rows, cols = w.shape
    rb = rows // 4

    def body(w_ref, g_ref, m_ref, v_ref, d_ref, m2_ref, v2_ref):
        d, m2, v2 = _adamw_vals(w_ref[...], g_ref[...], m_ref[...], v_ref[...])
        d_ref[...] = d
        m2_ref[...] = m2
        v2_ref[...] = v2

    spec = pl.BlockSpec((rb, cols), lambda i: (i, 0))
    return pl.pallas_call(
        body, name=name, grid=(4,), in_specs=[spec] * 4, out_specs=[spec] * 3,
        out_shape=[SDS(w.shape, F32)] * 3, compiler_params=_params(1),
    )(w, g, m, v)


def _adamw_small(ws, gs, ms, vs):
    n = len(ws)

    def body(*refs):
        for i in range(n):
            d, m2, v2 = _adamw_vals(refs[i][...], refs[n + i][...], refs[2 * n + i][...], refs[3 * n + i][...])
            refs[4 * n + i][...] = d
            refs[5 * n + i][...] = m2
            refs[6 * n + i][...] = v2

    shapes = [SDS(w.shape, F32) for w in ws]
    outs = pl.pallas_call(
        body, name="adamw_small", out_shape=shapes * 3,
        in_specs=[VMEM_SPEC] * (4 * n), out_specs=[VMEM_SPEC] * (3 * n),
    )(*ws, *gs, *ms, *vs)
    return outs[:n], outs[n:2 * n], outs[2 * n:]


def _mod_fwd(cond64, w_mod_s, b_mod_s):
    def body(c_ref, w_ref, b_ref, o_ref):
        cc = c_ref[...]
        o_ref[...] = _dot_f32(cc * _sig(cc), w_ref[...]) + b_ref[...]

    return pl.pallas_call(
        body, name="mod_fwd", out_shape=SDS((cond64.shape[0], w_mod_s.shape[1]), F32),
        in_specs=[VMEM_SPEC] * 3, out_specs=VMEM_SPEC,
        compiler_params=pltpu.CompilerParams(vmem_limit_bytes=VMEM_LIMIT),
    )(cond64, w_mod_s, b_mod_s)


def _mod_bwd(cond64, dmod64, dmod64_my, w_mod_s, c_ctx):
    def body(c_ref, g_ref, gm_ref, w_ref, cc_ref, gw_ref, gb_ref, gcc_ref):
        cc = c_ref[...]
        act = cc * _sig(cc)
        gm = gm_ref[...]
        gw_ref[...] = _dot_f32(act, gm, (((0,), (0,)), ((), ())))
        gb_ref[...] = _colsum(g_ref[...])
        dact = _dot_f32(gm, w_ref[...], (((1,), (1,)), ((), ())))
        tot = dact[4:5, :]
        for dev in range(1, N_DEV):
            tot = tot + dact[8 * dev + 4:8 * dev + 5, :]
        c0 = cc_ref[...]
        s0 = _sig(c0)
        gcc_ref[...] = tot * (s0 * (1.0 + c0 * (1.0 - s0)))

    return pl.pallas_call(
        body, name="mod_bwd",
        out_shape=[SDS(w_mod_s.shape, F32), SDS((1, dmod64.shape[1]), F32), SDS((1, D_MODEL), F32)],
        in_specs=[VMEM_SPEC] * 5, out_specs=[VMEM_SPEC] * 3,
        compiler_params=pltpu.CompilerParams(vmem_limit_bytes=VMEM_LIMIT),
    )(cond64, dmod64, dmod64_my, w_mod_s, c_ctx)


def _in_fwd(x, ctx, g_mix, mod_a, w_in, carry=None):
    bs, seq, _ = x.shape
    nb = seq // ROW_BLOCK + 1

    def body(x_ref, ctx_ref, g_ref, mod_ref, w_ref, p_ref, h_ref):
        is_ctx = pl.program_id(1) == 0
        xin = jnp.where(is_ctx, ctx_ref[0], x_ref[0])
        shift = jnp.where(is_ctx, mod_ref[0, 2:3, :], mod_ref[0, 0:1, :])
        scale = jnp.where(is_ctx, mod_ref[0, 3:4, :], mod_ref[0, 1:2, :])
        xn, _ = _rms(xin)
        hb = ((xn * g_ref[...]) * (1.0 + scale) + shift).astype(BF16)
        h_ref[0] = hb
        p_ref[0] = jnp.dot(hb, w_ref[...], preferred_element_type=F32)

    return _host_call(
        body, name="in_fwd", grid=(bs, nb),
        in_specs=[pl.BlockSpec((1, ROW_BLOCK, D_MODEL), lambda b, j: (b, jnp.maximum(j - 1, 0), 0)),
                  pl.BlockSpec((1, ROW_BLOCK, D_MODEL), lambda b, j: (b, 0, 0)),
                  pl.BlockSpec((1, D_MODEL), lambda b, j: (0, 0)),
                  pl.BlockSpec((1, 8, D_MODEL), lambda b, j: (b, 0, 0)),
                  pl.BlockSpec((D_MODEL, IN_COLS), lambda b, j: (0, 0))],
        out_specs=[pl.BlockSpec((1, ROW_BLOCK, IN_COLS), lambda b, j: (b, j, 0)),
                   pl.BlockSpec((1, ROW_BLOCK, D_MODEL), lambda b, j: (b, j, 0))],
        out_shape=[SDS((bs, nb * ROW_BLOCK, IN_COLS), F32), SDS((bs, nb * ROW_BLOCK, D_MODEL), BF16)],
        args=(x, ctx, g_mix, mod_a, w_in), carry=carry)


def _in_bwd(x, ctx, dx1, g_mix, mod_a, w_in, df_f, df_b, di, dq, dpc, carry=None):
    bs, seq, _ = x.shape
    nb = seq // ROW_BLOCK + 1

    def body(x_ref, ctx_ref, dx1_ref, g_ref, mod_ref, w_ref, dff_ref, dfb_ref, di_ref, dq_ref, dpc_ref,
             gx_ref, dp_ref, dg_ref, dmod_ref):
        b, j = pl.program_id(0), pl.program_id(1)
        is_ctx = j == 0

        @pl.when((b == 0) & (j == 0))
        def _():
            dg_ref[...] = jnp.zeros_like(dg_ref)

        @pl.when(j == 0)
        def _():
            dmod_ref[...] = jnp.zeros_like(dmod_ref)

        dp = jnp.concatenate([dff_ref[0], dfb_ref[0], di_ref[0], dq_ref[0], dpc_ref[0]], axis=1)
        dp_ref[0] = dp
        dh = lax.dot_general(dp, w_ref[...], (((1,), (1,)), ((), ())), preferred_element_type=F32)
        xin = jnp.where(is_ctx, ctx_ref[0], x_ref[0])
        scale = jnp.where(is_ctx, mod_ref[0, 3:4, :], mod_ref[0, 1:2, :])
        xn, r = _rms(xin)
        g = g_ref[...]
        hn = xn * g
        d_shift = _colsum(dh)
        d_scale = _colsum(dh * hn)
        dhn = dh * (1.0 + scale)
        dg_ref[...] += _colsum(dhn * xn)
        dx = _rms_bwd(dhn * g, xn, r)

        @pl.when(is_ctx)
        def _():
            dmod_ref[0, 2:3, :] += d_shift
            dmod_ref[0, 3:4, :] += d_scale

        @pl.when(jnp.logical_not(is_ctx))
        def _():
            dmod_ref[0, 0:1, :] += d_shift
            dmod_ref[0, 1:2, :] += d_scale
            gx_ref[0] = dx + dx1_ref[0]

    def rows(w):
        return pl.BlockSpec((1, ROW_BLOCK, w), lambda b, j: (b, j, 0))

    lat = pl.BlockSpec((1, ROW_BLOCK, D_MODEL), lambda b, j: (b, jnp.maximum(j - 1, 0), 0))
    return _host_call(
        body, name="in_bwd", grid=(bs, nb),
        in_specs=[lat, pl.BlockSpec((1, ROW_BLOCK, D_MODEL), lambda b, j: (b, 0, 0)), lat,
                  pl.BlockSpec((1, D_MODEL), lambda b, j: (0, 0)),
                  pl.BlockSpec((1, 8, D_MODEL), lambda b, j: (b, 0, 0)),
                  pl.BlockSpec((D_MODEL, IN_COLS), lambda b, j: (0, 0)),
                  rows(KW), rows(KW), rows(KW), rows(KW), rows(7 * KW)],
        out_specs=[lat, rows(IN_COLS), pl.BlockSpec((1, D_MODEL), lambda b, j: (0, 0)),
                   pl.BlockSpec((1, 8, D_MODEL), lambda b, j: (b, 0, 0))],
        out_shape=[SDS(x.shape, F32), SDS((bs, nb * ROW_BLOCK, IN_COLS), BF16), SDS((1, D_MODEL), F32),
                   SDS((bs, 8, D_MODEL), F32)],
        args=(x, ctx, dx1, g_mix, mod_a, w_in, df_f, df_b, di, dq, dpc), carry=carry)


def _lower_bound(lbg_ref, direction):
    return _sig(lbg_ref[0, direction:direction + 1, :] - lbg_ref[1, direction:direction + 1, :])


def _chunk_gates(fl, lb, tri_f32, upper):
    sg = _sig(fl)
    f = lb + (1.0 - lb) * sg
    g = jnp.log(f)
    k = 1.0 - f
    bcum = _dot_f32(tri_f32, g)
    bl = bcum[0:1] if upper else bcum[HGRN_CHUNK - 1:HGRN_CHUNK]
    return sg, f, k, bcum, bl


def _hgrn_block_order(direction, nb):
    if direction == 0:
        return lambda j: j
    return lambda j: jnp.where(j == 0, 0, nb - j)


def _hgrn_fwd(p, lbg, direction, carry=None):
    bs, rows, _ = p.shape
    nb = rows // ROW_BLOCK
    ncb = ROW_BLOCK // HGRN_CHUNK
    upper = direction == 1
    order = _hgrn_block_order(direction, nb)

    def body(f_ref, i_ref, q_ref, lbg_ref, o_ref, s_ref, st):
        @pl.when(pl.program_id(1) == 0)
        def _():
            st[...] = jnp.zeros_like(st)

        lb = _lower_bound(lbg_ref, direction)
        mask = _tri(HGRN_CHUNK, upper)
        tri_f32 = mask.astype(F32)
        for ci in (reversed(range(ncb)) if upper else range(ncb)):
            rs = slice(ci * HGRN_CHUNK, (ci + 1) * HGRN_CHUNK)
            _, _, k, bcum, bl = _chunk_gates(f_ref[0, rs, :], lb, tri_f32, upper)
            q = q_ref[0, rs, :]
            v = i_ref[0, rs, :]
            qi = q * jnp.exp(bcum - 0.5 * bl)
            ki = k * jnp.exp(0.5 * bl - bcum)
            kd = k * jnp.exp(bl - bcum)
            qe = q * jnp.exp(bcum)
            dec = jnp.exp(bl)
            outs = []
            for h in range(N_HEADS):
                ls = slice(h * HEAD_DIM, (h + 1) * HEAD_DIM)
                s_in = st[h]
                s_ref[0, 0, ci, h] = s_in
                a = jnp.where(mask, _dot_nt(qi[:, ls], ki[:, ls]), 0.0)
                outs.append(_dot(a, v[:, ls]) + _dot_nt(qe[:, ls], s_in))
                st[h] = s_in * dec[:, ls] + _dot_tn(v[:, ls], kd[:, ls])
            o_ref[0, rs, :] = jnp.concatenate(outs, axis=1)

    def col(cb):
        return pl.BlockSpec((1, ROW_BLOCK, KW), lambda b, j: (b, order(j), cb))

    return _host_call(
        body, name=f"hgrn_fwd{direction}", grid=(bs, nb),
        in_specs=[col(direction), col(2), col(3), pl.BlockSpec((2, 2, KW), lambda b, j: (0, 0, 0))],
        out_specs=[pl.BlockSpec((1, ROW_BLOCK, KW), lambda b, j: (b, order(j), 0)),
                   pl.BlockSpec((1, 1, ncb, N_HEADS, HEAD_DIM, HEAD_DIM), lambda b, j: (b, order(j), 0, 0, 0, 0))],
        out_shape=[SDS((bs, rows, KW), F32), SDS((bs, nb, ncb, N_HEADS, HEAD_DIM, HEAD_DIM), F32)],
        scratch_shapes=[pltpu.VMEM((N_HEADS, HEAD_DIM, HEAD_DIM), F32)],
        args=(p, p, p, lbg), carry=carry)


def _hgrn_bwd(p, lbg, s_saved, do_raw, direction, dq_prev=None, dv_prev=None, carry=None):
    bs, rows, _ = p.shape
    nb = rows // ROW_BLOCK
    ncb = ROW_BLOCK // HGRN_CHUNK
    upper = direction == 1
    fwd_order = _hgrn_block_order(direction, nb)
    order = lambda j: fwd_order(nb - 1 - j)
    last = dq_prev is not None
    out_dt = BF16 if last else F32

    def body(*refs):
        if last:
            f_ref, i_ref, q_ref, lbg_ref, s_ref, do_ref, dqp_ref, dvp_ref, df_ref, dq_ref, dv_ref, dlb_ref, dst = refs
        else:
            f_ref, i_ref, q_ref, lbg_ref, s_ref, do_ref, df_ref, dq_ref, dv_ref, dlb_ref, dst = refs
        b, j = pl.program_id(0), pl.program_id(1)

        @pl.when((b == 0) & (j == 0))
        def _():
            dlb_ref[...] = jnp.zeros_like(dlb_ref)

        @pl.when(j == 0)
        def _():
            dst[...] = jnp.zeros_like(dst)

        lb = _lower_bound(lbg_ref, direction)
        mask = _tri(HGRN_CHUNK, upper)
        mask_t = _tri(HGRN_CHUNK, not upper)
        tri_f32 = mask.astype(F32)
        tri_t_f32 = mask_t.astype(F32)
        dlb_acc = jnp.zeros((1, KW), F32)
        for ci in (range(ncb) if upper else reversed(range(ncb))):
            rs = slice(ci * HGRN_CHUNK, (ci + 1) * HGRN_CHUNK)
            sg, f, k, bcum, bl = _chunk_gates(f_ref[0, rs, :], lb, tri_f32, upper)
            q = q_ref[0, rs, :]
            v = i_ref[0, rs, :]
            do = do_ref[0, rs, :]
            eqm = jnp.exp(bcum - 0.5 * bl)
            ekm = jnp.exp(0.5 * bl - bcum)
            ekd = jnp.exp(bl - bcum)
            eb = jnp.exp(bcum)
            dec = jnp.exp(bl)
            qi, ki, kd, qe = q * eqm, k * ekm, k * ekd, q * eb
            dqi_l, dki_l, dkd_l, dqe_l, dv_l, ddec_l = [], [], [], [], [], []
            for h in range(N_HEADS):
                ls = slice(h * HEAD_DIM, (h + 1) * HEAD_DIM)
                s_in = s_ref[0, 0, ci, h]
                ds_out = dst[h]
                do_h, v_h = do[:, ls], v[:, ls]
                a_t = jnp.where(mask_t, _dot_nt(ki[:, ls], qi[:, ls]), 0.0)
                da = jnp.where(mask, _dot_nt(do_h, v_h), 0.0)
                da_t = jnp.where(mask_t, _dot_nt(v_h, do_h), 0.0)
                dqe_l.append(_dot(do_h, s_in))
                dkd_l.append(_dot(v_h, ds_out))
                dv_l.append(_dot(a_t, do_h) + _dot_nt(kd[:, ls], ds_out))
                dqi_l.append(_dot(da, ki[:, ls]))
                dki_l.append(_dot(da_t, qi[:, ls]))
                ddec_l.append(_colsum(ds_out * s_in))
                dst[h] = ds_out * dec[:, ls] + _dot_tn(do_h, qe[:, ls])
            dqi, dki, dkd, dqe = (jnp.concatenate(t, axis=1) for t in (dqi_l, dki_l, dkd_l, dqe_l))
            dv = jnp.concatenate(dv_l, axis=1)
            ddec = jnp.concatenate(ddec_l, axis=1)
            dq = dqi * eqm + dqe * eb
            dk = dki * ekm + dkd * ekd
            db = dqi * qi - dki * ki - dkd * kd + dqe * qe
            dbl = _colsum(dkd * kd) + ddec * dec
            dg = _dot_f32(tri_t_f32, db) + dbl
            df = dg / f - dk
            dlb_acc = dlb_acc + _colsum(df * (1.0 - sg))
            dfl = df * (1.0 - lb) * sg * (1.0 - sg)
            df_ref[0, rs, :] = dfl.astype(BF16)
            if last:
                dq_ref[0, rs, :] = (dq + dqp_ref[0, rs, :]).astype(out_dt)
                dv_ref[0, rs, :] = (dv + dvp_ref[0, rs, :]).astype(out_dt)
            else:
                dq_ref[0, rs, :] = dq
                dv_ref[0, rs, :] = dv
        dlb_ref[...] += dlb_acc

    def col(cb):
        return pl.BlockSpec((1, ROW_BLOCK, KW), lambda b, j: (b, order(j), cb))

    row = pl.BlockSpec((1, ROW_BLOCK, KW), lambda b, j: (b, order(j), 0))
    in_specs = [col(direction), col(2), col(3), pl.BlockSpec((2, 2, KW), lambda b, j: (0, 0, 0)),
                pl.BlockSpec((1, 1, ncb, N_HEADS, HEAD_DIM, HEAD_DIM), lambda b, j: (b, order(j), 0, 0, 0, 0)), row]
    args = [p, p, p, lbg, s_saved, do_raw]
    if last:
        in_specs += [row, row]
        args += [dq_prev, dv_prev]
    return _host_call(
        body, name=f"hgrn_bwd{direction}", grid=(bs, nb), in_specs=in_specs,
        out_specs=[row, row, row, pl.BlockSpec((1, KW), lambda b, j: (0, 0))],
        out_shape=[SDS((bs, rows, KW), BF16), SDS((bs, rows, KW), out_dt), SDS((bs, rows, KW), out_dt),
                   SDS((1, KW), F32)],
        scratch_shapes=[pltpu.VMEM((N_HEADS, HEAD_DIM, HEAD_DIM), F32)],
        args=args, carry=carry)


def _mix_values(og, u, v, ga, gb, o_raw, gna, lng, lnb, ws_ref, bst, wpa, wpb, wo):
    t = {}
    sog = _sig(og)
    t["sog"], t["silu_og"] = sog, og * sog
    xh_l, r_l = [], []
    for h in range(N_HEADS):
        xh, r = _rms(o_raw[:, h * HEAD_DIM:(h + 1) * HEAD_DIM])
        xh_l.append(xh)
        r_l.append(r)
    t["xh"], t["r"] = jnp.concatenate(xh_l, axis=1), r_l
    gna4 = jnp.concatenate([gna] * N_HEADS, axis=1)
    t["gna4"] = gna4
    t["o_n"] = t["xh"] * gna4
    t["o_a"] = t["o_n"] * t["silu_og"]
    t["gu"], t["tu"] = _gelu(u)
    gv, t["tv"] = _gelu(v)
    mu = jnp.mean(gv, axis=-1, keepdims=True)
    cen = gv - mu
    t["rstd"] = lax.rsqrt(jnp.mean(cen * cen, axis=-1, keepdims=True) + EPS)
    t["xhat"] = cen * t["rstd"]
    vn = t["xhat"] * lng + lnb
    t["vn"] = vn
    chunks = []
    for n in range(ROW_BLOCK // SGU_CHUNK):
        rs = slice(n * SGU_CHUNK, (n + 1) * SGU_CHUNK)
        groups = []
        for g in range(N_HEADS):
            ls = slice(g * HEAD_DIM, (g + 1) * HEAD_DIM)
            groups.append(_dot(ws_ref[g], vn[rs, ls]) + bst[:, g:g + 1])
        chunks.append(jnp.concatenate(groups, axis=1))
    t["mixed"] = jnp.concatenate(chunks, axis=0)
    t["o_bm"] = t["gu"] * t["mixed"]
    t["ya"] = _dot(t["o_a"], wpa)
    t["yb"] = _dot(t["o_bm"], wpb)
    t["sa"], t["sb"] = _sig(ga), _sig(gb)
    t["merged"] = t["sa"] * t["ya"] + t["sb"] * t["yb"]
    t["mix"] = _dot(t["merged"], wo)
    return t


def _mix_in_specs(row_of):
    def col(cb):
        return pl.BlockSpec((1, ROW_BLOCK, KW), lambda b, j: (b, row_of(j), cb))
    return [col(cb) for cb in range(4, 11)]


def _mix_param_specs():
    full2 = lambda r, c: pl.BlockSpec((r, c), lambda b, j: (0, 0))
    return [full2(1, HEAD_DIM), full2(1, KW), full2(1, KW),
            pl.BlockSpec((N_HEADS, SGU_CHUNK, SGU_CHUNK), lambda b, j: (0, 0, 0)),
            full2(SGU_CHUNK, N_HEADS), full2(KW, D_MODEL), full2(KW, D_MODEL), full2(D_MODEL, D_MODEL)]


def _mix_fwd(p, o_f, o_b, x, mod_c, gna, lng, lnb, w_s, bst, wpa, wpb, wo):
    bs, seq, _ = x.shape
    nbl = seq // ROW_BLOCK

    def body(og_r, u_r, v_r, ga0_r, ga1_r, gb0_r, gb1_r, of_r, ob_r, x_r, mod_r,
             gna_r, lng_r, lnb_r, ws_r, bst_r, wpa_r, wpb_r, wo_r, x1_r):
        ga = jnp.concatenate([ga0_r[0], ga1_r[0]], axis=1)
        gb = jnp.concatenate([gb0_r[0], gb1_r[0]], axis=1)
        t = _mix_values(og_r[0], u_r[0], v_r[0], ga, gb, of_r[0] + ob_r[0], gna_r[...], lng_r[...], lnb_r[...],
                        ws_r, bst_r[...], wpa_r[...], wpb_r[...], wo_r[...])
        x1_r[0] = x_r[0] + mod_r[0, 0:1, :] * t["mix"]

    row = lambda w: pl.BlockSpec((1, ROW_BLOCK, w), lambda b, j: (b, j + 1, 0))
    lat = pl.BlockSpec((1, ROW_BLOCK, D_MODEL), lambda b, j: (b, j, 0))
    return pl.pallas_call(
        body, name="mix_fwd", grid=(bs, nbl),
        in_specs=_mix_in_specs(lambda j: j + 1) + [row(KW), row(KW), lat,
                                                    pl.BlockSpec((1, 8, D_MODEL), lambda b, j: (b, 0, 0))]
        + _mix_param_specs(),
        out_specs=lat, out_shape=SDS(x.shape, F32), compiler_params=_params(2),
    )(p, p, p, p, p, p, p, o_f, o_b, x, mod_c, gna, lng, lnb, w_s, bst, wpa, wpb, wo)


def _mix_bwd(p, o_f, o_b, dx1, mod_c, gna, lng, lnb, w_s, w_s_t, bst, wpa, wpb, wo, carry=None):
    bs, rows, _ = p.shape
    nb = rows // ROW_BLOCK

    def body(og_r, u_r, v_r, ga0_r, ga1_r, gb0_r, gb1_r, of_r, ob_r, dx1_r, mod_r,
             gna_r, lng_r, lnb_r, ws_r, bst_r, wpa_r, wpb_r, wo_r, wst_r,
             dor_r, dpc_r, dwpa_r, dwpb_r, dwo_r, dgna_r, dlng_r, dlnb_r, dws_r, dbst_r, dmod_r):
        b, j = pl.program_id(0), pl.program_id(1)

        @pl.when((b == 0) & (j == 0))
        def _():
            for r in (dwpa_r, dwpb_r, dwo_r, dgna_r, dlng_r, dlnb_r, dws_r, dbst_r):
                r[...] = jnp.zeros_like(r)

        @pl.when(j == 0)
        def _():
            dmod_r[...] = jnp.zeros_like(dmod_r)
            dor_r[...] = jnp.zeros_like(dor_r)
            dpc_r[...] = jnp.zeros_like(dpc_r)

        @pl.when(j > 0)
        def _():
            og, u, v = og_r[0], u_r[0], v_r[0]
            ga = jnp.concatenate([ga0_r[0], ga1_r[0]], axis=1)
            gb = jnp.concatenate([gb0_r[0], gb1_r[0]], axis=1)
            gna, lng = gna_r[...], lng_r[...]
            wpa, wpb, wo = wpa_r[...], wpb_r[...], wo_r[...]
            t = _mix_values(og, u, v, ga, gb, of_r[0] + ob_r[0], gna, lng, lnb_r[...],
                            ws_r, bst_r[...], wpa, wpb, wo)
            dx1 = dx1_r[0]
            dmod_r[0, 0:1, :] += _colsum(dx1 * t["mix"])
            dmix = mod_r[0, 0:1, :] * dx1
            dmerged = _dot_nt(dmix, wo)
            dwo_r[...] += _dot_tn(t["merged"], dmix)
            sa, sb = t["sa"], t["sb"]
            dya, dyb = sa * dmerged, sb * dmerged
            dga = dmerged * t["ya"] * sa * (1.0 - sa)
            dgb = dmerged * t["yb"] * sb * (1.0 - sb)
            do_a = _dot_nt(dya, wpa)
            dwpa_r[...] += _dot_tn(t["o_a"], dya)
            do_bm = _dot_nt(dyb, wpb)
            dwpb_r[...] += _dot_tn(t["o_bm"], dyb)
            sog = t["sog"]
            dog = do_a * t["o_n"] * (sog * (1.0 + og * (1.0 - sog)))
            do_n = do_a * t["silu_og"]
            dxh = do_n * t["gna4"]
            prod = do_n * t["xh"]
            dgna = jnp.zeros((1, HEAD_DIM), F32)
            dor_l = []
            for h in range(N_HEADS):
                ls = slice(h * HEAD_DIM, (h + 1) * HEAD_DIM)
                dgna = dgna + _colsum(prod[:, ls])
                dor_l.append(_rms_bwd(dxh[:, ls], t["xh"][:, ls], t["r"][h]))
            dgna_r[...] += dgna
            dor_r[0] = jnp.concatenate(dor_l, axis=1)
            du = do_bm * t["mixed"] * _dgelu(u, t["tu"])
            dmixed = do_bm * t["gu"]
            vn = t["vn"]
            dvn_chunks = []
            for n in range(ROW_BLOCK // SGU_CHUNK):
                rs = slice(n * SGU_CHUNK, (n + 1) * SGU_CHUNK)
                groups = []
                for g in range(N_HEADS):
                    ls = slice(g * HEAD_DIM, (g + 1) * HEAD_DIM)
                    dm = dmixed[rs, ls]
                    dws_r[g] += _dot_nt(dm, vn[rs, ls])
                    dbst_r[:, g:g + 1] += jnp.sum(dm, axis=1, keepdims=True)
                    groups.append(_dot(wst_r[g], dm))
                dvn_chunks.append(jnp.concatenate(groups, axis=1))
            dvn = jnp.concatenate(dvn_chunks, axis=0)
            xhat = t["xhat"]
            dlng_r[...] += _colsum(dvn * xhat)
            dlnb_r[...] += _colsum(dvn)
            dxhat = dvn * lng
            dgv = t["rstd"] * (dxhat - jnp.mean(dxhat, axis=-1, keepdims=True)
                               - xhat * jnp.mean(dxhat * xhat, axis=-1, keepdims=True))
            dv = dgv * _dgelu(v, t["tv"])
            dpc_r[0] = jnp.concatenate([dog, du, dv, dga, dgb], axis=1).astype(BF16)

    row = lambda w: pl.BlockSpec((1, ROW_BLOCK, w), lambda b, j: (b, j, 0))
    lat = pl.BlockSpec((1, ROW_BLOCK, D_MODEL), lambda b, j: (b, jnp.maximum(j - 1, 0), 0))
    full2 = lambda r, c: pl.BlockSpec((r, c), lambda b, j: (0, 0))
    ws_spec = pl.BlockSpec((N_HEADS, SGU_CHUNK, SGU_CHUNK), lambda b, j: (0, 0, 0))
    return _host_call(
        body, name="mix_bwd", grid=(bs, nb),
        in_specs=_mix_in_specs(lambda j: j) + [row(KW), row(KW), lat,
                                                pl.BlockSpec((1, 8, D_MODEL), lambda b, j: (b, 0, 0))]
        + _mix_param_specs() + [ws_spec],
        out_specs=[row(KW), row(7 * KW), full2(KW, D_MODEL), full2(KW, D_MODEL), full2(D_MODEL, D_MODEL),
                   full2(1, HEAD_DIM), full2(1, KW), full2(1, KW), ws_spec, full2(SGU_CHUNK, N_HEADS),
                   pl.BlockSpec((1, 8, D_MODEL), lambda b, j: (b, 0, 0))],
        out_shape=[SDS((bs, rows, KW), F32), SDS((bs, rows, 7 * KW), BF16), SDS((KW, D_MODEL), F32),
                   SDS((KW, D_MODEL), F32), SDS((D_MODEL, D_MODEL), F32), SDS((1, HEAD_DIM), F32),
                   SDS((1, KW), F32), SDS((1, KW), F32), SDS((N_HEADS, SGU_CHUNK, SGU_CHUNK), F32),
                   SDS((SGU_CHUNK, N_HEADS), F32), SDS((bs, 8, D_MODEL), F32)],
        args=(p, p, p, p, p, p, p, o_f, o_b, dx1, mod_c, gna, lng, lnb, w_s, bst, wpa, wpb, wo, w_s_t), carry=carry)


def _ffn(x1, target, mod_c, g_ffn, g_final, w_up, w_down):
    bs, seq, _ = x1.shape
    nbl = seq // ROW_BLOCK

    def body(x1_r, tg_r, mod_r, gf_r, gl_r, wu_r, wd_r,
             dx1_r, h2_r, dab_r, hid_r, dffn_r, loss_r, dgl_r, dgf_r, dmod_r):
        b, j = pl.program_id(0), pl.program_id(1)

        @pl.when((b == 0) & (j == 0))
        def _():
            for r in (loss_r, dgl_r, dgf_r):
                r[...] = jnp.zeros_like(r)

        @pl.when(j == 0)
        def _():
            dmod_r[...] = jnp.zeros_like(dmod_r)

        x1 = x1_r[0]
        shift, scale, gate = mod_r[0, 1:2, :], mod_r[0, 2:3, :], mod_r[0, 3:4, :]
        gf, gl = gf_r[...], gl_r[...]
        xn2, r2 = _rms(x1)
        hn2 = xn2 * gf
        h2 = (hn2 * (1.0 + scale) + shift).astype(BF16)
        h2_r[0] = h2
        ab = jnp.dot(h2, wu_r[...], preferred_element_type=F32)
        a, bb = ab[:, :D_FF], ab[:, D_FF:]
        sa = _sig(a)
        silu_a = a * sa
        hid = (silu_a * bb).astype(BF16)
        hid_r[0] = hid
        ffn = jnp.dot(hid, wd_r[...], preferred_element_type=F32)
        x2 = x1 + gate * ffn
        xn3, r3 = _rms(x2)
        err = xn3 * gl - tg_r[0]
        loss_r[...] += 0.5 * jnp.sum(jnp.mean(err * err, axis=-1, keepdims=True), axis=0, keepdims=True)
        dy = err * (1.0 / D_MODEL)
        dgl_r[...] += _colsum(dy * xn3)
        dx2 = _rms_bwd(dy * gl, xn3, r3)
        dmod_r[0, 3:4, :] += _colsum(dx2 * ffn)
        dffn = (gate * dx2).astype(BF16)
        dffn_r[0] = dffn
        dhid = lax.dot_general(dffn, wd_r[...], (((1,), (1,)), ((), ())), preferred_element_type=F32)
        da = dhid * bb * (sa * (1.0 + a * (1.0 - sa)))
        db = dhid * silu_a
        dab = jnp.concatenate([da, db], axis=1).astype(BF16)
        dab_r[0] = dab
        dh2 = lax.dot_general(dab, wu_r[...], (((1,), (1,)), ((), ())), preferred_element_type=F32)
        dmod_r[0, 1:2, :] += _colsum(dh2)
        dmod_r[0, 2:3, :] += _colsum(dh2 * hn2)
        dhn2 = dh2 * (1.0 + scale)
        dgf_r[...] += _colsum(dhn2 * xn2)
        dx1_r[0] = dx2 + _rms_bwd(dhn2 * gf, xn2, r2)

    lat = lambda w: pl.BlockSpec((1, ROW_BLOCK, w), lambda b, j: (b, j, 0))
    full2 = lambda r, c: pl.BlockSpec((r, c), lambda b, j: (0, 0))
    mod_spec = pl.BlockSpec((1, 8, D_MODEL), lambda b, j: (b, 0, 0))
    return pl.pallas_call(
        body, name="ffn", grid=(bs, nbl),
        in_specs=[lat(D_MODEL), lat(D_MODEL), mod_spec, full2(1, D_MODEL), full2(1, D_MODEL),
                  full2(D_MODEL, 2 * D_FF), full2(D_FF, D_MODEL)],
        out_specs=[lat(D_MODEL), lat(D_MODEL), lat(2 * D_FF), lat(D_FF), lat(D_MODEL),
                   full2(1, 1), full2(1, D_MODEL), full2(1, D_MODEL), mod_spec],
        out_shape=[SDS(x1.shape, F32), SDS(x1.shape, BF16), SDS((bs, seq, 2 * D_FF), BF16),
                   SDS((bs, seq, D_FF), BF16), SDS(x1.shape, BF16), SDS((1, 1), F32),
                   SDS((1, D_MODEL), F32), SDS((1, D_MODEL), F32), SDS((bs, 8, D_MODEL), F32)],
        compiler_params=_params(2),
    )(x1, target, mod_c, g_ffn, g_final, w_up, w_down)


def _matmul_tn(a, b, n_blocks, tk, name):
    t, m = a.shape
    n = b.shape[1]
    tn = n // n_blocks

    def body(a_ref, b_ref, o_ref):
        @pl.when(pl.program_id(1) == 0)
        def _():
            o_ref[...] = jnp.zeros_like(o_ref)
        o_ref[0] += _dot_tn(a_ref[...], b_ref[...])

    return pl.pallas_call(
        body, name=name, grid=(n_blocks, t // tk),
        in_specs=[pl.BlockSpec((tk, m), lambda i, k: (k, 0)), pl.BlockSpec((tk, tn), lambda i, k: (k, i))],
        out_specs=pl.BlockSpec((1, m, tn), lambda i, k: (i, 0, 0)),
        out_shape=SDS((n_blocks, m, tn), F32), compiler_params=_params(2),
    )(a, b)


SMALL_ROWS = 80
ROW_CCTX = 3


def _small_reduce(gathered, lbg):
    def body(g_ref, lbg_ref, s_ref, dgam_ref):
        tot = g_ref[0:SMALL_ROWS, :]
        for dev in range(1, N_DEV):
            tot = tot + g_ref[dev * SMALL_ROWS:(dev + 1) * SMALL_ROWS, :]
        s_ref[...] = tot
        cc = g_ref[ROW_CCTX:ROW_CCTX + 1, :]
        for dev in range(2, N_DEV, 2):
            cc = cc + g_ref[dev * SMALL_ROWS + ROW_CCTX:dev * SMALL_ROWS + ROW_CCTX + 1, :]
        s_ref[ROW_CCTX:ROW_CCTX + 1, :] = cc
        dlb = tot[7:8, :]
        for d in range(2):
            s0 = _sig(lbg_ref[0, d:d + 1, :] - lbg_ref[1, d:d + 1, :])
            dgam_ref[d:d + 1, :] = dlb[:, d * KW:(d + 1) * KW] * s0 * (1.0 - s0)

    return pl.pallas_call(
        body, name="small_reduce", out_shape=[SDS((SMALL_ROWS, D_MODEL), F32), SDS((2, KW), F32)],
        in_specs=[VMEM_SPEC] * 2, out_specs=[VMEM_SPEC] * 2,
    )(gathered, lbg)


def _pad_cols(a, width):
    return jnp.pad(a, ((0, 0), (0, width - a.shape[1])))


def kernel(x, c, ctx, c_ctx, w_mod, b_mod, g_mix, g_ffn, w_in, lb_gamma, g_norm_a, ln_v_g, ln_v_b, w_s, b_s, w_pa, w_pb, w_o, w_up, w_down, g_final, loss_target, m_c_ctx, m_w_mod, m_b_mod, m_g_mix, m_g_ffn, m_w_in, m_lb_gamma, m_g_norm_a, m_ln_v_g, m_ln_v_b, m_w_s, m_b_s, m_w_pa, m_w_pb, m_w_o, m_w_up, m_w_down, m_g_final, v_c_ctx, v_w_mod, v_b_mod, v_g_mix, v_g_ffn, v_w_in, v_lb_gamma, v_g_norm_a, v_ln_v_g, v_ln_v_b, v_w_s, v_b_s, v_w_pa, v_w_pb, v_w_o, v_w_up, v_w_down, v_g_final):
    ax, ay, ac = lax.axis_index("x"), lax.axis_index("y"), lax.axis_index("c")
    kc = 2 * ax + ay
    dev = 2 * kc + ac
    pos = jnp.stack([kc, ac]).astype(jnp.int32)
    bs, seq, _ = x.shape
    assert bs <= 4 and ctx.shape[1] == ROW_BLOCK and seq % ROW_BLOCK == 0
    mod_cols = w_mod.shape[2]

    lbg_row = _pad_cols(lb_gamma.reshape(1, -1), D_MODEL)
    pay1 = jnp.concatenate([c, jnp.zeros((4 - bs, D_MODEL), F32), c_ctx[None, :], lbg_row,
                            jnp.zeros((2, D_MODEL), F32)], axis=0)
    cond64 = _all_gather8(pay1, "gather_cond")
    lbg_full = cond64.reshape(N_DEV, 8, D_MODEL)[0::2, 5, :KW].reshape(N_CHIPS, 2, 2, HEAD_DIM)
    lbg_full = jnp.transpose(lbg_full, (1, 2, 0, 3)).reshape(2, 2, KW)

    b_mod_s = lax.dynamic_slice(b_mod, (0, kc * mod_cols), (1, mod_cols))
    mod_s = _mod_fwd(cond64, w_mod[0], b_mod_s)
    mod_g = _all_gather8(mod_s, "gather_mod").reshape(N_DEV, 64, mod_cols)[0::2]
    mod_full = jnp.transpose(mod_g, (1, 0, 2)).reshape(64, N_CHIPS * mod_cols)
    mod_mine = lax.dynamic_slice(mod_full, (dev * 8, 0), (8, 6 * D_MODEL)).reshape(8, 6, D_MODEL)
    mod, mc = mod_mine[:bs], mod_mine[4]
    zeros4 = jnp.zeros((bs, 4, D_MODEL), F32)
    mod_a = jnp.concatenate([mod[:, 0:2], jnp.broadcast_to(mc[None, 0:2], (bs, 2, D_MODEL)), zeros4], axis=1)
    mod_c = jnp.concatenate([mod[:, 2:6], zeros4], axis=1)

    shards = [w_in[0], w_up[0], w_pa[0], w_pb[0], w_o[0], w_down[0]]
    bufs = _cast_bf16(pos, shards)
    (w_in_g,) = _gather_weights(bufs[:1])

    def cols_major(a):
        return jnp.transpose(a, (1, 0, 2)).reshape(a.shape[1], -1)

    gna, lng, lnb = g_norm_a, ln_v_g, ln_v_b
    ws3 = w_s[0]
    ws3_t = jnp.transpose(ws3, (0, 2, 1))
    bst = jnp.transpose(b_s[0])

    w_in_f = cols_major(w_in_g)
    (p, h_all), sent = _in_fwd(x, ctx, g_mix, mod_a, w_in_f, carry=_carry_gather_send(bufs[1:]))
    (o_f, s_f), gathered = _hgrn_fwd(p, lbg_full, 0, carry=_carry_gather_forward(sent))
    (o_b, s_b), _ = _hgrn_fwd(p, lbg_full, 1)
    w_up_f, w_pa_f, w_pb_f = (cols_major(a) for a in gathered[:3])
    w_o_f = gathered[3].reshape(-1, D_MODEL)
    w_down_f = gathered[4].reshape(-1, D_MODEL)
    x1 = _mix_fwd(p, o_f, o_b, x, mod_c, gna, lng, lnb, ws3, bst, w_pa_f, w_pb_f, w_o_f)
    dx1, h2, dab, hid, dffn, loss_part, dg_final, dg_ffn, dmod_ffn = _ffn(
        x1, loss_target, mod_c, g_ffn, g_final[None, :], w_up_f, w_down_f)
    rows_lat = bs * seq
    tk_lat = 2 * ROW_BLOCK if rows_lat % (2 * ROW_BLOCK) == 0 else ROW_BLOCK
    dw_up = _matmul_tn(h2.reshape(rows_lat, D_MODEL), dab.reshape(rows_lat, 2 * D_FF), N_CHIPS, tk_lat, "dw_up")
    dw_down = _matmul_tn(hid.reshape(rows_lat, D_FF), dffn.reshape(rows_lat, D_MODEL), 1, tk_lat, "dw_down")

    def shard_major(a):
        return jnp.transpose(a.reshape(a.shape[0], N_CHIPS, -1), (1, 0, 2))

    def add_halves(parts, recvs, names):
        sums = [_rs_add_halves(pos, g, r, "rs_add_" + nm) for g, r, nm in zip(parts, recvs, names)]
        return [s[0] for s in sums], [s[1] for s in sums]

    def sum_owner(cp32s, recvs, names):
        return [_rs_sum_owner(pos, a, r, "rs_sum_" + nm) for a, r, nm in zip(cp32s, recvs, names)]

    ffn_names, mix_names = ["w_up", "w_down"], ["w_pa", "w_pb", "w_o"]
    part_ffn = [dw_up, dw_down.reshape(N_CHIPS, -1, D_MODEL)]
    (do_raw, dpc, dw_pa, dw_pb, dw_o, dgna, dlng, dlnb, dws, dbst, dmod_mix), sib_ffn = _mix_bwd(
        p, o_f, o_b, dx1, mod_c, gna, lng, lnb, ws3, ws3_t, bst, w_pa_f, w_pb_f, w_o_f,
        carry=_carry_sibling_halves(part_ffn))
    cp32_ffn, cpbf_ffn = add_halves(part_ffn, sib_ffn, ffn_names)
    part_mix = [shard_major(dw_pa), shard_major(dw_pb), dw_o.reshape(N_CHIPS, -1, D_MODEL)]
    (df_f, dq0, dv0, dlb0), got = _hgrn_bwd(
        p, lbg_full, s_f, do_raw, 0,
        carry=_merge_carries(_carry_to_owner(cpbf_ffn), _carry_sibling_halves(part_mix)))
    own_ffn, sib_mix = got[:2], got[2:]
    half_ffn = sum_owner(cp32_ffn, own_ffn, ffn_names)
    cp32_mix, cpbf_mix = add_halves(part_mix, sib_mix, mix_names)
    (df_b, dq, di, dlb1), got = _hgrn_bwd(
        p, lbg_full, s_b, do_raw, 1, dq0, dv0,
        carry=_merge_carries(_carry_join_halves(half_ffn), _carry_to_owner(cpbf_mix)))
    g_ffn_w, own_mix = got[:2], got[2:]
    half_mix = sum_owner(cp32_mix, own_mix, mix_names)
    (grad_x, dp, dg_mix, dmod_in), g_mix_w = _in_bwd(
        x, ctx, dx1, g_mix, mod_a, w_in_f, df_f, df_b, di, dq, dpc, carry=_carry_join_halves(half_mix))

    rows_all = dp.shape[0] * dp.shape[1]
    tk_all = 2 * ROW_BLOCK if rows_all % (2 * ROW_BLOCK) == 0 else ROW_BLOCK
    dw_in = _matmul_tn(h_all.reshape(rows_all, D_MODEL), dp.reshape(rows_all, IN_COLS), N_CHIPS, tk_all, "dw_in")
    sib_in = _comm_call("rs_sibling_w_in", _carry_sibling_halves([dw_in]))
    cp32_in, cpbf_in = add_halves([dw_in], sib_in, ["w_in"])
    own_in = _comm_call("rs_owner_w_in", _carry_to_owner(cpbf_in))
    g_in_w = _comm_call("rs_join_w_in", _carry_join_halves(sum_owner(cp32_in, own_in, ["w_in"])))

    big_names = ["w_in", "w_up", "w_pa", "w_pb", "w_o", "w_down"]
    g_big = [g_in_w[0], g_ffn_w[0], g_mix_w[0], g_mix_w[1], g_mix_w[2], g_ffn_w[1]]
    big_m = [m_w_in, m_w_up, m_w_pa, m_w_pb, m_w_o, m_w_down]
    big_v = [v_w_in, v_w_up, v_w_pa, v_w_pb, v_w_o, v_w_down]
    res = {}
    for name, w, g, m, v in zip(big_names, shards, g_big, big_m, big_v):
        d, m2, v2 = _adamw_big(w, g, m[0], v[0], "adamw_" + name)
        res[name] = (g[None], d[None], m2[None], v2[None])

    dmod_mine = jnp.concatenate([dmod_in[:, 0], dmod_in[:, 1], dmod_mix[:, 0], dmod_ffn[:, 1], dmod_ffn[:, 2],
                                 dmod_ffn[:, 3]], axis=1)
    dmc = jnp.concatenate([jnp.sum(dmod_in[:, 2], axis=0), jnp.sum(dmod_in[:, 3], axis=0),
                           jnp.zeros((4 * D_MODEL,), F32)])[None, :]
    pay3 = jnp.concatenate([dmod_mine, jnp.zeros((4 - bs, 6 * D_MODEL), F32), dmc,
                            jnp.zeros((3, 6 * D_MODEL), F32)], axis=0)
    dmod64 = _all_gather8(pay3, "gather_dmod")
    dmod64_my = lax.dynamic_slice(dmod64, (0, kc * mod_cols), (64, mod_cols))
    g_w_mod, g_b_mod, g_cctx_part = _mod_bwd(cond64, dmod64, dmod64_my, w_mod[0], c_ctx[None, :])
    d, m2, v2 = _adamw_big(w_mod[0], g_w_mod, m_w_mod[0], v_w_mod[0], "adamw_w_mod")
    res["w_mod"] = (g_w_mod[None], d[None], m2[None], v2[None])

    def row(*parts):
        return _pad_cols(jnp.concatenate([q.reshape(1, -1) for q in parts], axis=1), D_MODEL)

    small_rows = [dg_mix, dg_ffn, dg_final, g_cctx_part, row(dgna), row(dlng, dlnb), row(jnp.transpose(dbst)),
                  row(dlb0, dlb1), row(loss_part), jnp.zeros((7, D_MODEL), F32), dws.reshape(64, D_MODEL)]
    pay4 = jnp.concatenate(small_rows, axis=0)
    tot, dgam0 = _small_reduce(_all_gather8(pay4, "gather_small"), lbg_full)
    loss = tot[8, 0]
    dgam_full = jnp.stack([dgam0, -dgam0])
    g_lbg = lax.dynamic_slice(dgam_full, (0, 0, kc * HEAD_DIM), (2, 2, HEAD_DIM))

    small = [
        ("c_ctx", c_ctx[None, :], tot[3:4], m_c_ctx, v_c_ctx),
        ("b_mod", b_mod, g_b_mod, m_b_mod, v_b_mod),
        ("g_mix", g_mix, tot[0:1], m_g_mix, v_g_mix),
        ("g_ffn", g_ffn, tot[1:2], m_g_ffn, v_g_ffn),
        ("lb_gamma", lb_gamma.reshape(4, HEAD_DIM), g_lbg.reshape(4, HEAD_DIM), m_lb_gamma, v_lb_gamma),
        ("g_norm_a", g_norm_a, tot[4:5, :HEAD_DIM], m_g_norm_a, v_g_norm_a),
        ("ln_v_g", ln_v_g, tot[5:6, :KW], m_ln_v_g, v_ln_v_g),
        ("ln_v_b", ln_v_b, tot[5:6, KW:], m_ln_v_b, v_ln_v_b),
        ("w_s", w_s.reshape(N_HEADS * SGU_CHUNK, SGU_CHUNK), tot[16:80].reshape(N_HEADS * SGU_CHUNK, SGU_CHUNK),
         m_w_s, v_w_s),
        ("b_s", b_s[0], tot[6:7, :KW].reshape(N_HEADS, SGU_CHUNK), m_b_s, v_b_s),
        ("g_final", g_final[None, :], tot[2:3], m_g_final, v_g_final),
    ]
    ws_, gs_ = [s[1] for s in small], [s[2] for s in small]
    ms_ = [s[3].reshape(s[1].shape) for s in small]
    vs_ = [s[4].reshape(s[1].shape) for s in small]
    ds_, m2s_, v2s_ = _adamw_small(ws_, gs_, ms_, vs_)
    for (name, _, g, m, _), d, m2, v2 in zip(small, ds_, m2s_, v2s_):
        res[name] = tuple(t.reshape(m.shape) for t in (g, d, m2, v2))

    order = ["c_ctx", "w_mod", "b_mod", "g_mix", "g_ffn", "w_in", "lb_gamma", "g_norm_a", "ln_v_g", "ln_v_b",
             "w_s", "b_s", "w_pa", "w_pb", "w_o", "w_up", "w_down", "g_final"]
    outs = [loss, grad_x]
    for part in range(4):
        outs += [res[n][part] for n in order]
    return tuple(outs)
```

```python
import functools
import math

import jax
import jax.numpy as jnp
import numpy as np
from jax import lax
from jax.experimental import pallas as pl
from jax.experimental.pallas import tpu as pltpu

F32 = jnp.float32
BF16 = jnp.bfloat16
SDS = jax.ShapeDtypeStruct
MESH = pl.DeviceIdType.MESH

EPS = 1e-6
D_MODEL = 1024
N_HEADS = 4
HEAD_DIM = 128
KW = N_HEADS * HEAD_DIM
IN_COLS = 11 * KW
D_FF = 2816
HGRN_CHUNK = 64
SGU_CHUNK = 128
ROW_BLOCK = 256
N_CHIPS = 4
N_DEV = 8
V7X_VMEM_BYTES = 64 * 1024 * 1024
VMEM_LIMIT = V7X_VMEM_BYTES - 6 * 1024 * 1024

ADAM_LR, ADAM_B1, ADAM_B2, ADAM_EPS, ADAM_WD, ADAM_STEP = 0.001, 0.9, 0.999, 1e-08, 0.01, 10
GELU_C0 = math.sqrt(2.0 / math.pi)
GELU_C1 = 0.044715

VMEM_SPEC = pl.BlockSpec(memory_space=pltpu.VMEM)
ANY_SPEC = pl.BlockSpec(memory_space=pl.ANY)


def _params(n_grid):
    return pltpu.CompilerParams(dimension_semantics=("arbitrary",) * n_grid, vmem_limit_bytes=VMEM_LIMIT)


def _sig(x):
    return 0.5 * jnp.tanh(0.5 * x) + 0.5


def _gelu(x):
    t = jnp.tanh(GELU_C0 * (x + GELU_C1 * x * x * x))
    return 0.5 * x * (1.0 + t), t


def _dgelu(x, t):
    return 0.5 * (1.0 + t) + 0.5 * x * (1.0 - t * t) * GELU_C0 * (1.0 + 3.0 * GELU_C1 * x * x)


def _dot(a, b):
    return jnp.dot(a.astype(BF16), b.astype(BF16), preferred_element_type=F32)


def _dot_nt(a, b):
    return lax.dot_general(a.astype(BF16), b.astype(BF16), (((1,), (1,)), ((), ())), preferred_element_type=F32)


def _dot_tn(a, b):
    return lax.dot_general(a.astype(BF16), b.astype(BF16), (((0,), (0,)), ((), ())), preferred_element_type=F32)


def _dot_f32(a, b, dims=(((1,), (0,)), ((), ()))):
    return lax.dot_general(a, b, dims, precision=lax.Precision.HIGHEST, preferred_element_type=F32)


def _rms(x):
    r = lax.rsqrt(jnp.mean(x * x, axis=-1, keepdims=True) + EPS)
    return x * r, r


def _rms_bwd(dxn, xn, r):
    return r * (dxn - xn * jnp.mean(dxn * xn, axis=-1, keepdims=True))


def _colsum(a):
    return jnp.sum(a, axis=0, keepdims=True)


def _tri(n, upper):
    t = lax.broadcasted_iota(jnp.int32, (n, n), 0)
    s = lax.broadcasted_iota(jnp.int32, (n, n), 1)
    return (s >= t) if upper else (s <= t)


def _all_gather8(x_shard, name):
    m_per, n = x_shard.shape

    def body(x_ref, out_ref, send_sems, recv_sems, local_sem):
        x, y, c = lax.axis_index("x"), lax.axis_index("y"), lax.axis_index("c")
        me, sibling = (x, y, c), (x, y, 1 - c)
        chips = [(1 - x, y), (x, 1 - y), (1 - x, 1 - y)]

        def rows(px, py, pc):
            return out_ref.at[pl.ds((4 * px + 2 * py + pc) * m_per, m_per), :]

        def copy(k, block, to, src=None):
            return pltpu.make_async_remote_copy(
                src_ref=rows(*block) if src is None else src, dst_ref=rows(*block),
                send_sem=send_sems.at[k], recv_sem=recv_sems.at[k], device_id=to, device_id_type=MESH)

        mine = pltpu.make_async_copy(x_ref, rows(*me), local_sem)
        mine.start()
        first = [copy(0, me, sibling, src=x_ref)]
        first += [copy(1 + j, me, (*chip, c), src=x_ref) for j, chip in enumerate(chips)]
        for cp in first:
            cp.start()
        passed = [copy(4 + j, (*chip, c), sibling) for j, chip in enumerate(chips)]
        for j, chip in enumerate(chips):
            copy(1 + j, (*chip, c), me).wait_recv()
            passed[j].start()
        copy(0, sibling, me).wait_recv()
        for j, chip in enumerate(chips):
            copy(4 + j, (*chip, 1 - c), me).wait_recv()
        for cp in first + passed:
            cp.wait_send()
        mine.wait()

    return pl.pallas_call(
        body, name=name, out_shape=SDS((N_DEV * m_per, n), x_shard.dtype),
        in_specs=[VMEM_SPEC], out_specs=VMEM_SPEC,
        scratch_shapes=[pltpu.SemaphoreType.DMA((7,)), pltpu.SemaphoreType.DMA((7,)), pltpu.SemaphoreType.DMA],
    )(x_shard)


def _mesh_pos():
    x, y, c = lax.axis_index("x"), lax.axis_index("y"), lax.axis_index("c")
    chips = [(1 - x, y), (x, 1 - y), (1 - x, 1 - y)]
    return x, y, c, 2 * x + y, (x, y, 1 - c), chips


def _half_rows(c, rh):
    return pl.ds(pl.multiple_of(c * rh, 16), rh)


def _gather_weights(bufs):
    n = len(bufs)

    def body(*refs):
        outs = refs[n:2 * n]
        send_sems, recv_sems = refs[2 * n:]
        x, y, c, kc, sibling, chips = _mesh_pos()
        firsts, passed = [], []
        for wi in range(n):
            rh = outs[wi].shape[1] // 2
            for jj, chip in enumerate(chips):
                mine = outs[wi].at[kc, _half_rows(c, rh), :]
                cp = pltpu.make_async_remote_copy(
                    src_ref=mine, dst_ref=mine, send_sem=send_sems.at[wi, jj], recv_sem=recv_sems.at[wi, jj],
                    device_id=(*chip, c), device_id_type=MESH)
                cp.start()
                firsts.append(cp)
        for wi in range(n):
            rh = outs[wi].shape[1] // 2
            for jj, chip in enumerate(chips):
                blk = outs[wi].at[2 * chip[0] + chip[1], _half_rows(c, rh), :]
                pltpu.make_async_remote_copy(
                    src_ref=blk, dst_ref=blk, send_sem=send_sems.at[wi, jj], recv_sem=recv_sems.at[wi, jj],
                    device_id=(*chip, c), device_id_type=MESH).wait_recv()
                fw = pltpu.make_async_remote_copy(
                    src_ref=blk, dst_ref=blk, send_sem=send_sems.at[wi, 3 + jj], recv_sem=recv_sems.at[wi, 3 + jj],
                    device_id=sibling, device_id_type=MESH)
                fw.start()
                passed.append(fw)
        for wi in range(n):
            rh = outs[wi].shape[1] // 2
            for jj, chip in enumerate(chips):
                blk = outs[wi].at[2 * chip[0] + chip[1], _half_rows(1 - c, rh), :]
                pltpu.make_async_remote_copy(
                    src_ref=blk, dst_ref=blk, send_sem=send_sems.at[wi, 3 + jj], recv_sem=recv_sems.at[wi, 3 + jj],
                    device_id=sibling, device_id_type=MESH).wait_recv()
        for cp in firsts + passed:
            cp.wait_send()

    return pl.pallas_call(
        body, name="gather_weights",
        out_shape=[SDS(b.shape, b.dtype) for b in bufs],
        in_specs=[ANY_SPEC] * n, out_specs=[ANY_SPEC] * n,
        input_output_aliases={i: i for i in range(n)},
        scratch_shapes=[pltpu.SemaphoreType.DMA((n, 6)), pltpu.SemaphoreType.DMA((n, 6))],
    )(*bufs)


class _Carry:
    def __init__(self, ins, outs, alias, sems, copies):
        self.ins, self.outs, self.alias, self.sems, self.copies = list(ins), list(outs), dict(alias), list(sems), copies


def _remote(src, dst, send, recv, to):
    return functools.partial(pltpu.make_async_remote_copy, src_ref=src, dst_ref=dst, send_sem=send, recv_sem=recv,
                             device_id=to, device_id_type=MESH)


def _carry_gather_send(bufs):
    n = len(bufs)

    def copies(ins, outs, sems):
        x, y, c, kc, sibling, chips = _mesh_pos()
        starts, waits = [], []
        for wi in range(n):
            rh = outs[wi].shape[1] // 2
            for jj, chip in enumerate(chips):
                mine = outs[wi].at[kc, _half_rows(c, rh), :]
                cp = _remote(mine, mine, sems[0].at[wi, jj], sems[1].at[wi, jj], (*chip, c))
                starts.append(cp)
                waits.append((cp, "send"))
                theirs = outs[wi].at[2 * chip[0] + chip[1], _half_rows(c, rh), :]
                waits.append((_remote(theirs, theirs, sems[0].at[wi, jj], sems[1].at[wi, jj], (*chip, c)), "recv"))
        return starts, waits

    return _Carry(bufs, [SDS(b.shape, b.dtype) for b in bufs], {i: i for i in range(n)},
                  [pltpu.SemaphoreType.DMA((n, 3)), pltpu.SemaphoreType.DMA((n, 3))], copies)


def _carry_gather_forward(bufs):
    n = len(bufs)

    def copies(ins, outs, sems):
        x, y, c, kc, sibling, chips = _mesh_pos()
        starts, waits = [], []
        for wi in range(n):
            rh = outs[wi].shape[1] // 2
            for jj, chip in enumerate(chips):
                got = outs[wi].at[2 * chip[0] + chip[1], _half_rows(c, rh), :]
                cp = _remote(got, got, sems[0].at[wi, jj], sems[1].at[wi, jj], sibling)
                starts.append(cp)
                waits.append((cp, "send"))
                other = outs[wi].at[2 * chip[0] + chip[1], _half_rows(1 - c, rh), :]
                waits.append((_remote(other, other, sems[0].at[wi, jj], sems[1].at[wi, jj], sibling), "recv"))
        return starts, waits

    return _Carry(bufs, [SDS(b.shape, b.dtype) for b in bufs], {i: i for i in range(n)},
                  [pltpu.SemaphoreType.DMA((n, 3)), pltpu.SemaphoreType.DMA((n, 3))], copies)


def _carry_sibling_halves(grads):
    n = len(grads)

    def copies(ins, outs, sems):
        x, y, c, kc, sibling, chips = _mesh_pos()
        cps = [_remote(ins[wi].at[:, _half_rows(1 - c, ins[wi].shape[1] // 2), :], outs[wi],
                       sems[0].at[wi], sems[1].at[wi], sibling) for wi in range(n)]
        return cps, [(cp, "both") for cp in cps]

    return _Carry(grads, [SDS((N_CHIPS, g.shape[1] // 2, g.shape[2]), F32) for g in grads], {},
                  [pltpu.SemaphoreType.DMA((n,)), pltpu.SemaphoreType.DMA((n,))], copies)


def _carry_to_owner(cpbfs):
    n = len(cpbfs)

    def copies(ins, outs, sems):
        x, y, c, kc, sibling, chips = _mesh_pos()
        starts, waits = [], []
        for wi in range(n):
            for jj, chip in enumerate(chips):
                cp = _remote(ins[wi].at[2 * chip[0] + chip[1]], outs[wi].at[kc],
                             sems[0].at[wi, jj], sems[1].at[wi, jj], (*chip, c))
                starts.append(cp)
                waits.append((cp, "send"))
                slot = outs[wi].at[2 * chip[0] + chip[1]]
                waits.append((_remote(slot, slot, sems[0].at[wi, jj], sems[1].at[wi, jj], (*chip, c)), "recv"))
        return starts, waits

    return _Carry(cpbfs, [SDS(g.shape, BF16) for g in cpbfs], {},
                  [pltpu.SemaphoreType.DMA((n, 3)), pltpu.SemaphoreType.DMA((n, 3))], copies)


def _carry_join_halves(bufs):
    n = len(bufs)

    def copies(ins, outs, sems):
        x, y, c, kc, sibling, chips = _mesh_pos()
        cps = []
        for wi in range(n):
            mine = outs[wi].at[_half_rows(c, outs[wi].shape[0] // 2), :]
            cps.append(_remote(mine, mine, sems[0].at[wi], sems[1].at[wi], sibling))
        return cps, [(cp, "both") for cp in cps]

    return _Carry(bufs, [SDS(b.shape, F32) for b in bufs], {i: i for i in range(n)},
                  [pltpu.SemaphoreType.DMA((n,)), pltpu.SemaphoreType.DMA((n,))], copies)


def _merge_carries(*carries):
    ins, outs, alias, sems, parts = [], [], {}, [], []
    for cy in carries:
        parts.append((len(ins), len(cy.ins), len(outs), len(cy.outs), len(sems), len(cy.sems), cy.copies))
        alias.update({len(ins) + i: len(outs) + o for i, o in cy.alias.items()})
        ins += cy.ins
        outs += cy.outs
        sems += cy.sems

    def copies(i, o, s):
        starts, waits = [], []
        for i0, ni, o0, no, s0, ns, fn in parts:
            st, wt = fn(i[i0:i0 + ni], o[o0:o0 + no], s[s0:s0 + ns])
            starts += st
            waits += wt
        return starts, waits

    return _Carry(ins, outs, alias, sems, copies)


def _start_all(starts):
    for cp in starts:
        cp().start()


def _wait_all(waits):
    for cp, which in waits:
        if which == "send":
            cp().wait_send()
        elif which == "recv":
            cp().wait_recv()
        else:
            cp().wait()


def _comm_call(name, carry):
    n_i, n_o = len(carry.ins), len(carry.outs)

    def body(*refs):
        ins, outs, sems = refs[:n_i], refs[n_i:n_i + n_o], refs[n_i + n_o:]
        _start_all(carry.copies(ins, outs, sems)[0])
        _wait_all(carry.copies(ins, outs, sems)[1])

    return pl.pallas_call(
        body, name=name, out_shape=carry.outs, in_specs=[ANY_SPEC] * n_i, out_specs=[ANY_SPEC] * n_o,
        input_output_aliases=carry.alias, scratch_shapes=carry.sems,
    )(*carry.ins)


def _host_call(body, *, name, grid, in_specs, out_specs, out_shape, args, scratch_shapes=(), carry=None):
    n_in, n_out, n_scr = len(in_specs), len(out_specs), len(scratch_shapes)
    if carry is None:
        res = pl.pallas_call(body, name=name, grid=grid, in_specs=in_specs, out_specs=out_specs, out_shape=out_shape,
                             scratch_shapes=list(scratch_shapes), compiler_params=_params(len(grid)))(*args)
        return list(res), []
    n_ci, n_co = len(carry.ins), len(carry.outs)

    def wrapped(*refs):
        ins, cins = refs[:n_in], refs[n_in:n_in + n_ci]
        o0 = n_in + n_ci
        outs, couts = refs[o0:o0 + n_out], refs[o0 + n_out:o0 + n_out + n_co]
        s0 = o0 + n_out + n_co
        scr, sems = refs[s0:s0 + n_scr], refs[s0 + n_scr:]
        idx = [pl.program_id(a) for a in range(len(grid))]
        first = functools.reduce(jnp.logical_and, [i == 0 for i in idx])
        last = functools.reduce(jnp.logical_and, [i == g - 1 for i, g in zip(idx, grid)])

        @pl.when(first)
        def _():
            _start_all(carry.copies(cins, couts, sems)[0])

        body(*ins, *outs, *scr)

        @pl.when(last)
        def _():
            _wait_all(carry.copies(cins, couts, sems)[1])

    res = pl.pallas_call(
        wrapped, name=name, grid=grid, in_specs=list(in_specs) + [ANY_SPEC] * n_ci,
        out_specs=list(out_specs) + [ANY_SPEC] * n_co, out_shape=list(out_shape) + carry.outs,
        scratch_shapes=list(scratch_shapes) + carry.sems,
        input_output_aliases={n_in + i: n_out + o for i, o in carry.alias.items()},
        compiler_params=_params(len(grid)),
    )(*args, *carry.ins)
    return list(res[:n_out]), list(res[n_out:])


def _rs_add_halves(pos, grad, recv, name):
    _, rs, cs = grad.shape
    rh = rs // 2
    rb = rh // 2

    def body(pos_ref, g_ref, r_ref, o32_ref, obf_ref):
        s = g_ref[...] + r_ref[...]
        o32_ref[...] = s
        obf_ref[...] = s.astype(BF16)

    blk = (1, rb, cs)
    return pl.pallas_call(
        body, name=name,
        grid_spec=pltpu.PrefetchScalarGridSpec(
            num_scalar_prefetch=1, grid=(N_CHIPS, 2),
            in_specs=[pl.BlockSpec(blk, lambda k, i, p: (k, p[1] * 2 + i, 0)),
                      pl.BlockSpec(blk, lambda k, i, p: (k, i, 0))],
            out_specs=[pl.BlockSpec(blk, lambda k, i, p: (k, i, 0)), pl.BlockSpec(blk, lambda k, i, p: (k, i, 0))]),
        out_shape=[SDS((N_CHIPS, rh, cs), F32), SDS((N_CHIPS, rh, cs), BF16)],
        compiler_params=_params(2),
    )(pos, grad, recv)


def _rs_sum_owner(pos, cp32, recv3, name):
    _, rh, cs = cp32.shape
    rb = rh // 2

    def body(pos_ref, own_ref, r1_ref, r2_ref, r3_ref, o_ref):
        o_ref[...] = ((own_ref[0] + r1_ref[0].astype(F32)) + r2_ref[0].astype(F32)) + r3_ref[0].astype(F32)

    blk = (1, rb, cs)

    def slot(d):
        return pl.BlockSpec(blk, lambda i, p: ((p[0] + d) % N_CHIPS, i, 0))

    return pl.pallas_call(
        body, name=name,
        grid_spec=pltpu.PrefetchScalarGridSpec(
            num_scalar_prefetch=1, grid=(2,),
            in_specs=[slot(0), slot(1), slot(2), slot(3)],
            out_specs=pl.BlockSpec((rb, cs), lambda i, p: (p[1] * 2 + i, 0))),
        out_shape=SDS((2 * rh, cs), F32),
        compiler_params=_params(1),
    )(pos, cp32, recv3, recv3, recv3)


def _cast_bf16(pos, arrs):
    n = len(arrs)

    def body(pos_ref, *refs):
        for i in range(n):
            refs[n + i][0] = refs[i][...].astype(BF16)

    return pl.pallas_call(
        body, name="cast_bf16",
        grid_spec=pltpu.PrefetchScalarGridSpec(
            num_scalar_prefetch=1, grid=(2,),
            in_specs=[pl.BlockSpec((a.shape[0] // 2, a.shape[1]), lambda i, p: (i, 0)) for a in arrs],
            out_specs=[pl.BlockSpec((1, a.shape[0] // 2, a.shape[1]), lambda i, p: (p[0], i, 0)) for a in arrs]),
        out_shape=[SDS((N_CHIPS,) + a.shape, BF16) for a in arrs],
        compiler_params=_params(1),
    )(pos, *arrs)


def _adamw_vals(w, g, m, v):
    m2 = ADAM_B1 * m + (1.0 - ADAM_B1) * g
    v2 = ADAM_B2 * v + (1.0 - ADAM_B2) * (g * g)
    m_hat = m2 / (1.0 - ADAM_B1 ** ADAM_STEP)
    v_hat = v2 / (1.0 - ADAM_B2 ** ADAM_STEP)
    delta = -ADAM_LR * (m_hat / (jnp.sqrt(v_hat) + ADAM_EPS) + ADAM_WD * w)
    return delta, m2, v2


def _adamw_big(w, g, m, v, name):
    rows, cols = w.shape
    rb = rows // 4

    def body(w_ref, g_ref, m_ref, v_ref, d_ref, m2_ref, v2_ref):
        d, m2, v2 = _adamw_vals(w_ref[...], g_ref[...], m_ref[...], v_ref[...])
        d_ref[...] = d
        m2_ref[...] = m2
        v2_ref[...] = v2

    spec = pl.BlockSpec((rb, cols), lambda i: (i, 0))
    return pl.pallas_call(
        body, name=name, grid=(4,), in_specs=[spec] * 4, out_specs=[spec] * 3,
        out_shape=[SDS(w.shape, F32)] * 3, compiler_params=_params(1),
    )(w, g, m, v)


def _adamw_small(ws, gs, ms, vs):
    n = len(ws)

    def body(*refs):
        for i in range(n):
            d, m2, v2 = _adamw_vals(refs[i][...], refs[n + i][...], refs[2 * n + i][...], refs[3 * n + i][...])
            refs[4 * n + i][...] = d
            refs[5 * n + i][...] = m2
            refs[6 * n + i][...] = v2

    shapes = [SDS(w.shape, F32) for w in ws]
    outs = pl.pallas_call(
        body, name="adamw_small", out_shape=shapes * 3,
        in_specs=[VMEM_SPEC] * (4 * n), out_specs=[VMEM_SPEC] * (3 * n),
    )(*ws, *gs, *ms, *vs)
    return outs[:n], outs[n:2 * n], outs[2 * n:]


def _mod_fwd(cond64, w_mod_s, b_mod_s):
    def body(c_ref, w_ref, b_ref, o_ref):
        cc = c_ref[...]
        o_ref[...] = _dot_f32(cc * _sig(cc), w_ref[...]) + b_ref[...]

    return pl.pallas_call(
        body, name="mod_fwd", out_shape=SDS((cond64.shape[0], w_mod_s.shape[1]), F32),
        in_specs=[VMEM_SPEC] * 3, out_specs=VMEM_SPEC,
        compiler_params=pltpu.CompilerParams(vmem_limit_bytes=VMEM_LIMIT),
    )(cond64, w_mod_s, b_mod_s)


def _mod_bwd(cond64, dmod64, dmod64_my, w_mod_s, c_ctx):
    def body(c_ref, g_ref, gm_ref, w_ref, cc_ref, gw_ref, gb_ref, gcc_ref):
        cc = c_ref[...]
        act = cc * _sig(cc)
        gm = gm_ref[...]
        gw_ref[...] = _dot_f32(act, gm, (((0,), (0,)), ((), ())))
        gb_ref[...] = _colsum(g_ref[...])
        dact = _dot_f32(gm, w_ref[...], (((1,), (1,)), ((), ())))
        tot = dact[4:5, :]
        for dev in range(1, N_DEV):
            tot = tot + dact[8 * dev + 4:8 * dev + 5, :]
        c0 = cc_ref[...]
        s0 = _sig(c0)
        gcc_ref[...] = tot * (s0 * (1.0 + c0 * (1.0 - s0)))

    return pl.pallas_call(
        body, name="mod_bwd",
        out_shape=[SDS(w_mod_s.shape, F32), SDS((1, dmod64.shape[1]), F32), SDS((1, D_MODEL), F32)],
        in_specs=[VMEM_SPEC] * 5, out_specs=[VMEM_SPEC] * 3,
        compiler_params=pltpu.CompilerParams(vmem_limit_bytes=VMEM_LIMIT),
    )(cond64, dmod64, dmod64_my, w_mod_s, c_ctx)


def _in_fwd(x, ctx, g_mix, mod_a, w_in, carry=None):
    bs, seq, _ = x.shape
    nb = seq // ROW_BLOCK + 1

    def body(x_ref, ctx_ref, g_ref, mod_ref, w_ref, p_ref, h_ref):
        is_ctx = pl.program_id(1) == 0
        xin = jnp.where(is_ctx, ctx_ref[0], x_ref[0])
        shift = jnp.where(is_ctx, mod_ref[0, 2:3, :], mod_ref[0, 0:1, :])
        scale = jnp.where(is_ctx, mod_ref[0, 3:4, :], mod_ref[0, 1:2, :])
        xn, _ = _rms(xin)
        hb = ((xn * g_ref[...]) * (1.0 + scale) + shift).astype(BF16)
        h_ref[0] = hb
        p_ref[0] = jnp.dot(hb, w_ref[...], preferred_element_type=F32)

    return _host_call(
        body, name="in_fwd", grid=(bs, nb),
        in_specs=[pl.BlockSpec((1, ROW_BLOCK, D_MODEL), lambda b, j: (b, jnp.maximum(j - 1, 0), 0)),
                  pl.BlockSpec((1, ROW_BLOCK, D_MODEL), lambda b, j: (b, 0, 0)),
                  pl.BlockSpec((1, D_MODEL), lambda b, j: (0, 0)),
                  pl.BlockSpec((1, 8, D_MODEL), lambda b, j: (b, 0, 0)),
                  pl.BlockSpec((D_MODEL, IN_COLS), lambda b, j: (0, 0))],
        out_specs=[pl.BlockSpec((1, ROW_BLOCK, IN_COLS), lambda b, j: (b, j, 0)),
                   pl.BlockSpec((1, ROW_BLOCK, D_MODEL), lambda b, j: (b, j, 0))],
        out_shape=[SDS((bs, nb * ROW_BLOCK, IN_COLS), F32), SDS((bs, nb * ROW_BLOCK, D_MODEL), BF16)],
        args=(x, ctx, g_mix, mod_a, w_in), carry=carry)


def _in_bwd(x, ctx, dx1, g_mix, mod_a, w_in, df_f, df_b, di, dq, dpc, carry=None):
    bs, seq, _ = x.shape
    nb = seq // ROW_BLOCK + 1

    def body(x_ref, ctx_ref, dx1_ref, g_ref, mod_ref, w_ref, dff_ref, dfb_ref, di_ref, dq_ref, dpc_ref,
             gx_ref, dp_ref, dg_ref, dmod_ref):
        b, j = pl.program_id(0), pl.program_id(1)
        is_ctx = j == 0

        @pl.when((b == 0) & (j == 0))
        def _():
            dg_ref[...] = jnp.zeros_like(dg_ref)

        @pl.when(j == 0)
        def _():
            dmod_ref[...] = jnp.zeros_like(dmod_ref)

        dp = jnp.concatenate([dff_ref[0], dfb_ref[0], di_ref[0], dq_ref[0], dpc_ref[0]], axis=1)
        dp_ref[0] = dp
        dh = lax.dot_general(dp, w_ref[...], (((1,), (1,)), ((), ())), preferred_element_type=F32)
        xin = jnp.where(is_ctx, ctx_ref[0], x_ref[0])
        scale = jnp.where(is_ctx, mod_ref[0, 3:4, :], mod_ref[0, 1:2, :])
        xn, r = _rms(xin)
        g = g_ref[...]
        hn = xn * g
        d_shift = _colsum(dh)
        d_scale = _colsum(dh * hn)
        dhn = dh * (1.0 + scale)
        dg_ref[...] += _colsum(dhn * xn)
        dx = _rms_bwd(dhn * g, xn, r)

        @pl.when(is_ctx)
        def _():
            dmod_ref[0, 2:3, :] += d_shift
            dmod_ref[0, 3:4, :] += d_scale

        @pl.when(jnp.logical_not(is_ctx))
        def _():
            dmod_ref[0, 0:1, :] += d_shift
            dmod_ref[0, 1:2, :] += d_scale
            gx_ref[0] = dx + dx1_ref[0]

    def rows(w):
        return pl.BlockSpec((1, ROW_BLOCK, w), lambda b, j: (b, j, 0))

    lat = pl.BlockSpec((1, ROW_BLOCK, D_MODEL), lambda b, j: (b, jnp.maximum(j - 1, 0), 0))
    return _host_call(
        body, name="in_bwd", grid=(bs, nb),
        in_specs=[lat, pl.BlockSpec((1, ROW_BLOCK, D_MODEL), lambda b, j: (b, 0, 0)), lat,
                  pl.BlockSpec((1, D_MODEL), lambda b, j: (0, 0)),
                  pl.BlockSpec((1, 8, D_MODEL), lambda b, j: (b, 0, 0)),
                  pl.BlockSpec((D_MODEL, IN_COLS), lambda b, j: (0, 0)),
                  rows(KW), rows(KW), rows(KW), rows(KW), rows(7 * KW)],
        out_specs=[lat, rows(IN_COLS), pl.BlockSpec((1, D_MODEL), lambda b, j: (0, 0)),
                   pl.BlockSpec((1, 8, D_MODEL), lambda b, j: (b, 0, 0))],
        out_shape=[SDS(x.shape, F32), SDS((bs, nb * ROW_BLOCK, IN_COLS), BF16), SDS((1, D_MODEL), F32),
                   SDS((bs, 8, D_MODEL), F32)],
        args=(x, ctx, dx1, g_mix, mod_a, w_in, df_f, df_b, di, dq, dpc), carry=carry)


def _lower_bound(lbg_ref, direction):
    return _sig(lbg_ref[0, direction:direction + 1, :] - lbg_ref[1, direction:direction + 1, :])


N_CHUNKS = ROW_BLOCK // HGRN_CHUNK


def _block_tri(upper):
    t = np.arange(ROW_BLOCK)[:, None]
    s = np.arange(ROW_BLOCK)[None, :]
    same = (t // HGRN_CHUNK) == (s // HGRN_CHUNK)
    return jnp.asarray(same & ((s >= t) if upper else (s <= t)), dtype=BF16)


TRI_SPEC = pl.BlockSpec((ROW_BLOCK, ROW_BLOCK), lambda b, j: (0, 0))


def _tri_matmul_f32(tri, g):
    g0 = g.astype(BF16)
    r1 = g - g0.astype(F32)
    g1 = r1.astype(BF16)
    g2 = (r1 - g1.astype(F32)).astype(BF16)
    return (jnp.dot(tri, g2, preferred_element_type=F32) + jnp.dot(tri, g1, preferred_element_type=F32)) \
        + jnp.dot(tri, g0, preferred_element_type=F32)


def _chunk_rows(rows):
    return jnp.concatenate([jnp.broadcast_to(r, (HGRN_CHUNK, r.shape[1])) for r in rows], axis=0)


def _block_gates(fl, q, lb, tri, upper):
    t = {}
    t["sg"] = _sig(fl)
    t["f"] = lb + (1.0 - lb) * t["sg"]
    k = 1.0 - t["f"]
    bcum = _tri_matmul_f32(tri, jnp.log(t["f"]))
    ends = [bcum[ci * HGRN_CHUNK:ci * HGRN_CHUNK + 1] if upper else bcum[(ci + 1) * HGRN_CHUNK - 1:(ci + 1) * HGRN_CHUNK]
            for ci in range(N_CHUNKS)]
    mid = _chunk_rows([0.5 * r for r in ends])
    t["dec"] = [jnp.exp(r) for r in ends]
    t["e1"] = jnp.exp(bcum - mid)
    t["e2"] = jnp.exp(mid - bcum)
    t["eh"] = _chunk_rows([jnp.exp(0.5 * r) for r in ends])
    t["qi"] = q * t["e1"]
    t["ki"] = k * t["e2"]
    t["kd"] = t["ki"] * t["eh"]
    t["qe"] = t["qi"] * t["eh"]
    return t


def _hgrn_block_order(direction, nb):
    if direction == 0:
        return lambda j: j
    return lambda j: jnp.where(j == 0, 0, nb - j)


def _hgrn_fwd(p, lbg, direction, carry=None):
    bs, rows, _ = p.shape
    nb = rows // ROW_BLOCK
    ncb = ROW_BLOCK // HGRN_CHUNK
    upper = direction == 1
    order = _hgrn_block_order(direction, nb)

    def body(f_ref, i_ref, q_ref, lbg_ref, tri_ref, o_ref, s_ref, st):
        @pl.when(pl.program_id(1) == 0)
        def _():
            st[...] = jnp.zeros_like(st)

        lb = _lower_bound(lbg_ref, direction)
        mask = _tri(HGRN_CHUNK, upper)
        t = _block_gates(f_ref[0], q_ref[0], lb, tri_ref[...], upper)
        v = i_ref[0]
        chunk = lambda a, ci, h: a[ci * HGRN_CHUNK:(ci + 1) * HGRN_CHUNK, h * HEAD_DIM:(h + 1) * HEAD_DIM]
        intra = [[None] * N_HEADS for _ in range(ncb)]
        ds_loc = [[None] * N_HEADS for _ in range(ncb)]
        for ci in range(ncb):
            for h in range(N_HEADS):
                a = jnp.where(mask, _dot_nt(chunk(t["qi"], ci, h), chunk(t["ki"], ci, h)), 0.0)
                intra[ci][h] = _dot(a, chunk(v, ci, h))
                ds_loc[ci][h] = _dot_tn(chunk(v, ci, h), chunk(t["kd"], ci, h))
        for h in range(N_HEADS):
            ls = slice(h * HEAD_DIM, (h + 1) * HEAD_DIM)
            s = st[h]
            for ci in (reversed(range(ncb)) if upper else range(ncb)):
                s_ref[0, 0, ci, h] = s
                o_ref[0, ci * HGRN_CHUNK:(ci + 1) * HGRN_CHUNK, ls] = intra[ci][h] + _dot_nt(chunk(t["qe"], ci, h), s)
                s = s * t["dec"][ci][:, ls] + ds_loc[ci][h]
            st[h] = s

    def col(cb):
        return pl.BlockSpec((1, ROW_BLOCK, KW), lambda b, j: (b, order(j), cb))

    return _host_call(
        body, name=f"hgrn_fwd{direction}", grid=(bs, nb),
        in_specs=[col(direction), col(2), col(3), pl.BlockSpec((2, 2, KW), lambda b, j: (0, 0, 0)), TRI_SPEC],
        out_specs=[pl.BlockSpec((1, ROW_BLOCK, KW), lambda b, j: (b, order(j), 0)),
                   pl.BlockSpec((1, 1, ncb, N_HEADS, HEAD_DIM, HEAD_DIM), lambda b, j: (b, order(j), 0, 0, 0, 0))],
        out_shape=[SDS((bs, rows, KW), F32), SDS((bs, nb, ncb, N_HEADS, HEAD_DIM, HEAD_DIM), F32)],
        scratch_shapes=[pltpu.VMEM((N_HEADS, HEAD_DIM, HEAD_DIM), F32)],
        args=(p, p, p, lbg, _block_tri(upper)), carry=carry)


def _hgrn_bwd(p, lbg, s_saved, do_raw, direction, dq_prev=None, dv_prev=None, carry=None):
    bs, rows, _ = p.shape
    nb = rows // ROW_BLOCK
    ncb = ROW_BLOCK // HGRN_CHUNK
    upper = direction == 1
    fwd_order = _hgrn_block_order(direction, nb)
    order = lambda j: fwd_order(nb - 1 - j)
    last = dq_prev is not None
    out_dt = BF16 if last else F32

    def body(*refs):
        if last:
            (f_ref, i_ref, q_ref, lbg_ref, tri_ref, trit_ref, s_ref, do_ref, dqp_ref, dvp_ref,
             df_ref, dq_ref, dv_ref, dlb_ref, dst, acc) = refs
        else:
            (f_ref, i_ref, q_ref, lbg_ref, tri_ref, trit_ref, s_ref, do_ref,
             df_ref, dq_ref, dv_ref, dlb_ref, dst, acc) = refs
        b, j = pl.program_id(0), pl.program_id(1)

        @pl.when((b == 0) & (j == 0))
        def _():
            dlb_ref[...] = jnp.zeros_like(dlb_ref)

        @pl.when(j == 0)
        def _():
            dst[...] = jnp.zeros_like(dst)

        lb = _lower_bound(lbg_ref, direction)
        mask = _tri(HGRN_CHUNK, upper)
        mask_t = _tri(HGRN_CHUNK, not upper)
        t = _block_gates(f_ref[0], q_ref[0], lb, tri_ref[...], upper)
        qi, ki, kd, qe = t["qi"], t["ki"], t["kd"], t["qe"]
        v = i_ref[0]
        do = do_ref[0]
        chunk = lambda a, ci, h: a[ci * HGRN_CHUNK:(ci + 1) * HGRN_CHUNK, h * HEAD_DIM:(h + 1) * HEAD_DIM]
        grid2 = lambda: [[None] * N_HEADS for _ in range(ncb)]
        dv_in, ds_loc = grid2(), grid2()
        for ci in range(ncb):
            rs = slice(ci * HGRN_CHUNK, (ci + 1) * HGRN_CHUNK)
            for h in range(N_HEADS):
                ls = slice(h * HEAD_DIM, (h + 1) * HEAD_DIM)
                do_h, v_h = chunk(do, ci, h), chunk(v, ci, h)
                qi_h, ki_h = chunk(qi, ci, h), chunk(ki, ci, h)
                a_t = jnp.where(mask_t, _dot_nt(ki_h, qi_h), 0.0)
                da = jnp.where(mask, _dot_nt(do_h, v_h), 0.0)
                da_t = jnp.where(mask_t, _dot_nt(v_h, do_h), 0.0)
                acc[0, rs, ls] = _dot(da, ki_h)
                acc[1, rs, ls] = _dot(da_t, qi_h)
                acc[3, rs, ls] = _dot(do_h, s_ref[0, 0, ci, h])
                dv_in[ci][h] = _dot(a_t, do_h)
                ds_loc[ci][h] = _dot_tn(do_h, chunk(qe, ci, h))
        ddec = [[None] * N_HEADS for _ in range(ncb)]
        for h in range(N_HEADS):
            ls = slice(h * HEAD_DIM, (h + 1) * HEAD_DIM)
            ds = dst[h]
            for ci in (range(ncb) if upper else reversed(range(ncb))):
                rs = slice(ci * HGRN_CHUNK, (ci + 1) * HGRN_CHUNK)
                acc[2, rs, ls] = _dot(chunk(v, ci, h), ds)
                acc[4, rs, ls] = dv_in[ci][h] + _dot_nt(chunk(kd, ci, h), ds)
                ddec[ci][h] = _colsum(ds * s_ref[0, 0, ci, h])
                ds = ds * t["dec"][ci][:, ls] + ds_loc[ci][h]
            dst[h] = ds
        dqi, dki, dkd, dqe, dv = (acc[i] for i in range(5))
        dq = t["e1"] * (dqi + dqe * t["eh"])
        dk = t["e2"] * (dki + dkd * t["eh"])
        db = dqi * qi - dki * ki - dkd * kd + dqe * qe
        dkd_kd = dkd * kd
        dbl = [_colsum(dkd_kd[ci * HGRN_CHUNK:(ci + 1) * HGRN_CHUNK]) + jnp.concatenate(ddec[ci], axis=1) * t["dec"][ci]
               for ci in range(ncb)]
        dg = _tri_matmul_f32(trit_ref[...], db) + _chunk_rows(dbl)
        df = dg / t["f"] - dk
        sg = t["sg"]
        dlb_ref[...] += _colsum(df * (1.0 - sg))
        df_ref[0] = (df * (1.0 - lb) * sg * (1.0 - sg)).astype(BF16)
        if last:
            dq_ref[0] = (dq + dqp_ref[0]).astype(out_dt)
            dv_ref[0] = (dv + dvp_ref[0]).astype(out_dt)
        else:
            dq_ref[0] = dq
            dv_ref[0] = dv

    def col(cb):
        return pl.BlockSpec((1, ROW_BLOCK, KW), lambda b, j: (b, order(j), cb))

    row = pl.BlockSpec((1, ROW_BLOCK, KW), lambda b, j: (b, order(j), 0))
    in_specs = [col(direction), col(2), col(3), pl.BlockSpec((2, 2, KW), lambda b, j: (0, 0, 0)), TRI_SPEC, TRI_SPEC,
                pl.BlockSpec((1, 1, ncb, N_HEADS, HEAD_DIM, HEAD_DIM), lambda b, j: (b, order(j), 0, 0, 0, 0)), row]
    args = [p, p, p, lbg, _block_tri(upper), _block_tri(not upper), s_saved, do_raw]
    if last:
        in_specs += [row, row]
        args += [dq_prev, dv_prev]
    return _host_call(
        body, name=f"hgrn_bwd{direction}", grid=(bs, nb), in_specs=in_specs,
        out_specs=[row, row, row, pl.BlockSpec((1, KW), lambda b, j: (0, 0))],
        out_shape=[SDS((bs, rows, KW), BF16), SDS((bs, rows, KW), out_dt), SDS((bs, rows, KW), out_dt),
                   SDS((1, KW), F32)],
        scratch_shapes=[pltpu.VMEM((N_HEADS, HEAD_DIM, HEAD_DIM), F32), pltpu.VMEM((5, ROW_BLOCK, KW), F32)],
        args=args, carry=carry)


def _mix_values(og, u, v, ga, gb, o_raw, gna, lng, lnb, ws_ref, bst, wpa, wpb, wo):
    t = {}
    sog = _sig(og)
    t["sog"], t["silu_og"] = sog, og * sog
    xh_l, r_l = [], []
    for h in range(N_HEADS):
        xh, r = _rms(o_raw[:, h * HEAD_DIM:(h + 1) * HEAD_DIM])
        xh_l.append(xh)
        r_l.append(r)
    t["xh"], t["r"] = jnp.concatenate(xh_l, axis=1), r_l
    gna4 = jnp.concatenate([gna] * N_HEADS, axis=1)
    t["gna4"] = gna4
    t["o_n"] = t["xh"] * gna4
    t["o_a"] = t["o_n"] * t["silu_og"]
    t["gu"], t["tu"] = _gelu(u)
    gv, t["tv"] = _gelu(v)
    mu = jnp.mean(gv, axis=-1, keepdims=True)
    cen = gv - mu
    t["rstd"] = lax.rsqrt(jnp.mean(cen * cen, axis=-1, keepdims=True) + EPS)
    t["xhat"] = cen * t["rstd"]
    vn = t["xhat"] * lng + lnb
    t["vn"] = vn
    chunks = []
    for n in range(ROW_BLOCK // SGU_CHUNK):
        rs = slice(n * SGU_CHUNK, (n + 1) * SGU_CHUNK)
        groups = []
        for g in range(N_HEADS):
            ls = slice(g * HEAD_DIM, (g + 1) * HEAD_DIM)
            groups.append(_dot(ws_ref[g], vn[rs, ls]) + bst[:, g:g + 1])
        chunks.append(jnp.concatenate(groups, axis=1))
    t["mixed"] = jnp.concatenate(chunks, axis=0)
    t["o_bm"] = t["gu"] * t["mixed"]
    t["ya"] = _dot(t["o_a"], wpa)
    t["yb"] = _dot(t["o_bm"], wpb)
    t["sa"], t["sb"] = _sig(ga), _sig(gb)
    t["merged"] = t["sa"] * t["ya"] + t["sb"] * t["yb"]
    t["mix"] = _dot(t["merged"], wo)
    return t


def _mix_in_specs(row_of):
    def col(cb):
        return pl.BlockSpec((1, ROW_BLOCK, KW), lambda b, j: (b, row_of(j), cb))
    return [col(cb) for cb in range(4, 11)]


def _mix_param_specs():
    full2 = lambda r, c: pl.BlockSpec((r, c), lambda b, j: (0, 0))
    return [full2(1, HEAD_DIM), full2(1, KW), full2(1, KW),
            pl.BlockSpec((N_HEADS, SGU_CHUNK, SGU_CHUNK), lambda b, j: (0, 0, 0)),
            full2(SGU_CHUNK, N_HEADS), full2(KW, D_MODEL), full2(KW, D_MODEL), full2(D_MODEL, D_MODEL)]


def _mix_fwd(p, o_f, o_b, x, mod_c, gna, lng, lnb, w_s, bst, wpa, wpb, wo):
    bs, seq, _ = x.shape
    nbl = seq // ROW_BLOCK

    def body(og_r, u_r, v_r, ga0_r, ga1_r, gb0_r, gb1_r, of_r, ob_r, x_r, mod_r,
             gna_r, lng_r, lnb_r, ws_r, bst_r, wpa_r, wpb_r, wo_r, x1_r):
        ga = jnp.concatenate([ga0_r[0], ga1_r[0]], axis=1)
        gb = jnp.concatenate([gb0_r[0], gb1_r[0]], axis=1)
        t = _mix_values(og_r[0], u_r[0], v_r[0], ga, gb, of_r[0] + ob_r[0], gna_r[...], lng_r[...], lnb_r[...],
                        ws_r, bst_r[...], wpa_r[...], wpb_r[...], wo_r[...])
        x1_r[0] = x_r[0] + mod_r[0, 0:1, :] * t["mix"]

    row = lambda w: pl.BlockSpec((1, ROW_BLOCK, w), lambda b, j: (b, j + 1, 0))
    lat = pl.BlockSpec((1, ROW_BLOCK, D_MODEL), lambda b, j: (b, j, 0))
    return pl.pallas_call(
        body, name="mix_fwd", grid=(bs, nbl),
        in_specs=_mix_in_specs(lambda j: j + 1) + [row(KW), row(KW), lat,
                                                    pl.BlockSpec((1, 8, D_MODEL), lambda b, j: (b, 0, 0))]
        + _mix_param_specs(),
        out_specs=lat, out_shape=SDS(x.shape, F32), compiler_params=_params(2),
    )(p, p, p, p, p, p, p, o_f, o_b, x, mod_c, gna, lng, lnb, w_s, bst, wpa, wpb, wo)


def _mix_bwd(p, o_f, o_b, dx1, mod_c, gna, lng, lnb, w_s, w_s_t, bst, wpa, wpb, wo, carry=None):
    bs, rows, _ = p.shape
    nb = rows // ROW_BLOCK

    def body(og_r, u_r, v_r, ga0_r, ga1_r, gb0_r, gb1_r, of_r, ob_r, dx1_r, mod_r,
             gna_r, lng_r, lnb_r, ws_r, bst_r, wpa_r, wpb_r, wo_r, wst_r,
             dor_r, dpc_r, dwpa_r, dwpb_r, dwo_r, dgna_r, dlng_r, dlnb_r, dws_r, dbst_r, dmod_r):
        b, j = pl.program_id(0), pl.program_id(1)

        @pl.when((b == 0) & (j == 0))
        def _():
            for r in (dwpa_r, dwpb_r, dwo_r, dgna_r, dlng_r, dlnb_r, dws_r, dbst_r):
                r[...] = jnp.zeros_like(r)

        @pl.when(j == 0)
        def _():
            dmod_r[...] = jnp.zeros_like(dmod_r)
            dor_r[...] = jnp.zeros_like(dor_r)
            dpc_r[...] = jnp.zeros_like(dpc_r)

        @pl.when(j > 0)
        def _():
            og, u, v = og_r[0], u_r[0], v_r[0]
            ga = jnp.concatenate([ga0_r[0], ga1_r[0]], axis=1)
            gb = jnp.concatenate([gb0_r[0], gb1_r[0]], axis=1)
            gna, lng = gna_r[...], lng_r[...]
            wpa, wpb, wo = wpa_r[...], wpb_r[...], wo_r[...]
            t = _mix_values(og, u, v, ga, gb, of_r[0] + ob_r[0], gna, lng, lnb_r[...],
                            ws_r, bst_r[...], wpa, wpb, wo)
            dx1 = dx1_r[0]
            dmod_r[0, 0:1, :] += _colsum(dx1 * t["mix"])
            dmix = mod_r[0, 0:1, :] * dx1
            dmerged = _dot_nt(dmix, wo)
            dwo_r[...] += _dot_tn(t["merged"], dmix)
            sa, sb = t["sa"], t["sb"]
            dya, dyb = sa * dmerged, sb * dmerged
            dga = dmerged * t["ya"] * sa * (1.0 - sa)
            dgb = dmerged * t["yb"] * sb * (1.0 - sb)
            do_a = _dot_nt(dya, wpa)
            dwpa_r[...] += _dot_tn(t["o_a"], dya)
            do_bm = _dot_nt(dyb, wpb)
            dwpb_r[...] += _dot_tn(t["o_bm"], dyb)
            sog = t["sog"]
            dog = do_a * t["o_n"] * (sog * (1.0 + og * (1.0 - sog)))
            do_n = do_a * t["silu_og"]
            dxh = do_n * t["gna4"]
            prod = do_n * t["xh"]
            dgna = jnp.zeros((1, HEAD_DIM), F32)
            dor_l = []
            for h in range(N_HEADS):
                ls = slice(h * HEAD_DIM, (h + 1) * HEAD_DIM)
                dgna = dgna + _colsum(prod[:, ls])
                dor_l.append(_rms_bwd(dxh[:, ls], t["xh"][:, ls], t["r"][h]))
            dgna_r[...] += dgna
            dor_r[0] = jnp.concatenate(dor_l, axis=1)
            du = do_bm * t["mixed"] * _dgelu(u, t["tu"])
            dmixed = do_bm * t["gu"]
            vn = t["vn"]
            dvn_chunks = []
            for n in range(ROW_BLOCK // SGU_CHUNK):
                rs = slice(n * SGU_CHUNK, (n + 1) * SGU_CHUNK)
                groups = []
                for g in range(N_HEADS):
                    ls = slice(g * HEAD_DIM, (g + 1) * HEAD_DIM)
                    dm = dmixed[rs, ls]
                    dws_r[g] += _dot_nt(dm, vn[rs, ls])
                    dbst_r[:, g:g + 1] += jnp.sum(dm, axis=1, keepdims=True)
                    groups.append(_dot(wst_r[g], dm))
                dvn_chunks.append(jnp.concatenate(groups, axis=1))
            dvn = jnp.concatenate(dvn_chunks, axis=0)
            xhat = t["xhat"]
            dlng_r[...] += _colsum(dvn * xhat)
            dlnb_r[...] += _colsum(dvn)
            dxhat = dvn * lng
            dgv = t["rstd"] * (dxhat - jnp.mean(dxhat, axis=-1, keepdims=True)
                               - xhat * jnp.mean(dxhat * xhat, axis=-1, keepdims=True))
            dv = dgv * _dgelu(v, t["tv"])
            dpc_r[0] = jnp.concatenate([dog, du, dv, dga, dgb], axis=1).astype(BF16)

    row = lambda w: pl.BlockSpec((1, ROW_BLOCK, w), lambda b, j: (b, j, 0))
    lat = pl.BlockSpec((1, ROW_BLOCK, D_MODEL), lambda b, j: (b, jnp.maximum(j - 1, 0), 0))
    full2 = lambda r, c: pl.BlockSpec((r, c), lambda b, j: (0, 0))
    ws_spec = pl.BlockSpec((N_HEADS, SGU_CHUNK, SGU_CHUNK), lambda b, j: (0, 0, 0))
    return _host_call(
        body, name="mix_bwd", grid=(bs, nb),
        in_specs=_mix_in_specs(lambda j: j) + [row(KW), row(KW), lat,
                                                pl.BlockSpec((1, 8, D_MODEL), lambda b, j: (b, 0, 0))]
        + _mix_param_specs() + [ws_spec],
        out_specs=[row(KW), row(7 * KW), full2(KW, D_MODEL), full2(KW, D_MODEL), full2(D_MODEL, D_MODEL),
                   full2(1, HEAD_DIM), full2(1, KW), full2(1, KW), ws_spec, full2(SGU_CHUNK, N_HEADS),
                   pl.BlockSpec((1, 8, D_MODEL), lambda b, j: (b, 0, 0))],
        out_shape=[SDS((bs, rows, KW), F32), SDS((bs, rows, 7 * KW), BF16), SDS((KW, D_MODEL), F32),
                   SDS((KW, D_MODEL), F32), SDS((D_MODEL, D_MODEL), F32), SDS((1, HEAD_DIM), F32),
                   SDS((1, KW), F32), SDS((1, KW), F32), SDS((N_HEADS, SGU_CHUNK, SGU_CHUNK), F32),
                   SDS((SGU_CHUNK, N_HEADS), F32), SDS((bs, 8, D_MODEL), F32)],
        args=(p, p, p, p, p, p, p, o_f, o_b, dx1, mod_c, gna, lng, lnb, w_s, bst, wpa, wpb, wo, w_s_t), carry=carry)


def _ffn(x1, target, mod_c, g_ffn, g_final, w_up, w_down):
    bs, seq, _ = x1.shape
    nbl = seq // ROW_BLOCK

    def body(x1_r, tg_r, mod_r, gf_r, gl_r, wu_r, wd_r,
             dx1_r, h2_r, dab_r, hid_r, dffn_r, loss_r, dgl_r, dgf_r, dmod_r):
        b, j = pl.program_id(0), pl.program_id(1)

        @pl.when((b == 0) & (j == 0))
        def _():
            for r in (loss_r, dgl_r, dgf_r):
                r[...] = jnp.zeros_like(r)

        @pl.when(j == 0)
        def _():
            dmod_r[...] = jnp.zeros_like(dmod_r)

        x1 = x1_r[0]
        shift, scale, gate = mod_r[0, 1:2, :], mod_r[0, 2:3, :], mod_r[0, 3:4, :]
        gf, gl = gf_r[...], gl_r[...]
        xn2, r2 = _rms(x1)
        hn2 = xn2 * gf
        h2 = (hn2 * (1.0 + scale) + shift).astype(BF16)
        h2_r[0] = h2
        ab = jnp.dot(h2, wu_r[...], preferred_element_type=F32)
        a, bb = ab[:, :D_FF], ab[:, D_FF:]
        sa = _sig(a)
        silu_a = a * sa
        hid = (silu_a * bb).astype(BF16)
        hid_r[0] = hid
        ffn = jnp.dot(hid, wd_r[...], preferred_element_type=F32)
        x2 = x1 + gate * ffn
        xn3, r3 = _rms(x2)
        err = xn3 * gl - tg_r[0]
        loss_r[...] += 0.5 * jnp.sum(jnp.mean(err * err, axis=-1, keepdims=True), axis=0, keepdims=True)
        dy = err * (1.0 / D_MODEL)
        dgl_r[...] += _colsum(dy * xn3)
        dx2 = _rms_bwd(dy * gl, xn3, r3)
        dmod_r[0, 3:4, :] += _colsum(dx2 * ffn)
        dffn = (gate * dx2).astype(BF16)
        dffn_r[0] = dffn
        dhid = lax.dot_general(dffn, wd_r[...], (((1,), (1,)), ((), ())), preferred_element_type=F32)
        da = dhid * bb * (sa * (1.0 + a * (1.0 - sa)))
        db = dhid * silu_a
        dab = jnp.concatenate([da, db], axis=1).astype(BF16)
        dab_r[0] = dab
        dh2 = lax.dot_general(dab, wu_r[...], (((1,), (1,)), ((), ())), preferred_element_type=F32)
        dmod_r[0, 1:2, :] += _colsum(dh2)
        dmod_r[0, 2:3, :] += _colsum(dh2 * hn2)
        dhn2 = dh2 * (1.0 + scale)
        dgf_r[...] += _colsum(dhn2 * xn2)
        dx1_r[0] = dx2 + _rms_bwd(dhn2 * gf, xn2, r2)

    lat = lambda w: pl.BlockSpec((1, ROW_BLOCK, w), lambda b, j: (b, j, 0))
    full2 = lambda r, c: pl.BlockSpec((r, c), lambda b, j: (0, 0))
    mod_spec = pl.BlockSpec((1, 8, D_MODEL), lambda b, j: (b, 0, 0))
    return pl.pallas_call(
        body, name="ffn", grid=(bs, nbl),
        in_specs=[lat(D_MODEL), lat(D_MODEL), mod_spec, full2(1, D_MODEL), full2(1, D_MODEL),
                  full2(D_MODEL, 2 * D_FF), full2(D_FF, D_MODEL)],
        out_specs=[lat(D_MODEL), lat(D_MODEL), lat(2 * D_FF), lat(D_FF), lat(D_MODEL),
                   full2(1, 1), full2(1, D_MODEL), full2(1, D_MODEL), mod_spec],
        out_shape=[SDS(x1.shape, F32), SDS(x1.shape, BF16), SDS((bs, seq, 2 * D_FF), BF16),
                   SDS((bs, seq, D_FF), BF16), SDS(x1.shape, BF16), SDS((1, 1), F32),
                   SDS((1, D_MODEL), F32), SDS((1, D_MODEL), F32), SDS((bs, 8, D_MODEL), F32)],
        compiler_params=_params(2),
    )(x1, target, mod_c, g_ffn, g_final, w_up, w_down)


def _matmul_tn(a, b, n_blocks, tk, name):
    t, m = a.shape
    n = b.shape[1]
    tn = n // n_blocks

    def body(a_ref, b_ref, o_ref):
        @pl.when(pl.program_id(1) == 0)
        def _():
            o_ref[...] = jnp.zeros_like(o_ref)
        o_ref[0] += _dot_tn(a_ref[...], b_ref[...])

    return pl.pallas_call(
        body, name=name, grid=(n_blocks, t // tk),
        in_specs=[pl.BlockSpec((tk, m), lambda i, k: (k, 0)), pl.BlockSpec((tk, tn), lambda i, k: (k, i))],
        out_specs=pl.BlockSpec((1, m, tn), lambda i, k: (i, 0, 0)),
        out_shape=SDS((n_blocks, m, tn), F32), compiler_params=_params(2),
    )(a, b)


SMALL_ROWS = 80
ROW_CCTX = 3


def _small_reduce(gathered, lbg):
    def body(g_ref, lbg_ref, s_ref, dgam_ref):
        tot = g_ref[0:SMALL_ROWS, :]
        for dev in range(1, N_DEV):
            tot = tot + g_ref[dev * SMALL_ROWS:(dev + 1) * SMALL_ROWS, :]
        s_ref[...] = tot
        cc = g_ref[ROW_CCTX:ROW_CCTX + 1, :]
        for dev in range(2, N_DEV, 2):
            cc = cc + g_ref[dev * SMALL_ROWS + ROW_CCTX:dev * SMALL_ROWS + ROW_CCTX + 1, :]
        s_ref[ROW_CCTX:ROW_CCTX + 1, :] = cc
        dlb = tot[7:8, :]
        for d in range(2):
            s0 = _sig(lbg_ref[0, d:d + 1, :] - lbg_ref[1, d:d + 1, :])
            dgam_ref[d:d + 1, :] = dlb[:, d * KW:(d + 1) * KW] * s0 * (1.0 - s0)

    return pl.pallas_call(
        body, name="small_reduce", out_shape=[SDS((SMALL_ROWS, D_MODEL), F32), SDS((2, KW), F32)],
        in_specs=[VMEM_SPEC] * 2, out_specs=[VMEM_SPEC] * 2,
    )(gathered, lbg)


def _pad_cols(a, width):
    return jnp.pad(a, ((0, 0), (0, width - a.shape[1])))


def kernel(x, c, ctx, c_ctx, w_mod, b_mod, g_mix, g_ffn, w_in, lb_gamma, g_norm_a, ln_v_g, ln_v_b, w_s, b_s, w_pa, w_pb, w_o, w_up, w_down, g_final, loss_target, m_c_ctx, m_w_mod, m_b_mod, m_g_mix, m_g_ffn, m_w_in, m_lb_gamma, m_g_norm_a, m_ln_v_g, m_ln_v_b, m_w_s, m_b_s, m_w_pa, m_w_pb, m_w_o, m_w_up, m_w_down, m_g_final, v_c_ctx, v_w_mod, v_b_mod, v_g_mix, v_g_ffn, v_w_in, v_lb_gamma, v_g_norm_a, v_ln_v_g, v_ln_v_b, v_w_s, v_b_s, v_w_pa, v_w_pb, v_w_o, v_w_up, v_w_down, v_g_final):
    ax, ay, ac = lax.axis_index("x"), lax.axis_index("y"), lax.axis_index("c")
    kc = 2 * ax + ay
    dev = 2 * kc + ac
    pos = jnp.stack([kc, ac]).astype(jnp.int32)
    bs, seq, _ = x.shape
    assert bs <= 4 and ctx.shape[1] == ROW_BLOCK and seq % ROW_BLOCK == 0
    mod_cols = w_mod.shape[2]

    lbg_row = _pad_cols(lb_gamma.reshape(1, -1), D_MODEL)
    pay1 = jnp.concatenate([c, jnp.zeros((4 - bs, D_MODEL), F32), c_ctx[None, :], lbg_row,
                            jnp.zeros((2, D_MODEL), F32)], axis=0)
    cond64 = _all_gather8(pay1, "gather_cond")
    lbg_full = cond64.reshape(N_DEV, 8, D_MODEL)[0::2, 5, :KW].reshape(N_CHIPS, 2, 2, HEAD_DIM)
    lbg_full = jnp.transpose(lbg_full, (1, 2, 0, 3)).reshape(2, 2, KW)

    b_mod_s = lax.dynamic_slice(b_mod, (0, kc * mod_cols), (1, mod_cols))
    mod_s = _mod_fwd(cond64, w_mod[0], b_mod_s)
    mod_g = _all_gather8(mod_s, "gather_mod").reshape(N_DEV, 64, mod_cols)[0::2]
    mod_full = jnp.transpose(mod_g, (1, 0, 2)).reshape(64, N_CHIPS * mod_cols)
    mod_mine = lax.dynamic_slice(mod_full, (dev * 8, 0), (8, 6 * D_MODEL)).reshape(8, 6, D_MODEL)
    mod, mc = mod_mine[:bs], mod_mine[4]
    zeros4 = jnp.zeros((bs, 4, D_MODEL), F32)
    mod_a = jnp.concatenate([mod[:, 0:2], jnp.broadcast_to(mc[None, 0:2], (bs, 2, D_MODEL)), zeros4], axis=1)
    mod_c = jnp.concatenate([mod[:, 2:6], zeros4], axis=1)

    shards = [w_in[0], w_up[0], w_pa[0], w_pb[0], w_o[0], w_down[0]]
    bufs = _cast_bf16(pos, shards)
    (w_in_g,) = _gather_weights(bufs[:1])

    def cols_major(a):
        return jnp.transpose(a, (1, 0, 2)).reshape(a.shape[1], -1)

    gna, lng, lnb = g_norm_a, ln_v_g, ln_v_b
    ws3 = w_s[0]
    ws3_t = jnp.transpose(ws3, (0, 2, 1))
    bst = jnp.transpose(b_s[0])

    w_in_f = cols_major(w_in_g)
    (p, h_all), sent = _in_fwd(x, ctx, g_mix, mod_a, w_in_f, carry=_carry_gather_send(bufs[1:]))
    (o_f, s_f), gathered = _hgrn_fwd(p, lbg_full, 0, carry=_carry_gather_forward(sent))
    (o_b, s_b), _ = _hgrn_fwd(p, lbg_full, 1)
    w_up_f, w_pa_f, w_pb_f = (cols_major(a) for a in gathered[:3])
    w_o_f = gathered[3].reshape(-1, D_MODEL)
    w_down_f = gathered[4].reshape(-1, D_MODEL)
    x1 = _mix_fwd(p, o_f, o_b, x, mod_c, gna, lng, lnb, ws3, bst, w_pa_f, w_pb_f, w_o_f)
    dx1, h2, dab, hid, dffn, loss_part, dg_final, dg_ffn, dmod_ffn = _ffn(
        x1, loss_target, mod_c, g_ffn, g_final[None, :], w_up_f, w_down_f)
    rows_lat = bs * seq
    tk_lat = 2 * ROW_BLOCK if rows_lat % (2 * ROW_BLOCK) == 0 else ROW_BLOCK
    dw_up = _matmul_tn(h2.reshape(rows_lat, D_MODEL), dab.reshape(rows_lat, 2 * D_FF), N_CHIPS, tk_lat, "dw_up")
    dw_down = _matmul_tn(hid.reshape(rows_lat, D_FF), dffn.reshape(rows_lat, D_MODEL), 1, tk_lat, "dw_down")

    def shard_major(a):
        return jnp.transpose(a.reshape(a.shape[0], N_CHIPS, -1), (1, 0, 2))

    def add_halves(parts, recvs, names):
        sums = [_rs_add_halves(pos, g, r, "rs_add_" + nm) for g, r, nm in zip(parts, recvs, names)]
        return [s[0] for s in sums], [s[1] for s in sums]

    def sum_owner(cp32s, recvs, names):
        return [_rs_sum_owner(pos, a, r, "rs_sum_" + nm) for a, r, nm in zip(cp32s, recvs, names)]

    ffn_names, mix_names = ["w_up", "w_down"], ["w_pa", "w_pb", "w_o"]
    part_ffn = [dw_up, dw_down.reshape(N_CHIPS, -1, D_MODEL)]
    (do_raw, dpc, dw_pa, dw_pb, dw_o, dgna, dlng, dlnb, dws, dbst, dmod_mix), sib_ffn = _mix_bwd(
        p, o_f, o_b, dx1, mod_c, gna, lng, lnb, ws3, ws3_t, bst, w_pa_f, w_pb_f, w_o_f,
        carry=_carry_sibling_halves(part_ffn))
    cp32_ffn, cpbf_ffn = add_halves(part_ffn, sib_ffn, ffn_names)
    part_mix = [shard_major(dw_pa), shard_major(dw_pb), dw_o.reshape(N_CHIPS, -1, D_MODEL)]
    (df_f, dq0, dv0, dlb0), got = _hgrn_bwd(
        p, lbg_full, s_f, do_raw, 0,
        carry=_merge_carries(_carry_to_owner(cpbf_ffn), _carry_sibling_halves(part_mix)))
    own_ffn, sib_mix = got[:2], got[2:]
    half_ffn = sum_owner(cp32_ffn, own_ffn, ffn_names)
    cp32_mix, cpbf_mix = add_halves(part_mix, sib_mix, mix_names)
    (df_b, dq, di, dlb1), got = _hgrn_bwd(
        p, lbg_full, s_b, do_raw, 1, dq0, dv0,
        carry=_merge_carries(_carry_join_halves(half_ffn), _carry_to_owner(cpbf_mix)))
    g_ffn_w, own_mix = got[:2], got[2:]
    half_mix = sum_owner(cp32_mix, own_mix, mix_names)
    (grad_x, dp, dg_mix, dmod_in), g_mix_w = _in_bwd(
        x, ctx, dx1, g_mix, mod_a, w_in_f, df_f, df_b, di, dq, dpc, carry=_carry_join_halves(half_mix))

    rows_all = dp.shape[0] * dp.shape[1]
    tk_all = 2 * ROW_BLOCK if rows_all % (2 * ROW_BLOCK) == 0 else ROW_BLOCK
    dw_in = _matmul_tn(h_all.reshape(rows_all, D_MODEL), dp.reshape(rows_all, IN_COLS), N_CHIPS, tk_all, "dw_in")
    sib_in = _comm_call("rs_sibling_w_in", _carry_sibling_halves([dw_in]))
    cp32_in, cpbf_in = add_halves([dw_in], sib_in, ["w_in"])
    own_in = _comm_call("rs_owner_w_in", _carry_to_owner(cpbf_in))
    g_in_w = _comm_call("rs_join_w_in", _carry_join_halves(sum_owner(cp32_in, own_in, ["w_in"])))

    big_names = ["w_in", "w_up", "w_pa", "w_pb", "w_o", "w_down"]
    g_big = [g_in_w[0], g_ffn_w[0], g_mix_w[0], g_mix_w[1], g_mix_w[2], g_ffn_w[1]]
    big_m = [m_w_in, m_w_up, m_w_pa, m_w_pb, m_w_o, m_w_down]
    big_v = [v_w_in, v_w_up, v_w_pa, v_w_pb, v_w_o, v_w_down]
    res = {}
    for name, w, g, m, v in zip(big_names, shards, g_big, big_m, big_v):
        d, m2, v2 = _adamw_big(w, g, m[0], v[0], "adamw_" + name)
        res[name] = (g[None], d[None], m2[None], v2[None])

    dmod_mine = jnp.concatenate([dmod_in[:, 0], dmod_in[:, 1], dmod_mix[:, 0], dmod_ffn[:, 1], dmod_ffn[:, 2],
                                 dmod_ffn[:, 3]], axis=1)
    dmc = jnp.concatenate([jnp.sum(dmod_in[:, 2], axis=0), jnp.sum(dmod_in[:, 3], axis=0),
                           jnp.zeros((4 * D_MODEL,), F32)])[None, :]
    pay3 = jnp.concatenate([dmod_mine, jnp.zeros((4 - bs, 6 * D_MODEL), F32), dmc,
                            jnp.zeros((3, 6 * D_MODEL), F32)], axis=0)
    dmod64 = _all_gather8(pay3, "gather_dmod")
    dmod64_my = lax.dynamic_slice(dmod64, (0, kc * mod_cols), (64, mod_cols))
    g_w_mod, g_b_mod, g_cctx_part = _mod_bwd(cond64, dmod64, dmod64_my, w_mod[0], c_ctx[None, :])
    d, m2, v2 = _adamw_big(w_mod[0], g_w_mod, m_w_mod[0], v_w_mod[0], "adamw_w_mod")
    res["w_mod"] = (g_w_mod[None], d[None], m2[None], v2[None])

    def row(*parts):
        return _pad_cols(jnp.concatenate([q.reshape(1, -1) for q in parts], axis=1), D_MODEL)

    small_rows = [dg_mix, dg_ffn, dg_final, g_cctx_part, row(dgna), row(dlng, dlnb), row(jnp.transpose(dbst)),
                  row(dlb0, dlb1), row(loss_part), jnp.zeros((7, D_MODEL), F32), dws.reshape(64, D_MODEL)]
    pay4 = jnp.concatenate(small_rows, axis=0)
    tot, dgam0 = _small_reduce(_all_gather8(pay4, "gather_small"), lbg_full)
    loss = tot[8, 0]
    dgam_full = jnp.stack([dgam0, -dgam0])
    g_lbg = lax.dynamic_slice(dgam_full, (0, 0, kc * HEAD_DIM), (2, 2, HEAD_DIM))

    small = [
        ("c_ctx", c_ctx[None, :], tot[3:4], m_c_ctx, v_c_ctx),
        ("b_mod", b_mod, g_b_mod, m_b_mod, v_b_mod),
        ("g_mix", g_mix, tot[0:1], m_g_mix, v_g_mix),
        ("g_ffn", g_ffn, tot[1:2], m_g_ffn, v_g_ffn),
        ("lb_gamma", lb_gamma.reshape(4, HEAD_DIM), g_lbg.reshape(4, HEAD_DIM), m_lb_gamma, v_lb_gamma),
        ("g_norm_a", g_norm_a, tot[4:5, :HEAD_DIM], m_g_norm_a, v_g_norm_a),
        ("ln_v_g", ln_v_g, tot[5:6, :KW], m_ln_v_g, v_ln_v_g),
        ("ln_v_b", ln_v_b, tot[5:6, KW:], m_ln_v_b, v_ln_v_b),
        ("w_s", w_s.reshape(N_HEADS * SGU_CHUNK, SGU_CHUNK), tot[16:80].reshape(N_HEADS * SGU_CHUNK, SGU_CHUNK),
         m_w_s, v_w_s),
        ("b_s", b_s[0], tot[6:7, :KW].reshape(N_HEADS, SGU_CHUNK), m_b_s, v_b_s),
        ("g_final", g_final[None, :], tot[2:3], m_g_final, v_g_final),
    ]
    ws_, gs_ = [s[1] for s in small], [s[2] for s in small]
    ms_ = [s[3].reshape(s[1].shape) for s in small]
    vs_ = [s[4].reshape(s[1].shape) for s in small]
    ds_, m2s_, v2s_ = _adamw_small(ws_, gs_, ms_, vs_)
    for (name, _, g, m, _), d, m2, v2 in zip(small, ds_, m2s_, v2s_):
        res[name] = tuple(t.reshape(m.shape) for t in (g, d, m2, v2))

    order = ["c_ctx", "w_mod", "b_mod", "g_mix", "g_ffn", "w_in", "lb_gamma", "g_norm_a", "ln_v_g", "ln_v_b",
             "w_s", "b_s", "w_pa", "w_pb", "w_o", "w_up", "w_down", "g_final"]
    outs = [loss, grad_x]
    for part in range(4):
        outs += [res[n][part] for n in order]
    return tuple(outs)
```

```python
import functools
import math

import jax
import jax.numpy as jnp
import numpy as np
from jax import lax
from jax.experimental import pallas as pl
from jax.experimental.pallas import tpu as pltpu

F32 = jnp.float32
BF16 = jnp.bfloat16
SDS = jax.ShapeDtypeStruct
MESH = pl.DeviceIdType.MESH

EPS = 1e-6
D_MODEL = 1024
N_HEADS = 4
HEAD_DIM = 128
KW = N_HEADS * HEAD_DIM
IN_COLS = 11 * KW
D_FF = 2816
HGRN_CHUNK = 64
SGU_CHUNK = 128
ROW_BLOCK = 256
N_CHIPS = 4
N_DEV = 8
V7X_VMEM_BYTES = 64 * 1024 * 1024
VMEM_LIMIT = V7X_VMEM_BYTES - 6 * 1024 * 1024

ADAM_LR, ADAM_B1, ADAM_B2, ADAM_EPS, ADAM_WD, ADAM_STEP = 0.001, 0.9, 0.999, 1e-08, 0.01, 10
GELU_C0 = math.sqrt(2.0 / math.pi)
GELU_C1 = 0.044715

VMEM_SPEC = pl.BlockSpec(memory_space=pltpu.VMEM)
ANY_SPEC = pl.BlockSpec(memory_space=pl.ANY)


def _params(n_grid):
    return pltpu.CompilerParams(dimension_semantics=("arbitrary",) * n_grid, vmem_limit_bytes=VMEM_LIMIT)


def _sig(x):
    return 0.5 * jnp.tanh(0.5 * x) + 0.5


def _gelu(x):
    t = jnp.tanh(GELU_C0 * (x + GELU_C1 * x * x * x))
    return 0.5 * x * (1.0 + t), t


def _dgelu(x, t):
    return 0.5 * (1.0 + t) + 0.5 * x * (1.0 - t * t) * GELU_C0 * (1.0 + 3.0 * GELU_C1 * x * x)


def _dot(a, b):
    return jnp.dot(a.astype(BF16), b.astype(BF16), preferred_element_type=F32)


def _dot_nt(a, b):
    return lax.dot_general(a.astype(BF16), b.astype(BF16), (((1,), (1,)), ((), ())), preferred_element_type=F32)


def _dot_tn(a, b):
    return lax.dot_general(a.astype(BF16), b.astype(BF16), (((0,), (0,)), ((), ())), preferred_element_type=F32)


def _dot_f32(a, b, dims=(((1,), (0,)), ((), ()))):
    return lax.dot_general(a, b, dims, precision=lax.Precision.HIGHEST, preferred_element_type=F32)


def _rms(x):
    r = lax.rsqrt(jnp.mean(x * x, axis=-1, keepdims=True) + EPS)
    return x * r, r


def _rms_bwd(dxn, xn, r):
    return r * (dxn - xn * jnp.mean(dxn * xn, axis=-1, keepdims=True))


def _colsum(a):
    return jnp.sum(a, axis=0, keepdims=True)


def _tri(n, upper):
    t = lax.broadcasted_iota(jnp.int32, (n, n), 0)
    s = lax.broadcasted_iota(jnp.int32, (n, n), 1)
    return (s >= t) if upper else (s <= t)


def _all_gather8(x_shard, name):
    m_per, n = x_shard.shape

    def body(x_ref, out_ref, send_sems, recv_sems, local_sem):
        x, y, c = lax.axis_index("x"), lax.axis_index("y"), lax.axis_index("c")
        me, sibling = (x, y, c), (x, y, 1 - c)
        chips = [(1 - x, y), (x, 1 - y), (1 - x, 1 - y)]

        def rows(px, py, pc):
            return out_ref.at[pl.ds((4 * px + 2 * py + pc) * m_per, m_per), :]

        def copy(k, block, to, src=None):
            return pltpu.make_async_remote_copy(
                src_ref=rows(*block) if src is None else src, dst_ref=rows(*block),
                send_sem=send_sems.at[k], recv_sem=recv_sems.at[k], device_id=to, device_id_type=MESH)

        mine = pltpu.make_async_copy(x_ref, rows(*me), local_sem)
        mine.start()
        first = [copy(0, me, sibling, src=x_ref)]
        first += [copy(1 + j, me, (*chip, c), src=x_ref) for j, chip in enumerate(chips)]
        for cp in first:
            cp.start()
        passed = [copy(4 + j, (*chip, c), sibling) for j, chip in enumerate(chips)]
        for j, chip in enumerate(chips):
            copy(1 + j, (*chip, c), me).wait_recv()
            passed[j].start()
        copy(0, sibling, me).wait_recv()
        for j, chip in enumerate(chips):
            copy(4 + j, (*chip, 1 - c), me).wait_recv()
        for cp in first + passed:
            cp.wait_send()
        mine.wait()

    return pl.pallas_call(
        body, name=name, out_shape=SDS((N_DEV * m_per, n), x_shard.dtype),
        in_specs=[VMEM_SPEC], out_specs=VMEM_SPEC,
        scratch_shapes=[pltpu.SemaphoreType.DMA((7,)), pltpu.SemaphoreType.DMA((7,)), pltpu.SemaphoreType.DMA],
    )(x_shard)


def _mesh_pos():
    x, y, c = lax.axis_index("x"), lax.axis_index("y"), lax.axis_index("c")
    chips = [(1 - x, y), (x, 1 - y), (1 - x, 1 - y)]
    return x, y, c, 2 * x + y, (x, y, 1 - c), chips


def _half_rows(c, rh):
    return pl.ds(pl.multiple_of(c * rh, 16), rh)


def _gather_weights(bufs):
    n = len(bufs)

    def body(*refs):
        outs = refs[n:2 * n]
        send_sems, recv_sems = refs[2 * n:]
        x, y, c, kc, sibling, chips = _mesh_pos()
        firsts, passed = [], []
        for wi in range(n):
            rh = outs[wi].shape[1] // 2
            for jj, chip in enumerate(chips):
                mine = outs[wi].at[kc, _half_rows(c, rh), :]
                cp = pltpu.make_async_remote_copy(
                    src_ref=mine, dst_ref=mine, send_sem=send_sems.at[wi, jj], recv_sem=recv_sems.at[wi, jj],
                    device_id=(*chip, c), device_id_type=MESH)
                cp.start()
                firsts.append(cp)
        for wi in range(n):
            rh = outs[wi].shape[1] // 2
            for jj, chip in enumerate(chips):
                blk = outs[wi].at[2 * chip[0] + chip[1], _half_rows(c, rh), :]
                pltpu.make_async_remote_copy(
                    src_ref=blk, dst_ref=blk, send_sem=send_sems.at[wi, jj], recv_sem=recv_sems.at[wi, jj],
                    device_id=(*chip, c), device_id_type=MESH).wait_recv()
                fw = pltpu.make_async_remote_copy(
                    src_ref=blk, dst_ref=blk, send_sem=send_sems.at[wi, 3 + jj], recv_sem=recv_sems.at[wi, 3 + jj],
                    device_id=sibling, device_id_type=MESH)
                fw.start()
                passed.append(fw)
        for wi in range(n):
            rh = outs[wi].shape[1] // 2
            for jj, chip in enumerate(chips):
                blk = outs[wi].at[2 * chip[0] + chip[1], _half_rows(1 - c, rh), :]
                pltpu.make_async_remote_copy(
                    src_ref=blk, dst_ref=blk, send_sem=send_sems.at[wi, 3 + jj], recv_sem=recv_sems.at[wi, 3 + jj],
                    device_id=sibling, device_id_type=MESH).wait_recv()
        for cp in firsts + passed:
            cp.wait_send()

    return pl.pallas_call(
        body, name="gather_weights",
        out_shape=[SDS(b.shape, b.dtype) for b in bufs],
        in_specs=[ANY_SPEC] * n, out_specs=[ANY_SPEC] * n,
        input_output_aliases={i: i for i in range(n)},
        scratch_shapes=[pltpu.SemaphoreType.DMA((n, 6)), pltpu.SemaphoreType.DMA((n, 6))],
    )(*bufs)


class _Carry:
    def __init__(self, ins, outs, alias, sems, copies):
        self.ins, self.outs, self.alias, self.sems, self.copies = list(ins), list(outs), dict(alias), list(sems), copies


def _remote(src, dst, send, recv, to):
    return functools.partial(pltpu.make_async_remote_copy, src_ref=src, dst_ref=dst, send_sem=send, recv_sem=recv,
                             device_id=to, device_id_type=MESH)


def _carry_gather_send(bufs):
    n = len(bufs)

    def copies(ins, outs, sems):
        x, y, c, kc, sibling, chips = _mesh_pos()
        starts, waits = [], []
        for wi in range(n):
            rh = outs[wi].shape[1] // 2
            for jj, chip in enumerate(chips):
                mine = outs[wi].at[kc, _half_rows(c, rh), :]
                cp = _remote(mine, mine, sems[0].at[wi, jj], sems[1].at[wi, jj], (*chip, c))
                starts.append(cp)
                waits.append((cp, "send"))
                theirs = outs[wi].at[2 * chip[0] + chip[1], _half_rows(c, rh), :]
                waits.append((_remote(theirs, theirs, sems[0].at[wi, jj], sems[1].at[wi, jj], (*chip, c)), "recv"))
        return starts, waits

    return _Carry(bufs, [SDS(b.shape, b.dtype) for b in bufs], {i: i for i in range(n)},
                  [pltpu.SemaphoreType.DMA((n, 3)), pltpu.SemaphoreType.DMA((n, 3))], copies)


def _carry_gather_forward(bufs):
    n = len(bufs)

    def copies(ins, outs, sems):
        x, y, c, kc, sibling, chips = _mesh_pos()
        starts, waits = [], []
        for wi in range(n):
            rh = outs[wi].shape[1] // 2
            for jj, chip in enumerate(chips):
                got = outs[wi].at[2 * chip[0] + chip[1], _half_rows(c, rh), :]
                cp = _remote(got, got, sems[0].at[wi, jj], sems[1].at[wi, jj], sibling)
                starts.append(cp)
                waits.append((cp, "send"))
                other = outs[wi].at[2 * chip[0] + chip[1], _half_rows(1 - c, rh), :]
                waits.append((_remote(other, other, sems[0].at[wi, jj], sems[1].at[wi, jj], sibling), "recv"))
        return starts, waits

    return _Carry(bufs, [SDS(b.shape, b.dtype) for b in bufs], {i: i for i in range(n)},
                  [pltpu.SemaphoreType.DMA((n, 3)), pltpu.SemaphoreType.DMA((n, 3))], copies)


def _carry_sibling_halves(grads):
    n = len(grads)

    def copies(ins, outs, sems):
        x, y, c, kc, sibling, chips = _mesh_pos()
        cps = [_remote(ins[wi].at[:, _half_rows(1 - c, ins[wi].shape[1] // 2), :], outs[wi],
                       sems[0].at[wi], sems[1].at[wi], sibling) for wi in range(n)]
        return cps, [(cp, "both") for cp in cps]

    return _Carry(grads, [SDS((N_CHIPS, g.shape[1] // 2, g.shape[2]), F32) for g in grads], {},
                  [pltpu.SemaphoreType.DMA((n,)), pltpu.SemaphoreType.DMA((n,))], copies)


def _carry_to_owner(cpbfs):
    n = len(cpbfs)

    def copies(ins, outs, sems):
        x, y, c, kc, sibling, chips = _mesh_pos()
        starts, waits = [], []
        for wi in range(n):
            for jj, chip in enumerate(chips):
                cp = _remote(ins[wi].at[2 * chip[0] + chip[1]], outs[wi].at[kc],
                             sems[0].at[wi, jj], sems[1].at[wi, jj], (*chip, c))
                starts.append(cp)
                waits.append((cp, "send"))
                slot = outs[wi].at[2 * chip[0] + chip[1]]
                waits.append((_remote(slot, slot, sems[0].at[wi, jj], sems[1].at[wi, jj], (*chip, c)), "recv"))
        return starts, waits

    return _Carry(cpbfs, [SDS(g.shape, BF16) for g in cpbfs], {},
                  [pltpu.SemaphoreType.DMA((n, 3)), pltpu.SemaphoreType.DMA((n, 3))], copies)


def _carry_join_halves(bufs):
    n = len(bufs)

    def copies(ins, outs, sems):
        x, y, c, kc, sibling, chips = _mesh_pos()
        cps = []
        for wi in range(n):
            mine = outs[wi].at[_half_rows(c, outs[wi].shape[0] // 2), :]
            cps.append(_remote(mine, mine, sems[0].at[wi], sems[1].at[wi], sibling))
        return cps, [(cp, "both") for cp in cps]

    return _Carry(bufs, [SDS(b.shape, F32) for b in bufs], {i: i for i in range(n)},
                  [pltpu.SemaphoreType.DMA((n,)), pltpu.SemaphoreType.DMA((n,))], copies)


def _merge_carries(*carries):
    ins, outs, alias, sems, parts = [], [], {}, [], []
    for cy in carries:
        parts.append((len(ins), len(cy.ins), len(outs), len(cy.outs), len(sems), len(cy.sems), cy.copies))
        alias.update({len(ins) + i: len(outs) + o for i, o in cy.alias.items()})
        ins += cy.ins
        outs += cy.outs
        sems += cy.sems

    def copies(i, o, s):
        starts, waits = [], []
        for i0, ni, o0, no, s0, ns, fn in parts:
            st, wt = fn(i[i0:i0 + ni], o[o0:o0 + no], s[s0:s0 + ns])
            starts += st
            waits += wt
        return starts, waits

    return _Carry(ins, outs, alias, sems, copies)


def _start_all(starts):
    for cp in starts:
        cp().start()


def _wait_all(waits):
    for cp, which in waits:
        if which == "send":
            cp().wait_send()
        elif which == "recv":
            cp().wait_recv()
        else:
            cp().wait()


def _comm_call(name, carry):
    n_i, n_o = len(carry.ins), len(carry.outs)

    def body(*refs):
        ins, outs, sems = refs[:n_i], refs[n_i:n_i + n_o], refs[n_i + n_o:]
        _start_all(carry.copies(ins, outs, sems)[0])
        _wait_all(carry.copies(ins, outs, sems)[1])

    return pl.pallas_call(
        body, name=name, out_shape=carry.outs, in_specs=[ANY_SPEC] * n_i, out_specs=[ANY_SPEC] * n_o,
        input_output_aliases=carry.alias, scratch_shapes=carry.sems,
    )(*carry.ins)


def _host_call(body, *, name, grid, in_specs, out_specs, out_shape, args, scratch_shapes=(), carry=None):
    n_in, n_out, n_scr = len(in_specs), len(out_specs), len(scratch_shapes)
    if carry is None:
        res = pl.pallas_call(body, name=name, grid=grid, in_specs=in_specs, out_specs=out_specs, out_shape=out_shape,
                             scratch_shapes=list(scratch_shapes), compiler_params=_params(len(grid)))(*args)
        return list(res), []
    n_ci, n_co = len(carry.ins), len(carry.outs)

    def wrapped(*refs):
        ins, cins = refs[:n_in], refs[n_in:n_in + n_ci]
        o0 = n_in + n_ci
        outs, couts = refs[o0:o0 + n_out], refs[o0 + n_out:o0 + n_out + n_co]
        s0 = o0 + n_out + n_co
        scr, sems = refs[s0:s0 + n_scr], refs[s0 + n_scr:]
        idx = [pl.program_id(a) for a in range(len(grid))]
        first = functools.reduce(jnp.logical_and, [i == 0 for i in idx])
        last = functools.reduce(jnp.logical_and, [i == g - 1 for i, g in zip(idx, grid)])

        @pl.when(first)
        def _():
            _start_all(carry.copies(cins, couts, sems)[0])

        body(*ins, *outs, *scr)

        @pl.when(last)
        def _():
            _wait_all(carry.copies(cins, couts, sems)[1])

    res = pl.pallas_call(
        wrapped, name=name, grid=grid, in_specs=list(in_specs) + [ANY_SPEC] * n_ci,
        out_specs=list(out_specs) + [ANY_SPEC] * n_co, out_shape=list(out_shape) + carry.outs,
        scratch_shapes=list(scratch_shapes) + carry.sems,
        input_output_aliases={n_in + i: n_out + o for i, o in carry.alias.items()},
        compiler_params=_params(len(grid)),
    )(*args, *carry.ins)
    return list(res[:n_out]), list(res[n_out:])


def _rs_add_halves(pos, grad, recv, name):
    _, rs, cs = grad.shape
    rh = rs // 2
    rb = rh // 2

    def body(pos_ref, g_ref, r_ref, o32_ref, obf_ref):
        s = g_ref[...] + r_ref[...]
        o32_ref[...] = s
        obf_ref[...] = s.astype(BF16)

    blk = (1, rb, cs)
    return pl.pallas_call(
        body, name=name,
        grid_spec=pltpu.PrefetchScalarGridSpec(
            num_scalar_prefetch=1, grid=(N_CHIPS, 2),
            in_specs=[pl.BlockSpec(blk, lambda k, i, p: (k, p[1] * 2 + i, 0)),
                      pl.BlockSpec(blk, lambda k, i, p: (k, i, 0))],
            out_specs=[pl.BlockSpec(blk, lambda k, i, p: (k, i, 0)), pl.BlockSpec(blk, lambda k, i, p: (k, i, 0))]),
        out_shape=[SDS((N_CHIPS, rh, cs), F32), SDS((N_CHIPS, rh, cs), BF16)],
        compiler_params=_params(2),
    )(pos, grad, recv)


def _rs_sum_owner(pos, cp32, recv3, name):
    _, rh, cs = cp32.shape
    rb = rh // 2

    def body(pos_ref, own_ref, r1_ref, r2_ref, r3_ref, o_ref):
        o_ref[...] = ((own_ref[0] + r1_ref[0].astype(F32)) + r2_ref[0].astype(F32)) + r3_ref[0].astype(F32)

    blk = (1, rb, cs)

    def slot(d):
        return pl.BlockSpec(blk, lambda i, p: ((p[0] + d) % N_CHIPS, i, 0))

    return pl.pallas_call(
        body, name=name,
        grid_spec=pltpu.PrefetchScalarGridSpec(
            num_scalar_prefetch=1, grid=(2,),
            in_specs=[slot(0), slot(1), slot(2), slot(3)],
            out_specs=pl.BlockSpec((rb, cs), lambda i, p: (p[1] * 2 + i, 0))),
        out_shape=SDS((2 * rh, cs), F32),
        compiler_params=_params(1),
    )(pos, cp32, recv3, recv3, recv3)


def _cast_bf16(pos, arrs):
    n = len(arrs)

    def body(pos_ref, *refs):
        for i in range(n):
            refs[n + i][0] = refs[i][...].astype(BF16)

    return pl.pallas_call(
        body, name="cast_bf16",
        grid_spec=pltpu.PrefetchScalarGridSpec(
            num_scalar_prefetch=1, grid=(2,),
            in_specs=[pl.BlockSpec((a.shape[0] // 2, a.shape[1]), lambda i, p: (i, 0)) for a in arrs],
            out_specs=[pl.BlockSpec((1, a.shape[0] // 2, a.shape[1]), lambda i, p: (p[0], i, 0)) for a in arrs]),
        out_shape=[SDS((N_CHIPS,) + a.shape, BF16) for a in arrs],
        compiler_params=_params(1),
    )(pos, *arrs)


def _adamw_vals(w, g, m, v):
    m2 = ADAM_B1 * m + (1.0 - ADAM_B1) * g
    v2 = ADAM_B2 * v + (1.0 - ADAM_B2) * (g * g)
    m_hat = m2 / (1.0 - ADAM_B1 ** ADAM_STEP)
    v_hat = v2 / (1.0 - ADAM_B2 ** ADAM_STEP)
    delta = -ADAM_LR * (m_hat / (jnp.sqrt(v_hat) + ADAM_EPS) + ADAM_WD * w)
    return delta, m2, v2


def _adamw_big(w, g, m, v, name):
    rows, cols = w.shape
    rb = rows // 4

    def body(w_ref, g_ref, m_ref, v_ref, d_ref, m2_ref, v2_ref):
        d, m2, v2 = _adamw_vals(w_ref[...], g_ref[...], m_ref[...], v_ref[...])
        d_ref[...] = d
        m2_ref[...] = m2
        v2_ref[...] = v2

    spec = pl.BlockSpec((rb, cols), lambda i: (i, 0))
    return pl.pallas_call(
        body, name=name, grid=(4,), in_specs=[spec] * 4, out_specs=[spec] * 3,
        out_shape=[SDS(w.shape, F32)] * 3, compiler_params=_params(1),
    )(w, g, m, v)


def _adamw_small(ws, gs, ms, vs):
    n = len(ws)

    def body(*refs):
        for i in range(n):
            d, m2, v2 = _adamw_vals(refs[i][...], refs[n + i][...], refs[2 * n + i][...], refs[3 * n + i][...])
            refs[4 * n + i][...] = d
            refs[5 * n + i][...] = m2
            refs[6 * n + i][...] = v2

    shapes = [SDS(w.shape, F32) for w in ws]
    outs = pl.pallas_call(
        body, name="adamw_small", out_shape=shapes * 3,
        in_specs=[VMEM_SPEC] * (4 * n), out_specs=[VMEM_SPEC] * (3 * n),
    )(*ws, *gs, *ms, *vs)
    return outs[:n], outs[n:2 * n], outs[2 * n:]


def _mod_fwd(cond64, w_mod_s, b_mod_s):
    def body(c_ref, w_ref, b_ref, o_ref):
        cc = c_ref[...]
        o_ref[...] = _dot_f32(cc * _sig(cc), w_ref[...]) + b_ref[...]

    return pl.pallas_call(
        body, name="mod_fwd", out_shape=SDS((cond64.shape[0], w_mod_s.shape[1]), F32),
        in_specs=[VMEM_SPEC] * 3, out_specs=VMEM_SPEC,
        compiler_params=pltpu.CompilerParams(vmem_limit_bytes=VMEM_LIMIT),
    )(cond64, w_mod_s, b_mod_s)


def _mod_bwd(cond64, dmod64, dmod64_my, w_mod_s, c_ctx):
    def body(c_ref, g_ref, gm_ref, w_ref, cc_ref, gw_ref, gb_ref, gcc_ref):
        cc = c_ref[...]
        act = cc * _sig(cc)
        gm = gm_ref[...]
        gw_ref[...] = _dot_f32(act, gm, (((0,), (0,)), ((), ())))
        gb_ref[...] = _colsum(g_ref[...])
        dact = _dot_f32(gm, w_ref[...], (((1,), (1,)), ((), ())))
        tot = dact[4:5, :]
        for dev in range(1, N_DEV):
            tot = tot + dact[8 * dev + 4:8 * dev + 5, :]
        c0 = cc_ref[...]
        s0 = _sig(c0)
        gcc_ref[...] = tot * (s0 * (1.0 + c0 * (1.0 - s0)))

    return pl.pallas_call(
        body, name="mod_bwd",
        out_shape=[SDS(w_mod_s.shape, F32), SDS((1, dmod64.shape[1]), F32), SDS((1, D_MODEL), F32)],
        in_specs=[VMEM_SPEC] * 5, out_specs=[VMEM_SPEC] * 3,
        compiler_params=pltpu.CompilerParams(vmem_limit_bytes=VMEM_LIMIT),
    )(cond64, dmod64, dmod64_my, w_mod_s, c_ctx)


def _in_fwd(x, ctx, g_mix, mod_a, w_in, carry=None):
    bs, seq, _ = x.shape
    nb = seq // ROW_BLOCK + 1

    def body(x_ref, ctx_ref, g_ref, mod_ref, w_ref, p_ref, h_ref):
        is_ctx = pl.program_id(1) == 0
        xin = jnp.where(is_ctx, ctx_ref[0], x_ref[0])
        shift = jnp.where(is_ctx, mod_ref[0, 2:3, :], mod_ref[0, 0:1, :])
        scale = jnp.where(is_ctx, mod_ref[0, 3:4, :], mod_ref[0, 1:2, :])
        xn, _ = _rms(xin)
        hb = ((xn * g_ref[...]) * (1.0 + scale) + shift).astype(BF16)
        h_ref[0] = hb
        p_ref[0] = jnp.dot(hb, w_ref[...], preferred_element_type=F32)

    return _host_call(
        body, name="in_fwd", grid=(bs, nb),
        in_specs=[pl.BlockSpec((1, ROW_BLOCK, D_MODEL), lambda b, j: (b, jnp.maximum(j - 1, 0), 0)),
                  pl.BlockSpec((1, ROW_BLOCK, D_MODEL), lambda b, j: (b, 0, 0)),
                  pl.BlockSpec((1, D_MODEL), lambda b, j: (0, 0)),
                  pl.BlockSpec((1, 8, D_MODEL), lambda b, j: (b, 0, 0)),
                  pl.BlockSpec((D_MODEL, IN_COLS), lambda b, j: (0, 0))],
        out_specs=[pl.BlockSpec((1, ROW_BLOCK, IN_COLS), lambda b, j: (b, j, 0)),
                   pl.BlockSpec((1, ROW_BLOCK, D_MODEL), lambda b, j: (b, j, 0))],
        out_shape=[SDS((bs, nb * ROW_BLOCK, IN_COLS), F32), SDS((bs, nb * ROW_BLOCK, D_MODEL), BF16)],
        args=(x, ctx, g_mix, mod_a, w_in), carry=carry)


def _in_bwd(x, ctx, dx1, g_mix, mod_a, w_in, df_f, df_b, di, dq, dpc, carry=None):
    bs, seq, _ = x.shape
    nb = seq // ROW_BLOCK + 1

    def body(x_ref, ctx_ref, dx1_ref, g_ref, mod_ref, w_ref, dff_ref, dfb_ref, di_ref, dq_ref, dpc_ref,
             gx_ref, dp_ref, dg_ref, dmod_ref):
        b, j = pl.program_id(0), pl.program_id(1)
        is_ctx = j == 0

        @pl.when((b == 0) & (j == 0))
        def _():
            dg_ref[...] = jnp.zeros_like(dg_ref)

        @pl.when(j == 0)
        def _():
            dmod_ref[...] = jnp.zeros_like(dmod_ref)

        dp = jnp.concatenate([dff_ref[0], dfb_ref[0], di_ref[0], dq_ref[0], dpc_ref[0]], axis=1)
        dp_ref[0] = dp
        dh = lax.dot_general(dp, w_ref[...], (((1,), (1,)), ((), ())), preferred_element_type=F32)
        xin = jnp.where(is_ctx, ctx_ref[0], x_ref[0])
        scale = jnp.where(is_ctx, mod_ref[0, 3:4, :], mod_ref[0, 1:2, :])
        xn, r = _rms(xin)
        g = g_ref[...]
        hn = xn * g
        d_shift = _colsum(dh)
        d_scale = _colsum(dh * hn)
        dhn = dh * (1.0 + scale)
        dg_ref[...] += _colsum(dhn * xn)
        dx = _rms_bwd(dhn * g, xn, r)

        @pl.when(is_ctx)
        def _():
            dmod_ref[0, 2:3, :] += d_shift
            dmod_ref[0, 3:4, :] += d_scale

        @pl.when(jnp.logical_not(is_ctx))
        def _():
            dmod_ref[0, 0:1, :] += d_shift
            dmod_ref[0, 1:2, :] += d_scale
            gx_ref[0] = dx + dx1_ref[0]

    def rows(w):
        return pl.BlockSpec((1, ROW_BLOCK, w), lambda b, j: (b, j, 0))

    lat = pl.BlockSpec((1, ROW_BLOCK, D_MODEL), lambda b, j: (b, jnp.maximum(j - 1, 0), 0))
    return _host_call(
        body, name="in_bwd", grid=(bs, nb),
        in_specs=[lat, pl.BlockSpec((1, ROW_BLOCK, D_MODEL), lambda b, j: (b, 0, 0)), lat,
                  pl.BlockSpec((1, D_MODEL), lambda b, j: (0, 0)),
                  pl.BlockSpec((1, 8, D_MODEL), lambda b, j: (b, 0, 0)),
                  pl.BlockSpec((D_MODEL, IN_COLS), lambda b, j: (0, 0)),
                  rows(KW), rows(KW), rows(KW), rows(KW), rows(7 * KW)],
        out_specs=[lat, rows(IN_COLS), pl.BlockSpec((1, D_MODEL), lambda b, j: (0, 0)),
                   pl.BlockSpec((1, 8, D_MODEL), lambda b, j: (b, 0, 0))],
        out_shape=[SDS(x.shape, F32), SDS((bs, nb * ROW_BLOCK, IN_COLS), BF16), SDS((1, D_MODEL), F32),
                   SDS((bs, 8, D_MODEL), F32)],
        args=(x, ctx, dx1, g_mix, mod_a, w_in, df_f, df_b, di, dq, dpc), carry=carry)


def _lower_bound(lbg_ref, direction):
    return _sig(lbg_ref[0, direction:direction + 1, :] - lbg_ref[1, direction:direction + 1, :])


N_CHUNKS = ROW_BLOCK // HGRN_CHUNK


def _block_tri(upper):
    t = np.arange(ROW_BLOCK)[:, None]
    s = np.arange(ROW_BLOCK)[None, :]
    same = (t // HGRN_CHUNK) == (s // HGRN_CHUNK)
    return jnp.asarray(same & ((s >= t) if upper else (s <= t)), dtype=BF16)


TRI_SPEC = pl.BlockSpec((ROW_BLOCK, ROW_BLOCK), lambda b, j: (0, 0))


def _tri_matmul_f32(tri, g):
    g0 = g.astype(BF16)
    r1 = g - g0.astype(F32)
    g1 = r1.astype(BF16)
    g2 = (r1 - g1.astype(F32)).astype(BF16)
    return (jnp.dot(tri, g2, preferred_element_type=F32) + jnp.dot(tri, g1, preferred_element_type=F32)) \
        + jnp.dot(tri, g0, preferred_element_type=F32)


def _chunk_rows(rows):
    return jnp.concatenate([jnp.broadcast_to(r, (HGRN_CHUNK, r.shape[1])) for r in rows], axis=0)


def _block_gates(fl, q, lb, tri, upper):
    t = {}
    t["sg"] = _sig(fl)
    t["f"] = lb + (1.0 - lb) * t["sg"]
    k = 1.0 - t["f"]
    bcum = _tri_matmul_f32(tri, jnp.log(t["f"]))
    ends = [bcum[ci * HGRN_CHUNK:ci * HGRN_CHUNK + 1] if upper else bcum[(ci + 1) * HGRN_CHUNK - 1:(ci + 1) * HGRN_CHUNK]
            for ci in range(N_CHUNKS)]
    mid = _chunk_rows([0.5 * r for r in ends])
    t["dec"] = [jnp.exp(r) for r in ends]
    t["e1"] = jnp.exp(bcum - mid)
    t["e2"] = jnp.exp(mid - bcum)
    t["eh"] = _chunk_rows([jnp.exp(0.5 * r) for r in ends])
    t["qi"] = q * t["e1"]
    t["ki"] = k * t["e2"]
    t["kd"] = t["ki"] * t["eh"]
    t["qe"] = t["qi"] * t["eh"]
    return t


def _hgrn_block_order(direction, nb):
    if direction == 0:
        return lambda j: j
    return lambda j: jnp.where(j == 0, 0, nb - j)


def _hgrn_fwd(p, lbg, direction, carry=None):
    bs, rows, _ = p.shape
    nb = rows // ROW_BLOCK
    ncb = ROW_BLOCK // HGRN_CHUNK
    upper = direction == 1
    order = _hgrn_block_order(direction, nb)

    def body(f_ref, i_ref, q_ref, lbg_ref, tri_ref, o_ref, s_ref, st):
        @pl.when(pl.program_id(1) == 0)
        def _():
            st[...] = jnp.zeros_like(st)

        lb = _lower_bound(lbg_ref, direction)
        mask = _tri(HGRN_CHUNK, upper)
        t = _block_gates(f_ref[0], q_ref[0], lb, tri_ref[...], upper)
        v = i_ref[0]
        chunk = lambda a, ci, h: a[ci * HGRN_CHUNK:(ci + 1) * HGRN_CHUNK, h * HEAD_DIM:(h + 1) * HEAD_DIM]
        intra = [[None] * N_HEADS for _ in range(ncb)]
        ds_loc = [[None] * N_HEADS for _ in range(ncb)]
        for ci in range(ncb):
            for h in range(N_HEADS):
                a = jnp.where(mask, _dot_nt(chunk(t["qi"], ci, h), chunk(t["ki"], ci, h)), 0.0)
                intra[ci][h] = _dot(a, chunk(v, ci, h))
                ds_loc[ci][h] = _dot_tn(chunk(v, ci, h), chunk(t["kd"], ci, h))
        for h in range(N_HEADS):
            ls = slice(h * HEAD_DIM, (h + 1) * HEAD_DIM)
            s = st[h]
            for ci in (reversed(range(ncb)) if upper else range(ncb)):
                s_ref[0, 0, ci, h] = s
                o_ref[0, ci * HGRN_CHUNK:(ci + 1) * HGRN_CHUNK, ls] = intra[ci][h] + _dot_nt(chunk(t["qe"], ci, h), s)
                s = s * t["dec"][ci][:, ls] + ds_loc[ci][h]
            st[h] = s

    def col(cb):
        return pl.BlockSpec((1, ROW_BLOCK, KW), lambda b, j: (b, order(j), cb))

    return _host_call(
        body, name=f"hgrn_fwd{direction}", grid=(bs, nb),
        in_specs=[col(direction), col(2), col(3), pl.BlockSpec((2, 2, KW), lambda b, j: (0, 0, 0)), TRI_SPEC],
        out_specs=[pl.BlockSpec((1, ROW_BLOCK, KW), lambda b, j: (b, order(j), 0)),
                   pl.BlockSpec((1, 1, ncb, N_HEADS, HEAD_DIM, HEAD_DIM), lambda b, j: (b, order(j), 0, 0, 0, 0))],
        out_shape=[SDS((bs, rows, KW), F32), SDS((bs, nb, ncb, N_HEADS, HEAD_DIM, HEAD_DIM), F32)],
        scratch_shapes=[pltpu.VMEM((N_HEADS, HEAD_DIM, HEAD_DIM), F32)],
        args=(p, p, p, lbg, _block_tri(upper)), carry=carry)


def _hgrn_bwd(p, lbg, s_saved, do_raw, direction, dq_prev=None, dv_prev=None, carry=None):
    bs, rows, _ = p.shape
    nb = rows // ROW_BLOCK
    ncb = ROW_BLOCK // HGRN_CHUNK
    upper = direction == 1
    fwd_order = _hgrn_block_order(direction, nb)
    order = lambda j: fwd_order(nb - 1 - j)
    last = dq_prev is not None
    out_dt = BF16 if last else F32

    def body(*refs):
        if last:
            (f_ref, i_ref, q_ref, lbg_ref, tri_ref, trit_ref, s_ref, do_ref, dqp_ref, dvp_ref,
             df_ref, dq_ref, dv_ref, dlb_ref, dst, acc) = refs
        else:
            (f_ref, i_ref, q_ref, lbg_ref, tri_ref, trit_ref, s_ref, do_ref,
             df_ref, dq_ref, dv_ref, dlb_ref, dst, acc) = refs
        b, j = pl.program_id(0), pl.program_id(1)

        @pl.when((b == 0) & (j == 0))
        def _():
            dlb_ref[...] = jnp.zeros_like(dlb_ref)

        @pl.when(j == 0)
        def _():
            dst[...] = jnp.zeros_like(dst)

        lb = _lower_bound(lbg_ref, direction)
        mask = _tri(HGRN_CHUNK, upper)
        mask_t = _tri(HGRN_CHUNK, not upper)
        t = _block_gates(f_ref[0], q_ref[0], lb, tri_ref[...], upper)
        qi, ki, kd, qe = t["qi"], t["ki"], t["kd"], t["qe"]
        v = i_ref[0]
        do = do_ref[0]
        chunk = lambda a, ci, h: a[ci * HGRN_CHUNK:(ci + 1) * HGRN_CHUNK, h * HEAD_DIM:(h + 1) * HEAD_DIM]
        grid2 = lambda: [[None] * N_HEADS for _ in range(ncb)]
        pairs = [(ci, h) for ci in range(ncb) for h in range(N_HEADS)]
        rows_of = lambda ci: slice(ci * HGRN_CHUNK, (ci + 1) * HGRN_CHUNK)
        lanes_of = lambda h: slice(h * HEAD_DIM, (h + 1) * HEAD_DIM)
        a_t, da, da_t, dv_in, ds_loc = (grid2() for _ in range(5))
        for ci, h in pairs:
            a_t[ci][h] = _dot_nt(chunk(ki, ci, h), chunk(qi, ci, h))
        for ci, h in pairs:
            da[ci][h] = _dot_nt(chunk(do, ci, h), chunk(v, ci, h))
        for ci, h in pairs:
            da_t[ci][h] = _dot_nt(chunk(v, ci, h), chunk(do, ci, h))
        for ci, h in pairs:
            acc[3, rows_of(ci), lanes_of(h)] = _dot(chunk(do, ci, h), s_ref[0, 0, ci, h])
        for ci, h in pairs:
            ds_loc[ci][h] = _dot_tn(chunk(do, ci, h), chunk(qe, ci, h))
        for ci, h in pairs:
            acc[0, rows_of(ci), lanes_of(h)] = _dot(jnp.where(mask, da[ci][h], 0.0), chunk(ki, ci, h))
        for ci, h in pairs:
            acc[1, rows_of(ci), lanes_of(h)] = _dot(jnp.where(mask_t, da_t[ci][h], 0.0), chunk(qi, ci, h))
        for ci, h in pairs:
            dv_in[ci][h] = _dot(jnp.where(mask_t, a_t[ci][h], 0.0), chunk(do, ci, h))
        ddec = [[None] * N_HEADS for _ in range(ncb)]
        for h in range(N_HEADS):
            ls = slice(h * HEAD_DIM, (h + 1) * HEAD_DIM)
            ds = dst[h]
            for ci in (range(ncb) if upper else reversed(range(ncb))):
                rs = slice(ci * HGRN_CHUNK, (ci + 1) * HGRN_CHUNK)
                acc[2, rs, ls] = _dot(chunk(v, ci, h), ds)
                acc[4, rs, ls] = dv_in[ci][h] + _dot_nt(chunk(kd, ci, h), ds)
                ddec[ci][h] = _colsum(ds * s_ref[0, 0, ci, h])
                ds = ds * t["dec"][ci][:, ls] + ds_loc[ci][h]
            dst[h] = ds
        dqi, dki, dkd, dqe, dv = (acc[i] for i in range(5))
        dq = t["e1"] * (dqi + dqe * t["eh"])
        dk = t["e2"] * (dki + dkd * t["eh"])
        db = dqi * qi - dki * ki - dkd * kd + dqe * qe
        dkd_kd = dkd * kd
        dbl = [_colsum(dkd_kd[ci * HGRN_CHUNK:(ci + 1) * HGRN_CHUNK]) + jnp.concatenate(ddec[ci], axis=1) * t["dec"][ci]
               for ci in range(ncb)]
        dg = _tri_matmul_f32(trit_ref[...], db) + _chunk_rows(dbl)
        df = dg / t["f"] - dk
        sg = t["sg"]
        dlb_ref[...] += _colsum(df * (1.0 - sg))
        df_ref[0] = (df * (1.0 - lb) * sg * (1.0 - sg)).astype(BF16)
        if last:
            dq_ref[0] = (dq + dqp_ref[0]).astype(out_dt)
            dv_ref[0] = (dv + dvp_ref[0]).astype(out_dt)
        else:
            dq_ref[0] = dq
            dv_ref[0] = dv

    def col(cb):
        return pl.BlockSpec((1, ROW_BLOCK, KW), lambda b, j: (b, order(j), cb))

    row = pl.BlockSpec((1, ROW_BLOCK, KW), lambda b, j: (b, order(j), 0))
    in_specs = [col(direction), col(2), col(3), pl.BlockSpec((2, 2, KW), lambda b, j: (0, 0, 0)), TRI_SPEC, TRI_SPEC,
                pl.BlockSpec((1, 1, ncb, N_HEADS, HEAD_DIM, HEAD_DIM), lambda b, j: (b, order(j), 0, 0, 0, 0)), row]
    args = [p, p, p, lbg, _block_tri(upper), _block_tri(not upper), s_saved, do_raw]
    if last:
        in_specs += [row, row]
        args += [dq_prev, dv_prev]
    return _host_call(
        body, name=f"hgrn_bwd{direction}", grid=(bs, nb), in_specs=in_specs,
        out_specs=[row, row, row, pl.BlockSpec((1, KW), lambda b, j: (0, 0))],
        out_shape=[SDS((bs, rows, KW), BF16), SDS((bs, rows, KW), out_dt), SDS((bs, rows, KW), out_dt),
                   SDS((1, KW), F32)],
        scratch_shapes=[pltpu.VMEM((N_HEADS, HEAD_DIM, HEAD_DIM), F32), pltpu.VMEM((5, ROW_BLOCK, KW), F32)],
        args=args, carry=carry)


def _mix_values(og, u, v, ga, gb, o_raw, gna, lng, lnb, ws_ref, bst, wpa, wpb, wo):
    t = {}
    sog = _sig(og)
    t["sog"], t["silu_og"] = sog, og * sog
    xh_l, r_l = [], []
    for h in range(N_HEADS):
        xh, r = _rms(o_raw[:, h * HEAD_DIM:(h + 1) * HEAD_DIM])
        xh_l.append(xh)
        r_l.append(r)
    t["xh"], t["r"] = jnp.concatenate(xh_l, axis=1), r_l
    gna4 = jnp.concatenate([gna] * N_HEADS, axis=1)
    t["gna4"] = gna4
    t["o_n"] = t["xh"] * gna4
    t["o_a"] = t["o_n"] * t["silu_og"]
    t["gu"], t["tu"] = _gelu(u)
    gv, t["tv"] = _gelu(v)
    mu = jnp.mean(gv, axis=-1, keepdims=True)
    cen = gv - mu
    t["rstd"] = lax.rsqrt(jnp.mean(cen * cen, axis=-1, keepdims=True) + EPS)
    t["xhat"] = cen * t["rstd"]
    vn = t["xhat"] * lng + lnb
    t["vn"] = vn
    chunks = []
    for n in range(ROW_BLOCK // SGU_CHUNK):
        rs = slice(n * SGU_CHUNK, (n + 1) * SGU_CHUNK)
        groups = []
        for g in range(N_HEADS):
            ls = slice(g * HEAD_DIM, (g + 1) * HEAD_DIM)
            groups.append(_dot(ws_ref[g], vn[rs, ls]) + bst[:, g:g + 1])
        chunks.append(jnp.concatenate(groups, axis=1))
    t["mixed"] = jnp.concatenate(chunks, axis=0)
    t["o_bm"] = t["gu"] * t["mixed"]
    t["ya"] = _dot(t["o_a"], wpa)
    t["yb"] = _dot(t["o_bm"], wpb)
    t["sa"], t["sb"] = _sig(ga), _sig(gb)
    t["merged"] = t["sa"] * t["ya"] + t["sb"] * t["yb"]
    t["mix"] = _dot(t["merged"], wo)
    return t


def _mix_in_specs(row_of):
    def col(cb):
        return pl.BlockSpec((1, ROW_BLOCK, KW), lambda b, j: (b, row_of(j), cb))
    return [col(cb) for cb in range(4, 11)]


def _mix_param_specs():
    full2 = lambda r, c: pl.BlockSpec((r, c), lambda b, j: (0, 0))
    return [full2(1, HEAD_DIM), full2(1, KW), full2(1, KW),
            pl.BlockSpec((N_HEADS, SGU_CHUNK, SGU_CHUNK), lambda b, j: (0, 0, 0)),
            full2(SGU_CHUNK, N_HEADS), full2(KW, D_MODEL), full2(KW, D_MODEL), full2(D_MODEL, D_MODEL)]


def _mix_fwd(p, o_f, o_b, x, mod_c, gna, lng, lnb, w_s, bst, wpa, wpb, wo):
    bs, seq, _ = x.shape
    nbl = seq // ROW_BLOCK

    def body(og_r, u_r, v_r, ga0_r, ga1_r, gb0_r, gb1_r, of_r, ob_r, x_r, mod_r,
             gna_r, lng_r, lnb_r, ws_r, bst_r, wpa_r, wpb_r, wo_r, x1_r):
        ga = jnp.concatenate([ga0_r[0], ga1_r[0]], axis=1)
        gb = jnp.concatenate([gb0_r[0], gb1_r[0]], axis=1)
        t = _mix_values(og_r[0], u_r[0], v_r[0], ga, gb, of_r[0] + ob_r[0], gna_r[...], lng_r[...], lnb_r[...],
                        ws_r, bst_r[...], wpa_r[...], wpb_r[...], wo_r[...])
        x1_r[0] = x_r[0] + mod_r[0, 0:1, :] * t["mix"]

    row = lambda w: pl.BlockSpec((1, ROW_BLOCK, w), lambda b, j: (b, j + 1, 0))
    lat = pl.BlockSpec((1, ROW_BLOCK, D_MODEL), lambda b, j: (b, j, 0))
    return pl.pallas_call(
        body, name="mix_fwd", grid=(bs, nbl),
        in_specs=_mix_in_specs(lambda j: j + 1) + [row(KW), row(KW), lat,
                                                    pl.BlockSpec((1, 8, D_MODEL), lambda b, j: (b, 0, 0))]
        + _mix_param_specs(),
        out_specs=lat, out_shape=SDS(x.shape, F32), compiler_params=_params(2),
    )(p, p, p, p, p, p, p, o_f, o_b, x, mod_c, gna, lng, lnb, w_s, bst, wpa, wpb, wo)


def _mix_bwd(p, o_f, o_b, dx1, mod_c, gna, lng, lnb, w_s, w_s_t, bst, wpa, wpb, wo, carry=None):
    bs, rows, _ = p.shape
    nb = rows // ROW_BLOCK

    def body(og_r, u_r, v_r, ga0_r, ga1_r, gb0_r, gb1_r, of_r, ob_r, dx1_r, mod_r,
             gna_r, lng_r, lnb_r, ws_r, bst_r, wpa_r, wpb_r, wo_r, wst_r,
             dor_r, dpc_r, dwpa_r, dwpb_r, dwo_r, dgna_r, dlng_r, dlnb_r, dws_r, dbst_r, dmod_r):
        b, j = pl.program_id(0), pl.program_id(1)

        @pl.when((b == 0) & (j == 0))
        def _():
            for r in (dwpa_r, dwpb_r, dwo_r, dgna_r, dlng_r, dlnb_r, dws_r, dbst_r):
                r[...] = jnp.zeros_like(r)

        @pl.when(j == 0)
        def _():
            dmod_r[...] = jnp.zeros_like(dmod_r)
            dor_r[...] = jnp.zeros_like(dor_r)
            dpc_r[...] = jnp.zeros_like(dpc_r)

        @pl.when(j > 0)
        def _():
            og, u, v = og_r[0], u_r[0], v_r[0]
            ga = jnp.concatenate([ga0_r[0], ga1_r[0]], axis=1)
            gb = jnp.concatenate([gb0_r[0], gb1_r[0]], axis=1)
            gna, lng = gna_r[...], lng_r[...]
            wpa, wpb, wo = wpa_r[...], wpb_r[...], wo_r[...]
            t = _mix_values(og, u, v, ga, gb, of_r[0] + ob_r[0], gna, lng, lnb_r[...],
                            ws_r, bst_r[...], wpa, wpb, wo)
            dx1 = dx1_r[0]
            dmod_r[0, 0:1, :] += _colsum(dx1 * t["mix"])
            dmix = mod_r[0, 0:1, :] * dx1
            dmerged = _dot_nt(dmix, wo)
            dwo_r[...] += _dot_tn(t["merged"], dmix)
            sa, sb = t["sa"], t["sb"]
            dya, dyb = sa * dmerged, sb * dmerged
            dga = dmerged * t["ya"] * sa * (1.0 - sa)
            dgb = dmerged * t["yb"] * sb * (1.0 - sb)
            do_a = _dot_nt(dya, wpa)
            dwpa_r[...] += _dot_tn(t["o_a"], dya)
            do_bm = _dot_nt(dyb, wpb)
            dwpb_r[...] += _dot_tn(t["o_bm"], dyb)
            sog = t["sog"]
            dog = do_a * t["o_n"] * (sog * (1.0 + og * (1.0 - sog)))
            do_n = do_a * t["silu_og"]
            dxh = do_n * t["gna4"]
            prod = do_n * t["xh"]
            dgna = jnp.zeros((1, HEAD_DIM), F32)
            dor_l = []
            for h in range(N_HEADS):
                ls = slice(h * HEAD_DIM, (h + 1) * HEAD_DIM)
                dgna = dgna + _colsum(prod[:, ls])
                dor_l.append(_rms_bwd(dxh[:, ls], t["xh"][:, ls], t["r"][h]))
            dgna_r[...] += dgna
            dor_r[0] = jnp.concatenate(dor_l, axis=1)
            du = do_bm * t["mixed"] * _dgelu(u, t["tu"])
            dmixed = do_bm * t["gu"]
            vn = t["vn"]
            dvn_chunks = []
            for n in range(ROW_BLOCK // SGU_CHUNK):
                rs = slice(n * SGU_CHUNK, (n + 1) * SGU_CHUNK)
                groups = []
                for g in range(N_HEADS):
                    ls = slice(g * HEAD_DIM, (g + 1) * HEAD_DIM)
                    dm = dmixed[rs, ls]
                    dws_r[g] += _dot_nt(dm, vn[rs, ls])
                    dbst_r[:, g:g + 1] += jnp.sum(dm, axis=1, keepdims=True)
                    groups.append(_dot(wst_r[g], dm))
                dvn_chunks.append(jnp.concatenate(groups, axis=1))
            dvn = jnp.concatenate(dvn_chunks, axis=0)
            xhat = t["xhat"]
            dlng_r[...] += _colsum(dvn * xhat)
            dlnb_r[...] += _colsum(dvn)
            dxhat = dvn * lng
            dgv = t["rstd"] * (dxhat - jnp.mean(dxhat, axis=-1, keepdims=True)
                               - xhat * jnp.mean(dxhat * xhat, axis=-1, keepdims=True))
            dv = dgv * _dgelu(v, t["tv"])
            dpc_r[0] = jnp.concatenate([dog, du, dv, dga, dgb], axis=1).astype(BF16)

    row = lambda w: pl.BlockSpec((1, ROW_BLOCK, w), lambda b, j: (b, j, 0))
    lat = pl.BlockSpec((1, ROW_BLOCK, D_MODEL), lambda b, j: (b, jnp.maximum(j - 1, 0), 0))
    full2 = lambda r, c: pl.BlockSpec((r, c), lambda b, j: (0, 0))
    ws_spec = pl.BlockSpec((N_HEADS, SGU_CHUNK, SGU_CHUNK), lambda b, j: (0, 0, 0))
    return _host_call(
        body, name="mix_bwd", grid=(bs, nb),
        in_specs=_mix_in_specs(lambda j: j) + [row(KW), row(KW), lat,
                                                pl.BlockSpec((1, 8, D_MODEL), lambda b, j: (b, 0, 0))]
        + _mix_param_specs() + [ws_spec],
        out_specs=[row(KW), row(7 * KW), full2(KW, D_MODEL), full2(KW, D_MODEL), full2(D_MODEL, D_MODEL),
                   full2(1, HEAD_DIM), full2(1, KW), full2(1, KW), ws_spec, full2(SGU_CHUNK, N_HEADS),
                   pl.BlockSpec((1, 8, D_MODEL), lambda b, j: (b, 0, 0))],
        out_shape=[SDS((bs, rows, KW), F32), SDS((bs, rows, 7 * KW), BF16), SDS((KW, D_MODEL), F32),
                   SDS((KW, D_MODEL), F32), SDS((D_MODEL, D_MODEL), F32), SDS((1, HEAD_DIM), F32),
                   SDS((1, KW), F32), SDS((1, KW), F32), SDS((N_HEADS, SGU_CHUNK, SGU_CHUNK), F32),
                   SDS((SGU_CHUNK, N_HEADS), F32), SDS((bs, 8, D_MODEL), F32)],
        args=(p, p, p, p, p, p, p, o_f, o_b, dx1, mod_c, gna, lng, lnb, w_s, bst, wpa, wpb, wo, w_s_t), carry=carry)


def _ffn(x1, target, mod_c, g_ffn, g_final, w_up, w_down):
    bs, seq, _ = x1.shape
    nbl = seq // ROW_BLOCK

    def body(x1_r, tg_r, mod_r, gf_r, gl_r, wu_r, wd_r,
             dx1_r, h2_r, dab_r, hid_r, dffn_r, loss_r, dgl_r, dgf_r, dmod_r):
        b, j = pl.program_id(0), pl.program_id(1)

        @pl.when((b == 0) & (j == 0))
        def _():
            for r in (loss_r, dgl_r, dgf_r):
                r[...] = jnp.zeros_like(r)

        @pl.when(j == 0)
        def _():
            dmod_r[...] = jnp.zeros_like(dmod_r)

        x1 = x1_r[0]
        shift, scale, gate = mod_r[0, 1:2, :], mod_r[0, 2:3, :], mod_r[0, 3:4, :]
        gf, gl = gf_r[...], gl_r[...]
        xn2, r2 = _rms(x1)
        hn2 = xn2 * gf
        h2 = (hn2 * (1.0 + scale) + shift).astype(BF16)
        h2_r[0] = h2
        ab = jnp.dot(h2, wu_r[...], preferred_element_type=F32)
        a, bb = ab[:, :D_FF], ab[:, D_FF:]
        sa = _sig(a)
        silu_a = a * sa
        hid = (silu_a * bb).astype(BF16)
        hid_r[0] = hid
        ffn = jnp.dot(hid, wd_r[...], preferred_element_type=F32)
        x2 = x1 + gate * ffn
        xn3, r3 = _rms(x2)
        err = xn3 * gl - tg_r[0]
        loss_r[...] += 0.5 * jnp.sum(jnp.mean(err * err, axis=-1, keepdims=True), axis=0, keepdims=True)
        dy = err * (1.0 / D_MODEL)
        dgl_r[...] += _colsum(dy * xn3)
        dx2 = _rms_bwd(dy * gl, xn3, r3)
        dmod_r[0, 3:4, :] += _colsum(dx2 * ffn)
        dffn = (gate * dx2).astype(BF16)
        dffn_r[0] = dffn
        dhid = lax.dot_general(dffn, wd_r[...], (((1,), (1,)), ((), ())), preferred_element_type=F32)
        da = dhid * bb * (sa * (1.0 + a * (1.0 - sa)))
        db = dhid * silu_a
        dab = jnp.concatenate([da, db], axis=1).astype(BF16)
        dab_r[0] = dab
        dh2 = lax.dot_general(dab, wu_r[...], (((1,), (1,)), ((), ())), preferred_element_type=F32)
        dmod_r[0, 1:2, :] += _colsum(dh2)
        dmod_r[0, 2:3, :] += _colsum(dh2 * hn2)
        dhn2 = dh2 * (1.0 + scale)
        dgf_r[...] += _colsum(dhn2 * xn2)
        dx1_r[0] = dx2 + _rms_bwd(dhn2 * gf, xn2, r2)

    lat = lambda w: pl.BlockSpec((1, ROW_BLOCK, w), lambda b, j: (b, j, 0))
    full2 = lambda r, c: pl.BlockSpec((r, c), lambda b, j: (0, 0))
    mod_spec = pl.BlockSpec((1, 8, D_MODEL), lambda b, j: (b, 0, 0))
    return pl.pallas_call(
        body, name="ffn", grid=(bs, nbl),
        in_specs=[lat(D_MODEL), lat(D_MODEL), mod_spec, full2(1, D_MODEL), full2(1, D_MODEL),
                  full2(D_MODEL, 2 * D_FF), full2(D_FF, D_MODEL)],
        out_specs=[lat(D_MODEL), lat(D_MODEL), lat(2 * D_FF), lat(D_FF), lat(D_MODEL),
                   full2(1, 1), full2(1, D_MODEL), full2(1, D_MODEL), mod_spec],
        out_shape=[SDS(x1.shape, F32), SDS(x1.shape, BF16), SDS((bs, seq, 2 * D_FF), BF16),
                   SDS((bs, seq, D_FF), BF16), SDS(x1.shape, BF16), SDS((1, 1), F32),
                   SDS((1, D_MODEL), F32), SDS((1, D_MODEL), F32), SDS((bs, 8, D_MODEL), F32)],
        compiler_params=_params(2),
    )(x1, target, mod_c, g_ffn, g_final, w_up, w_down)


def _matmul_tn(a, b, n_blocks, tk, name, carry=None):
    t, m = a.shape
    n = b.shape[1]
    tn = n // n_blocks

    def body(a_ref, b_ref, o_ref):
        @pl.when(pl.program_id(1) == 0)
        def _():
            o_ref[...] = jnp.zeros_like(o_ref)
        o_ref[0] += _dot_tn(a_ref[...], b_ref[...])

    (out,), carried = _host_call(
        body, name=name, grid=(n_blocks, t // tk),
        in_specs=[pl.BlockSpec((tk, m), lambda i, k: (k, 0)), pl.BlockSpec((tk, tn), lambda i, k: (k, i))],
        out_specs=[pl.BlockSpec((1, m, tn), lambda i, k: (i, 0, 0))],
        out_shape=[SDS((n_blocks, m, tn), F32)], args=(a, b), carry=carry)
    return out if carry is None else (out, carried)


SMALL_ROWS = 80
ROW_CCTX = 3


def _small_reduce(gathered, lbg):
    def body(g_ref, lbg_ref, s_ref, dgam_ref):
        tot = g_ref[0:SMALL_ROWS, :]
        for dev in range(1, N_DEV):
            tot = tot + g_ref[dev * SMALL_ROWS:(dev + 1) * SMALL_ROWS, :]
        s_ref[...] = tot
        cc = g_ref[ROW_CCTX:ROW_CCTX + 1, :]
        for dev in range(2, N_DEV, 2):
            cc = cc + g_ref[dev * SMALL_ROWS + ROW_CCTX:dev * SMALL_ROWS + ROW_CCTX + 1, :]
        s_ref[ROW_CCTX:ROW_CCTX + 1, :] = cc
        dlb = tot[7:8, :]
        for d in range(2):
            s0 = _sig(lbg_ref[0, d:d + 1, :] - lbg_ref[1, d:d + 1, :])
            dgam_ref[d:d + 1, :] = dlb[:, d * KW:(d + 1) * KW] * s0 * (1.0 - s0)

    return pl.pallas_call(
        body, name="small_reduce", out_shape=[SDS((SMALL_ROWS, D_MODEL), F32), SDS((2, KW), F32)],
        in_specs=[VMEM_SPEC] * 2, out_specs=[VMEM_SPEC] * 2,
    )(gathered, lbg)


def _pad_cols(a, width):
    return jnp.pad(a, ((0, 0), (0, width - a.shape[1])))


def kernel(x, c, ctx, c_ctx, w_mod, b_mod, g_mix, g_ffn, w_in, lb_gamma, g_norm_a, ln_v_g, ln_v_b, w_s, b_s, w_pa, w_pb, w_o, w_up, w_down, g_final, loss_target, m_c_ctx, m_w_mod, m_b_mod, m_g_mix, m_g_ffn, m_w_in, m_lb_gamma, m_g_norm_a, m_ln_v_g, m_ln_v_b, m_w_s, m_b_s, m_w_pa, m_w_pb, m_w_o, m_w_up, m_w_down, m_g_final, v_c_ctx, v_w_mod, v_b_mod, v_g_mix, v_g_ffn, v_w_in, v_lb_gamma, v_g_norm_a, v_ln_v_g, v_ln_v_b, v_w_s, v_b_s, v_w_pa, v_w_pb, v_w_o, v_w_up, v_w_down, v_g_final):
    ax, ay, ac = lax.axis_index("x"), lax.axis_index("y"), lax.axis_index("c")
    kc = 2 * ax + ay
    dev = 2 * kc + ac
    pos = jnp.stack([kc, ac]).astype(jnp.int32)
    bs, seq, _ = x.shape
    assert bs <= 4 and ctx.shape[1] == ROW_BLOCK and seq % ROW_BLOCK == 0
    mod_cols = w_mod.shape[2]

    lbg_row = _pad_cols(lb_gamma.reshape(1, -1), D_MODEL)
    pay1 = jnp.concatenate([c, jnp.zeros((4 - bs, D_MODEL), F32), c_ctx[None, :], lbg_row,
                            jnp.zeros((2, D_MODEL), F32)], axis=0)
    cond64 = _all_gather8(pay1, "gather_cond")
    lbg_full = cond64.reshape(N_DEV, 8, D_MODEL)[0::2, 5, :KW].reshape(N_CHIPS, 2, 2, HEAD_DIM)
    lbg_full = jnp.transpose(lbg_full, (1, 2, 0, 3)).reshape(2, 2, KW)

    b_mod_s = lax.dynamic_slice(b_mod, (0, kc * mod_cols), (1, mod_cols))
    mod_s = _mod_fwd(cond64, w_mod[0], b_mod_s)
    mod_g = _all_gather8(mod_s, "gather_mod").reshape(N_DEV, 64, mod_cols)[0::2]
    mod_full = jnp.transpose(mod_g, (1, 0, 2)).reshape(64, N_CHIPS * mod_cols)
    mod_mine = lax.dynamic_slice(mod_full, (dev * 8, 0), (8, 6 * D_MODEL)).reshape(8, 6, D_MODEL)
    mod, mc = mod_mine[:bs], mod_mine[4]
    zeros4 = jnp.zeros((bs, 4, D_MODEL), F32)
    mod_a = jnp.concatenate([mod[:, 0:2], jnp.broadcast_to(mc[None, 0:2], (bs, 2, D_MODEL)), zeros4], axis=1)
    mod_c = jnp.concatenate([mod[:, 2:6], zeros4], axis=1)

    shards = [w_in[0], w_up[0], w_pa[0], w_pb[0], w_o[0], w_down[0]]
    bufs = _cast_bf16(pos, shards)
    (w_in_g,) = _gather_weights(bufs[:1])

    def cols_major(a):
        return jnp.transpose(a, (1, 0, 2)).reshape(a.shape[1], -1)

    gna, lng, lnb = g_norm_a, ln_v_g, ln_v_b
    ws3 = w_s[0]
    ws3_t = jnp.transpose(ws3, (0, 2, 1))
    bst = jnp.transpose(b_s[0])

    w_in_f = cols_major(w_in_g)
    (p, h_all), sent = _in_fwd(x, ctx, g_mix, mod_a, w_in_f, carry=_carry_gather_send(bufs[1:]))
    (o_f, s_f), gathered = _hgrn_fwd(p, lbg_full, 0, carry=_carry_gather_forward(sent))
    (o_b, s_b), _ = _hgrn_fwd(p, lbg_full, 1)
    w_up_f, w_pa_f, w_pb_f = (cols_major(a) for a in gathered[:3])
    w_o_f = gathered[3].reshape(-1, D_MODEL)
    w_down_f = gathered[4].reshape(-1, D_MODEL)
    x1 = _mix_fwd(p, o_f, o_b, x, mod_c, gna, lng, lnb, ws3, bst, w_pa_f, w_pb_f, w_o_f)
    dx1, h2, dab, hid, dffn, loss_part, dg_final, dg_ffn, dmod_ffn = _ffn(
        x1, loss_target, mod_c, g_ffn, g_final[None, :], w_up_f, w_down_f)
    rows_lat = bs * seq
    tk_lat = 2 * ROW_BLOCK if rows_lat % (2 * ROW_BLOCK) == 0 else ROW_BLOCK
    dw_up = _matmul_tn(h2.reshape(rows_lat, D_MODEL), dab.reshape(rows_lat, 2 * D_FF), N_CHIPS, tk_lat, "dw_up")
    dw_down = _matmul_tn(hid.reshape(rows_lat, D_FF), dffn.reshape(rows_lat, D_MODEL), 1, tk_lat, "dw_down")

    def shard_major(a):
        return jnp.transpose(a.reshape(a.shape[0], N_CHIPS, -1), (1, 0, 2))

    def add_halves(parts, recvs, names):
        sums = [_rs_add_halves(pos, g, r, "rs_add_" + nm) for g, r, nm in zip(parts, recvs, names)]
        return [s[0] for s in sums], [s[1] for s in sums]

    def sum_owner(cp32s, recvs, names):
        return [_rs_sum_owner(pos, a, r, "rs_sum_" + nm) for a, r, nm in zip(cp32s, recvs, names)]

    ffn_names, mix_names = ["w_up", "w_down"], ["w_pa", "w_pb", "w_o"]
    part_ffn = [dw_up, dw_down.reshape(N_CHIPS, -1, D_MODEL)]
    (do_raw, dpc, dw_pa, dw_pb, dw_o, dgna, dlng, dlnb, dws, dbst, dmod_mix), sib_ffn = _mix_bwd(
        p, o_f, o_b, dx1, mod_c, gna, lng, lnb, ws3, ws3_t, bst, w_pa_f, w_pb_f, w_o_f,
        carry=_carry_sibling_halves(part_ffn))
    cp32_ffn, cpbf_ffn = add_halves(part_ffn, sib_ffn, ffn_names)
    part_mix = [shard_major(dw_pa), shard_major(dw_pb), dw_o.reshape(N_CHIPS, -1, D_MODEL)]
    (df_f, dq0, dv0, dlb0), got = _hgrn_bwd(
        p, lbg_full, s_f, do_raw, 0,
        carry=_merge_carries(_carry_to_owner(cpbf_ffn), _carry_sibling_halves(part_mix)))
    own_ffn, sib_mix = got[:2], got[2:]
    half_ffn = sum_owner(cp32_ffn, own_ffn, ffn_names)
    cp32_mix, cpbf_mix = add_halves(part_mix, sib_mix, mix_names)
    (df_b, dq, di, dlb1), got = _hgrn_bwd(
        p, lbg_full, s_b, do_raw, 1, dq0, dv0,
        carry=_merge_carries(_carry_join_halves(half_ffn), _carry_to_owner(cpbf_mix)))
    g_ffn_w, own_mix = got[:2], got[2:]
    half_mix = sum_owner(cp32_mix, own_mix, mix_names)
    (grad_x, dp, dg_mix, dmod_in), _ = _in_bwd(x, ctx, dx1, g_mix, mod_a, w_in_f, df_f, df_b, di, dq, dpc)

    rows_all = dp.shape[0] * dp.shape[1]
    tk_all = 2 * ROW_BLOCK if rows_all % (2 * ROW_BLOCK) == 0 else ROW_BLOCK
    dw_in, g_mix_w = _matmul_tn(h_all.reshape(rows_all, D_MODEL), dp.reshape(rows_all, IN_COLS), N_CHIPS, tk_all,
                                "dw_in", carry=_carry_join_halves(half_mix))
    sib_in = _comm_call("rs_sibling_w_in", _carry_sibling_halves([dw_in]))
    cp32_in, cpbf_in = add_halves([dw_in], sib_in, ["w_in"])
    own_in = _comm_call("rs_owner_w_in", _carry_to_owner(cpbf_in))
    g_in_w = _comm_call("rs_join_w_in", _carry_join_halves(sum_owner(cp32_in, own_in, ["w_in"])))

    big_names = ["w_in", "w_up", "w_pa", "w_pb", "w_o", "w_down"]
    g_big = [g_in_w[0], g_ffn_w[0], g_mix_w[0], g_mix_w[1], g_mix_w[2], g_ffn_w[1]]
    big_m = [m_w_in, m_w_up, m_w_pa, m_w_pb, m_w_o, m_w_down]
    big_v = [v_w_in, v_w_up, v_w_pa, v_w_pb, v_w_o, v_w_down]
    res = {}
    for name, w, g, m, v in zip(big_names, shards, g_big, big_m, big_v):
        d, m2, v2 = _adamw_big(w, g, m[0], v[0], "adamw_" + name)
        res[name] = (g[None], d[None], m2[None], v2[None])

    dmod_mine = jnp.concatenate([dmod_in[:, 0], dmod_in[:, 1], dmod_mix[:, 0], dmod_ffn[:, 1], dmod_ffn[:, 2],
                                 dmod_ffn[:, 3]], axis=1)
    dmc = jnp.concatenate([jnp.sum(dmod_in[:, 2], axis=0), jnp.sum(dmod_in[:, 3], axis=0),
                           jnp.zeros((4 * D_MODEL,), F32)])[None, :]
    pay3 = jnp.concatenate([dmod_mine, jnp.zeros((4 - bs, 6 * D_MODEL), F32), dmc,
                            jnp.zeros((3, 6 * D_MODEL), F32)], axis=0)
    dmod64 = _all_gather8(pay3, "gather_dmod")
    dmod64_my = lax.dynamic_slice(dmod64, (0, kc * mod_cols), (64, mod_cols))
    g_w_mod, g_b_mod, g_cctx_part = _mod_bwd(cond64, dmod64, dmod64_my, w_mod[0], c_ctx[None, :])
    d, m2, v2 = _adamw_big(w_mod[0], g_w_mod, m_w_mod[0], v_w_mod[0], "adamw_w_mod")
    res["w_mod"] = (g_w_mod[None], d[None], m2[None], v2[None])

    def row(*parts):
        return _pad_cols(jnp.concatenate([q.reshape(1, -1) for q in parts], axis=1), D_MODEL)

    small_rows = [dg_mix, dg_ffn, dg_final, g_cctx_part, row(dgna), row(dlng, dlnb), row(jnp.transpose(dbst)),
                  row(dlb0, dlb1), row(loss_part), jnp.zeros((7, D_MODEL), F32), dws.reshape(64, D_MODEL)]
    pay4 = jnp.concatenate(small_rows, axis=0)
    tot, dgam0 = _small_reduce(_all_gather8(pay4, "gather_small"), lbg_full)
    loss = tot[8, 0]
    dgam_full = jnp.stack([dgam0, -dgam0])
    g_lbg = lax.dynamic_slice(dgam_full, (0, 0, kc * HEAD_DIM), (2, 2, HEAD_DIM))

    small = [
        ("c_ctx", c_ctx[None, :], tot[3:4], m_c_ctx, v_c_ctx),
        ("b_mod", b_mod, g_b_mod, m_b_mod, v_b_mod),
        ("g_mix", g_mix, tot[0:1], m_g_mix, v_g_mix),
        ("g_ffn", g_ffn, tot[1:2], m_g_ffn, v_g_ffn),
        ("lb_gamma", lb_gamma.reshape(4, HEAD_DIM), g_lbg.reshape(4, HEAD_DIM), m_lb_gamma, v_lb_gamma),
        ("g_norm_a", g_norm_a, tot[4:5, :HEAD_DIM], m_g_norm_a, v_g_norm_a),
        ("ln_v_g", ln_v_g, tot[5:6, :KW], m_ln_v_g, v_ln_v_g),
        ("ln_v_b", ln_v_b, tot[5:6, KW:], m_ln_v_b, v_ln_v_b),
        ("w_s", w_s.reshape(N_HEADS * SGU_CHUNK, SGU_CHUNK), tot[16:80].reshape(N_HEADS * SGU_CHUNK, SGU_CHUNK),
         m_w_s, v_w_s),
        ("b_s", b_s[0], tot[6:7, :KW].reshape(N_HEADS, SGU_CHUNK), m_b_s, v_b_s),
        ("g_final", g_final[None, :], tot[2:3], m_g_final, v_g_final),
    ]
    ws_, gs_ = [s[1] for s in small], [s[2] for s in small]
    ms_ = [s[3].reshape(s[1].shape) for s in small]
    vs_ = [s[4].reshape(s[1].shape) for s in small]
    ds_, m2s_, v2s_ = _adamw_small(ws_, gs_, ms_, vs_)
    for (name, _, g, m, _), d, m2, v2 in zip(small, ds_, m2s_, v2s_):
        res[name] = tuple(t.reshape(m.shape) for t in (g, d, m2, v2))

    order = ["c_ctx", "w_mod", "b_mod", "g_mix", "g_ffn", "w_in", "lb_gamma", "g_norm_a", "ln_v_g", "ln_v_b",
             "w_s", "b_s", "w_pa", "w_pb", "w_o", "w_up", "w_down", "g_final"]
    outs = [loss, grad_x]
    for part in range(4):
        outs += [res[n][part] for n in order]
    return tuple(outs)
```

```python
import functools
import math

import jax
import jax.numpy as jnp
import numpy as np
from jax import lax
from jax.experimental import pallas as pl
from jax.experimental.pallas import tpu as pltpu

F32 = jnp.float32
BF16 = jnp.bfloat16
SDS = jax.ShapeDtypeStruct
MESH = pl.DeviceIdType.MESH

EPS = 1e-6
D_MODEL = 1024
N_HEADS = 4
HEAD_DIM = 128
KW = N_HEADS * HEAD_DIM
IN_COLS = 11 * KW
D_FF = 2816
HGRN_CHUNK = 64
SGU_CHUNK = 128
ROW_BLOCK = 256
N_CHIPS = 4
N_DEV = 8
V7X_VMEM_BYTES = 64 * 1024 * 1024
VMEM_LIMIT = V7X_VMEM_BYTES - 6 * 1024 * 1024

ADAM_LR, ADAM_B1, ADAM_B2, ADAM_EPS, ADAM_WD, ADAM_STEP = 0.001, 0.9, 0.999, 1e-08, 0.01, 10
GELU_C0 = math.sqrt(2.0 / math.pi)
GELU_C1 = 0.044715

VMEM_SPEC = pl.BlockSpec(memory_space=pltpu.VMEM)
ANY_SPEC = pl.BlockSpec(memory_space=pl.ANY)


def _params(n_grid):
    return pltpu.CompilerParams(dimension_semantics=("arbitrary",) * n_grid, vmem_limit_bytes=VMEM_LIMIT)


def _sig(x):
    return 0.5 * jnp.tanh(0.5 * x) + 0.5


def _gelu(x):
    t = jnp.tanh(GELU_C0 * (x + GELU_C1 * x * x * x))
    return 0.5 * x * (1.0 + t), t


def _dgelu(x, t):
    return 0.5 * (1.0 + t) + 0.5 * x * (1.0 - t * t) * GELU_C0 * (1.0 + 3.0 * GELU_C1 * x * x)


def _dot(a, b):
    return jnp.dot(a.astype(BF16), b.astype(BF16), preferred_element_type=F32)


def _dot_nt(a, b):
    return lax.dot_general(a.astype(BF16), b.astype(BF16), (((1,), (1,)), ((), ())), preferred_element_type=F32)


def _dot_tn(a, b):
    return lax.dot_general(a.astype(BF16), b.astype(BF16), (((0,), (0,)), ((), ())), preferred_element_type=F32)


def _dot_f32(a, b, dims=(((1,), (0,)), ((), ()))):
    return lax.dot_general(a, b, dims, precision=lax.Precision.HIGHEST, preferred_element_type=F32)


def _rms(x):
    r = lax.rsqrt(jnp.mean(x * x, axis=-1, keepdims=True) + EPS)
    return x * r, r


def _rms_bwd(dxn, xn, r):
    return r * (dxn - xn * jnp.mean(dxn * xn, axis=-1, keepdims=True))


def _colsum(a):
    return jnp.sum(a, axis=0, keepdims=True)


def _tri(n, upper):
    t = lax.broadcasted_iota(jnp.int32, (n, n), 0)
    s = lax.broadcasted_iota(jnp.int32, (n, n), 1)
    return (s >= t) if upper else (s <= t)


def _all_gather8(x_shard, name, carry=None, then=None):
    m_per, n = x_shard.shape
    n_ci = len(carry.ins) if carry else 0
    n_co = len(carry.outs) if carry else 0
    n_cs = len(carry.sems) if carry else 0

    def body(*refs):
        x_ref, cins = refs[0], refs[1:1 + n_ci]
        out_ref, couts = refs[1 + n_ci], refs[2 + n_ci:2 + n_ci + n_co]
        send_sems, recv_sems, local_sem = refs[2 + n_ci + n_co:5 + n_ci + n_co]
        csems = refs[5 + n_ci + n_co:5 + n_ci + n_co + n_cs]
        tsems = refs[5 + n_ci + n_co + n_cs:]
        if carry:
            _start_all(carry.copies(cins, couts, csems)[0])
        _gather8_body(x_ref, out_ref, send_sems, recv_sems, local_sem, m_per)
        if carry:
            _wait_all(carry.copies(cins, couts, csems)[1])
        if then:
            _start_all(then.copies(couts, couts, tsems)[0])
            _wait_all(then.copies(couts, couts, tsems)[1])

    res = pl.pallas_call(
        body, name=name, out_shape=[SDS((N_DEV * m_per, n), x_shard.dtype)] + (carry.outs if carry else []),
        in_specs=[VMEM_SPEC] + [ANY_SPEC] * n_ci, out_specs=[VMEM_SPEC] + [ANY_SPEC] * n_co,
        input_output_aliases={1 + i: 1 + o for i, o in carry.alias.items()} if carry else {},
        scratch_shapes=[pltpu.SemaphoreType.DMA((7,)), pltpu.SemaphoreType.DMA((7,)), pltpu.SemaphoreType.DMA]
        + (carry.sems if carry else []) + (then.sems if then else []),
    )(x_shard, *(carry.ins if carry else []))
    return res[0] if carry is None else (res[0], list(res[1:]))


def _gather8_body(x_ref, out_ref, send_sems, recv_sems, local_sem, m_per):
    x, y, c = lax.axis_index("x"), lax.axis_index("y"), lax.axis_index("c")
    me, sibling = (x, y, c), (x, y, 1 - c)
    chips = [(1 - x, y), (x, 1 - y), (1 - x, 1 - y)]

    def rows(px, py, pc):
        return out_ref.at[pl.ds((4 * px + 2 * py + pc) * m_per, m_per), :]

    def copy(k, block, to, src=None):
        return pltpu.make_async_remote_copy(
            src_ref=rows(*block) if src is None else src, dst_ref=rows(*block),
            send_sem=send_sems.at[k], recv_sem=recv_sems.at[k], device_id=to, device_id_type=MESH)

    mine = pltpu.make_async_copy(x_ref, rows(*me), local_sem)
    mine.start()
    first = [copy(0, me, sibling, src=x_ref)]
    first += [copy(1 + j, me, (*chip, c), src=x_ref) for j, chip in enumerate(chips)]
    for cp in first:
        cp.start()
    passed = [copy(4 + j, (*chip, c), sibling) for j, chip in enumerate(chips)]
    for j, chip in enumerate(chips):
        copy(1 + j, (*chip, c), me).wait_recv()
        passed[j].start()
    copy(0, sibling, me).wait_recv()
    for j, chip in enumerate(chips):
        copy(4 + j, (*chip, 1 - c), me).wait_recv()
    for cp in first + passed:
        cp.wait_send()
    mine.wait()


def _mesh_pos():
    x, y, c = lax.axis_index("x"), lax.axis_index("y"), lax.axis_index("c")
    chips = [(1 - x, y), (x, 1 - y), (1 - x, 1 - y)]
    return x, y, c, 2 * x + y, (x, y, 1 - c), chips


def _half_rows(c, rh):
    return pl.ds(pl.multiple_of(c * rh, 16), rh)


class _Carry:
    def __init__(self, ins, outs, alias, sems, copies):
        self.ins, self.outs, self.alias, self.sems, self.copies = list(ins), list(outs), dict(alias), list(sems), copies


def _remote(src, dst, send, recv, to):
    return functools.partial(pltpu.make_async_remote_copy, src_ref=src, dst_ref=dst, send_sem=send, recv_sem=recv,
                             device_id=to, device_id_type=MESH)


def _carry_gather_send(bufs):
    n = len(bufs)

    def copies(ins, outs, sems):
        x, y, c, kc, sibling, chips = _mesh_pos()
        starts, waits = [], []
        for wi in range(n):
            rh = outs[wi].shape[1] // 2
            for jj, chip in enumerate(chips):
                mine = outs[wi].at[kc, _half_rows(c, rh), :]
                cp = _remote(mine, mine, sems[0].at[wi, jj], sems[1].at[wi, jj], (*chip, c))
                starts.append(cp)
                waits.append((cp, "send"))
                theirs = outs[wi].at[2 * chip[0] + chip[1], _half_rows(c, rh), :]
                waits.append((_remote(theirs, theirs, sems[0].at[wi, jj], sems[1].at[wi, jj], (*chip, c)), "recv"))
        return starts, waits

    return _Carry(bufs, [SDS(b.shape, b.dtype) for b in bufs], {i: i for i in range(n)},
                  [pltpu.SemaphoreType.DMA((n, 3)), pltpu.SemaphoreType.DMA((n, 3))], copies)


def _carry_gather_forward(bufs):
    n = len(bufs)

    def copies(ins, outs, sems):
        x, y, c, kc, sibling, chips = _mesh_pos()
        starts, waits = [], []
        for wi in range(n):
            rh = outs[wi].shape[1] // 2
            for jj, chip in enumerate(chips):
                got = outs[wi].at[2 * chip[0] + chip[1], _half_rows(c, rh), :]
                cp = _remote(got, got, sems[0].at[wi, jj], sems[1].at[wi, jj], sibling)
                starts.append(cp)
                waits.append((cp, "send"))
                other = outs[wi].at[2 * chip[0] + chip[1], _half_rows(1 - c, rh), :]
                waits.append((_remote(other, other, sems[0].at[wi, jj], sems[1].at[wi, jj], sibling), "recv"))
        return starts, waits

    return _Carry(bufs, [SDS(b.shape, b.dtype) for b in bufs], {i: i for i in range(n)},
                  [pltpu.SemaphoreType.DMA((n, 3)), pltpu.SemaphoreType.DMA((n, 3))], copies)


def _carry_sibling_halves(grads):
    n = len(grads)

    def copies(ins, outs, sems):
        x, y, c, kc, sibling, chips = _mesh_pos()
        cps = [_remote(ins[wi].at[:, _half_rows(1 - c, ins[wi].shape[1] // 2), :], outs[wi],
                       sems[0].at[wi], sems[1].at[wi], sibling) for wi in range(n)]
        return cps, [(cp, "both") for cp in cps]

    return _Carry(grads, [SDS((N_CHIPS, g.shape[1] // 2, g.shape[2]), F32) for g in grads], {},
                  [pltpu.SemaphoreType.DMA((n,)), pltpu.SemaphoreType.DMA((n,))], copies)


def _carry_to_owner(cpbfs):
    n = len(cpbfs)

    def copies(ins, outs, sems):
        x, y, c, kc, sibling, chips = _mesh_pos()
        starts, waits = [], []
        for wi in range(n):
            for jj, chip in enumerate(chips):
                cp = _remote(ins[wi].at[2 * chip[0] + chip[1]], outs[wi].at[kc],
                             sems[0].at[wi, jj], sems[1].at[wi, jj], (*chip, c))
                starts.append(cp)
                waits.append((cp, "send"))
                slot = outs[wi].at[2 * chip[0] + chip[1]]
                waits.append((_remote(slot, slot, sems[0].at[wi, jj], sems[1].at[wi, jj], (*chip, c)), "recv"))
        return starts, waits

    return _Carry(cpbfs, [SDS(g.shape, BF16) for g in cpbfs], {},
                  [pltpu.SemaphoreType.DMA((n, 3)), pltpu.SemaphoreType.DMA((n, 3))], copies)


def _carry_join_halves(bufs):
    n = len(bufs)

    def copies(ins, outs, sems):
        x, y, c, kc, sibling, chips = _mesh_pos()
        cps = []
        for wi in range(n):
            mine = outs[wi].at[_half_rows(c, outs[wi].shape[0] // 2), :]
            cps.append(_remote(mine, mine, sems[0].at[wi], sems[1].at[wi], sibling))
        return cps, [(cp, "both") for cp in cps]

    return _Carry(bufs, [SDS(b.shape, F32) for b in bufs], {i: i for i in range(n)},
                  [pltpu.SemaphoreType.DMA((n,)), pltpu.SemaphoreType.DMA((n,))], copies)


def _merge_carries(*carries):
    ins, outs, alias, sems, parts = [], [], {}, [], []
    for cy in carries:
        parts.append((len(ins), len(cy.ins), len(outs), len(cy.outs), len(sems), len(cy.sems), cy.copies))
        alias.update({len(ins) + i: len(outs) + o for i, o in cy.alias.items()})
        ins += cy.ins
        outs += cy.outs
        sems += cy.sems

    def copies(i, o, s):
        starts, waits = [], []
        for i0, ni, o0, no, s0, ns, fn in parts:
            st, wt = fn(i[i0:i0 + ni], o[o0:o0 + no], s[s0:s0 + ns])
            starts += st
            waits += wt
        return starts, waits

    return _Carry(ins, outs, alias, sems, copies)


def _start_all(starts):
    for cp in starts:
        cp().start()


def _wait_all(waits):
    for cp, which in waits:
        if which == "send":
            cp().wait_send()
        elif which == "recv":
            cp().wait_recv()
        else:
            cp().wait()


def _comm_call(name, carry):
    n_i, n_o = len(carry.ins), len(carry.outs)

    def body(*refs):
        ins, outs, sems = refs[:n_i], refs[n_i:n_i + n_o], refs[n_i + n_o:]
        _start_all(carry.copies(ins, outs, sems)[0])
        _wait_all(carry.copies(ins, outs, sems)[1])

    return pl.pallas_call(
        body, name=name, out_shape=carry.outs, in_specs=[ANY_SPEC] * n_i, out_specs=[ANY_SPEC] * n_o,
        input_output_aliases=carry.alias, scratch_shapes=carry.sems,
    )(*carry.ins)


def _host_call(body, *, name, grid, in_specs, out_specs, out_shape, args, scratch_shapes=(), carry=None):
    n_in, n_out, n_scr = len(in_specs), len(out_specs), len(scratch_shapes)
    if carry is None:
        res = pl.pallas_call(body, name=name, grid=grid, in_specs=in_specs, out_specs=out_specs, out_shape=out_shape,
                             scratch_shapes=list(scratch_shapes), compiler_params=_params(len(grid)))(*args)
        return list(res), []
    n_ci, n_co = len(carry.ins), len(carry.outs)

    def wrapped(*refs):
        ins, cins = refs[:n_in], refs[n_in:n_in + n_ci]
        o0 = n_in + n_ci
        outs, couts = refs[o0:o0 + n_out], refs[o0 + n_out:o0 + n_out + n_co]
        s0 = o0 + n_out + n_co
        scr, sems = refs[s0:s0 + n_scr], refs[s0 + n_scr:]
        idx = [pl.program_id(a) for a in range(len(grid))]
        first = functools.reduce(jnp.logical_and, [i == 0 for i in idx])
        last = functools.reduce(jnp.logical_and, [i == g - 1 for i, g in zip(idx, grid)])

        @pl.when(first)
        def _():
            _start_all(carry.copies(cins, couts, sems)[0])

        body(*ins, *outs, *scr)

        @pl.when(last)
        def _():
            _wait_all(carry.copies(cins, couts, sems)[1])

    res = pl.pallas_call(
        wrapped, name=name, grid=grid, in_specs=list(in_specs) + [ANY_SPEC] * n_ci,
        out_specs=list(out_specs) + [ANY_SPEC] * n_co, out_shape=list(out_shape) + carry.outs,
        scratch_shapes=list(scratch_shapes) + carry.sems,
        input_output_aliases={n_in + i: n_out + o for i, o in carry.alias.items()},
        compiler_params=_params(len(grid)),
    )(*args, *carry.ins)
    return list(res[:n_out]), list(res[n_out:])


def _rs_add_halves(pos, grad, recv, name):
    _, rs, cs = grad.shape
    rh = rs // 2
    rb = rh // 2

    def body(pos_ref, g_ref, r_ref, o32_ref, obf_ref):
        s = g_ref[...] + r_ref[...]
        o32_ref[...] = s
        obf_ref[...] = s.astype(BF16)

    blk = (1, rb, cs)
    return pl.pallas_call(
        body, name=name,
        grid_spec=pltpu.PrefetchScalarGridSpec(
            num_scalar_prefetch=1, grid=(N_CHIPS, 2),
            in_specs=[pl.BlockSpec(blk, lambda k, i, p: (k, p[1] * 2 + i, 0)),
                      pl.BlockSpec(blk, lambda k, i, p: (k, i, 0))],
            out_specs=[pl.BlockSpec(blk, lambda k, i, p: (k, i, 0)), pl.BlockSpec(blk, lambda k, i, p: (k, i, 0))]),
        out_shape=[SDS((N_CHIPS, rh, cs), F32), SDS((N_CHIPS, rh, cs), BF16)],
        compiler_params=_params(2),
    )(pos, grad, recv)


def _rs_sum_owner(pos, cp32, recv3, name):
    _, rh, cs = cp32.shape
    rb = rh // 2

    def body(pos_ref, own_ref, r1_ref, r2_ref, r3_ref, o_ref):
        o_ref[...] = ((own_ref[0] + r1_ref[0].astype(F32)) + r2_ref[0].astype(F32)) + r3_ref[0].astype(F32)

    blk = (1, rb, cs)

    def slot(d):
        return pl.BlockSpec(blk, lambda i, p: ((p[0] + d) % N_CHIPS, i, 0))

    return pl.pallas_call(
        body, name=name,
        grid_spec=pltpu.PrefetchScalarGridSpec(
            num_scalar_prefetch=1, grid=(2,),
            in_specs=[slot(0), slot(1), slot(2), slot(3)],
            out_specs=pl.BlockSpec((rb, cs), lambda i, p: (p[1] * 2 + i, 0))),
        out_shape=SDS((2 * rh, cs), F32),
        compiler_params=_params(1),
    )(pos, cp32, recv3, recv3, recv3)


def _cast_bf16(pos, arrs):
    n = len(arrs)

    def body(pos_ref, *refs):
        for i in range(n):
            refs[n + i][0] = refs[i][...].astype(BF16)

    return pl.pallas_call(
        body, name="cast_bf16",
        grid_spec=pltpu.PrefetchScalarGridSpec(
            num_scalar_prefetch=1, grid=(2,),
            in_specs=[pl.BlockSpec((a.shape[0] // 2, a.shape[1]), lambda i, p: (i, 0)) for a in arrs],
            out_specs=[pl.BlockSpec((1, a.shape[0] // 2, a.shape[1]), lambda i, p: (p[0], i, 0)) for a in arrs]),
        out_shape=[SDS((N_CHIPS,) + a.shape, BF16) for a in arrs],
        compiler_params=_params(1),
    )(pos, *arrs)


def _adamw_vals(w, g, m, v):
    m2 = ADAM_B1 * m + (1.0 - ADAM_B1) * g
    v2 = ADAM_B2 * v + (1.0 - ADAM_B2) * (g * g)
    m_hat = m2 / (1.0 - ADAM_B1 ** ADAM_STEP)
    v_hat = v2 / (1.0 - ADAM_B2 ** ADAM_STEP)
    delta = -ADAM_LR * (m_hat / (jnp.sqrt(v_hat) + ADAM_EPS) + ADAM_WD * w)
    return delta, m2, v2


def _adamw_big(w, g, m, v, name):
    rows, cols = w.shape
    rb = rows // 4

    def body(w_ref, g_ref, m_ref, v_ref, d_ref, m2_ref, v2_ref):
        d, m2, v2 = _adamw_vals(w_ref[...], g_ref[...], m_ref[...], v_ref[...])
        d_ref[...] = d
        m2_ref[...] = m2
        v2_ref[...] = v2

    spec = pl.BlockSpec((rb, cols), lambda i: (i, 0))
    return pl.pallas_call(
        body, name=name, grid=(4,), in_specs=[spec] * 4, out_specs=[spec] * 3,
        out_shape=[SDS(w.shape, F32)] * 3, compiler_params=_params(1),
    )(w, g, m, v)


ADAMW_GROUP_STEPS = 8


def _adamw_group(ws, gs, ms, vs, name, carry=None):
    n = len(ws)

    def body(*refs):
        for i in range(n):
            d, m2, v2 = _adamw_vals(refs[i][...], refs[n + i][...], refs[2 * n + i][...], refs[3 * n + i][...])
            refs[4 * n + i][...] = d
            refs[5 * n + i][...] = m2
            refs[6 * n + i][...] = v2

    specs = [pl.BlockSpec((w.shape[0] // ADAMW_GROUP_STEPS, w.shape[1]), lambda i: (i, 0)) for w in ws]
    shapes = [SDS(w.shape, F32) for w in ws]
    outs, carried = _host_call(
        body, name=name, grid=(ADAMW_GROUP_STEPS,), in_specs=specs * 4, out_specs=specs * 3, out_shape=shapes * 3,
        args=(*ws, *gs, *ms, *vs), carry=carry)
    return (outs[:n], outs[n:2 * n], outs[2 * n:]), carried


def _adamw_small(ws, gs, ms, vs):
    n = len(ws)

    def body(*refs):
        for i in range(n):
            d, m2, v2 = _adamw_vals(refs[i][...], refs[n + i][...], refs[2 * n + i][...], refs[3 * n + i][...])
            refs[4 * n + i][...] = d
            refs[5 * n + i][...] = m2
            refs[6 * n + i][...] = v2

    shapes = [SDS(w.shape, F32) for w in ws]
    outs = pl.pallas_call(
        body, name="adamw_small", out_shape=shapes * 3,
        in_specs=[VMEM_SPEC] * (4 * n), out_specs=[VMEM_SPEC] * (3 * n),
    )(*ws, *gs, *ms, *vs)
    return outs[:n], outs[n:2 * n], outs[2 * n:]


def _mod_fwd(cond64, w_mod_s, b_mod_s):
    def body(c_ref, w_ref, b_ref, o_ref):
        cc = c_ref[...]
        o_ref[...] = _dot_f32(cc * _sig(cc), w_ref[...]) + b_ref[...]

    return pl.pallas_call(
        body, name="mod_fwd", out_shape=SDS((cond64.shape[0], w_mod_s.shape[1]), F32),
        in_specs=[VMEM_SPEC] * 3, out_specs=VMEM_SPEC,
        compiler_params=pltpu.CompilerParams(vmem_limit_bytes=VMEM_LIMIT),
    )(cond64, w_mod_s, b_mod_s)


def _mod_bwd(cond64, dmod64, dmod64_my, w_mod_s, c_ctx):
    def body(c_ref, g_ref, gm_ref, w_ref, cc_ref, gw_ref, gb_ref, gcc_ref):
        cc = c_ref[...]
        act = cc * _sig(cc)
        gm = gm_ref[...]
        gw_ref[...] = _dot_f32(act, gm, (((0,), (0,)), ((), ())))
        gb_ref[...] = _colsum(g_ref[...])
        dact = _dot_f32(gm, w_ref[...], (((1,), (1,)), ((), ())))
        tot = dact[4:5, :]
        for dev in range(1, N_DEV):
            tot = tot + dact[8 * dev + 4:8 * dev + 5, :]
        c0 = cc_ref[...]
        s0 = _sig(c0)
        gcc_ref[...] = tot * (s0 * (1.0 + c0 * (1.0 - s0)))

    return pl.pallas_call(
        body, name="mod_bwd",
        out_shape=[SDS(w_mod_s.shape, F32), SDS((1, dmod64.shape[1]), F32), SDS((1, D_MODEL), F32)],
        in_specs=[VMEM_SPEC] * 5, out_specs=[VMEM_SPEC] * 3,
        compiler_params=pltpu.CompilerParams(vmem_limit_bytes=VMEM_LIMIT),
    )(cond64, dmod64, dmod64_my, w_mod_s, c_ctx)


def _in_fwd(x, ctx, g_mix, mod_a, w_in, carry=None):
    bs, seq, _ = x.shape
    nb = seq // ROW_BLOCK + 1

    def body(x_ref, ctx_ref, g_ref, mod_ref, w_ref, p_ref, h_ref):
        is_ctx = pl.program_id(1) == 0
        xin = jnp.where(is_ctx, ctx_ref[0], x_ref[0])
        shift = jnp.where(is_ctx, mod_ref[0, 2:3, :], mod_ref[0, 0:1, :])
        scale = jnp.where(is_ctx, mod_ref[0, 3:4, :], mod_ref[0, 1:2, :])
        xn, _ = _rms(xin)
        hb = ((xn * g_ref[...]) * (1.0 + scale) + shift).astype(BF16)
        h_ref[0] = hb
        p_ref[0] = jnp.dot(hb, w_ref[...], preferred_element_type=F32)

    return _host_call(
        body, name="in_fwd", grid=(bs, nb),
        in_specs=[pl.BlockSpec((1, ROW_BLOCK, D_MODEL), lambda b, j: (b, jnp.maximum(j - 1, 0), 0)),
                  pl.BlockSpec((1, ROW_BLOCK, D_MODEL), lambda b, j: (b, 0, 0)),
                  pl.BlockSpec((1, D_MODEL), lambda b, j: (0, 0)),
                  pl.BlockSpec((1, 8, D_MODEL), lambda b, j: (b, 0, 0)),
                  pl.BlockSpec((D_MODEL, IN_COLS), lambda b, j: (0, 0))],
        out_specs=[pl.BlockSpec((1, ROW_BLOCK, IN_COLS), lambda b, j: (b, j, 0)),
                   pl.BlockSpec((1, ROW_BLOCK, D_MODEL), lambda b, j: (b, j, 0))],
        out_shape=[SDS((bs, nb * ROW_BLOCK, IN_COLS), F32), SDS((bs, nb * ROW_BLOCK, D_MODEL), BF16)],
        args=(x, ctx, g_mix, mod_a, w_in), carry=carry)


def _in_bwd(x, ctx, dx1, g_mix, mod_a, w_in, df_f, df_b, di, dq, dpc, carry=None):
    bs, seq, _ = x.shape
    nb = seq // ROW_BLOCK + 1

    def body(x_ref, ctx_ref, dx1_ref, g_ref, mod_ref, w_ref, dff_ref, dfb_ref, di_ref, dq_ref, dpc_ref,
             gx_ref, dp_ref, dg_ref, dmod_ref):
        b, j = pl.program_id(0), pl.program_id(1)
        is_ctx = j == 0

        @pl.when((b == 0) & (j == 0))
        def _():
            dg_ref[...] = jnp.zeros_like(dg_ref)

        @pl.when(j == 0)
        def _():
            dmod_ref[...] = jnp.zeros_like(dmod_ref)

        dp = jnp.concatenate([dff_ref[0], dfb_ref[0], di_ref[0], dq_ref[0], dpc_ref[0]], axis=1)
        dp_ref[0] = dp
        dh = lax.dot_general(dp, w_ref[...], (((1,), (1,)), ((), ())), preferred_element_type=F32)
        xin = jnp.where(is_ctx, ctx_ref[0], x_ref[0])
        scale = jnp.where(is_ctx, mod_ref[0, 3:4, :], mod_ref[0, 1:2, :])
        xn, r = _rms(xin)
        g = g_ref[...]
        hn = xn * g
        d_shift = _colsum(dh)
        d_scale = _colsum(dh * hn)
        dhn = dh * (1.0 + scale)
        dg_ref[...] += _colsum(dhn * xn)
        dx = _rms_bwd(dhn * g, xn, r)

        @pl.when(is_ctx)
        def _():
            dmod_ref[0, 2:3, :] += d_shift
            dmod_ref[0, 3:4, :] += d_scale

        @pl.when(jnp.logical_not(is_ctx))
        def _():
            dmod_ref[0, 0:1, :] += d_shift
            dmod_ref[0, 1:2, :] += d_scale
            gx_ref[0] = dx + dx1_ref[0]

    def rows(w):
        return pl.BlockSpec((1, ROW_BLOCK, w), lambda b, j: (b, j, 0))

    lat = pl.BlockSpec((1, ROW_BLOCK, D_MODEL), lambda b, j: (b, jnp.maximum(j - 1, 0), 0))
    return _host_call(
        body, name="in_bwd", grid=(bs, nb),
        in_specs=[lat, pl.BlockSpec((1, ROW_BLOCK, D_MODEL), lambda b, j: (b, 0, 0)), lat,
                  pl.BlockSpec((1, D_MODEL), lambda b, j: (0, 0)),
                  pl.BlockSpec((1, 8, D_MODEL), lambda b, j: (b, 0, 0)),
                  pl.BlockSpec((D_MODEL, IN_COLS), lambda b, j: (0, 0)),
                  rows(KW), rows(KW), rows(KW), rows(KW), rows(7 * KW)],
        out_specs=[lat, rows(IN_COLS), pl.BlockSpec((1, D_MODEL), lambda b, j: (0, 0)),
                   pl.BlockSpec((1, 8, D_MODEL), lambda b, j: (b, 0, 0))],
        out_shape=[SDS(x.shape, F32), SDS((bs, nb * ROW_BLOCK, IN_COLS), BF16), SDS((1, D_MODEL), F32),
                   SDS((bs, 8, D_MODEL), F32)],
        args=(x, ctx, dx1, g_mix, mod_a, w_in, df_f, df_b, di, dq, dpc), carry=carry)


def _lower_bound(lbg_ref, direction):
    return _sig(lbg_ref[0, direction:direction + 1, :] - lbg_ref[1, direction:direction + 1, :])


N_CHUNKS = ROW_BLOCK // HGRN_CHUNK


def _block_tri(upper):
    t = np.arange(ROW_BLOCK)[:, None]
    s = np.arange(ROW_BLOCK)[None, :]
    same = (t // HGRN_CHUNK) == (s // HGRN_CHUNK)
    return jnp.asarray(same & ((s >= t) if upper else (s <= t)), dtype=BF16)


TRI_SPEC = pl.BlockSpec((ROW_BLOCK, ROW_BLOCK), lambda b, j: (0, 0))


def _tri_matmul_f32(tri, g):
    g0 = g.astype(BF16)
    r1 = g - g0.astype(F32)
    g1 = r1.astype(BF16)
    g2 = (r1 - g1.astype(F32)).astype(BF16)
    return (jnp.dot(tri, g2, preferred_element_type=F32) + jnp.dot(tri, g1, preferred_element_type=F32)) \
        + jnp.dot(tri, g0, preferred_element_type=F32)


def _chunk_rows(rows):
    return jnp.concatenate([jnp.broadcast_to(r, (HGRN_CHUNK, r.shape[1])) for r in rows], axis=0)


def _block_gates(fl, q, lb, tri, upper):
    t = {}
    t["sg"] = _sig(fl)
    t["f"] = lb + (1.0 - lb) * t["sg"]
    k = 1.0 - t["f"]
    bcum = _tri_matmul_f32(tri, jnp.log(t["f"]))
    ends = [bcum[ci * HGRN_CHUNK:ci * HGRN_CHUNK + 1] if upper else bcum[(ci + 1) * HGRN_CHUNK - 1:(ci + 1) * HGRN_CHUNK]
            for ci in range(fl.shape[0] // HGRN_CHUNK)]
    mid = _chunk_rows([0.5 * r for r in ends])
    t["dec"] = [jnp.exp(r) for r in ends]
    t["e1"] = jnp.exp(bcum - mid)
    t["e2"] = jnp.exp(mid - bcum)
    t["eh"] = _chunk_rows([jnp.exp(0.5 * r) for r in ends])
    t["qi"] = q * t["e1"]
    t["ki"] = k * t["e2"]
    t["kd"] = t["ki"] * t["eh"]
    t["qe"] = t["qi"] * t["eh"]
    return t


def _hgrn_block_order(direction, nb):
    if direction == 0:
        return lambda j: j
    return lambda j: jnp.where(j == 0, 0, nb - j)


def _hgrn_fwd(p, lbg, direction, carry=None):
    bs, rows, _ = p.shape
    nb = rows // ROW_BLOCK
    ncb = ROW_BLOCK // HGRN_CHUNK
    upper = direction == 1
    order = _hgrn_block_order(direction, nb)

    def body(f_ref, i_ref, q_ref, lbg_ref, tri_ref, o_ref, s_ref, st):
        @pl.when(pl.program_id(1) == 0)
        def _():
            st[...] = jnp.zeros_like(st)

        lb = _lower_bound(lbg_ref, direction)
        mask = _tri(HGRN_CHUNK, upper)
        t = _block_gates(f_ref[0], q_ref[0], lb, tri_ref[...], upper)
        v = i_ref[0]
        chunk = lambda a, ci, h: a[ci * HGRN_CHUNK:(ci + 1) * HGRN_CHUNK, h * HEAD_DIM:(h + 1) * HEAD_DIM]
        intra = [[None] * N_HEADS for _ in range(ncb)]
        ds_loc = [[None] * N_HEADS for _ in range(ncb)]
        for ci in range(ncb):
            for h in range(N_HEADS):
                a = jnp.where(mask, _dot_nt(chunk(t["qi"], ci, h), chunk(t["ki"], ci, h)), 0.0)
                intra[ci][h] = _dot(a, chunk(v, ci, h))
                ds_loc[ci][h] = _dot_tn(chunk(v, ci, h), chunk(t["kd"], ci, h))
        for h in range(N_HEADS):
            ls = slice(h * HEAD_DIM, (h + 1) * HEAD_DIM)
            s = st[h]
            for ci in (reversed(range(ncb)) if upper else range(ncb)):
                s_ref[0, 0, ci, h] = s
                o_ref[0, ci * HGRN_CHUNK:(ci + 1) * HGRN_CHUNK, ls] = intra[ci][h] + _dot_nt(chunk(t["qe"], ci, h), s)
                s = s * t["dec"][ci][:, ls] + ds_loc[ci][h]
            st[h] = s

    def col(cb):
        return pl.BlockSpec((1, ROW_BLOCK, KW), lambda b, j: (b, order(j), cb))

    return _host_call(
        body, name=f"hgrn_fwd{direction}", grid=(bs, nb),
        in_specs=[col(direction), col(2), col(3), pl.BlockSpec((2, 2, KW), lambda b, j: (0, 0, 0)), TRI_SPEC],
        out_specs=[pl.BlockSpec((1, ROW_BLOCK, KW), lambda b, j: (b, order(j), 0)),
                   pl.BlockSpec((1, 1, ncb, N_HEADS, HEAD_DIM, HEAD_DIM), lambda b, j: (b, order(j), 0, 0, 0, 0))],
        out_shape=[SDS((bs, rows, KW), F32), SDS((bs, nb, ncb, N_HEADS, HEAD_DIM, HEAD_DIM), F32)],
        scratch_shapes=[pltpu.VMEM((N_HEADS, HEAD_DIM, HEAD_DIM), F32)],
        args=(p, p, p, lbg, _block_tri(upper)), carry=carry)


def _hgrn_bwd(p, lbg, s_saved, do_raw, direction, dq_prev=None, dv_prev=None, carry=None):
    bs, rows, _ = p.shape
    nb = rows // ROW_BLOCK
    ncb = ROW_BLOCK // HGRN_CHUNK
    upper = direction == 1
    fwd_order = _hgrn_block_order(direction, nb)
    order = lambda j: fwd_order(nb - 1 - j)
    last = dq_prev is not None
    out_dt = BF16 if last else F32

    def body(*refs):
        if last:
            (f_ref, i_ref, q_ref, lbg_ref, tri_ref, trit_ref, s_ref, do_ref, dqp_ref, dvp_ref,
             df_ref, dq_ref, dv_ref, dlb_ref, dst, acc) = refs
        else:
            (f_ref, i_ref, q_ref, lbg_ref, tri_ref, trit_ref, s_ref, do_ref,
             df_ref, dq_ref, dv_ref, dlb_ref, dst, acc) = refs
        b, j = pl.program_id(0), pl.program_id(1)

        @pl.when((b == 0) & (j == 0))
        def _():
            dlb_ref[...] = jnp.zeros_like(dlb_ref)

        @pl.when(j == 0)
        def _():
            dst[...] = jnp.zeros_like(dst)

        lb = _lower_bound(lbg_ref, direction)
        mask = _tri(HGRN_CHUNK, upper)
        mask_t = _tri(HGRN_CHUNK, not upper)
        t = _block_gates(f_ref[0], q_ref[0], lb, tri_ref[...], upper)
        qi, ki, kd, qe = t["qi"], t["ki"], t["kd"], t["qe"]
        v = i_ref[0]
        do = do_ref[0]
        chunk = lambda a, ci, h: a[ci * HGRN_CHUNK:(ci + 1) * HGRN_CHUNK, h * HEAD_DIM:(h + 1) * HEAD_DIM]
        grid2 = lambda: [[None] * N_HEADS for _ in range(ncb)]
        pairs = [(ci, h) for ci in range(ncb) for h in range(N_HEADS)]
        rows_of = lambda ci: slice(ci * HGRN_CHUNK, (ci + 1) * HGRN_CHUNK)
        lanes_of = lambda h: slice(h * HEAD_DIM, (h + 1) * HEAD_DIM)
        a_t, da, da_t, dv_in, ds_loc = (grid2() for _ in range(5))
        for ci, h in pairs:
            a_t[ci][h] = _dot_nt(chunk(ki, ci, h), chunk(qi, ci, h))
        for ci, h in pairs:
            da[ci][h] = _dot_nt(chunk(do, ci, h), chunk(v, ci, h))
        for ci, h in pairs:
            da_t[ci][h] = _dot_nt(chunk(v, ci, h), chunk(do, ci, h))
        for ci, h in pairs:
            acc[3, rows_of(ci), lanes_of(h)] = _dot(chunk(do, ci, h), s_ref[0, 0, ci, h])
        for ci, h in pairs:
            ds_loc[ci][h] = _dot_tn(chunk(do, ci, h), chunk(qe, ci, h))
        for ci, h in pairs:
            acc[0, rows_of(ci), lanes_of(h)] = _dot(jnp.where(mask, da[ci][h], 0.0), chunk(ki, ci, h))
        for ci, h in pairs:
            acc[1, rows_of(ci), lanes_of(h)] = _dot(jnp.where(mask_t, da_t[ci][h], 0.0), chunk(qi, ci, h))
        for ci, h in pairs:
            dv_in[ci][h] = _dot(jnp.where(mask_t, a_t[ci][h], 0.0), chunk(do, ci, h))
        ddec = [[None] * N_HEADS for _ in range(ncb)]
        for h in range(N_HEADS):
            ls = slice(h * HEAD_DIM, (h + 1) * HEAD_DIM)
            ds = dst[h]
            for ci in (range(ncb) if upper else reversed(range(ncb))):
                rs = slice(ci * HGRN_CHUNK, (ci + 1) * HGRN_CHUNK)
                acc[2, rs, ls] = _dot(chunk(v, ci, h), ds)
                acc[4, rs, ls] = dv_in[ci][h] + _dot_nt(chunk(kd, ci, h), ds)
                ddec[ci][h] = _colsum(ds * s_ref[0, 0, ci, h])
                ds = ds * t["dec"][ci][:, ls] + ds_loc[ci][h]
            dst[h] = ds
        dqi, dki, dkd, dqe, dv = (acc[i] for i in range(5))
        dq = t["e1"] * (dqi + dqe * t["eh"])
        dk = t["e2"] * (dki + dkd * t["eh"])
        db = dqi * qi - dki * ki - dkd * kd + dqe * qe
        dkd_kd = dkd * kd
        dbl = [_colsum(dkd_kd[ci * HGRN_CHUNK:(ci + 1) * HGRN_CHUNK]) + jnp.concatenate(ddec[ci], axis=1) * t["dec"][ci]
               for ci in range(ncb)]
        dg = _tri_matmul_f32(trit_ref[...], db) + _chunk_rows(dbl)
        df = dg / t["f"] - dk
        sg = t["sg"]
        dlb_ref[...] += _colsum(df * (1.0 - sg))
        df_ref[0] = (df * (1.0 - lb) * sg * (1.0 - sg)).astype(BF16)
        if last:
            dq_ref[0] = (dq + dqp_ref[0]).astype(out_dt)
            dv_ref[0] = (dv + dvp_ref[0]).astype(out_dt)
        else:
            dq_ref[0] = dq
            dv_ref[0] = dv

    def col(cb):
        return pl.BlockSpec((1, ROW_BLOCK, KW), lambda b, j: (b, order(j), cb))

    row = pl.BlockSpec((1, ROW_BLOCK, KW), lambda b, j: (b, order(j), 0))
    in_specs = [col(direction), col(2), col(3), pl.BlockSpec((2, 2, KW), lambda b, j: (0, 0, 0)), TRI_SPEC, TRI_SPEC,
                pl.BlockSpec((1, 1, ncb, N_HEADS, HEAD_DIM, HEAD_DIM), lambda b, j: (b, order(j), 0, 0, 0, 0)), row]
    args = [p, p, p, lbg, _block_tri(upper), _block_tri(not upper), s_saved, do_raw]
    if last:
        in_specs += [row, row]
        args += [dq_prev, dv_prev]
    return _host_call(
        body, name=f"hgrn_bwd{direction}", grid=(bs, nb), in_specs=in_specs,
        out_specs=[row, row, row, pl.BlockSpec((1, KW), lambda b, j: (0, 0))],
        out_shape=[SDS((bs, rows, KW), BF16), SDS((bs, rows, KW), out_dt), SDS((bs, rows, KW), out_dt),
                   SDS((1, KW), F32)],
        scratch_shapes=[pltpu.VMEM((N_HEADS, HEAD_DIM, HEAD_DIM), F32), pltpu.VMEM((5, ROW_BLOCK, KW), F32)],
        args=args, carry=carry)


def _mix_values(og, u, v, ga, gb, o_raw, gna, lng, lnb, ws_ref, bst, wpa, wpb, wo):
    t = {}
    sog = _sig(og)
    t["sog"], t["silu_og"] = sog, og * sog
    xh_l, r_l = [], []
    for h in range(N_HEADS):
        xh, r = _rms(o_raw[:, h * HEAD_DIM:(h + 1) * HEAD_DIM])
        xh_l.append(xh)
        r_l.append(r)
    t["xh"], t["r"] = jnp.concatenate(xh_l, axis=1), r_l
    gna4 = jnp.concatenate([gna] * N_HEADS, axis=1)
    t["gna4"] = gna4
    t["o_n"] = t["xh"] * gna4
    t["o_a"] = t["o_n"] * t["silu_og"]
    t["gu"], t["tu"] = _gelu(u)
    gv, t["tv"] = _gelu(v)
    mu = jnp.mean(gv, axis=-1, keepdims=True)
    cen = gv - mu
    t["rstd"] = lax.rsqrt(jnp.mean(cen * cen, axis=-1, keepdims=True) + EPS)
    t["xhat"] = cen * t["rstd"]
    vn = t["xhat"] * lng + lnb
    t["vn"] = vn
    chunks = []
    for n in range(ROW_BLOCK // SGU_CHUNK):
        rs = slice(n * SGU_CHUNK, (n + 1) * SGU_CHUNK)
        groups = []
        for g in range(N_HEADS):
            ls = slice(g * HEAD_DIM, (g + 1) * HEAD_DIM)
            groups.append(_dot(ws_ref[g], vn[rs, ls]) + bst[:, g:g + 1])
        chunks.append(jnp.concatenate(groups, axis=1))
    t["mixed"] = jnp.concatenate(chunks, axis=0)
    t["o_bm"] = t["gu"] * t["mixed"]
    t["ya"] = _dot(t["o_a"], wpa)
    t["yb"] = _dot(t["o_bm"], wpb)
    t["sa"], t["sb"] = _sig(ga), _sig(gb)
    t["merged"] = t["sa"] * t["ya"] + t["sb"] * t["yb"]
    t["mix"] = _dot(t["merged"], wo)
    return t


def _mix_in_specs(row_of):
    def col(cb):
        return pl.BlockSpec((1, ROW_BLOCK, KW), lambda b, j: (b, row_of(j), cb))
    return [col(cb) for cb in range(4, 11)]


def _mix_param_specs():
    full2 = lambda r, c: pl.BlockSpec((r, c), lambda b, j: (0, 0))
    return [full2(1, HEAD_DIM), full2(1, KW), full2(1, KW),
            pl.BlockSpec((N_HEADS, SGU_CHUNK, SGU_CHUNK), lambda b, j: (0, 0, 0)),
            full2(SGU_CHUNK, N_HEADS), full2(KW, D_MODEL), full2(KW, D_MODEL), full2(D_MODEL, D_MODEL)]


def _mix_fwd(p, o_f, o_b, x, mod_c, gna, lng, lnb, w_s, bst, wpa, wpb, wo):
    bs, seq, _ = x.shape
    nbl = seq // ROW_BLOCK

    def body(og_r, u_r, v_r, ga0_r, ga1_r, gb0_r, gb1_r, of_r, ob_r, x_r, mod_r,
             gna_r, lng_r, lnb_r, ws_r, bst_r, wpa_r, wpb_r, wo_r, x1_r):
        ga = jnp.concatenate([ga0_r[0], ga1_r[0]], axis=1)
        gb = jnp.concatenate([gb0_r[0], gb1_r[0]], axis=1)
        t = _mix_values(og_r[0], u_r[0], v_r[0], ga, gb, of_r[0] + ob_r[0], gna_r[...], lng_r[...], lnb_r[...],
                        ws_r, bst_r[...], wpa_r[...], wpb_r[...], wo_r[...])
        x1_r[0] = x_r[0] + mod_r[0, 0:1, :] * t["mix"]

    row = lambda w: pl.BlockSpec((1, ROW_BLOCK, w), lambda b, j: (b, j + 1, 0))
    lat = pl.BlockSpec((1, ROW_BLOCK, D_MODEL), lambda b, j: (b, j, 0))
    return pl.pallas_call(
        body, name="mix_fwd", grid=(bs, nbl),
        in_specs=_mix_in_specs(lambda j: j + 1) + [row(KW), row(KW), lat,
                                                    pl.BlockSpec((1, 8, D_MODEL), lambda b, j: (b, 0, 0))]
        + _mix_param_specs(),
        out_specs=lat, out_shape=SDS(x.shape, F32), compiler_params=_params(2),
    )(p, p, p, p, p, p, p, o_f, o_b, x, mod_c, gna, lng, lnb, w_s, bst, wpa, wpb, wo)


def _mix_bwd(p, o_f, o_b, dx1, mod_c, gna, lng, lnb, w_s, w_s_t, bst, wpa, wpb, wo, carry=None):
    bs, rows, _ = p.shape
    nb = rows // ROW_BLOCK

    def body(og_r, u_r, v_r, ga0_r, ga1_r, gb0_r, gb1_r, of_r, ob_r, dx1_r, mod_r,
             gna_r, lng_r, lnb_r, ws_r, bst_r, wpa_r, wpb_r, wo_r, wst_r,
             dor_r, dpc_r, dwpa_r, dwpb_r, dwo_r, dgna_r, dlng_r, dlnb_r, dws_r, dbst_r, dmod_r):
        b, j = pl.program_id(0), pl.program_id(1)

        @pl.when((b == 0) & (j == 0))
        def _():
            for r in (dwpa_r, dwpb_r, dwo_r, dgna_r, dlng_r, dlnb_r, dws_r, dbst_r):
                r[...] = jnp.zeros_like(r)

        @pl.when(j == 0)
        def _():
            dmod_r[...] = jnp.zeros_like(dmod_r)
            dor_r[...] = jnp.zeros_like(dor_r)
            dpc_r[...] = jnp.zeros_like(dpc_r)

        @pl.when(j > 0)
        def _():
            og, u, v = og_r[0], u_r[0], v_r[0]
            ga = jnp.concatenate([ga0_r[0], ga1_r[0]], axis=1)
            gb = jnp.concatenate([gb0_r[0], gb1_r[0]], axis=1)
            gna, lng = gna_r[...], lng_r[...]
            wpa, wpb, wo = wpa_r[...], wpb_r[...], wo_r[...]
            t = _mix_values(og, u, v, ga, gb, of_r[0] + ob_r[0], gna, lng, lnb_r[...],
                            ws_r, bst_r[...], wpa, wpb, wo)
            dx1 = dx1_r[0]
            dmod_r[0, 0:1, :] += _colsum(dx1 * t["mix"])
            dmix = mod_r[0, 0:1, :] * dx1
            dmerged = _dot_nt(dmix, wo)
            dwo_r[...] += _dot_tn(t["merged"], dmix)
            sa, sb = t["sa"], t["sb"]
            dya, dyb = sa * dmerged, sb * dmerged
            dga = dmerged * t["ya"] * sa * (1.0 - sa)
            dgb = dmerged * t["yb"] * sb * (1.0 - sb)
            do_a = _dot_nt(dya, wpa)
            dwpa_r[...] += _dot_tn(t["o_a"], dya)
            do_bm = _dot_nt(dyb, wpb)
            dwpb_r[...] += _dot_tn(t["o_bm"], dyb)
            sog = t["sog"]
            dog = do_a * t["o_n"] * (sog * (1.0 + og * (1.0 - sog)))
            do_n = do_a * t["silu_og"]
            dxh = do_n * t["gna4"]
            prod = do_n * t["xh"]
            dgna = jnp.zeros((1, HEAD_DIM), F32)
            dor_l = []
            for h in range(N_HEADS):
                ls = slice(h * HEAD_DIM, (h + 1) * HEAD_DIM)
                dgna = dgna + _colsum(prod[:, ls])
                dor_l.append(_rms_bwd(dxh[:, ls], t["xh"][:, ls], t["r"][h]))
            dgna_r[...] += dgna
            dor_r[0] = jnp.concatenate(dor_l, axis=1)
            du = do_bm * t["mixed"] * _dgelu(u, t["tu"])
            dmixed = do_bm * t["gu"]
            vn = t["vn"]
            dvn_chunks = []
            for n in range(ROW_BLOCK // SGU_CHUNK):
                rs = slice(n * SGU_CHUNK, (n + 1) * SGU_CHUNK)
                groups = []
                for g in range(N_HEADS):
                    ls = slice(g * HEAD_DIM, (g + 1) * HEAD_DIM)
                    dm = dmixed[rs, ls]
                    dws_r[g] += _dot_nt(dm, vn[rs, ls])
                    dbst_r[:, g:g + 1] += jnp.sum(dm, axis=1, keepdims=True)
                    groups.append(_dot(wst_r[g], dm))
                dvn_chunks.append(jnp.concatenate(groups, axis=1))
            dvn = jnp.concatenate(dvn_chunks, axis=0)
            xhat = t["xhat"]
            dlng_r[...] += _colsum(dvn * xhat)
            dlnb_r[...] += _colsum(dvn)
            dxhat = dvn * lng
            dgv = t["rstd"] * (dxhat - jnp.mean(dxhat, axis=-1, keepdims=True)
                               - xhat * jnp.mean(dxhat * xhat, axis=-1, keepdims=True))
            dv = dgv * _dgelu(v, t["tv"])
            dpc_r[0] = jnp.concatenate([dog, du, dv, dga, dgb], axis=1).astype(BF16)

    row = lambda w: pl.BlockSpec((1, ROW_BLOCK, w), lambda b, j: (b, j, 0))
    lat = pl.BlockSpec((1, ROW_BLOCK, D_MODEL), lambda b, j: (b, jnp.maximum(j - 1, 0), 0))
    full2 = lambda r, c: pl.BlockSpec((r, c), lambda b, j: (0, 0))
    ws_spec = pl.BlockSpec((N_HEADS, SGU_CHUNK, SGU_CHUNK), lambda b, j: (0, 0, 0))
    return _host_call(
        body, name="mix_bwd", grid=(bs, nb),
        in_specs=_mix_in_specs(lambda j: j) + [row(KW), row(KW), lat,
                                                pl.BlockSpec((1, 8, D_MODEL), lambda b, j: (b, 0, 0))]
        + _mix_param_specs() + [ws_spec],
        out_specs=[row(KW), row(7 * KW), full2(KW, D_MODEL), full2(KW, D_MODEL), full2(D_MODEL, D_MODEL),
                   full2(1, HEAD_DIM), full2(1, KW), full2(1, KW), ws_spec, full2(SGU_CHUNK, N_HEADS),
                   pl.BlockSpec((1, 8, D_MODEL), lambda b, j: (b, 0, 0))],
        out_shape=[SDS((bs, rows, KW), F32), SDS((bs, rows, 7 * KW), BF16), SDS((KW, D_MODEL), F32),
                   SDS((KW, D_MODEL), F32), SDS((D_MODEL, D_MODEL), F32), SDS((1, HEAD_DIM), F32),
                   SDS((1, KW), F32), SDS((1, KW), F32), SDS((N_HEADS, SGU_CHUNK, SGU_CHUNK), F32),
                   SDS((SGU_CHUNK, N_HEADS), F32), SDS((bs, 8, D_MODEL), F32)],
        args=(p, p, p, p, p, p, p, o_f, o_b, dx1, mod_c, gna, lng, lnb, w_s, bst, wpa, wpb, wo, w_s_t), carry=carry)


def _ffn(x1, target, mod_c, g_ffn, g_final, w_up, w_down):
    bs, seq, _ = x1.shape
    nbl = seq // ROW_BLOCK

    def body(x1_r, tg_r, mod_r, gf_r, gl_r, wu_r, wd_r,
             dx1_r, h2_r, dab_r, hid_r, dffn_r, loss_r, dgl_r, dgf_r, dmod_r):
        b, j = pl.program_id(0), pl.program_id(1)

        @pl.when((b == 0) & (j == 0))
        def _():
            for r in (loss_r, dgl_r, dgf_r):
                r[...] = jnp.zeros_like(r)

        @pl.when(j == 0)
        def _():
            dmod_r[...] = jnp.zeros_like(dmod_r)

        x1 = x1_r[0]
        shift, scale, gate = mod_r[0, 1:2, :], mod_r[0, 2:3, :], mod_r[0, 3:4, :]
        gf, gl = gf_r[...], gl_r[...]
        xn2, r2 = _rms(x1)
        hn2 = xn2 * gf
        h2 = (hn2 * (1.0 + scale) + shift).astype(BF16)
        h2_r[0] = h2
        ab = jnp.dot(h2, wu_r[...], preferred_element_type=F32)
        a, bb = ab[:, :D_FF], ab[:, D_FF:]
        sa = _sig(a)
        silu_a = a * sa
        hid = (silu_a * bb).astype(BF16)
        hid_r[0] = hid
        ffn = jnp.dot(hid, wd_r[...], preferred_element_type=F32)
        x2 = x1 + gate * ffn
        xn3, r3 = _rms(x2)
        err = xn3 * gl - tg_r[0]
        loss_r[...] += 0.5 * jnp.sum(jnp.mean(err * err, axis=-1, keepdims=True), axis=0, keepdims=True)
        dy = err * (1.0 / D_MODEL)
        dgl_r[...] += _colsum(dy * xn3)
        dx2 = _rms_bwd(dy * gl, xn3, r3)
        dmod_r[0, 3:4, :] += _colsum(dx2 * ffn)
        dffn = (gate * dx2).astype(BF16)
        dffn_r[0] = dffn
        dhid = lax.dot_general(dffn, wd_r[...], (((1,), (1,)), ((), ())), preferred_element_type=F32)
        da = dhid * bb * (sa * (1.0 + a * (1.0 - sa)))
        db = dhid * silu_a
        dab = jnp.concatenate([da, db], axis=1).astype(BF16)
        dab_r[0] = dab
        dh2 = lax.dot_general(dab, wu_r[...], (((1,), (1,)), ((), ())), preferred_element_type=F32)
        dmod_r[0, 1:2, :] += _colsum(dh2)
        dmod_r[0, 2:3, :] += _colsum(dh2 * hn2)
        dhn2 = dh2 * (1.0 + scale)
        dgf_r[...] += _colsum(dhn2 * xn2)
        dx1_r[0] = dx2 + _rms_bwd(dhn2 * gf, xn2, r2)

    lat = lambda w: pl.BlockSpec((1, ROW_BLOCK, w), lambda b, j: (b, j, 0))
    full2 = lambda r, c: pl.BlockSpec((r, c), lambda b, j: (0, 0))
    mod_spec = pl.BlockSpec((1, 8, D_MODEL), lambda b, j: (b, 0, 0))
    return pl.pallas_call(
        body, name="ffn", grid=(bs, nbl),
        in_specs=[lat(D_MODEL), lat(D_MODEL), mod_spec, full2(1, D_MODEL), full2(1, D_MODEL),
                  full2(D_MODEL, 2 * D_FF), full2(D_FF, D_MODEL)],
        out_specs=[lat(D_MODEL), lat(D_MODEL), lat(2 * D_FF), lat(D_FF), lat(D_MODEL),
                   full2(1, 1), full2(1, D_MODEL), full2(1, D_MODEL), mod_spec],
        out_shape=[SDS(x1.shape, F32), SDS(x1.shape, BF16), SDS((bs, seq, 2 * D_FF), BF16),
                   SDS((bs, seq, D_FF), BF16), SDS(x1.shape, BF16), SDS((1, 1), F32),
                   SDS((1, D_MODEL), F32), SDS((1, D_MODEL), F32), SDS((bs, 8, D_MODEL), F32)],
        compiler_params=_params(2),
    )(x1, target, mod_c, g_ffn, g_final, w_up, w_down)


def _row_tile(rows):
    return next(m * ROW_BLOCK for m in (4, 2, 1) if rows % (m * ROW_BLOCK) == 0)


def _matmul_tn(a, b, n_blocks, tk, name, carry=None):
    t, m = a.shape
    n = b.shape[1]
    tn = n // n_blocks

    def body(a_ref, b_ref, o_ref):
        @pl.when(pl.program_id(1) == 0)
        def _():
            o_ref[...] = jnp.zeros_like(o_ref)
        o_ref[0] += _dot_tn(a_ref[...], b_ref[...])

    (out,), carried = _host_call(
        body, name=name, grid=(n_blocks, t // tk),
        in_specs=[pl.BlockSpec((tk, m), lambda i, k: (k, 0)), pl.BlockSpec((tk, tn), lambda i, k: (k, i))],
        out_specs=[pl.BlockSpec((1, m, tn), lambda i, k: (i, 0, 0))],
        out_shape=[SDS((n_blocks, m, tn), F32)], args=(a, b), carry=carry)
    return out if carry is None else (out, carried)


SMALL_ROWS = 80
ROW_CCTX = 3


def _small_reduce(gathered, lbg):
    def body(g_ref, lbg_ref, s_ref, dgam_ref):
        tot = g_ref[0:SMALL_ROWS, :]
        for dev in range(1, N_DEV):
            tot = tot + g_ref[dev * SMALL_ROWS:(dev + 1) * SMALL_ROWS, :]
        s_ref[...] = tot
        cc = g_ref[ROW_CCTX:ROW_CCTX + 1, :]
        for dev in range(2, N_DEV, 2):
            cc = cc + g_ref[dev * SMALL_ROWS + ROW_CCTX:dev * SMALL_ROWS + ROW_CCTX + 1, :]
        s_ref[ROW_CCTX:ROW_CCTX + 1, :] = cc
        dlb = tot[7:8, :]
        for d in range(2):
            s0 = _sig(lbg_ref[0, d:d + 1, :] - lbg_ref[1, d:d + 1, :])
            dgam_ref[d:d + 1, :] = dlb[:, d * KW:(d + 1) * KW] * s0 * (1.0 - s0)

    return pl.pallas_call(
        body, name="small_reduce", out_shape=[SDS((SMALL_ROWS, D_MODEL), F32), SDS((2, KW), F32)],
        in_specs=[VMEM_SPEC] * 2, out_specs=[VMEM_SPEC] * 2,
    )(gathered, lbg)


def _pad_cols(a, width):
    return jnp.pad(a, ((0, 0), (0, width - a.shape[1])))


def kernel(x, c, ctx, c_ctx, w_mod, b_mod, g_mix, g_ffn, w_in, lb_gamma, g_norm_a, ln_v_g, ln_v_b, w_s, b_s, w_pa, w_pb, w_o, w_up, w_down, g_final, loss_target, m_c_ctx, m_w_mod, m_b_mod, m_g_mix, m_g_ffn, m_w_in, m_lb_gamma, m_g_norm_a, m_ln_v_g, m_ln_v_b, m_w_s, m_b_s, m_w_pa, m_w_pb, m_w_o, m_w_up, m_w_down, m_g_final, v_c_ctx, v_w_mod, v_b_mod, v_g_mix, v_g_ffn, v_w_in, v_lb_gamma, v_g_norm_a, v_ln_v_g, v_ln_v_b, v_w_s, v_b_s, v_w_pa, v_w_pb, v_w_o, v_w_up, v_w_down, v_g_final):
    ax, ay, ac = lax.axis_index("x"), lax.axis_index("y"), lax.axis_index("c")
    kc = 2 * ax + ay
    dev = 2 * kc + ac
    pos = jnp.stack([kc, ac]).astype(jnp.int32)
    bs, seq, _ = x.shape
    assert bs <= 4 and ctx.shape[1] == ROW_BLOCK and seq % ROW_BLOCK == 0
    mod_cols = w_mod.shape[2]

    lbg_row = _pad_cols(lb_gamma.reshape(1, -1), D_MODEL)
    pay1 = jnp.concatenate([c, jnp.zeros((4 - bs, D_MODEL), F32), c_ctx[None, :], lbg_row,
                            jnp.zeros((2, D_MODEL), F32)], axis=0)
    cond64 = _all_gather8(pay1, "gather_cond")
    lbg_full = cond64.reshape(N_DEV, 8, D_MODEL)[0::2, 5, :KW].reshape(N_CHIPS, 2, 2, HEAD_DIM)
    lbg_full = jnp.transpose(lbg_full, (1, 2, 0, 3)).reshape(2, 2, KW)

    b_mod_s = lax.dynamic_slice(b_mod, (0, kc * mod_cols), (1, mod_cols))
    mod_s = _mod_fwd(cond64, w_mod[0], b_mod_s)
    shards = [w_in[0], w_up[0], w_pa[0], w_pb[0], w_o[0], w_down[0]]
    bufs = _cast_bf16(pos, shards)
    mod_g, (w_in_g,) = _all_gather8(mod_s, "gather_mod", carry=_carry_gather_send(bufs[:1]),
                                    then=_carry_gather_forward(bufs[:1]))
    mod_g = mod_g.reshape(N_DEV, 64, mod_cols)[0::2]
    mod_full = jnp.transpose(mod_g, (1, 0, 2)).reshape(64, N_CHIPS * mod_cols)
    mod_mine = lax.dynamic_slice(mod_full, (dev * 8, 0), (8, 6 * D_MODEL)).reshape(8, 6, D_MODEL)
    mod, mc = mod_mine[:bs], mod_mine[4]
    zeros4 = jnp.zeros((bs, 4, D_MODEL), F32)
    mod_a = jnp.concatenate([mod[:, 0:2], jnp.broadcast_to(mc[None, 0:2], (bs, 2, D_MODEL)), zeros4], axis=1)
    mod_c = jnp.concatenate([mod[:, 2:6], zeros4], axis=1)

    def cols_major(a):
        return jnp.transpose(a, (1, 0, 2)).reshape(a.shape[1], -1)

    gna, lng, lnb = g_norm_a, ln_v_g, ln_v_b
    ws3 = w_s[0]
    ws3_t = jnp.transpose(ws3, (0, 2, 1))
    bst = jnp.transpose(b_s[0])

    w_in_f = cols_major(w_in_g)
    (p, h_all), sent = _in_fwd(x, ctx, g_mix, mod_a, w_in_f, carry=_carry_gather_send(bufs[1:]))
    (o_f, s_f), gathered = _hgrn_fwd(p, lbg_full, 0, carry=_carry_gather_forward(sent))
    (o_b, s_b), _ = _hgrn_fwd(p, lbg_full, 1)
    w_up_f, w_pa_f, w_pb_f = (cols_major(a) for a in gathered[:3])
    w_o_f = gathered[3].reshape(-1, D_MODEL)
    w_down_f = gathered[4].reshape(-1, D_MODEL)
    x1 = _mix_fwd(p, o_f, o_b, x, mod_c, gna, lng, lnb, ws3, bst, w_pa_f, w_pb_f, w_o_f)
    dx1, h2, dab, hid, dffn, loss_part, dg_final, dg_ffn, dmod_ffn = _ffn(
        x1, loss_target, mod_c, g_ffn, g_final[None, :], w_up_f, w_down_f)
    rows_lat = bs * seq
    tk_lat = _row_tile(rows_lat)
    dw_up = _matmul_tn(h2.reshape(rows_lat, D_MODEL), dab.reshape(rows_lat, 2 * D_FF), N_CHIPS, tk_lat, "dw_up")
    dw_down = _matmul_tn(hid.reshape(rows_lat, D_FF), dffn.reshape(rows_lat, D_MODEL), 1, tk_lat, "dw_down")

    def shard_major(a):
        return jnp.transpose(a.reshape(a.shape[0], N_CHIPS, -1), (1, 0, 2))

    def add_halves(parts, recvs, names):
        sums = [_rs_add_halves(pos, g, r, "rs_add_" + nm) for g, r, nm in zip(parts, recvs, names)]
        return [s[0] for s in sums], [s[1] for s in sums]

    def sum_owner(cp32s, recvs, names):
        return [_rs_sum_owner(pos, a, r, "rs_sum_" + nm) for a, r, nm in zip(cp32s, recvs, names)]

    ffn_names, mix_names = ["w_up", "w_down"], ["w_pa", "w_pb", "w_o"]
    part_ffn = [dw_up, dw_down.reshape(N_CHIPS, -1, D_MODEL)]
    (do_raw, dpc, dw_pa, dw_pb, dw_o, dgna, dlng, dlnb, dws, dbst, dmod_mix), sib_ffn = _mix_bwd(
        p, o_f, o_b, dx1, mod_c, gna, lng, lnb, ws3, ws3_t, bst, w_pa_f, w_pb_f, w_o_f,
        carry=_carry_sibling_halves(part_ffn))
    cp32_ffn, cpbf_ffn = add_halves(part_ffn, sib_ffn, ffn_names)
    part_mix = [shard_major(dw_pa), shard_major(dw_pb), dw_o.reshape(N_CHIPS, -1, D_MODEL)]
    (df_f, dq0, dv0, dlb0), got = _hgrn_bwd(
        p, lbg_full, s_f, do_raw, 0,
        carry=_merge_carries(_carry_to_owner(cpbf_ffn), _carry_sibling_halves(part_mix)))
    own_ffn, sib_mix = got[:2], got[2:]
    half_ffn = sum_owner(cp32_ffn, own_ffn, ffn_names)
    cp32_mix, cpbf_mix = add_halves(part_mix, sib_mix, mix_names)
    (df_b, dq, di, dlb1), got = _hgrn_bwd(
        p, lbg_full, s_b, do_raw, 1, dq0, dv0,
        carry=_merge_carries(_carry_join_halves(half_ffn), _carry_to_owner(cpbf_mix)))
    g_ffn_w, own_mix = got[:2], got[2:]
    half_mix = sum_owner(cp32_mix, own_mix, mix_names)
    (grad_x, dp, dg_mix, dmod_in), _ = _in_bwd(x, ctx, dx1, g_mix, mod_a, w_in_f, df_f, df_b, di, dq, dpc)

    rows_all = dp.shape[0] * dp.shape[1]
    tk_all = _row_tile(rows_all)
    dw_in, g_mix_w = _matmul_tn(h_all.reshape(rows_all, D_MODEL), dp.reshape(rows_all, IN_COLS), N_CHIPS, tk_all,
                                "dw_in", carry=_carry_join_halves(half_mix))

    dmod_mine = jnp.concatenate([dmod_in[:, 0], dmod_in[:, 1], dmod_mix[:, 0], dmod_ffn[:, 1], dmod_ffn[:, 2],
                                 dmod_ffn[:, 3]], axis=1)
    dmc = jnp.concatenate([jnp.sum(dmod_in[:, 2], axis=0), jnp.sum(dmod_in[:, 3], axis=0),
                           jnp.zeros((4 * D_MODEL,), F32)])[None, :]
    pay3 = jnp.concatenate([dmod_mine, jnp.zeros((4 - bs, 6 * D_MODEL), F32), dmc,
                            jnp.zeros((3, 6 * D_MODEL), F32)], axis=0)
    dmod64, sib_in = _all_gather8(pay3, "gather_dmod", carry=_carry_sibling_halves([dw_in]))
    cp32_in, cpbf_in = add_halves([dw_in], sib_in, ["w_in"])
    dmod64_my = lax.dynamic_slice(dmod64, (0, kc * mod_cols), (64, mod_cols))
    g_w_mod, g_b_mod, g_cctx_part = _mod_bwd(cond64, dmod64, dmod64_my, w_mod[0], c_ctx[None, :])

    def row(*parts):
        return _pad_cols(jnp.concatenate([q.reshape(1, -1) for q in parts], axis=1), D_MODEL)

    small_rows = [dg_mix, dg_ffn, dg_final, g_cctx_part, row(dgna), row(dlng, dlnb), row(jnp.transpose(dbst)),
                  row(dlb0, dlb1), row(loss_part), jnp.zeros((7, D_MODEL), F32), dws.reshape(64, D_MODEL)]
    pay4 = jnp.concatenate(small_rows, axis=0)
    small64, own_in = _all_gather8(pay4, "gather_small", carry=_carry_to_owner(cpbf_in))
    tot, dgam0 = _small_reduce(small64, lbg_full)

    rest_names = ["w_up", "w_pa", "w_pb", "w_o", "w_down", "w_mod"]
    rest_w = shards[1:] + [w_mod[0]]
    rest_g = [g_ffn_w[0], g_mix_w[0], g_mix_w[1], g_mix_w[2], g_ffn_w[1], g_w_mod]
    rest_m = [m_w_up[0], m_w_pa[0], m_w_pb[0], m_w_o[0], m_w_down[0], m_w_mod[0]]
    rest_v = [v_w_up[0], v_w_pa[0], v_w_pb[0], v_w_o[0], v_w_down[0], v_w_mod[0]]
    (ds_r, m2s_r, v2s_r), _ = _adamw_group(rest_w, rest_g, rest_m, rest_v, "adamw_rest")
    res = {}
    for name, g, d, m2, v2 in zip(rest_names, rest_g, ds_r, m2s_r, v2s_r):
        res[name] = (g[None], d[None], m2[None], v2[None])
    g_in_w = _comm_call("rs_join_w_in", _carry_join_halves(sum_owner(cp32_in, own_in, ["w_in"])))
    d, m2, v2 = _adamw_big(shards[0], g_in_w[0], m_w_in[0], v_w_in[0], "adamw_w_in")
    res["w_in"] = (g_in_w[0][None], d[None], m2[None], v2[None])

    loss = tot[8, 0]
    dgam_full = jnp.stack([dgam0, -dgam0])
    g_lbg = lax.dynamic_slice(dgam_full, (0, 0, kc * HEAD_DIM), (2, 2, HEAD_DIM))

    small = [
        ("c_ctx", c_ctx[None, :], tot[3:4], m_c_ctx, v_c_ctx),
        ("b_mod", b_mod, g_b_mod, m_b_mod, v_b_mod),
        ("g_mix", g_mix, tot[0:1], m_g_mix, v_g_mix),
        ("g_ffn", g_ffn, tot[1:2], m_g_ffn, v_g_ffn),
        ("lb_gamma", lb_gamma.reshape(4, HEAD_DIM), g_lbg.reshape(4, HEAD_DIM), m_lb_gamma, v_lb_gamma),
        ("g_norm_a", g_norm_a, tot[4:5, :HEAD_DIM], m_g_norm_a, v_g_norm_a),
        ("ln_v_g", ln_v_g, tot[5:6, :KW], m_ln_v_g, v_ln_v_g),
        ("ln_v_b", ln_v_b, tot[5:6, KW:], m_ln_v_b, v_ln_v_b),
        ("w_s", w_s.reshape(N_HEADS * SGU_CHUNK, SGU_CHUNK), tot[16:80].reshape(N_HEADS * SGU_CHUNK, SGU_CHUNK),
         m_w_s, v_w_s),
        ("b_s", b_s[0], tot[6:7, :KW].reshape(N_HEADS, SGU_CHUNK), m_b_s, v_b_s),
        ("g_final", g_final[None, :], tot[2:3], m_g_final, v_g_final),
    ]
    ws_, gs_ = [s[1] for s in small], [s[2] for s in small]
    ms_ = [s[3].reshape(s[1].shape) for s in small]
    vs_ = [s[4].reshape(s[1].shape) for s in small]
    ds_, m2s_, v2s_ = _adamw_small(ws_, gs_, ms_, vs_)
    for (name, _, g, m, _), d, m2, v2 in zip(small, ds_, m2s_, v2s_):
        res[name] = tuple(t.reshape(m.shape) for t in (g, d, m2, v2))

    order = ["c_ctx", "w_mod", "b_mod", "g_mix", "g_ffn", "w_in", "lb_gamma", "g_norm_a", "ln_v_g", "ln_v_b",
             "w_s", "b_s", "w_pa", "w_pb", "w_o", "w_up", "w_down", "g_final"]
    outs = [loss, grad_x]
    for part in range(4):
        outs += [res[n][part] for n in order]
    return tuple(outs)
```

```python
import functools
import math

import jax
import jax.numpy as jnp
import numpy as np
from jax import lax
from jax.experimental import pallas as pl
from jax.experimental.pallas import tpu as pltpu

F32 = jnp.float32
BF16 = jnp.bfloat16
SDS = jax.ShapeDtypeStruct
MESH = pl.DeviceIdType.MESH

EPS = 1e-6
D_MODEL = 1024
N_HEADS = 4
HEAD_DIM = 128
KW = N_HEADS * HEAD_DIM
IN_COLS = 11 * KW
D_FF = 2816
HGRN_CHUNK = 64
SGU_CHUNK = 128
ROW_BLOCK = 256
N_CHIPS = 4
N_DEV = 8
V7X_VMEM_BYTES = 64 * 1024 * 1024
VMEM_LIMIT = V7X_VMEM_BYTES - 6 * 1024 * 1024

ADAM_LR, ADAM_B1, ADAM_B2, ADAM_EPS, ADAM_WD, ADAM_STEP = 0.001, 0.9, 0.999, 1e-08, 0.01, 10
GELU_C0 = math.sqrt(2.0 / math.pi)
GELU_C1 = 0.044715

VMEM_SPEC = pl.BlockSpec(memory_space=pltpu.VMEM)
ANY_SPEC = pl.BlockSpec(memory_space=pl.ANY)


def _params(n_grid):
    return pltpu.CompilerParams(dimension_semantics=("arbitrary",) * n_grid, vmem_limit_bytes=VMEM_LIMIT)


def _sig(x):
    return 0.5 * jnp.tanh(0.5 * x) + 0.5


def _gelu(x):
    t = jnp.tanh(GELU_C0 * (x + GELU_C1 * x * x * x))
    return 0.5 * x * (1.0 + t), t


def _dgelu(x, t):
    return 0.5 * (1.0 + t) + 0.5 * x * (1.0 - t * t) * GELU_C0 * (1.0 + 3.0 * GELU_C1 * x * x)


def _dot(a, b):
    return jnp.dot(a.astype(BF16), b.astype(BF16), preferred_element_type=F32)


def _dot_nt(a, b):
    return lax.dot_general(a.astype(BF16), b.astype(BF16), (((1,), (1,)), ((), ())), preferred_element_type=F32)


def _dot_tn(a, b):
    return lax.dot_general(a.astype(BF16), b.astype(BF16), (((0,), (0,)), ((), ())), preferred_element_type=F32)


def _dot_f32(a, b, dims=(((1,), (0,)), ((), ()))):
    return lax.dot_general(a, b, dims, precision=lax.Precision.HIGHEST, preferred_element_type=F32)


def _rms(x):
    r = lax.rsqrt(jnp.mean(x * x, axis=-1, keepdims=True) + EPS)
    return x * r, r


def _rms_bwd(dxn, xn, r):
    return r * (dxn - xn * jnp.mean(dxn * xn, axis=-1, keepdims=True))


def _colsum(a):
    return jnp.sum(a, axis=0, keepdims=True)


def _tri(n, upper):
    t = lax.broadcasted_iota(jnp.int32, (n, n), 0)
    s = lax.broadcasted_iota(jnp.int32, (n, n), 1)
    return (s >= t) if upper else (s <= t)


def _all_gather8(x_shard, name, carry=None, then=None):
    m_per, n = x_shard.shape
    n_ci = len(carry.ins) if carry else 0
    n_co = len(carry.outs) if carry else 0
    n_cs = len(carry.sems) if carry else 0

    def body(*refs):
        x_ref, cins = refs[0], refs[1:1 + n_ci]
        out_ref, couts = refs[1 + n_ci], refs[2 + n_ci:2 + n_ci + n_co]
        send_sems, recv_sems, local_sem = refs[2 + n_ci + n_co:5 + n_ci + n_co]
        csems = refs[5 + n_ci + n_co:5 + n_ci + n_co + n_cs]
        tsems = refs[5 + n_ci + n_co + n_cs:]
        if carry:
            _start_all(carry.copies(cins, couts, csems)[0])
        _gather8_body(x_ref, out_ref, send_sems, recv_sems, local_sem, m_per)
        if carry:
            _wait_all(carry.copies(cins, couts, csems)[1])
        if then:
            _start_all(then.copies(couts, couts, tsems)[0])
            _wait_all(then.copies(couts, couts, tsems)[1])

    res = pl.pallas_call(
        body, name=name, out_shape=[SDS((N_DEV * m_per, n), x_shard.dtype)] + (carry.outs if carry else []),
        in_specs=[VMEM_SPEC] + [ANY_SPEC] * n_ci, out_specs=[VMEM_SPEC] + [ANY_SPEC] * n_co,
        input_output_aliases={1 + i: 1 + o for i, o in carry.alias.items()} if carry else {},
        scratch_shapes=[pltpu.SemaphoreType.DMA((7,)), pltpu.SemaphoreType.DMA((7,)), pltpu.SemaphoreType.DMA]
        + (carry.sems if carry else []) + (then.sems if then else []),
    )(x_shard, *(carry.ins if carry else []))
    return res[0] if carry is None else (res[0], list(res[1:]))


def _gather8_body(x_ref, out_ref, send_sems, recv_sems, local_sem, m_per):
    x, y, c = lax.axis_index("x"), lax.axis_index("y"), lax.axis_index("c")
    me, sibling = (x, y, c), (x, y, 1 - c)
    chips = [(1 - x, y), (x, 1 - y), (1 - x, 1 - y)]

    def rows(px, py, pc):
        return out_ref.at[pl.ds((4 * px + 2 * py + pc) * m_per, m_per), :]

    def copy(k, block, to, src=None):
        return pltpu.make_async_remote_copy(
            src_ref=rows(*block) if src is None else src, dst_ref=rows(*block),
            send_sem=send_sems.at[k], recv_sem=recv_sems.at[k], device_id=to, device_id_type=MESH)

    mine = pltpu.make_async_copy(x_ref, rows(*me), local_sem)
    mine.start()
    first = [copy(0, me, sibling, src=x_ref)]
    first += [copy(1 + j, me, (*chip, c), src=x_ref) for j, chip in enumerate(chips)]
    for cp in first:
        cp.start()
    passed = [copy(4 + j, (*chip, c), sibling) for j, chip in enumerate(chips)]
    for j, chip in enumerate(chips):
        copy(1 + j, (*chip, c), me).wait_recv()
        passed[j].start()
    copy(0, sibling, me).wait_recv()
    for j, chip in enumerate(chips):
        copy(4 + j, (*chip, 1 - c), me).wait_recv()
    for cp in first + passed:
        cp.wait_send()
    mine.wait()


def _mesh_pos():
    x, y, c = lax.axis_index("x"), lax.axis_index("y"), lax.axis_index("c")
    chips = [(1 - x, y), (x, 1 - y), (1 - x, 1 - y)]
    return x, y, c, 2 * x + y, (x, y, 1 - c), chips


def _half_rows(c, rh):
    return pl.ds(pl.multiple_of(c * rh, 16), rh)


class _Carry:
    def __init__(self, ins, outs, alias, sems, copies):
        self.ins, self.outs, self.alias, self.sems, self.copies = list(ins), list(outs), dict(alias), list(sems), copies


def _remote(src, dst, send, recv, to):
    return functools.partial(pltpu.make_async_remote_copy, src_ref=src, dst_ref=dst, send_sem=send, recv_sem=recv,
                             device_id=to, device_id_type=MESH)


def _carry_gather_send(bufs):
    n = len(bufs)

    def copies(ins, outs, sems):
        x, y, c, kc, sibling, chips = _mesh_pos()
        starts, waits = [], []
        for wi in range(n):
            rh = outs[wi].shape[1] // 2
            for jj, chip in enumerate(chips):
                mine = outs[wi].at[kc, _half_rows(c, rh), :]
                cp = _remote(mine, mine, sems[0].at[wi, jj], sems[1].at[wi, jj], (*chip, c))
                starts.append(cp)
                waits.append((cp, "send"))
                theirs = outs[wi].at[2 * chip[0] + chip[1], _half_rows(c, rh), :]
                waits.append((_remote(theirs, theirs, sems[0].at[wi, jj], sems[1].at[wi, jj], (*chip, c)), "recv"))
        return starts, waits

    return _Carry(bufs, [SDS(b.shape, b.dtype) for b in bufs], {i: i for i in range(n)},
                  [pltpu.SemaphoreType.DMA((n, 3)), pltpu.SemaphoreType.DMA((n, 3))], copies)


def _carry_gather_forward(bufs):
    n = len(bufs)

    def copies(ins, outs, sems):
        x, y, c, kc, sibling, chips = _mesh_pos()
        starts, waits = [], []
        for wi in range(n):
            rh = outs[wi].shape[1] // 2
            for jj, chip in enumerate(chips):
                got = outs[wi].at[2 * chip[0] + chip[1], _half_rows(c, rh), :]
                cp = _remote(got, got, sems[0].at[wi, jj], sems[1].at[wi, jj], sibling)
                starts.append(cp)
                waits.append((cp, "send"))
                other = outs[wi].at[2 * chip[0] + chip[1], _half_rows(1 - c, rh), :]
                waits.append((_remote(other, other, sems[0].at[wi, jj], sems[1].at[wi, jj], sibling), "recv"))
        return starts, waits

    return _Carry(bufs, [SDS(b.shape, b.dtype) for b in bufs], {i: i for i in range(n)},
                  [pltpu.SemaphoreType.DMA((n, 3)), pltpu.SemaphoreType.DMA((n, 3))], copies)


def _carry_sibling_halves(grads):
    n = len(grads)

    def copies(ins, outs, sems):
        x, y, c, kc, sibling, chips = _mesh_pos()
        cps = [_remote(ins[wi].at[:, _half_rows(1 - c, ins[wi].shape[1] // 2), :], outs[wi],
                       sems[0].at[wi], sems[1].at[wi], sibling) for wi in range(n)]
        return cps, [(cp, "both") for cp in cps]

    return _Carry(grads, [SDS((N_CHIPS, g.shape[1] // 2, g.shape[2]), F32) for g in grads], {},
                  [pltpu.SemaphoreType.DMA((n,)), pltpu.SemaphoreType.DMA((n,))], copies)


def _carry_to_owner(cpbfs):
    n = len(cpbfs)

    def copies(ins, outs, sems):
        x, y, c, kc, sibling, chips = _mesh_pos()
        starts, waits = [], []
        for wi in range(n):
            for jj, chip in enumerate(chips):
                cp = _remote(ins[wi].at[2 * chip[0] + chip[1]], outs[wi].at[kc],
                             sems[0].at[wi, jj], sems[1].at[wi, jj], (*chip, c))
                starts.append(cp)
                waits.append((cp, "send"))
                slot = outs[wi].at[2 * chip[0] + chip[1]]
                waits.append((_remote(slot, slot, sems[0].at[wi, jj], sems[1].at[wi, jj], (*chip, c)), "recv"))
        return starts, waits

    return _Carry(cpbfs, [SDS(g.shape, BF16) for g in cpbfs], {},
                  [pltpu.SemaphoreType.DMA((n, 3)), pltpu.SemaphoreType.DMA((n, 3))], copies)


def _carry_join_halves(bufs):
    n = len(bufs)

    def copies(ins, outs, sems):
        x, y, c, kc, sibling, chips = _mesh_pos()
        cps = []
        for wi in range(n):
            mine = outs[wi].at[_half_rows(c, outs[wi].shape[0] // 2), :]
            cps.append(_remote(mine, mine, sems[0].at[wi], sems[1].at[wi], sibling))
        return cps, [(cp, "both") for cp in cps]

    return _Carry(bufs, [SDS(b.shape, F32) for b in bufs], {i: i for i in range(n)},
                  [pltpu.SemaphoreType.DMA((n,)), pltpu.SemaphoreType.DMA((n,))], copies)


def _merge_carries(*carries):
    ins, outs, alias, sems, parts = [], [], {}, [], []
    for cy in carries:
        parts.append((len(ins), len(cy.ins), len(outs), len(cy.outs), len(sems), len(cy.sems), cy.copies))
        alias.update({len(ins) + i: len(outs) + o for i, o in cy.alias.items()})
        ins += cy.ins
        outs += cy.outs
        sems += cy.sems

    def copies(i, o, s):
        starts, waits = [], []
        for i0, ni, o0, no, s0, ns, fn in parts:
            st, wt = fn(i[i0:i0 + ni], o[o0:o0 + no], s[s0:s0 + ns])
            starts += st
            waits += wt
        return starts, waits

    return _Carry(ins, outs, alias, sems, copies)


def _start_all(starts):
    for cp in starts:
        cp().start()


def _wait_all(waits):
    for cp, which in waits:
        if which == "send":
            cp().wait_send()
        elif which == "recv":
            cp().wait_recv()
        else:
            cp().wait()


def _comm_call(name, carry):
    n_i, n_o = len(carry.ins), len(carry.outs)

    def body(*refs):
        ins, outs, sems = refs[:n_i], refs[n_i:n_i + n_o], refs[n_i + n_o:]
        _start_all(carry.copies(ins, outs, sems)[0])
        _wait_all(carry.copies(ins, outs, sems)[1])

    return pl.pallas_call(
        body, name=name, out_shape=carry.outs, in_specs=[ANY_SPEC] * n_i, out_specs=[ANY_SPEC] * n_o,
        input_output_aliases=carry.alias, scratch_shapes=carry.sems,
    )(*carry.ins)


def _host_call(body, *, name, grid, in_specs, out_specs, out_shape, args, scratch_shapes=(), carry=None):
    n_in, n_out, n_scr = len(in_specs), len(out_specs), len(scratch_shapes)
    if carry is None:
        res = pl.pallas_call(body, name=name, grid=grid, in_specs=in_specs, out_specs=out_specs, out_shape=out_shape,
                             scratch_shapes=list(scratch_shapes), compiler_params=_params(len(grid)))(*args)
        return list(res), []
    n_ci, n_co = len(carry.ins), len(carry.outs)

    def wrapped(*refs):
        ins, cins = refs[:n_in], refs[n_in:n_in + n_ci]
        o0 = n_in + n_ci
        outs, couts = refs[o0:o0 + n_out], refs[o0 + n_out:o0 + n_out + n_co]
        s0 = o0 + n_out + n_co
        scr, sems = refs[s0:s0 + n_scr], refs[s0 + n_scr:]
        idx = [pl.program_id(a) for a in range(len(grid))]
        first = functools.reduce(jnp.logical_and, [i == 0 for i in idx])
        last = functools.reduce(jnp.logical_and, [i == g - 1 for i, g in zip(idx, grid)])

        @pl.when(first)
        def _():
            _start_all(carry.copies(cins, couts, sems)[0])

        body(*ins, *outs, *scr)

        @pl.when(last)
        def _():
            _wait_all(carry.copies(cins, couts, sems)[1])

    res = pl.pallas_call(
        wrapped, name=name, grid=grid, in_specs=list(in_specs) + [ANY_SPEC] * n_ci,
        out_specs=list(out_specs) + [ANY_SPEC] * n_co, out_shape=list(out_shape) + carry.outs,
        scratch_shapes=list(scratch_shapes) + carry.sems,
        input_output_aliases={n_in + i: n_out + o for i, o in carry.alias.items()},
        compiler_params=_params(len(grid)),
    )(*args, *carry.ins)
    return list(res[:n_out]), list(res[n_out:])


def _rs_add_halves(pos, grad, recv, name):
    _, rs, cs = grad.shape
    rh = rs // 2
    rb = rh // 2

    def body(pos_ref, g_ref, r_ref, o32_ref, obf_ref):
        s = g_ref[...] + r_ref[...]
        o32_ref[...] = s
        obf_ref[...] = s.astype(BF16)

    blk = (1, rb, cs)
    return pl.pallas_call(
        body, name=name,
        grid_spec=pltpu.PrefetchScalarGridSpec(
            num_scalar_prefetch=1, grid=(N_CHIPS, 2),
            in_specs=[pl.BlockSpec(blk, lambda k, i, p: (k, p[1] * 2 + i, 0)),
                      pl.BlockSpec(blk, lambda k, i, p: (k, i, 0))],
            out_specs=[pl.BlockSpec(blk, lambda k, i, p: (k, i, 0)), pl.BlockSpec(blk, lambda k, i, p: (k, i, 0))]),
        out_shape=[SDS((N_CHIPS, rh, cs), F32), SDS((N_CHIPS, rh, cs), BF16)],
        compiler_params=_params(2),
    )(pos, grad, recv)


def _rs_sum_owner(pos, cp32, recv3, name):
    _, rh, cs = cp32.shape
    rb = rh // 2

    def body(pos_ref, own_ref, r1_ref, r2_ref, r3_ref, o_ref):
        o_ref[...] = ((own_ref[0] + r1_ref[0].astype(F32)) + r2_ref[0].astype(F32)) + r3_ref[0].astype(F32)

    blk = (1, rb, cs)

    def slot(d):
        return pl.BlockSpec(blk, lambda i, p: ((p[0] + d) % N_CHIPS, i, 0))

    return pl.pallas_call(
        body, name=name,
        grid_spec=pltpu.PrefetchScalarGridSpec(
            num_scalar_prefetch=1, grid=(2,),
            in_specs=[slot(0), slot(1), slot(2), slot(3)],
            out_specs=pl.BlockSpec((rb, cs), lambda i, p: (p[1] * 2 + i, 0))),
        out_shape=SDS((2 * rh, cs), F32),
        compiler_params=_params(1),
    )(pos, cp32, recv3, recv3, recv3)


def _cast_bf16(pos, arrs):
    n = len(arrs)

    def body(pos_ref, *refs):
        for i in range(n):
            refs[n + i][0] = refs[i][...].astype(BF16)

    return pl.pallas_call(
        body, name="cast_bf16",
        grid_spec=pltpu.PrefetchScalarGridSpec(
            num_scalar_prefetch=1, grid=(2,),
            in_specs=[pl.BlockSpec((a.shape[0] // 2, a.shape[1]), lambda i, p: (i, 0)) for a in arrs],
            out_specs=[pl.BlockSpec((1, a.shape[0] // 2, a.shape[1]), lambda i, p: (p[0], i, 0)) for a in arrs]),
        out_shape=[SDS((N_CHIPS,) + a.shape, BF16) for a in arrs],
        compiler_params=_params(1),
    )(pos, *arrs)


def _adamw_vals(w, g, m, v):
    m2 = ADAM_B1 * m + (1.0 - ADAM_B1) * g
    v2 = ADAM_B2 * v + (1.0 - ADAM_B2) * (g * g)
    m_hat = m2 / (1.0 - ADAM_B1 ** ADAM_STEP)
    v_hat = v2 / (1.0 - ADAM_B2 ** ADAM_STEP)
    delta = -ADAM_LR * (m_hat / (jnp.sqrt(v_hat) + ADAM_EPS) + ADAM_WD * w)
    return delta, m2, v2


def _adamw_big(w, g, m, v, name):
    rows, cols = w.shape
    rb = rows // 4

    def body(w_ref, g_ref, m_ref, v_ref, d_ref, m2_ref, v2_ref):
        d, m2, v2 = _adamw_vals(w_ref[...], g_ref[...], m_ref[...], v_ref[...])
        d_ref[...] = d
        m2_ref[...] = m2
        v2_ref[...] = v2

    spec = pl.BlockSpec((rb, cols), lambda i: (i, 0))
    return pl.pallas_call(
        body, name=name, grid=(4,), in_specs=[spec] * 4, out_specs=[spec] * 3,
        out_shape=[SDS(w.shape, F32)] * 3, compiler_params=_params(1),
    )(w, g, m, v)


ADAMW_GROUP_STEPS = 8


def _adamw_group(ws, gs, ms, vs, name, carry=None):
    n = len(ws)

    def body(*refs):
        for i in range(n):
            d, m2, v2 = _adamw_vals(refs[i][...], refs[n + i][...], refs[2 * n + i][...], refs[3 * n + i][...])
            refs[4 * n + i][...] = d
            refs[5 * n + i][...] = m2
            refs[6 * n + i][...] = v2

    specs = [pl.BlockSpec((w.shape[0] // ADAMW_GROUP_STEPS, w.shape[1]), lambda i: (i, 0)) for w in ws]
    shapes = [SDS(w.shape, F32) for w in ws]
    outs, carried = _host_call(
        body, name=name, grid=(ADAMW_GROUP_STEPS,), in_specs=specs * 4, out_specs=specs * 3, out_shape=shapes * 3,
        args=(*ws, *gs, *ms, *vs), carry=carry)
    return (outs[:n], outs[n:2 * n], outs[2 * n:]), carried


def _adamw_small(ws, gs, ms, vs):
    n = len(ws)

    def body(*refs):
        for i in range(n):
            d, m2, v2 = _adamw_vals(refs[i][...], refs[n + i][...], refs[2 * n + i][...], refs[3 * n + i][...])
            refs[4 * n + i][...] = d
            refs[5 * n + i][...] = m2
            refs[6 * n + i][...] = v2

    shapes = [SDS(w.shape, F32) for w in ws]
    outs = pl.pallas_call(
        body, name="adamw_small", out_shape=shapes * 3,
        in_specs=[VMEM_SPEC] * (4 * n), out_specs=[VMEM_SPEC] * (3 * n),
    )(*ws, *gs, *ms, *vs)
    return outs[:n], outs[n:2 * n], outs[2 * n:]


def _mod_fwd(cond64, w_mod_s, b_mod_s):
    def body(c_ref, w_ref, b_ref, o_ref):
        cc = c_ref[...]
        o_ref[...] = _dot_f32(cc * _sig(cc), w_ref[...]) + b_ref[...]

    return pl.pallas_call(
        body, name="mod_fwd", out_shape=SDS((cond64.shape[0], w_mod_s.shape[1]), F32),
        in_specs=[VMEM_SPEC] * 3, out_specs=VMEM_SPEC,
        compiler_params=pltpu.CompilerParams(vmem_limit_bytes=VMEM_LIMIT),
    )(cond64, w_mod_s, b_mod_s)


def _mod_bwd(cond64, dmod64, dmod64_my, w_mod_s, c_ctx):
    def body(c_ref, g_ref, gm_ref, w_ref, cc_ref, gw_ref, gb_ref, gcc_ref):
        cc = c_ref[...]
        act = cc * _sig(cc)
        gm = gm_ref[...]
        gw_ref[...] = _dot_f32(act, gm, (((0,), (0,)), ((), ())))
        gb_ref[...] = _colsum(g_ref[...])
        dact = _dot_f32(gm, w_ref[...], (((1,), (1,)), ((), ())))
        tot = dact[4:5, :]
        for dev in range(1, N_DEV):
            tot = tot + dact[8 * dev + 4:8 * dev + 5, :]
        c0 = cc_ref[...]
        s0 = _sig(c0)
        gcc_ref[...] = tot * (s0 * (1.0 + c0 * (1.0 - s0)))

    return pl.pallas_call(
        body, name="mod_bwd",
        out_shape=[SDS(w_mod_s.shape, F32), SDS((1, dmod64.shape[1]), F32), SDS((1, D_MODEL), F32)],
        in_specs=[VMEM_SPEC] * 5, out_specs=[VMEM_SPEC] * 3,
        compiler_params=pltpu.CompilerParams(vmem_limit_bytes=VMEM_LIMIT),
    )(cond64, dmod64, dmod64_my, w_mod_s, c_ctx)


def _in_fwd(x, ctx, g_mix, mod_a, w_in, carry=None):
    bs, seq, _ = x.shape
    nb = seq // ROW_BLOCK + 1

    def body(x_ref, ctx_ref, g_ref, mod_ref, w_ref, p_ref, h_ref):
        is_ctx = pl.program_id(1) == 0
        xin = jnp.where(is_ctx, ctx_ref[0], x_ref[0])
        shift = jnp.where(is_ctx, mod_ref[0, 2:3, :], mod_ref[0, 0:1, :])
        scale = jnp.where(is_ctx, mod_ref[0, 3:4, :], mod_ref[0, 1:2, :])
        xn, _ = _rms(xin)
        hb = ((xn * g_ref[...]) * (1.0 + scale) + shift).astype(BF16)
        h_ref[0] = hb
        p_ref[0] = jnp.dot(hb, w_ref[...], preferred_element_type=F32)

    return _host_call(
        body, name="in_fwd", grid=(bs, nb),
        in_specs=[pl.BlockSpec((1, ROW_BLOCK, D_MODEL), lambda b, j: (b, jnp.maximum(j - 1, 0), 0)),
                  pl.BlockSpec((1, ROW_BLOCK, D_MODEL), lambda b, j: (b, 0, 0)),
                  pl.BlockSpec((1, D_MODEL), lambda b, j: (0, 0)),
                  pl.BlockSpec((1, 8, D_MODEL), lambda b, j: (b, 0, 0)),
                  pl.BlockSpec((D_MODEL, IN_COLS), lambda b, j: (0, 0))],
        out_specs=[pl.BlockSpec((1, ROW_BLOCK, IN_COLS), lambda b, j: (b, j, 0)),
                   pl.BlockSpec((1, ROW_BLOCK, D_MODEL), lambda b, j: (b, j, 0))],
        out_shape=[SDS((bs, nb * ROW_BLOCK, IN_COLS), F32), SDS((bs, nb * ROW_BLOCK, D_MODEL), BF16)],
        args=(x, ctx, g_mix, mod_a, w_in), carry=carry)


def _in_bwd(x, ctx, dx1, g_mix, mod_a, w_in, df_f, df_b, dv_f, dv_b, dq_f, dq_b, dpc, carry=None):
    bs, seq, _ = x.shape
    nb = seq // ROW_BLOCK + 1

    def body(x_ref, ctx_ref, dx1_ref, g_ref, mod_ref, w_ref, dff_ref, dfb_ref, dvf_ref, dvb_ref, dqf_ref, dqb_ref,
             dpc_ref, gx_ref, dp_ref, dg_ref, dmod_ref):
        b, j = pl.program_id(0), pl.program_id(1)
        is_ctx = j == 0

        @pl.when((b == 0) & (j == 0))
        def _():
            dg_ref[...] = jnp.zeros_like(dg_ref)

        @pl.when(j == 0)
        def _():
            dmod_ref[...] = jnp.zeros_like(dmod_ref)

        di = (dvf_ref[0] + dvb_ref[0]).astype(BF16)
        dq = (dqf_ref[0] + dqb_ref[0]).astype(BF16)
        dp = jnp.concatenate([dff_ref[0], dfb_ref[0], di, dq, dpc_ref[0]], axis=1)
        dp_ref[0] = dp
        dh = lax.dot_general(dp, w_ref[...], (((1,), (1,)), ((), ())), preferred_element_type=F32)
        xin = jnp.where(is_ctx, ctx_ref[0], x_ref[0])
        scale = jnp.where(is_ctx, mod_ref[0, 3:4, :], mod_ref[0, 1:2, :])
        xn, r = _rms(xin)
        g = g_ref[...]
        hn = xn * g
        d_shift = _colsum(dh)
        d_scale = _colsum(dh * hn)
        dhn = dh * (1.0 + scale)
        dg_ref[...] += _colsum(dhn * xn)
        dx = _rms_bwd(dhn * g, xn, r)

        @pl.when(is_ctx)
        def _():
            dmod_ref[0, 2:3, :] += d_shift
            dmod_ref[0, 3:4, :] += d_scale

        @pl.when(jnp.logical_not(is_ctx))
        def _():
            dmod_ref[0, 0:1, :] += d_shift
            dmod_ref[0, 1:2, :] += d_scale
            gx_ref[0] = dx + dx1_ref[0]

    def rows(w):
        return pl.BlockSpec((1, ROW_BLOCK, w), lambda b, j: (b, j, 0))

    lat = pl.BlockSpec((1, ROW_BLOCK, D_MODEL), lambda b, j: (b, jnp.maximum(j - 1, 0), 0))
    return _host_call(
        body, name="in_bwd", grid=(bs, nb),
        in_specs=[lat, pl.BlockSpec((1, ROW_BLOCK, D_MODEL), lambda b, j: (b, 0, 0)), lat,
                  pl.BlockSpec((1, D_MODEL), lambda b, j: (0, 0)),
                  pl.BlockSpec((1, 8, D_MODEL), lambda b, j: (b, 0, 0)),
                  pl.BlockSpec((D_MODEL, IN_COLS), lambda b, j: (0, 0)),
                  rows(KW), rows(KW), rows(KW), rows(KW), rows(KW), rows(KW), rows(7 * KW)],
        out_specs=[lat, rows(IN_COLS), pl.BlockSpec((1, D_MODEL), lambda b, j: (0, 0)),
                   pl.BlockSpec((1, 8, D_MODEL), lambda b, j: (b, 0, 0))],
        out_shape=[SDS(x.shape, F32), SDS((bs, nb * ROW_BLOCK, IN_COLS), BF16), SDS((1, D_MODEL), F32),
                   SDS((bs, 8, D_MODEL), F32)],
        args=(x, ctx, dx1, g_mix, mod_a, w_in, df_f, df_b, dv_f, dv_b, dq_f, dq_b, dpc), carry=carry)


def _lower_bound(lbg_ref, direction):
    return _sig(lbg_ref[0, direction:direction + 1, :] - lbg_ref[1, direction:direction + 1, :])


N_CHUNKS = ROW_BLOCK // HGRN_CHUNK


def _block_tri(upper):
    t = np.arange(ROW_BLOCK)[:, None]
    s = np.arange(ROW_BLOCK)[None, :]
    same = (t // HGRN_CHUNK) == (s // HGRN_CHUNK)
    return jnp.asarray(same & ((s >= t) if upper else (s <= t)), dtype=BF16)


TRI_SPEC = pl.BlockSpec((ROW_BLOCK, ROW_BLOCK), lambda b, j: (0, 0))


def _tri_matmul_f32(tri, g):
    g0 = g.astype(BF16)
    r1 = g - g0.astype(F32)
    g1 = r1.astype(BF16)
    g2 = (r1 - g1.astype(F32)).astype(BF16)
    return (jnp.dot(tri, g2, preferred_element_type=F32) + jnp.dot(tri, g1, preferred_element_type=F32)) \
        + jnp.dot(tri, g0, preferred_element_type=F32)


def _chunk_rows(rows):
    return jnp.concatenate([jnp.broadcast_to(r, (HGRN_CHUNK, r.shape[1])) for r in rows], axis=0)


def _block_gates(fl, q, lb, tri, upper):
    t = {}
    t["sg"] = _sig(fl)
    t["f"] = lb + (1.0 - lb) * t["sg"]
    k = 1.0 - t["f"]
    bcum = _tri_matmul_f32(tri, jnp.log(t["f"]))
    ends = [bcum[ci * HGRN_CHUNK:ci * HGRN_CHUNK + 1] if upper else bcum[(ci + 1) * HGRN_CHUNK - 1:(ci + 1) * HGRN_CHUNK]
            for ci in range(fl.shape[0] // HGRN_CHUNK)]
    mid = _chunk_rows([0.5 * r for r in ends])
    t["dec"] = [jnp.exp(r) for r in ends]
    t["e1"] = jnp.exp(bcum - mid)
    t["e2"] = jnp.exp(mid - bcum)
    t["eh"] = _chunk_rows([jnp.exp(0.5 * r) for r in ends])
    t["qi"] = q * t["e1"]
    t["ki"] = k * t["e2"]
    t["kd"] = t["ki"] * t["eh"]
    t["qe"] = t["qi"] * t["eh"]
    return t


def _hgrn_block_order(direction, nb):
    if direction == 0:
        return lambda j: j
    return lambda j: jnp.where(j == 0, 0, nb - j)


def _hgrn_fwd(p, lbg, carry=None):
    bs, rows, _ = p.shape
    nb = rows // ROW_BLOCK
    ncb = ROW_BLOCK // HGRN_CHUNK
    orders = [_hgrn_block_order(d, nb) for d in (0, 1)]
    dirs = (0, 1)

    def body(f0_ref, i0_ref, q0_ref, f1_ref, i1_ref, q1_ref, lbg_ref, tri0_ref, tri1_ref,
             o0_ref, s0_ref, o1_ref, s1_ref, st):
        @pl.when(pl.program_id(1) == 0)
        def _():
            st[...] = jnp.zeros_like(st)

        f_refs, i_refs, q_refs = (f0_ref, f1_ref), (i0_ref, i1_ref), (q0_ref, q1_ref)
        tri_refs, o_refs, s_refs = (tri0_ref, tri1_ref), (o0_ref, o1_ref), (s0_ref, s1_ref)
        chunk = lambda a, ci, h: a[ci * HGRN_CHUNK:(ci + 1) * HGRN_CHUNK, h * HEAD_DIM:(h + 1) * HEAD_DIM]
        masks = [_tri(HGRN_CHUNK, d == 1) for d in dirs]
        t = [_block_gates(f_refs[d][0], q_refs[d][0], _lower_bound(lbg_ref, d), tri_refs[d][...], d == 1) for d in dirs]
        v = [i_refs[d][0] for d in dirs]
        intra = [[[None] * N_HEADS for _ in range(ncb)] for _ in dirs]
        ds_loc = [[[None] * N_HEADS for _ in range(ncb)] for _ in dirs]
        for ci in range(ncb):
            for h in range(N_HEADS):
                for d in dirs:
                    a = jnp.where(masks[d], _dot_nt(chunk(t[d]["qi"], ci, h), chunk(t[d]["ki"], ci, h)), 0.0)
                    intra[d][ci][h] = _dot(a, chunk(v[d], ci, h))
                    ds_loc[d][ci][h] = _dot_tn(chunk(v[d], ci, h), chunk(t[d]["kd"], ci, h))
        for h in range(N_HEADS):
            ls = slice(h * HEAD_DIM, (h + 1) * HEAD_DIM)
            s = [st[d, h] for d in dirs]
            for step in range(ncb):
                for d in dirs:
                    ci = ncb - 1 - step if d == 1 else step
                    s_refs[d][0, 0, ci, h] = s[d]
                    o_refs[d][0, ci * HGRN_CHUNK:(ci + 1) * HGRN_CHUNK, ls] = (
                        intra[d][ci][h] + _dot_nt(chunk(t[d]["qe"], ci, h), s[d]))
                    s[d] = s[d] * t[d]["dec"][ci][:, ls] + ds_loc[d][ci][h]
            for d in dirs:
                st[d, h] = s[d]

    def col(d, cb):
        return pl.BlockSpec((1, ROW_BLOCK, KW), lambda b, j: (b, orders[d](j), cb))

    def outs(d):
        return [pl.BlockSpec((1, ROW_BLOCK, KW), lambda b, j: (b, orders[d](j), 0)),
                pl.BlockSpec((1, 1, ncb, N_HEADS, HEAD_DIM, HEAD_DIM), lambda b, j: (b, orders[d](j), 0, 0, 0, 0))]

    shapes = [SDS((bs, rows, KW), F32), SDS((bs, nb, ncb, N_HEADS, HEAD_DIM, HEAD_DIM), F32)]
    return _host_call(
        body, name="hgrn_fwd", grid=(bs, nb),
        in_specs=[col(0, 0), col(0, 2), col(0, 3), col(1, 1), col(1, 2), col(1, 3),
                  pl.BlockSpec((2, 2, KW), lambda b, j: (0, 0, 0)), TRI_SPEC, TRI_SPEC],
        out_specs=outs(0) + outs(1), out_shape=shapes * 2,
        scratch_shapes=[pltpu.VMEM((2, N_HEADS, HEAD_DIM, HEAD_DIM), F32)],
        args=(p, p, p, p, p, p, lbg, _block_tri(False), _block_tri(True)), carry=carry)


def _hgrn_bwd_pair(p, lbg, s_saved, do_raw, carry=None):
    bs, rows, _ = p.shape
    nb = rows // ROW_BLOCK
    ncb = ROW_BLOCK // HGRN_CHUNK
    dirs = (0, 1)
    fwd_orders = [_hgrn_block_order(d, nb) for d in dirs]
    orders = [lambda j, d=d: fwd_orders[d](nb - 1 - j) for d in dirs]
    pairs = [(ci, h) for ci in range(ncb) for h in range(N_HEADS)]

    def body(f0_ref, i0_ref, q0_ref, s0_ref, do0_ref, f1_ref, i1_ref, q1_ref, s1_ref, do1_ref,
             lbg_ref, tril_ref, triu_ref,
             df0_ref, dq0_ref, dv0_ref, dlb0_ref, df1_ref, dq1_ref, dv1_ref, dlb1_ref, dst, acc):
        b, j = pl.program_id(0), pl.program_id(1)
        f_refs, i_refs, q_refs = (f0_ref, f1_ref), (i0_ref, i1_ref), (q0_ref, q1_ref)
        s_refs, do_refs = (s0_ref, s1_ref), (do0_ref, do1_ref)
        df_refs, dq_refs, dv_refs, dlb_refs = (df0_ref, df1_ref), (dq0_ref, dq1_ref), (dv0_ref, dv1_ref), (dlb0_ref, dlb1_ref)
        tri_refs, trit_refs = (tril_ref, triu_ref), (triu_ref, tril_ref)

        @pl.when((b == 0) & (j == 0))
        def _():
            dlb0_ref[...] = jnp.zeros_like(dlb0_ref)
            dlb1_ref[...] = jnp.zeros_like(dlb1_ref)

        @pl.when(j == 0)
        def _():
            dst[...] = jnp.zeros_like(dst)

        chunk = lambda a, ci, h: a[ci * HGRN_CHUNK:(ci + 1) * HGRN_CHUNK, h * HEAD_DIM:(h + 1) * HEAD_DIM]
        rows_of = lambda ci: slice(ci * HGRN_CHUNK, (ci + 1) * HGRN_CHUNK)
        lanes_of = lambda h: slice(h * HEAD_DIM, (h + 1) * HEAD_DIM)
        grid3 = lambda: [[[None] * N_HEADS for _ in range(ncb)] for _ in dirs]
        lbs = [_lower_bound(lbg_ref, d) for d in dirs]
        masks = [_tri(HGRN_CHUNK, d == 1) for d in dirs]
        masks_t = [_tri(HGRN_CHUNK, d != 1) for d in dirs]
        t = [_block_gates(f_refs[d][0], q_refs[d][0], lbs[d], tri_refs[d][...], d == 1) for d in dirs]
        v = [i_refs[d][0] for d in dirs]
        do = [do_refs[d][0] for d in dirs]
        a_t, da, da_t, dv_in, ds_loc = (grid3() for _ in range(5))
        for ci, h in pairs:
            for d in dirs:
                a_t[d][ci][h] = _dot_nt(chunk(t[d]["ki"], ci, h), chunk(t[d]["qi"], ci, h))
        for ci, h in pairs:
            for d in dirs:
                da[d][ci][h] = _dot_nt(chunk(do[d], ci, h), chunk(v[d], ci, h))
        for ci, h in pairs:
            for d in dirs:
                da_t[d][ci][h] = _dot_nt(chunk(v[d], ci, h), chunk(do[d], ci, h))
        for ci, h in pairs:
            for d in dirs:
                acc[d, 3, rows_of(ci), lanes_of(h)] = _dot(chunk(do[d], ci, h), s_refs[d][0, 0, ci, h])
        for ci, h in pairs:
            for d in dirs:
                ds_loc[d][ci][h] = _dot_tn(chunk(do[d], ci, h), chunk(t[d]["qe"], ci, h))
        for ci, h in pairs:
            for d in dirs:
                acc[d, 0, rows_of(ci), lanes_of(h)] = _dot(jnp.where(masks[d], da[d][ci][h], 0.0),
                                                           chunk(t[d]["ki"], ci, h))
        for ci, h in pairs:
            for d in dirs:
                acc[d, 1, rows_of(ci), lanes_of(h)] = _dot(jnp.where(masks_t[d], da_t[d][ci][h], 0.0),
                                                           chunk(t[d]["qi"], ci, h))
        for ci, h in pairs:
            for d in dirs:
                dv_in[d][ci][h] = _dot(jnp.where(masks_t[d], a_t[d][ci][h], 0.0), chunk(do[d], ci, h))
        ddec = grid3()
        for h in range(N_HEADS):
            ls = lanes_of(h)
            ds = [dst[d, h] for d in dirs]
            for step in range(ncb):
                for d in dirs:
                    ci = step if d == 1 else ncb - 1 - step
                    acc[d, 2, rows_of(ci), ls] = _dot(chunk(v[d], ci, h), ds[d])
                    acc[d, 4, rows_of(ci), ls] = dv_in[d][ci][h] + _dot_nt(chunk(t[d]["kd"], ci, h), ds[d])
                    ddec[d][ci][h] = _colsum(ds[d] * s_refs[d][0, 0, ci, h])
                    ds[d] = ds[d] * t[d]["dec"][ci][:, ls] + ds_loc[d][ci][h]
            for d in dirs:
                dst[d, h] = ds[d]
        for d in dirs:
            td = t[d]
            dqi, dki, dkd, dqe = (acc[d, i] for i in range(4))
            dq_refs[d][0] = td["e1"] * (dqi + dqe * td["eh"])
            dv_refs[d][0] = acc[d, 4]
            dk = td["e2"] * (dki + dkd * td["eh"])
            dkd_kd = dkd * td["kd"]
            db = dqi * td["qi"] - dki * td["ki"] - dkd_kd + dqe * td["qe"]
            dbl = [_colsum(dkd_kd[rows_of(ci)]) + jnp.concatenate(ddec[d][ci], axis=1) * td["dec"][ci]
                   for ci in range(ncb)]
            dg = _tri_matmul_f32(trit_refs[d][...], db) + _chunk_rows(dbl)
            df = dg / td["f"] - dk
            sg = td["sg"]
            dlb_refs[d][...] += _colsum(df * (1.0 - sg))
            df_refs[d][0] = (df * (1.0 - lbs[d]) * sg * (1.0 - sg)).astype(BF16)

    def ins(d):
        col = lambda cb: pl.BlockSpec((1, ROW_BLOCK, KW), lambda b, j: (b, orders[d](j), cb))
        return [col(d), col(2), col(3),
                pl.BlockSpec((1, 1, ncb, N_HEADS, HEAD_DIM, HEAD_DIM), lambda b, j: (b, orders[d](j), 0, 0, 0, 0)),
                pl.BlockSpec((1, ROW_BLOCK, KW), lambda b, j: (b, orders[d](j), 0))]

    def outs(d):
        row = pl.BlockSpec((1, ROW_BLOCK, KW), lambda b, j: (b, orders[d](j), 0))
        return [row, row, row, pl.BlockSpec((1, KW), lambda b, j: (0, 0))]

    shapes = [SDS((bs, rows, KW), BF16), SDS((bs, rows, KW), F32), SDS((bs, rows, KW), F32), SDS((1, KW), F32)]
    return _host_call(
        body, name="hgrn_bwd", grid=(bs, nb),
        in_specs=ins(0) + ins(1) + [pl.BlockSpec((2, 2, KW), lambda b, j: (0, 0, 0)), TRI_SPEC, TRI_SPEC],
        out_specs=outs(0) + outs(1), out_shape=shapes * 2,
        scratch_shapes=[pltpu.VMEM((2, N_HEADS, HEAD_DIM, HEAD_DIM), F32), pltpu.VMEM((2, 5, ROW_BLOCK, KW), F32)],
        args=(p, p, p, s_saved[0], do_raw, p, p, p, s_saved[1], do_raw, lbg, _block_tri(False), _block_tri(True)),
        carry=carry)


def _mix_values(og, u, v, ga, gb, o_raw, gna, lng, lnb, ws_ref, bst, wpa, wpb, wo):
    t = {}
    sog = _sig(og)
    t["sog"], t["silu_og"] = sog, og * sog
    xh_l, r_l = [], []
    for h in range(N_HEADS):
        xh, r = _rms(o_raw[:, h * HEAD_DIM:(h + 1) * HEAD_DIM])
        xh_l.append(xh)
        r_l.append(r)
    t["xh"], t["r"] = jnp.concatenate(xh_l, axis=1), r_l
    gna4 = jnp.concatenate([gna] * N_HEADS, axis=1)
    t["gna4"] = gna4
    t["o_n"] = t["xh"] * gna4
    t["o_a"] = t["o_n"] * t["silu_og"]
    t["ya"] = _dot(t["o_a"], wpa)
    t["gu"], t["tu"] = _gelu(u)
    gv, t["tv"] = _gelu(v)
    mu = jnp.mean(gv, axis=-1, keepdims=True)
    cen = gv - mu
    t["rstd"] = lax.rsqrt(jnp.mean(cen * cen, axis=-1, keepdims=True) + EPS)
    t["xhat"] = cen * t["rstd"]
    vn = t["xhat"] * lng + lnb
    t["vn"] = vn
    chunks = []
    for n in range(ROW_BLOCK // SGU_CHUNK):
        rs = slice(n * SGU_CHUNK, (n + 1) * SGU_CHUNK)
        groups = []
        for g in range(N_HEADS):
            ls = slice(g * HEAD_DIM, (g + 1) * HEAD_DIM)
            groups.append(_dot(ws_ref[g], vn[rs, ls]) + bst[:, g:g + 1])
        chunks.append(jnp.concatenate(groups, axis=1))
    t["mixed"] = jnp.concatenate(chunks, axis=0)
    t["o_bm"] = t["gu"] * t["mixed"]
    t["yb"] = _dot(t["o_bm"], wpb)
    t["sa"], t["sb"] = _sig(ga), _sig(gb)
    t["merged"] = t["sa"] * t["ya"] + t["sb"] * t["yb"]
    t["mix"] = _dot(t["merged"], wo)
    return t


def _mix_in_specs(row_of):
    def col(cb):
        return pl.BlockSpec((1, ROW_BLOCK, KW), lambda b, j: (b, row_of(j), cb))
    return [col(cb) for cb in range(4, 11)]


def _mix_param_specs():
    full2 = lambda r, c: pl.BlockSpec((r, c), lambda b, j: (0, 0))
    return [full2(1, HEAD_DIM), full2(1, KW), full2(1, KW),
            pl.BlockSpec((N_HEADS, SGU_CHUNK, SGU_CHUNK), lambda b, j: (0, 0, 0)),
            full2(SGU_CHUNK, N_HEADS), full2(KW, D_MODEL), full2(KW, D_MODEL), full2(D_MODEL, D_MODEL)]


def _mix_fwd(p, o_f, o_b, x, mod_c, gna, lng, lnb, w_s, bst, wpa, wpb, wo):
    bs, seq, _ = x.shape
    nbl = seq // ROW_BLOCK

    def body(og_r, u_r, v_r, ga0_r, ga1_r, gb0_r, gb1_r, of_r, ob_r, x_r, mod_r,
             gna_r, lng_r, lnb_r, ws_r, bst_r, wpa_r, wpb_r, wo_r, x1_r):
        ga = jnp.concatenate([ga0_r[0], ga1_r[0]], axis=1)
        gb = jnp.concatenate([gb0_r[0], gb1_r[0]], axis=1)
        t = _mix_values(og_r[0], u_r[0], v_r[0], ga, gb, of_r[0] + ob_r[0], gna_r[...], lng_r[...], lnb_r[...],
                        ws_r, bst_r[...], wpa_r[...], wpb_r[...], wo_r[...])
        x1_r[0] = x_r[0] + mod_r[0, 0:1, :] * t["mix"]

    row = lambda w: pl.BlockSpec((1, ROW_BLOCK, w), lambda b, j: (b, j + 1, 0))
    lat = pl.BlockSpec((1, ROW_BLOCK, D_MODEL), lambda b, j: (b, j, 0))
    return pl.pallas_call(
        body, name="mix_fwd", grid=(bs, nbl),
        in_specs=_mix_in_specs(lambda j: j + 1) + [row(KW), row(KW), lat,
                                                    pl.BlockSpec((1, 8, D_MODEL), lambda b, j: (b, 0, 0))]
        + _mix_param_specs(),
        out_specs=lat, out_shape=SDS(x.shape, F32), compiler_params=_params(2),
    )(p, p, p, p, p, p, p, o_f, o_b, x, mod_c, gna, lng, lnb, w_s, bst, wpa, wpb, wo)


def _mix_bwd(p, o_f, o_b, dx1, mod_c, gna, lng, lnb, w_s, w_s_t, bst, wpa, wpb, wo, carry=None):
    bs, rows, _ = p.shape
    nb = rows // ROW_BLOCK

    def body(og_r, u_r, v_r, ga0_r, ga1_r, gb0_r, gb1_r, of_r, ob_r, dx1_r, mod_r,
             gna_r, lng_r, lnb_r, ws_r, bst_r, wpa_r, wpb_r, wo_r, wst_r,
             dor_r, dpc_r, dwpa_r, dwpb_r, dwo_r, dgna_r, dlng_r, dlnb_r, dws_r, dbst_r, dmod_r):
        b, j = pl.program_id(0), pl.program_id(1)

        @pl.when((b == 0) & (j == 0))
        def _():
            for r in (dwpa_r, dwpb_r, dwo_r, dgna_r, dlng_r, dlnb_r, dws_r, dbst_r):
                r[...] = jnp.zeros_like(r)

        @pl.when(j == 0)
        def _():
            dmod_r[...] = jnp.zeros_like(dmod_r)
            dor_r[...] = jnp.zeros_like(dor_r)
            dpc_r[...] = jnp.zeros_like(dpc_r)

        @pl.when(j > 0)
        def _():
            og, u, v = og_r[0], u_r[0], v_r[0]
            ga = jnp.concatenate([ga0_r[0], ga1_r[0]], axis=1)
            gb = jnp.concatenate([gb0_r[0], gb1_r[0]], axis=1)
            gna, lng = gna_r[...], lng_r[...]
            wpa, wpb, wo = wpa_r[...], wpb_r[...], wo_r[...]
            dx1 = dx1_r[0]
            dmix = mod_r[0, 0:1, :] * dx1
            dmerged = _dot_nt(dmix, wo)
            t = _mix_values(og, u, v, ga, gb, of_r[0] + ob_r[0], gna, lng, lnb_r[...],
                            ws_r, bst_r[...], wpa, wpb, wo)
            dmod_r[0, 0:1, :] += _colsum(dx1 * t["mix"])
            dwo_r[...] += _dot_tn(t["merged"], dmix)
            sa, sb = t["sa"], t["sb"]
            dya, dyb = sa * dmerged, sb * dmerged
            dga = dmerged * t["ya"] * sa * (1.0 - sa)
            dgb = dmerged * t["yb"] * sb * (1.0 - sb)
            do_a = _dot_nt(dya, wpa)
            dwpa_r[...] += _dot_tn(t["o_a"], dya)
            do_bm = _dot_nt(dyb, wpb)
            dwpb_r[...] += _dot_tn(t["o_bm"], dyb)
            sog = t["sog"]
            dog = do_a * t["o_n"] * (sog * (1.0 + og * (1.0 - sog)))
            do_n = do_a * t["silu_og"]
            dxh = do_n * t["gna4"]
            prod = do_n * t["xh"]
            dgna = jnp.zeros((1, HEAD_DIM), F32)
            dor_l = []
            for h in range(N_HEADS):
                ls = slice(h * HEAD_DIM, (h + 1) * HEAD_DIM)
                dgna = dgna + _colsum(prod[:, ls])
                dor_l.append(_rms_bwd(dxh[:, ls], t["xh"][:, ls], t["r"][h]))
            dgna_r[...] += dgna
            dor_r[0] = jnp.concatenate(dor_l, axis=1)
            du = do_bm * t["mixed"] * _dgelu(u, t["tu"])
            dmixed = do_bm * t["gu"]
            vn = t["vn"]
            dvn_chunks = []
            for n in range(ROW_BLOCK // SGU_CHUNK):
                rs = slice(n * SGU_CHUNK, (n + 1) * SGU_CHUNK)
                groups = []
                for g in range(N_HEADS):
                    ls = slice(g * HEAD_DIM, (g + 1) * HEAD_DIM)
                    dm = dmixed[rs, ls]
                    dws_r[g] += _dot_nt(dm, vn[rs, ls])
                    dbst_r[:, g:g + 1] += jnp.sum(dm, axis=1, keepdims=True)
                    groups.append(_dot(wst_r[g], dm))
                dvn_chunks.append(jnp.concatenate(groups, axis=1))
            dvn = jnp.concatenate(dvn_chunks, axis=0)
            xhat = t["xhat"]
            dlng_r[...] += _colsum(dvn * xhat)
            dlnb_r[...] += _colsum(dvn)
            dxhat = dvn * lng
            dgv = t["rstd"] * (dxhat - jnp.mean(dxhat, axis=-1, keepdims=True)
                               - xhat * jnp.mean(dxhat * xhat, axis=-1, keepdims=True))
            dv = dgv * _dgelu(v, t["tv"])
            dpc_r[0] = jnp.concatenate([dog, du, dv, dga, dgb], axis=1).astype(BF16)

    row = lambda w: pl.BlockSpec((1, ROW_BLOCK, w), lambda b, j: (b, j, 0))
    lat = pl.BlockSpec((1, ROW_BLOCK, D_MODEL), lambda b, j: (b, jnp.maximum(j - 1, 0), 0))
    full2 = lambda r, c: pl.BlockSpec((r, c), lambda b, j: (0, 0))
    ws_spec = pl.BlockSpec((N_HEADS, SGU_CHUNK, SGU_CHUNK), lambda b, j: (0, 0, 0))
    return _host_call(
        body, name="mix_bwd", grid=(bs, nb),
        in_specs=_mix_in_specs(lambda j: j) + [row(KW), row(KW), lat,
                                                pl.BlockSpec((1, 8, D_MODEL), lambda b, j: (b, 0, 0))]
        + _mix_param_specs() + [ws_spec],
        out_specs=[row(KW), row(7 * KW), full2(KW, D_MODEL), full2(KW, D_MODEL), full2(D_MODEL, D_MODEL),
                   full2(1, HEAD_DIM), full2(1, KW), full2(1, KW), ws_spec, full2(SGU_CHUNK, N_HEADS),
                   pl.BlockSpec((1, 8, D_MODEL), lambda b, j: (b, 0, 0))],
        out_shape=[SDS((bs, rows, KW), F32), SDS((bs, rows, 7 * KW), BF16), SDS((KW, D_MODEL), F32),
                   SDS((KW, D_MODEL), F32), SDS((D_MODEL, D_MODEL), F32), SDS((1, HEAD_DIM), F32),
                   SDS((1, KW), F32), SDS((1, KW), F32), SDS((N_HEADS, SGU_CHUNK, SGU_CHUNK), F32),
                   SDS((SGU_CHUNK, N_HEADS), F32), SDS((bs, 8, D_MODEL), F32)],
        args=(p, p, p, p, p, p, p, o_f, o_b, dx1, mod_c, gna, lng, lnb, w_s, bst, wpa, wpb, wo, w_s_t), carry=carry)


def _ffn(x1, target, mod_c, g_ffn, g_final, w_up, w_down):
    bs, seq, _ = x1.shape
    nbl = seq // ROW_BLOCK

    def body(x1_r, tg_r, mod_r, gf_r, gl_r, wu_r, wd_r,
             dx1_r, h2_r, dab_r, hid_r, dffn_r, loss_r, dgl_r, dgf_r, dmod_r):
        b, j = pl.program_id(0), pl.program_id(1)

        @pl.when((b == 0) & (j == 0))
        def _():
            for r in (loss_r, dgl_r, dgf_r):
                r[...] = jnp.zeros_like(r)

        @pl.when(j == 0)
        def _():
            dmod_r[...] = jnp.zeros_like(dmod_r)

        x1 = x1_r[0]
        shift, scale, gate = mod_r[0, 1:2, :], mod_r[0, 2:3, :], mod_r[0, 3:4, :]
        gf, gl = gf_r[...], gl_r[...]
        xn2, r2 = _rms(x1)
        hn2 = xn2 * gf
        h2 = (hn2 * (1.0 + scale) + shift).astype(BF16)
        h2_r[0] = h2
        ab = jnp.dot(h2, wu_r[...], preferred_element_type=F32)
        a, bb = ab[:, :D_FF], ab[:, D_FF:]
        sa = _sig(a)
        silu_a = a * sa
        hid = (silu_a * bb).astype(BF16)
        hid_r[0] = hid
        ffn = jnp.dot(hid, wd_r[...], preferred_element_type=F32)
        x2 = x1 + gate * ffn
        xn3, r3 = _rms(x2)
        err = xn3 * gl - tg_r[0]
        loss_r[...] += 0.5 * jnp.sum(jnp.mean(err * err, axis=-1, keepdims=True), axis=0, keepdims=True)
        dy = err * (1.0 / D_MODEL)
        dgl_r[...] += _colsum(dy * xn3)
        dx2 = _rms_bwd(dy * gl, xn3, r3)
        dmod_r[0, 3:4, :] += _colsum(dx2 * ffn)
        dffn = (gate * dx2).astype(BF16)
        dffn_r[0] = dffn
        dhid = lax.dot_general(dffn, wd_r[...], (((1,), (1,)), ((), ())), preferred_element_type=F32)
        da = dhid * bb * (sa * (1.0 + a * (1.0 - sa)))
        db = dhid * silu_a
        dab = jnp.concatenate([da, db], axis=1).astype(BF16)
        dab_r[0] = dab
        dh2 = lax.dot_general(dab, wu_r[...], (((1,), (1,)), ((), ())), preferred_element_type=F32)
        dmod_r[0, 1:2, :] += _colsum(dh2)
        dmod_r[0, 2:3, :] += _colsum(dh2 * hn2)
        dhn2 = dh2 * (1.0 + scale)
        dgf_r[...] += _colsum(dhn2 * xn2)
        dx1_r[0] = dx2 + _rms_bwd(dhn2 * gf, xn2, r2)

    lat = lambda w: pl.BlockSpec((1, ROW_BLOCK, w), lambda b, j: (b, j, 0))
    full2 = lambda r, c: pl.BlockSpec((r, c), lambda b, j: (0, 0))
    mod_spec = pl.BlockSpec((1, 8, D_MODEL), lambda b, j: (b, 0, 0))
    return pl.pallas_call(
        body, name="ffn", grid=(bs, nbl),
        in_specs=[lat(D_MODEL), lat(D_MODEL), mod_spec, full2(1, D_MODEL), full2(1, D_MODEL),
                  full2(D_MODEL, 2 * D_FF), full2(D_FF, D_MODEL)],
        out_specs=[lat(D_MODEL), lat(D_MODEL), lat(2 * D_FF), lat(D_FF), lat(D_MODEL),
                   full2(1, 1), full2(1, D_MODEL), full2(1, D_MODEL), mod_spec],
        out_shape=[SDS(x1.shape, F32), SDS(x1.shape, BF16), SDS((bs, seq, 2 * D_FF), BF16),
                   SDS((bs, seq, D_FF), BF16), SDS(x1.shape, BF16), SDS((1, 1), F32),
                   SDS((1, D_MODEL), F32), SDS((1, D_MODEL), F32), SDS((bs, 8, D_MODEL), F32)],
        compiler_params=_params(2),
    )(x1, target, mod_c, g_ffn, g_final, w_up, w_down)


def _row_tile(rows):
    return next(m * ROW_BLOCK for m in (4, 2, 1) if rows % (m * ROW_BLOCK) == 0)


def _matmul_tn(a, b, n_blocks, tk, name, carry=None):
    t, m = a.shape
    n = b.shape[1]
    tn = n // n_blocks

    def body(a_ref, b_ref, o_ref):
        @pl.when(pl.program_id(1) == 0)
        def _():
            o_ref[...] = jnp.zeros_like(o_ref)
        o_ref[0] += _dot_tn(a_ref[...], b_ref[...])

    (out,), carried = _host_call(
        body, name=name, grid=(n_blocks, t // tk),
        in_specs=[pl.BlockSpec((tk, m), lambda i, k: (k, 0)), pl.BlockSpec((tk, tn), lambda i, k: (k, i))],
        out_specs=[pl.BlockSpec((1, m, tn), lambda i, k: (i, 0, 0))],
        out_shape=[SDS((n_blocks, m, tn), F32)], args=(a, b), carry=carry)
    return out if carry is None else (out, carried)


SMALL_ROWS = 80
ROW_CCTX = 3


def _small_reduce(gathered, lbg):
    def body(g_ref, lbg_ref, s_ref, dgam_ref):
        tot = g_ref[0:SMALL_ROWS, :]
        for dev in range(1, N_DEV):
            tot = tot + g_ref[dev * SMALL_ROWS:(dev + 1) * SMALL_ROWS, :]
        s_ref[...] = tot
        cc = g_ref[ROW_CCTX:ROW_CCTX + 1, :]
        for dev in range(2, N_DEV, 2):
            cc = cc + g_ref[dev * SMALL_ROWS + ROW_CCTX:dev * SMALL_ROWS + ROW_CCTX + 1, :]
        s_ref[ROW_CCTX:ROW_CCTX + 1, :] = cc
        dlb = tot[7:8, :]
        for d in range(2):
            s0 = _sig(lbg_ref[0, d:d + 1, :] - lbg_ref[1, d:d + 1, :])
            dgam_ref[d:d + 1, :] = dlb[:, d * KW:(d + 1) * KW] * s0 * (1.0 - s0)

    return pl.pallas_call(
        body, name="small_reduce", out_shape=[SDS((SMALL_ROWS, D_MODEL), F32), SDS((2, KW), F32)],
        in_specs=[VMEM_SPEC] * 2, out_specs=[VMEM_SPEC] * 2,
    )(gathered, lbg)


def _pad_cols(a, width):
    return jnp.pad(a, ((0, 0), (0, width - a.shape[1])))


def kernel(x, c, ctx, c_ctx, w_mod, b_mod, g_mix, g_ffn, w_in, lb_gamma, g_norm_a, ln_v_g, ln_v_b, w_s, b_s, w_pa, w_pb, w_o, w_up, w_down, g_final, loss_target, m_c_ctx, m_w_mod, m_b_mod, m_g_mix, m_g_ffn, m_w_in, m_lb_gamma, m_g_norm_a, m_ln_v_g, m_ln_v_b, m_w_s, m_b_s, m_w_pa, m_w_pb, m_w_o, m_w_up, m_w_down, m_g_final, v_c_ctx, v_w_mod, v_b_mod, v_g_mix, v_g_ffn, v_w_in, v_lb_gamma, v_g_norm_a, v_ln_v_g, v_ln_v_b, v_w_s, v_b_s, v_w_pa, v_w_pb, v_w_o, v_w_up, v_w_down, v_g_final):
    ax, ay, ac = lax.axis_index("x"), lax.axis_index("y"), lax.axis_index("c")
    kc = 2 * ax + ay
    dev = 2 * kc + ac
    pos = jnp.stack([kc, ac]).astype(jnp.int32)
    bs, seq, _ = x.shape
    assert bs <= 4 and ctx.shape[1] == ROW_BLOCK and seq % ROW_BLOCK == 0
    mod_cols = w_mod.shape[2]

    lbg_row = _pad_cols(lb_gamma.reshape(1, -1), D_MODEL)
    pay1 = jnp.concatenate([c, jnp.zeros((4 - bs, D_MODEL), F32), c_ctx[None, :], lbg_row,
                            jnp.zeros((2, D_MODEL), F32)], axis=0)
    cond64 = _all_gather8(pay1, "gather_cond")
    lbg_full = cond64.reshape(N_DEV, 8, D_MODEL)[0::2, 5, :KW].reshape(N_CHIPS, 2, 2, HEAD_DIM)
    lbg_full = jnp.transpose(lbg_full, (1, 2, 0, 3)).reshape(2, 2, KW)

    b_mod_s = lax.dynamic_slice(b_mod, (0, kc * mod_cols), (1, mod_cols))
    mod_s = _mod_fwd(cond64, w_mod[0], b_mod_s)
    shards = [w_in[0], w_up[0], w_pa[0], w_pb[0], w_o[0], w_down[0]]
    bufs = _cast_bf16(pos, shards)
    mod_g, (w_in_g,) = _all_gather8(mod_s, "gather_mod", carry=_carry_gather_send(bufs[:1]),
                                    then=_carry_gather_forward(bufs[:1]))
    mod_g = mod_g.reshape(N_DEV, 64, mod_cols)[0::2]
    mod_full = jnp.transpose(mod_g, (1, 0, 2)).reshape(64, N_CHIPS * mod_cols)
    mod_mine = lax.dynamic_slice(mod_full, (dev * 8, 0), (8, 6 * D_MODEL)).reshape(8, 6, D_MODEL)
    mod, mc = mod_mine[:bs], mod_mine[4]
    zeros4 = jnp.zeros((bs, 4, D_MODEL), F32)
    mod_a = jnp.concatenate([mod[:, 0:2], jnp.broadcast_to(mc[None, 0:2], (bs, 2, D_MODEL)), zeros4], axis=1)
    mod_c = jnp.concatenate([mod[:, 2:6], zeros4], axis=1)

    def cols_major(a):
        return jnp.transpose(a, (1, 0, 2)).reshape(a.shape[1], -1)

    gna, lng, lnb = g_norm_a, ln_v_g, ln_v_b
    ws3 = w_s[0]
    ws3_t = jnp.transpose(ws3, (0, 2, 1))
    bst = jnp.transpose(b_s[0])

    w_in_f = cols_major(w_in_g)
    (p, h_all), sent = _in_fwd(x, ctx, g_mix, mod_a, w_in_f, carry=_carry_gather_send(bufs[1:]))
    (o_f, s_f, o_b, s_b), gathered = _hgrn_fwd(p, lbg_full, carry=_carry_gather_forward(sent))
    w_up_f, w_pa_f, w_pb_f = (cols_major(a) for a in gathered[:3])
    w_o_f = gathered[3].reshape(-1, D_MODEL)
    w_down_f = gathered[4].reshape(-1, D_MODEL)
    x1 = _mix_fwd(p, o_f, o_b, x, mod_c, gna, lng, lnb, ws3, bst, w_pa_f, w_pb_f, w_o_f)
    dx1, h2, dab, hid, dffn, loss_part, dg_final, dg_ffn, dmod_ffn = _ffn(
        x1, loss_target, mod_c, g_ffn, g_final[None, :], w_up_f, w_down_f)
    rows_lat = bs * seq
    tk_lat = _row_tile(rows_lat)
    dw_up = _matmul_tn(h2.reshape(rows_lat, D_MODEL), dab.reshape(rows_lat, 2 * D_FF), N_CHIPS, tk_lat, "dw_up")
    dw_down = _matmul_tn(hid.reshape(rows_lat, D_FF), dffn.reshape(rows_lat, D_MODEL), 1, tk_lat, "dw_down")

    def shard_major(a):
        return jnp.transpose(a.reshape(a.shape[0], N_CHIPS, -1), (1, 0, 2))

    def add_halves(parts, recvs, names):
        sums = [_rs_add_halves(pos, g, r, "rs_add_" + nm) for g, r, nm in zip(parts, recvs, names)]
        return [s[0] for s in sums], [s[1] for s in sums]

    def sum_owner(cp32s, recvs, names):
        return [_rs_sum_owner(pos, a, r, "rs_sum_" + nm) for a, r, nm in zip(cp32s, recvs, names)]

    ffn_names, mix_names = ["w_up", "w_down"], ["w_pa", "w_pb", "w_o"]
    part_ffn = [dw_up, dw_down.reshape(N_CHIPS, -1, D_MODEL)]
    (do_raw, dpc, dw_pa, dw_pb, dw_o, dgna, dlng, dlnb, dws, dbst, dmod_mix), sib_ffn = _mix_bwd(
        p, o_f, o_b, dx1, mod_c, gna, lng, lnb, ws3, ws3_t, bst, w_pa_f, w_pb_f, w_o_f,
        carry=_carry_sibling_halves(part_ffn))
    cp32_ffn, cpbf_ffn = add_halves(part_ffn, sib_ffn, ffn_names)
    part_mix = [shard_major(dw_pa), shard_major(dw_pb), dw_o.reshape(N_CHIPS, -1, D_MODEL)]
    (df_f, dq_f, dv_f, dlb0, df_b, dq_b, dv_b, dlb1), got = _hgrn_bwd_pair(
        p, lbg_full, (s_f, s_b), do_raw,
        carry=_merge_carries(_carry_to_owner(cpbf_ffn), _carry_sibling_halves(part_mix)))
    own_ffn, sib_mix = got[:2], got[2:]
    half_ffn = sum_owner(cp32_ffn, own_ffn, ffn_names)
    cp32_mix, cpbf_mix = add_halves(part_mix, sib_mix, mix_names)
    (grad_x, dp, dg_mix, dmod_in), _ = _in_bwd(x, ctx, dx1, g_mix, mod_a, w_in_f, df_f, df_b, dv_f, dv_b, dq_f, dq_b,
                                               dpc)

    rows_all = dp.shape[0] * dp.shape[1]
    tk_all = _row_tile(rows_all)
    dw_in, got = _matmul_tn(h_all.reshape(rows_all, D_MODEL), dp.reshape(rows_all, IN_COLS), N_CHIPS, tk_all, "dw_in",
                            carry=_merge_carries(_carry_join_halves(half_ffn), _carry_to_owner(cpbf_mix)))
    g_ffn_w, own_mix = got[:2], got[2:]
    half_mix = sum_owner(cp32_mix, own_mix, mix_names)

    dmod_mine = jnp.concatenate([dmod_in[:, 0], dmod_in[:, 1], dmod_mix[:, 0], dmod_ffn[:, 1], dmod_ffn[:, 2],
                                 dmod_ffn[:, 3]], axis=1)
    dmc = jnp.concatenate([jnp.sum(dmod_in[:, 2], axis=0), jnp.sum(dmod_in[:, 3], axis=0),
                           jnp.zeros((4 * D_MODEL,), F32)])[None, :]
    pay3 = jnp.concatenate([dmod_mine, jnp.zeros((4 - bs, 6 * D_MODEL), F32), dmc,
                            jnp.zeros((3, 6 * D_MODEL), F32)], axis=0)
    dmod64, got = _all_gather8(pay3, "gather_dmod", carry=_merge_carries(_carry_sibling_halves([dw_in]),
                                                                          _carry_join_halves(half_mix)))
    sib_in, g_mix_w = got[:1], got[1:]
    cp32_in, cpbf_in = add_halves([dw_in], sib_in, ["w_in"])
    dmod64_my = lax.dynamic_slice(dmod64, (0, kc * mod_cols), (64, mod_cols))
    g_w_mod, g_b_mod, g_cctx_part = _mod_bwd(cond64, dmod64, dmod64_my, w_mod[0], c_ctx[None, :])

    def row(*parts):
        return _pad_cols(jnp.concatenate([q.reshape(1, -1) for q in parts], axis=1), D_MODEL)

    small_rows = [dg_mix, dg_ffn, dg_final, g_cctx_part, row(dgna), row(dlng, dlnb), row(jnp.transpose(dbst)),
                  row(dlb0, dlb1), row(loss_part), jnp.zeros((7, D_MODEL), F32), dws.reshape(64, D_MODEL)]
    pay4 = jnp.concatenate(small_rows, axis=0)
    small64, own_in = _all_gather8(pay4, "gather_small", carry=_carry_to_owner(cpbf_in))
    tot, dgam0 = _small_reduce(small64, lbg_full)

    rest_names = ["w_up", "w_pa", "w_pb", "w_o", "w_down", "w_mod"]
    rest_w = shards[1:] + [w_mod[0]]
    rest_g = [g_ffn_w[0], g_mix_w[0], g_mix_w[1], g_mix_w[2], g_ffn_w[1], g_w_mod]
    rest_m = [m_w_up[0], m_w_pa[0], m_w_pb[0], m_w_o[0], m_w_down[0], m_w_mod[0]]
    rest_v = [v_w_up[0], v_w_pa[0], v_w_pb[0], v_w_o[0], v_w_down[0], v_w_mod[0]]
    (ds_r, m2s_r, v2s_r), _ = _adamw_group(rest_w, rest_g, rest_m, rest_v, "adamw_rest")
    res = {}
    for name, g, d, m2, v2 in zip(rest_names, rest_g, ds_r, m2s_r, v2s_r):
        res[name] = (g[None], d[None], m2[None], v2[None])
    g_in_w = _comm_call("rs_join_w_in", _carry_join_halves(sum_owner(cp32_in, own_in, ["w_in"])))
    d, m2, v2 = _adamw_big(shards[0], g_in_w[0], m_w_in[0], v_w_in[0], "adamw_w_in")
    res["w_in"] = (g_in_w[0][None], d[None], m2[None], v2[None])

    loss = tot[8, 0]
    dgam_full = jnp.stack([dgam0, -dgam0])
    g_lbg = lax.dynamic_slice(dgam_full, (0, 0, kc * HEAD_DIM), (2, 2, HEAD_DIM))

    small = [
        ("c_ctx", c_ctx[None, :], tot[3:4], m_c_ctx, v_c_ctx),
        ("b_mod", b_mod, g_b_mod, m_b_mod, v_b_mod),
        ("g_mix", g_mix, tot[0:1], m_g_mix, v_g_mix),
        ("g_ffn", g_ffn, tot[1:2], m_g_ffn, v_g_ffn),
        ("lb_gamma", lb_gamma.reshape(4, HEAD_DIM), g_lbg.reshape(4, HEAD_DIM), m_lb_gamma, v_lb_gamma),
        ("g_norm_a", g_norm_a, tot[4:5, :HEAD_DIM], m_g_norm_a, v_g_norm_a),
        ("ln_v_g", ln_v_g, tot[5:6, :KW], m_ln_v_g, v_ln_v_g),
        ("ln_v_b", ln_v_b, tot[5:6, KW:], m_ln_v_b, v_ln_v_b),
        ("w_s", w_s.reshape(N_HEADS * SGU_CHUNK, SGU_CHUNK), tot[16:80].reshape(N_HEADS * SGU_CHUNK, SGU_CHUNK),
         m_w_s, v_w_s),
        ("b_s", b_s[0], tot[6:7, :KW].reshape(N_HEADS, SGU_CHUNK), m_b_s, v_b_s),
        ("g_final", g_final[None, :], tot[2:3], m_g_final, v_g_final),
    ]
    ws_, gs_ = [s[1] for s in small], [s[2] for s in small]
    ms_ = [s[3].reshape(s[1].shape) for s in small]
    vs_ = [s[4].reshape(s[1].shape) for s in small]
    ds_, m2s_, v2s_ = _adamw_small(ws_, gs_, ms_, vs_)
    for (name, _, g, m, _), d, m2, v2 in zip(small, ds_, m2s_, v2s_):
        res[name] = tuple(t.reshape(m.shape) for t in (g, d, m2, v2))

    order = ["c_ctx", "w_mod", "b_mod", "g_mix", "g_ffn", "w_in", "lb_gamma", "g_norm_a", "ln_v_g", "ln_v_b",
             "w_s", "b_s", "w_pa", "w_pb", "w_o", "w_up", "w_down", "g_final"]
    outs = [loss, grad_x]
    for part in range(4):
        outs += [res[n][part] for n in order]
    return tuple(outs)
```

```python
import functools
import math

import jax
import jax.numpy as jnp
import numpy as np
from jax import lax
from jax.experimental import pallas as pl
from jax.experimental.pallas import tpu as pltpu

F32 = jnp.float32
BF16 = jnp.bfloat16
SDS = jax.ShapeDtypeStruct
MESH = pl.DeviceIdType.MESH

EPS = 1e-6
D_MODEL = 1024
N_HEADS = 4
HEAD_DIM = 128
KW = N_HEADS * HEAD_DIM
IN_COLS = 11 * KW
D_FF = 2816
HGRN_CHUNK = 64
SGU_CHUNK = 128
ROW_BLOCK = 256
N_CHIPS = 4
N_DEV = 8
V7X_VMEM_BYTES = 64 * 1024 * 1024
VMEM_LIMIT = V7X_VMEM_BYTES - 6 * 1024 * 1024

ADAM_LR, ADAM_B1, ADAM_B2, ADAM_EPS, ADAM_WD, ADAM_STEP = 0.001, 0.9, 0.999, 1e-08, 0.01, 10
GELU_C0 = math.sqrt(2.0 / math.pi)
GELU_C1 = 0.044715

VMEM_SPEC = pl.BlockSpec(memory_space=pltpu.VMEM)
ANY_SPEC = pl.BlockSpec(memory_space=pl.ANY)


def _params(n_grid):
    return pltpu.CompilerParams(dimension_semantics=("arbitrary",) * n_grid, vmem_limit_bytes=VMEM_LIMIT)


def _sig(x):
    return 0.5 * jnp.tanh(0.5 * x) + 0.5


def _gelu(x):
    t = jnp.tanh(GELU_C0 * (x + GELU_C1 * x * x * x))
    return 0.5 * x * (1.0 + t), t


def _dgelu(x, t):
    return 0.5 * (1.0 + t) + 0.5 * x * (1.0 - t * t) * GELU_C0 * (1.0 + 3.0 * GELU_C1 * x * x)


def _dot(a, b):
    return jnp.dot(a.astype(BF16), b.astype(BF16), preferred_element_type=F32)


def _dot_nt(a, b):
    return lax.dot_general(a.astype(BF16), b.astype(BF16), (((1,), (1,)), ((), ())), preferred_element_type=F32)


def _dot_tn(a, b):
    return lax.dot_general(a.astype(BF16), b.astype(BF16), (((0,), (0,)), ((), ())), preferred_element_type=F32)


def _dot_f32(a, b, dims=(((1,), (0,)), ((), ()))):
    return lax.dot_general(a, b, dims, precision=lax.Precision.HIGHEST, preferred_element_type=F32)


def _rms(x):
    r = lax.rsqrt(jnp.mean(x * x, axis=-1, keepdims=True) + EPS)
    return x * r, r


def _rms_bwd(dxn, xn, r):
    return r * (dxn - xn * jnp.mean(dxn * xn, axis=-1, keepdims=True))


def _colsum(a):
    return jnp.sum(a, axis=0, keepdims=True)


def _tri(n, upper):
    t = lax.broadcasted_iota(jnp.int32, (n, n), 0)
    s = lax.broadcasted_iota(jnp.int32, (n, n), 1)
    return (s >= t) if upper else (s <= t)


def _all_gather8(x_shard, name, carry=None, then=None):
    m_per, n = x_shard.shape
    n_ci = len(carry.ins) if carry else 0
    n_co = len(carry.outs) if carry else 0
    n_cs = len(carry.sems) if carry else 0

    def body(*refs):
        x_ref, cins = refs[0], refs[1:1 + n_ci]
        out_ref, couts = refs[1 + n_ci], refs[2 + n_ci:2 + n_ci + n_co]
        send_sems, recv_sems, local_sem = refs[2 + n_ci + n_co:5 + n_ci + n_co]
        csems = refs[5 + n_ci + n_co:5 + n_ci + n_co + n_cs]
        tsems = refs[5 + n_ci + n_co + n_cs:]
        if carry:
            _start_all(carry.copies(cins, couts, csems)[0])
        _gather8_body(x_ref, out_ref, send_sems, recv_sems, local_sem, m_per)
        if carry:
            _wait_all(carry.copies(cins, couts, csems)[1])
        if then:
            _start_all(then.copies(couts, couts, tsems)[0])
            _wait_all(then.copies(couts, couts, tsems)[1])

    res = pl.pallas_call(
        body, name=name, out_shape=[SDS((N_DEV * m_per, n), x_shard.dtype)] + (carry.outs if carry else []),
        in_specs=[VMEM_SPEC] + [ANY_SPEC] * n_ci, out_specs=[VMEM_SPEC] + [ANY_SPEC] * n_co,
        input_output_aliases={1 + i: 1 + o for i, o in carry.alias.items()} if carry else {},
        scratch_shapes=[pltpu.SemaphoreType.DMA((7,)), pltpu.SemaphoreType.DMA((7,)), pltpu.SemaphoreType.DMA]
        + (carry.sems if carry else []) + (then.sems if then else []),
    )(x_shard, *(carry.ins if carry else []))
    return res[0] if carry is None else (res[0], list(res[1:]))


def _gather8_body(x_ref, out_ref, send_sems, recv_sems, local_sem, m_per):
    x, y, c = lax.axis_index("x"), lax.axis_index("y"), lax.axis_index("c")
    me, sibling = (x, y, c), (x, y, 1 - c)
    chips = [(1 - x, y), (x, 1 - y), (1 - x, 1 - y)]

    def rows(px, py, pc):
        return out_ref.at[pl.ds((4 * px + 2 * py + pc) * m_per, m_per), :]

    def copy(k, block, to, src=None):
        return pltpu.make_async_remote_copy(
            src_ref=rows(*block) if src is None else src, dst_ref=rows(*block),
            send_sem=send_sems.at[k], recv_sem=recv_sems.at[k], device_id=to, device_id_type=MESH)

    mine = pltpu.make_async_copy(x_ref, rows(*me), local_sem)
    mine.start()
    first = [copy(0, me, sibling, src=x_ref)]
    first += [copy(1 + j, me, (*chip, c), src=x_ref) for j, chip in enumerate(chips)]
    for cp in first:
        cp.start()
    passed = [copy(4 + j, (*chip, c), sibling) for j, chip in enumerate(chips)]
    for j, chip in enumerate(chips):
        copy(1 + j, (*chip, c), me).wait_recv()
        passed[j].start()
    copy(0, sibling, me).wait_recv()
    for j, chip in enumerate(chips):
        copy(4 + j, (*chip, 1 - c), me).wait_recv()
    for cp in first + passed:
        cp.wait_send()
    mine.wait()


def _mesh_pos():
    x, y, c = lax.axis_index("x"), lax.axis_index("y"), lax.axis_index("c")
    chips = [(1 - x, y), (x, 1 - y), (1 - x, 1 - y)]
    return x, y, c, 2 * x + y, (x, y, 1 - c), chips


def _half_rows(c, rh):
    return pl.ds(pl.multiple_of(c * rh, 16), rh)


class _Carry:
    def __init__(self, ins, outs, alias, sems, copies):
        self.ins, self.outs, self.alias, self.sems, self.copies = list(ins), list(outs), dict(alias), list(sems), copies


def _remote(src, dst, send, recv, to):
    return functools.partial(pltpu.make_async_remote_copy, src_ref=src, dst_ref=dst, send_sem=send, recv_sem=recv,
                             device_id=to, device_id_type=MESH)


def _carry_gather_send(bufs):
    n = len(bufs)

    def copies(ins, outs, sems):
        x, y, c, kc, sibling, chips = _mesh_pos()
        starts, waits = [], []
        for wi in range(n):
            rh = outs[wi].shape[1] // 2
            for jj, chip in enumerate(chips):
                mine = outs[wi].at[kc, _half_rows(c, rh), :]
                cp = _remote(mine, mine, sems[0].at[wi, jj], sems[1].at[wi, jj], (*chip, c))
                starts.append(cp)
                waits.append((cp, "send"))
                theirs = outs[wi].at[2 * chip[0] + chip[1], _half_rows(c, rh), :]
                waits.append((_remote(theirs, theirs, sems[0].at[wi, jj], sems[1].at[wi, jj], (*chip, c)), "recv"))
        return starts, waits

    return _Carry(bufs, [SDS(b.shape, b.dtype) for b in bufs], {i: i for i in range(n)},
                  [pltpu.SemaphoreType.DMA((n, 3)), pltpu.SemaphoreType.DMA((n, 3))], copies)


def _carry_gather_forward(bufs):
    n = len(bufs)

    def copies(ins, outs, sems):
        x, y, c, kc, sibling, chips = _mesh_pos()
        starts, waits = [], []
        for wi in range(n):
            rh = outs[wi].shape[1] // 2
            for jj, chip in enumerate(chips):
                got = outs[wi].at[2 * chip[0] + chip[1], _half_rows(c, rh), :]
                cp = _remote(got, got, sems[0].at[wi, jj], sems[1].at[wi, jj], sibling)
                starts.append(cp)
                waits.append((cp, "send"))
                other = outs[wi].at[2 * chip[0] + chip[1], _half_rows(1 - c, rh), :]
                waits.append((_remote(other, other, sems[0].at[wi, jj], sems[1].at[wi, jj], sibling), "recv"))
        return starts, waits

    return _Carry(bufs, [SDS(b.shape, b.dtype) for b in bufs], {i: i for i in range(n)},
                  [pltpu.SemaphoreType.DMA((n, 3)), pltpu.SemaphoreType.DMA((n, 3))], copies)


def _carry_sibling_halves(grads):
    n = len(grads)

    def copies(ins, outs, sems):
        x, y, c, kc, sibling, chips = _mesh_pos()
        cps = [_remote(ins[wi].at[:, _half_rows(1 - c, ins[wi].shape[1] // 2), :], outs[wi],
                       sems[0].at[wi], sems[1].at[wi], sibling) for wi in range(n)]
        return cps, [(cp, "both") for cp in cps]

    return _Carry(grads, [SDS((N_CHIPS, g.shape[1] // 2, g.shape[2]), F32) for g in grads], {},
                  [pltpu.SemaphoreType.DMA((n,)), pltpu.SemaphoreType.DMA((n,))], copies)


def _carry_to_owner(cpbfs):
    n = len(cpbfs)

    def copies(ins, outs, sems):
        x, y, c, kc, sibling, chips = _mesh_pos()
        starts, waits = [], []
        for wi in range(n):
            for jj, chip in enumerate(chips):
                cp = _remote(ins[wi].at[2 * chip[0] + chip[1]], outs[wi].at[kc],
                             sems[0].at[wi, jj], sems[1].at[wi, jj], (*chip, c))
                starts.append(cp)
                waits.append((cp, "send"))
                slot = outs[wi].at[2 * chip[0] + chip[1]]
                waits.append((_remote(slot, slot, sems[0].at[wi, jj], sems[1].at[wi, jj], (*chip, c)), "recv"))
        return starts, waits

    return _Carry(cpbfs, [SDS(g.shape, BF16) for g in cpbfs], {},
                  [pltpu.SemaphoreType.DMA((n, 3)), pltpu.SemaphoreType.DMA((n, 3))], copies)


def _carry_join_halves(bufs):
    n = len(bufs)

    def copies(ins, outs, sems):
        x, y, c, kc, sibling, chips = _mesh_pos()
        cps = []
        for wi in range(n):
            mine = outs[wi].at[_half_rows(c, outs[wi].shape[0] // 2), :]
            cps.append(_remote(mine, mine, sems[0].at[wi], sems[1].at[wi], sibling))
        return cps, [(cp, "both") for cp in cps]

    return _Carry(bufs, [SDS(b.shape, F32) for b in bufs], {i: i for i in range(n)},
                  [pltpu.SemaphoreType.DMA((n,)), pltpu.SemaphoreType.DMA((n,))], copies)


def _merge_carries(*carries):
    ins, outs, alias, sems, parts = [], [], {}, [], []
    for cy in carries:
        parts.append((len(ins), len(cy.ins), len(outs), len(cy.outs), len(sems), len(cy.sems), cy.copies))
        alias.update({len(ins) + i: len(outs) + o for i, o in cy.alias.items()})
        ins += cy.ins
        outs += cy.outs
        sems += cy.sems

    def copies(i, o, s):
        starts, waits = [], []
        for i0, ni, o0, no, s0, ns, fn in parts:
            st, wt = fn(i[i0:i0 + ni], o[o0:o0 + no], s[s0:s0 + ns])
            starts += st
            waits += wt
        return starts, waits

    return _Carry(ins, outs, alias, sems, copies)


def _start_all(starts):
    for cp in starts:
        cp().start()


def _wait_all(waits):
    for cp, which in waits:
        if which == "send":
            cp().wait_send()
        elif which == "recv":
            cp().wait_recv()
        else:
            cp().wait()


def _comm_call(name, carry):
    n_i, n_o = len(carry.ins), len(carry.outs)

    def body(*refs):
        ins, outs, sems = refs[:n_i], refs[n_i:n_i + n_o], refs[n_i + n_o:]
        _start_all(carry.copies(ins, outs, sems)[0])
        _wait_all(carry.copies(ins, outs, sems)[1])

    return pl.pallas_call(
        body, name=name, out_shape=carry.outs, in_specs=[ANY_SPEC] * n_i, out_specs=[ANY_SPEC] * n_o,
        input_output_aliases=carry.alias, scratch_shapes=carry.sems,
    )(*carry.ins)


def _host_call(body, *, name, grid, in_specs, out_specs, out_shape, args, scratch_shapes=(), carry=None, then=None,
               prefetch=None, aliases=None):
    n_in, n_out, n_scr = len(in_specs), len(out_specs), len(scratch_shapes)
    n_ci = len(carry.ins) if carry else 0
    n_co = len(carry.outs) if carry else 0
    n_cs = len(carry.sems) if carry else 0
    n_pf = 0 if prefetch is None else 1

    def wrapped(*refs):
        pf, refs = refs[:n_pf], refs[n_pf:]
        ins, cins = refs[:n_in], refs[n_in:n_in + n_ci]
        o0 = n_in + n_ci
        outs, couts = refs[o0:o0 + n_out], refs[o0 + n_out:o0 + n_out + n_co]
        s0 = o0 + n_out + n_co
        scr, sems, tsems = refs[s0:s0 + n_scr], refs[s0 + n_scr:s0 + n_scr + n_cs], refs[s0 + n_scr + n_cs:]
        idx = [pl.program_id(a) for a in range(len(grid))]
        first = functools.reduce(jnp.logical_and, [i == 0 for i in idx])
        last = functools.reduce(jnp.logical_and, [i == g - 1 for i, g in zip(idx, grid)])

        if carry:
            @pl.when(first)
            def _():
                _start_all(carry.copies(cins, couts, sems)[0])

        body(*pf, *ins, *outs, *scr)

        if carry:
            @pl.when(last)
            def _():
                _wait_all(carry.copies(cins, couts, sems)[1])
                if then:
                    _start_all(then.copies(couts, couts, tsems)[0])
                    _wait_all(then.copies(couts, couts, tsems)[1])

    all_in = list(in_specs) + [ANY_SPEC] * n_ci
    all_out = list(out_specs) + [ANY_SPEC] * n_co
    all_scr = list(scratch_shapes) + (carry.sems if carry else []) + (then.sems if then else [])
    alias = {n_pf + i: o for i, o in (aliases or {}).items()}
    if carry:
        alias.update({n_pf + n_in + i: n_out + o for i, o in carry.alias.items()})
    kwargs = dict(name=name, out_shape=list(out_shape) + (carry.outs if carry else []), input_output_aliases=alias,
                  compiler_params=_params(len(grid)))
    if prefetch is None:
        call = pl.pallas_call(wrapped, grid=grid, in_specs=all_in, out_specs=all_out, scratch_shapes=all_scr, **kwargs)
        res = call(*args, *(carry.ins if carry else []))
    else:
        call = pl.pallas_call(wrapped, grid_spec=pltpu.PrefetchScalarGridSpec(
            num_scalar_prefetch=1, grid=grid, in_specs=all_in, out_specs=all_out, scratch_shapes=all_scr), **kwargs)
        res = call(prefetch, *args, *(carry.ins if carry else []))
    return list(res[:n_out]), list(res[n_out:])


def _rs_add_halves(pos, grad, recv, name):
    _, rs, cs = grad.shape
    rh = rs // 2
    rb = rh // 2

    def body(pos_ref, g_ref, r_ref, o32_ref, obf_ref):
        s = g_ref[...] + r_ref[...]
        o32_ref[...] = s
        obf_ref[...] = s.astype(BF16)

    blk = (1, rb, cs)
    return pl.pallas_call(
        body, name=name,
        grid_spec=pltpu.PrefetchScalarGridSpec(
            num_scalar_prefetch=1, grid=(N_CHIPS, 2),
            in_specs=[pl.BlockSpec(blk, lambda k, i, p: (k, p[1] * 2 + i, 0)),
                      pl.BlockSpec(blk, lambda k, i, p: (k, i, 0))],
            out_specs=[pl.BlockSpec(blk, lambda k, i, p: (k, i, 0)), pl.BlockSpec(blk, lambda k, i, p: (k, i, 0))]),
        out_shape=[SDS((N_CHIPS, rh, cs), F32), SDS((N_CHIPS, rh, cs), BF16)],
        compiler_params=_params(2),
    )(pos, grad, recv)


def _rs_sum_owner(pos, cp32, recv3, name):
    _, rh, cs = cp32.shape
    rb = rh // 2

    def body(pos_ref, own_ref, r1_ref, r2_ref, r3_ref, o_ref):
        o_ref[...] = ((own_ref[0] + r1_ref[0].astype(F32)) + r2_ref[0].astype(F32)) + r3_ref[0].astype(F32)

    blk = (1, rb, cs)

    def slot(d):
        return pl.BlockSpec(blk, lambda i, p: ((p[0] + d) % N_CHIPS, i, 0))

    return pl.pallas_call(
        body, name=name,
        grid_spec=pltpu.PrefetchScalarGridSpec(
            num_scalar_prefetch=1, grid=(2,),
            in_specs=[slot(0), slot(1), slot(2), slot(3)],
            out_specs=pl.BlockSpec((rb, cs), lambda i, p: (p[1] * 2 + i, 0))),
        out_shape=SDS((2 * rh, cs), F32),
        compiler_params=_params(1),
    )(pos, cp32, recv3, recv3, recv3)


def _cast_bf16(pos, arrs):
    n = len(arrs)

    def body(pos_ref, *refs):
        for i in range(n):
            refs[n + i][0] = refs[i][...].astype(BF16)

    return pl.pallas_call(
        body, name="cast_bf16",
        grid_spec=pltpu.PrefetchScalarGridSpec(
            num_scalar_prefetch=1, grid=(2,),
            in_specs=[pl.BlockSpec((a.shape[0] // 2, a.shape[1]), lambda i, p: (i, 0)) for a in arrs],
            out_specs=[pl.BlockSpec((1, a.shape[0] // 2, a.shape[1]), lambda i, p: (p[0], i, 0)) for a in arrs]),
        out_shape=[SDS((N_CHIPS,) + a.shape, BF16) for a in arrs],
        compiler_params=_params(1),
    )(pos, *arrs)


def _adamw_vals(w, g, m, v):
    m2 = ADAM_B1 * m + (1.0 - ADAM_B1) * g
    v2 = ADAM_B2 * v + (1.0 - ADAM_B2) * (g * g)
    m_hat = m2 / (1.0 - ADAM_B1 ** ADAM_STEP)
    v_hat = v2 / (1.0 - ADAM_B2 ** ADAM_STEP)
    delta = -ADAM_LR * (m_hat / (jnp.sqrt(v_hat) + ADAM_EPS) + ADAM_WD * w)
    return delta, m2, v2


def _adamw_big(w, g, m, v, name):
    rows, cols = w.shape
    rb = rows // 4

    def body(w_ref, g_ref, m_ref, v_ref, d_ref, m2_ref, v2_ref):
        d, m2, v2 = _adamw_vals(w_ref[...], g_ref[...], m_ref[...], v_ref[...])
        d_ref[...] = d
        m2_ref[...] = m2
        v2_ref[...] = v2

    spec = pl.BlockSpec((rb, cols), lambda i: (i, 0))
    return pl.pallas_call(
        body, name=name, grid=(4,), in_specs=[spec] * 4, out_specs=[spec] * 3,
        out_shape=[SDS(w.shape, F32)] * 3, compiler_params=_params(1),
    )(w, g, m, v)


ADAMW_GROUP_STEPS = 8


def _adamw_group(ws, gs, ms, vs, name, carry=None):
    n = len(ws)

    def body(*refs):
        for i in range(n):
            d, m2, v2 = _adamw_vals(refs[i][...], refs[n + i][...], refs[2 * n + i][...], refs[3 * n + i][...])
            refs[4 * n + i][...] = d
            refs[5 * n + i][...] = m2
            refs[6 * n + i][...] = v2

    specs = [pl.BlockSpec((w.shape[0] // ADAMW_GROUP_STEPS, w.shape[1]), lambda i: (i, 0)) for w in ws]
    shapes = [SDS(w.shape, F32) for w in ws]
    outs, carried = _host_call(
        body, name=name, grid=(ADAMW_GROUP_STEPS,), in_specs=specs * 4, out_specs=specs * 3, out_shape=shapes * 3,
        args=(*ws, *gs, *ms, *vs), carry=carry)
    return (outs[:n], outs[n:2 * n], outs[2 * n:]), carried


def _adamw_small(ws, gs, ms, vs):
    n = len(ws)

    def body(*refs):
        for i in range(n):
            d, m2, v2 = _adamw_vals(refs[i][...], refs[n + i][...], refs[2 * n + i][...], refs[3 * n + i][...])
            refs[4 * n + i][...] = d
            refs[5 * n + i][...] = m2
            refs[6 * n + i][...] = v2

    shapes = [SDS(w.shape, F32) for w in ws]
    outs = pl.pallas_call(
        body, name="adamw_small", out_shape=shapes * 3,
        in_specs=[VMEM_SPEC] * (4 * n), out_specs=[VMEM_SPEC] * (3 * n),
    )(*ws, *gs, *ms, *vs)
    return outs[:n], outs[n:2 * n], outs[2 * n:]


def _mod_fwd(cond64, w_mod_s, b_mod_s):
    def body(c_ref, w_ref, b_ref, o_ref):
        cc = c_ref[...]
        o_ref[...] = _dot_f32(cc * _sig(cc), w_ref[...]) + b_ref[...]

    return pl.pallas_call(
        body, name="mod_fwd", out_shape=SDS((cond64.shape[0], w_mod_s.shape[1]), F32),
        in_specs=[VMEM_SPEC] * 3, out_specs=VMEM_SPEC,
        compiler_params=pltpu.CompilerParams(vmem_limit_bytes=VMEM_LIMIT),
    )(cond64, w_mod_s, b_mod_s)


def _mod_bwd(cond64, dmod64, dmod64_my, w_mod_s, c_ctx):
    def body(c_ref, g_ref, gm_ref, w_ref, cc_ref, gw_ref, gb_ref, gcc_ref):
        cc = c_ref[...]
        act = cc * _sig(cc)
        gm = gm_ref[...]
        gw_ref[...] = _dot_f32(act, gm, (((0,), (0,)), ((), ())))
        gb_ref[...] = _colsum(g_ref[...])
        dact = _dot_f32(gm, w_ref[...], (((1,), (1,)), ((), ())))
        tot = dact[4:5, :]
        for dev in range(1, N_DEV):
            tot = tot + dact[8 * dev + 4:8 * dev + 5, :]
        c0 = cc_ref[...]
        s0 = _sig(c0)
        gcc_ref[...] = tot * (s0 * (1.0 + c0 * (1.0 - s0)))

    return pl.pallas_call(
        body, name="mod_bwd",
        out_shape=[SDS(w_mod_s.shape, F32), SDS((1, dmod64.shape[1]), F32), SDS((1, D_MODEL), F32)],
        in_specs=[VMEM_SPEC] * 5, out_specs=[VMEM_SPEC] * 3,
        compiler_params=pltpu.CompilerParams(vmem_limit_bytes=VMEM_LIMIT),
    )(cond64, dmod64, dmod64_my, w_mod_s, c_ctx)


SHARD_COLS = IN_COLS // N_CHIPS


def _in_fwd_own(pos, x, ctx, g_mix, mod_a, w_own, carry=None, then=None):
    bs, seq, _ = x.shape
    nb = seq // ROW_BLOCK + 1

    def body(pos_ref, x_ref, ctx_ref, g_ref, mod_ref, w_ref, p_ref, h_ref, w_bf):
        b, j = pl.program_id(0), pl.program_id(1)

        @pl.when((b == 0) & (j == 0))
        def _():
            w_bf[...] = w_ref[...].astype(BF16)

        is_ctx = j == 0
        xin = jnp.where(is_ctx, ctx_ref[0], x_ref[0])
        shift = jnp.where(is_ctx, mod_ref[0, 2:3, :], mod_ref[0, 0:1, :])
        scale = jnp.where(is_ctx, mod_ref[0, 3:4, :], mod_ref[0, 1:2, :])
        xn, _ = _rms(xin)
        hb = ((xn * g_ref[...]) * (1.0 + scale) + shift).astype(BF16)
        h_ref[0] = hb
        p_ref[0] = jnp.dot(hb, w_bf[...], preferred_element_type=F32)

    return _host_call(
        body, name="in_fwd_own", grid=(bs, nb), prefetch=pos,
        in_specs=[pl.BlockSpec((1, ROW_BLOCK, D_MODEL), lambda b, j, p: (b, jnp.maximum(j - 1, 0), 0)),
                  pl.BlockSpec((1, ROW_BLOCK, D_MODEL), lambda b, j, p: (b, 0, 0)),
                  pl.BlockSpec((1, D_MODEL), lambda b, j, p: (0, 0)),
                  pl.BlockSpec((1, 8, D_MODEL), lambda b, j, p: (b, 0, 0)),
                  pl.BlockSpec((D_MODEL, SHARD_COLS), lambda b, j, p: (0, 0))],
        out_specs=[pl.BlockSpec((1, ROW_BLOCK, SHARD_COLS), lambda b, j, p: (b, j, p[0])),
                   pl.BlockSpec((1, ROW_BLOCK, D_MODEL), lambda b, j, p: (b, j, 0))],
        out_shape=[SDS((bs, nb * ROW_BLOCK, IN_COLS), F32), SDS((bs, nb * ROW_BLOCK, D_MODEL), BF16)],
        scratch_shapes=[pltpu.VMEM((D_MODEL, SHARD_COLS), BF16)],
        args=(x, ctx, g_mix, mod_a, w_own), carry=carry, then=then)


def _in_fwd_rest(pos, h_all, w_in_g, p, carry=None):
    bs, rows, _ = h_all.shape
    rows_all = bs * rows
    tile = next(m * ROW_BLOCK for m in (3, 1) if rows_all % (m * ROW_BLOCK) == 0)

    def body(pos_ref, h_ref, w_ref, p_in_ref, p_ref):
        p_ref[...] = jnp.dot(h_ref[...], w_ref[0], preferred_element_type=F32)

    shard = lambda n, p: (p[0] + 1 + n) % N_CHIPS
    (p2,), carried = _host_call(
        body, name="in_fwd_rest", grid=(N_CHIPS - 1, rows_all // tile), prefetch=pos,
        in_specs=[pl.BlockSpec((tile, D_MODEL), lambda n, t, p: (t, 0)),
                  pl.BlockSpec((1, D_MODEL, SHARD_COLS), lambda n, t, p: (shard(n, p), 0, 0)),
                  ANY_SPEC],
        out_specs=[pl.BlockSpec((tile, SHARD_COLS), lambda n, t, p: (t, shard(n, p)))],
        out_shape=[SDS((rows_all, IN_COLS), F32)], aliases={2: 0},
        args=(h_all.reshape(rows_all, D_MODEL), w_in_g, p.reshape(rows_all, IN_COLS)), carry=carry)
    return p2.reshape(bs, rows, IN_COLS), carried


def _in_bwd(x, ctx, dx1, g_mix, mod_a, w_in, df_f, df_b, dv_f, dv_b, dq_f, dq_b, dpc, carry=None):
    bs, seq, _ = x.shape
    nb = seq // ROW_BLOCK + 1

    def body(x_ref, ctx_ref, dx1_ref, g_ref, mod_ref, w_ref, dff_ref, dfb_ref, dvf_ref, dvb_ref, dqf_ref, dqb_ref,
             dpc_ref, gx_ref, dp_ref, dg_ref, dmod_ref):
        b, j = pl.program_id(0), pl.program_id(1)
        is_ctx = j == 0

        @pl.when((b == 0) & (j == 0))
        def _():
            dg_ref[...] = jnp.zeros_like(dg_ref)

        @pl.when(j == 0)
        def _():
            dmod_ref[...] = jnp.zeros_like(dmod_ref)

        di = (dvf_ref[0] + dvb_ref[0]).astype(BF16)
        dq = (dqf_ref[0] + dqb_ref[0]).astype(BF16)
        dp = jnp.concatenate([dff_ref[0], dfb_ref[0], di, dq, dpc_ref[0]], axis=1)
        dp_ref[0] = dp
        dh = lax.dot_general(dp, w_ref[...], (((1,), (1,)), ((), ())), preferred_element_type=F32)
        xin = jnp.where(is_ctx, ctx_ref[0], x_ref[0])
        scale = jnp.where(is_ctx, mod_ref[0, 3:4, :], mod_ref[0, 1:2, :])
        xn, r = _rms(xin)
        g = g_ref[...]
        hn = xn * g
        d_shift = _colsum(dh)
        d_scale = _colsum(dh * hn)
        dhn = dh * (1.0 + scale)
        dg_ref[...] += _colsum(dhn * xn)
        dx = _rms_bwd(dhn * g, xn, r)

        @pl.when(is_ctx)
        def _():
            dmod_ref[0, 2:3, :] += d_shift
            dmod_ref[0, 3:4, :] += d_scale

        @pl.when(jnp.logical_not(is_ctx))
        def _():
            dmod_ref[0, 0:1, :] += d_shift
            dmod_ref[0, 1:2, :] += d_scale
            gx_ref[0] = dx + dx1_ref[0]

    def rows(w):
        return pl.BlockSpec((1, ROW_BLOCK, w), lambda b, j: (b, j, 0))

    lat = pl.BlockSpec((1, ROW_BLOCK, D_MODEL), lambda b, j: (b, jnp.maximum(j - 1, 0), 0))
    return _host_call(
        body, name="in_bwd", grid=(bs, nb),
        in_specs=[lat, pl.BlockSpec((1, ROW_BLOCK, D_MODEL), lambda b, j: (b, 0, 0)), lat,
                  pl.BlockSpec((1, D_MODEL), lambda b, j: (0, 0)),
                  pl.BlockSpec((1, 8, D_MODEL), lambda b, j: (b, 0, 0)),
                  pl.BlockSpec((D_MODEL, IN_COLS), lambda b, j: (0, 0)),
                  rows(KW), rows(KW), rows(KW), rows(KW), rows(KW), rows(KW), rows(7 * KW)],
        out_specs=[lat, rows(IN_COLS), pl.BlockSpec((1, D_MODEL), lambda b, j: (0, 0)),
                   pl.BlockSpec((1, 8, D_MODEL), lambda b, j: (b, 0, 0))],
        out_shape=[SDS(x.shape, F32), SDS((bs, nb * ROW_BLOCK, IN_COLS), BF16), SDS((1, D_MODEL), F32),
                   SDS((bs, 8, D_MODEL), F32)],
        args=(x, ctx, dx1, g_mix, mod_a, w_in, df_f, df_b, dv_f, dv_b, dq_f, dq_b, dpc), carry=carry)


def _lower_bound(lbg_ref, direction):
    return _sig(lbg_ref[0, direction:direction + 1, :] - lbg_ref[1, direction:direction + 1, :])


N_CHUNKS = ROW_BLOCK // HGRN_CHUNK


def _block_tri(upper):
    t = np.arange(ROW_BLOCK)[:, None]
    s = np.arange(ROW_BLOCK)[None, :]
    same = (t // HGRN_CHUNK) == (s // HGRN_CHUNK)
    return jnp.asarray(same & ((s >= t) if upper else (s <= t)), dtype=BF16)


TRI_SPEC = pl.BlockSpec((ROW_BLOCK, ROW_BLOCK), lambda b, j: (0, 0))


def _tri_matmul_f32(tri, g):
    g0 = g.astype(BF16)
    r1 = g - g0.astype(F32)
    g1 = r1.astype(BF16)
    g2 = (r1 - g1.astype(F32)).astype(BF16)
    return (jnp.dot(tri, g2, preferred_element_type=F32) + jnp.dot(tri, g1, preferred_element_type=F32)) \
        + jnp.dot(tri, g0, preferred_element_type=F32)


def _chunk_rows(rows):
    return jnp.concatenate([jnp.broadcast_to(r, (HGRN_CHUNK, r.shape[1])) for r in rows], axis=0)


def _block_gates(fl, q, lb, tri, upper):
    t = {}
    t["sg"] = _sig(fl)
    t["f"] = lb + (1.0 - lb) * t["sg"]
    k = 1.0 - t["f"]
    bcum = _tri_matmul_f32(tri, jnp.log(t["f"]))
    ends = [bcum[ci * HGRN_CHUNK:ci * HGRN_CHUNK + 1] if upper else bcum[(ci + 1) * HGRN_CHUNK - 1:(ci + 1) * HGRN_CHUNK]
            for ci in range(fl.shape[0] // HGRN_CHUNK)]
    mid = _chunk_rows([0.5 * r for r in ends])
    t["dec"] = [jnp.exp(r) for r in ends]
    t["e1"] = jnp.exp(bcum - mid)
    t["e2"] = jnp.exp(mid - bcum)
    t["eh"] = _chunk_rows([jnp.exp(0.5 * r) for r in ends])
    t["qi"] = q * t["e1"]
    t["ki"] = k * t["e2"]
    t["kd"] = t["ki"] * t["eh"]
    t["qe"] = t["qi"] * t["eh"]
    return t


def _hgrn_block_order(direction, nb):
    if direction == 0:
        return lambda j: j
    return lambda j: jnp.where(j == 0, 0, nb - j)


def _hgrn_fwd(p, lbg, carry=None, then=None):
    bs, rows, _ = p.shape
    nb = rows // ROW_BLOCK
    ncb = ROW_BLOCK // HGRN_CHUNK
    orders = [_hgrn_block_order(d, nb) for d in (0, 1)]
    dirs = (0, 1)

    def body(f0_ref, i0_ref, q0_ref, f1_ref, i1_ref, q1_ref, lbg_ref, tri0_ref, tri1_ref,
             o0_ref, s0_ref, o1_ref, s1_ref, st):
        @pl.when(pl.program_id(1) == 0)
        def _():
            st[...] = jnp.zeros_like(st)

        f_refs, i_refs, q_refs = (f0_ref, f1_ref), (i0_ref, i1_ref), (q0_ref, q1_ref)
        tri_refs, o_refs, s_refs = (tri0_ref, tri1_ref), (o0_ref, o1_ref), (s0_ref, s1_ref)
        chunk = lambda a, ci, h: a[ci * HGRN_CHUNK:(ci + 1) * HGRN_CHUNK, h * HEAD_DIM:(h + 1) * HEAD_DIM]
        masks = [_tri(HGRN_CHUNK, d == 1) for d in dirs]
        t = [_block_gates(f_refs[d][0], q_refs[d][0], _lower_bound(lbg_ref, d), tri_refs[d][...], d == 1) for d in dirs]
        v = [i_refs[d][0] for d in dirs]
        intra = [[[None] * N_HEADS for _ in range(ncb)] for _ in dirs]
        ds_loc = [[[None] * N_HEADS for _ in range(ncb)] for _ in dirs]
        for ci in range(ncb):
            for h in range(N_HEADS):
                for d in dirs:
                    a = jnp.where(masks[d], _dot_nt(chunk(t[d]["qi"], ci, h), chunk(t[d]["ki"], ci, h)), 0.0)
                    intra[d][ci][h] = _dot(a, chunk(v[d], ci, h))
                    ds_loc[d][ci][h] = _dot_tn(chunk(v[d], ci, h), chunk(t[d]["kd"], ci, h))
        for h in range(N_HEADS):
            ls = slice(h * HEAD_DIM, (h + 1) * HEAD_DIM)
            s = [st[d, h] for d in dirs]
            for step in range(ncb):
                for d in dirs:
                    ci = ncb - 1 - step if d == 1 else step
                    s_refs[d][0, 0, ci, h] = s[d]
                    o_refs[d][0, ci * HGRN_CHUNK:(ci + 1) * HGRN_CHUNK, ls] = (
                        intra[d][ci][h] + _dot_nt(chunk(t[d]["qe"], ci, h), s[d]))
                    s[d] = s[d] * t[d]["dec"][ci][:, ls] + ds_loc[d][ci][h]
            for d in dirs:
                st[d, h] = s[d]

    def col(d, cb):
        return pl.BlockSpec((1, ROW_BLOCK, KW), lambda b, j: (b, orders[d](j), cb))

    def outs(d):
        return [pl.BlockSpec((1, ROW_BLOCK, KW), lambda b, j: (b, orders[d](j), 0)),
                pl.BlockSpec((1, 1, ncb, N_HEADS, HEAD_DIM, HEAD_DIM), lambda b, j: (b, orders[d](j), 0, 0, 0, 0))]

    shapes = [SDS((bs, rows, KW), F32), SDS((bs, nb, ncb, N_HEADS, HEAD_DIM, HEAD_DIM), F32)]
    return _host_call(
        body, name="hgrn_fwd", grid=(bs, nb),
        in_specs=[col(0, 0), col(0, 2), col(0, 3), col(1, 1), col(1, 2), col(1, 3),
                  pl.BlockSpec((2, 2, KW), lambda b, j: (0, 0, 0)), TRI_SPEC, TRI_SPEC],
        out_specs=outs(0) + outs(1), out_shape=shapes * 2,
        scratch_shapes=[pltpu.VMEM((2, N_HEADS, HEAD_DIM, HEAD_DIM), F32)],
        args=(p, p, p, p, p, p, lbg, _block_tri(False), _block_tri(True)), carry=carry, then=then)


def _hgrn_bwd_pair(p, lbg, s_saved, do_raw, carry=None):
    bs, rows, _ = p.shape
    nb = rows // ROW_BLOCK
    ncb = ROW_BLOCK // HGRN_CHUNK
    dirs = (0, 1)
    fwd_orders = [_hgrn_block_order(d, nb) for d in dirs]
    orders = [lambda j, d=d: fwd_orders[d](nb - 1 - j) for d in dirs]
    pairs = [(ci, h) for ci in range(ncb) for h in range(N_HEADS)]

    def body(f0_ref, i0_ref, q0_ref, s0_ref, do0_ref, f1_ref, i1_ref, q1_ref, s1_ref, do1_ref,
             lbg_ref, tril_ref, triu_ref,
             df0_ref, dq0_ref, dv0_ref, dlb0_ref, df1_ref, dq1_ref, dv1_ref, dlb1_ref, dst, acc):
        b, j = pl.program_id(0), pl.program_id(1)
        f_refs, i_refs, q_refs = (f0_ref, f1_ref), (i0_ref, i1_ref), (q0_ref, q1_ref)
        s_refs, do_refs = (s0_ref, s1_ref), (do0_ref, do1_ref)
        df_refs, dq_refs, dv_refs, dlb_refs = (df0_ref, df1_ref), (dq0_ref, dq1_ref), (dv0_ref, dv1_ref), (dlb0_ref, dlb1_ref)
        tri_refs, trit_refs = (tril_ref, triu_ref), (triu_ref, tril_ref)

        @pl.when((b == 0) & (j == 0))
        def _():
            dlb0_ref[...] = jnp.zeros_like(dlb0_ref)
            dlb1_ref[...] = jnp.zeros_like(dlb1_ref)

        @pl.when(j == 0)
        def _():
            dst[...] = jnp.zeros_like(dst)

        chunk = lambda a, ci, h: a[ci * HGRN_CHUNK:(ci + 1) * HGRN_CHUNK, h * HEAD_DIM:(h + 1) * HEAD_DIM]
        rows_of = lambda ci: slice(ci * HGRN_CHUNK, (ci + 1) * HGRN_CHUNK)
        lanes_of = lambda h: slice(h * HEAD_DIM, (h + 1) * HEAD_DIM)
        grid3 = lambda: [[[None] * N_HEADS for _ in range(ncb)] for _ in dirs]
        lbs = [_lower_bound(lbg_ref, d) for d in dirs]
        masks = [_tri(HGRN_CHUNK, d == 1) for d in dirs]
        masks_t = [_tri(HGRN_CHUNK, d != 1) for d in dirs]
        t = [_block_gates(f_refs[d][0], q_refs[d][0], lbs[d], tri_refs[d][...], d == 1) for d in dirs]
        v = [i_refs[d][0] for d in dirs]
        do = [do_refs[d][0] for d in dirs]
        a_t, da, da_t, dv_in, ds_loc = (grid3() for _ in range(5))
        for ci, h in pairs:
            for d in dirs:
                a_t[d][ci][h] = _dot_nt(chunk(t[d]["ki"], ci, h), chunk(t[d]["qi"], ci, h))
        for ci, h in pairs:
            for d in dirs:
                da[d][ci][h] = _dot_nt(chunk(do[d], ci, h), chunk(v[d], ci, h))
        for ci, h in pairs:
            for d in dirs:
                da_t[d][ci][h] = _dot_nt(chunk(v[d], ci, h), chunk(do[d], ci, h))
        for ci, h in pairs:
            for d in dirs:
                acc[d, 3, rows_of(ci), lanes_of(h)] = _dot(chunk(do[d], ci, h), s_refs[d][0, 0, ci, h])
        for ci, h in pairs:
            for d in dirs:
                ds_loc[d][ci][h] = _dot_tn(chunk(do[d], ci, h), chunk(t[d]["qe"], ci, h))
        for ci, h in pairs:
            for d in dirs:
                acc[d, 0, rows_of(ci), lanes_of(h)] = _dot(jnp.where(masks[d], da[d][ci][h], 0.0),
                                                           chunk(t[d]["ki"], ci, h))
        for ci, h in pairs:
            for d in dirs:
                acc[d, 1, rows_of(ci), lanes_of(h)] = _dot(jnp.where(masks_t[d], da_t[d][ci][h], 0.0),
                                                           chunk(t[d]["qi"], ci, h))
        for ci, h in pairs:
            for d in dirs:
                dv_in[d][ci][h] = _dot(jnp.where(masks_t[d], a_t[d][ci][h], 0.0), chunk(do[d], ci, h))
        ddec = grid3()
        for h in range(N_HEADS):
            ls = lanes_of(h)
            ds = [dst[d, h] for d in dirs]
            for step in range(ncb):
                for d in dirs:
                    ci = step if d == 1 else ncb - 1 - step
                    acc[d, 2, rows_of(ci), ls] = _dot(chunk(v[d], ci, h), ds[d])
                    acc[d, 4, rows_of(ci), ls] = dv_in[d][ci][h] + _dot_nt(chunk(t[d]["kd"], ci, h), ds[d])
                    ddec[d][ci][h] = _colsum(ds[d] * s_refs[d][0, 0, ci, h])
                    ds[d] = ds[d] * t[d]["dec"][ci][:, ls] + ds_loc[d][ci][h]
            for d in dirs:
                dst[d, h] = ds[d]
        for d in dirs:
            td = t[d]
            dqi, dki, dkd, dqe = (acc[d, i] for i in range(4))
            dq_refs[d][0] = td["e1"] * (dqi + dqe * td["eh"])
            dv_refs[d][0] = acc[d, 4]
            dk = td["e2"] * (dki + dkd * td["eh"])
            dkd_kd = dkd * td["kd"]
            db = dqi * td["qi"] - dki * td["ki"] - dkd_kd + dqe * td["qe"]
            dbl = [_colsum(dkd_kd[rows_of(ci)]) + jnp.concatenate(ddec[d][ci], axis=1) * td["dec"][ci]
                   for ci in range(ncb)]
            dg = _tri_matmul_f32(trit_refs[d][...], db) + _chunk_rows(dbl)
            df = dg / td["f"] - dk
            sg = td["sg"]
            dlb_refs[d][...] += _colsum(df * (1.0 - sg))
            df_refs[d][0] = (df * (1.0 - lbs[d]) * sg * (1.0 - sg)).astype(BF16)

    def ins(d):
        col = lambda cb: pl.BlockSpec((1, ROW_BLOCK, KW), lambda b, j: (b, orders[d](j), cb))
        return [col(d), col(2), col(3),
                pl.BlockSpec((1, 1, ncb, N_HEADS, HEAD_DIM, HEAD_DIM), lambda b, j: (b, orders[d](j), 0, 0, 0, 0)),
                pl.BlockSpec((1, ROW_BLOCK, KW), lambda b, j: (b, orders[d](j), 0))]

    def outs(d):
        row = pl.BlockSpec((1, ROW_BLOCK, KW), lambda b, j: (b, orders[d](j), 0))
        return [row, row, row, pl.BlockSpec((1, KW), lambda b, j: (0, 0))]

    shapes = [SDS((bs, rows, KW), BF16), SDS((bs, rows, KW), F32), SDS((bs, rows, KW), F32), SDS((1, KW), F32)]
    return _host_call(
        body, name="hgrn_bwd", grid=(bs, nb),
        in_specs=ins(0) + ins(1) + [pl.BlockSpec((2, 2, KW), lambda b, j: (0, 0, 0)), TRI_SPEC, TRI_SPEC],
        out_specs=outs(0) + outs(1), out_shape=shapes * 2,
        scratch_shapes=[pltpu.VMEM((2, N_HEADS, HEAD_DIM, HEAD_DIM), F32), pltpu.VMEM((2, 5, ROW_BLOCK, KW), F32)],
        args=(p, p, p, s_saved[0], do_raw, p, p, p, s_saved[1], do_raw, lbg, _block_tri(False), _block_tri(True)),
        carry=carry)


def _mix_values(og, u, v, ga, gb, o_raw, gna, lng, lnb, ws_ref, bst, wpa, wpb, wo):
    t = {}
    sog = _sig(og)
    t["sog"], t["silu_og"] = sog, og * sog
    xh_l, r_l = [], []
    for h in range(N_HEADS):
        xh, r = _rms(o_raw[:, h * HEAD_DIM:(h + 1) * HEAD_DIM])
        xh_l.append(xh)
        r_l.append(r)
    t["xh"], t["r"] = jnp.concatenate(xh_l, axis=1), r_l
    gna4 = jnp.concatenate([gna] * N_HEADS, axis=1)
    t["gna4"] = gna4
    t["o_n"] = t["xh"] * gna4
    t["o_a"] = t["o_n"] * t["silu_og"]
    t["ya"] = _dot(t["o_a"], wpa)
    t["gu"], t["tu"] = _gelu(u)
    gv, t["tv"] = _gelu(v)
    mu = jnp.mean(gv, axis=-1, keepdims=True)
    cen = gv - mu
    t["rstd"] = lax.rsqrt(jnp.mean(cen * cen, axis=-1, keepdims=True) + EPS)
    t["xhat"] = cen * t["rstd"]
    vn = t["xhat"] * lng + lnb
    t["vn"] = vn
    chunks = []
    for n in range(ROW_BLOCK // SGU_CHUNK):
        rs = slice(n * SGU_CHUNK, (n + 1) * SGU_CHUNK)
        groups = []
        for g in range(N_HEADS):
            ls = slice(g * HEAD_DIM, (g + 1) * HEAD_DIM)
            groups.append(_dot(ws_ref[g], vn[rs, ls]) + bst[:, g:g + 1])
        chunks.append(jnp.concatenate(groups, axis=1))
    t["mixed"] = jnp.concatenate(chunks, axis=0)
    t["o_bm"] = t["gu"] * t["mixed"]
    t["yb"] = _dot(t["o_bm"], wpb)
    t["sa"], t["sb"] = _sig(ga), _sig(gb)
    t["merged"] = t["sa"] * t["ya"] + t["sb"] * t["yb"]
    t["mix"] = _dot(t["merged"], wo)
    return t


def _mix_in_specs(row_of):
    def col(cb):
        return pl.BlockSpec((1, ROW_BLOCK, KW), lambda b, j: (b, row_of(j), cb))
    return [col(cb) for cb in range(4, 11)]


def _mix_param_specs():
    full2 = lambda r, c: pl.BlockSpec((r, c), lambda b, j: (0, 0))
    return [full2(1, HEAD_DIM), full2(1, KW), full2(1, KW),
            pl.BlockSpec((N_HEADS, SGU_CHUNK, SGU_CHUNK), lambda b, j: (0, 0, 0)),
            full2(SGU_CHUNK, N_HEADS), full2(KW, D_MODEL), full2(KW, D_MODEL), full2(D_MODEL, D_MODEL)]


def _mix_fwd(p, o_f, o_b, x, mod_c, gna, lng, lnb, w_s, bst, wpa, wpb, wo):
    bs, seq, _ = x.shape
    nbl = seq // ROW_BLOCK

    def body(og_r, u_r, v_r, ga0_r, ga1_r, gb0_r, gb1_r, of_r, ob_r, x_r, mod_r,
             gna_r, lng_r, lnb_r, ws_r, bst_r, wpa_r, wpb_r, wo_r, x1_r):
        ga = jnp.concatenate([ga0_r[0], ga1_r[0]], axis=1)
        gb = jnp.concatenate([gb0_r[0], gb1_r[0]], axis=1)
        t = _mix_values(og_r[0], u_r[0], v_r[0], ga, gb, of_r[0] + ob_r[0], gna_r[...], lng_r[...], lnb_r[...],
                        ws_r, bst_r[...], wpa_r[...], wpb_r[...], wo_r[...])
        x1_r[0] = x_r[0] + mod_r[0, 0:1, :] * t["mix"]

    row = lambda w: pl.BlockSpec((1, ROW_BLOCK, w), lambda b, j: (b, j + 1, 0))
    lat = pl.BlockSpec((1, ROW_BLOCK, D_MODEL), lambda b, j: (b, j, 0))
    return pl.pallas_call(
        body, name="mix_fwd", grid=(bs, nbl),
        in_specs=_mix_in_specs(lambda j: j + 1) + [row(KW), row(KW), lat,
                                                    pl.BlockSpec((1, 8, D_MODEL), lambda b, j: (b, 0, 0))]
        + _mix_param_specs(),
        out_specs=lat, out_shape=SDS(x.shape, F32), compiler_params=_params(2),
    )(p, p, p, p, p, p, p, o_f, o_b, x, mod_c, gna, lng, lnb, w_s, bst, wpa, wpb, wo)


def _mix_bwd(p, o_f, o_b, dx1, mod_c, gna, lng, lnb, w_s, w_s_t, bst, wpa, wpb, wo, carry=None):
    bs, rows, _ = p.shape
    nb = rows // ROW_BLOCK

    def body(og_r, u_r, v_r, ga0_r, ga1_r, gb0_r, gb1_r, of_r, ob_r, dx1_r, mod_r,
             gna_r, lng_r, lnb_r, ws_r, bst_r, wpa_r, wpb_r, wo_r, wst_r,
             dor_r, dpc_r, dwpa_r, dwpb_r, dwo_r, dgna_r, dlng_r, dlnb_r, dws_r, dbst_r, dmod_r):
        b, j = pl.program_id(0), pl.program_id(1)

        @pl.when((b == 0) & (j == 0))
        def _():
            for r in (dwpa_r, dwpb_r, dwo_r, dgna_r, dlng_r, dlnb_r, dws_r, dbst_r):
                r[...] = jnp.zeros_like(r)

        @pl.when(j == 0)
        def _():
            dmod_r[...] = jnp.zeros_like(dmod_r)
            dor_r[...] = jnp.zeros_like(dor_r)
            dpc_r[...] = jnp.zeros_like(dpc_r)

        @pl.when(j > 0)
        def _():
            og, u, v = og_r[0], u_r[0], v_r[0]
            ga = jnp.concatenate([ga0_r[0], ga1_r[0]], axis=1)
            gb = jnp.concatenate([gb0_r[0], gb1_r[0]], axis=1)
            gna, lng = gna_r[...], lng_r[...]
            wpa, wpb, wo = wpa_r[...], wpb_r[...], wo_r[...]
            dx1 = dx1_r[0]
            dmix = mod_r[0, 0:1, :] * dx1
            dmerged = _dot_nt(dmix, wo)
            t = _mix_values(og, u, v, ga, gb, of_r[0] + ob_r[0], gna, lng, lnb_r[...],
                            ws_r, bst_r[...], wpa, wpb, wo)
            dmod_r[0, 0:1, :] += _colsum(dx1 * t["mix"])
            dwo_r[...] += _dot_tn(t["merged"], dmix)
            sa, sb = t["sa"], t["sb"]
            dya, dyb = sa * dmerged, sb * dmerged
            dga = dmerged * t["ya"] * sa * (1.0 - sa)
            dgb = dmerged * t["yb"] * sb * (1.0 - sb)
            do_a = _dot_nt(dya, wpa)
            dwpa_r[...] += _dot_tn(t["o_a"], dya)
            do_bm = _dot_nt(dyb, wpb)
            dwpb_r[...] += _dot_tn(t["o_bm"], dyb)
            sog = t["sog"]
            dog = do_a * t["o_n"] * (sog * (1.0 + og * (1.0 - sog)))
            do_n = do_a * t["silu_og"]
            dxh = do_n * t["gna4"]
            prod = do_n * t["xh"]
            dgna = jnp.zeros((1, HEAD_DIM), F32)
            dor_l = []
            for h in range(N_HEADS):
                ls = slice(h * HEAD_DIM, (h + 1) * HEAD_DIM)
                dgna = dgna + _colsum(prod[:, ls])
                dor_l.append(_rms_bwd(dxh[:, ls], t["xh"][:, ls], t["r"][h]))
            dgna_r[...] += dgna
            dor_r[0] = jnp.concatenate(dor_l, axis=1)
            du = do_bm * t["mixed"] * _dgelu(u, t["tu"])
            dmixed = do_bm * t["gu"]
            vn = t["vn"]
            dvn_chunks = []
            for n in range(ROW_BLOCK // SGU_CHUNK):
                rs = slice(n * SGU_CHUNK, (n + 1) * SGU_CHUNK)
                groups = []
                for g in range(N_HEADS):
                    ls = slice(g * HEAD_DIM, (g + 1) * HEAD_DIM)
                    dm = dmixed[rs, ls]
                    dws_r[g] += _dot_nt(dm, vn[rs, ls])
                    dbst_r[:, g:g + 1] += jnp.sum(dm, axis=1, keepdims=True)
                    groups.append(_dot(wst_r[g], dm))
                dvn_chunks.append(jnp.concatenate(groups, axis=1))
            dvn = jnp.concatenate(dvn_chunks, axis=0)
            xhat = t["xhat"]
            dlng_r[...] += _colsum(dvn * xhat)
            dlnb_r[...] += _colsum(dvn)
            dxhat = dvn * lng
            dgv = t["rstd"] * (dxhat - jnp.mean(dxhat, axis=-1, keepdims=True)
                               - xhat * jnp.mean(dxhat * xhat, axis=-1, keepdims=True))
            dv = dgv * _dgelu(v, t["tv"])
            dpc_r[0] = jnp.concatenate([dog, du, dv, dga, dgb], axis=1).astype(BF16)

    row = lambda w: pl.BlockSpec((1, ROW_BLOCK, w), lambda b, j: (b, j, 0))
    lat = pl.BlockSpec((1, ROW_BLOCK, D_MODEL), lambda b, j: (b, jnp.maximum(j - 1, 0), 0))
    full2 = lambda r, c: pl.BlockSpec((r, c), lambda b, j: (0, 0))
    ws_spec = pl.BlockSpec((N_HEADS, SGU_CHUNK, SGU_CHUNK), lambda b, j: (0, 0, 0))
    return _host_call(
        body, name="mix_bwd", grid=(bs, nb),
        in_specs=_mix_in_specs(lambda j: j) + [row(KW), row(KW), lat,
                                                pl.BlockSpec((1, 8, D_MODEL), lambda b, j: (b, 0, 0))]
        + _mix_param_specs() + [ws_spec],
        out_specs=[row(KW), row(7 * KW), full2(KW, D_MODEL), full2(KW, D_MODEL), full2(D_MODEL, D_MODEL),
                   full2(1, HEAD_DIM), full2(1, KW), full2(1, KW), ws_spec, full2(SGU_CHUNK, N_HEADS),
                   pl.BlockSpec((1, 8, D_MODEL), lambda b, j: (b, 0, 0))],
        out_shape=[SDS((bs, rows, KW), F32), SDS((bs, rows, 7 * KW), BF16), SDS((KW, D_MODEL), F32),
                   SDS((KW, D_MODEL), F32), SDS((D_MODEL, D_MODEL), F32), SDS((1, HEAD_DIM), F32),
                   SDS((1, KW), F32), SDS((1, KW), F32), SDS((N_HEADS, SGU_CHUNK, SGU_CHUNK), F32),
                   SDS((SGU_CHUNK, N_HEADS), F32), SDS((bs, 8, D_MODEL), F32)],
        args=(p, p, p, p, p, p, p, o_f, o_b, dx1, mod_c, gna, lng, lnb, w_s, bst, wpa, wpb, wo, w_s_t), carry=carry)


def _ffn(x1, target, mod_c, g_ffn, g_final, w_up, w_down):
    bs, seq, _ = x1.shape
    nbl = seq // ROW_BLOCK

    def body(x1_r, tg_r, mod_r, gf_r, gl_r, wu_r, wd_r,
             dx1_r, h2_r, dab_r, hid_r, dffn_r, loss_r, dgl_r, dgf_r, dmod_r):
        b, j = pl.program_id(0), pl.program_id(1)

        @pl.when((b == 0) & (j == 0))
        def _():
            for r in (loss_r, dgl_r, dgf_r):
                r[...] = jnp.zeros_like(r)

        @pl.when(j == 0)
        def _():
            dmod_r[...] = jnp.zeros_like(dmod_r)

        x1 = x1_r[0]
        shift, scale, gate = mod_r[0, 1:2, :], mod_r[0, 2:3, :], mod_r[0, 3:4, :]
        gf, gl = gf_r[...], gl_r[...]
        xn2, r2 = _rms(x1)
        hn2 = xn2 * gf
        h2 = (hn2 * (1.0 + scale) + shift).astype(BF16)
        h2_r[0] = h2
        ab = jnp.dot(h2, wu_r[...], preferred_element_type=F32)
        a, bb = ab[:, :D_FF], ab[:, D_FF:]
        sa = _sig(a)
        silu_a = a * sa
        hid = (silu_a * bb).astype(BF16)
        hid_r[0] = hid
        ffn = jnp.dot(hid, wd_r[...], preferred_element_type=F32)
        x2 = x1 + gate * ffn
        xn3, r3 = _rms(x2)
        err = xn3 * gl - tg_r[0]
        loss_r[...] += 0.5 * jnp.sum(jnp.mean(err * err, axis=-1, keepdims=True), axis=0, keepdims=True)
        dy = err * (1.0 / D_MODEL)
        dgl_r[...] += _colsum(dy * xn3)
        dx2 = _rms_bwd(dy * gl, xn3, r3)
        dmod_r[0, 3:4, :] += _colsum(dx2 * ffn)
        dffn = (gate * dx2).astype(BF16)
        dffn_r[0] = dffn
        dhid = lax.dot_general(dffn, wd_r[...], (((1,), (1,)), ((), ())), preferred_element_type=F32)
        da = dhid * bb * (sa * (1.0 + a * (1.0 - sa)))
        db = dhid * silu_a
        dab = jnp.concatenate([da, db], axis=1).astype(BF16)
        dab_r[0] = dab
        dh2 = lax.dot_general(dab, wu_r[...], (((1,), (1,)), ((), ())), preferred_element_type=F32)
        dmod_r[0, 1:2, :] += _colsum(dh2)
        dmod_r[0, 2:3, :] += _colsum(dh2 * hn2)
        dhn2 = dh2 * (1.0 + scale)
        dgf_r[...] += _colsum(dhn2 * xn2)
        dx1_r[0] = dx2 + _rms_bwd(dhn2 * gf, xn2, r2)

    lat = lambda w: pl.BlockSpec((1, ROW_BLOCK, w), lambda b, j: (b, j, 0))
    full2 = lambda r, c: pl.BlockSpec((r, c), lambda b, j: (0, 0))
    mod_spec = pl.BlockSpec((1, 8, D_MODEL), lambda b, j: (b, 0, 0))
    return pl.pallas_call(
        body, name="ffn", grid=(bs, nbl),
        in_specs=[lat(D_MODEL), lat(D_MODEL), mod_spec, full2(1, D_MODEL), full2(1, D_MODEL),
                  full2(D_MODEL, 2 * D_FF), full2(D_FF, D_MODEL)],
        out_specs=[lat(D_MODEL), lat(D_MODEL), lat(2 * D_FF), lat(D_FF), lat(D_MODEL),
                   full2(1, 1), full2(1, D_MODEL), full2(1, D_MODEL), mod_spec],
        out_shape=[SDS(x1.shape, F32), SDS(x1.shape, BF16), SDS((bs, seq, 2 * D_FF), BF16),
                   SDS((bs, seq, D_FF), BF16), SDS(x1.shape, BF16), SDS((1, 1), F32),
                   SDS((1, D_MODEL), F32), SDS((1, D_MODEL), F32), SDS((bs, 8, D_MODEL), F32)],
        compiler_params=_params(2),
    )(x1, target, mod_c, g_ffn, g_final, w_up, w_down)


def _row_tile(rows):
    return next(m * ROW_BLOCK for m in (4, 2, 1) if rows % (m * ROW_BLOCK) == 0)


def _matmul_tn(a, b, n_blocks, tk, name, carry=None):
    t, m = a.shape
    n = b.shape[1]
    tn = n // n_blocks

    def body(a_ref, b_ref, o_ref):
        @pl.when(pl.program_id(1) == 0)
        def _():
            o_ref[...] = jnp.zeros_like(o_ref)
        o_ref[0] += _dot_tn(a_ref[...], b_ref[...])

    (out,), carried = _host_call(
        body, name=name, grid=(n_blocks, t // tk),
        in_specs=[pl.BlockSpec((tk, m), lambda i, k: (k, 0)), pl.BlockSpec((tk, tn), lambda i, k: (k, i))],
        out_specs=[pl.BlockSpec((1, m, tn), lambda i, k: (i, 0, 0))],
        out_shape=[SDS((n_blocks, m, tn), F32)], args=(a, b), carry=carry)
    return out if carry is None else (out, carried)


SMALL_ROWS = 80
ROW_CCTX = 3


def _small_reduce(gathered, lbg):
    def body(g_ref, lbg_ref, s_ref, dgam_ref):
        tot = g_ref[0:SMALL_ROWS, :]
        for dev in range(1, N_DEV):
            tot = tot + g_ref[dev * SMALL_ROWS:(dev + 1) * SMALL_ROWS, :]
        s_ref[...] = tot
        cc = g_ref[ROW_CCTX:ROW_CCTX + 1, :]
        for dev in range(2, N_DEV, 2):
            cc = cc + g_ref[dev * SMALL_ROWS + ROW_CCTX:dev * SMALL_ROWS + ROW_CCTX + 1, :]
        s_ref[ROW_CCTX:ROW_CCTX + 1, :] = cc
        dlb = tot[7:8, :]
        for d in range(2):
            s0 = _sig(lbg_ref[0, d:d + 1, :] - lbg_ref[1, d:d + 1, :])
            dgam_ref[d:d + 1, :] = dlb[:, d * KW:(d + 1) * KW] * s0 * (1.0 - s0)

    return pl.pallas_call(
        body, name="small_reduce", out_shape=[SDS((SMALL_ROWS, D_MODEL), F32), SDS((2, KW), F32)],
        in_specs=[VMEM_SPEC] * 2, out_specs=[VMEM_SPEC] * 2,
    )(gathered, lbg)


def _pad_cols(a, width):
    return jnp.pad(a, ((0, 0), (0, width - a.shape[1])))


def kernel(x, c, ctx, c_ctx, w_mod, b_mod, g_mix, g_ffn, w_in, lb_gamma, g_norm_a, ln_v_g, ln_v_b, w_s, b_s, w_pa, w_pb, w_o, w_up, w_down, g_final, loss_target, m_c_ctx, m_w_mod, m_b_mod, m_g_mix, m_g_ffn, m_w_in, m_lb_gamma, m_g_norm_a, m_ln_v_g, m_ln_v_b, m_w_s, m_b_s, m_w_pa, m_w_pb, m_w_o, m_w_up, m_w_down, m_g_final, v_c_ctx, v_w_mod, v_b_mod, v_g_mix, v_g_ffn, v_w_in, v_lb_gamma, v_g_norm_a, v_ln_v_g, v_ln_v_b, v_w_s, v_b_s, v_w_pa, v_w_pb, v_w_o, v_w_up, v_w_down, v_g_final):
    ax, ay, ac = lax.axis_index("x"), lax.axis_index("y"), lax.axis_index("c")
    kc = 2 * ax + ay
    dev = 2 * kc + ac
    pos = jnp.stack([kc, ac]).astype(jnp.int32)
    bs, seq, _ = x.shape
    assert bs <= 4 and ctx.shape[1] == ROW_BLOCK and seq % ROW_BLOCK == 0
    mod_cols = w_mod.shape[2]

    lbg_row = _pad_cols(lb_gamma.reshape(1, -1), D_MODEL)
    pay1 = jnp.concatenate([c, jnp.zeros((4 - bs, D_MODEL), F32), c_ctx[None, :], lbg_row,
                            jnp.zeros((2, D_MODEL), F32)], axis=0)
    cond64 = _all_gather8(pay1, "gather_cond")
    lbg_full = cond64.reshape(N_DEV, 8, D_MODEL)[0::2, 5, :KW].reshape(N_CHIPS, 2, 2, HEAD_DIM)
    lbg_full = jnp.transpose(lbg_full, (1, 2, 0, 3)).reshape(2, 2, KW)

    b_mod_s = lax.dynamic_slice(b_mod, (0, kc * mod_cols), (1, mod_cols))
    mod_s = _mod_fwd(cond64, w_mod[0], b_mod_s)
    shards = [w_in[0], w_up[0], w_pa[0], w_pb[0], w_o[0], w_down[0]]
    bufs = _cast_bf16(pos, shards)
    mod_g = _all_gather8(mod_s, "gather_mod").reshape(N_DEV, 64, mod_cols)[0::2]
    mod_full = jnp.transpose(mod_g, (1, 0, 2)).reshape(64, N_CHIPS * mod_cols)
    mod_mine = lax.dynamic_slice(mod_full, (dev * 8, 0), (8, 6 * D_MODEL)).reshape(8, 6, D_MODEL)
    mod, mc = mod_mine[:bs], mod_mine[4]
    zeros4 = jnp.zeros((bs, 4, D_MODEL), F32)
    mod_a = jnp.concatenate([mod[:, 0:2], jnp.broadcast_to(mc[None, 0:2], (bs, 2, D_MODEL)), zeros4], axis=1)
    mod_c = jnp.concatenate([mod[:, 2:6], zeros4], axis=1)

    def cols_major(a):
        return jnp.transpose(a, (1, 0, 2)).reshape(a.shape[1], -1)

    gna, lng, lnb = g_norm_a, ln_v_g, ln_v_b
    ws3 = w_s[0]
    ws3_t = jnp.transpose(ws3, (0, 2, 1))
    bst = jnp.transpose(b_s[0])

    (p, h_all), (w_in_g,) = _in_fwd_own(pos, x, ctx, g_mix, mod_a, w_in[0], carry=_carry_gather_send(bufs[:1]),
                                        then=_carry_gather_forward(bufs[:1]))
    p, sent_up = _in_fwd_rest(pos, h_all, w_in_g, p, carry=_carry_gather_send(bufs[1:2]))
    w_in_f = cols_major(w_in_g)
    fwd_rest = _carry_gather_forward(bufs[2:])
    then_rest = _Carry([], [], {}, fwd_rest.sems, lambda i, o, s: fwd_rest.copies(i[1:], o[1:], s))
    (o_f, s_f, o_b, s_b), gathered = _hgrn_fwd(
        p, lbg_full, carry=_merge_carries(_carry_gather_forward(sent_up), _carry_gather_send(bufs[2:])),
        then=then_rest)
    w_up_f, w_pa_f, w_pb_f = (cols_major(a) for a in gathered[:3])
    w_o_f = gathered[3].reshape(-1, D_MODEL)
    w_down_f = gathered[4].reshape(-1, D_MODEL)
    x1 = _mix_fwd(p, o_f, o_b, x, mod_c, gna, lng, lnb, ws3, bst, w_pa_f, w_pb_f, w_o_f)
    dx1, h2, dab, hid, dffn, loss_part, dg_final, dg_ffn, dmod_ffn = _ffn(
        x1, loss_target, mod_c, g_ffn, g_final[None, :], w_up_f, w_down_f)
    rows_lat = bs * seq
    tk_lat = _row_tile(rows_lat)
    dw_up = _matmul_tn(h2.reshape(rows_lat, D_MODEL), dab.reshape(rows_lat, 2 * D_FF), N_CHIPS, tk_lat, "dw_up")
    dw_down = _matmul_tn(hid.reshape(rows_lat, D_FF), dffn.reshape(rows_lat, D_MODEL), 1, tk_lat, "dw_down")

    def shard_major(a):
        return jnp.transpose(a.reshape(a.shape[0], N_CHIPS, -1), (1, 0, 2))

    def add_halves(parts, recvs, names):
        sums = [_rs_add_halves(pos, g, r, "rs_add_" + nm) for g, r, nm in zip(parts, recvs, names)]
        return [s[0] for s in sums], [s[1] for s in sums]

    def sum_owner(cp32s, recvs, names):
        return [_rs_sum_owner(pos, a, r, "rs_sum_" + nm) for a, r, nm in zip(cp32s, recvs, names)]

    ffn_names, mix_names = ["w_up", "w_down"], ["w_pa", "w_pb", "w_o"]
    part_ffn = [dw_up, dw_down.reshape(N_CHIPS, -1, D_MODEL)]
    (do_raw, dpc, dw_pa, dw_pb, dw_o, dgna, dlng, dlnb, dws, dbst, dmod_mix), sib_ffn = _mix_bwd(
        p, o_f, o_b, dx1, mod_c, gna, lng, lnb, ws3, ws3_t, bst, w_pa_f, w_pb_f, w_o_f,
        carry=_carry_sibling_halves(part_ffn))
    cp32_ffn, cpbf_ffn = add_halves(part_ffn, sib_ffn, ffn_names)
    part_mix = [shard_major(dw_pa), shard_major(dw_pb), dw_o.reshape(N_CHIPS, -1, D_MODEL)]
    (df_f, dq_f, dv_f, dlb0, df_b, dq_b, dv_b, dlb1), got = _hgrn_bwd_pair(
        p, lbg_full, (s_f, s_b), do_raw,
        carry=_merge_carries(_carry_to_owner(cpbf_ffn), _carry_sibling_halves(part_mix)))
    own_ffn, sib_mix = got[:2], got[2:]
    half_ffn = sum_owner(cp32_ffn, own_ffn, ffn_names)
    cp32_mix, cpbf_mix = add_halves(part_mix, sib_mix, mix_names)
    (grad_x, dp, dg_mix, dmod_in), _ = _in_bwd(x, ctx, dx1, g_mix, mod_a, w_in_f, df_f, df_b, dv_f, dv_b, dq_f, dq_b,
                                               dpc)

    rows_all = dp.shape[0] * dp.shape[1]
    tk_all = _row_tile(rows_all)
    dw_in, got = _matmul_tn(h_all.reshape(rows_all, D_MODEL), dp.reshape(rows_all, IN_COLS), N_CHIPS, tk_all, "dw_in",
                            carry=_merge_carries(_carry_join_halves(half_ffn), _carry_to_owner(cpbf_mix)))
    g_ffn_w, own_mix = got[:2], got[2:]
    half_mix = sum_owner(cp32_mix, own_mix, mix_names)

    dmod_mine = jnp.concatenate([dmod_in[:, 0], dmod_in[:, 1], dmod_mix[:, 0], dmod_ffn[:, 1], dmod_ffn[:, 2],
                                 dmod_ffn[:, 3]], axis=1)
    dmc = jnp.concatenate([jnp.sum(dmod_in[:, 2], axis=0), jnp.sum(dmod_in[:, 3], axis=0),
                           jnp.zeros((4 * D_MODEL,), F32)])[None, :]
    pay3 = jnp.concatenate([dmod_mine, jnp.zeros((4 - bs, 6 * D_MODEL), F32), dmc,
                            jnp.zeros((3, 6 * D_MODEL), F32)], axis=0)
    dmod64, got = _all_gather8(pay3, "gather_dmod", carry=_merge_carries(_carry_sibling_halves([dw_in]),
                                                                          _carry_join_halves(half_mix)))
    sib_in, g_mix_w = got[:1], got[1:]
    cp32_in, cpbf_in = add_halves([dw_in], sib_in, ["w_in"])
    dmod64_my = lax.dynamic_slice(dmod64, (0, kc * mod_cols), (64, mod_cols))
    g_w_mod, g_b_mod, g_cctx_part = _mod_bwd(cond64, dmod64, dmod64_my, w_mod[0], c_ctx[None, :])

    def row(*parts):
        return _pad_cols(jnp.concatenate([q.reshape(1, -1) for q in parts], axis=1), D_MODEL)

    small_rows = [dg_mix, dg_ffn, dg_final, g_cctx_part, row(dgna), row(dlng, dlnb), row(jnp.transpose(dbst)),
                  row(dlb0, dlb1), row(loss_part), jnp.zeros((7, D_MODEL), F32), dws.reshape(64, D_MODEL)]
    pay4 = jnp.concatenate(small_rows, axis=0)
    small64, own_in = _all_gather8(pay4, "gather_small", carry=_carry_to_owner(cpbf_in))
    tot, dgam0 = _small_reduce(small64, lbg_full)

    rest_names = ["w_up", "w_pa", "w_pb", "w_o", "w_down", "w_mod"]
    rest_w = shards[1:] + [w_mod[0]]
    rest_g = [g_ffn_w[0], g_mix_w[0], g_mix_w[1], g_mix_w[2], g_ffn_w[1], g_w_mod]
    rest_m = [m_w_up[0], m_w_pa[0], m_w_pb[0], m_w_o[0], m_w_down[0], m_w_mod[0]]
    rest_v = [v_w_up[0], v_w_pa[0], v_w_pb[0], v_w_o[0], v_w_down[0], v_w_mod[0]]
    (ds_r, m2s_r, v2s_r), _ = _adamw_group(rest_w, rest_g, rest_m, rest_v, "adamw_rest")
    res = {}
    for name, g, d, m2, v2 in zip(rest_names, rest_g, ds_r, m2s_r, v2s_r):
        res[name] = (g[None], d[None], m2[None], v2[None])
    g_in_w = _comm_call("rs_join_w_in", _carry_join_halves(sum_owner(cp32_in, own_in, ["w_in"])))
    d, m2, v2 = _adamw_big(shards[0], g_in_w[0], m_w_in[0], v_w_in[0], "adamw_w_in")
    res["w_in"] = (g_in_w[0][None], d[None], m2[None], v2[None])

    loss = tot[8, 0]
    dgam_full = jnp.stack([dgam0, -dgam0])
    g_lbg = lax.dynamic_slice(dgam_full, (0, 0, kc * HEAD_DIM), (2, 2, HEAD_DIM))

    small = [
        ("c_ctx", c_ctx[None, :], tot[3:4], m_c_ctx, v_c_ctx),
        ("b_mod", b_mod, g_b_mod, m_b_mod, v_b_mod),
        ("g_mix", g_mix, tot[0:1], m_g_mix, v_g_mix),
        ("g_ffn", g_ffn, tot[1:2], m_g_ffn, v_g_ffn),
        ("lb_gamma", lb_gamma.reshape(4, HEAD_DIM), g_lbg.reshape(4, HEAD_DIM), m_lb_gamma, v_lb_gamma),
        ("g_norm_a", g_norm_a, tot[4:5, :HEAD_DIM], m_g_norm_a, v_g_norm_a),
        ("ln_v_g", ln_v_g, tot[5:6, :KW], m_ln_v_g, v_ln_v_g),
        ("ln_v_b", ln_v_b, tot[5:6, KW:], m_ln_v_b, v_ln_v_b),
        ("w_s", w_s.reshape(N_HEADS * SGU_CHUNK, SGU_CHUNK), tot[16:80].reshape(N_HEADS * SGU_CHUNK, SGU_CHUNK),
         m_w_s, v_w_s),
        ("b_s", b_s[0], tot[6:7, :KW].reshape(N_HEADS, SGU_CHUNK), m_b_s, v_b_s),
        ("g_final", g_final[None, :], tot[2:3], m_g_final, v_g_final),
    ]
    ws_, gs_ = [s[1] for s in small], [s[2] for s in small]
    ms_ = [s[3].reshape(s[1].shape) for s in small]
    vs_ = [s[4].reshape(s[1].shape) for s in small]
    ds_, m2s_, v2s_ = _adamw_small(ws_, gs_, ms_, vs_)
    for (name, _, g, m, _), d, m2, v2 in zip(small, ds_, m2s_, v2s_):
        res[name] = tuple(t.reshape(m.shape) for t in (g, d, m2, v2))

    order = ["c_ctx", "w_mod", "b_mod", "g_mix", "g_ffn", "w_in", "lb_gamma", "g_norm_a", "ln_v_g", "ln_v_b",
             "w_s", "b_s", "w_pa", "w_pb", "w_o", "w_up", "w_down", "g_final"]
    outs = [loss, grad_x]
    for part in range(4):
        outs += [res[n][part] for n in order]
    return tuple(outs)
```

```python
import functools
import math

import jax
import jax.numpy as jnp
import numpy as np
from jax import lax
from jax.experimental import pallas as pl
from jax.experimental.pallas import tpu as pltpu

F32 = jnp.float32
BF16 = jnp.bfloat16
SDS = jax.ShapeDtypeStruct
MESH = pl.DeviceIdType.MESH

EPS = 1e-6
D_MODEL = 1024
N_HEADS = 4
HEAD_DIM = 128
KW = N_HEADS * HEAD_DIM
IN_COLS = 11 * KW
D_FF = 2816
HGRN_CHUNK = 64
SGU_CHUNK = 128
ROW_BLOCK = 256
N_CHIPS = 4
N_DEV = 8
V7X_VMEM_BYTES = 64 * 1024 * 1024
VMEM_LIMIT = V7X_VMEM_BYTES - 6 * 1024 * 1024

ADAM_LR, ADAM_B1, ADAM_B2, ADAM_EPS, ADAM_WD, ADAM_STEP = 0.001, 0.9, 0.999, 1e-08, 0.01, 10
GELU_C0 = math.sqrt(2.0 / math.pi)
GELU_C1 = 0.044715

VMEM_SPEC = pl.BlockSpec(memory_space=pltpu.VMEM)
ANY_SPEC = pl.BlockSpec(memory_space=pl.ANY)


def _params(n_grid):
    return pltpu.CompilerParams(dimension_semantics=("arbitrary",) * n_grid, vmem_limit_bytes=VMEM_LIMIT)


def _sig(x):
    return 0.5 * jnp.tanh(0.5 * x) + 0.5


def _gelu(x):
    t = jnp.tanh(GELU_C0 * (x + GELU_C1 * x * x * x))
    return 0.5 * x * (1.0 + t), t


def _dgelu(x, t):
    return 0.5 * (1.0 + t) + 0.5 * x * (1.0 - t * t) * GELU_C0 * (1.0 + 3.0 * GELU_C1 * x * x)


def _dot(a, b):
    return jnp.dot(a.astype(BF16), b.astype(BF16), preferred_element_type=F32)


def _dot_nt(a, b):
    return lax.dot_general(a.astype(BF16), b.astype(BF16), (((1,), (1,)), ((), ())), preferred_element_type=F32)


def _dot_tn(a, b):
    return lax.dot_general(a.astype(BF16), b.astype(BF16), (((0,), (0,)), ((), ())), preferred_element_type=F32)


def _dot_f32(a, b, dims=(((1,), (0,)), ((), ()))):
    return lax.dot_general(a, b, dims, precision=lax.Precision.HIGHEST, preferred_element_type=F32)


def _rms(x):
    r = lax.rsqrt(jnp.mean(x * x, axis=-1, keepdims=True) + EPS)
    return x * r, r


def _rms_bwd(dxn, xn, r):
    return r * (dxn - xn * jnp.mean(dxn * xn, axis=-1, keepdims=True))


def _colsum(a):
    return jnp.sum(a, axis=0, keepdims=True)


def _tri(n, upper):
    t = lax.broadcasted_iota(jnp.int32, (n, n), 0)
    s = lax.broadcasted_iota(jnp.int32, (n, n), 1)
    return (s >= t) if upper else (s <= t)


def _all_gather8(x_shard, name, carry=None, then=None):
    m_per, n = x_shard.shape
    n_ci = len(carry.ins) if carry else 0
    n_co = len(carry.outs) if carry else 0
    n_cs = len(carry.sems) if carry else 0

    def body(*refs):
        x_ref, cins = refs[0], refs[1:1 + n_ci]
        out_ref, couts = refs[1 + n_ci], refs[2 + n_ci:2 + n_ci + n_co]
        send_sems, recv_sems, local_sem = refs[2 + n_ci + n_co:5 + n_ci + n_co]
        csems = refs[5 + n_ci + n_co:5 + n_ci + n_co + n_cs]
        tsems = refs[5 + n_ci + n_co + n_cs:]
        if carry:
            _start_all(carry.copies(cins, couts, csems)[0])
        _gather8_body(x_ref, out_ref, send_sems, recv_sems, local_sem, m_per)
        if carry:
            _wait_all(carry.copies(cins, couts, csems)[1])
        if then:
            _start_all(then.copies(couts, couts, tsems)[0])
            _wait_all(then.copies(couts, couts, tsems)[1])

    res = pl.pallas_call(
        body, name=name, out_shape=[SDS((N_DEV * m_per, n), x_shard.dtype)] + (carry.outs if carry else []),
        in_specs=[VMEM_SPEC] + [ANY_SPEC] * n_ci, out_specs=[VMEM_SPEC] + [ANY_SPEC] * n_co,
        input_output_aliases={1 + i: 1 + o for i, o in carry.alias.items()} if carry else {},
        scratch_shapes=[pltpu.SemaphoreType.DMA((7,)), pltpu.SemaphoreType.DMA((7,)), pltpu.SemaphoreType.DMA]
        + (carry.sems if carry else []) + (then.sems if then else []),
    )(x_shard, *(carry.ins if carry else []))
    return res[0] if carry is None else (res[0], list(res[1:]))


def _gather8_body(x_ref, out_ref, send_sems, recv_sems, local_sem, m_per):
    x, y, c = lax.axis_index("x"), lax.axis_index("y"), lax.axis_index("c")
    me, sibling = (x, y, c), (x, y, 1 - c)
    chips = [(1 - x, y), (x, 1 - y), (1 - x, 1 - y)]

    def rows(px, py, pc):
        return out_ref.at[pl.ds((4 * px + 2 * py + pc) * m_per, m_per), :]

    def copy(k, block, to, src=None):
        return pltpu.make_async_remote_copy(
            src_ref=rows(*block) if src is None else src, dst_ref=rows(*block),
            send_sem=send_sems.at[k], recv_sem=recv_sems.at[k], device_id=to, device_id_type=MESH)

    mine = pltpu.make_async_copy(x_ref, rows(*me), local_sem)
    mine.start()
    first = [copy(0, me, sibling, src=x_ref)]
    first += [copy(1 + j, me, (*chip, c), src=x_ref) for j, chip in enumerate(chips)]
    for cp in first:
        cp.start()
    passed = [copy(4 + j, (*chip, c), sibling) for j, chip in enumerate(chips)]
    for j, chip in enumerate(chips):
        copy(1 + j, (*chip, c), me).wait_recv()
        passed[j].start()
    copy(0, sibling, me).wait_recv()
    for j, chip in enumerate(chips):
        copy(4 + j, (*chip, 1 - c), me).wait_recv()
    for cp in first + passed:
        cp.wait_send()
    mine.wait()


def _mesh_pos():
    x, y, c = lax.axis_index("x"), lax.axis_index("y"), lax.axis_index("c")
    chips = [(1 - x, y), (x, 1 - y), (1 - x, 1 - y)]
    return x, y, c, 2 * x + y, (x, y, 1 - c), chips


def _half_rows(c, rh):
    return pl.ds(pl.multiple_of(c * rh, 16), rh)


class _Carry:
    def __init__(self, ins, outs, alias, sems, copies):
        self.ins, self.outs, self.alias, self.sems, self.copies = list(ins), list(outs), dict(alias), list(sems), copies


def _remote(src, dst, send, recv, to):
    return functools.partial(pltpu.make_async_remote_copy, src_ref=src, dst_ref=dst, send_sem=send, recv_sem=recv,
                             device_id=to, device_id_type=MESH)


def _carry_gather_send(bufs):
    n = len(bufs)

    def copies(ins, outs, sems):
        x, y, c, kc, sibling, chips = _mesh_pos()
        starts, waits = [], []
        for wi in range(n):
            rh = outs[wi].shape[1] // 2
            for jj, chip in enumerate(chips):
                mine = outs[wi].at[kc, _half_rows(c, rh), :]
                cp = _remote(mine, mine, sems[0].at[wi, jj], sems[1].at[wi, jj], (*chip, c))
                starts.append(cp)
                waits.append((cp, "send"))
                theirs = outs[wi].at[2 * chip[0] + chip[1], _half_rows(c, rh), :]
                waits.append((_remote(theirs, theirs, sems[0].at[wi, jj], sems[1].at[wi, jj], (*chip, c)), "recv"))
        return starts, waits

    return _Carry(bufs, [SDS(b.shape, b.dtype) for b in bufs], {i: i for i in range(n)},
                  [pltpu.SemaphoreType.DMA((n, 3)), pltpu.SemaphoreType.DMA((n, 3))], copies)


def _carry_gather_forward(bufs):
    n = len(bufs)

    def copies(ins, outs, sems):
        x, y, c, kc, sibling, chips = _mesh_pos()
        starts, waits = [], []
        for wi in range(n):
            rh = outs[wi].shape[1] // 2
            for jj, chip in enumerate(chips):
                got = outs[wi].at[2 * chip[0] + chip[1], _half_rows(c, rh), :]
                cp = _remote(got, got, sems[0].at[wi, jj], sems[1].at[wi, jj], sibling)
                starts.append(cp)
                waits.append((cp, "send"))
                other = outs[wi].at[2 * chip[0] + chip[1], _half_rows(1 - c, rh), :]
                waits.append((_remote(other, other, sems[0].at[wi, jj], sems[1].at[wi, jj], sibling), "recv"))
        return starts, waits

    return _Carry(bufs, [SDS(b.shape, b.dtype) for b in bufs], {i: i for i in range(n)},
                  [pltpu.SemaphoreType.DMA((n, 3)), pltpu.SemaphoreType.DMA((n, 3))], copies)


def _carry_sibling_halves(grads):
    n = len(grads)

    def copies(ins, outs, sems):
        x, y, c, kc, sibling, chips = _mesh_pos()
        cps = [_remote(ins[wi].at[:, _half_rows(1 - c, ins[wi].shape[1] // 2), :], outs[wi],
                       sems[0].at[wi], sems[1].at[wi], sibling) for wi in range(n)]
        return cps, [(cp, "both") for cp in cps]

    return _Carry(grads, [SDS((N_CHIPS, g.shape[1] // 2, g.shape[2]), F32) for g in grads], {},
                  [pltpu.SemaphoreType.DMA((n,)), pltpu.SemaphoreType.DMA((n,))], copies)


def _carry_to_owner(cpbfs):
    n = len(cpbfs)

    def copies(ins, outs, sems):
        x, y, c, kc, sibling, chips = _mesh_pos()
        starts, waits = [], []
        for wi in range(n):
            for jj, chip in enumerate(chips):
                cp = _remote(ins[wi].at[2 * chip[0] + chip[1]], outs[wi].at[kc],
                             sems[0].at[wi, jj], sems[1].at[wi, jj], (*chip, c))
                starts.append(cp)
                waits.append((cp, "send"))
                slot = outs[wi].at[2 * chip[0] + chip[1]]
                waits.append((_remote(slot, slot, sems[0].at[wi, jj], sems[1].at[wi, jj], (*chip, c)), "recv"))
        return starts, waits

    return _Carry(cpbfs, [SDS(g.shape, BF16) for g in cpbfs], {},
                  [pltpu.SemaphoreType.DMA((n, 3)), pltpu.SemaphoreType.DMA((n, 3))], copies)


def _carry_join_halves(bufs):
    n = len(bufs)

    def copies(ins, outs, sems):
        x, y, c, kc, sibling, chips = _mesh_pos()
        cps = []
        for wi in range(n):
            mine = outs[wi].at[_half_rows(c, outs[wi].shape[0] // 2), :]
            cps.append(_remote(mine, mine, sems[0].at[wi], sems[1].at[wi], sibling))
        return cps, [(cp, "both") for cp in cps]

    return _Carry(bufs, [SDS(b.shape, F32) for b in bufs], {i: i for i in range(n)},
                  [pltpu.SemaphoreType.DMA((n,)), pltpu.SemaphoreType.DMA((n,))], copies)


def _merge_carries(*carries):
    ins, outs, alias, sems, parts = [], [], {}, [], []
    for cy in carries:
        parts.append((len(ins), len(cy.ins), len(outs), len(cy.outs), len(sems), len(cy.sems), cy.copies))
        alias.update({len(ins) + i: len(outs) + o for i, o in cy.alias.items()})
        ins += cy.ins
        outs += cy.outs
        sems += cy.sems

    def copies(i, o, s):
        starts, waits = [], []
        for i0, ni, o0, no, s0, ns, fn in parts:
            st, wt = fn(i[i0:i0 + ni], o[o0:o0 + no], s[s0:s0 + ns])
            starts += st
            waits += wt
        return starts, waits

    return _Carry(ins, outs, alias, sems, copies)


def _start_all(starts):
    for cp in starts:
        cp().start()


def _wait_all(waits):
    for cp, which in waits:
        if which == "send":
            cp().wait_send()
        elif which == "recv":
            cp().wait_recv()
        else:
            cp().wait()


HBM_SPEC = pl.BlockSpec(memory_space=pltpu.HBM)
SEM_SPEC = pl.BlockSpec(memory_space=pltpu.SEMAPHORE)
SPLIT_COPY_EFFECT = pltpu.SideEffectType.DATAFLOW_SIDE_EFFECTING


def _owner_send_start(cpbf):
    land = lax.empty(cpbf.shape, cpbf.dtype)

    def body(src_ref, land_ref, s0, s1, s2, r0, r1, r2, src_thru, land_thru, token):
        x, y, c, kc, sibling, chips = _mesh_pos()
        for jj, (chip, s_sem, r_sem) in enumerate(zip(chips, (s0, s1, s2), (r0, r1, r2))):
            pltpu.make_async_remote_copy(
                src_ref=src_ref.at[2 * chip[0] + chip[1]], dst_ref=land_ref.at[kc], send_sem=s_sem, recv_sem=r_sem,
                device_id=(*chip, c), device_id_type=MESH).start()
        token[...] = jnp.zeros_like(token)

    buf = pltpu.HBM(cpbf.shape, cpbf.dtype)
    outs = pl.pallas_call(
        body, name="rs_owner_w_in_start",
        out_shape=(pltpu.SemaphoreType.DMA(()),) * 6 + (buf, buf, SDS((8, 128), F32)),
        in_specs=(HBM_SPEC, HBM_SPEC), out_specs=(SEM_SPEC,) * 6 + (HBM_SPEC, HBM_SPEC, VMEM_SPEC),
        input_output_aliases={0: 6, 1: 7},
        compiler_params=pltpu.CompilerParams(has_side_effects=SPLIT_COPY_EFFECT),
    )(pltpu.with_memory_space_constraint(cpbf, pltpu.HBM), pltpu.with_memory_space_constraint(land, pltpu.HBM))
    return outs[:6], outs[6], outs[7], outs[8]


def _owner_send_wait(sems, src_thru, land_thru, after):
    n_after = len(after)

    def body(*refs):
        src_ref, land_ref = refs[0], refs[1]
        sends, recvs = refs[2:5], refs[5:8]
        x, y, c, kc, sibling, chips = _mesh_pos()
        for jj, chip in enumerate(chips):
            slot = 2 * chip[0] + chip[1]
            cp = pltpu.make_async_remote_copy(
                src_ref=src_ref.at[slot], dst_ref=land_ref.at[slot], send_sem=sends[jj], recv_sem=recvs[jj],
                device_id=(*chip, c), device_id_type=MESH)
            cp.wait_send()
            cp.wait_recv()

    buf = pltpu.HBM(land_thru.shape, land_thru.dtype)
    return pl.pallas_call(
        body, name="rs_owner_w_in_wait", out_shape=(buf, buf),
        in_specs=(HBM_SPEC, HBM_SPEC) + (SEM_SPEC,) * 6 + (ANY_SPEC,) * n_after, out_specs=(HBM_SPEC, HBM_SPEC),
        input_output_aliases={0: 0, 1: 1},
        compiler_params=pltpu.CompilerParams(has_side_effects=SPLIT_COPY_EFFECT),
    )(src_thru, land_thru, *sems, *after)[1]


def _comm_call(name, carry):
    n_i, n_o = len(carry.ins), len(carry.outs)

    def body(*refs):
        ins, outs, sems = refs[:n_i], refs[n_i:n_i + n_o], refs[n_i + n_o:]
        _start_all(carry.copies(ins, outs, sems)[0])
        _wait_all(carry.copies(ins, outs, sems)[1])

    return pl.pallas_call(
        body, name=name, out_shape=carry.outs, in_specs=[ANY_SPEC] * n_i, out_specs=[ANY_SPEC] * n_o,
        input_output_aliases=carry.alias, scratch_shapes=carry.sems,
    )(*carry.ins)


def _host_call(body, *, name, grid, in_specs, out_specs, out_shape, args, scratch_shapes=(), carry=None, then=None,
               prefetch=None, aliases=None):
    n_in, n_out, n_scr = len(in_specs), len(out_specs), len(scratch_shapes)
    n_ci = len(carry.ins) if carry else 0
    n_co = len(carry.outs) if carry else 0
    n_cs = len(carry.sems) if carry else 0
    n_pf = 0 if prefetch is None else 1

    def wrapped(*refs):
        pf, refs = refs[:n_pf], refs[n_pf:]
        ins, cins = refs[:n_in], refs[n_in:n_in + n_ci]
        o0 = n_in + n_ci
        outs, couts = refs[o0:o0 + n_out], refs[o0 + n_out:o0 + n_out + n_co]
        s0 = o0 + n_out + n_co
        scr, sems, tsems = refs[s0:s0 + n_scr], refs[s0 + n_scr:s0 + n_scr + n_cs], refs[s0 + n_scr + n_cs:]
        idx = [pl.program_id(a) for a in range(len(grid))]
        first = functools.reduce(jnp.logical_and, [i == 0 for i in idx])
        last = functools.reduce(jnp.logical_and, [i == g - 1 for i, g in zip(idx, grid)])

        if carry:
            @pl.when(first)
            def _():
                _start_all(carry.copies(cins, couts, sems)[0])

        body(*pf, *ins, *outs, *scr)

        if carry:
            @pl.when(last)
            def _():
                _wait_all(carry.copies(cins, couts, sems)[1])
                if then:
                    _start_all(then.copies(couts, couts, tsems)[0])
                    _wait_all(then.copies(couts, couts, tsems)[1])

    all_in = list(in_specs) + [ANY_SPEC] * n_ci
    all_out = list(out_specs) + [ANY_SPEC] * n_co
    all_scr = list(scratch_shapes) + (carry.sems if carry else []) + (then.sems if then else [])
    alias = {n_pf + i: o for i, o in (aliases or {}).items()}
    if carry:
        alias.update({n_pf + n_in + i: n_out + o for i, o in carry.alias.items()})
    kwargs = dict(name=name, out_shape=list(out_shape) + (carry.outs if carry else []), input_output_aliases=alias,
                  compiler_params=_params(len(grid)))
    if prefetch is None:
        call = pl.pallas_call(wrapped, grid=grid, in_specs=all_in, out_specs=all_out, scratch_shapes=all_scr, **kwargs)
        res = call(*args, *(carry.ins if carry else []))
    else:
        call = pl.pallas_call(wrapped, grid_spec=pltpu.PrefetchScalarGridSpec(
            num_scalar_prefetch=1, grid=grid, in_specs=all_in, out_specs=all_out, scratch_shapes=all_scr), **kwargs)
        res = call(prefetch, *args, *(carry.ins if carry else []))
    return list(res[:n_out]), list(res[n_out:])


def _rs_add_halves(pos, grad, recv, name):
    _, rs, cs = grad.shape
    rh = rs // 2
    rb = rh // 2

    def body(pos_ref, g_ref, r_ref, o32_ref, obf_ref):
        s = g_ref[...] + r_ref[...]
        o32_ref[...] = s
        obf_ref[...] = s.astype(BF16)

    blk = (1, rb, cs)
    return pl.pallas_call(
        body, name=name,
        grid_spec=pltpu.PrefetchScalarGridSpec(
            num_scalar_prefetch=1, grid=(N_CHIPS, 2),
            in_specs=[pl.BlockSpec(blk, lambda k, i, p: (k, p[1] * 2 + i, 0)),
                      pl.BlockSpec(blk, lambda k, i, p: (k, i, 0))],
            out_specs=[pl.BlockSpec(blk, lambda k, i, p: (k, i, 0)), pl.BlockSpec(blk, lambda k, i, p: (k, i, 0))]),
        out_shape=[SDS((N_CHIPS, rh, cs), F32), SDS((N_CHIPS, rh, cs), BF16)],
        compiler_params=_params(2),
    )(pos, grad, recv)


def _rs_sum_owner(pos, cp32, recv3, name):
    _, rh, cs = cp32.shape
    rb = rh // 2

    def body(pos_ref, own_ref, r1_ref, r2_ref, r3_ref, o_ref):
        o_ref[...] = ((own_ref[0] + r1_ref[0].astype(F32)) + r2_ref[0].astype(F32)) + r3_ref[0].astype(F32)

    blk = (1, rb, cs)

    def slot(d):
        return pl.BlockSpec(blk, lambda i, p: ((p[0] + d) % N_CHIPS, i, 0))

    return pl.pallas_call(
        body, name=name,
        grid_spec=pltpu.PrefetchScalarGridSpec(
            num_scalar_prefetch=1, grid=(2,),
            in_specs=[slot(0), slot(1), slot(2), slot(3)],
            out_specs=pl.BlockSpec((rb, cs), lambda i, p: (p[1] * 2 + i, 0))),
        out_shape=SDS((2 * rh, cs), F32),
        compiler_params=_params(1),
    )(pos, cp32, recv3, recv3, recv3)


def _cast_bf16(pos, arrs):
    n = len(arrs)

    def body(pos_ref, *refs):
        for i in range(n):
            refs[n + i][0] = refs[i][...].astype(BF16)

    return pl.pallas_call(
        body, name="cast_bf16",
        grid_spec=pltpu.PrefetchScalarGridSpec(
            num_scalar_prefetch=1, grid=(2,),
            in_specs=[pl.BlockSpec((a.shape[0] // 2, a.shape[1]), lambda i, p: (i, 0)) for a in arrs],
            out_specs=[pl.BlockSpec((1, a.shape[0] // 2, a.shape[1]), lambda i, p: (p[0], i, 0)) for a in arrs]),
        out_shape=[SDS((N_CHIPS,) + a.shape, BF16) for a in arrs],
        compiler_params=_params(1),
    )(pos, *arrs)


def _adamw_vals(w, g, m, v):
    m2 = ADAM_B1 * m + (1.0 - ADAM_B1) * g
    v2 = ADAM_B2 * v + (1.0 - ADAM_B2) * (g * g)
    m_hat = m2 / (1.0 - ADAM_B1 ** ADAM_STEP)
    v_hat = v2 / (1.0 - ADAM_B2 ** ADAM_STEP)
    delta = -ADAM_LR * (m_hat / (jnp.sqrt(v_hat) + ADAM_EPS) + ADAM_WD * w)
    return delta, m2, v2


def _adamw_big(w, g, m, v, name):
    rows, cols = w.shape
    rb = rows // 4

    def body(w_ref, g_ref, m_ref, v_ref, d_ref, m2_ref, v2_ref):
        d, m2, v2 = _adamw_vals(w_ref[...], g_ref[...], m_ref[...], v_ref[...])
        d_ref[...] = d
        m2_ref[...] = m2
        v2_ref[...] = v2

    spec = pl.BlockSpec((rb, cols), lambda i: (i, 0))
    return pl.pallas_call(
        body, name=name, grid=(4,), in_specs=[spec] * 4, out_specs=[spec] * 3,
        out_shape=[SDS(w.shape, F32)] * 3, compiler_params=_params(1),
    )(w, g, m, v)


ADAMW_GROUP_STEPS = 8


def _adamw_group(ws, gs, ms, vs, name, carry=None):
    n = len(ws)

    def body(*refs):
        for i in range(n):
            d, m2, v2 = _adamw_vals(refs[i][...], refs[n + i][...], refs[2 * n + i][...], refs[3 * n + i][...])
            refs[4 * n + i][...] = d
            refs[5 * n + i][...] = m2
            refs[6 * n + i][...] = v2

    specs = [pl.BlockSpec((w.shape[0] // ADAMW_GROUP_STEPS, w.shape[1]), lambda i: (i, 0)) for w in ws]
    shapes = [SDS(w.shape, F32) for w in ws]
    outs, carried = _host_call(
        body, name=name, grid=(ADAMW_GROUP_STEPS,), in_specs=specs * 4, out_specs=specs * 3, out_shape=shapes * 3,
        args=(*ws, *gs, *ms, *vs), carry=carry)
    return (outs[:n], outs[n:2 * n], outs[2 * n:]), carried


def _adamw_small(ws, gs, ms, vs):
    n = len(ws)

    def body(*refs):
        for i in range(n):
            d, m2, v2 = _adamw_vals(refs[i][...], refs[n + i][...], refs[2 * n + i][...], refs[3 * n + i][...])
            refs[4 * n + i][...] = d
            refs[5 * n + i][...] = m2
            refs[6 * n + i][...] = v2

    shapes = [SDS(w.shape, F32) for w in ws]
    outs = pl.pallas_call(
        body, name="adamw_small", out_shape=shapes * 3,
        in_specs=[VMEM_SPEC] * (4 * n), out_specs=[VMEM_SPEC] * (3 * n),
    )(*ws, *gs, *ms, *vs)
    return outs[:n], outs[n:2 * n], outs[2 * n:]


def _mod_fwd(cond64, w_mod_s, b_mod_s):
    def body(c_ref, w_ref, b_ref, o_ref):
        cc = c_ref[...]
        o_ref[...] = _dot_f32(cc * _sig(cc), w_ref[...]) + b_ref[...]

    return pl.pallas_call(
        body, name="mod_fwd", out_shape=SDS((cond64.shape[0], w_mod_s.shape[1]), F32),
        in_specs=[VMEM_SPEC] * 3, out_specs=VMEM_SPEC,
        compiler_params=pltpu.CompilerParams(vmem_limit_bytes=VMEM_LIMIT),
    )(cond64, w_mod_s, b_mod_s)


def _mod_bwd(cond64, dmod64, dmod64_my, w_mod_s, c_ctx):
    def body(c_ref, g_ref, gm_ref, w_ref, cc_ref, gw_ref, gb_ref, gcc_ref):
        cc = c_ref[...]
        act = cc * _sig(cc)
        gm = gm_ref[...]
        gw_ref[...] = _dot_f32(act, gm, (((0,), (0,)), ((), ())))
        gb_ref[...] = _colsum(g_ref[...])
        dact = _dot_f32(gm, w_ref[...], (((1,), (1,)), ((), ())))
        tot = dact[4:5, :]
        for dev in range(1, N_DEV):
            tot = tot + dact[8 * dev + 4:8 * dev + 5, :]
        c0 = cc_ref[...]
        s0 = _sig(c0)
        gcc_ref[...] = tot * (s0 * (1.0 + c0 * (1.0 - s0)))

    return pl.pallas_call(
        body, name="mod_bwd",
        out_shape=[SDS(w_mod_s.shape, F32), SDS((1, dmod64.shape[1]), F32), SDS((1, D_MODEL), F32)],
        in_specs=[VMEM_SPEC] * 5, out_specs=[VMEM_SPEC] * 3,
        compiler_params=pltpu.CompilerParams(vmem_limit_bytes=VMEM_LIMIT),
    )(cond64, dmod64, dmod64_my, w_mod_s, c_ctx)


SHARD_COLS = IN_COLS // N_CHIPS


def _in_fwd_own(pos, x, ctx, g_mix, mod_a, w_own, carry=None, then=None):
    bs, seq, _ = x.shape
    nb = seq // ROW_BLOCK + 1

    def body(pos_ref, x_ref, ctx_ref, g_ref, mod_ref, w_ref, p_ref, h_ref, w_bf):
        b, j = pl.program_id(0), pl.program_id(1)

        @pl.when((b == 0) & (j == 0))
        def _():
            w_bf[...] = w_ref[...].astype(BF16)

        is_ctx = j == 0
        xin = jnp.where(is_ctx, ctx_ref[0], x_ref[0])
        shift = jnp.where(is_ctx, mod_ref[0, 2:3, :], mod_ref[0, 0:1, :])
        scale = jnp.where(is_ctx, mod_ref[0, 3:4, :], mod_ref[0, 1:2, :])
        xn, _ = _rms(xin)
        hb = ((xn * g_ref[...]) * (1.0 + scale) + shift).astype(BF16)
        h_ref[0] = hb
        p_ref[0] = jnp.dot(hb, w_bf[...], preferred_element_type=F32)

    return _host_call(
        body, name="in_fwd_own", grid=(bs, nb), prefetch=pos,
        in_specs=[pl.BlockSpec((1, ROW_BLOCK, D_MODEL), lambda b, j, p: (b, jnp.maximum(j - 1, 0), 0)),
                  pl.BlockSpec((1, ROW_BLOCK, D_MODEL), lambda b, j, p: (b, 0, 0)),
                  pl.BlockSpec((1, D_MODEL), lambda b, j, p: (0, 0)),
                  pl.BlockSpec((1, 8, D_MODEL), lambda b, j, p: (b, 0, 0)),
                  pl.BlockSpec((D_MODEL, SHARD_COLS), lambda b, j, p: (0, 0))],
        out_specs=[pl.BlockSpec((1, ROW_BLOCK, SHARD_COLS), lambda b, j, p: (b, j, p[0])),
                   pl.BlockSpec((1, ROW_BLOCK, D_MODEL), lambda b, j, p: (b, j, 0))],
        out_shape=[SDS((bs, nb * ROW_BLOCK, IN_COLS), F32), SDS((bs, nb * ROW_BLOCK, D_MODEL), BF16)],
        scratch_shapes=[pltpu.VMEM((D_MODEL, SHARD_COLS), BF16)],
        args=(x, ctx, g_mix, mod_a, w_own), carry=carry, then=then)


def _in_fwd_rest(pos, h_all, w_in_g, p, carry=None):
    bs, rows, _ = h_all.shape
    rows_all = bs * rows
    tile = next(m * ROW_BLOCK for m in (3, 1) if rows_all % (m * ROW_BLOCK) == 0)

    def body(pos_ref, h_ref, w_ref, p_in_ref, p_ref):
        p_ref[...] = jnp.dot(h_ref[...], w_ref[0], preferred_element_type=F32)

    shard = lambda n, p: (p[0] + 1 + n) % N_CHIPS
    (p2,), carried = _host_call(
        body, name="in_fwd_rest", grid=(N_CHIPS - 1, rows_all // tile), prefetch=pos,
        in_specs=[pl.BlockSpec((tile, D_MODEL), lambda n, t, p: (t, 0)),
                  pl.BlockSpec((1, D_MODEL, SHARD_COLS), lambda n, t, p: (shard(n, p), 0, 0)),
                  ANY_SPEC],
        out_specs=[pl.BlockSpec((tile, SHARD_COLS), lambda n, t, p: (t, shard(n, p)))],
        out_shape=[SDS((rows_all, IN_COLS), F32)], aliases={2: 0},
        args=(h_all.reshape(rows_all, D_MODEL), w_in_g, p.reshape(rows_all, IN_COLS)), carry=carry)
    return p2.reshape(bs, rows, IN_COLS), carried


def _in_bwd(x, ctx, dx1, g_mix, mod_a, w_in, df_f, df_b, dv_f, dv_b, dq_f, dq_b, dpc, carry=None):
    bs, seq, _ = x.shape
    nb = seq // ROW_BLOCK + 1

    def body(x_ref, ctx_ref, dx1_ref, g_ref, mod_ref, w_ref, dff_ref, dfb_ref, dvf_ref, dvb_ref, dqf_ref, dqb_ref,
             dpc_ref, gx_ref, dp_ref, dg_ref, dmod_ref):
        b, j = pl.program_id(0), pl.program_id(1)
        is_ctx = j == 0

        @pl.when((b == 0) & (j == 0))
        def _():
            dg_ref[...] = jnp.zeros_like(dg_ref)

        @pl.when(j == 0)
        def _():
            dmod_ref[...] = jnp.zeros_like(dmod_ref)

        di = (dvf_ref[0] + dvb_ref[0]).astype(BF16)
        dq = (dqf_ref[0] + dqb_ref[0]).astype(BF16)
        dp = jnp.concatenate([dff_ref[0], dfb_ref[0], di, dq, dpc_ref[0]], axis=1)
        dp_ref[0] = dp
        dh = lax.dot_general(dp, w_ref[...], (((1,), (1,)), ((), ())), preferred_element_type=F32)
        xin = jnp.where(is_ctx, ctx_ref[0], x_ref[0])
        scale = jnp.where(is_ctx, mod_ref[0, 3:4, :], mod_ref[0, 1:2, :])
        xn, r = _rms(xin)
        g = g_ref[...]
        hn = xn * g
        d_shift = _colsum(dh)
        d_scale = _colsum(dh * hn)
        dhn = dh * (1.0 + scale)
        dg_ref[...] += _colsum(dhn * xn)
        dx = _rms_bwd(dhn * g, xn, r)

        @pl.when(is_ctx)
        def _():
            dmod_ref[0, 2:3, :] += d_shift
            dmod_ref[0, 3:4, :] += d_scale

        @pl.when(jnp.logical_not(is_ctx))
        def _():
            dmod_ref[0, 0:1, :] += d_shift
            dmod_ref[0, 1:2, :] += d_scale
            gx_ref[0] = dx + dx1_ref[0]

    def rows(w):
        return pl.BlockSpec((1, ROW_BLOCK, w), lambda b, j: (b, j, 0))

    lat = pl.BlockSpec((1, ROW_BLOCK, D_MODEL), lambda b, j: (b, jnp.maximum(j - 1, 0), 0))
    return _host_call(
        body, name="in_bwd", grid=(bs, nb),
        in_specs=[lat, pl.BlockSpec((1, ROW_BLOCK, D_MODEL), lambda b, j: (b, 0, 0)), lat,
                  pl.BlockSpec((1, D_MODEL), lambda b, j: (0, 0)),
                  pl.BlockSpec((1, 8, D_MODEL), lambda b, j: (b, 0, 0)),
                  pl.BlockSpec((D_MODEL, IN_COLS), lambda b, j: (0, 0)),
                  rows(KW), rows(KW), rows(KW), rows(KW), rows(KW), rows(KW), rows(7 * KW)],
        out_specs=[lat, rows(IN_COLS), pl.BlockSpec((1, D_MODEL), lambda b, j: (0, 0)),
                   pl.BlockSpec((1, 8, D_MODEL), lambda b, j: (b, 0, 0))],
        out_shape=[SDS(x.shape, F32), SDS((bs, nb * ROW_BLOCK, IN_COLS), BF16), SDS((1, D_MODEL), F32),
                   SDS((bs, 8, D_MODEL), F32)],
        args=(x, ctx, dx1, g_mix, mod_a, w_in, df_f, df_b, dv_f, dv_b, dq_f, dq_b, dpc), carry=carry)


def _lower_bound(lbg_ref, direction):
    return _sig(lbg_ref[0, direction:direction + 1, :] - lbg_ref[1, direction:direction + 1, :])


N_CHUNKS = ROW_BLOCK // HGRN_CHUNK


def _block_tri(upper):
    t = np.arange(ROW_BLOCK)[:, None]
    s = np.arange(ROW_BLOCK)[None, :]
    same = (t // HGRN_CHUNK) == (s // HGRN_CHUNK)
    return jnp.asarray(same & ((s >= t) if upper else (s <= t)), dtype=BF16)


TRI_SPEC = pl.BlockSpec((ROW_BLOCK, ROW_BLOCK), lambda b, j: (0, 0))


def _tri_matmul_f32(tri, g):
    g0 = g.astype(BF16)
    r1 = g - g0.astype(F32)
    g1 = r1.astype(BF16)
    g2 = (r1 - g1.astype(F32)).astype(BF16)
    return (jnp.dot(tri, g2, preferred_element_type=F32) + jnp.dot(tri, g1, preferred_element_type=F32)) \
        + jnp.dot(tri, g0, preferred_element_type=F32)


def _chunk_rows(rows):
    return jnp.concatenate([jnp.broadcast_to(r, (HGRN_CHUNK, r.shape[1])) for r in rows], axis=0)


def _block_gates(fl, q, lb, tri, upper):
    t = {}
    t["sg"] = _sig(fl)
    t["f"] = lb + (1.0 - lb) * t["sg"]
    k = 1.0 - t["f"]
    bcum = _tri_matmul_f32(tri, jnp.log(t["f"]))
    ends = [bcum[ci * HGRN_CHUNK:ci * HGRN_CHUNK + 1] if upper else bcum[(ci + 1) * HGRN_CHUNK - 1:(ci + 1) * HGRN_CHUNK]
            for ci in range(fl.shape[0] // HGRN_CHUNK)]
    mid = _chunk_rows([0.5 * r for r in ends])
    t["dec"] = [jnp.exp(r) for r in ends]
    t["e1"] = jnp.exp(bcum - mid)
    t["e2"] = jnp.exp(mid - bcum)
    t["eh"] = _chunk_rows([jnp.exp(0.5 * r) for r in ends])
    t["qi"] = q * t["e1"]
    t["ki"] = k * t["e2"]
    t["kd"] = t["ki"] * t["eh"]
    t["qe"] = t["qi"] * t["eh"]
    return t


def _hgrn_block_order(direction, nb):
    if direction == 0:
        return lambda j: j
    return lambda j: jnp.where(j == 0, 0, nb - j)


def _hgrn_fwd(p, lbg, carry=None, then=None):
    bs, rows, _ = p.shape
    nb = rows // ROW_BLOCK
    ncb = ROW_BLOCK // HGRN_CHUNK
    orders = [_hgrn_block_order(d, nb) for d in (0, 1)]
    dirs = (0, 1)

    def body(f0_ref, i0_ref, q0_ref, f1_ref, i1_ref, q1_ref, lbg_ref, tri0_ref, tri1_ref,
             o0_ref, s0_ref, o1_ref, s1_ref, st):
        @pl.when(pl.program_id(1) == 0)
        def _():
            st[...] = jnp.zeros_like(st)

        f_refs, i_refs, q_refs = (f0_ref, f1_ref), (i0_ref, i1_ref), (q0_ref, q1_ref)
        tri_refs, o_refs, s_refs = (tri0_ref, tri1_ref), (o0_ref, o1_ref), (s0_ref, s1_ref)
        chunk = lambda a, ci, h: a[ci * HGRN_CHUNK:(ci + 1) * HGRN_CHUNK, h * HEAD_DIM:(h + 1) * HEAD_DIM]
        masks = [_tri(HGRN_CHUNK, d == 1) for d in dirs]
        t = [_block_gates(f_refs[d][0], q_refs[d][0], _lower_bound(lbg_ref, d), tri_refs[d][...], d == 1) for d in dirs]
        v = [i_refs[d][0] for d in dirs]
        intra = [[[None] * N_HEADS for _ in range(ncb)] for _ in dirs]
        ds_loc = [[[None] * N_HEADS for _ in range(ncb)] for _ in dirs]
        for ci in range(ncb):
            for h in range(N_HEADS):
                for d in dirs:
                    a = jnp.where(masks[d], _dot_nt(chunk(t[d]["qi"], ci, h), chunk(t[d]["ki"], ci, h)), 0.0)
                    intra[d][ci][h] = _dot(a, chunk(v[d], ci, h))
                    ds_loc[d][ci][h] = _dot_tn(chunk(v[d], ci, h), chunk(t[d]["kd"], ci, h))
        for h in range(N_HEADS):
            ls = slice(h * HEAD_DIM, (h + 1) * HEAD_DIM)
            s = [st[d, h] for d in dirs]
            for step in range(ncb):
                for d in dirs:
                    ci = ncb - 1 - step if d == 1 else step
                    s_refs[d][0, 0, ci, h] = s[d]
                    o_refs[d][0, ci * HGRN_CHUNK:(ci + 1) * HGRN_CHUNK, ls] = (
                        intra[d][ci][h] + _dot_nt(chunk(t[d]["qe"], ci, h), s[d]))
                    s[d] = s[d] * t[d]["dec"][ci][:, ls] + ds_loc[d][ci][h]
            for d in dirs:
                st[d, h] = s[d]

    def col(d, cb):
        return pl.BlockSpec((1, ROW_BLOCK, KW), lambda b, j: (b, orders[d](j), cb))

    def outs(d):
        return [pl.BlockSpec((1, ROW_BLOCK, KW), lambda b, j: (b, orders[d](j), 0)),
                pl.BlockSpec((1, 1, ncb, N_HEADS, HEAD_DIM, HEAD_DIM), lambda b, j: (b, orders[d](j), 0, 0, 0, 0))]

    shapes = [SDS((bs, rows, KW), F32), SDS((bs, nb, ncb, N_HEADS, HEAD_DIM, HEAD_DIM), F32)]
    return _host_call(
        body, name="hgrn_fwd", grid=(bs, nb),
        in_specs=[col(0, 0), col(0, 2), col(0, 3), col(1, 1), col(1, 2), col(1, 3),
                  pl.BlockSpec((2, 2, KW), lambda b, j: (0, 0, 0)), TRI_SPEC, TRI_SPEC],
        out_specs=outs(0) + outs(1), out_shape=shapes * 2,
        scratch_shapes=[pltpu.VMEM((2, N_HEADS, HEAD_DIM, HEAD_DIM), F32)],
        args=(p, p, p, p, p, p, lbg, _block_tri(False), _block_tri(True)), carry=carry, then=then)


def _hgrn_bwd_pair(p, lbg, s_saved, do_raw, carry=None):
    bs, rows, _ = p.shape
    nb = rows // ROW_BLOCK
    ncb = ROW_BLOCK // HGRN_CHUNK
    dirs = (0, 1)
    fwd_orders = [_hgrn_block_order(d, nb) for d in dirs]
    orders = [lambda j, d=d: fwd_orders[d](nb - 1 - j) for d in dirs]
    pairs = [(ci, h) for ci in range(ncb) for h in range(N_HEADS)]

    def body(f0_ref, i0_ref, q0_ref, s0_ref, do0_ref, f1_ref, i1_ref, q1_ref, s1_ref, do1_ref,
             lbg_ref, tril_ref, triu_ref,
             df0_ref, dq0_ref, dv0_ref, dlb0_ref, df1_ref, dq1_ref, dv1_ref, dlb1_ref, dst, acc):
        b, j = pl.program_id(0), pl.program_id(1)
        f_refs, i_refs, q_refs = (f0_ref, f1_ref), (i0_ref, i1_ref), (q0_ref, q1_ref)
        s_refs, do_refs = (s0_ref, s1_ref), (do0_ref, do1_ref)
        df_refs, dq_refs, dv_refs, dlb_refs = (df0_ref, df1_ref), (dq0_ref, dq1_ref), (dv0_ref, dv1_ref), (dlb0_ref, dlb1_ref)
        tri_refs, trit_refs = (tril_ref, triu_ref), (triu_ref, tril_ref)

        @pl.when((b == 0) & (j == 0))
        def _():
            dlb0_ref[...] = jnp.zeros_like(dlb0_ref)
            dlb1_ref[...] = jnp.zeros_like(dlb1_ref)

        @pl.when(j == 0)
        def _():
            dst[...] = jnp.zeros_like(dst)

        chunk = lambda a, ci, h: a[ci * HGRN_CHUNK:(ci + 1) * HGRN_CHUNK, h * HEAD_DIM:(h + 1) * HEAD_DIM]
        rows_of = lambda ci: slice(ci * HGRN_CHUNK, (ci + 1) * HGRN_CHUNK)
        lanes_of = lambda h: slice(h * HEAD_DIM, (h + 1) * HEAD_DIM)
        grid3 = lambda: [[[None] * N_HEADS for _ in range(ncb)] for _ in dirs]
        lbs = [_lower_bound(lbg_ref, d) for d in dirs]
        masks = [_tri(HGRN_CHUNK, d == 1) for d in dirs]
        masks_t = [_tri(HGRN_CHUNK, d != 1) for d in dirs]
        t = [_block_gates(f_refs[d][0], q_refs[d][0], lbs[d], tri_refs[d][...], d == 1) for d in dirs]
        v = [i_refs[d][0] for d in dirs]
        do = [do_refs[d][0] for d in dirs]
        a_t, da, da_t, dv_in, ds_loc = (grid3() for _ in range(5))
        for ci, h in pairs:
            for d in dirs:
                a_t[d][ci][h] = _dot_nt(chunk(t[d]["ki"], ci, h), chunk(t[d]["qi"], ci, h))
        for ci, h in pairs:
            for d in dirs:
                da[d][ci][h] = _dot_nt(chunk(do[d], ci, h), chunk(v[d], ci, h))
        for ci, h in pairs:
            for d in dirs:
                da_t[d][ci][h] = _dot_nt(chunk(v[d], ci, h), chunk(do[d], ci, h))
        for ci, h in pairs:
            for d in dirs:
                acc[d, 3, rows_of(ci), lanes_of(h)] = _dot(chunk(do[d], ci, h), s_refs[d][0, 0, ci, h])
        for ci, h in pairs:
            for d in dirs:
                ds_loc[d][ci][h] = _dot_tn(chunk(do[d], ci, h), chunk(t[d]["qe"], ci, h))
        for ci, h in pairs:
            for d in dirs:
                acc[d, 0, rows_of(ci), lanes_of(h)] = _dot(jnp.where(masks[d], da[d][ci][h], 0.0),
                                                           chunk(t[d]["ki"], ci, h))
        for ci, h in pairs:
            for d in dirs:
                acc[d, 1, rows_of(ci), lanes_of(h)] = _dot(jnp.where(masks_t[d], da_t[d][ci][h], 0.0),
                                                           chunk(t[d]["qi"], ci, h))
        for ci, h in pairs:
            for d in dirs:
                dv_in[d][ci][h] = _dot(jnp.where(masks_t[d], a_t[d][ci][h], 0.0), chunk(do[d], ci, h))
        ddec = grid3()
        for h in range(N_HEADS):
            ls = lanes_of(h)
            ds = [dst[d, h] for d in dirs]
            for step in range(ncb):
                for d in dirs:
                    ci = step if d == 1 else ncb - 1 - step
                    acc[d, 2, rows_of(ci), ls] = _dot(chunk(v[d], ci, h), ds[d])
                    acc[d, 4, rows_of(ci), ls] = dv_in[d][ci][h] + _dot_nt(chunk(t[d]["kd"], ci, h), ds[d])
                    ddec[d][ci][h] = _colsum(ds[d] * s_refs[d][0, 0, ci, h])
                    ds[d] = ds[d] * t[d]["dec"][ci][:, ls] + ds_loc[d][ci][h]
            for d in dirs:
                dst[d, h] = ds[d]
        for d in dirs:
            td = t[d]
            dqi, dki, dkd, dqe = (acc[d, i] for i in range(4))
            dq_refs[d][0] = td["e1"] * (dqi + dqe * td["eh"])
            dv_refs[d][0] = acc[d, 4]
            dk = td["e2"] * (dki + dkd * td["eh"])
            dkd_kd = dkd * td["kd"]
            db = dqi * td["qi"] - dki * td["ki"] - dkd_kd + dqe * td["qe"]
            dbl = [_colsum(dkd_kd[rows_of(ci)]) + jnp.concatenate(ddec[d][ci], axis=1) * td["dec"][ci]
                   for ci in range(ncb)]
            dg = _tri_matmul_f32(trit_refs[d][...], db) + _chunk_rows(dbl)
            df = dg / td["f"] - dk
            sg = td["sg"]
            dlb_refs[d][...] += _colsum(df * (1.0 - sg))
            df_refs[d][0] = (df * (1.0 - lbs[d]) * sg * (1.0 - sg)).astype(BF16)

    def ins(d):
        col = lambda cb: pl.BlockSpec((1, ROW_BLOCK, KW), lambda b, j: (b, orders[d](j), cb))
        return [col(d), col(2), col(3),
                pl.BlockSpec((1, 1, ncb, N_HEADS, HEAD_DIM, HEAD_DIM), lambda b, j: (b, orders[d](j), 0, 0, 0, 0)),
                pl.BlockSpec((1, ROW_BLOCK, KW), lambda b, j: (b, orders[d](j), 0))]

    def outs(d):
        row = pl.BlockSpec((1, ROW_BLOCK, KW), lambda b, j: (b, orders[d](j), 0))
        return [row, row, row, pl.BlockSpec((1, KW), lambda b, j: (0, 0))]

    shapes = [SDS((bs, rows, KW), BF16), SDS((bs, rows, KW), F32), SDS((bs, rows, KW), F32), SDS((1, KW), F32)]
    return _host_call(
        body, name="hgrn_bwd", grid=(bs, nb),
        in_specs=ins(0) + ins(1) + [pl.BlockSpec((2, 2, KW), lambda b, j: (0, 0, 0)), TRI_SPEC, TRI_SPEC],
        out_specs=outs(0) + outs(1), out_shape=shapes * 2,
        scratch_shapes=[pltpu.VMEM((2, N_HEADS, HEAD_DIM, HEAD_DIM), F32), pltpu.VMEM((2, 5, ROW_BLOCK, KW), F32)],
        args=(p, p, p, s_saved[0], do_raw, p, p, p, s_saved[1], do_raw, lbg, _block_tri(False), _block_tri(True)),
        carry=carry)


def _mix_values(og, u, v, ga, gb, o_raw, gna, lng, lnb, ws_ref, bst, wpa, wpb, wo):
    t = {}
    sog = _sig(og)
    t["sog"], t["silu_og"] = sog, og * sog
    xh_l, r_l = [], []
    for h in range(N_HEADS):
        xh, r = _rms(o_raw[:, h * HEAD_DIM:(h + 1) * HEAD_DIM])
        xh_l.append(xh)
        r_l.append(r)
    t["xh"], t["r"] = jnp.concatenate(xh_l, axis=1), r_l
    gna4 = jnp.concatenate([gna] * N_HEADS, axis=1)
    t["gna4"] = gna4
    t["o_n"] = t["xh"] * gna4
    t["o_a"] = t["o_n"] * t["silu_og"]
    t["ya"] = _dot(t["o_a"], wpa)
    t["gu"], t["tu"] = _gelu(u)
    gv, t["tv"] = _gelu(v)
    mu = jnp.mean(gv, axis=-1, keepdims=True)
    cen = gv - mu
    t["rstd"] = lax.rsqrt(jnp.mean(cen * cen, axis=-1, keepdims=True) + EPS)
    t["xhat"] = cen * t["rstd"]
    vn = t["xhat"] * lng + lnb
    t["vn"] = vn
    chunks = []
    for n in range(ROW_BLOCK // SGU_CHUNK):
        rs = slice(n * SGU_CHUNK, (n + 1) * SGU_CHUNK)
        groups = []
        for g in range(N_HEADS):
            ls = slice(g * HEAD_DIM, (g + 1) * HEAD_DIM)
            groups.append(_dot(ws_ref[g], vn[rs, ls]) + bst[:, g:g + 1])
        chunks.append(jnp.concatenate(groups, axis=1))
    t["mixed"] = jnp.concatenate(chunks, axis=0)
    t["o_bm"] = t["gu"] * t["mixed"]
    t["yb"] = _dot(t["o_bm"], wpb)
    t["sa"], t["sb"] = _sig(ga), _sig(gb)
    t["merged"] = t["sa"] * t["ya"] + t["sb"] * t["yb"]
    t["mix"] = _dot(t["merged"], wo)
    return t


def _mix_in_specs(row_of):
    def col(cb):
        return pl.BlockSpec((1, ROW_BLOCK, KW), lambda b, j: (b, row_of(j), cb))
    return [col(cb) for cb in range(4, 11)]


def _mix_param_specs():
    full2 = lambda r, c: pl.BlockSpec((r, c), lambda b, j: (0, 0))
    return [full2(1, HEAD_DIM), full2(1, KW), full2(1, KW),
            pl.BlockSpec((N_HEADS, SGU_CHUNK, SGU_CHUNK), lambda b, j: (0, 0, 0)),
            full2(SGU_CHUNK, N_HEADS), full2(KW, D_MODEL), full2(KW, D_MODEL), full2(D_MODEL, D_MODEL)]


def _mix_fwd(p, o_f, o_b, x, mod_c, gna, lng, lnb, w_s, bst, wpa, wpb, wo):
    bs, seq, _ = x.shape
    nbl = seq // ROW_BLOCK

    def body(og_r, u_r, v_r, ga0_r, ga1_r, gb0_r, gb1_r, of_r, ob_r, x_r, mod_r,
             gna_r, lng_r, lnb_r, ws_r, bst_r, wpa_r, wpb_r, wo_r, x1_r):
        ga = jnp.concatenate([ga0_r[0], ga1_r[0]], axis=1)
        gb = jnp.concatenate([gb0_r[0], gb1_r[0]], axis=1)
        t = _mix_values(og_r[0], u_r[0], v_r[0], ga, gb, of_r[0] + ob_r[0], gna_r[...], lng_r[...], lnb_r[...],
                        ws_r, bst_r[...], wpa_r[...], wpb_r[...], wo_r[...])
        x1_r[0] = x_r[0] + mod_r[0, 0:1, :] * t["mix"]

    row = lambda w: pl.BlockSpec((1, ROW_BLOCK, w), lambda b, j: (b, j + 1, 0))
    lat = pl.BlockSpec((1, ROW_BLOCK, D_MODEL), lambda b, j: (b, j, 0))
    return pl.pallas_call(
        body, name="mix_fwd", grid=(bs, nbl),
        in_specs=_mix_in_specs(lambda j: j + 1) + [row(KW), row(KW), lat,
                                                    pl.BlockSpec((1, 8, D_MODEL), lambda b, j: (b, 0, 0))]
        + _mix_param_specs(),
        out_specs=lat, out_shape=SDS(x.shape, F32), compiler_params=_params(2),
    )(p, p, p, p, p, p, p, o_f, o_b, x, mod_c, gna, lng, lnb, w_s, bst, wpa, wpb, wo)


def _mix_bwd(p, o_f, o_b, dx1, mod_c, gna, lng, lnb, w_s, w_s_t, bst, wpa, wpb, wo, carry=None):
    bs, rows, _ = p.shape
    nb = rows // ROW_BLOCK

    def body(og_r, u_r, v_r, ga0_r, ga1_r, gb0_r, gb1_r, of_r, ob_r, dx1_r, mod_r,
             gna_r, lng_r, lnb_r, ws_r, bst_r, wpa_r, wpb_r, wo_r, wst_r,
             dor_r, dpc_r, dwpa_r, dwpb_r, dwo_r, dgna_r, dlng_r, dlnb_r, dws_r, dbst_r, dmod_r):
        b, j = pl.program_id(0), pl.program_id(1)

        @pl.when((b == 0) & (j == 0))
        def _():
            for r in (dwpa_r, dwpb_r, dwo_r, dgna_r, dlng_r, dlnb_r, dws_r, dbst_r):
                r[...] = jnp.zeros_like(r)

        @pl.when(j == 0)
        def _():
            dmod_r[...] = jnp.zeros_like(dmod_r)
            dor_r[...] = jnp.zeros_like(dor_r)
            dpc_r[...] = jnp.zeros_like(dpc_r)

        @pl.when(j > 0)
        def _():
            og, u, v = og_r[0], u_r[0], v_r[0]
            ga = jnp.concatenate([ga0_r[0], ga1_r[0]], axis=1)
            gb = jnp.concatenate([gb0_r[0], gb1_r[0]], axis=1)
            gna, lng = gna_r[...], lng_r[...]
            wpa, wpb, wo = wpa_r[...], wpb_r[...], wo_r[...]
            dx1 = dx1_r[0]
            dmix = mod_r[0, 0:1, :] * dx1
            dmerged = _dot_nt(dmix, wo)
            t = _mix_values(og, u, v, ga, gb, of_r[0] + ob_r[0], gna, lng, lnb_r[...],
                            ws_r, bst_r[...], wpa, wpb, wo)
            dmod_r[0, 0:1, :] += _colsum(dx1 * t["mix"])
            dwo_r[...] += _dot_tn(t["merged"], dmix)
            sa, sb = t["sa"], t["sb"]
            dya, dyb = sa * dmerged, sb * dmerged
            dga = dmerged * t["ya"] * sa * (1.0 - sa)
            dgb = dmerged * t["yb"] * sb * (1.0 - sb)
            do_a = _dot_nt(dya, wpa)
            dwpa_r[...] += _dot_tn(t["o_a"], dya)
            do_bm = _dot_nt(dyb, wpb)
            dwpb_r[...] += _dot_tn(t["o_bm"], dyb)
            sog = t["sog"]
            dog = do_a * t["o_n"] * (sog * (1.0 + og * (1.0 - sog)))
            do_n = do_a * t["silu_og"]
            dxh = do_n * t["gna4"]
            prod = do_n * t["xh"]
            dgna = jnp.zeros((1, HEAD_DIM), F32)
            dor_l = []
            for h in range(N_HEADS):
                ls = slice(h * HEAD_DIM, (h + 1) * HEAD_DIM)
                dgna = dgna + _colsum(prod[:, ls])
                dor_l.append(_rms_bwd(dxh[:, ls], t["xh"][:, ls], t["r"][h]))
            dgna_r[...] += dgna
            dor_r[0] = jnp.concatenate(dor_l, axis=1)
            du = do_bm * t["mixed"] * _dgelu(u, t["tu"])
            dmixed = do_bm * t["gu"]
            vn = t["vn"]
            dvn_chunks = []
            for n in range(ROW_BLOCK // SGU_CHUNK):
                rs = slice(n * SGU_CHUNK, (n + 1) * SGU_CHUNK)
                groups = []
                for g in range(N_HEADS):
                    ls = slice(g * HEAD_DIM, (g + 1) * HEAD_DIM)
                    dm = dmixed[rs, ls]
                    dws_r[g] += _dot_nt(dm, vn[rs, ls])
                    dbst_r[:, g:g + 1] += jnp.sum(dm, axis=1, keepdims=True)
                    groups.append(_dot(wst_r[g], dm))
                dvn_chunks.append(jnp.concatenate(groups, axis=1))
            dvn = jnp.concatenate(dvn_chunks, axis=0)
            xhat = t["xhat"]
            dlng_r[...] += _colsum(dvn * xhat)
            dlnb_r[...] += _colsum(dvn)
            dxhat = dvn * lng
            dgv = t["rstd"] * (dxhat - jnp.mean(dxhat, axis=-1, keepdims=True)
                               - xhat * jnp.mean(dxhat * xhat, axis=-1, keepdims=True))
            dv = dgv * _dgelu(v, t["tv"])
            dpc_r[0] = jnp.concatenate([dog, du, dv, dga, dgb], axis=1).astype(BF16)

    row = lambda w: pl.BlockSpec((1, ROW_BLOCK, w), lambda b, j: (b, j, 0))
    lat = pl.BlockSpec((1, ROW_BLOCK, D_MODEL), lambda b, j: (b, jnp.maximum(j - 1, 0), 0))
    full2 = lambda r, c: pl.BlockSpec((r, c), lambda b, j: (0, 0))
    ws_spec = pl.BlockSpec((N_HEADS, SGU_CHUNK, SGU_CHUNK), lambda b, j: (0, 0, 0))
    return _host_call(
        body, name="mix_bwd", grid=(bs, nb),
        in_specs=_mix_in_specs(lambda j: j) + [row(KW), row(KW), lat,
                                                pl.BlockSpec((1, 8, D_MODEL), lambda b, j: (b, 0, 0))]
        + _mix_param_specs() + [ws_spec],
        out_specs=[row(KW), row(7 * KW), full2(KW, D_MODEL), full2(KW, D_MODEL), full2(D_MODEL, D_MODEL),
                   full2(1, HEAD_DIM), full2(1, KW), full2(1, KW), ws_spec, full2(SGU_CHUNK, N_HEADS),
                   pl.BlockSpec((1, 8, D_MODEL), lambda b, j: (b, 0, 0))],
        out_shape=[SDS((bs, rows, KW), F32), SDS((bs, rows, 7 * KW), BF16), SDS((KW, D_MODEL), F32),
                   SDS((KW, D_MODEL), F32), SDS((D_MODEL, D_MODEL), F32), SDS((1, HEAD_DIM), F32),
                   SDS((1, KW), F32), SDS((1, KW), F32), SDS((N_HEADS, SGU_CHUNK, SGU_CHUNK), F32),
                   SDS((SGU_CHUNK, N_HEADS), F32), SDS((bs, 8, D_MODEL), F32)],
        args=(p, p, p, p, p, p, p, o_f, o_b, dx1, mod_c, gna, lng, lnb, w_s, bst, wpa, wpb, wo, w_s_t), carry=carry)


def _ffn(x1, target, mod_c, g_ffn, g_final, w_up, w_down):
    bs, seq, _ = x1.shape
    nbl = seq // ROW_BLOCK

    def body(x1_r, tg_r, mod_r, gf_r, gl_r, wu_r, wd_r,
             dx1_r, h2_r, dab_r, hid_r, dffn_r, loss_r, dgl_r, dgf_r, dmod_r):
        b, j = pl.program_id(0), pl.program_id(1)

        @pl.when((b == 0) & (j == 0))
        def _():
            for r in (loss_r, dgl_r, dgf_r):
                r[...] = jnp.zeros_like(r)

        @pl.when(j == 0)
        def _():
            dmod_r[...] = jnp.zeros_like(dmod_r)

        x1 = x1_r[0]
        shift, scale, gate = mod_r[0, 1:2, :], mod_r[0, 2:3, :], mod_r[0, 3:4, :]
        gf, gl = gf_r[...], gl_r[...]
        xn2, r2 = _rms(x1)
        hn2 = xn2 * gf
        h2 = (hn2 * (1.0 + scale) + shift).astype(BF16)
        h2_r[0] = h2
        ab = jnp.dot(h2, wu_r[...], preferred_element_type=F32)
        a, bb = ab[:, :D_FF], ab[:, D_FF:]
        sa = _sig(a)
        silu_a = a * sa
        hid = (silu_a * bb).astype(BF16)
        hid_r[0] = hid
        ffn = jnp.dot(hid, wd_r[...], preferred_element_type=F32)
        x2 = x1 + gate * ffn
        xn3, r3 = _rms(x2)
        err = xn3 * gl - tg_r[0]
        loss_r[...] += 0.5 * jnp.sum(jnp.mean(err * err, axis=-1, keepdims=True), axis=0, keepdims=True)
        dy = err * (1.0 / D_MODEL)
        dgl_r[...] += _colsum(dy * xn3)
        dx2 = _rms_bwd(dy * gl, xn3, r3)
        dmod_r[0, 3:4, :] += _colsum(dx2 * ffn)
        dffn = (gate * dx2).astype(BF16)
        dffn_r[0] = dffn
        dhid = lax.dot_general(dffn, wd_r[...], (((1,), (1,)), ((), ())), preferred_element_type=F32)
        da = dhid * bb * (sa * (1.0 + a * (1.0 - sa)))
        db = dhid * silu_a
        dab = jnp.concatenate([da, db], axis=1).astype(BF16)
        dab_r[0] = dab
        dh2 = lax.dot_general(dab, wu_r[...], (((1,), (1,)), ((), ())), preferred_element_type=F32)
        dmod_r[0, 1:2, :] += _colsum(dh2)
        dmod_r[0, 2:3, :] += _colsum(dh2 * hn2)
        dhn2 = dh2 * (1.0 + scale)
        dgf_r[...] += _colsum(dhn2 * xn2)
        dx1_r[0] = dx2 + _rms_bwd(dhn2 * gf, xn2, r2)

    lat = lambda w: pl.BlockSpec((1, ROW_BLOCK, w), lambda b, j: (b, j, 0))
    full2 = lambda r, c: pl.BlockSpec((r, c), lambda b, j: (0, 0))
    mod_spec = pl.BlockSpec((1, 8, D_MODEL), lambda b, j: (b, 0, 0))
    return pl.pallas_call(
        body, name="ffn", grid=(bs, nbl),
        in_specs=[lat(D_MODEL), lat(D_MODEL), mod_spec, full2(1, D_MODEL), full2(1, D_MODEL),
                  full2(D_MODEL, 2 * D_FF), full2(D_FF, D_MODEL)],
        out_specs=[lat(D_MODEL), lat(D_MODEL), lat(2 * D_FF), lat(D_FF), lat(D_MODEL),
                   full2(1, 1), full2(1, D_MODEL), full2(1, D_MODEL), mod_spec],
        out_shape=[SDS(x1.shape, F32), SDS(x1.shape, BF16), SDS((bs, seq, 2 * D_FF), BF16),
                   SDS((bs, seq, D_FF), BF16), SDS(x1.shape, BF16), SDS((1, 1), F32),
                   SDS((1, D_MODEL), F32), SDS((1, D_MODEL), F32), SDS((bs, 8, D_MODEL), F32)],
        compiler_params=_params(2),
    )(x1, target, mod_c, g_ffn, g_final, w_up, w_down)


def _row_tile(rows):
    return next(m * ROW_BLOCK for m in (4, 2, 1) if rows % (m * ROW_BLOCK) == 0)


def _matmul_tn(a, b, n_blocks, tk, name, carry=None):
    t, m = a.shape
    n = b.shape[1]
    tn = n // n_blocks

    def body(a_ref, b_ref, o_ref):
        @pl.when(pl.program_id(1) == 0)
        def _():
            o_ref[...] = jnp.zeros_like(o_ref)
        o_ref[0] += _dot_tn(a_ref[...], b_ref[...])

    (out,), carried = _host_call(
        body, name=name, grid=(n_blocks, t // tk),
        in_specs=[pl.BlockSpec((tk, m), lambda i, k: (k, 0)), pl.BlockSpec((tk, tn), lambda i, k: (k, i))],
        out_specs=[pl.BlockSpec((1, m, tn), lambda i, k: (i, 0, 0))],
        out_shape=[SDS((n_blocks, m, tn), F32)], args=(a, b), carry=carry)
    return out if carry is None else (out, carried)


SMALL_ROWS = 80
ROW_CCTX = 3


def _small_reduce(gathered, lbg):
    def body(g_ref, lbg_ref, s_ref, dgam_ref):
        tot = g_ref[0:SMALL_ROWS, :]
        for dev in range(1, N_DEV):
            tot = tot + g_ref[dev * SMALL_ROWS:(dev + 1) * SMALL_ROWS, :]
        s_ref[...] = tot
        cc = g_ref[ROW_CCTX:ROW_CCTX + 1, :]
        for dev in range(2, N_DEV, 2):
            cc = cc + g_ref[dev * SMALL_ROWS + ROW_CCTX:dev * SMALL_ROWS + ROW_CCTX + 1, :]
        s_ref[ROW_CCTX:ROW_CCTX + 1, :] = cc
        dlb = tot[7:8, :]
        for d in range(2):
            s0 = _sig(lbg_ref[0, d:d + 1, :] - lbg_ref[1, d:d + 1, :])
            dgam_ref[d:d + 1, :] = dlb[:, d * KW:(d + 1) * KW] * s0 * (1.0 - s0)

    return pl.pallas_call(
        body, name="small_reduce", out_shape=[SDS((SMALL_ROWS, D_MODEL), F32), SDS((2, KW), F32)],
        in_specs=[VMEM_SPEC] * 2, out_specs=[VMEM_SPEC] * 2,
    )(gathered, lbg)


def _pad_cols(a, width):
    return jnp.pad(a, ((0, 0), (0, width - a.shape[1])))


def kernel(x, c, ctx, c_ctx, w_mod, b_mod, g_mix, g_ffn, w_in, lb_gamma, g_norm_a, ln_v_g, ln_v_b, w_s, b_s, w_pa, w_pb, w_o, w_up, w_down, g_final, loss_target, m_c_ctx, m_w_mod, m_b_mod, m_g_mix, m_g_ffn, m_w_in, m_lb_gamma, m_g_norm_a, m_ln_v_g, m_ln_v_b, m_w_s, m_b_s, m_w_pa, m_w_pb, m_w_o, m_w_up, m_w_down, m_g_final, v_c_ctx, v_w_mod, v_b_mod, v_g_mix, v_g_ffn, v_w_in, v_lb_gamma, v_g_norm_a, v_ln_v_g, v_ln_v_b, v_w_s, v_b_s, v_w_pa, v_w_pb, v_w_o, v_w_up, v_w_down, v_g_final):
    ax, ay, ac = lax.axis_index("x"), lax.axis_index("y"), lax.axis_index("c")
    kc = 2 * ax + ay
    dev = 2 * kc + ac
    pos = jnp.stack([kc, ac]).astype(jnp.int32)
    bs, seq, _ = x.shape
    assert bs <= 4 and ctx.shape[1] == ROW_BLOCK and seq % ROW_BLOCK == 0
    mod_cols = w_mod.shape[2]

    lbg_row = _pad_cols(lb_gamma.reshape(1, -1), D_MODEL)
    pay1 = jnp.concatenate([c, jnp.zeros((4 - bs, D_MODEL), F32), c_ctx[None, :], lbg_row,
                            jnp.zeros((2, D_MODEL), F32)], axis=0)
    cond64 = _all_gather8(pay1, "gather_cond")
    lbg_full = cond64.reshape(N_DEV, 8, D_MODEL)[0::2, 5, :KW].reshape(N_CHIPS, 2, 2, HEAD_DIM)
    lbg_full = jnp.transpose(lbg_full, (1, 2, 0, 3)).reshape(2, 2, KW)

    b_mod_s = lax.dynamic_slice(b_mod, (0, kc * mod_cols), (1, mod_cols))
    mod_s = _mod_fwd(cond64, w_mod[0], b_mod_s)
    shards = [w_in[0], w_up[0], w_pa[0], w_pb[0], w_o[0], w_down[0]]
    bufs = _cast_bf16(pos, shards)
    mod_g = _all_gather8(mod_s, "gather_mod").reshape(N_DEV, 64, mod_cols)[0::2]
    mod_full = jnp.transpose(mod_g, (1, 0, 2)).reshape(64, N_CHIPS * mod_cols)
    mod_mine = lax.dynamic_slice(mod_full, (dev * 8, 0), (8, 6 * D_MODEL)).reshape(8, 6, D_MODEL)
    mod, mc = mod_mine[:bs], mod_mine[4]
    zeros4 = jnp.zeros((bs, 4, D_MODEL), F32)
    mod_a = jnp.concatenate([mod[:, 0:2], jnp.broadcast_to(mc[None, 0:2], (bs, 2, D_MODEL)), zeros4], axis=1)
    mod_c = jnp.concatenate([mod[:, 2:6], zeros4], axis=1)

    def cols_major(a):
        return jnp.transpose(a, (1, 0, 2)).reshape(a.shape[1], -1)

    gna, lng, lnb = g_norm_a, ln_v_g, ln_v_b
    ws3 = w_s[0]
    ws3_t = jnp.transpose(ws3, (0, 2, 1))
    bst = jnp.transpose(b_s[0])

    (p, h_all), (w_in_g,) = _in_fwd_own(pos, x, ctx, g_mix, mod_a, w_in[0], carry=_carry_gather_send(bufs[:1]),
                                        then=_carry_gather_forward(bufs[:1]))
    p, sent_up = _in_fwd_rest(pos, h_all, w_in_g, p, carry=_carry_gather_send(bufs[1:2]))
    w_in_f = cols_major(w_in_g)
    fwd_rest = _carry_gather_forward(bufs[2:])
    then_rest = _Carry([], [], {}, fwd_rest.sems, lambda i, o, s: fwd_rest.copies(i[1:], o[1:], s))
    (o_f, s_f, o_b, s_b), gathered = _hgrn_fwd(
        p, lbg_full, carry=_merge_carries(_carry_gather_forward(sent_up), _carry_gather_send(bufs[2:])),
        then=then_rest)
    w_up_f, w_pa_f, w_pb_f = (cols_major(a) for a in gathered[:3])
    w_o_f = gathered[3].reshape(-1, D_MODEL)
    w_down_f = gathered[4].reshape(-1, D_MODEL)
    x1 = _mix_fwd(p, o_f, o_b, x, mod_c, gna, lng, lnb, ws3, bst, w_pa_f, w_pb_f, w_o_f)
    dx1, h2, dab, hid, dffn, loss_part, dg_final, dg_ffn, dmod_ffn = _ffn(
        x1, loss_target, mod_c, g_ffn, g_final[None, :], w_up_f, w_down_f)
    rows_lat = bs * seq
    tk_lat = _row_tile(rows_lat)
    dw_up = _matmul_tn(h2.reshape(rows_lat, D_MODEL), dab.reshape(rows_lat, 2 * D_FF), N_CHIPS, tk_lat, "dw_up")
    dw_down = _matmul_tn(hid.reshape(rows_lat, D_FF), dffn.reshape(rows_lat, D_MODEL), 1, tk_lat, "dw_down")

    def shard_major(a):
        return jnp.transpose(a.reshape(a.shape[0], N_CHIPS, -1), (1, 0, 2))

    def add_halves(parts, recvs, names):
        sums = [_rs_add_halves(pos, g, r, "rs_add_" + nm) for g, r, nm in zip(parts, recvs, names)]
        return [s[0] for s in sums], [s[1] for s in sums]

    def sum_owner(cp32s, recvs, names):
        return [_rs_sum_owner(pos, a, r, "rs_sum_" + nm) for a, r, nm in zip(cp32s, recvs, names)]

    ffn_names, mix_names = ["w_up", "w_down"], ["w_pa", "w_pb", "w_o"]
    part_ffn = [dw_up, dw_down.reshape(N_CHIPS, -1, D_MODEL)]
    (do_raw, dpc, dw_pa, dw_pb, dw_o, dgna, dlng, dlnb, dws, dbst, dmod_mix), sib_ffn = _mix_bwd(
        p, o_f, o_b, dx1, mod_c, gna, lng, lnb, ws3, ws3_t, bst, w_pa_f, w_pb_f, w_o_f,
        carry=_carry_sibling_halves(part_ffn))
    cp32_ffn, cpbf_ffn = add_halves(part_ffn, sib_ffn, ffn_names)
    part_mix = [shard_major(dw_pa), shard_major(dw_pb), dw_o.reshape(N_CHIPS, -1, D_MODEL)]
    (df_f, dq_f, dv_f, dlb0, df_b, dq_b, dv_b, dlb1), got = _hgrn_bwd_pair(
        p, lbg_full, (s_f, s_b), do_raw,
        carry=_merge_carries(_carry_to_owner(cpbf_ffn), _carry_sibling_halves(part_mix)))
    own_ffn, sib_mix = got[:2], got[2:]
    half_ffn = sum_owner(cp32_ffn, own_ffn, ffn_names)
    cp32_mix, cpbf_mix = add_halves(part_mix, sib_mix, mix_names)
    (grad_x, dp, dg_mix, dmod_in), _ = _in_bwd(x, ctx, dx1, g_mix, mod_a, w_in_f, df_f, df_b, dv_f, dv_b, dq_f, dq_b,
                                               dpc)

    rows_all = dp.shape[0] * dp.shape[1]
    tk_all = _row_tile(rows_all)
    dw_in, got = _matmul_tn(h_all.reshape(rows_all, D_MODEL), dp.reshape(rows_all, IN_COLS), N_CHIPS, tk_all, "dw_in",
                            carry=_merge_carries(_carry_join_halves(half_ffn), _carry_to_owner(cpbf_mix)))
    g_ffn_w, own_mix = got[:2], got[2:]
    half_mix = sum_owner(cp32_mix, own_mix, mix_names)

    dmod_mine = jnp.concatenate([dmod_in[:, 0], dmod_in[:, 1], dmod_mix[:, 0], dmod_ffn[:, 1], dmod_ffn[:, 2],
                                 dmod_ffn[:, 3]], axis=1)
    dmc = jnp.concatenate([jnp.sum(dmod_in[:, 2], axis=0), jnp.sum(dmod_in[:, 3], axis=0),
                           jnp.zeros((4 * D_MODEL,), F32)])[None, :]
    pay3 = jnp.concatenate([dmod_mine, jnp.zeros((4 - bs, 6 * D_MODEL), F32), dmc,
                            jnp.zeros((3, 6 * D_MODEL), F32)], axis=0)
    dmod64, got = _all_gather8(pay3, "gather_dmod", carry=_merge_carries(_carry_sibling_halves([dw_in]),
                                                                          _carry_join_halves(half_mix)))
    sib_in, g_mix_w = got[:1], got[1:]
    cp32_in, cpbf_in = add_halves([dw_in], sib_in, ["w_in"])
    own_sems, own_src, own_land, own_token = _owner_send_start(cpbf_in[0])
    dmod64_my = lax.dynamic_slice(dmod64, (0, kc * mod_cols), (64, mod_cols))
    g_w_mod, g_b_mod, g_cctx_part = _mod_bwd(cond64 + own_token[0, 0], dmod64, dmod64_my, w_mod[0], c_ctx[None, :])

    def row(*parts):
        return _pad_cols(jnp.concatenate([q.reshape(1, -1) for q in parts], axis=1), D_MODEL)

    small_rows = [dg_mix, dg_ffn, dg_final, g_cctx_part, row(dgna), row(dlng, dlnb), row(jnp.transpose(dbst)),
                  row(dlb0, dlb1), row(loss_part), jnp.zeros((7, D_MODEL), F32), dws.reshape(64, D_MODEL)]
    pay4 = jnp.concatenate(small_rows, axis=0)
    tot, dgam0 = _small_reduce(_all_gather8(pay4, "gather_small"), lbg_full)

    rest_names = ["w_up", "w_pa", "w_pb", "w_o", "w_down", "w_mod"]
    rest_w = shards[1:] + [w_mod[0]]
    rest_g = [g_ffn_w[0], g_mix_w[0], g_mix_w[1], g_mix_w[2], g_ffn_w[1], g_w_mod]
    rest_m = [m_w_up[0], m_w_pa[0], m_w_pb[0], m_w_o[0], m_w_down[0], m_w_mod[0]]
    rest_v = [v_w_up[0], v_w_pa[0], v_w_pb[0], v_w_o[0], v_w_down[0], v_w_mod[0]]
    (ds_r, m2s_r, v2s_r), _ = _adamw_group(rest_w, rest_g, rest_m, rest_v, "adamw_rest")
    res = {}
    for name, g, d, m2, v2 in zip(rest_names, rest_g, ds_r, m2s_r, v2s_r):
        res[name] = (g[None], d[None], m2[None], v2[None])
    own_in = [_owner_send_wait(own_sems, own_src, own_land, after=(tot, ds_r[0]))]
    g_in_w = _comm_call("rs_join_w_in", _carry_join_halves(sum_owner(cp32_in, own_in, ["w_in"])))
    d, m2, v2 = _adamw_big(shards[0], g_in_w[0], m_w_in[0], v_w_in[0], "adamw_w_in")
    res["w_in"] = (g_in_w[0][None], d[None], m2[None], v2[None])

    loss = tot[8, 0]
    dgam_full = jnp.stack([dgam0, -dgam0])
    g_lbg = lax.dynamic_slice(dgam_full, (0, 0, kc * HEAD_DIM), (2, 2, HEAD_DIM))

    small = [
        ("c_ctx", c_ctx[None, :], tot[3:4], m_c_ctx, v_c_ctx),
        ("b_mod", b_mod, g_b_mod, m_b_mod, v_b_mod),
        ("g_mix", g_mix, tot[0:1], m_g_mix, v_g_mix),
        ("g_ffn", g_ffn, tot[1:2], m_g_ffn, v_g_ffn),
        ("lb_gamma", lb_gamma.reshape(4, HEAD_DIM), g_lbg.reshape(4, HEAD_DIM), m_lb_gamma, v_lb_gamma),
        ("g_norm_a", g_norm_a, tot[4:5, :HEAD_DIM], m_g_norm_a, v_g_norm_a),
        ("ln_v_g", ln_v_g, tot[5:6, :KW], m_ln_v_g, v_ln_v_g),
        ("ln_v_b", ln_v_b, tot[5:6, KW:], m_ln_v_b, v_ln_v_b),
        ("w_s", w_s.reshape(N_HEADS * SGU_CHUNK, SGU_CHUNK), tot[16:80].reshape(N_HEADS * SGU_CHUNK, SGU_CHUNK),
         m_w_s, v_w_s),
        ("b_s", b_s[0], tot[6:7, :KW].reshape(N_HEADS, SGU_CHUNK), m_b_s, v_b_s),
        ("g_final", g_final[None, :], tot[2:3], m_g_final, v_g_final),
    ]
    ws_, gs_ = [s[1] for s in small], [s[2] for s in small]
    ms_ = [s[3].reshape(s[1].shape) for s in small]
    vs_ = [s[4].reshape(s[1].shape) for s in small]
    ds_, m2s_, v2s_ = _adamw_small(ws_, gs_, ms_, vs_)
    for (name, _, g, m, _), d, m2, v2 in zip(small, ds_, m2s_, v2s_):
        res[name] = tuple(t.reshape(m.shape) for t in (g, d, m2, v2))

    order = ["c_ctx", "w_mod", "b_mod", "g_mix", "g_ffn", "w_in", "lb_gamma", "g_norm_a", "ln_v_g", "ln_v_b",
             "w_s", "b_s", "w_pa", "w_pb", "w_o", "w_up", "w_down", "g_final"]
    outs = [loss, grad_x]
    for part in range(4):
        outs += [res[n][part] for n in order]
    return tuple(outs)
```

```python
import functools
import math

import jax
import jax.numpy as jnp
import numpy as np
from jax import lax
from jax.experimental import pallas as pl
from jax.experimental.pallas import tpu as pltpu

F32 = jnp.float32
BF16 = jnp.bfloat16
SDS = jax.ShapeDtypeStruct
MESH = pl.DeviceIdType.MESH

EPS = 1e-6
D_MODEL = 1024
N_HEADS = 4
HEAD_DIM = 128
KW = N_HEADS * HEAD_DIM
IN_COLS = 11 * KW
D_FF = 2816
HGRN_CHUNK = 64
SGU_CHUNK = 128
ROW_BLOCK = 256
N_CHIPS = 4
N_DEV = 8
V7X_VMEM_BYTES = 64 * 1024 * 1024
VMEM_LIMIT = V7X_VMEM_BYTES - 6 * 1024 * 1024

ADAM_LR, ADAM_B1, ADAM_B2, ADAM_EPS, ADAM_WD, ADAM_STEP = 0.001, 0.9, 0.999, 1e-08, 0.01, 10
GELU_C0 = math.sqrt(2.0 / math.pi)
GELU_C1 = 0.044715

VMEM_SPEC = pl.BlockSpec(memory_space=pltpu.VMEM)
ANY_SPEC = pl.BlockSpec(memory_space=pl.ANY)


def _params(n_grid):
    return pltpu.CompilerParams(dimension_semantics=("arbitrary",) * n_grid, vmem_limit_bytes=VMEM_LIMIT)


def _sig(x):
    return 0.5 * jnp.tanh(0.5 * x) + 0.5


def _gelu(x):
    t = jnp.tanh(GELU_C0 * (x + GELU_C1 * x * x * x))
    return 0.5 * x * (1.0 + t), t


def _dgelu(x, t):
    return 0.5 * (1.0 + t) + 0.5 * x * (1.0 - t * t) * GELU_C0 * (1.0 + 3.0 * GELU_C1 * x * x)


def _dot(a, b):
    return jnp.dot(a.astype(BF16), b.astype(BF16), preferred_element_type=F32)


def _dot_nt(a, b):
    return lax.dot_general(a.astype(BF16), b.astype(BF16), (((1,), (1,)), ((), ())), preferred_element_type=F32)


def _dot_tn(a, b):
    return lax.dot_general(a.astype(BF16), b.astype(BF16), (((0,), (0,)), ((), ())), preferred_element_type=F32)


def _dot_f32(a, b, dims=(((1,), (0,)), ((), ()))):
    return lax.dot_general(a, b, dims, precision=lax.Precision.HIGHEST, preferred_element_type=F32)


def _rms(x):
    r = lax.rsqrt(jnp.mean(x * x, axis=-1, keepdims=True) + EPS)
    return x * r, r


def _rms_bwd(dxn, xn, r):
    return r * (dxn - xn * jnp.mean(dxn * xn, axis=-1, keepdims=True))


def _colsum(a):
    return jnp.sum(a, axis=0, keepdims=True)


def _tri(n, upper):
    t = lax.broadcasted_iota(jnp.int32, (n, n), 0)
    s = lax.broadcasted_iota(jnp.int32, (n, n), 1)
    return (s >= t) if upper else (s <= t)


def _all_gather8(x_shard, name, carry=None, then=None):
    m_per, n = x_shard.shape
    n_ci = len(carry.ins) if carry else 0
    n_co = len(carry.outs) if carry else 0
    n_cs = len(carry.sems) if carry else 0

    def body(*refs):
        x_ref, cins = refs[0], refs[1:1 + n_ci]
        out_ref, couts = refs[1 + n_ci], refs[2 + n_ci:2 + n_ci + n_co]
        send_sems, recv_sems, local_sem = refs[2 + n_ci + n_co:5 + n_ci + n_co]
        csems = refs[5 + n_ci + n_co:5 + n_ci + n_co + n_cs]
        tsems = refs[5 + n_ci + n_co + n_cs:]
        if carry:
            _start_all(carry.copies(cins, couts, csems)[0])
        _gather8_body(x_ref, out_ref, send_sems, recv_sems, local_sem, m_per)
        if carry:
            _wait_all(carry.copies(cins, couts, csems)[1])
        if then:
            _start_all(then.copies(couts, couts, tsems)[0])
            _wait_all(then.copies(couts, couts, tsems)[1])

    res = pl.pallas_call(
        body, name=name, out_shape=[SDS((N_DEV * m_per, n), x_shard.dtype)] + (carry.outs if carry else []),
        in_specs=[VMEM_SPEC] + [ANY_SPEC] * n_ci, out_specs=[VMEM_SPEC] + [ANY_SPEC] * n_co,
        input_output_aliases={1 + i: 1 + o for i, o in carry.alias.items()} if carry else {},
        scratch_shapes=[pltpu.SemaphoreType.DMA((7,)), pltpu.SemaphoreType.DMA((7,)), pltpu.SemaphoreType.DMA]
        + (carry.sems if carry else []) + (then.sems if then else []),
    )(x_shard, *(carry.ins if carry else []))
    return res[0] if carry is None else (res[0], list(res[1:]))


def _gather8_body(x_ref, out_ref, send_sems, recv_sems, local_sem, m_per):
    x, y, c = lax.axis_index("x"), lax.axis_index("y"), lax.axis_index("c")
    me, sibling = (x, y, c), (x, y, 1 - c)
    chips = [(1 - x, y), (x, 1 - y), (1 - x, 1 - y)]

    def rows(px, py, pc):
        return out_ref.at[pl.ds((4 * px + 2 * py + pc) * m_per, m_per), :]

    def copy(k, block, to, src=None):
        return pltpu.make_async_remote_copy(
            src_ref=rows(*block) if src is None else src, dst_ref=rows(*block),
            send_sem=send_sems.at[k], recv_sem=recv_sems.at[k], device_id=to, device_id_type=MESH)

    mine = pltpu.make_async_copy(x_ref, rows(*me), local_sem)
    mine.start()
    first = [copy(0, me, sibling, src=x_ref)]
    first += [copy(1 + j, me, (*chip, c), src=x_ref) for j, chip in enumerate(chips)]
    for cp in first:
        cp.start()
    passed = [copy(4 + j, (*chip, c), sibling) for j, chip in enumerate(chips)]
    for j, chip in enumerate(chips):
        copy(1 + j, (*chip, c), me).wait_recv()
        passed[j].start()
    copy(0, sibling, me).wait_recv()
    for j, chip in enumerate(chips):
        copy(4 + j, (*chip, 1 - c), me).wait_recv()
    for cp in first + passed:
        cp.wait_send()
    mine.wait()


def _mesh_pos():
    x, y, c = lax.axis_index("x"), lax.axis_index("y"), lax.axis_index("c")
    chips = [(1 - x, y), (x, 1 - y), (1 - x, 1 - y)]
    return x, y, c, 2 * x + y, (x, y, 1 - c), chips


def _half_rows(c, rh):
    return pl.ds(pl.multiple_of(c * rh, 16), rh)


class _Carry:
    def __init__(self, ins, outs, alias, sems, copies):
        self.ins, self.outs, self.alias, self.sems, self.copies = list(ins), list(outs), dict(alias), list(sems), copies


def _remote(src, dst, send, recv, to):
    return functools.partial(pltpu.make_async_remote_copy, src_ref=src, dst_ref=dst, send_sem=send, recv_sem=recv,
                             device_id=to, device_id_type=MESH)


def _carry_gather_send(bufs):
    n = len(bufs)

    def copies(ins, outs, sems):
        x, y, c, kc, sibling, chips = _mesh_pos()
        starts, waits = [], []
        for wi in range(n):
            rh = outs[wi].shape[1] // 2
            for jj, chip in enumerate(chips):
                mine = outs[wi].at[kc, _half_rows(c, rh), :]
                cp = _remote(mine, mine, sems[0].at[wi, jj], sems[1].at[wi, jj], (*chip, c))
                starts.append(cp)
                waits.append((cp, "send"))
                theirs = outs[wi].at[2 * chip[0] + chip[1], _half_rows(c, rh), :]
                waits.append((_remote(theirs, theirs, sems[0].at[wi, jj], sems[1].at[wi, jj], (*chip, c)), "recv"))
        return starts, waits

    return _Carry(bufs, [SDS(b.shape, b.dtype) for b in bufs], {i: i for i in range(n)},
                  [pltpu.SemaphoreType.DMA((n, 3)), pltpu.SemaphoreType.DMA((n, 3))], copies)


def _carry_gather_forward(bufs):
    n = len(bufs)

    def copies(ins, outs, sems):
        x, y, c, kc, sibling, chips = _mesh_pos()
        starts, waits = [], []
        for wi in range(n):
            rh = outs[wi].shape[1] // 2
            for jj, chip in enumerate(chips):
                got = outs[wi].at[2 * chip[0] + chip[1], _half_rows(c, rh), :]
                cp = _remote(got, got, sems[0].at[wi, jj], sems[1].at[wi, jj], sibling)
                starts.append(cp)
                waits.append((cp, "send"))
                other = outs[wi].at[2 * chip[0] + chip[1], _half_rows(1 - c, rh), :]
                waits.append((_remote(other, other, sems[0].at[wi, jj], sems[1].at[wi, jj], sibling), "recv"))
        return starts, waits

    return _Carry(bufs, [SDS(b.shape, b.dtype) for b in bufs], {i: i for i in range(n)},
                  [pltpu.SemaphoreType.DMA((n, 3)), pltpu.SemaphoreType.DMA((n, 3))], copies)


def _carry_sibling_halves(grads):
    n = len(grads)

    def copies(ins, outs, sems):
        x, y, c, kc, sibling, chips = _mesh_pos()
        cps = [_remote(ins[wi].at[:, _half_rows(1 - c, ins[wi].shape[1] // 2), :], outs[wi],
                       sems[0].at[wi], sems[1].at[wi], sibling) for wi in range(n)]
        return cps, [(cp, "both") for cp in cps]

    return _Carry(grads, [SDS((N_CHIPS, g.shape[1] // 2, g.shape[2]), F32) for g in grads], {},
                  [pltpu.SemaphoreType.DMA((n,)), pltpu.SemaphoreType.DMA((n,))], copies)


def _carry_to_owner(cpbfs):
    n = len(cpbfs)

    def copies(ins, outs, sems):
        x, y, c, kc, sibling, chips = _mesh_pos()
        starts, waits = [], []
        for wi in range(n):
            for jj, chip in enumerate(chips):
                cp = _remote(ins[wi].at[2 * chip[0] + chip[1]], outs[wi].at[kc],
                             sems[0].at[wi, jj], sems[1].at[wi, jj], (*chip, c))
                starts.append(cp)
                waits.append((cp, "send"))
                slot = outs[wi].at[2 * chip[0] + chip[1]]
                waits.append((_remote(slot, slot, sems[0].at[wi, jj], sems[1].at[wi, jj], (*chip, c)), "recv"))
        return starts, waits

    return _Carry(cpbfs, [SDS(g.shape, BF16) for g in cpbfs], {},
                  [pltpu.SemaphoreType.DMA((n, 3)), pltpu.SemaphoreType.DMA((n, 3))], copies)


def _carry_join_halves(bufs):
    n = len(bufs)

    def copies(ins, outs, sems):
        x, y, c, kc, sibling, chips = _mesh_pos()
        cps = []
        for wi in range(n):
            mine = outs[wi].at[_half_rows(c, outs[wi].shape[0] // 2), :]
            cps.append(_remote(mine, mine, sems[0].at[wi], sems[1].at[wi], sibling))
        return cps, [(cp, "both") for cp in cps]

    return _Carry(bufs, [SDS(b.shape, F32) for b in bufs], {i: i for i in range(n)},
                  [pltpu.SemaphoreType.DMA((n,)), pltpu.SemaphoreType.DMA((n,))], copies)


def _merge_carries(*carries):
    ins, outs, alias, sems, parts = [], [], {}, [], []
    for cy in carries:
        parts.append((len(ins), len(cy.ins), len(outs), len(cy.outs), len(sems), len(cy.sems), cy.copies))
        alias.update({len(ins) + i: len(outs) + o for i, o in cy.alias.items()})
        ins += cy.ins
        outs += cy.outs
        sems += cy.sems

    def copies(i, o, s):
        starts, waits = [], []
        for i0, ni, o0, no, s0, ns, fn in parts:
            st, wt = fn(i[i0:i0 + ni], o[o0:o0 + no], s[s0:s0 + ns])
            starts += st
            waits += wt
        return starts, waits

    return _Carry(ins, outs, alias, sems, copies)


def _start_all(starts):
    for cp in starts:
        cp().start()


def _wait_all(waits):
    for cp, which in waits:
        if which == "send":
            cp().wait_send()
        elif which == "recv":
            cp().wait_recv()
        else:
            cp().wait()


HBM_SPEC = pl.BlockSpec(memory_space=pltpu.HBM)
SEM_SPEC = pl.BlockSpec(memory_space=pltpu.SEMAPHORE)
SPLIT_COPY_EFFECT = pltpu.SideEffectType.DATAFLOW_SIDE_EFFECTING


def _owner_send_start(cpbf, after):
    land = lax.empty(cpbf.shape, cpbf.dtype)

    def body(src_ref, land_ref, after_ref, s0, s1, s2, r0, r1, r2, src_thru, land_thru, token):
        x, y, c, kc, sibling, chips = _mesh_pos()
        for jj, (chip, s_sem, r_sem) in enumerate(zip(chips, (s0, s1, s2), (r0, r1, r2))):
            pltpu.make_async_remote_copy(
                src_ref=src_ref.at[2 * chip[0] + chip[1]], dst_ref=land_ref.at[kc], send_sem=s_sem, recv_sem=r_sem,
                device_id=(*chip, c), device_id_type=MESH).start()
        token[...] = jnp.zeros_like(token)

    buf = pltpu.HBM(cpbf.shape, cpbf.dtype)
    outs = pl.pallas_call(
        body, name="rs_owner_w_in_start",
        out_shape=(pltpu.SemaphoreType.DMA(()),) * 6 + (buf, buf, SDS((8, 128), F32)),
        in_specs=(HBM_SPEC, HBM_SPEC, ANY_SPEC), out_specs=(SEM_SPEC,) * 6 + (HBM_SPEC, HBM_SPEC, VMEM_SPEC),
        input_output_aliases={0: 6, 1: 7},
        compiler_params=pltpu.CompilerParams(has_side_effects=SPLIT_COPY_EFFECT),
    )(pltpu.with_memory_space_constraint(cpbf, pltpu.HBM), pltpu.with_memory_space_constraint(land, pltpu.HBM), after)
    return outs[:6], outs[6], outs[7], outs[8]


def _owner_send_wait(sems, src_thru, land_thru, after):
    n_after = len(after)

    def body(*refs):
        src_ref, land_ref = refs[0], refs[1]
        sends, recvs = refs[2:5], refs[5:8]
        x, y, c, kc, sibling, chips = _mesh_pos()
        for jj, chip in enumerate(chips):
            slot = 2 * chip[0] + chip[1]
            cp = pltpu.make_async_remote_copy(
                src_ref=src_ref.at[slot], dst_ref=land_ref.at[slot], send_sem=sends[jj], recv_sem=recvs[jj],
                device_id=(*chip, c), device_id_type=MESH)
            cp.wait_send()
            cp.wait_recv()

    buf = pltpu.HBM(land_thru.shape, land_thru.dtype)
    return pl.pallas_call(
        body, name="rs_owner_w_in_wait", out_shape=(buf, buf),
        in_specs=(HBM_SPEC, HBM_SPEC) + (SEM_SPEC,) * 6 + (ANY_SPEC,) * n_after, out_specs=(HBM_SPEC, HBM_SPEC),
        input_output_aliases={0: 0, 1: 1},
        compiler_params=pltpu.CompilerParams(has_side_effects=SPLIT_COPY_EFFECT),
    )(src_thru, land_thru, *sems, *after)[1]


def _comm_call(name, carry):
    n_i, n_o = len(carry.ins), len(carry.outs)

    def body(*refs):
        ins, outs, sems = refs[:n_i], refs[n_i:n_i + n_o], refs[n_i + n_o:]
        _start_all(carry.copies(ins, outs, sems)[0])
        _wait_all(carry.copies(ins, outs, sems)[1])

    return pl.pallas_call(
        body, name=name, out_shape=carry.outs, in_specs=[ANY_SPEC] * n_i, out_specs=[ANY_SPEC] * n_o,
        input_output_aliases=carry.alias, scratch_shapes=carry.sems,
    )(*carry.ins)


def _host_call(body, *, name, grid, in_specs, out_specs, out_shape, args, scratch_shapes=(), carry=None, then=None,
               prefetch=None, aliases=None):
    n_in, n_out, n_scr = len(in_specs), len(out_specs), len(scratch_shapes)
    n_ci = len(carry.ins) if carry else 0
    n_co = len(carry.outs) if carry else 0
    n_cs = len(carry.sems) if carry else 0
    n_pf = 0 if prefetch is None else 1

    def wrapped(*refs):
        pf, refs = refs[:n_pf], refs[n_pf:]
        ins, cins = refs[:n_in], refs[n_in:n_in + n_ci]
        o0 = n_in + n_ci
        outs, couts = refs[o0:o0 + n_out], refs[o0 + n_out:o0 + n_out + n_co]
        s0 = o0 + n_out + n_co
        scr, sems, tsems = refs[s0:s0 + n_scr], refs[s0 + n_scr:s0 + n_scr + n_cs], refs[s0 + n_scr + n_cs:]
        idx = [pl.program_id(a) for a in range(len(grid))]
        first = functools.reduce(jnp.logical_and, [i == 0 for i in idx])
        last = functools.reduce(jnp.logical_and, [i == g - 1 for i, g in zip(idx, grid)])

        if carry:
            @pl.when(first)
            def _():
                _start_all(carry.copies(cins, couts, sems)[0])

        body(*pf, *ins, *outs, *scr)

        if carry:
            @pl.when(last)
            def _():
                _wait_all(carry.copies(cins, couts, sems)[1])
                if then:
                    _start_all(then.copies(couts, couts, tsems)[0])
                    _wait_all(then.copies(couts, couts, tsems)[1])

    all_in = list(in_specs) + [ANY_SPEC] * n_ci
    all_out = list(out_specs) + [ANY_SPEC] * n_co
    all_scr = list(scratch_shapes) + (carry.sems if carry else []) + (then.sems if then else [])
    alias = {n_pf + i: o for i, o in (aliases or {}).items()}
    if carry:
        alias.update({n_pf + n_in + i: n_out + o for i, o in carry.alias.items()})
    kwargs = dict(name=name, out_shape=list(out_shape) + (carry.outs if carry else []), input_output_aliases=alias,
                  compiler_params=_params(len(grid)))
    if prefetch is None:
        call = pl.pallas_call(wrapped, grid=grid, in_specs=all_in, out_specs=all_out, scratch_shapes=all_scr, **kwargs)
        res = call(*args, *(carry.ins if carry else []))
    else:
        call = pl.pallas_call(wrapped, grid_spec=pltpu.PrefetchScalarGridSpec(
            num_scalar_prefetch=1, grid=grid, in_specs=all_in, out_specs=all_out, scratch_shapes=all_scr), **kwargs)
        res = call(prefetch, *args, *(carry.ins if carry else []))
    return list(res[:n_out]), list(res[n_out:])


def _rs_add_halves(pos, grad, recv, name):
    _, rs, cs = grad.shape
    rh = rs // 2
    rb = rh // 2

    def body(pos_ref, g_ref, r_ref, o32_ref, obf_ref):
        s = g_ref[...] + r_ref[...]
        o32_ref[...] = s
        obf_ref[...] = s.astype(BF16)

    blk = (1, rb, cs)
    return pl.pallas_call(
        body, name=name,
        grid_spec=pltpu.PrefetchScalarGridSpec(
            num_scalar_prefetch=1, grid=(N_CHIPS, 2),
            in_specs=[pl.BlockSpec(blk, lambda k, i, p: (k, p[1] * 2 + i, 0)),
                      pl.BlockSpec(blk, lambda k, i, p: (k, i, 0))],
            out_specs=[pl.BlockSpec(blk, lambda k, i, p: (k, i, 0)), pl.BlockSpec(blk, lambda k, i, p: (k, i, 0))]),
        out_shape=[SDS((N_CHIPS, rh, cs), F32), SDS((N_CHIPS, rh, cs), BF16)],
        compiler_params=_params(2),
    )(pos, grad, recv)


def _rs_sum_owner(pos, cp32, recv3, name):
    _, rh, cs = cp32.shape
    rb = rh // 2

    def body(pos_ref, own_ref, r1_ref, r2_ref, r3_ref, o_ref):
        o_ref[...] = ((own_ref[0] + r1_ref[0].astype(F32)) + r2_ref[0].astype(F32)) + r3_ref[0].astype(F32)

    blk = (1, rb, cs)

    def slot(d):
        return pl.BlockSpec(blk, lambda i, p: ((p[0] + d) % N_CHIPS, i, 0))

    return pl.pallas_call(
        body, name=name,
        grid_spec=pltpu.PrefetchScalarGridSpec(
            num_scalar_prefetch=1, grid=(2,),
            in_specs=[slot(0), slot(1), slot(2), slot(3)],
            out_specs=pl.BlockSpec((rb, cs), lambda i, p: (p[1] * 2 + i, 0))),
        out_shape=SDS((2 * rh, cs), F32),
        compiler_params=_params(1),
    )(pos, cp32, recv3, recv3, recv3)


def _cast_bf16(pos, arrs):
    n = len(arrs)

    def body(pos_ref, *refs):
        for i in range(n):
            refs[n + i][0] = refs[i][...].astype(BF16)

    return pl.pallas_call(
        body, name="cast_bf16",
        grid_spec=pltpu.PrefetchScalarGridSpec(
            num_scalar_prefetch=1, grid=(2,),
            in_specs=[pl.BlockSpec((a.shape[0] // 2, a.shape[1]), lambda i, p: (i, 0)) for a in arrs],
            out_specs=[pl.BlockSpec((1, a.shape[0] // 2, a.shape[1]), lambda i, p: (p[0], i, 0)) for a in arrs]),
        out_shape=[SDS((N_CHIPS,) + a.shape, BF16) for a in arrs],
        compiler_params=_params(1),
    )(pos, *arrs)


def _adamw_vals(w, g, m, v):
    m2 = ADAM_B1 * m + (1.0 - ADAM_B1) * g
    v2 = ADAM_B2 * v + (1.0 - ADAM_B2) * (g * g)
    m_hat = m2 / (1.0 - ADAM_B1 ** ADAM_STEP)
    v_hat = v2 / (1.0 - ADAM_B2 ** ADAM_STEP)
    delta = -ADAM_LR * (m_hat / (jnp.sqrt(v_hat) + ADAM_EPS) + ADAM_WD * w)
    return delta, m2, v2


def _adamw_big(w, g, m, v, name):
    rows, cols = w.shape
    rb = rows // 4

    def body(w_ref, g_ref, m_ref, v_ref, d_ref, m2_ref, v2_ref):
        d, m2, v2 = _adamw_vals(w_ref[...], g_ref[...], m_ref[...], v_ref[...])
        d_ref[...] = d
        m2_ref[...] = m2
        v2_ref[...] = v2

    spec = pl.BlockSpec((rb, cols), lambda i: (i, 0))
    return pl.pallas_call(
        body, name=name, grid=(4,), in_specs=[spec] * 4, out_specs=[spec] * 3,
        out_shape=[SDS(w.shape, F32)] * 3, compiler_params=_params(1),
    )(w, g, m, v)


ADAMW_GROUP_STEPS = 8


def _adamw_group(ws, gs, ms, vs, name, carry=None):
    n = len(ws)

    def body(*refs):
        for i in range(n):
            d, m2, v2 = _adamw_vals(refs[i][...], refs[n + i][...], refs[2 * n + i][...], refs[3 * n + i][...])
            refs[4 * n + i][...] = d
            refs[5 * n + i][...] = m2
            refs[6 * n + i][...] = v2

    specs = [pl.BlockSpec((w.shape[0] // ADAMW_GROUP_STEPS, w.shape[1]), lambda i: (i, 0)) for w in ws]
    shapes = [SDS(w.shape, F32) for w in ws]
    outs, carried = _host_call(
        body, name=name, grid=(ADAMW_GROUP_STEPS,), in_specs=specs * 4, out_specs=specs * 3, out_shape=shapes * 3,
        args=(*ws, *gs, *ms, *vs), carry=carry)
    return (outs[:n], outs[n:2 * n], outs[2 * n:]), carried


def _adamw_small(ws, gs, ms, vs):
    n = len(ws)

    def body(*refs):
        for i in range(n):
            d, m2, v2 = _adamw_vals(refs[i][...], refs[n + i][...], refs[2 * n + i][...], refs[3 * n + i][...])
            refs[4 * n + i][...] = d
            refs[5 * n + i][...] = m2
            refs[6 * n + i][...] = v2

    shapes = [SDS(w.shape, F32) for w in ws]
    outs = pl.pallas_call(
        body, name="adamw_small", out_shape=shapes * 3,
        in_specs=[VMEM_SPEC] * (4 * n), out_specs=[VMEM_SPEC] * (3 * n),
    )(*ws, *gs, *ms, *vs)
    return outs[:n], outs[n:2 * n], outs[2 * n:]


def _mod_fwd(cond64, w_mod_s, b_mod_s):
    def body(c_ref, w_ref, b_ref, o_ref):
        cc = c_ref[...]
        o_ref[...] = _dot_f32(cc * _sig(cc), w_ref[...]) + b_ref[...]

    return pl.pallas_call(
        body, name="mod_fwd", out_shape=SDS((cond64.shape[0], w_mod_s.shape[1]), F32),
        in_specs=[VMEM_SPEC] * 3, out_specs=VMEM_SPEC,
        compiler_params=pltpu.CompilerParams(vmem_limit_bytes=VMEM_LIMIT),
    )(cond64, w_mod_s, b_mod_s)


def _mod_bwd(cond64, dmod64, dmod64_my, w_mod_s, c_ctx):
    def body(c_ref, g_ref, gm_ref, w_ref, cc_ref, gw_ref, gb_ref, gcc_ref):
        cc = c_ref[...]
        act = cc * _sig(cc)
        gm = gm_ref[...]
        gw_ref[...] = _dot_f32(act, gm, (((0,), (0,)), ((), ())))
        gb_ref[...] = _colsum(g_ref[...])
        dact = _dot_f32(gm, w_ref[...], (((1,), (1,)), ((), ())))
        tot = dact[4:5, :]
        for dev in range(1, N_DEV):
            tot = tot + dact[8 * dev + 4:8 * dev + 5, :]
        c0 = cc_ref[...]
        s0 = _sig(c0)
        gcc_ref[...] = tot * (s0 * (1.0 + c0 * (1.0 - s0)))

    return pl.pallas_call(
        body, name="mod_bwd",
        out_shape=[SDS(w_mod_s.shape, F32), SDS((1, dmod64.shape[1]), F32), SDS((1, D_MODEL), F32)],
        in_specs=[VMEM_SPEC] * 5, out_specs=[VMEM_SPEC] * 3,
        compiler_params=pltpu.CompilerParams(vmem_limit_bytes=VMEM_LIMIT),
    )(cond64, dmod64, dmod64_my, w_mod_s, c_ctx)


SHARD_COLS = IN_COLS // N_CHIPS


def _in_fwd_own(pos, x, ctx, g_mix, mod_a, w_own, carry=None, then=None):
    bs, seq, _ = x.shape
    nb = seq // ROW_BLOCK + 1

    def body(pos_ref, x_ref, ctx_ref, g_ref, mod_ref, w_ref, p_ref, h_ref, w_bf):
        b, j = pl.program_id(0), pl.program_id(1)

        @pl.when((b == 0) & (j == 0))
        def _():
            w_bf[...] = w_ref[...].astype(BF16)

        is_ctx = j == 0
        xin = jnp.where(is_ctx, ctx_ref[0], x_ref[0])
        shift = jnp.where(is_ctx, mod_ref[0, 2:3, :], mod_ref[0, 0:1, :])
        scale = jnp.where(is_ctx, mod_ref[0, 3:4, :], mod_ref[0, 1:2, :])
        xn, _ = _rms(xin)
        hb = ((xn * g_ref[...]) * (1.0 + scale) + shift).astype(BF16)
        h_ref[0] = hb
        p_ref[0] = jnp.dot(hb, w_bf[...], preferred_element_type=F32)

    return _host_call(
        body, name="in_fwd_own", grid=(bs, nb), prefetch=pos,
        in_specs=[pl.BlockSpec((1, ROW_BLOCK, D_MODEL), lambda b, j, p: (b, jnp.maximum(j - 1, 0), 0)),
                  pl.BlockSpec((1, ROW_BLOCK, D_MODEL), lambda b, j, p: (b, 0, 0)),
                  pl.BlockSpec((1, D_MODEL), lambda b, j, p: (0, 0)),
                  pl.BlockSpec((1, 8, D_MODEL), lambda b, j, p: (b, 0, 0)),
                  pl.BlockSpec((D_MODEL, SHARD_COLS), lambda b, j, p: (0, 0))],
        out_specs=[pl.BlockSpec((1, ROW_BLOCK, SHARD_COLS), lambda b, j, p: (b, j, p[0])),
                   pl.BlockSpec((1, ROW_BLOCK, D_MODEL), lambda b, j, p: (b, j, 0))],
        out_shape=[SDS((bs, nb * ROW_BLOCK, IN_COLS), F32), SDS((bs, nb * ROW_BLOCK, D_MODEL), BF16)],
        scratch_shapes=[pltpu.VMEM((D_MODEL, SHARD_COLS), BF16)],
        args=(x, ctx, g_mix, mod_a, w_own), carry=carry, then=then)


def _in_fwd_rest(pos, h_all, w_in_g, p, carry=None):
    bs, rows, _ = h_all.shape
    rows_all = bs * rows
    tile = next(m * ROW_BLOCK for m in (3, 1) if rows_all % (m * ROW_BLOCK) == 0)

    def body(pos_ref, h_ref, w_ref, p_in_ref, p_ref):
        p_ref[...] = jnp.dot(h_ref[...], w_ref[0], preferred_element_type=F32)

    shard = lambda n, p: (p[0] + 1 + n) % N_CHIPS
    (p2,), carried = _host_call(
        body, name="in_fwd_rest", grid=(N_CHIPS - 1, rows_all // tile), prefetch=pos,
        in_specs=[pl.BlockSpec((tile, D_MODEL), lambda n, t, p: (t, 0)),
                  pl.BlockSpec((1, D_MODEL, SHARD_COLS), lambda n, t, p: (shard(n, p), 0, 0)),
                  ANY_SPEC],
        out_specs=[pl.BlockSpec((tile, SHARD_COLS), lambda n, t, p: (t, shard(n, p)))],
        out_shape=[SDS((rows_all, IN_COLS), F32)], aliases={2: 0},
        args=(h_all.reshape(rows_all, D_MODEL), w_in_g, p.reshape(rows_all, IN_COLS)), carry=carry)
    return p2.reshape(bs, rows, IN_COLS), carried


def _in_bwd(x, ctx, dx1, g_mix, mod_a, w_in, df_f, df_b, dv_f, dv_b, dq_f, dq_b, dpc, carry=None):
    bs, seq, _ = x.shape
    nb = seq // ROW_BLOCK + 1

    def body(x_ref, ctx_ref, dx1_ref, g_ref, mod_ref, w_ref, dff_ref, dfb_ref, dvf_ref, dvb_ref, dqf_ref, dqb_ref,
             dpc_ref, gx_ref, dp_ref, dg_ref, dmod_ref):
        b, j = pl.program_id(0), pl.program_id(1)
        is_ctx = j == 0

        @pl.when((b == 0) & (j == 0))
        def _():
            dg_ref[...] = jnp.zeros_like(dg_ref)

        @pl.when(j == 0)
        def _():
            dmod_ref[...] = jnp.zeros_like(dmod_ref)

        di = (dvf_ref[0] + dvb_ref[0]).astype(BF16)
        dq = (dqf_ref[0] + dqb_ref[0]).astype(BF16)
        dp = jnp.concatenate([dff_ref[0], dfb_ref[0], di, dq, dpc_ref[0]], axis=1)
        dp_ref[0] = dp
        dh = lax.dot_general(dp, w_ref[...], (((1,), (1,)), ((), ())), preferred_element_type=F32)
        xin = jnp.where(is_ctx, ctx_ref[0], x_ref[0])
        scale = jnp.where(is_ctx, mod_ref[0, 3:4, :], mod_ref[0, 1:2, :])
        xn, r = _rms(xin)
        g = g_ref[...]
        hn = xn * g
        d_shift = _colsum(dh)
        d_scale = _colsum(dh * hn)
        dhn = dh * (1.0 + scale)
        dg_ref[...] += _colsum(dhn * xn)
        dx = _rms_bwd(dhn * g, xn, r)

        @pl.when(is_ctx)
        def _():
            dmod_ref[0, 2:3, :] += d_shift
            dmod_ref[0, 3:4, :] += d_scale

        @pl.when(jnp.logical_not(is_ctx))
        def _():
            dmod_ref[0, 0:1, :] += d_shift
            dmod_ref[0, 1:2, :] += d_scale
            gx_ref[0] = dx + dx1_ref[0]

    def rows(w):
        return pl.BlockSpec((1, ROW_BLOCK, w), lambda b, j: (b, j, 0))

    lat = pl.BlockSpec((1, ROW_BLOCK, D_MODEL), lambda b, j: (b, jnp.maximum(j - 1, 0), 0))
    return _host_call(
        body, name="in_bwd", grid=(bs, nb),
        in_specs=[lat, pl.BlockSpec((1, ROW_BLOCK, D_MODEL), lambda b, j: (b, 0, 0)), lat,
                  pl.BlockSpec((1, D_MODEL), lambda b, j: (0, 0)),
                  pl.BlockSpec((1, 8, D_MODEL), lambda b, j: (b, 0, 0)),
                  pl.BlockSpec((D_MODEL, IN_COLS), lambda b, j: (0, 0)),
                  rows(KW), rows(KW), rows(KW), rows(KW), rows(KW), rows(KW), rows(7 * KW)],
        out_specs=[lat, rows(IN_COLS), pl.BlockSpec((1, D_MODEL), lambda b, j: (0, 0)),
                   pl.BlockSpec((1, 8, D_MODEL), lambda b, j: (b, 0, 0))],
        out_shape=[SDS(x.shape, F32), SDS((bs, nb * ROW_BLOCK, IN_COLS), BF16), SDS((1, D_MODEL), F32),
                   SDS((bs, 8, D_MODEL), F32)],
        args=(x, ctx, dx1, g_mix, mod_a, w_in, df_f, df_b, dv_f, dv_b, dq_f, dq_b, dpc), carry=carry)


def _lower_bound(lbg_ref, direction):
    return _sig(lbg_ref[0, direction:direction + 1, :] - lbg_ref[1, direction:direction + 1, :])


N_CHUNKS = ROW_BLOCK // HGRN_CHUNK


def _block_tri(upper):
    t = np.arange(ROW_BLOCK)[:, None]
    s = np.arange(ROW_BLOCK)[None, :]
    same = (t // HGRN_CHUNK) == (s // HGRN_CHUNK)
    return jnp.asarray(same & ((s >= t) if upper else (s <= t)), dtype=BF16)


TRI_SPEC = pl.BlockSpec((ROW_BLOCK, ROW_BLOCK), lambda b, j: (0, 0))


def _tri_matmul_f32(tri, g):
    g0 = g.astype(BF16)
    r1 = g - g0.astype(F32)
    g1 = r1.astype(BF16)
    g2 = (r1 - g1.astype(F32)).astype(BF16)
    return (jnp.dot(tri, g2, preferred_element_type=F32) + jnp.dot(tri, g1, preferred_element_type=F32)) \
        + jnp.dot(tri, g0, preferred_element_type=F32)


def _chunk_rows(rows):
    return jnp.concatenate([jnp.broadcast_to(r, (HGRN_CHUNK, r.shape[1])) for r in rows], axis=0)


def _block_gates(fl, q, lb, tri, upper):
    t = {}
    t["sg"] = _sig(fl)
    t["f"] = lb + (1.0 - lb) * t["sg"]
    k = 1.0 - t["f"]
    bcum = _tri_matmul_f32(tri, jnp.log(t["f"]))
    ends = [bcum[ci * HGRN_CHUNK:ci * HGRN_CHUNK + 1] if upper else bcum[(ci + 1) * HGRN_CHUNK - 1:(ci + 1) * HGRN_CHUNK]
            for ci in range(fl.shape[0] // HGRN_CHUNK)]
    mid = _chunk_rows([0.5 * r for r in ends])
    t["dec"] = [jnp.exp(r) for r in ends]
    t["e1"] = jnp.exp(bcum - mid)
    t["e2"] = jnp.exp(mid - bcum)
    t["eh"] = _chunk_rows([jnp.exp(0.5 * r) for r in ends])
    t["qi"] = q * t["e1"]
    t["ki"] = k * t["e2"]
    t["kd"] = t["ki"] * t["eh"]
    t["qe"] = t["qi"] * t["eh"]
    return t


def _hgrn_block_order(direction, nb):
    if direction == 0:
        return lambda j: j
    return lambda j: jnp.where(j == 0, 0, nb - j)


def _hgrn_fwd(p, lbg, carry=None, then=None):
    bs, rows, _ = p.shape
    nb = rows // ROW_BLOCK
    ncb = ROW_BLOCK // HGRN_CHUNK
    orders = [_hgrn_block_order(d, nb) for d in (0, 1)]
    dirs = (0, 1)

    def body(f0_ref, i0_ref, q0_ref, f1_ref, i1_ref, q1_ref, lbg_ref, tri0_ref, tri1_ref,
             o0_ref, s0_ref, o1_ref, s1_ref, st):
        @pl.when(pl.program_id(1) == 0)
        def _():
            st[...] = jnp.zeros_like(st)

        f_refs, i_refs, q_refs = (f0_ref, f1_ref), (i0_ref, i1_ref), (q0_ref, q1_ref)
        tri_refs, o_refs, s_refs = (tri0_ref, tri1_ref), (o0_ref, o1_ref), (s0_ref, s1_ref)
        chunk = lambda a, ci, h: a[ci * HGRN_CHUNK:(ci + 1) * HGRN_CHUNK, h * HEAD_DIM:(h + 1) * HEAD_DIM]
        masks = [_tri(HGRN_CHUNK, d == 1) for d in dirs]
        t = [_block_gates(f_refs[d][0], q_refs[d][0], _lower_bound(lbg_ref, d), tri_refs[d][...], d == 1) for d in dirs]
        v = [i_refs[d][0] for d in dirs]
        intra = [[[None] * N_HEADS for _ in range(ncb)] for _ in dirs]
        ds_loc = [[[None] * N_HEADS for _ in range(ncb)] for _ in dirs]
        for ci in range(ncb):
            for h in range(N_HEADS):
                for d in dirs:
                    a = jnp.where(masks[d], _dot_nt(chunk(t[d]["qi"], ci, h), chunk(t[d]["ki"], ci, h)), 0.0)
                    intra[d][ci][h] = _dot(a, chunk(v[d], ci, h))
                    ds_loc[d][ci][h] = _dot_tn(chunk(v[d], ci, h), chunk(t[d]["kd"], ci, h))
        for h in range(N_HEADS):
            ls = slice(h * HEAD_DIM, (h + 1) * HEAD_DIM)
            s = [st[d, h] for d in dirs]
            for step in range(ncb):
                for d in dirs:
                    ci = ncb - 1 - step if d == 1 else step
                    s_refs[d][0, 0, ci, h] = s[d]
                    o_refs[d][0, ci * HGRN_CHUNK:(ci + 1) * HGRN_CHUNK, ls] = (
                        intra[d][ci][h] + _dot_nt(chunk(t[d]["qe"], ci, h), s[d]))
                    s[d] = s[d] * t[d]["dec"][ci][:, ls] + ds_loc[d][ci][h]
            for d in dirs:
                st[d, h] = s[d]

    def col(d, cb):
        return pl.BlockSpec((1, ROW_BLOCK, KW), lambda b, j: (b, orders[d](j), cb))

    def outs(d):
        return [pl.BlockSpec((1, ROW_BLOCK, KW), lambda b, j: (b, orders[d](j), 0)),
                pl.BlockSpec((1, 1, ncb, N_HEADS, HEAD_DIM, HEAD_DIM), lambda b, j: (b, orders[d](j), 0, 0, 0, 0))]

    shapes = [SDS((bs, rows, KW), F32), SDS((bs, nb, ncb, N_HEADS, HEAD_DIM, HEAD_DIM), F32)]
    return _host_call(
        body, name="hgrn_fwd", grid=(bs, nb),
        in_specs=[col(0, 0), col(0, 2), col(0, 3), col(1, 1), col(1, 2), col(1, 3),
                  pl.BlockSpec((2, 2, KW), lambda b, j: (0, 0, 0)), TRI_SPEC, TRI_SPEC],
        out_specs=outs(0) + outs(1), out_shape=shapes * 2,
        scratch_shapes=[pltpu.VMEM((2, N_HEADS, HEAD_DIM, HEAD_DIM), F32)],
        args=(p, p, p, p, p, p, lbg, _block_tri(False), _block_tri(True)), carry=carry, then=then)


def _hgrn_bwd_pair(p, lbg, s_saved, do_raw, carry=None):
    bs, rows, _ = p.shape
    nb = rows // ROW_BLOCK
    ncb = ROW_BLOCK // HGRN_CHUNK
    dirs = (0, 1)
    fwd_orders = [_hgrn_block_order(d, nb) for d in dirs]
    orders = [lambda j, d=d: fwd_orders[d](nb - 1 - j) for d in dirs]
    pairs = [(ci, h) for ci in range(ncb) for h in range(N_HEADS)]

    def body(f0_ref, i0_ref, q0_ref, s0_ref, do0_ref, f1_ref, i1_ref, q1_ref, s1_ref, do1_ref,
             lbg_ref, tril_ref, triu_ref,
             df0_ref, dq0_ref, dv0_ref, dlb0_ref, df1_ref, dq1_ref, dv1_ref, dlb1_ref, dst, acc):
        b, j = pl.program_id(0), pl.program_id(1)
        f_refs, i_refs, q_refs = (f0_ref, f1_ref), (i0_ref, i1_ref), (q0_ref, q1_ref)
        s_refs, do_refs = (s0_ref, s1_ref), (do0_ref, do1_ref)
        df_refs, dq_refs, dv_refs, dlb_refs = (df0_ref, df1_ref), (dq0_ref, dq1_ref), (dv0_ref, dv1_ref), (dlb0_ref, dlb1_ref)
        tri_refs, trit_refs = (tril_ref, triu_ref), (triu_ref, tril_ref)

        @pl.when((b == 0) & (j == 0))
        def _():
            dlb0_ref[...] = jnp.zeros_like(dlb0_ref)
            dlb1_ref[...] = jnp.zeros_like(dlb1_ref)

        @pl.when(j == 0)
        def _():
            dst[...] = jnp.zeros_like(dst)

        chunk = lambda a, ci, h: a[ci * HGRN_CHUNK:(ci + 1) * HGRN_CHUNK, h * HEAD_DIM:(h + 1) * HEAD_DIM]
        rows_of = lambda ci: slice(ci * HGRN_CHUNK, (ci + 1) * HGRN_CHUNK)
        lanes_of = lambda h: slice(h * HEAD_DIM, (h + 1) * HEAD_DIM)
        grid3 = lambda: [[[None] * N_HEADS for _ in range(ncb)] for _ in dirs]
        lbs = [_lower_bound(lbg_ref, d) for d in dirs]
        masks = [_tri(HGRN_CHUNK, d == 1) for d in dirs]
        masks_t = [_tri(HGRN_CHUNK, d != 1) for d in dirs]
        t = [_block_gates(f_refs[d][0], q_refs[d][0], lbs[d], tri_refs[d][...], d == 1) for d in dirs]
        v = [i_refs[d][0] for d in dirs]
        do = [do_refs[d][0] for d in dirs]
        a_t, da, da_t, dv_in, ds_loc = (grid3() for _ in range(5))
        for ci, h in pairs:
            for d in dirs:
                a_t[d][ci][h] = _dot_nt(chunk(t[d]["ki"], ci, h), chunk(t[d]["qi"], ci, h))
        for ci, h in pairs:
            for d in dirs:
                da[d][ci][h] = _dot_nt(chunk(do[d], ci, h), chunk(v[d], ci, h))
        for ci, h in pairs:
            for d in dirs:
                da_t[d][ci][h] = _dot_nt(chunk(v[d], ci, h), chunk(do[d], ci, h))
        for ci, h in pairs:
            for d in dirs:
                acc[d, 3, rows_of(ci), lanes_of(h)] = _dot(chunk(do[d], ci, h), s_refs[d][0, 0, ci, h])
        for ci, h in pairs:
            for d in dirs:
                ds_loc[d][ci][h] = _dot_tn(chunk(do[d], ci, h), chunk(t[d]["qe"], ci, h))
        for ci, h in pairs:
            for d in dirs:
                acc[d, 0, rows_of(ci), lanes_of(h)] = _dot(jnp.where(masks[d], da[d][ci][h], 0.0),
                                                           chunk(t[d]["ki"], ci, h))
        for ci, h in pairs:
            for d in dirs:
                acc[d, 1, rows_of(ci), lanes_of(h)] = _dot(jnp.where(masks_t[d], da_t[d][ci][h], 0.0),
                                                           chunk(t[d]["qi"], ci, h))
        for ci, h in pairs:
            for d in dirs:
                dv_in[d][ci][h] = _dot(jnp.where(masks_t[d], a_t[d][ci][h], 0.0), chunk(do[d], ci, h))
        ddec = grid3()
        for h in range(N_HEADS):
            ls = lanes_of(h)
            ds = [dst[d, h] for d in dirs]
            for step in range(ncb):
                for d in dirs:
                    ci = step if d == 1 else ncb - 1 - step
                    acc[d, 2, rows_of(ci), ls] = _dot(chunk(v[d], ci, h), ds[d])
                    acc[d, 4, rows_of(ci), ls] = dv_in[d][ci][h] + _dot_nt(chunk(t[d]["kd"], ci, h), ds[d])
                    ddec[d][ci][h] = _colsum(ds[d] * s_refs[d][0, 0, ci, h])
                    ds[d] = ds[d] * t[d]["dec"][ci][:, ls] + ds_loc[d][ci][h]
            for d in dirs:
                dst[d, h] = ds[d]
        for d in dirs:
            td = t[d]
            dqi, dki, dkd, dqe = (acc[d, i] for i in range(4))
            dq_refs[d][0] = td["e1"] * (dqi + dqe * td["eh"])
            dv_refs[d][0] = acc[d, 4]
            dk = td["e2"] * (dki + dkd * td["eh"])
            dkd_kd = dkd * td["kd"]
            db = dqi * td["qi"] - dki * td["ki"] - dkd_kd + dqe * td["qe"]
            dbl = [_colsum(dkd_kd[rows_of(ci)]) + jnp.concatenate(ddec[d][ci], axis=1) * td["dec"][ci]
                   for ci in range(ncb)]
            dg = _tri_matmul_f32(trit_refs[d][...], db) + _chunk_rows(dbl)
            df = dg / td["f"] - dk
            sg = td["sg"]
            dlb_refs[d][...] += _colsum(df * (1.0 - sg))
            df_refs[d][0] = (df * (1.0 - lbs[d]) * sg * (1.0 - sg)).astype(BF16)

    def ins(d):
        col = lambda cb: pl.BlockSpec((1, ROW_BLOCK, KW), lambda b, j: (b, orders[d](j), cb))
        return [col(d), col(2), col(3),
                pl.BlockSpec((1, 1, ncb, N_HEADS, HEAD_DIM, HEAD_DIM), lambda b, j: (b, orders[d](j), 0, 0, 0, 0)),
                pl.BlockSpec((1, ROW_BLOCK, KW), lambda b, j: (b, orders[d](j), 0))]

    def outs(d):
        row = pl.BlockSpec((1, ROW_BLOCK, KW), lambda b, j: (b, orders[d](j), 0))
        return [row, row, row, pl.BlockSpec((1, KW), lambda b, j: (0, 0))]

    shapes = [SDS((bs, rows, KW), BF16), SDS((bs, rows, KW), F32), SDS((bs, rows, KW), F32), SDS((1, KW), F32)]
    return _host_call(
        body, name="hgrn_bwd", grid=(bs, nb),
        in_specs=ins(0) + ins(1) + [pl.BlockSpec((2, 2, KW), lambda b, j: (0, 0, 0)), TRI_SPEC, TRI_SPEC],
        out_specs=outs(0) + outs(1), out_shape=shapes * 2,
        scratch_shapes=[pltpu.VMEM((2, N_HEADS, HEAD_DIM, HEAD_DIM), F32), pltpu.VMEM((2, 5, ROW_BLOCK, KW), F32)],
        args=(p, p, p, s_saved[0], do_raw, p, p, p, s_saved[1], do_raw, lbg, _block_tri(False), _block_tri(True)),
        carry=carry)


def _mix_values(og, u, v, ga, gb, o_raw, gna, lng, lnb, ws_ref, bst, wpa, wpb, wo):
    t = {}
    sog = _sig(og)
    t["sog"], t["silu_og"] = sog, og * sog
    xh_l, r_l = [], []
    for h in range(N_HEADS):
        xh, r = _rms(o_raw[:, h * HEAD_DIM:(h + 1) * HEAD_DIM])
        xh_l.append(xh)
        r_l.append(r)
    t["xh"], t["r"] = jnp.concatenate(xh_l, axis=1), r_l
    gna4 = jnp.concatenate([gna] * N_HEADS, axis=1)
    t["gna4"] = gna4
    t["o_n"] = t["xh"] * gna4
    t["o_a"] = t["o_n"] * t["silu_og"]
    t["ya"] = _dot(t["o_a"], wpa)
    t["gu"], t["tu"] = _gelu(u)
    gv, t["tv"] = _gelu(v)
    mu = jnp.mean(gv, axis=-1, keepdims=True)
    cen = gv - mu
    t["rstd"] = lax.rsqrt(jnp.mean(cen * cen, axis=-1, keepdims=True) + EPS)
    t["xhat"] = cen * t["rstd"]
    vn = t["xhat"] * lng + lnb
    t["vn"] = vn
    chunks = []
    for n in range(ROW_BLOCK // SGU_CHUNK):
        rs = slice(n * SGU_CHUNK, (n + 1) * SGU_CHUNK)
        groups = []
        for g in range(N_HEADS):
            ls = slice(g * HEAD_DIM, (g + 1) * HEAD_DIM)
            groups.append(_dot(ws_ref[g], vn[rs, ls]) + bst[:, g:g + 1])
        chunks.append(jnp.concatenate(groups, axis=1))
    t["mixed"] = jnp.concatenate(chunks, axis=0)
    t["o_bm"] = t["gu"] * t["mixed"]
    t["yb"] = _dot(t["o_bm"], wpb)
    t["sa"], t["sb"] = _sig(ga), _sig(gb)
    t["merged"] = t["sa"] * t["ya"] + t["sb"] * t["yb"]
    t["mix"] = _dot(t["merged"], wo)
    return t


def _mix_in_specs(row_of):
    def col(cb):
        return pl.BlockSpec((1, ROW_BLOCK, KW), lambda b, j: (b, row_of(j), cb))
    return [col(cb) for cb in range(4, 11)]


def _mix_param_specs():
    full2 = lambda r, c: pl.BlockSpec((r, c), lambda b, j: (0, 0))
    return [full2(1, HEAD_DIM), full2(1, KW), full2(1, KW),
            pl.BlockSpec((N_HEADS, SGU_CHUNK, SGU_CHUNK), lambda b, j: (0, 0, 0)),
            full2(SGU_CHUNK, N_HEADS), full2(KW, D_MODEL), full2(KW, D_MODEL), full2(D_MODEL, D_MODEL)]


def _mix_fwd(p, o_f, o_b, x, mod_c, gna, lng, lnb, w_s, bst, wpa, wpb, wo):
    bs, seq, _ = x.shape
    nbl = seq // ROW_BLOCK

    def body(og_r, u_r, v_r, ga0_r, ga1_r, gb0_r, gb1_r, of_r, ob_r, x_r, mod_r,
             gna_r, lng_r, lnb_r, ws_r, bst_r, wpa_r, wpb_r, wo_r, x1_r):
        ga = jnp.concatenate([ga0_r[0], ga1_r[0]], axis=1)
        gb = jnp.concatenate([gb0_r[0], gb1_r[0]], axis=1)
        t = _mix_values(og_r[0], u_r[0], v_r[0], ga, gb, of_r[0] + ob_r[0], gna_r[...], lng_r[...], lnb_r[...],
                        ws_r, bst_r[...], wpa_r[...], wpb_r[...], wo_r[...])
        x1_r[0] = x_r[0] + mod_r[0, 0:1, :] * t["mix"]

    row = lambda w: pl.BlockSpec((1, ROW_BLOCK, w), lambda b, j: (b, j + 1, 0))
    lat = pl.BlockSpec((1, ROW_BLOCK, D_MODEL), lambda b, j: (b, j, 0))
    return pl.pallas_call(
        body, name="mix_fwd", grid=(bs, nbl),
        in_specs=_mix_in_specs(lambda j: j + 1) + [row(KW), row(KW), lat,
                                                    pl.BlockSpec((1, 8, D_MODEL), lambda b, j: (b, 0, 0))]
        + _mix_param_specs(),
        out_specs=lat, out_shape=SDS(x.shape, F32), compiler_params=_params(2),
    )(p, p, p, p, p, p, p, o_f, o_b, x, mod_c, gna, lng, lnb, w_s, bst, wpa, wpb, wo)


def _mix_bwd(p, o_f, o_b, dx1, mod_c, gna, lng, lnb, w_s, w_s_t, bst, wpa, wpb, wo, carry=None):
    bs, rows, _ = p.shape
    nb = rows // ROW_BLOCK

    def body(og_r, u_r, v_r, ga0_r, ga1_r, gb0_r, gb1_r, of_r, ob_r, dx1_r, mod_r,
             gna_r, lng_r, lnb_r, ws_r, bst_r, wpa_r, wpb_r, wo_r, wst_r,
             dor_r, dpc_r, dwpa_r, dwpb_r, dwo_r, dgna_r, dlng_r, dlnb_r, dws_r, dbst_r, dmod_r):
        b, j = pl.program_id(0), pl.program_id(1)

        @pl.when((b == 0) & (j == 0))
        def _():
            for r in (dwpa_r, dwpb_r, dwo_r, dgna_r, dlng_r, dlnb_r, dws_r, dbst_r):
                r[...] = jnp.zeros_like(r)

        @pl.when(j == 0)
        def _():
            dmod_r[...] = jnp.zeros_like(dmod_r)
            dor_r[...] = jnp.zeros_like(dor_r)
            dpc_r[...] = jnp.zeros_like(dpc_r)

        @pl.when(j > 0)
        def _():
            og, u, v = og_r[0], u_r[0], v_r[0]
            ga = jnp.concatenate([ga0_r[0], ga1_r[0]], axis=1)
            gb = jnp.concatenate([gb0_r[0], gb1_r[0]], axis=1)
            gna, lng = gna_r[...], lng_r[...]
            wpa, wpb, wo = wpa_r[...], wpb_r[...], wo_r[...]
            dx1 = dx1_r[0]
            dmix = mod_r[0, 0:1, :] * dx1
            dmerged = _dot_nt(dmix, wo)
            t = _mix_values(og, u, v, ga, gb, of_r[0] + ob_r[0], gna, lng, lnb_r[...],
                            ws_r, bst_r[...], wpa, wpb, wo)
            dmod_r[0, 0:1, :] += _colsum(dx1 * t["mix"])
            dwo_r[...] += _dot_tn(t["merged"], dmix)
            sa, sb = t["sa"], t["sb"]
            dya, dyb = sa * dmerged, sb * dmerged
            dga = dmerged * t["ya"] * sa * (1.0 - sa)
            dgb = dmerged * t["yb"] * sb * (1.0 - sb)
            do_a = _dot_nt(dya, wpa)
            dwpa_r[...] += _dot_tn(t["o_a"], dya)
            do_bm = _dot_nt(dyb, wpb)
            dwpb_r[...] += _dot_tn(t["o_bm"], dyb)
            sog = t["sog"]
            dog = do_a * t["o_n"] * (sog * (1.0 + og * (1.0 - sog)))
            do_n = do_a * t["silu_og"]
            dxh = do_n * t["gna4"]
            prod = do_n * t["xh"]
            dgna = jnp.zeros((1, HEAD_DIM), F32)
            dor_l = []
            for h in range(N_HEADS):
                ls = slice(h * HEAD_DIM, (h + 1) * HEAD_DIM)
                dgna = dgna + _colsum(prod[:, ls])
                dor_l.append(_rms_bwd(dxh[:, ls], t["xh"][:, ls], t["r"][h]))
            dgna_r[...] += dgna
            dor_r[0] = jnp.concatenate(dor_l, axis=1)
            du = do_bm * t["mixed"] * _dgelu(u, t["tu"])
            dmixed = do_bm * t["gu"]
            vn = t["vn"]
            dvn_chunks = []
            for n in range(ROW_BLOCK // SGU_CHUNK):
                rs = slice(n * SGU_CHUNK, (n + 1) * SGU_CHUNK)
                groups = []
                for g in range(N_HEADS):
                    ls = slice(g * HEAD_DIM, (g + 1) * HEAD_DIM)
                    dm = dmixed[rs, ls]
                    dws_r[g] += _dot_nt(dm, vn[rs, ls])
                    dbst_r[:, g:g + 1] += jnp.sum(dm, axis=1, keepdims=True)
                    groups.append(_dot(wst_r[g], dm))
                dvn_chunks.append(jnp.concatenate(groups, axis=1))
            dvn = jnp.concatenate(dvn_chunks, axis=0)
            xhat = t["xhat"]
            dlng_r[...] += _colsum(dvn * xhat)
            dlnb_r[...] += _colsum(dvn)
            dxhat = dvn * lng
            dgv = t["rstd"] * (dxhat - jnp.mean(dxhat, axis=-1, keepdims=True)
                               - xhat * jnp.mean(dxhat * xhat, axis=-1, keepdims=True))
            dv = dgv * _dgelu(v, t["tv"])
            dpc_r[0] = jnp.concatenate([dog, du, dv, dga, dgb], axis=1).astype(BF16)

    row = lambda w: pl.BlockSpec((1, ROW_BLOCK, w), lambda b, j: (b, j, 0))
    lat = pl.BlockSpec((1, ROW_BLOCK, D_MODEL), lambda b, j: (b, jnp.maximum(j - 1, 0), 0))
    full2 = lambda r, c: pl.BlockSpec((r, c), lambda b, j: (0, 0))
    ws_spec = pl.BlockSpec((N_HEADS, SGU_CHUNK, SGU_CHUNK), lambda b, j: (0, 0, 0))
    return _host_call(
        body, name="mix_bwd", grid=(bs, nb),
        in_specs=_mix_in_specs(lambda j: j) + [row(KW), row(KW), lat,
                                                pl.BlockSpec((1, 8, D_MODEL), lambda b, j: (b, 0, 0))]
        + _mix_param_specs() + [ws_spec],
        out_specs=[row(KW), row(7 * KW), full2(KW, D_MODEL), full2(KW, D_MODEL), full2(D_MODEL, D_MODEL),
                   full2(1, HEAD_DIM), full2(1, KW), full2(1, KW), ws_spec, full2(SGU_CHUNK, N_HEADS),
                   pl.BlockSpec((1, 8, D_MODEL), lambda b, j: (b, 0, 0))],
        out_shape=[SDS((bs, rows, KW), F32), SDS((bs, rows, 7 * KW), BF16), SDS((KW, D_MODEL), F32),
                   SDS((KW, D_MODEL), F32), SDS((D_MODEL, D_MODEL), F32), SDS((1, HEAD_DIM), F32),
                   SDS((1, KW), F32), SDS((1, KW), F32), SDS((N_HEADS, SGU_CHUNK, SGU_CHUNK), F32),
                   SDS((SGU_CHUNK, N_HEADS), F32), SDS((bs, 8, D_MODEL), F32)],
        args=(p, p, p, p, p, p, p, o_f, o_b, dx1, mod_c, gna, lng, lnb, w_s, bst, wpa, wpb, wo, w_s_t), carry=carry)


def _ffn(x1, target, mod_c, g_ffn, g_final, w_up, w_down):
    bs, seq, _ = x1.shape
    nbl = seq // ROW_BLOCK

    def body(x1_r, tg_r, mod_r, gf_r, gl_r, wu_r, wd_r,
             dx1_r, h2_r, dab_r, hid_r, dffn_r, loss_r, dgl_r, dgf_r, dmod_r):
        b, j = pl.program_id(0), pl.program_id(1)

        @pl.when((b == 0) & (j == 0))
        def _():
            for r in (loss_r, dgl_r, dgf_r):
                r[...] = jnp.zeros_like(r)

        @pl.when(j == 0)
        def _():
            dmod_r[...] = jnp.zeros_like(dmod_r)

        x1 = x1_r[0]
        shift, scale, gate = mod_r[0, 1:2, :], mod_r[0, 2:3, :], mod_r[0, 3:4, :]
        gf, gl = gf_r[...], gl_r[...]
        xn2, r2 = _rms(x1)
        hn2 = xn2 * gf
        h2 = (hn2 * (1.0 + scale) + shift).astype(BF16)
        h2_r[0] = h2
        ab = jnp.dot(h2, wu_r[...], preferred_element_type=F32)
        a, bb = ab[:, :D_FF], ab[:, D_FF:]
        sa = _sig(a)
        silu_a = a * sa
        hid = (silu_a * bb).astype(BF16)
        hid_r[0] = hid
        ffn = jnp.dot(hid, wd_r[...], preferred_element_type=F32)
        x2 = x1 + gate * ffn
        xn3, r3 = _rms(x2)
        err = xn3 * gl - tg_r[0]
        loss_r[...] += 0.5 * jnp.sum(jnp.mean(err * err, axis=-1, keepdims=True), axis=0, keepdims=True)
        dy = err * (1.0 / D_MODEL)
        dgl_r[...] += _colsum(dy * xn3)
        dx2 = _rms_bwd(dy * gl, xn3, r3)
        dmod_r[0, 3:4, :] += _colsum(dx2 * ffn)
        dffn = (gate * dx2).astype(BF16)
        dffn_r[0] = dffn
        dhid = lax.dot_general(dffn, wd_r[...], (((1,), (1,)), ((), ())), preferred_element_type=F32)
        da = dhid * bb * (sa * (1.0 + a * (1.0 - sa)))
        db = dhid * silu_a
        dab = jnp.concatenate([da, db], axis=1).astype(BF16)
        dab_r[0] = dab
        dh2 = lax.dot_general(dab, wu_r[...], (((1,), (1,)), ((), ())), preferred_element_type=F32)
        dmod_r[0, 1:2, :] += _colsum(dh2)
        dmod_r[0, 2:3, :] += _colsum(dh2 * hn2)
        dhn2 = dh2 * (1.0 + scale)
        dgf_r[...] += _colsum(dhn2 * xn2)
        dx1_r[0] = dx2 + _rms_bwd(dhn2 * gf, xn2, r2)

    lat = lambda w: pl.BlockSpec((1, ROW_BLOCK, w), lambda b, j: (b, j, 0))
    full2 = lambda r, c: pl.BlockSpec((r, c), lambda b, j: (0, 0))
    mod_spec = pl.BlockSpec((1, 8, D_MODEL), lambda b, j: (b, 0, 0))
    return pl.pallas_call(
        body, name="ffn", grid=(bs, nbl),
        in_specs=[lat(D_MODEL), lat(D_MODEL), mod_spec, full2(1, D_MODEL), full2(1, D_MODEL),
                  full2(D_MODEL, 2 * D_FF), full2(D_FF, D_MODEL)],
        out_specs=[lat(D_MODEL), lat(D_MODEL), lat(2 * D_FF), lat(D_FF), lat(D_MODEL),
                   full2(1, 1), full2(1, D_MODEL), full2(1, D_MODEL), mod_spec],
        out_shape=[SDS(x1.shape, F32), SDS(x1.shape, BF16), SDS((bs, seq, 2 * D_FF), BF16),
                   SDS((bs, seq, D_FF), BF16), SDS(x1.shape, BF16), SDS((1, 1), F32),
                   SDS((1, D_MODEL), F32), SDS((1, D_MODEL), F32), SDS((bs, 8, D_MODEL), F32)],
        compiler_params=_params(2),
    )(x1, target, mod_c, g_ffn, g_final, w_up, w_down)


def _row_tile(rows):
    return next(m * ROW_BLOCK for m in (4, 2, 1) if rows % (m * ROW_BLOCK) == 0)


def _matmul_tn(a, b, n_blocks, tk, name, carry=None):
    t, m = a.shape
    n = b.shape[1]
    tn = n // n_blocks

    def body(a_ref, b_ref, o_ref):
        @pl.when(pl.program_id(1) == 0)
        def _():
            o_ref[...] = jnp.zeros_like(o_ref)
        o_ref[0] += _dot_tn(a_ref[...], b_ref[...])

    (out,), carried = _host_call(
        body, name=name, grid=(n_blocks, t // tk),
        in_specs=[pl.BlockSpec((tk, m), lambda i, k: (k, 0)), pl.BlockSpec((tk, tn), lambda i, k: (k, i))],
        out_specs=[pl.BlockSpec((1, m, tn), lambda i, k: (i, 0, 0))],
        out_shape=[SDS((n_blocks, m, tn), F32)], args=(a, b), carry=carry)
    return out if carry is None else (out, carried)


SMALL_ROWS = 80
ROW_CCTX = 3


def _small_reduce(gathered, lbg):
    def body(g_ref, lbg_ref, s_ref, dgam_ref):
        tot = g_ref[0:SMALL_ROWS, :]
        for dev in range(1, N_DEV):
            tot = tot + g_ref[dev * SMALL_ROWS:(dev + 1) * SMALL_ROWS, :]
        s_ref[...] = tot
        cc = g_ref[ROW_CCTX:ROW_CCTX + 1, :]
        for dev in range(2, N_DEV, 2):
            cc = cc + g_ref[dev * SMALL_ROWS + ROW_CCTX:dev * SMALL_ROWS + ROW_CCTX + 1, :]
        s_ref[ROW_CCTX:ROW_CCTX + 1, :] = cc
        dlb = tot[7:8, :]
        for d in range(2):
            s0 = _sig(lbg_ref[0, d:d + 1, :] - lbg_ref[1, d:d + 1, :])
            dgam_ref[d:d + 1, :] = dlb[:, d * KW:(d + 1) * KW] * s0 * (1.0 - s0)

    return pl.pallas_call(
        body, name="small_reduce", out_shape=[SDS((SMALL_ROWS, D_MODEL), F32), SDS((2, KW), F32)],
        in_specs=[VMEM_SPEC] * 2, out_specs=[VMEM_SPEC] * 2,
    )(gathered, lbg)


def _pad_cols(a, width):
    return jnp.pad(a, ((0, 0), (0, width - a.shape[1])))


def kernel(x, c, ctx, c_ctx, w_mod, b_mod, g_mix, g_ffn, w_in, lb_gamma, g_norm_a, ln_v_g, ln_v_b, w_s, b_s, w_pa, w_pb, w_o, w_up, w_down, g_final, loss_target, m_c_ctx, m_w_mod, m_b_mod, m_g_mix, m_g_ffn, m_w_in, m_lb_gamma, m_g_norm_a, m_ln_v_g, m_ln_v_b, m_w_s, m_b_s, m_w_pa, m_w_pb, m_w_o, m_w_up, m_w_down, m_g_final, v_c_ctx, v_w_mod, v_b_mod, v_g_mix, v_g_ffn, v_w_in, v_lb_gamma, v_g_norm_a, v_ln_v_g, v_ln_v_b, v_w_s, v_b_s, v_w_pa, v_w_pb, v_w_o, v_w_up, v_w_down, v_g_final):
    ax, ay, ac = lax.axis_index("x"), lax.axis_index("y"), lax.axis_index("c")
    kc = 2 * ax + ay
    dev = 2 * kc + ac
    pos = jnp.stack([kc, ac]).astype(jnp.int32)
    bs, seq, _ = x.shape
    assert bs <= 4 and ctx.shape[1] == ROW_BLOCK and seq % ROW_BLOCK == 0
    mod_cols = w_mod.shape[2]

    lbg_row = _pad_cols(lb_gamma.reshape(1, -1), D_MODEL)
    pay1 = jnp.concatenate([c, jnp.zeros((4 - bs, D_MODEL), F32), c_ctx[None, :], lbg_row,
                            jnp.zeros((2, D_MODEL), F32)], axis=0)
    cond64 = _all_gather8(pay1, "gather_cond")
    lbg_full = cond64.reshape(N_DEV, 8, D_MODEL)[0::2, 5, :KW].reshape(N_CHIPS, 2, 2, HEAD_DIM)
    lbg_full = jnp.transpose(lbg_full, (1, 2, 0, 3)).reshape(2, 2, KW)

    b_mod_s = lax.dynamic_slice(b_mod, (0, kc * mod_cols), (1, mod_cols))
    mod_s = _mod_fwd(cond64, w_mod[0], b_mod_s)
    shards = [w_in[0], w_up[0], w_pa[0], w_pb[0], w_o[0], w_down[0]]
    bufs = _cast_bf16(pos, shards)
    mod_g = _all_gather8(mod_s, "gather_mod").reshape(N_DEV, 64, mod_cols)[0::2]
    mod_full = jnp.transpose(mod_g, (1, 0, 2)).reshape(64, N_CHIPS * mod_cols)
    mod_mine = lax.dynamic_slice(mod_full, (dev * 8, 0), (8, 6 * D_MODEL)).reshape(8, 6, D_MODEL)
    mod, mc = mod_mine[:bs], mod_mine[4]
    zeros4 = jnp.zeros((bs, 4, D_MODEL), F32)
    mod_a = jnp.concatenate([mod[:, 0:2], jnp.broadcast_to(mc[None, 0:2], (bs, 2, D_MODEL)), zeros4], axis=1)
    mod_c = jnp.concatenate([mod[:, 2:6], zeros4], axis=1)

    def cols_major(a):
        return jnp.transpose(a, (1, 0, 2)).reshape(a.shape[1], -1)

    gna, lng, lnb = g_norm_a, ln_v_g, ln_v_b
    ws3 = w_s[0]
    ws3_t = jnp.transpose(ws3, (0, 2, 1))
    bst = jnp.transpose(b_s[0])

    (p, h_all), (w_in_g,) = _in_fwd_own(pos, x, ctx, g_mix, mod_a, w_in[0], carry=_carry_gather_send(bufs[:1]),
                                        then=_carry_gather_forward(bufs[:1]))
    p, sent_up = _in_fwd_rest(pos, h_all, w_in_g, p, carry=_carry_gather_send(bufs[1:2]))
    w_in_f = cols_major(w_in_g)
    fwd_rest = _carry_gather_forward(bufs[2:])
    then_rest = _Carry([], [], {}, fwd_rest.sems, lambda i, o, s: fwd_rest.copies(i[1:], o[1:], s))
    (o_f, s_f, o_b, s_b), gathered = _hgrn_fwd(
        p, lbg_full, carry=_merge_carries(_carry_gather_forward(sent_up), _carry_gather_send(bufs[2:])),
        then=then_rest)
    w_up_f, w_pa_f, w_pb_f = (cols_major(a) for a in gathered[:3])
    w_o_f = gathered[3].reshape(-1, D_MODEL)
    w_down_f = gathered[4].reshape(-1, D_MODEL)
    x1 = _mix_fwd(p, o_f, o_b, x, mod_c, gna, lng, lnb, ws3, bst, w_pa_f, w_pb_f, w_o_f)
    dx1, h2, dab, hid, dffn, loss_part, dg_final, dg_ffn, dmod_ffn = _ffn(
        x1, loss_target, mod_c, g_ffn, g_final[None, :], w_up_f, w_down_f)
    rows_lat = bs * seq
    tk_lat = _row_tile(rows_lat)
    dw_up = _matmul_tn(h2.reshape(rows_lat, D_MODEL), dab.reshape(rows_lat, 2 * D_FF), N_CHIPS, tk_lat, "dw_up")
    dw_down = _matmul_tn(hid.reshape(rows_lat, D_FF), dffn.reshape(rows_lat, D_MODEL), 1, tk_lat, "dw_down")

    def shard_major(a):
        return jnp.transpose(a.reshape(a.shape[0], N_CHIPS, -1), (1, 0, 2))

    def add_halves(parts, recvs, names):
        sums = [_rs_add_halves(pos, g, r, "rs_add_" + nm) for g, r, nm in zip(parts, recvs, names)]
        return [s[0] for s in sums], [s[1] for s in sums]

    def sum_owner(cp32s, recvs, names):
        return [_rs_sum_owner(pos, a, r, "rs_sum_" + nm) for a, r, nm in zip(cp32s, recvs, names)]

    ffn_names, mix_names = ["w_up", "w_down"], ["w_pa", "w_pb", "w_o"]
    part_ffn = [dw_up, dw_down.reshape(N_CHIPS, -1, D_MODEL)]
    (do_raw, dpc, dw_pa, dw_pb, dw_o, dgna, dlng, dlnb, dws, dbst, dmod_mix), sib_ffn = _mix_bwd(
        p, o_f, o_b, dx1, mod_c, gna, lng, lnb, ws3, ws3_t, bst, w_pa_f, w_pb_f, w_o_f,
        carry=_carry_sibling_halves(part_ffn))
    cp32_ffn, cpbf_ffn = add_halves(part_ffn, sib_ffn, ffn_names)
    part_mix = [shard_major(dw_pa), shard_major(dw_pb), dw_o.reshape(N_CHIPS, -1, D_MODEL)]
    (df_f, dq_f, dv_f, dlb0, df_b, dq_b, dv_b, dlb1), got = _hgrn_bwd_pair(
        p, lbg_full, (s_f, s_b), do_raw,
        carry=_merge_carries(_carry_to_owner(cpbf_ffn), _carry_sibling_halves(part_mix)))
    own_ffn, sib_mix = got[:2], got[2:]
    half_ffn = sum_owner(cp32_ffn, own_ffn, ffn_names)
    cp32_mix, cpbf_mix = add_halves(part_mix, sib_mix, mix_names)
    (grad_x, dp, dg_mix, dmod_in), _ = _in_bwd(x, ctx, dx1, g_mix, mod_a, w_in_f, df_f, df_b, dv_f, dv_b, dq_f, dq_b,
                                               dpc)

    rows_all = dp.shape[0] * dp.shape[1]
    tk_all = _row_tile(rows_all)
    dw_in, got = _matmul_tn(h_all.reshape(rows_all, D_MODEL), dp.reshape(rows_all, IN_COLS), N_CHIPS, tk_all, "dw_in",
                            carry=_merge_carries(_carry_join_halves(half_ffn), _carry_to_owner(cpbf_mix)))
    g_ffn_w, own_mix = got[:2], got[2:]
    half_mix = sum_owner(cp32_mix, own_mix, mix_names)

    dmod_mine = jnp.concatenate([dmod_in[:, 0], dmod_in[:, 1], dmod_mix[:, 0], dmod_ffn[:, 1], dmod_ffn[:, 2],
                                 dmod_ffn[:, 3]], axis=1)
    dmc = jnp.concatenate([jnp.sum(dmod_in[:, 2], axis=0), jnp.sum(dmod_in[:, 3], axis=0),
                           jnp.zeros((4 * D_MODEL,), F32)])[None, :]
    pay3 = jnp.concatenate([dmod_mine, jnp.zeros((4 - bs, 6 * D_MODEL), F32), dmc,
                            jnp.zeros((3, 6 * D_MODEL), F32)], axis=0)
    dmod64, got = _all_gather8(pay3, "gather_dmod", carry=_merge_carries(_carry_sibling_halves([dw_in]),
                                                                          _carry_join_halves(half_mix)))
    sib_in, g_mix_w = got[:1], got[1:]
    cp32_in, cpbf_in = add_halves([dw_in], sib_in, ["w_in"])
    dmod64_my = lax.dynamic_slice(dmod64, (0, kc * mod_cols), (64, mod_cols))
    g_w_mod, g_b_mod, g_cctx_part = _mod_bwd(cond64, dmod64, dmod64_my, w_mod[0], c_ctx[None, :])

    def row(*parts):
        return _pad_cols(jnp.concatenate([q.reshape(1, -1) for q in parts], axis=1), D_MODEL)

    small_rows = [dg_mix, dg_ffn, dg_final, g_cctx_part, row(dgna), row(dlng, dlnb), row(jnp.transpose(dbst)),
                  row(dlb0, dlb1), row(loss_part), jnp.zeros((7, D_MODEL), F32), dws.reshape(64, D_MODEL)]
    pay4 = jnp.concatenate(small_rows, axis=0)
    tot, dgam0 = _small_reduce(_all_gather8(pay4, "gather_small"), lbg_full)

    own_sems, own_src, own_land, own_token = _owner_send_start(cpbf_in[0], after=tot)

    rest_names = ["w_up", "w_pa", "w_pb", "w_o", "w_down", "w_mod"]
    rest_w = shards[1:] + [w_mod[0]]
    rest_g = [g_ffn_w[0], g_mix_w[0] + own_token[0, 0], g_mix_w[1], g_mix_w[2], g_ffn_w[1], g_w_mod]
    rest_m = [m_w_up[0], m_w_pa[0], m_w_pb[0], m_w_o[0], m_w_down[0], m_w_mod[0]]
    rest_v = [v_w_up[0], v_w_pa[0], v_w_pb[0], v_w_o[0], v_w_down[0], v_w_mod[0]]
    (ds_r, m2s_r, v2s_r), _ = _adamw_group(rest_w, rest_g, rest_m, rest_v, "adamw_rest")
    res = {}
    for name, g, d, m2, v2 in zip(rest_names, rest_g, ds_r, m2s_r, v2s_r):
        res[name] = (g[None], d[None], m2[None], v2[None])
    own_in = [_owner_send_wait(own_sems, own_src, own_land, after=(ds_r[0],))]
    g_in_w = _comm_call("rs_join_w_in", _carry_join_halves(sum_owner(cp32_in, own_in, ["w_in"])))
    d, m2, v2 = _adamw_big(shards[0], g_in_w[0], m_w_in[0], v_w_in[0], "adamw_w_in")
    res["w_in"] = (g_in_w[0][None], d[None], m2[None], v2[None])

    loss = tot[8, 0]
    dgam_full = jnp.stack([dgam0, -dgam0])
    g_lbg = lax.dynamic_slice(dgam_full, (0, 0, kc * HEAD_DIM), (2, 2, HEAD_DIM))

    small = [
        ("c_ctx", c_ctx[None, :], tot[3:4], m_c_ctx, v_c_ctx),
        ("b_mod", b_mod, g_b_mod, m_b_mod, v_b_mod),
        ("g_mix", g_mix, tot[0:1], m_g_mix, v_g_mix),
        ("g_ffn", g_ffn, tot[1:2], m_g_ffn, v_g_ffn),
        ("lb_gamma", lb_gamma.reshape(4, HEAD_DIM), g_lbg.reshape(4, HEAD_DIM), m_lb_gamma, v_lb_gamma),
        ("g_norm_a", g_norm_a, tot[4:5, :HEAD_DIM], m_g_norm_a, v_g_norm_a),
        ("ln_v_g", ln_v_g, tot[5:6, :KW], m_ln_v_g, v_ln_v_g),
        ("ln_v_b", ln_v_b, tot[5:6, KW:], m_ln_v_b, v_ln_v_b),
        ("w_s", w_s.reshape(N_HEADS * SGU_CHUNK, SGU_CHUNK), tot[16:80].reshape(N_HEADS * SGU_CHUNK, SGU_CHUNK),
         m_w_s, v_w_s),
        ("b_s", b_s[0], tot[6:7, :KW].reshape(N_HEADS, SGU_CHUNK), m_b_s, v_b_s),
        ("g_final", g_final[None, :], tot[2:3], m_g_final, v_g_final),
    ]
    ws_, gs_ = [s[1] for s in small], [s[2] for s in small]
    ms_ = [s[3].reshape(s[1].shape) for s in small]
    vs_ = [s[4].reshape(s[1].shape) for s in small]
    ds_, m2s_, v2s_ = _adamw_small(ws_, gs_, ms_, vs_)
    for (name, _, g, m, _), d, m2, v2 in zip(small, ds_, m2s_, v2s_):
        res[name] = tuple(t.reshape(m.shape) for t in (g, d, m2, v2))

    order = ["c_ctx", "w_mod", "b_mod", "g_mix", "g_ffn", "w_in", "lb_gamma", "g_norm_a", "ln_v_g", "ln_v_b",
             "w_s", "b_s", "w_pa", "w_pb", "w_o", "w_up", "w_down", "g_final"]
    outs = [loss, grad_x]
    for part in range(4):
        outs += [res[n][part] for n in order]
    return tuple(outs)
```

```python
import functools
import math

import jax
import jax.numpy as jnp
import numpy as np
from jax import lax
from jax.experimental import pallas as pl
from jax.experimental.pallas import tpu as pltpu

F32 = jnp.float32
BF16 = jnp.bfloat16
SDS = jax.ShapeDtypeStruct
MESH = pl.DeviceIdType.MESH

EPS = 1e-6
D_MODEL = 1024
N_HEADS = 4
HEAD_DIM = 128
KW = N_HEADS * HEAD_DIM
IN_COLS = 11 * KW
D_FF = 2816
HGRN_CHUNK = 64
SGU_CHUNK = 128
ROW_BLOCK = 256
N_CHIPS = 4
N_DEV = 8
V7X_VMEM_BYTES = 64 * 1024 * 1024
VMEM_LIMIT = V7X_VMEM_BYTES - 6 * 1024 * 1024

ADAM_LR, ADAM_B1, ADAM_B2, ADAM_EPS, ADAM_WD, ADAM_STEP = 0.001, 0.9, 0.999, 1e-08, 0.01, 10
GELU_C0 = math.sqrt(2.0 / math.pi)
GELU_C1 = 0.044715

VMEM_SPEC = pl.BlockSpec(memory_space=pltpu.VMEM)
ANY_SPEC = pl.BlockSpec(memory_space=pl.ANY)


def _params(n_grid):
    return pltpu.CompilerParams(dimension_semantics=("arbitrary",) * n_grid, vmem_limit_bytes=VMEM_LIMIT)


def _sig(x):
    return 0.5 * jnp.tanh(0.5 * x) + 0.5


def _gelu(x):
    t = jnp.tanh(GELU_C0 * (x + GELU_C1 * x * x * x))
    return 0.5 * x * (1.0 + t), t


def _dgelu(x, t):
    return 0.5 * (1.0 + t) + 0.5 * x * (1.0 - t * t) * GELU_C0 * (1.0 + 3.0 * GELU_C1 * x * x)


def _dot(a, b):
    return jnp.dot(a.astype(BF16), b.astype(BF16), preferred_element_type=F32)


def _dot_nt(a, b):
    return lax.dot_general(a.astype(BF16), b.astype(BF16), (((1,), (1,)), ((), ())), preferred_element_type=F32)


def _dot_tn(a, b):
    return lax.dot_general(a.astype(BF16), b.astype(BF16), (((0,), (0,)), ((), ())), preferred_element_type=F32)


def _dot_f32(a, b, dims=(((1,), (0,)), ((), ()))):
    return lax.dot_general(a, b, dims, precision=lax.Precision.HIGHEST, preferred_element_type=F32)


def _rms(x):
    r = lax.rsqrt(jnp.mean(x * x, axis=-1, keepdims=True) + EPS)
    return x * r, r


def _rms_bwd(dxn, xn, r):
    return r * (dxn - xn * jnp.mean(dxn * xn, axis=-1, keepdims=True))


def _colsum(a):
    return jnp.sum(a, axis=0, keepdims=True)


def _tri(n, upper):
    t = lax.broadcasted_iota(jnp.int32, (n, n), 0)
    s = lax.broadcasted_iota(jnp.int32, (n, n), 1)
    return (s >= t) if upper else (s <= t)


def _all_gather8(x_shard, name, carry=None, then=None):
    m_per, n = x_shard.shape
    n_ci = len(carry.ins) if carry else 0
    n_co = len(carry.outs) if carry else 0
    n_cs = len(carry.sems) if carry else 0

    def body(*refs):
        x_ref, cins = refs[0], refs[1:1 + n_ci]
        out_ref, couts = refs[1 + n_ci], refs[2 + n_ci:2 + n_ci + n_co]
        send_sems, recv_sems, local_sem = refs[2 + n_ci + n_co:5 + n_ci + n_co]
        csems = refs[5 + n_ci + n_co:5 + n_ci + n_co + n_cs]
        tsems = refs[5 + n_ci + n_co + n_cs:]
        if carry:
            _start_all(carry.copies(cins, couts, csems)[0])
        _gather8_body(x_ref, out_ref, send_sems, recv_sems, local_sem, m_per)
        if carry:
            _wait_all(carry.copies(cins, couts, csems)[1])
        if then:
            _start_all(then.copies(couts, couts, tsems)[0])
            _wait_all(then.copies(couts, couts, tsems)[1])

    res = pl.pallas_call(
        body, name=name, out_shape=[SDS((N_DEV * m_per, n), x_shard.dtype)] + (carry.outs if carry else []),
        in_specs=[VMEM_SPEC] + [ANY_SPEC] * n_ci, out_specs=[VMEM_SPEC] + [ANY_SPEC] * n_co,
        input_output_aliases={1 + i: 1 + o for i, o in carry.alias.items()} if carry else {},
        scratch_shapes=[pltpu.SemaphoreType.DMA((7,)), pltpu.SemaphoreType.DMA((7,)), pltpu.SemaphoreType.DMA]
        + (carry.sems if carry else []) + (then.sems if then else []),
    )(x_shard, *(carry.ins if carry else []))
    return res[0] if carry is None else (res[0], list(res[1:]))


def _gather8_body(x_ref, out_ref, send_sems, recv_sems, local_sem, m_per):
    x, y, c = lax.axis_index("x"), lax.axis_index("y"), lax.axis_index("c")
    me, sibling = (x, y, c), (x, y, 1 - c)
    chips = [(1 - x, y), (x, 1 - y), (1 - x, 1 - y)]

    def rows(px, py, pc):
        return out_ref.at[pl.ds((4 * px + 2 * py + pc) * m_per, m_per), :]

    def copy(k, block, to, src=None):
        return pltpu.make_async_remote_copy(
            src_ref=rows(*block) if src is None else src, dst_ref=rows(*block),
            send_sem=send_sems.at[k], recv_sem=recv_sems.at[k], device_id=to, device_id_type=MESH)

    mine = pltpu.make_async_copy(x_ref, rows(*me), local_sem)
    mine.start()
    first = [copy(0, me, sibling, src=x_ref)]
    first += [copy(1 + j, me, (*chip, c), src=x_ref) for j, chip in enumerate(chips)]
    for cp in first:
        cp.start()
    passed = [copy(4 + j, (*chip, c), sibling) for j, chip in enumerate(chips)]
    for j, chip in enumerate(chips):
        copy(1 + j, (*chip, c), me).wait_recv()
        passed[j].start()
    copy(0, sibling, me).wait_recv()
    for j, chip in enumerate(chips):
        copy(4 + j, (*chip, 1 - c), me).wait_recv()
    for cp in first + passed:
        cp.wait_send()
    mine.wait()


def _mesh_pos():
    x, y, c = lax.axis_index("x"), lax.axis_index("y"), lax.axis_index("c")
    chips = [(1 - x, y), (x, 1 - y), (1 - x, 1 - y)]
    return x, y, c, 2 * x + y, (x, y, 1 - c), chips


def _half_rows(c, rh):
    return pl.ds(pl.multiple_of(c * rh, 16), rh)


class _Carry:
    def __init__(self, ins, outs, alias, sems, copies):
        self.ins, self.outs, self.alias, self.sems, self.copies = list(ins), list(outs), dict(alias), list(sems), copies


def _remote(src, dst, send, recv, to):
    return functools.partial(pltpu.make_async_remote_copy, src_ref=src, dst_ref=dst, send_sem=send, recv_sem=recv,
                             device_id=to, device_id_type=MESH)


def _carry_gather_send(bufs):
    n = len(bufs)

    def copies(ins, outs, sems):
        x, y, c, kc, sibling, chips = _mesh_pos()
        starts, waits = [], []
        for wi in range(n):
            rh = outs[wi].shape[1] // 2
            for jj, chip in enumerate(chips):
                mine = outs[wi].at[kc, _half_rows(c, rh), :]
                cp = _remote(mine, mine, sems[0].at[wi, jj], sems[1].at[wi, jj], (*chip, c))
                starts.append(cp)
                waits.append((cp, "send"))
                theirs = outs[wi].at[2 * chip[0] + chip[1], _half_rows(c, rh), :]
                waits.append((_remote(theirs, theirs, sems[0].at[wi, jj], sems[1].at[wi, jj], (*chip, c)), "recv"))
        return starts, waits

    return _Carry(bufs, [SDS(b.shape, b.dtype) for b in bufs], {i: i for i in range(n)},
                  [pltpu.SemaphoreType.DMA((n, 3)), pltpu.SemaphoreType.DMA((n, 3))], copies)


def _carry_gather_forward(bufs):
    n = len(bufs)

    def copies(ins, outs, sems):
        x, y, c, kc, sibling, chips = _mesh_pos()
        starts, waits = [], []
        for wi in range(n):
            rh = outs[wi].shape[1] // 2
            for jj, chip in enumerate(chips):
                got = outs[wi].at[2 * chip[0] + chip[1], _half_rows(c, rh), :]
                cp = _remote(got, got, sems[0].at[wi, jj], sems[1].at[wi, jj], sibling)
                starts.append(cp)
                waits.append((cp, "send"))
                other = outs[wi].at[2 * chip[0] + chip[1], _half_rows(1 - c, rh), :]
                waits.append((_remote(other, other, sems[0].at[wi, jj], sems[1].at[wi, jj], sibling), "recv"))
        return starts, waits

    return _Carry(bufs, [SDS(b.shape, b.dtype) for b in bufs], {i: i for i in range(n)},
                  [pltpu.SemaphoreType.DMA((n, 3)), pltpu.SemaphoreType.DMA((n, 3))], copies)


def _carry_sibling_halves(grads):
    n = len(grads)

    def copies(ins, outs, sems):
        x, y, c, kc, sibling, chips = _mesh_pos()
        cps = [_remote(ins[wi].at[:, _half_rows(1 - c, ins[wi].shape[1] // 2), :], outs[wi],
                       sems[0].at[wi], sems[1].at[wi], sibling) for wi in range(n)]
        return cps, [(cp, "both") for cp in cps]

    return _Carry(grads, [SDS((N_CHIPS, g.shape[1] // 2, g.shape[2]), F32) for g in grads], {},
                  [pltpu.SemaphoreType.DMA((n,)), pltpu.SemaphoreType.DMA((n,))], copies)


def _carry_to_owner(cpbfs):
    n = len(cpbfs)

    def copies(ins, outs, sems):
        x, y, c, kc, sibling, chips = _mesh_pos()
        starts, waits = [], []
        for wi in range(n):
            for jj, chip in enumerate(chips):
                cp = _remote(ins[wi].at[2 * chip[0] + chip[1]], outs[wi].at[kc],
                             sems[0].at[wi, jj], sems[1].at[wi, jj], (*chip, c))
                starts.append(cp)
                waits.append((cp, "send"))
                slot = outs[wi].at[2 * chip[0] + chip[1]]
                waits.append((_remote(slot, slot, sems[0].at[wi, jj], sems[1].at[wi, jj], (*chip, c)), "recv"))
        return starts, waits

    return _Carry(cpbfs, [SDS(g.shape, BF16) for g in cpbfs], {},
                  [pltpu.SemaphoreType.DMA((n, 3)), pltpu.SemaphoreType.DMA((n, 3))], copies)


def _carry_join_halves(bufs):
    n = len(bufs)

    def copies(ins, outs, sems):
        x, y, c, kc, sibling, chips = _mesh_pos()
        cps = []
        for wi in range(n):
            mine = outs[wi].at[_half_rows(c, outs[wi].shape[0] // 2), :]
            cps.append(_remote(mine, mine, sems[0].at[wi], sems[1].at[wi], sibling))
        return cps, [(cp, "both") for cp in cps]

    return _Carry(bufs, [SDS(b.shape, F32) for b in bufs], {i: i for i in range(n)},
                  [pltpu.SemaphoreType.DMA((n,)), pltpu.SemaphoreType.DMA((n,))], copies)


def _merge_carries(*carries):
    ins, outs, alias, sems, parts = [], [], {}, [], []
    for cy in carries:
        parts.append((len(ins), len(cy.ins), len(outs), len(cy.outs), len(sems), len(cy.sems), cy.copies))
        alias.update({len(ins) + i: len(outs) + o for i, o in cy.alias.items()})
        ins += cy.ins
        outs += cy.outs
        sems += cy.sems

    def copies(i, o, s):
        starts, waits = [], []
        for i0, ni, o0, no, s0, ns, fn in parts:
            st, wt = fn(i[i0:i0 + ni], o[o0:o0 + no], s[s0:s0 + ns])
            starts += st
            waits += wt
        return starts, waits

    return _Carry(ins, outs, alias, sems, copies)


def _start_all(starts):
    for cp in starts:
        cp().start()


def _wait_all(waits):
    for cp, which in waits:
        if which == "send":
            cp().wait_send()
        elif which == "recv":
            cp().wait_recv()
        else:
            cp().wait()


HBM_SPEC = pl.BlockSpec(memory_space=pltpu.HBM)
SEM_SPEC = pl.BlockSpec(memory_space=pltpu.SEMAPHORE)
SPLIT_COPY_EFFECT = pltpu.SideEffectType.DATAFLOW_SIDE_EFFECTING


def _owner_send_start(cpbf, after):
    land = lax.empty(cpbf.shape, cpbf.dtype)

    def body(src_ref, land_ref, after_ref, s0, s1, s2, r0, r1, r2, src_thru, land_thru, token):
        x, y, c, kc, sibling, chips = _mesh_pos()
        for jj, (chip, s_sem, r_sem) in enumerate(zip(chips, (s0, s1, s2), (r0, r1, r2))):
            pltpu.make_async_remote_copy(
                src_ref=src_ref.at[2 * chip[0] + chip[1]], dst_ref=land_ref.at[kc], send_sem=s_sem, recv_sem=r_sem,
                device_id=(*chip, c), device_id_type=MESH).start()
        token[...] = jnp.zeros_like(token)

    buf = pltpu.HBM(cpbf.shape, cpbf.dtype)
    outs = pl.pallas_call(
        body, name="rs_owner_w_in_start",
        out_shape=(pltpu.SemaphoreType.DMA(()),) * 6 + (buf, buf, SDS((8, 128), F32)),
        in_specs=(HBM_SPEC, HBM_SPEC, ANY_SPEC), out_specs=(SEM_SPEC,) * 6 + (HBM_SPEC, HBM_SPEC, VMEM_SPEC),
        input_output_aliases={0: 6, 1: 7},
        compiler_params=pltpu.CompilerParams(has_side_effects=SPLIT_COPY_EFFECT),
    )(pltpu.with_memory_space_constraint(cpbf, pltpu.HBM), pltpu.with_memory_space_constraint(land, pltpu.HBM), after)
    return outs[:6], outs[6], outs[7], outs[8]


def _owner_send_wait(sems, src_thru, land_thru, after):
    n_after = len(after)

    def body(*refs):
        src_ref, land_ref = refs[0], refs[1]
        sends, recvs = refs[2:5], refs[5:8]
        x, y, c, kc, sibling, chips = _mesh_pos()
        for jj, chip in enumerate(chips):
            slot = 2 * chip[0] + chip[1]
            cp = pltpu.make_async_remote_copy(
                src_ref=src_ref.at[slot], dst_ref=land_ref.at[slot], send_sem=sends[jj], recv_sem=recvs[jj],
                device_id=(*chip, c), device_id_type=MESH)
            cp.wait_send()
            cp.wait_recv()

    buf = pltpu.HBM(land_thru.shape, land_thru.dtype)
    return pl.pallas_call(
        body, name="rs_owner_w_in_wait", out_shape=(buf, buf),
        in_specs=(HBM_SPEC, HBM_SPEC) + (SEM_SPEC,) * 6 + (ANY_SPEC,) * n_after, out_specs=(HBM_SPEC, HBM_SPEC),
        input_output_aliases={0: 0, 1: 1},
        compiler_params=pltpu.CompilerParams(has_side_effects=SPLIT_COPY_EFFECT),
    )(src_thru, land_thru, *sems, *after)[1]


def _comm_call(name, carry):
    n_i, n_o = len(carry.ins), len(carry.outs)

    def body(*refs):
        ins, outs, sems = refs[:n_i], refs[n_i:n_i + n_o], refs[n_i + n_o:]
        _start_all(carry.copies(ins, outs, sems)[0])
        _wait_all(carry.copies(ins, outs, sems)[1])

    return pl.pallas_call(
        body, name=name, out_shape=carry.outs, in_specs=[ANY_SPEC] * n_i, out_specs=[ANY_SPEC] * n_o,
        input_output_aliases=carry.alias, scratch_shapes=carry.sems,
    )(*carry.ins)


def _host_call(body, *, name, grid, in_specs, out_specs, out_shape, args, scratch_shapes=(), carry=None, then=None,
               prefetch=None, aliases=None):
    n_in, n_out, n_scr = len(in_specs), len(out_specs), len(scratch_shapes)
    n_ci = len(carry.ins) if carry else 0
    n_co = len(carry.outs) if carry else 0
    n_cs = len(carry.sems) if carry else 0
    n_pf = 0 if prefetch is None else 1

    def wrapped(*refs):
        pf, refs = refs[:n_pf], refs[n_pf:]
        ins, cins = refs[:n_in], refs[n_in:n_in + n_ci]
        o0 = n_in + n_ci
        outs, couts = refs[o0:o0 + n_out], refs[o0 + n_out:o0 + n_out + n_co]
        s0 = o0 + n_out + n_co
        scr, sems, tsems = refs[s0:s0 + n_scr], refs[s0 + n_scr:s0 + n_scr + n_cs], refs[s0 + n_scr + n_cs:]
        idx = [pl.program_id(a) for a in range(len(grid))]
        first = functools.reduce(jnp.logical_and, [i == 0 for i in idx])
        last = functools.reduce(jnp.logical_and, [i == g - 1 for i, g in zip(idx, grid)])

        if carry:
            @pl.when(first)
            def _():
                _start_all(carry.copies(cins, couts, sems)[0])

        body(*pf, *ins, *outs, *scr)

        if carry:
            @pl.when(last)
            def _():
                _wait_all(carry.copies(cins, couts, sems)[1])
                if then:
                    _start_all(then.copies(couts, couts, tsems)[0])
                    _wait_all(then.copies(couts, couts, tsems)[1])

    all_in = list(in_specs) + [ANY_SPEC] * n_ci
    all_out = list(out_specs) + [ANY_SPEC] * n_co
    all_scr = list(scratch_shapes) + (carry.sems if carry else []) + (then.sems if then else [])
    alias = {n_pf + i: o for i, o in (aliases or {}).items()}
    if carry:
        alias.update({n_pf + n_in + i: n_out + o for i, o in carry.alias.items()})
    kwargs = dict(name=name, out_shape=list(out_shape) + (carry.outs if carry else []), input_output_aliases=alias,
                  compiler_params=_params(len(grid)))
    if prefetch is None:
        call = pl.pallas_call(wrapped, grid=grid, in_specs=all_in, out_specs=all_out, scratch_shapes=all_scr, **kwargs)
        res = call(*args, *(carry.ins if carry else []))
    else:
        call = pl.pallas_call(wrapped, grid_spec=pltpu.PrefetchScalarGridSpec(
            num_scalar_prefetch=1, grid=grid, in_specs=all_in, out_specs=all_out, scratch_shapes=all_scr), **kwargs)
        res = call(prefetch, *args, *(carry.ins if carry else []))
    return list(res[:n_out]), list(res[n_out:])


def _rs_add_halves(pos, grads, recvs, name):
    n = len(grads)

    def body(pos_ref, *refs):
        for i in range(n):
            refs[2 * n + i][...] = (refs[i][...] + refs[n + i][...]).astype(BF16)

    blks = [(1, g.shape[1] // 4, g.shape[2]) for g in grads]
    mine = [pl.BlockSpec(b, lambda k, i, p: (k, p[1] * 2 + i, 0)) for b in blks]
    half = [pl.BlockSpec(b, lambda k, i, p: (k, i, 0)) for b in blks]
    return pl.pallas_call(
        body, name=name,
        grid_spec=pltpu.PrefetchScalarGridSpec(
            num_scalar_prefetch=1, grid=(N_CHIPS, 2), in_specs=mine + half, out_specs=half),
        out_shape=[SDS((N_CHIPS, g.shape[1] // 2, g.shape[2]), BF16) for g in grads],
        compiler_params=_params(2),
    )(pos, *grads, *recvs)


def _rs_sum_owner(pos, grads, recvs, recv3s, name):
    n = len(grads)

    def body(pos_ref, *refs):
        for i in range(n):
            g, s, r1, r2, r3 = (refs[j * n + i] for j in range(5))
            own = g[0] + s[0]
            refs[5 * n + i][...] = ((own + r1[0].astype(F32)) + r2[0].astype(F32)) + r3[0].astype(F32)

    blks = [(1, g.shape[1] // 4, g.shape[2]) for g in grads]
    mine = [pl.BlockSpec(b, lambda i, p: (p[0], p[1] * 2 + i, 0)) for b in blks]

    def slot(d):
        return [pl.BlockSpec(b, lambda i, p: ((p[0] + d) % N_CHIPS, i, 0)) for b in blks]

    return pl.pallas_call(
        body, name=name,
        grid_spec=pltpu.PrefetchScalarGridSpec(
            num_scalar_prefetch=1, grid=(2,), in_specs=mine + slot(0) + slot(1) + slot(2) + slot(3),
            out_specs=[pl.BlockSpec(b[1:], lambda i, p: (p[1] * 2 + i, 0)) for b in blks]),
        out_shape=[SDS(g.shape[1:], F32) for g in grads],
        compiler_params=_params(1),
    )(pos, *grads, *recvs, *recv3s, *recv3s, *recv3s)


def _cast_bf16(pos, arrs):
    n = len(arrs)

    def body(pos_ref, *refs):
        for i in range(n):
            refs[n + i][0] = refs[i][...].astype(BF16)

    return pl.pallas_call(
        body, name="cast_bf16",
        grid_spec=pltpu.PrefetchScalarGridSpec(
            num_scalar_prefetch=1, grid=(2,),
            in_specs=[pl.BlockSpec((a.shape[0] // 2, a.shape[1]), lambda i, p: (i, 0)) for a in arrs],
            out_specs=[pl.BlockSpec((1, a.shape[0] // 2, a.shape[1]), lambda i, p: (p[0], i, 0)) for a in arrs]),
        out_shape=[SDS((N_CHIPS,) + a.shape, BF16) for a in arrs],
        compiler_params=_params(1),
    )(pos, *arrs)


def _adamw_vals(w, g, m, v):
    m2 = ADAM_B1 * m + (1.0 - ADAM_B1) * g
    v2 = ADAM_B2 * v + (1.0 - ADAM_B2) * (g * g)
    m_hat = m2 / (1.0 - ADAM_B1 ** ADAM_STEP)
    v_hat = v2 / (1.0 - ADAM_B2 ** ADAM_STEP)
    delta = -ADAM_LR * (m_hat / (jnp.sqrt(v_hat) + ADAM_EPS) + ADAM_WD * w)
    return delta, m2, v2


def _adamw_big(w, g, m, v, name):
    rows, cols = w.shape
    rb = rows // 4

    def body(w_ref, g_ref, m_ref, v_ref, d_ref, m2_ref, v2_ref):
        d, m2, v2 = _adamw_vals(w_ref[...], g_ref[...], m_ref[...], v_ref[...])
        d_ref[...] = d
        m2_ref[...] = m2
        v2_ref[...] = v2

    spec = pl.BlockSpec((rb, cols), lambda i: (i, 0))
    return pl.pallas_call(
        body, name=name, grid=(4,), in_specs=[spec] * 4, out_specs=[spec] * 3,
        out_shape=[SDS(w.shape, F32)] * 3, compiler_params=_params(1),
    )(w, g, m, v)


ADAMW_GROUP_STEPS = 8


def _adamw_group(ws, gs, ms, vs, name, carry=None):
    n = len(ws)

    def body(*refs):
        for i in range(n):
            d, m2, v2 = _adamw_vals(refs[i][...], refs[n + i][...], refs[2 * n + i][...], refs[3 * n + i][...])
            refs[4 * n + i][...] = d
            refs[5 * n + i][...] = m2
            refs[6 * n + i][...] = v2

    specs = [pl.BlockSpec((w.shape[0] // ADAMW_GROUP_STEPS, w.shape[1]), lambda i: (i, 0)) for w in ws]
    shapes = [SDS(w.shape, F32) for w in ws]
    outs, carried = _host_call(
        body, name=name, grid=(ADAMW_GROUP_STEPS,), in_specs=specs * 4, out_specs=specs * 3, out_shape=shapes * 3,
        args=(*ws, *gs, *ms, *vs), carry=carry)
    return (outs[:n], outs[n:2 * n], outs[2 * n:]), carried


def _adamw_small(ws, gs, ms, vs):
    n = len(ws)

    def body(*refs):
        for i in range(n):
            d, m2, v2 = _adamw_vals(refs[i][...], refs[n + i][...], refs[2 * n + i][...], refs[3 * n + i][...])
            refs[4 * n + i][...] = d
            refs[5 * n + i][...] = m2
            refs[6 * n + i][...] = v2

    shapes = [SDS(w.shape, F32) for w in ws]
    outs = pl.pallas_call(
        body, name="adamw_small", out_shape=shapes * 3,
        in_specs=[VMEM_SPEC] * (4 * n), out_specs=[VMEM_SPEC] * (3 * n),
    )(*ws, *gs, *ms, *vs)
    return outs[:n], outs[n:2 * n], outs[2 * n:]


def _mod_fwd(cond64, w_mod_s, b_mod_s):
    def body(c_ref, w_ref, b_ref, o_ref):
        cc = c_ref[...]
        o_ref[...] = _dot_f32(cc * _sig(cc), w_ref[...]) + b_ref[...]

    return pl.pallas_call(
        body, name="mod_fwd", out_shape=SDS((cond64.shape[0], w_mod_s.shape[1]), F32),
        in_specs=[VMEM_SPEC] * 3, out_specs=VMEM_SPEC,
        compiler_params=pltpu.CompilerParams(vmem_limit_bytes=VMEM_LIMIT),
    )(cond64, w_mod_s, b_mod_s)


def _mod_bwd(cond64, dmod64, dmod64_my, w_mod_s, c_ctx):
    def body(c_ref, g_ref, gm_ref, w_ref, cc_ref, gw_ref, gb_ref, gcc_ref):
        cc = c_ref[...]
        act = cc * _sig(cc)
        gm = gm_ref[...]
        gw_ref[...] = _dot_f32(act, gm, (((0,), (0,)), ((), ())))
        gb_ref[...] = _colsum(g_ref[...])
        dact = _dot_f32(gm, w_ref[...], (((1,), (1,)), ((), ())))
        tot = dact[4:5, :]
        for dev in range(1, N_DEV):
            tot = tot + dact[8 * dev + 4:8 * dev + 5, :]
        c0 = cc_ref[...]
        s0 = _sig(c0)
        gcc_ref[...] = tot * (s0 * (1.0 + c0 * (1.0 - s0)))

    return pl.pallas_call(
        body, name="mod_bwd",
        out_shape=[SDS(w_mod_s.shape, F32), SDS((1, dmod64.shape[1]), F32), SDS((1, D_MODEL), F32)],
        in_specs=[VMEM_SPEC] * 5, out_specs=[VMEM_SPEC] * 3,
        compiler_params=pltpu.CompilerParams(vmem_limit_bytes=VMEM_LIMIT),
    )(cond64, dmod64, dmod64_my, w_mod_s, c_ctx)


SHARD_COLS = IN_COLS // N_CHIPS


def _in_fwd_own(pos, x, ctx, g_mix, mod_a, w_own, carry=None, then=None):
    bs, seq, _ = x.shape
    nb = seq // ROW_BLOCK + 1

    def body(pos_ref, x_ref, ctx_ref, g_ref, mod_ref, w_ref, p_ref, h_ref, w_bf):
        b, j = pl.program_id(0), pl.program_id(1)

        @pl.when((b == 0) & (j == 0))
        def _():
            w_bf[...] = w_ref[...].astype(BF16)

        is_ctx = j == 0
        xin = jnp.where(is_ctx, ctx_ref[0], x_ref[0])
        shift = jnp.where(is_ctx, mod_ref[0, 2:3, :], mod_ref[0, 0:1, :])
        scale = jnp.where(is_ctx, mod_ref[0, 3:4, :], mod_ref[0, 1:2, :])
        xn, _ = _rms(xin)
        hb = ((xn * g_ref[...]) * (1.0 + scale) + shift).astype(BF16)
        h_ref[0] = hb
        p_ref[0] = jnp.dot(hb, w_bf[...], preferred_element_type=F32)

    return _host_call(
        body, name="in_fwd_own", grid=(bs, nb), prefetch=pos,
        in_specs=[pl.BlockSpec((1, ROW_BLOCK, D_MODEL), lambda b, j, p: (b, jnp.maximum(j - 1, 0), 0)),
                  pl.BlockSpec((1, ROW_BLOCK, D_MODEL), lambda b, j, p: (b, 0, 0)),
                  pl.BlockSpec((1, D_MODEL), lambda b, j, p: (0, 0)),
                  pl.BlockSpec((1, 8, D_MODEL), lambda b, j, p: (b, 0, 0)),
                  pl.BlockSpec((D_MODEL, SHARD_COLS), lambda b, j, p: (0, 0))],
        out_specs=[pl.BlockSpec((1, ROW_BLOCK, SHARD_COLS), lambda b, j, p: (b, j, p[0])),
                   pl.BlockSpec((1, ROW_BLOCK, D_MODEL), lambda b, j, p: (b, j, 0))],
        out_shape=[SDS((bs, nb * ROW_BLOCK, IN_COLS), F32), SDS((bs, nb * ROW_BLOCK, D_MODEL), BF16)],
        scratch_shapes=[pltpu.VMEM((D_MODEL, SHARD_COLS), BF16)],
        args=(x, ctx, g_mix, mod_a, w_own), carry=carry, then=then)


def _in_fwd_rest(pos, h_all, w_in_g, p, carry=None):
    bs, rows, _ = h_all.shape
    rows_all = bs * rows
    tile = next(m * ROW_BLOCK for m in (3, 1) if rows_all % (m * ROW_BLOCK) == 0)

    def body(pos_ref, h_ref, w_ref, p_in_ref, p_ref):
        p_ref[...] = jnp.dot(h_ref[...], w_ref[0], preferred_element_type=F32)

    shard = lambda n, p: (p[0] + 1 + n) % N_CHIPS
    (p2,), carried = _host_call(
        body, name="in_fwd_rest", grid=(N_CHIPS - 1, rows_all // tile), prefetch=pos,
        in_specs=[pl.BlockSpec((tile, D_MODEL), lambda n, t, p: (t, 0)),
                  pl.BlockSpec((1, D_MODEL, SHARD_COLS), lambda n, t, p: (shard(n, p), 0, 0)),
                  ANY_SPEC],
        out_specs=[pl.BlockSpec((tile, SHARD_COLS), lambda n, t, p: (t, shard(n, p)))],
        out_shape=[SDS((rows_all, IN_COLS), F32)], aliases={2: 0},
        args=(h_all.reshape(rows_all, D_MODEL), w_in_g, p.reshape(rows_all, IN_COLS)), carry=carry)
    return p2.reshape(bs, rows, IN_COLS), carried


def _in_bwd(x, ctx, dx1, g_mix, mod_a, w_in, df_f, df_b, dv_f, dv_b, dq_f, dq_b, dpc, carry=None):
    bs, seq, _ = x.shape
    nb = seq // ROW_BLOCK + 1

    def body(x_ref, ctx_ref, dx1_ref, g_ref, mod_ref, w_ref, dff_ref, dfb_ref, dvf_ref, dvb_ref, dqf_ref, dqb_ref,
             dpc_ref, gx_ref, dp_ref, dg_ref, dmod_ref):
        b, j = pl.program_id(0), pl.program_id(1)
        is_ctx = j == 0

        @pl.when((b == 0) & (j == 0))
        def _():
            dg_ref[...] = jnp.zeros_like(dg_ref)

        @pl.when(j == 0)
        def _():
            dmod_ref[...] = jnp.zeros_like(dmod_ref)

        di = (dvf_ref[0] + dvb_ref[0]).astype(BF16)
        dq = (dqf_ref[0] + dqb_ref[0]).astype(BF16)
        dp = jnp.concatenate([dff_ref[0], dfb_ref[0], di, dq, dpc_ref[0]], axis=1)
        dp_ref[0] = dp
        dh = lax.dot_general(dp, w_ref[...], (((1,), (1,)), ((), ())), preferred_element_type=F32)
        xin = jnp.where(is_ctx, ctx_ref[0], x_ref[0])
        scale = jnp.where(is_ctx, mod_ref[0, 3:4, :], mod_ref[0, 1:2, :])
        xn, r = _rms(xin)
        g = g_ref[...]
        hn = xn * g
        d_shift = _colsum(dh)
        d_scale = _colsum(dh * hn)
        dhn = dh * (1.0 + scale)
        dg_ref[...] += _colsum(dhn * xn)
        dx = _rms_bwd(dhn * g, xn, r)

        @pl.when(is_ctx)
        def _():
            dmod_ref[0, 2:3, :] += d_shift
            dmod_ref[0, 3:4, :] += d_scale

        @pl.when(jnp.logical_not(is_ctx))
        def _():
            dmod_ref[0, 0:1, :] += d_shift
            dmod_ref[0, 1:2, :] += d_scale
            gx_ref[0] = dx + dx1_ref[0]

    def rows(w):
        return pl.BlockSpec((1, ROW_BLOCK, w), lambda b, j: (b, j, 0))

    lat = pl.BlockSpec((1, ROW_BLOCK, D_MODEL), lambda b, j: (b, jnp.maximum(j - 1, 0), 0))
    return _host_call(
        body, name="in_bwd", grid=(bs, nb),
        in_specs=[lat, pl.BlockSpec((1, ROW_BLOCK, D_MODEL), lambda b, j: (b, 0, 0)), lat,
                  pl.BlockSpec((1, D_MODEL), lambda b, j: (0, 0)),
                  pl.BlockSpec((1, 8, D_MODEL), lambda b, j: (b, 0, 0)),
                  pl.BlockSpec((D_MODEL, IN_COLS), lambda b, j: (0, 0)),
                  rows(KW), rows(KW), rows(KW), rows(KW), rows(KW), rows(KW), rows(7 * KW)],
        out_specs=[lat, rows(IN_COLS), pl.BlockSpec((1, D_MODEL), lambda b, j: (0, 0)),
                   pl.BlockSpec((1, 8, D_MODEL), lambda b, j: (b, 0, 0))],
        out_shape=[SDS(x.shape, F32), SDS((bs, nb * ROW_BLOCK, IN_COLS), BF16), SDS((1, D_MODEL), F32),
                   SDS((bs, 8, D_MODEL), F32)],
        args=(x, ctx, dx1, g_mix, mod_a, w_in, df_f, df_b, dv_f, dv_b, dq_f, dq_b, dpc), carry=carry)


def _lower_bound(lbg_ref, direction):
    return _sig(lbg_ref[0, direction:direction + 1, :] - lbg_ref[1, direction:direction + 1, :])


N_CHUNKS = ROW_BLOCK // HGRN_CHUNK


def _block_tri(upper):
    t = np.arange(ROW_BLOCK)[:, None]
    s = np.arange(ROW_BLOCK)[None, :]
    same = (t // HGRN_CHUNK) == (s // HGRN_CHUNK)
    return jnp.asarray(same & ((s >= t) if upper else (s <= t)), dtype=BF16)


TRI_SPEC = pl.BlockSpec((ROW_BLOCK, ROW_BLOCK), lambda b, j: (0, 0))


def _tri_matmul_f32(tri, g):
    g0 = g.astype(BF16)
    r1 = g - g0.astype(F32)
    g1 = r1.astype(BF16)
    g2 = (r1 - g1.astype(F32)).astype(BF16)
    return (jnp.dot(tri, g2, preferred_element_type=F32) + jnp.dot(tri, g1, preferred_element_type=F32)) \
        + jnp.dot(tri, g0, preferred_element_type=F32)


def _chunk_rows(rows):
    return jnp.concatenate([jnp.broadcast_to(r, (HGRN_CHUNK, r.shape[1])) for r in rows], axis=0)


def _block_gates(fl, q, lb, tri, upper):
    t = {}
    t["sg"] = _sig(fl)
    t["f"] = lb + (1.0 - lb) * t["sg"]
    k = 1.0 - t["f"]
    bcum = _tri_matmul_f32(tri, jnp.log(t["f"]))
    ends = [bcum[ci * HGRN_CHUNK:ci * HGRN_CHUNK + 1] if upper else bcum[(ci + 1) * HGRN_CHUNK - 1:(ci + 1) * HGRN_CHUNK]
            for ci in range(fl.shape[0] // HGRN_CHUNK)]
    mid = _chunk_rows([0.5 * r for r in ends])
    t["dec"] = [jnp.exp(r) for r in ends]
    t["e1"] = jnp.exp(bcum - mid)
    t["e2"] = jnp.exp(mid - bcum)
    t["eh"] = _chunk_rows([jnp.exp(0.5 * r) for r in ends])
    t["qi"] = q * t["e1"]
    t["ki"] = k * t["e2"]
    t["kd"] = t["ki"] * t["eh"]
    t["qe"] = t["qi"] * t["eh"]
    return t


def _hgrn_block_order(direction, nb):
    if direction == 0:
        return lambda j: j
    return lambda j: jnp.where(j == 0, 0, nb - j)


def _hgrn_fwd(p, lbg, carry=None, then=None):
    bs, rows, _ = p.shape
    nb = rows // ROW_BLOCK
    ncb = ROW_BLOCK // HGRN_CHUNK
    orders = [_hgrn_block_order(d, nb) for d in (0, 1)]
    dirs = (0, 1)

    def body(f0_ref, i0_ref, q0_ref, f1_ref, i1_ref, q1_ref, lbg_ref, tri0_ref, tri1_ref,
             o0_ref, s0_ref, o1_ref, s1_ref, st):
        @pl.when(pl.program_id(1) == 0)
        def _():
            st[...] = jnp.zeros_like(st)

        f_refs, i_refs, q_refs = (f0_ref, f1_ref), (i0_ref, i1_ref), (q0_ref, q1_ref)
        tri_refs, o_refs, s_refs = (tri0_ref, tri1_ref), (o0_ref, o1_ref), (s0_ref, s1_ref)
        chunk = lambda a, ci, h: a[ci * HGRN_CHUNK:(ci + 1) * HGRN_CHUNK, h * HEAD_DIM:(h + 1) * HEAD_DIM]
        masks = [_tri(HGRN_CHUNK, d == 1) for d in dirs]
        t = [_block_gates(f_refs[d][0], q_refs[d][0], _lower_bound(lbg_ref, d), tri_refs[d][...], d == 1) for d in dirs]
        v = [i_refs[d][0] for d in dirs]
        intra = [[[None] * N_HEADS for _ in range(ncb)] for _ in dirs]
        ds_loc = [[[None] * N_HEADS for _ in range(ncb)] for _ in dirs]
        for ci in range(ncb):
            for h in range(N_HEADS):
                for d in dirs:
                    a = jnp.where(masks[d], _dot_nt(chunk(t[d]["qi"], ci, h), chunk(t[d]["ki"], ci, h)), 0.0)
                    intra[d][ci][h] = _dot(a, chunk(v[d], ci, h))
                    ds_loc[d][ci][h] = _dot_tn(chunk(v[d], ci, h), chunk(t[d]["kd"], ci, h))
        for h in range(N_HEADS):
            ls = slice(h * HEAD_DIM, (h + 1) * HEAD_DIM)
            s = [st[d, h] for d in dirs]
            for step in range(ncb):
                for d in dirs:
                    ci = ncb - 1 - step if d == 1 else step
                    s_refs[d][0, 0, ci, h] = s[d]
                    o_refs[d][0, ci * HGRN_CHUNK:(ci + 1) * HGRN_CHUNK, ls] = (
                        intra[d][ci][h] + _dot_nt(chunk(t[d]["qe"], ci, h), s[d]))
                    s[d] = s[d] * t[d]["dec"][ci][:, ls] + ds_loc[d][ci][h]
            for d in dirs:
                st[d, h] = s[d]

    def col(d, cb):
        return pl.BlockSpec((1, ROW_BLOCK, KW), lambda b, j: (b, orders[d](j), cb))

    def outs(d):
        return [pl.BlockSpec((1, ROW_BLOCK, KW), lambda b, j: (b, orders[d](j), 0)),
                pl.BlockSpec((1, 1, ncb, N_HEADS, HEAD_DIM, HEAD_DIM), lambda b, j: (b, orders[d](j), 0, 0, 0, 0))]

    shapes = [SDS((bs, rows, KW), F32), SDS((bs, nb, ncb, N_HEADS, HEAD_DIM, HEAD_DIM), F32)]
    return _host_call(
        body, name="hgrn_fwd", grid=(bs, nb),
        in_specs=[col(0, 0), col(0, 2), col(0, 3), col(1, 1), col(1, 2), col(1, 3),
                  pl.BlockSpec((2, 2, KW), lambda b, j: (0, 0, 0)), TRI_SPEC, TRI_SPEC],
        out_specs=outs(0) + outs(1), out_shape=shapes * 2,
        scratch_shapes=[pltpu.VMEM((2, N_HEADS, HEAD_DIM, HEAD_DIM), F32)],
        args=(p, p, p, p, p, p, lbg, _block_tri(False), _block_tri(True)), carry=carry, then=then)


def _hgrn_bwd_pair(p, lbg, s_saved, do_raw, carry=None):
    bs, rows, _ = p.shape
    nb = rows // ROW_BLOCK
    ncb = ROW_BLOCK // HGRN_CHUNK
    dirs = (0, 1)
    fwd_orders = [_hgrn_block_order(d, nb) for d in dirs]
    orders = [lambda j, d=d: fwd_orders[d](nb - 1 - j) for d in dirs]
    pairs = [(ci, h) for ci in range(ncb) for h in range(N_HEADS)]

    def body(f0_ref, i0_ref, q0_ref, s0_ref, do0_ref, f1_ref, i1_ref, q1_ref, s1_ref, do1_ref,
             lbg_ref, tril_ref, triu_ref,
             df0_ref, dq0_ref, dv0_ref, dlb0_ref, df1_ref, dq1_ref, dv1_ref, dlb1_ref, dst, acc):
        b, j = pl.program_id(0), pl.program_id(1)
        f_refs, i_refs, q_refs = (f0_ref, f1_ref), (i0_ref, i1_ref), (q0_ref, q1_ref)
        s_refs, do_refs = (s0_ref, s1_ref), (do0_ref, do1_ref)
        df_refs, dq_refs, dv_refs, dlb_refs = (df0_ref, df1_ref), (dq0_ref, dq1_ref), (dv0_ref, dv1_ref), (dlb0_ref, dlb1_ref)
        tri_refs, trit_refs = (tril_ref, triu_ref), (triu_ref, tril_ref)

        @pl.when((b == 0) & (j == 0))
        def _():
            dlb0_ref[...] = jnp.zeros_like(dlb0_ref)
            dlb1_ref[...] = jnp.zeros_like(dlb1_ref)

        @pl.when(j == 0)
        def _():
            dst[...] = jnp.zeros_like(dst)

        chunk = lambda a, ci, h: a[ci * HGRN_CHUNK:(ci + 1) * HGRN_CHUNK, h * HEAD_DIM:(h + 1) * HEAD_DIM]
        rows_of = lambda ci: slice(ci * HGRN_CHUNK, (ci + 1) * HGRN_CHUNK)
        lanes_of = lambda h: slice(h * HEAD_DIM, (h + 1) * HEAD_DIM)
        grid3 = lambda: [[[None] * N_HEADS for _ in range(ncb)] for _ in dirs]
        lbs = [_lower_bound(lbg_ref, d) for d in dirs]
        masks = [_tri(HGRN_CHUNK, d == 1) for d in dirs]
        masks_t = [_tri(HGRN_CHUNK, d != 1) for d in dirs]
        t = [_block_gates(f_refs[d][0], q_refs[d][0], lbs[d], tri_refs[d][...], d == 1) for d in dirs]
        v = [i_refs[d][0] for d in dirs]
        do = [do_refs[d][0] for d in dirs]
        a_t, da, da_t, dv_in, ds_loc = (grid3() for _ in range(5))
        for ci, h in pairs:
            for d in dirs:
                a_t[d][ci][h] = _dot_nt(chunk(t[d]["ki"], ci, h), chunk(t[d]["qi"], ci, h))
        for ci, h in pairs:
            for d in dirs:
                da[d][ci][h] = _dot_nt(chunk(do[d], ci, h), chunk(v[d], ci, h))
        for ci, h in pairs:
            for d in dirs:
                da_t[d][ci][h] = _dot_nt(chunk(v[d], ci, h), chunk(do[d], ci, h))
        for ci, h in pairs:
            for d in dirs:
                acc[d, 3, rows_of(ci), lanes_of(h)] = _dot(chunk(do[d], ci, h), s_refs[d][0, 0, ci, h])
        for ci, h in pairs:
            for d in dirs:
                ds_loc[d][ci][h] = _dot_tn(chunk(do[d], ci, h), chunk(t[d]["qe"], ci, h))
        for ci, h in pairs:
            for d in dirs:
                acc[d, 0, rows_of(ci), lanes_of(h)] = _dot(jnp.where(masks[d], da[d][ci][h], 0.0),
                                                           chunk(t[d]["ki"], ci, h))
        for ci, h in pairs:
            for d in dirs:
                acc[d, 1, rows_of(ci), lanes_of(h)] = _dot(jnp.where(masks_t[d], da_t[d][ci][h], 0.0),
                                                           chunk(t[d]["qi"], ci, h))
        for ci, h in pairs:
            for d in dirs:
                dv_in[d][ci][h] = _dot(jnp.where(masks_t[d], a_t[d][ci][h], 0.0), chunk(do[d], ci, h))
        ddec = grid3()
        for h in range(N_HEADS):
            ls = lanes_of(h)
            ds = [dst[d, h] for d in dirs]
            for step in range(ncb):
                for d in dirs:
                    ci = step if d == 1 else ncb - 1 - step
                    acc[d, 2, rows_of(ci), ls] = _dot(chunk(v[d], ci, h), ds[d])
                    acc[d, 4, rows_of(ci), ls] = dv_in[d][ci][h] + _dot_nt(chunk(t[d]["kd"], ci, h), ds[d])
                    ddec[d][ci][h] = _colsum(ds[d] * s_refs[d][0, 0, ci, h])
                    ds[d] = ds[d] * t[d]["dec"][ci][:, ls] + ds_loc[d][ci][h]
            for d in dirs:
                dst[d, h] = ds[d]
        for d in dirs:
            td = t[d]
            dqi, dki, dkd, dqe = (acc[d, i] for i in range(4))
            dq_refs[d][0] = td["e1"] * (dqi + dqe * td["eh"])
            dv_refs[d][0] = acc[d, 4]
            dk = td["e2"] * (dki + dkd * td["eh"])
            dkd_kd = dkd * td["kd"]
            db = dqi * td["qi"] - dki * td["ki"] - dkd_kd + dqe * td["qe"]
            dbl = [_colsum(dkd_kd[rows_of(ci)]) + jnp.concatenate(ddec[d][ci], axis=1) * td["dec"][ci]
                   for ci in range(ncb)]
            dg = _tri_matmul_f32(trit_refs[d][...], db) + _chunk_rows(dbl)
            df = dg / td["f"] - dk
            sg = td["sg"]
            dlb_refs[d][...] += _colsum(df * (1.0 - sg))
            df_refs[d][0] = (df * (1.0 - lbs[d]) * sg * (1.0 - sg)).astype(BF16)

    def ins(d):
        col = lambda cb: pl.BlockSpec((1, ROW_BLOCK, KW), lambda b, j: (b, orders[d](j), cb))
        return [col(d), col(2), col(3),
                pl.BlockSpec((1, 1, ncb, N_HEADS, HEAD_DIM, HEAD_DIM), lambda b, j: (b, orders[d](j), 0, 0, 0, 0)),
                pl.BlockSpec((1, ROW_BLOCK, KW), lambda b, j: (b, orders[d](j), 0))]

    def outs(d):
        row = pl.BlockSpec((1, ROW_BLOCK, KW), lambda b, j: (b, orders[d](j), 0))
        return [row, row, row, pl.BlockSpec((1, KW), lambda b, j: (0, 0))]

    shapes = [SDS((bs, rows, KW), BF16), SDS((bs, rows, KW), F32), SDS((bs, rows, KW), F32), SDS((1, KW), F32)]
    return _host_call(
        body, name="hgrn_bwd", grid=(bs, nb),
        in_specs=ins(0) + ins(1) + [pl.BlockSpec((2, 2, KW), lambda b, j: (0, 0, 0)), TRI_SPEC, TRI_SPEC],
        out_specs=outs(0) + outs(1), out_shape=shapes * 2,
        scratch_shapes=[pltpu.VMEM((2, N_HEADS, HEAD_DIM, HEAD_DIM), F32), pltpu.VMEM((2, 5, ROW_BLOCK, KW), F32)],
        args=(p, p, p, s_saved[0], do_raw, p, p, p, s_saved[1], do_raw, lbg, _block_tri(False), _block_tri(True)),
        carry=carry)


def _mix_values(og, u, v, ga, gb, o_raw, gna, lng, lnb, ws_ref, bst, wpa, wpb, wo):
    t = {}
    sog = _sig(og)
    t["sog"], t["silu_og"] = sog, og * sog
    xh_l, r_l = [], []
    for h in range(N_HEADS):
        xh, r = _rms(o_raw[:, h * HEAD_DIM:(h + 1) * HEAD_DIM])
        xh_l.append(xh)
        r_l.append(r)
    t["xh"], t["r"] = jnp.concatenate(xh_l, axis=1), r_l
    gna4 = jnp.concatenate([gna] * N_HEADS, axis=1)
    t["gna4"] = gna4
    t["o_n"] = t["xh"] * gna4
    t["o_a"] = t["o_n"] * t["silu_og"]
    t["ya"] = _dot(t["o_a"], wpa)
    t["gu"], t["tu"] = _gelu(u)
    gv, t["tv"] = _gelu(v)
    mu = jnp.mean(gv, axis=-1, keepdims=True)
    cen = gv - mu
    t["rstd"] = lax.rsqrt(jnp.mean(cen * cen, axis=-1, keepdims=True) + EPS)
    t["xhat"] = cen * t["rstd"]
    vn = t["xhat"] * lng + lnb
    t["vn"] = vn
    chunks = []
    for n in range(ROW_BLOCK // SGU_CHUNK):
        rs = slice(n * SGU_CHUNK, (n + 1) * SGU_CHUNK)
        groups = []
        for g in range(N_HEADS):
            ls = slice(g * HEAD_DIM, (g + 1) * HEAD_DIM)
            groups.append(_dot(ws_ref[g], vn[rs, ls]) + bst[:, g:g + 1])
        chunks.append(jnp.concatenate(groups, axis=1))
    t["mixed"] = jnp.concatenate(chunks, axis=0)
    t["o_bm"] = t["gu"] * t["mixed"]
    t["yb"] = _dot(t["o_bm"], wpb)
    t["sa"], t["sb"] = _sig(ga), _sig(gb)
    t["merged"] = t["sa"] * t["ya"] + t["sb"] * t["yb"]
    t["mix"] = _dot(t["merged"], wo)
    return t


def _mix_in_specs(row_of):
    def col(cb):
        return pl.BlockSpec((1, ROW_BLOCK, KW), lambda b, j: (b, row_of(j), cb))
    return [col(cb) for cb in range(4, 11)]


def _mix_param_specs():
    full2 = lambda r, c: pl.BlockSpec((r, c), lambda b, j: (0, 0))
    return [full2(1, HEAD_DIM), full2(1, KW), full2(1, KW),
            pl.BlockSpec((N_HEADS, SGU_CHUNK, SGU_CHUNK), lambda b, j: (0, 0, 0)),
            full2(SGU_CHUNK, N_HEADS), full2(KW, D_MODEL), full2(KW, D_MODEL), full2(D_MODEL, D_MODEL)]


def _mix_fwd(p, o_f, o_b, x, mod_c, gna, lng, lnb, w_s, bst, wpa, wpb, wo):
    bs, seq, _ = x.shape
    nbl = seq // ROW_BLOCK

    def body(og_r, u_r, v_r, ga0_r, ga1_r, gb0_r, gb1_r, of_r, ob_r, x_r, mod_r,
             gna_r, lng_r, lnb_r, ws_r, bst_r, wpa_r, wpb_r, wo_r, x1_r):
        ga = jnp.concatenate([ga0_r[0], ga1_r[0]], axis=1)
        gb = jnp.concatenate([gb0_r[0], gb1_r[0]], axis=1)
        t = _mix_values(og_r[0], u_r[0], v_r[0], ga, gb, of_r[0] + ob_r[0], gna_r[...], lng_r[...], lnb_r[...],
                        ws_r, bst_r[...], wpa_r[...], wpb_r[...], wo_r[...])
        x1_r[0] = x_r[0] + mod_r[0, 0:1, :] * t["mix"]

    row = lambda w: pl.BlockSpec((1, ROW_BLOCK, w), lambda b, j: (b, j + 1, 0))
    lat = pl.BlockSpec((1, ROW_BLOCK, D_MODEL), lambda b, j: (b, j, 0))
    return pl.pallas_call(
        body, name="mix_fwd", grid=(bs, nbl),
        in_specs=_mix_in_specs(lambda j: j + 1) + [row(KW), row(KW), lat,
                                                    pl.BlockSpec((1, 8, D_MODEL), lambda b, j: (b, 0, 0))]
        + _mix_param_specs(),
        out_specs=lat, out_shape=SDS(x.shape, F32), compiler_params=_params(2),
    )(p, p, p, p, p, p, p, o_f, o_b, x, mod_c, gna, lng, lnb, w_s, bst, wpa, wpb, wo)


def _mix_bwd(p, o_f, o_b, dx1, mod_c, gna, lng, lnb, w_s, w_s_t, bst, wpa, wpb, wo, carry=None):
    bs, rows, _ = p.shape
    nb = rows // ROW_BLOCK

    def body(og_r, u_r, v_r, ga0_r, ga1_r, gb0_r, gb1_r, of_r, ob_r, dx1_r, mod_r,
             gna_r, lng_r, lnb_r, ws_r, bst_r, wpa_r, wpb_r, wo_r, wst_r,
             dor_r, dpc_r, dwpa_r, dwpb_r, dwo_r, dgna_r, dlng_r, dlnb_r, dws_r, dbst_r, dmod_r):
        b, j = pl.program_id(0), pl.program_id(1)

        @pl.when((b == 0) & (j == 0))
        def _():
            for r in (dwpa_r, dwpb_r, dwo_r, dgna_r, dlng_r, dlnb_r, dws_r, dbst_r):
                r[...] = jnp.zeros_like(r)

        @pl.when(j == 0)
        def _():
            dmod_r[...] = jnp.zeros_like(dmod_r)
            dor_r[...] = jnp.zeros_like(dor_r)
            dpc_r[...] = jnp.zeros_like(dpc_r)

        @pl.when(j > 0)
        def _():
            og, u, v = og_r[0], u_r[0], v_r[0]
            ga = jnp.concatenate([ga0_r[0], ga1_r[0]], axis=1)
            gb = jnp.concatenate([gb0_r[0], gb1_r[0]], axis=1)
            gna, lng = gna_r[...], lng_r[...]
            wpa, wpb, wo = wpa_r[...], wpb_r[...], wo_r[...]
            dx1 = dx1_r[0]
            dmix = mod_r[0, 0:1, :] * dx1
            dmerged = _dot_nt(dmix, wo)
            t = _mix_values(og, u, v, ga, gb, of_r[0] + ob_r[0], gna, lng, lnb_r[...],
                            ws_r, bst_r[...], wpa, wpb, wo)
            dmod_r[0, 0:1, :] += _colsum(dx1 * t["mix"])
            dwo_r[...] += _dot_tn(t["merged"], dmix)
            sa, sb = t["sa"], t["sb"]
            dya, dyb = sa * dmerged, sb * dmerged
            dga = dmerged * t["ya"] * sa * (1.0 - sa)
            dgb = dmerged * t["yb"] * sb * (1.0 - sb)
            do_a = _dot_nt(dya, wpa)
            dwpa_r[...] += _dot_tn(t["o_a"], dya)
            do_bm = _dot_nt(dyb, wpb)
            dwpb_r[...] += _dot_tn(t["o_bm"], dyb)
            sog = t["sog"]
            dog = do_a * t["o_n"] * (sog * (1.0 + og * (1.0 - sog)))
            do_n = do_a * t["silu_og"]
            dxh = do_n * t["gna4"]
            prod = do_n * t["xh"]
            dgna = jnp.zeros((1, HEAD_DIM), F32)
            dor_l = []
            for h in range(N_HEADS):
                ls = slice(h * HEAD_DIM, (h + 1) * HEAD_DIM)
                dgna = dgna + _colsum(prod[:, ls])
                dor_l.append(_rms_bwd(dxh[:, ls], t["xh"][:, ls], t["r"][h]))
            dgna_r[...] += dgna
            dor_r[0] = jnp.concatenate(dor_l, axis=1)
            du = do_bm * t["mixed"] * _dgelu(u, t["tu"])
            dmixed = do_bm * t["gu"]
            vn = t["vn"]
            dvn_chunks = []
            for n in range(ROW_BLOCK // SGU_CHUNK):
                rs = slice(n * SGU_CHUNK, (n + 1) * SGU_CHUNK)
                groups = []
                for g in range(N_HEADS):
                    ls = slice(g * HEAD_DIM, (g + 1) * HEAD_DIM)
                    dm = dmixed[rs, ls]
                    dws_r[g] += _dot_nt(dm, vn[rs, ls])
                    dbst_r[:, g:g + 1] += jnp.sum(dm, axis=1, keepdims=True)
                    groups.append(_dot(wst_r[g], dm))
                dvn_chunks.append(jnp.concatenate(groups, axis=1))
            dvn = jnp.concatenate(dvn_chunks, axis=0)
            xhat = t["xhat"]
            dlng_r[...] += _colsum(dvn * xhat)
            dlnb_r[...] += _colsum(dvn)
            dxhat = dvn * lng
            dgv = t["rstd"] * (dxhat - jnp.mean(dxhat, axis=-1, keepdims=True)
                               - xhat * jnp.mean(dxhat * xhat, axis=-1, keepdims=True))
            dv = dgv * _dgelu(v, t["tv"])
            dpc_r[0] = jnp.concatenate([dog, du, dv, dga, dgb], axis=1).astype(BF16)

    row = lambda w: pl.BlockSpec((1, ROW_BLOCK, w), lambda b, j: (b, j, 0))
    lat = pl.BlockSpec((1, ROW_BLOCK, D_MODEL), lambda b, j: (b, jnp.maximum(j - 1, 0), 0))
    full2 = lambda r, c: pl.BlockSpec((r, c), lambda b, j: (0, 0))
    ws_spec = pl.BlockSpec((N_HEADS, SGU_CHUNK, SGU_CHUNK), lambda b, j: (0, 0, 0))
    return _host_call(
        body, name="mix_bwd", grid=(bs, nb),
        in_specs=_mix_in_specs(lambda j: j) + [row(KW), row(KW), lat,
                                                pl.BlockSpec((1, 8, D_MODEL), lambda b, j: (b, 0, 0))]
        + _mix_param_specs() + [ws_spec],
        out_specs=[row(KW), row(7 * KW), full2(KW, D_MODEL), full2(KW, D_MODEL), full2(D_MODEL, D_MODEL),
                   full2(1, HEAD_DIM), full2(1, KW), full2(1, KW), ws_spec, full2(SGU_CHUNK, N_HEADS),
                   pl.BlockSpec((1, 8, D_MODEL), lambda b, j: (b, 0, 0))],
        out_shape=[SDS((bs, rows, KW), F32), SDS((bs, rows, 7 * KW), BF16), SDS((KW, D_MODEL), F32),
                   SDS((KW, D_MODEL), F32), SDS((D_MODEL, D_MODEL), F32), SDS((1, HEAD_DIM), F32),
                   SDS((1, KW), F32), SDS((1, KW), F32), SDS((N_HEADS, SGU_CHUNK, SGU_CHUNK), F32),
                   SDS((SGU_CHUNK, N_HEADS), F32), SDS((bs, 8, D_MODEL), F32)],
        args=(p, p, p, p, p, p, p, o_f, o_b, dx1, mod_c, gna, lng, lnb, w_s, bst, wpa, wpb, wo, w_s_t), carry=carry)


def _ffn(x1, target, mod_c, g_ffn, g_final, w_up, w_down):
    bs, seq, _ = x1.shape
    nbl = seq // ROW_BLOCK

    def body(x1_r, tg_r, mod_r, gf_r, gl_r, wu_r, wd_r,
             dx1_r, h2_r, dab_r, hid_r, dffn_r, loss_r, dgl_r, dgf_r, dmod_r):
        b, j = pl.program_id(0), pl.program_id(1)

        @pl.when((b == 0) & (j == 0))
        def _():
            for r in (loss_r, dgl_r, dgf_r):
                r[...] = jnp.zeros_like(r)

        @pl.when(j == 0)
        def _():
            dmod_r[...] = jnp.zeros_like(dmod_r)

        x1 = x1_r[0]
        shift, scale, gate = mod_r[0, 1:2, :], mod_r[0, 2:3, :], mod_r[0, 3:4, :]
        gf, gl = gf_r[...], gl_r[...]
        xn2, r2 = _rms(x1)
        hn2 = xn2 * gf
        h2 = (hn2 * (1.0 + scale) + shift).astype(BF16)
        h2_r[0] = h2
        ab = jnp.dot(h2, wu_r[...], preferred_element_type=F32)
        a, bb = ab[:, :D_FF], ab[:, D_FF:]
        sa = _sig(a)
        silu_a = a * sa
        hid = (silu_a * bb).astype(BF16)
        hid_r[0] = hid
        ffn = jnp.dot(hid, wd_r[...], preferred_element_type=F32)
        x2 = x1 + gate * ffn
        xn3, r3 = _rms(x2)
        err = xn3 * gl - tg_r[0]
        loss_r[...] += 0.5 * jnp.sum(jnp.mean(err * err, axis=-1, keepdims=True), axis=0, keepdims=True)
        dy = err * (1.0 / D_MODEL)
        dgl_r[...] += _colsum(dy * xn3)
        dx2 = _rms_bwd(dy * gl, xn3, r3)
        dmod_r[0, 3:4, :] += _colsum(dx2 * ffn)
        dffn = (gate * dx2).astype(BF16)
        dffn_r[0] = dffn
        dhid = lax.dot_general(dffn, wd_r[...], (((1,), (1,)), ((), ())), preferred_element_type=F32)
        da = dhid * bb * (sa * (1.0 + a * (1.0 - sa)))
        db = dhid * silu_a
        dab = jnp.concatenate([da, db], axis=1).astype(BF16)
        dab_r[0] = dab
        dh2 = lax.dot_general(dab, wu_r[...], (((1,), (1,)), ((), ())), preferred_element_type=F32)
        dmod_r[0, 1:2, :] += _colsum(dh2)
        dmod_r[0, 2:3, :] += _colsum(dh2 * hn2)
        dhn2 = dh2 * (1.0 + scale)
        dgf_r[...] += _colsum(dhn2 * xn2)
        dx1_r[0] = dx2 + _rms_bwd(dhn2 * gf, xn2, r2)

    lat = lambda w: pl.BlockSpec((1, ROW_BLOCK, w), lambda b, j: (b, j, 0))
    full2 = lambda r, c: pl.BlockSpec((r, c), lambda b, j: (0, 0))
    mod_spec = pl.BlockSpec((1, 8, D_MODEL), lambda b, j: (b, 0, 0))
    return pl.pallas_call(
        body, name="ffn", grid=(bs, nbl),
        in_specs=[lat(D_MODEL), lat(D_MODEL), mod_spec, full2(1, D_MODEL), full2(1, D_MODEL),
                  full2(D_MODEL, 2 * D_FF), full2(D_FF, D_MODEL)],
        out_specs=[lat(D_MODEL), lat(D_MODEL), lat(2 * D_FF), lat(D_FF), lat(D_MODEL),
                   full2(1, 1), full2(1, D_MODEL), full2(1, D_MODEL), mod_spec],
        out_shape=[SDS(x1.shape, F32), SDS(x1.shape, BF16), SDS((bs, seq, 2 * D_FF), BF16),
                   SDS((bs, seq, D_FF), BF16), SDS(x1.shape, BF16), SDS((1, 1), F32),
                   SDS((1, D_MODEL), F32), SDS((1, D_MODEL), F32), SDS((bs, 8, D_MODEL), F32)],
        compiler_params=_params(2),
    )(x1, target, mod_c, g_ffn, g_final, w_up, w_down)


def _row_tile(rows):
    return next(m * ROW_BLOCK for m in (4, 2, 1) if rows % (m * ROW_BLOCK) == 0)


def _matmul_tn(a, b, n_blocks, tk, name, carry=None):
    t, m = a.shape
    n = b.shape[1]
    tn = n // n_blocks

    def body(a_ref, b_ref, o_ref):
        @pl.when(pl.program_id(1) == 0)
        def _():
            o_ref[...] = jnp.zeros_like(o_ref)
        o_ref[0] += _dot_tn(a_ref[...], b_ref[...])

    (out,), carried = _host_call(
        body, name=name, grid=(n_blocks, t // tk),
        in_specs=[pl.BlockSpec((tk, m), lambda i, k: (k, 0)), pl.BlockSpec((tk, tn), lambda i, k: (k, i))],
        out_specs=[pl.BlockSpec((1, m, tn), lambda i, k: (i, 0, 0))],
        out_shape=[SDS((n_blocks, m, tn), F32)], args=(a, b), carry=carry)
    return out if carry is None else (out, carried)


SMALL_ROWS = 80
ROW_CCTX = 3


def _small_reduce(gathered, lbg):
    def body(g_ref, lbg_ref, s_ref, dgam_ref):
        tot = g_ref[0:SMALL_ROWS, :]
        for dev in range(1, N_DEV):
            tot = tot + g_ref[dev * SMALL_ROWS:(dev + 1) * SMALL_ROWS, :]
        s_ref[...] = tot
        cc = g_ref[ROW_CCTX:ROW_CCTX + 1, :]
        for dev in range(2, N_DEV, 2):
            cc = cc + g_ref[dev * SMALL_ROWS + ROW_CCTX:dev * SMALL_ROWS + ROW_CCTX + 1, :]
        s_ref[ROW_CCTX:ROW_CCTX + 1, :] = cc
        dlb = tot[7:8, :]
        for d in range(2):
            s0 = _sig(lbg_ref[0, d:d + 1, :] - lbg_ref[1, d:d + 1, :])
            dgam_ref[d:d + 1, :] = dlb[:, d * KW:(d + 1) * KW] * s0 * (1.0 - s0)

    return pl.pallas_call(
        body, name="small_reduce", out_shape=[SDS((SMALL_ROWS, D_MODEL), F32), SDS((2, KW), F32)],
        in_specs=[VMEM_SPEC] * 2, out_specs=[VMEM_SPEC] * 2,
    )(gathered, lbg)


def _pad_cols(a, width):
    return jnp.pad(a, ((0, 0), (0, width - a.shape[1])))


def kernel(x, c, ctx, c_ctx, w_mod, b_mod, g_mix, g_ffn, w_in, lb_gamma, g_norm_a, ln_v_g, ln_v_b, w_s, b_s, w_pa, w_pb, w_o, w_up, w_down, g_final, loss_target, m_c_ctx, m_w_mod, m_b_mod, m_g_mix, m_g_ffn, m_w_in, m_lb_gamma, m_g_norm_a, m_ln_v_g, m_ln_v_b, m_w_s, m_b_s, m_w_pa, m_w_pb, m_w_o, m_w_up, m_w_down, m_g_final, v_c_ctx, v_w_mod, v_b_mod, v_g_mix, v_g_ffn, v_w_in, v_lb_gamma, v_g_norm_a, v_ln_v_g, v_ln_v_b, v_w_s, v_b_s, v_w_pa, v_w_pb, v_w_o, v_w_up, v_w_down, v_g_final):
    ax, ay, ac = lax.axis_index("x"), lax.axis_index("y"), lax.axis_index("c")
    kc = 2 * ax + ay
    dev = 2 * kc + ac
    pos = jnp.stack([kc, ac]).astype(jnp.int32)
    bs, seq, _ = x.shape
    assert bs <= 4 and ctx.shape[1] == ROW_BLOCK and seq % ROW_BLOCK == 0
    mod_cols = w_mod.shape[2]

    lbg_row = _pad_cols(lb_gamma.reshape(1, -1), D_MODEL)
    pay1 = jnp.concatenate([c, jnp.zeros((4 - bs, D_MODEL), F32), c_ctx[None, :], lbg_row,
                            jnp.zeros((2, D_MODEL), F32)], axis=0)
    cond64 = _all_gather8(pay1, "gather_cond")
    lbg_full = cond64.reshape(N_DEV, 8, D_MODEL)[0::2, 5, :KW].reshape(N_CHIPS, 2, 2, HEAD_DIM)
    lbg_full = jnp.transpose(lbg_full, (1, 2, 0, 3)).reshape(2, 2, KW)

    b_mod_s = lax.dynamic_slice(b_mod, (0, kc * mod_cols), (1, mod_cols))
    mod_s = _mod_fwd(cond64, w_mod[0], b_mod_s)
    shards = [w_in[0], w_up[0], w_pa[0], w_pb[0], w_o[0], w_down[0]]
    bufs = _cast_bf16(pos, shards)
    mod_g = _all_gather8(mod_s, "gather_mod").reshape(N_DEV, 64, mod_cols)[0::2]
    mod_full = jnp.transpose(mod_g, (1, 0, 2)).reshape(64, N_CHIPS * mod_cols)
    mod_mine = lax.dynamic_slice(mod_full, (dev * 8, 0), (8, 6 * D_MODEL)).reshape(8, 6, D_MODEL)
    mod, mc = mod_mine[:bs], mod_mine[4]
    zeros4 = jnp.zeros((bs, 4, D_MODEL), F32)
    mod_a = jnp.concatenate([mod[:, 0:2], jnp.broadcast_to(mc[None, 0:2], (bs, 2, D_MODEL)), zeros4], axis=1)
    mod_c = jnp.concatenate([mod[:, 2:6], zeros4], axis=1)

    def cols_major(a):
        return jnp.transpose(a, (1, 0, 2)).reshape(a.shape[1], -1)

    gna, lng, lnb = g_norm_a, ln_v_g, ln_v_b
    ws3 = w_s[0]
    ws3_t = jnp.transpose(ws3, (0, 2, 1))
    bst = jnp.transpose(b_s[0])

    (p, h_all), (w_in_g,) = _in_fwd_own(pos, x, ctx, g_mix, mod_a, w_in[0], carry=_carry_gather_send(bufs[:1]),
                                        then=_carry_gather_forward(bufs[:1]))
    p, sent_up = _in_fwd_rest(pos, h_all, w_in_g, p, carry=_carry_gather_send(bufs[1:2]))
    w_in_f = cols_major(w_in_g)
    fwd_rest = _carry_gather_forward(bufs[2:])
    then_rest = _Carry([], [], {}, fwd_rest.sems, lambda i, o, s: fwd_rest.copies(i[1:], o[1:], s))
    (o_f, s_f, o_b, s_b), gathered = _hgrn_fwd(
        p, lbg_full, carry=_merge_carries(_carry_gather_forward(sent_up), _carry_gather_send(bufs[2:])),
        then=then_rest)
    w_up_f, w_pa_f, w_pb_f = (cols_major(a) for a in gathered[:3])
    w_o_f = gathered[3].reshape(-1, D_MODEL)
    w_down_f = gathered[4].reshape(-1, D_MODEL)
    x1 = _mix_fwd(p, o_f, o_b, x, mod_c, gna, lng, lnb, ws3, bst, w_pa_f, w_pb_f, w_o_f)
    dx1, h2, dab, hid, dffn, loss_part, dg_final, dg_ffn, dmod_ffn = _ffn(
        x1, loss_target, mod_c, g_ffn, g_final[None, :], w_up_f, w_down_f)
    rows_lat = bs * seq
    tk_lat = _row_tile(rows_lat)
    dw_up = _matmul_tn(h2.reshape(rows_lat, D_MODEL), dab.reshape(rows_lat, 2 * D_FF), N_CHIPS, tk_lat, "dw_up")
    dw_down = _matmul_tn(hid.reshape(rows_lat, D_FF), dffn.reshape(rows_lat, D_MODEL), 1, tk_lat, "dw_down")

    def shard_major(a):
        return jnp.transpose(a.reshape(a.shape[0], N_CHIPS, -1), (1, 0, 2))

    ffn_names, mix_names = ["w_up", "w_down"], ["w_pa", "w_pb", "w_o"]
    part_ffn = [dw_up, dw_down.reshape(N_CHIPS, -1, D_MODEL)]
    (do_raw, dpc, dw_pa, dw_pb, dw_o, dgna, dlng, dlnb, dws, dbst, dmod_mix), sib_ffn = _mix_bwd(
        p, o_f, o_b, dx1, mod_c, gna, lng, lnb, ws3, ws3_t, bst, w_pa_f, w_pb_f, w_o_f,
        carry=_carry_sibling_halves(part_ffn))
    cpbf_ffn = _rs_add_halves(pos, part_ffn, sib_ffn, "rs_add_ffn")
    part_mix = [shard_major(dw_pa), shard_major(dw_pb), dw_o.reshape(N_CHIPS, -1, D_MODEL)]
    (df_f, dq_f, dv_f, dlb0, df_b, dq_b, dv_b, dlb1), got = _hgrn_bwd_pair(
        p, lbg_full, (s_f, s_b), do_raw,
        carry=_merge_carries(_carry_to_owner(cpbf_ffn), _carry_sibling_halves(part_mix)))
    own_ffn, sib_mix = got[:2], got[2:]
    half_ffn = _rs_sum_owner(pos, part_ffn, sib_ffn, own_ffn, "rs_sum_ffn")
    cpbf_mix = _rs_add_halves(pos, part_mix, sib_mix, "rs_add_mix")
    (grad_x, dp, dg_mix, dmod_in), _ = _in_bwd(x, ctx, dx1, g_mix, mod_a, w_in_f, df_f, df_b, dv_f, dv_b, dq_f, dq_b,
                                               dpc)

    rows_all = dp.shape[0] * dp.shape[1]
    tk_all = _row_tile(rows_all)
    dw_in, got = _matmul_tn(h_all.reshape(rows_all, D_MODEL), dp.reshape(rows_all, IN_COLS), N_CHIPS, tk_all, "dw_in",
                            carry=_merge_carries(_carry_join_halves(half_ffn), _carry_to_owner(cpbf_mix)))
    g_ffn_w, own_mix = got[:2], got[2:]
    half_mix = _rs_sum_owner(pos, part_mix, sib_mix, own_mix, "rs_sum_mix")

    dmod_mine = jnp.concatenate([dmod_in[:, 0], dmod_in[:, 1], dmod_mix[:, 0], dmod_ffn[:, 1], dmod_ffn[:, 2],
                                 dmod_ffn[:, 3]], axis=1)
    dmc = jnp.concatenate([jnp.sum(dmod_in[:, 2], axis=0), jnp.sum(dmod_in[:, 3], axis=0),
                           jnp.zeros((4 * D_MODEL,), F32)])[None, :]
    pay3 = jnp.concatenate([dmod_mine, jnp.zeros((4 - bs, 6 * D_MODEL), F32), dmc,
                            jnp.zeros((3, 6 * D_MODEL), F32)], axis=0)
    dmod64, got = _all_gather8(pay3, "gather_dmod", carry=_merge_carries(_carry_sibling_halves([dw_in]),
                                                                          _carry_join_halves(half_mix)))
    sib_in, g_mix_w = got[:1], got[1:]
    cpbf_in = _rs_add_halves(pos, [dw_in], sib_in, "rs_add_w_in")
    dmod64_my = lax.dynamic_slice(dmod64, (0, kc * mod_cols), (64, mod_cols))
    g_w_mod, g_b_mod, g_cctx_part = _mod_bwd(cond64, dmod64, dmod64_my, w_mod[0], c_ctx[None, :])

    def row(*parts):
        return _pad_cols(jnp.concatenate([q.reshape(1, -1) for q in parts], axis=1), D_MODEL)

    small_rows = [dg_mix, dg_ffn, dg_final, g_cctx_part, row(dgna), row(dlng, dlnb), row(jnp.transpose(dbst)),
                  row(dlb0, dlb1), row(loss_part), jnp.zeros((7, D_MODEL), F32), dws.reshape(64, D_MODEL)]
    pay4 = jnp.concatenate(small_rows, axis=0)
    tot, dgam0 = _small_reduce(_all_gather8(pay4, "gather_small"), lbg_full)

    own_sems, own_src, own_land, own_token = _owner_send_start(cpbf_in[0], after=tot)

    rest_names = ["w_up", "w_pa", "w_pb", "w_o", "w_down", "w_mod"]
    rest_w = shards[1:] + [w_mod[0]]
    rest_g = [g_ffn_w[0], g_mix_w[0] + own_token[0, 0], g_mix_w[1], g_mix_w[2], g_ffn_w[1], g_w_mod]
    rest_m = [m_w_up[0], m_w_pa[0], m_w_pb[0], m_w_o[0], m_w_down[0], m_w_mod[0]]
    rest_v = [v_w_up[0], v_w_pa[0], v_w_pb[0], v_w_o[0], v_w_down[0], v_w_mod[0]]
    (ds_r, m2s_r, v2s_r), _ = _adamw_group(rest_w, rest_g, rest_m, rest_v, "adamw_rest")
    res = {}
    for name, g, d, m2, v2 in zip(rest_names, rest_g, ds_r, m2s_r, v2s_r):
        res[name] = (g[None], d[None], m2[None], v2[None])
    own_in = [_owner_send_wait(own_sems, own_src, own_land, after=(ds_r[0],))]
    g_in_w = _comm_call("rs_join_w_in",
                        _carry_join_halves(_rs_sum_owner(pos, [dw_in], sib_in, own_in, "rs_sum_w_in")))
    d, m2, v2 = _adamw_big(shards[0], g_in_w[0], m_w_in[0], v_w_in[0], "adamw_w_in")
    res["w_in"] = (g_in_w[0][None], d[None], m2[None], v2[None])

    loss = tot[8, 0]
    dgam_full = jnp.stack([dgam0, -dgam0])
    g_lbg = lax.dynamic_slice(dgam_full, (0, 0, kc * HEAD_DIM), (2, 2, HEAD_DIM))

    small = [
        ("c_ctx", c_ctx[None, :], tot[3:4], m_c_ctx, v_c_ctx),
        ("b_mod", b_mod, g_b_mod, m_b_mod, v_b_mod),
        ("g_mix", g_mix, tot[0:1], m_g_mix, v_g_mix),
        ("g_ffn", g_ffn, tot[1:2], m_g_ffn, v_g_ffn),
        ("lb_gamma", lb_gamma.reshape(4, HEAD_DIM), g_lbg.reshape(4, HEAD_DIM), m_lb_gamma, v_lb_gamma),
        ("g_norm_a", g_norm_a, tot[4:5, :HEAD_DIM], m_g_norm_a, v_g_norm_a),
        ("ln_v_g", ln_v_g, tot[5:6, :KW], m_ln_v_g, v_ln_v_g),
        ("ln_v_b", ln_v_b, tot[5:6, KW:], m_ln_v_b, v_ln_v_b),
        ("w_s", w_s.reshape(N_HEADS * SGU_CHUNK, SGU_CHUNK), tot[16:80].reshape(N_HEADS * SGU_CHUNK, SGU_CHUNK),
         m_w_s, v_w_s),
        ("b_s", b_s[0], tot[6:7, :KW].reshape(N_HEADS, SGU_CHUNK), m_b_s, v_b_s),
        ("g_final", g_final[None, :], tot[2:3], m_g_final, v_g_final),
    ]
    ws_, gs_ = [s[1] for s in small], [s[2] for s in small]
    ms_ = [s[3].reshape(s[1].shape) for s in small]
    vs_ = [s[4].reshape(s[1].shape) for s in small]
    ds_, m2s_, v2s_ = _adamw_small(ws_, gs_, ms_, vs_)
    for (name, _, g, m, _), d, m2, v2 in zip(small, ds_, m2s_, v2s_):
        res[name] = tuple(t.reshape(m.shape) for t in (g, d, m2, v2))

    order = ["c_ctx", "w_mod", "b_mod", "g_mix", "g_ffn", "w_in", "lb_gamma", "g_norm_a", "ln_v_g", "ln_v_b",
             "w_s", "b_s", "w_pa", "w_pb", "w_o", "w_up", "w_down", "g_final"]
    outs = [loss, grad_x]
    for part in range(4):
        outs += [res[n][part] for n in order]
    return tuple(outs)
```

```python
import functools
import math

import jax
import jax.numpy as jnp
import numpy as np
from jax import lax
from jax.experimental import pallas as pl
from jax.experimental.pallas import tpu as pltpu

F32 = jnp.float32
BF16 = jnp.bfloat16
SDS = jax.ShapeDtypeStruct
MESH = pl.DeviceIdType.MESH

EPS = 1e-6
D_MODEL = 1024
N_HEADS = 4
HEAD_DIM = 128
KW = N_HEADS * HEAD_DIM
IN_COLS = 11 * KW
D_FF = 2816
HGRN_CHUNK = 64
SGU_CHUNK = 128
ROW_BLOCK = 256
N_CHIPS = 4
N_DEV = 8
V7X_VMEM_BYTES = 64 * 1024 * 1024
VMEM_LIMIT = V7X_VMEM_BYTES - 6 * 1024 * 1024

ADAM_LR, ADAM_B1, ADAM_B2, ADAM_EPS, ADAM_WD, ADAM_STEP = 0.001, 0.9, 0.999, 1e-08, 0.01, 10
GELU_C0 = math.sqrt(2.0 / math.pi)
GELU_C1 = 0.044715

VMEM_SPEC = pl.BlockSpec(memory_space=pltpu.VMEM)
ANY_SPEC = pl.BlockSpec(memory_space=pl.ANY)


def _params(n_grid):
    return pltpu.CompilerParams(dimension_semantics=("arbitrary",) * n_grid, vmem_limit_bytes=VMEM_LIMIT)


def _sig(x):
    return 0.5 * jnp.tanh(0.5 * x) + 0.5


def _gelu(x):
    t = jnp.tanh(GELU_C0 * (x + GELU_C1 * x * x * x))
    return 0.5 * x * (1.0 + t), t


def _dgelu(x, t):
    return 0.5 * (1.0 + t) + 0.5 * x * (1.0 - t * t) * GELU_C0 * (1.0 + 3.0 * GELU_C1 * x * x)


def _dot(a, b):
    return jnp.dot(a.astype(BF16), b.astype(BF16), preferred_element_type=F32)


def _dot_nt(a, b):
    return lax.dot_general(a.astype(BF16), b.astype(BF16), (((1,), (1,)), ((), ())), preferred_element_type=F32)


def _dot_tn(a, b):
    return lax.dot_general(a.astype(BF16), b.astype(BF16), (((0,), (0,)), ((), ())), preferred_element_type=F32)


def _dot_f32(a, b, dims=(((1,), (0,)), ((), ()))):
    return lax.dot_general(a, b, dims, precision=lax.Precision.HIGHEST, preferred_element_type=F32)


def _rms(x):
    r = lax.rsqrt(jnp.mean(x * x, axis=-1, keepdims=True) + EPS)
    return x * r, r


def _rms_bwd(dxn, xn, r):
    return r * (dxn - xn * jnp.mean(dxn * xn, axis=-1, keepdims=True))


def _colsum(a):
    return jnp.sum(a, axis=0, keepdims=True)


def _tri(n, upper):
    t = lax.broadcasted_iota(jnp.int32, (n, n), 0)
    s = lax.broadcasted_iota(jnp.int32, (n, n), 1)
    return (s >= t) if upper else (s <= t)


def _all_gather8(x_shard, name, carry=None, then=None):
    m_per, n = x_shard.shape
    n_ci = len(carry.ins) if carry else 0
    n_co = len(carry.outs) if carry else 0
    n_cs = len(carry.sems) if carry else 0

    def body(*refs):
        x_ref, cins = refs[0], refs[1:1 + n_ci]
        out_ref, couts = refs[1 + n_ci], refs[2 + n_ci:2 + n_ci + n_co]
        send_sems, recv_sems, local_sem = refs[2 + n_ci + n_co:5 + n_ci + n_co]
        csems = refs[5 + n_ci + n_co:5 + n_ci + n_co + n_cs]
        tsems = refs[5 + n_ci + n_co + n_cs:]
        if carry:
            _start_all(carry.copies(cins, couts, csems)[0])
        _gather8_body(x_ref, out_ref, send_sems, recv_sems, local_sem, m_per)
        if carry:
            _wait_all(carry.copies(cins, couts, csems)[1])
        if then:
            _start_all(then.copies(couts, couts, tsems)[0])
            _wait_all(then.copies(couts, couts, tsems)[1])

    res = pl.pallas_call(
        body, name=name, out_shape=[SDS((N_DEV * m_per, n), x_shard.dtype)] + (carry.outs if carry else []),
        in_specs=[VMEM_SPEC] + [ANY_SPEC] * n_ci, out_specs=[VMEM_SPEC] + [ANY_SPEC] * n_co,
        input_output_aliases={1 + i: 1 + o for i, o in carry.alias.items()} if carry else {},
        scratch_shapes=[pltpu.SemaphoreType.DMA((7,)), pltpu.SemaphoreType.DMA((7,)), pltpu.SemaphoreType.DMA]
        + (carry.sems if carry else []) + (then.sems if then else []),
    )(x_shard, *(carry.ins if carry else []))
    return res[0] if carry is None else (res[0], list(res[1:]))


def _gather8_body(x_ref, out_ref, send_sems, recv_sems, local_sem, m_per):
    x, y, c = lax.axis_index("x"), lax.axis_index("y"), lax.axis_index("c")
    me, sibling = (x, y, c), (x, y, 1 - c)
    chips = [(1 - x, y), (x, 1 - y), (1 - x, 1 - y)]

    def rows(px, py, pc):
        return out_ref.at[pl.ds((4 * px + 2 * py + pc) * m_per, m_per), :]

    def copy(k, block, to, src=None):
        return pltpu.make_async_remote_copy(
            src_ref=rows(*block) if src is None else src, dst_ref=rows(*block),
            send_sem=send_sems.at[k], recv_sem=recv_sems.at[k], device_id=to, device_id_type=MESH)

    mine = pltpu.make_async_copy(x_ref, rows(*me), local_sem)
    mine.start()
    first = [copy(0, me, sibling, src=x_ref)]
    first += [copy(1 + j, me, (*chip, c), src=x_ref) for j, chip in enumerate(chips)]
    for cp in first:
        cp.start()
    passed = [copy(4 + j, (*chip, c), sibling) for j, chip in enumerate(chips)]
    for j, chip in enumerate(chips):
        copy(1 + j, (*chip, c), me).wait_recv()
        passed[j].start()
    copy(0, sibling, me).wait_recv()
    for j, chip in enumerate(chips):
        copy(4 + j, (*chip, 1 - c), me).wait_recv()
    for cp in first + passed:
        cp.wait_send()
    mine.wait()


def _mesh_pos():
    x, y, c = lax.axis_index("x"), lax.axis_index("y"), lax.axis_index("c")
    chips = [(1 - x, y), (x, 1 - y), (1 - x, 1 - y)]
    return x, y, c, 2 * x + y, (x, y, 1 - c), chips


def _half_rows(c, rh):
    return pl.ds(pl.multiple_of(c * rh, 16), rh)


class _Carry:
    def __init__(self, ins, outs, alias, sems, copies):
        self.ins, self.outs, self.alias, self.sems, self.copies = list(ins), list(outs), dict(alias), list(sems), copies


def _remote(src, dst, send, recv, to):
    return functools.partial(pltpu.make_async_remote_copy, src_ref=src, dst_ref=dst, send_sem=send, recv_sem=recv,
                             device_id=to, device_id_type=MESH)


def _carry_gather_send(bufs):
    n = len(bufs)

    def copies(ins, outs, sems):
        x, y, c, kc, sibling, chips = _mesh_pos()
        starts, waits = [], []
        for wi in range(n):
            rh = outs[wi].shape[1] // 2
            for jj, chip in enumerate(chips):
                mine = outs[wi].at[kc, _half_rows(c, rh), :]
                cp = _remote(mine, mine, sems[0].at[wi, jj], sems[1].at[wi, jj], (*chip, c))
                starts.append(cp)
                waits.append((cp, "send"))
                theirs = outs[wi].at[2 * chip[0] + chip[1], _half_rows(c, rh), :]
                waits.append((_remote(theirs, theirs, sems[0].at[wi, jj], sems[1].at[wi, jj], (*chip, c)), "recv"))
        return starts, waits

    return _Carry(bufs, [SDS(b.shape, b.dtype) for b in bufs], {i: i for i in range(n)},
                  [pltpu.SemaphoreType.DMA((n, 3)), pltpu.SemaphoreType.DMA((n, 3))], copies)


def _carry_gather_forward(bufs):
    n = len(bufs)

    def copies(ins, outs, sems):
        x, y, c, kc, sibling, chips = _mesh_pos()
        starts, waits = [], []
        for wi in range(n):
            rh = outs[wi].shape[1] // 2
            for jj, chip in enumerate(chips):
                got = outs[wi].at[2 * chip[0] + chip[1], _half_rows(c, rh), :]
                cp = _remote(got, got, sems[0].at[wi, jj], sems[1].at[wi, jj], sibling)
                starts.append(cp)
                waits.append((cp, "send"))
                other = outs[wi].at[2 * chip[0] + chip[1], _half_rows(1 - c, rh), :]
                waits.append((_remote(other, other, sems[0].at[wi, jj], sems[1].at[wi, jj], sibling), "recv"))
        return starts, waits

    return _Carry(bufs, [SDS(b.shape, b.dtype) for b in bufs], {i: i for i in range(n)},
                  [pltpu.SemaphoreType.DMA((n, 3)), pltpu.SemaphoreType.DMA((n, 3))], copies)


def _carry_sibling_halves(grads):
    n = len(grads)

    def copies(ins, outs, sems):
        x, y, c, kc, sibling, chips = _mesh_pos()
        cps = [_remote(ins[wi].at[:, _half_rows(1 - c, ins[wi].shape[1] // 2), :], outs[wi],
                       sems[0].at[wi], sems[1].at[wi], sibling) for wi in range(n)]
        return cps, [(cp, "both") for cp in cps]

    return _Carry(grads, [SDS((N_CHIPS, g.shape[1] // 2, g.shape[2]), F32) for g in grads], {},
                  [pltpu.SemaphoreType.DMA((n,)), pltpu.SemaphoreType.DMA((n,))], copies)


def _carry_to_owner(cpbfs):
    n = len(cpbfs)

    def copies(ins, outs, sems):
        x, y, c, kc, sibling, chips = _mesh_pos()
        starts, waits = [], []
        for wi in range(n):
            for jj, chip in enumerate(chips):
                cp = _remote(ins[wi].at[2 * chip[0] + chip[1]], outs[wi].at[kc],
                             sems[0].at[wi, jj], sems[1].at[wi, jj], (*chip, c))
                starts.append(cp)
                waits.append((cp, "send"))
                slot = outs[wi].at[2 * chip[0] + chip[1]]
                waits.append((_remote(slot, slot, sems[0].at[wi, jj], sems[1].at[wi, jj], (*chip, c)), "recv"))
        return starts, waits

    return _Carry(cpbfs, [SDS(g.shape, BF16) for g in cpbfs], {},
                  [pltpu.SemaphoreType.DMA((n, 3)), pltpu.SemaphoreType.DMA((n, 3))], copies)


def _carry_join_halves(bufs):
    n = len(bufs)

    def copies(ins, outs, sems):
        x, y, c, kc, sibling, chips = _mesh_pos()
        cps = []
        for wi in range(n):
            mine = outs[wi].at[_half_rows(c, outs[wi].shape[0] // 2), :]
            cps.append(_remote(mine, mine, sems[0].at[wi], sems[1].at[wi], sibling))
        return cps, [(cp, "both") for cp in cps]

    return _Carry(bufs, [SDS(b.shape, F32) for b in bufs], {i: i for i in range(n)},
                  [pltpu.SemaphoreType.DMA((n,)), pltpu.SemaphoreType.DMA((n,))], copies)


def _merge_carries(*carries):
    ins, outs, alias, sems, parts = [], [], {}, [], []
    for cy in carries:
        parts.append((len(ins), len(cy.ins), len(outs), len(cy.outs), len(sems), len(cy.sems), cy.copies))
        alias.update({len(ins) + i: len(outs) + o for i, o in cy.alias.items()})
        ins += cy.ins
        outs += cy.outs
        sems += cy.sems

    def copies(i, o, s):
        starts, waits = [], []
        for i0, ni, o0, no, s0, ns, fn in parts:
            st, wt = fn(i[i0:i0 + ni], o[o0:o0 + no], s[s0:s0 + ns])
            starts += st
            waits += wt
        return starts, waits

    return _Carry(ins, outs, alias, sems, copies)


def _start_all(starts):
    for cp in starts:
        cp().start()


def _wait_all(waits):
    for cp, which in waits:
        if which == "send":
            cp().wait_send()
        elif which == "recv":
            cp().wait_recv()
        else:
            cp().wait()


HBM_SPEC = pl.BlockSpec(memory_space=pltpu.HBM)
SEM_SPEC = pl.BlockSpec(memory_space=pltpu.SEMAPHORE)
SPLIT_COPY_EFFECT = pltpu.SideEffectType.DATAFLOW_SIDE_EFFECTING


def _owner_send_start(cpbf, after):
    land = lax.empty(cpbf.shape, cpbf.dtype)

    def body(src_ref, land_ref, after_ref, s0, s1, s2, r0, r1, r2, src_thru, land_thru, token):
        x, y, c, kc, sibling, chips = _mesh_pos()
        for jj, (chip, s_sem, r_sem) in enumerate(zip(chips, (s0, s1, s2), (r0, r1, r2))):
            pltpu.make_async_remote_copy(
                src_ref=src_ref.at[2 * chip[0] + chip[1]], dst_ref=land_ref.at[kc], send_sem=s_sem, recv_sem=r_sem,
                device_id=(*chip, c), device_id_type=MESH).start()
        token[...] = jnp.zeros_like(token)

    buf = pltpu.HBM(cpbf.shape, cpbf.dtype)
    outs = pl.pallas_call(
        body, name="rs_owner_w_in_start",
        out_shape=(pltpu.SemaphoreType.DMA(()),) * 6 + (buf, buf, SDS((8, 128), F32)),
        in_specs=(HBM_SPEC, HBM_SPEC, ANY_SPEC), out_specs=(SEM_SPEC,) * 6 + (HBM_SPEC, HBM_SPEC, VMEM_SPEC),
        input_output_aliases={0: 6, 1: 7},
        compiler_params=pltpu.CompilerParams(has_side_effects=SPLIT_COPY_EFFECT),
    )(pltpu.with_memory_space_constraint(cpbf, pltpu.HBM), pltpu.with_memory_space_constraint(land, pltpu.HBM), after)
    return outs[:6], outs[6], outs[7], outs[8]


def _owner_send_wait(sems, src_thru, land_thru, after):
    n_after = len(after)

    def body(*refs):
        src_ref, land_ref = refs[0], refs[1]
        sends, recvs = refs[2:5], refs[5:8]
        x, y, c, kc, sibling, chips = _mesh_pos()
        for jj, chip in enumerate(chips):
            slot = 2 * chip[0] + chip[1]
            cp = pltpu.make_async_remote_copy(
                src_ref=src_ref.at[slot], dst_ref=land_ref.at[slot], send_sem=sends[jj], recv_sem=recvs[jj],
                device_id=(*chip, c), device_id_type=MESH)
            cp.wait_send()
            cp.wait_recv()

    buf = pltpu.HBM(land_thru.shape, land_thru.dtype)
    return pl.pallas_call(
        body, name="rs_owner_w_in_wait", out_shape=(buf, buf),
        in_specs=(HBM_SPEC, HBM_SPEC) + (SEM_SPEC,) * 6 + (ANY_SPEC,) * n_after, out_specs=(HBM_SPEC, HBM_SPEC),
        input_output_aliases={0: 0, 1: 1},
        compiler_params=pltpu.CompilerParams(has_side_effects=SPLIT_COPY_EFFECT),
    )(src_thru, land_thru, *sems, *after)[1]


def _comm_call(name, carry):
    n_i, n_o = len(carry.ins), len(carry.outs)

    def body(*refs):
        ins, outs, sems = refs[:n_i], refs[n_i:n_i + n_o], refs[n_i + n_o:]
        _start_all(carry.copies(ins, outs, sems)[0])
        _wait_all(carry.copies(ins, outs, sems)[1])

    return pl.pallas_call(
        body, name=name, out_shape=carry.outs, in_specs=[ANY_SPEC] * n_i, out_specs=[ANY_SPEC] * n_o,
        input_output_aliases=carry.alias, scratch_shapes=carry.sems,
    )(*carry.ins)


def _host_call(body, *, name, grid, in_specs, out_specs, out_shape, args, scratch_shapes=(), carry=None, then=None,
               prefetch=None, aliases=None):
    n_in, n_out, n_scr = len(in_specs), len(out_specs), len(scratch_shapes)
    n_ci = len(carry.ins) if carry else 0
    n_co = len(carry.outs) if carry else 0
    n_cs = len(carry.sems) if carry else 0
    n_pf = 0 if prefetch is None else 1

    def wrapped(*refs):
        pf, refs = refs[:n_pf], refs[n_pf:]
        ins, cins = refs[:n_in], refs[n_in:n_in + n_ci]
        o0 = n_in + n_ci
        outs, couts = refs[o0:o0 + n_out], refs[o0 + n_out:o0 + n_out + n_co]
        s0 = o0 + n_out + n_co
        scr, sems, tsems = refs[s0:s0 + n_scr], refs[s0 + n_scr:s0 + n_scr + n_cs], refs[s0 + n_scr + n_cs:]
        idx = [pl.program_id(a) for a in range(len(grid))]
        first = functools.reduce(jnp.logical_and, [i == 0 for i in idx])
        last = functools.reduce(jnp.logical_and, [i == g - 1 for i, g in zip(idx, grid)])

        if carry:
            @pl.when(first)
            def _():
                _start_all(carry.copies(cins, couts, sems)[0])

        body(*pf, *ins, *outs, *scr)

        if carry:
            @pl.when(last)
            def _():
                _wait_all(carry.copies(cins, couts, sems)[1])
                if then:
                    _start_all(then.copies(couts, couts, tsems)[0])
                    _wait_all(then.copies(couts, couts, tsems)[1])

    all_in = list(in_specs) + [ANY_SPEC] * n_ci
    all_out = list(out_specs) + [ANY_SPEC] * n_co
    all_scr = list(scratch_shapes) + (carry.sems if carry else []) + (then.sems if then else [])
    alias = {n_pf + i: o for i, o in (aliases or {}).items()}
    if carry:
        alias.update({n_pf + n_in + i: n_out + o for i, o in carry.alias.items()})
    kwargs = dict(name=name, out_shape=list(out_shape) + (carry.outs if carry else []), input_output_aliases=alias,
                  compiler_params=_params(len(grid)))
    if prefetch is None:
        call = pl.pallas_call(wrapped, grid=grid, in_specs=all_in, out_specs=all_out, scratch_shapes=all_scr, **kwargs)
        res = call(*args, *(carry.ins if carry else []))
    else:
        call = pl.pallas_call(wrapped, grid_spec=pltpu.PrefetchScalarGridSpec(
            num_scalar_prefetch=1, grid=grid, in_specs=all_in, out_specs=all_out, scratch_shapes=all_scr), **kwargs)
        res = call(prefetch, *args, *(carry.ins if carry else []))
    return list(res[:n_out]), list(res[n_out:])


def _rs_add_halves(pos, grads, recvs, name):
    n = len(grads)

    def body(pos_ref, *refs):
        for i in range(n):
            refs[2 * n + i][...] = (refs[i][...] + refs[n + i][...]).astype(BF16)

    blks = [(1, g.shape[1] // 4, g.shape[2]) for g in grads]
    mine = [pl.BlockSpec(b, lambda k, i, p: (k, p[1] * 2 + i, 0)) for b in blks]
    half = [pl.BlockSpec(b, lambda k, i, p: (k, i, 0)) for b in blks]
    return pl.pallas_call(
        body, name=name,
        grid_spec=pltpu.PrefetchScalarGridSpec(
            num_scalar_prefetch=1, grid=(N_CHIPS, 2), in_specs=mine + half, out_specs=half),
        out_shape=[SDS((N_CHIPS, g.shape[1] // 2, g.shape[2]), BF16) for g in grads],
        compiler_params=_params(2),
    )(pos, *grads, *recvs)


def _rs_sum_owner(pos, grads, recvs, recv3s, name):
    n = len(grads)

    def body(pos_ref, *refs):
        for i in range(n):
            g, s, r1, r2, r3 = (refs[j * n + i] for j in range(5))
            own = g[0] + s[0]
            refs[5 * n + i][...] = ((own + r1[0].astype(F32)) + r2[0].astype(F32)) + r3[0].astype(F32)

    blks = [(1, g.shape[1] // 4, g.shape[2]) for g in grads]
    mine = [pl.BlockSpec(b, lambda i, p: (p[0], p[1] * 2 + i, 0)) for b in blks]

    def slot(d):
        return [pl.BlockSpec(b, lambda i, p: ((p[0] + d) % N_CHIPS, i, 0)) for b in blks]

    return pl.pallas_call(
        body, name=name,
        grid_spec=pltpu.PrefetchScalarGridSpec(
            num_scalar_prefetch=1, grid=(2,), in_specs=mine + slot(0) + slot(1) + slot(2) + slot(3),
            out_specs=[pl.BlockSpec(b[1:], lambda i, p: (p[1] * 2 + i, 0)) for b in blks]),
        out_shape=[SDS(g.shape[1:], F32) for g in grads],
        compiler_params=_params(1),
    )(pos, *grads, *recvs, *recv3s, *recv3s, *recv3s)


def _cast_bf16(pos, arrs):
    n = len(arrs)

    def body(pos_ref, *refs):
        for i in range(n):
            refs[n + i][0] = refs[i][...].astype(BF16)

    return pl.pallas_call(
        body, name="cast_bf16",
        grid_spec=pltpu.PrefetchScalarGridSpec(
            num_scalar_prefetch=1, grid=(2,),
            in_specs=[pl.BlockSpec((a.shape[0] // 2, a.shape[1]), lambda i, p: (i, 0)) for a in arrs],
            out_specs=[pl.BlockSpec((1, a.shape[0] // 2, a.shape[1]), lambda i, p: (p[0], i, 0)) for a in arrs]),
        out_shape=[SDS((N_CHIPS,) + a.shape, BF16) for a in arrs],
        compiler_params=_params(1),
    )(pos, *arrs)


def _adamw_vals(w, g, m, v):
    m2 = ADAM_B1 * m + (1.0 - ADAM_B1) * g
    v2 = ADAM_B2 * v + (1.0 - ADAM_B2) * (g * g)
    m_hat = m2 / (1.0 - ADAM_B1 ** ADAM_STEP)
    v_hat = v2 / (1.0 - ADAM_B2 ** ADAM_STEP)
    delta = -ADAM_LR * (m_hat / (jnp.sqrt(v_hat) + ADAM_EPS) + ADAM_WD * w)
    return delta, m2, v2


def _adamw_big(w, g, m, v, name):
    rows, cols = w.shape
    rb = rows // 4

    def body(w_ref, g_ref, m_ref, v_ref, d_ref, m2_ref, v2_ref):
        d, m2, v2 = _adamw_vals(w_ref[...], g_ref[...], m_ref[...], v_ref[...])
        d_ref[...] = d
        m2_ref[...] = m2
        v2_ref[...] = v2

    spec = pl.BlockSpec((rb, cols), lambda i: (i, 0))
    return pl.pallas_call(
        body, name=name, grid=(4,), in_specs=[spec] * 4, out_specs=[spec] * 3,
        out_shape=[SDS(w.shape, F32)] * 3, compiler_params=_params(1),
    )(w, g, m, v)


ADAMW_GROUP_STEPS = 8


def _adamw_group(ws, gs, ms, vs, name, carry=None):
    n = len(ws)

    def body(*refs):
        for i in range(n):
            d, m2, v2 = _adamw_vals(refs[i][...], refs[n + i][...], refs[2 * n + i][...], refs[3 * n + i][...])
            refs[4 * n + i][...] = d
            refs[5 * n + i][...] = m2
            refs[6 * n + i][...] = v2

    specs = [pl.BlockSpec((w.shape[0] // ADAMW_GROUP_STEPS, w.shape[1]), lambda i: (i, 0)) for w in ws]
    shapes = [SDS(w.shape, F32) for w in ws]
    outs, carried = _host_call(
        body, name=name, grid=(ADAMW_GROUP_STEPS,), in_specs=specs * 4, out_specs=specs * 3, out_shape=shapes * 3,
        args=(*ws, *gs, *ms, *vs), carry=carry)
    return (outs[:n], outs[n:2 * n], outs[2 * n:]), carried


def _adamw_small(ws, gs, ms, vs):
    n = len(ws)

    def body(*refs):
        for i in range(n):
            d, m2, v2 = _adamw_vals(refs[i][...], refs[n + i][...], refs[2 * n + i][...], refs[3 * n + i][...])
            refs[4 * n + i][...] = d
            refs[5 * n + i][...] = m2
            refs[6 * n + i][...] = v2

    shapes = [SDS(w.shape, F32) for w in ws]
    outs = pl.pallas_call(
        body, name="adamw_small", out_shape=shapes * 3,
        in_specs=[VMEM_SPEC] * (4 * n), out_specs=[VMEM_SPEC] * (3 * n),
    )(*ws, *gs, *ms, *vs)
    return outs[:n], outs[n:2 * n], outs[2 * n:]


def _cond_and_mod(pay, w_mod_s, b_mod_s):
    m_per, n_cols = pay.shape[0], w_mod_s.shape[1]
    rows = N_DEV * m_per

    def body(pay_ref, w_ref, b_ref, cond_ref, modg_ref, mod_s, sems_a, semr_a, loc_a, sems_b, semr_b, loc_b):
        _gather8_body(pay_ref, cond_ref, sems_a, semr_a, loc_a, m_per)
        cc = cond_ref[...]
        mod_s[...] = _dot_f32(cc * _sig(cc), w_ref[...]) + b_ref[...]
        _gather8_body(mod_s, modg_ref, sems_b, semr_b, loc_b, rows)

    dma7 = pltpu.SemaphoreType.DMA((7,))
    return pl.pallas_call(
        body, name="cond_and_mod",
        out_shape=[SDS((rows, pay.shape[1]), F32), SDS((N_DEV * rows, n_cols), F32)],
        in_specs=[VMEM_SPEC] * 3, out_specs=[VMEM_SPEC] * 2,
        scratch_shapes=[pltpu.VMEM((rows, n_cols), F32), dma7, dma7, pltpu.SemaphoreType.DMA,
                        dma7, dma7, pltpu.SemaphoreType.DMA],
        compiler_params=pltpu.CompilerParams(vmem_limit_bytes=VMEM_LIMIT),
    )(pay, w_mod_s, b_mod_s)


def _mod_bwd(cond64, dmod64, dmod64_my, w_mod_s, c_ctx):
    def body(c_ref, g_ref, gm_ref, w_ref, cc_ref, gw_ref, gb_ref, gcc_ref):
        cc = c_ref[...]
        act = cc * _sig(cc)
        gm = gm_ref[...]
        gw_ref[...] = _dot_f32(act, gm, (((0,), (0,)), ((), ())))
        gb_ref[...] = _colsum(g_ref[...])
        dact = _dot_f32(gm, w_ref[...], (((1,), (1,)), ((), ())))
        tot = dact[4:5, :]
        for dev in range(1, N_DEV):
            tot = tot + dact[8 * dev + 4:8 * dev + 5, :]
        c0 = cc_ref[...]
        s0 = _sig(c0)
        gcc_ref[...] = tot * (s0 * (1.0 + c0 * (1.0 - s0)))

    return pl.pallas_call(
        body, name="mod_bwd",
        out_shape=[SDS(w_mod_s.shape, F32), SDS((1, dmod64.shape[1]), F32), SDS((1, D_MODEL), F32)],
        in_specs=[VMEM_SPEC] * 5, out_specs=[VMEM_SPEC] * 3,
        compiler_params=pltpu.CompilerParams(vmem_limit_bytes=VMEM_LIMIT),
    )(cond64, dmod64, dmod64_my, w_mod_s, c_ctx)


SHARD_COLS = IN_COLS // N_CHIPS


def _in_fwd_own(pos, x, ctx, g_mix, mod_a, w_own, carry=None, then=None):
    bs, seq, _ = x.shape
    nb = seq // ROW_BLOCK + 1

    def body(pos_ref, x_ref, ctx_ref, g_ref, mod_ref, w_ref, p_ref, h_ref, w_bf):
        b, j = pl.program_id(0), pl.program_id(1)

        @pl.when((b == 0) & (j == 0))
        def _():
            w_bf[...] = w_ref[...].astype(BF16)

        is_ctx = j == 0
        xin = jnp.where(is_ctx, ctx_ref[0], x_ref[0])
        shift = jnp.where(is_ctx, mod_ref[0, 2:3, :], mod_ref[0, 0:1, :])
        scale = jnp.where(is_ctx, mod_ref[0, 3:4, :], mod_ref[0, 1:2, :])
        xn, _ = _rms(xin)
        hb = ((xn * g_ref[...]) * (1.0 + scale) + shift).astype(BF16)
        h_ref[0] = hb
        p_ref[0] = jnp.dot(hb, w_bf[...], preferred_element_type=F32)

    return _host_call(
        body, name="in_fwd_own", grid=(bs, nb), prefetch=pos,
        in_specs=[pl.BlockSpec((1, ROW_BLOCK, D_MODEL), lambda b, j, p: (b, jnp.maximum(j - 1, 0), 0)),
                  pl.BlockSpec((1, ROW_BLOCK, D_MODEL), lambda b, j, p: (b, 0, 0)),
                  pl.BlockSpec((1, D_MODEL), lambda b, j, p: (0, 0)),
                  pl.BlockSpec((1, 8, D_MODEL), lambda b, j, p: (b, 0, 0)),
                  pl.BlockSpec((D_MODEL, SHARD_COLS), lambda b, j, p: (0, 0))],
        out_specs=[pl.BlockSpec((1, ROW_BLOCK, SHARD_COLS), lambda b, j, p: (b, j, p[0])),
                   pl.BlockSpec((1, ROW_BLOCK, D_MODEL), lambda b, j, p: (b, j, 0))],
        out_shape=[SDS((bs, nb * ROW_BLOCK, IN_COLS), F32), SDS((bs, nb * ROW_BLOCK, D_MODEL), BF16)],
        scratch_shapes=[pltpu.VMEM((D_MODEL, SHARD_COLS), BF16)],
        args=(x, ctx, g_mix, mod_a, w_own), carry=carry, then=then)


def _in_fwd_rest(pos, h_all, w_in_g, p, carry=None):
    bs, rows, _ = h_all.shape
    rows_all = bs * rows
    tile = next(m * ROW_BLOCK for m in (3, 1) if rows_all % (m * ROW_BLOCK) == 0)

    def body(pos_ref, h_ref, w_ref, p_in_ref, p_ref):
        p_ref[...] = jnp.dot(h_ref[...], w_ref[0], preferred_element_type=F32)

    shard = lambda n, p: (p[0] + 1 + n) % N_CHIPS
    (p2,), carried = _host_call(
        body, name="in_fwd_rest", grid=(N_CHIPS - 1, rows_all // tile), prefetch=pos,
        in_specs=[pl.BlockSpec((tile, D_MODEL), lambda n, t, p: (t, 0)),
                  pl.BlockSpec((1, D_MODEL, SHARD_COLS), lambda n, t, p: (shard(n, p), 0, 0)),
                  ANY_SPEC],
        out_specs=[pl.BlockSpec((tile, SHARD_COLS), lambda n, t, p: (t, shard(n, p)))],
        out_shape=[SDS((rows_all, IN_COLS), F32)], aliases={2: 0},
        args=(h_all.reshape(rows_all, D_MODEL), w_in_g, p.reshape(rows_all, IN_COLS)), carry=carry)
    return p2.reshape(bs, rows, IN_COLS), carried


def _in_bwd(x, ctx, dx1, g_mix, mod_a, w_in, df_f, df_b, dv_f, dv_b, dq_f, dq_b, dpc, carry=None):
    bs, seq, _ = x.shape
    nb = seq // ROW_BLOCK + 1

    def body(x_ref, ctx_ref, dx1_ref, g_ref, mod_ref, w_ref, dff_ref, dfb_ref, dvf_ref, dvb_ref, dqf_ref, dqb_ref,
             dpc_ref, gx_ref, dp_ref, dg_ref, dmod_ref):
        b, j = pl.program_id(0), pl.program_id(1)
        is_ctx = j == 0

        @pl.when((b == 0) & (j == 0))
        def _():
            dg_ref[...] = jnp.zeros_like(dg_ref)

        @pl.when(j == 0)
        def _():
            dmod_ref[...] = jnp.zeros_like(dmod_ref)

        di = (dvf_ref[0] + dvb_ref[0]).astype(BF16)
        dq = (dqf_ref[0] + dqb_ref[0]).astype(BF16)
        dp = jnp.concatenate([dff_ref[0], dfb_ref[0], di, dq, dpc_ref[0]], axis=1)
        dp_ref[0] = dp
        dh = lax.dot_general(dp, w_ref[...], (((1,), (1,)), ((), ())), preferred_element_type=F32)
        xin = jnp.where(is_ctx, ctx_ref[0], x_ref[0])
        scale = jnp.where(is_ctx, mod_ref[0, 3:4, :], mod_ref[0, 1:2, :])
        xn, r = _rms(xin)
        g = g_ref[...]
        hn = xn * g
        d_shift = _colsum(dh)
        d_scale = _colsum(dh * hn)
        dhn = dh * (1.0 + scale)
        dg_ref[...] += _colsum(dhn * xn)
        dx = _rms_bwd(dhn * g, xn, r)

        @pl.when(is_ctx)
        def _():
            dmod_ref[0, 2:3, :] += d_shift
            dmod_ref[0, 3:4, :] += d_scale

        @pl.when(jnp.logical_not(is_ctx))
        def _():
            dmod_ref[0, 0:1, :] += d_shift
            dmod_ref[0, 1:2, :] += d_scale
            gx_ref[0] = dx + dx1_ref[0]

    def rows(w):
        return pl.BlockSpec((1, ROW_BLOCK, w), lambda b, j: (b, j, 0))

    lat = pl.BlockSpec((1, ROW_BLOCK, D_MODEL), lambda b, j: (b, jnp.maximum(j - 1, 0), 0))
    return _host_call(
        body, name="in_bwd", grid=(bs, nb),
        in_specs=[lat, pl.BlockSpec((1, ROW_BLOCK, D_MODEL), lambda b, j: (b, 0, 0)), lat,
                  pl.BlockSpec((1, D_MODEL), lambda b, j: (0, 0)),
                  pl.BlockSpec((1, 8, D_MODEL), lambda b, j: (b, 0, 0)),
                  pl.BlockSpec((D_MODEL, IN_COLS), lambda b, j: (0, 0)),
                  rows(KW), rows(KW), rows(KW), rows(KW), rows(KW), rows(KW), rows(7 * KW)],
        out_specs=[lat, rows(IN_COLS), pl.BlockSpec((1, D_MODEL), lambda b, j: (0, 0)),
                   pl.BlockSpec((1, 8, D_MODEL), lambda b, j: (b, 0, 0))],
        out_shape=[SDS(x.shape, F32), SDS((bs, nb * ROW_BLOCK, IN_COLS), BF16), SDS((1, D_MODEL), F32),
                   SDS((bs, 8, D_MODEL), F32)],
        args=(x, ctx, dx1, g_mix, mod_a, w_in, df_f, df_b, dv_f, dv_b, dq_f, dq_b, dpc), carry=carry)


def _lower_bound(lbg_ref, direction):
    return _sig(lbg_ref[0, direction:direction + 1, :] - lbg_ref[1, direction:direction + 1, :])


N_CHUNKS = ROW_BLOCK // HGRN_CHUNK


def _block_tri(upper):
    t = np.arange(ROW_BLOCK)[:, None]
    s = np.arange(ROW_BLOCK)[None, :]
    same = (t // HGRN_CHUNK) == (s // HGRN_CHUNK)
    return jnp.asarray(same & ((s >= t) if upper else (s <= t)), dtype=BF16)


TRI_SPEC = pl.BlockSpec((ROW_BLOCK, ROW_BLOCK), lambda b, j: (0, 0))


def _tri_matmul_f32(tri, g):
    g0 = g.astype(BF16)
    r1 = g - g0.astype(F32)
    g1 = r1.astype(BF16)
    g2 = (r1 - g1.astype(F32)).astype(BF16)
    return (jnp.dot(tri, g2, preferred_element_type=F32) + jnp.dot(tri, g1, preferred_element_type=F32)) \
        + jnp.dot(tri, g0, preferred_element_type=F32)


def _chunk_rows(rows):
    return jnp.concatenate([jnp.broadcast_to(r, (HGRN_CHUNK, r.shape[1])) for r in rows], axis=0)


def _block_gates(fl, q, lb, tri, upper):
    t = {}
    t["sg"] = _sig(fl)
    t["f"] = lb + (1.0 - lb) * t["sg"]
    k = 1.0 - t["f"]
    bcum = _tri_matmul_f32(tri, jnp.log(t["f"]))
    ends = [bcum[ci * HGRN_CHUNK:ci * HGRN_CHUNK + 1] if upper else bcum[(ci + 1) * HGRN_CHUNK - 1:(ci + 1) * HGRN_CHUNK]
            for ci in range(fl.shape[0] // HGRN_CHUNK)]
    mid = _chunk_rows([0.5 * r for r in ends])
    t["dec"] = [jnp.exp(r) for r in ends]
    t["e1"] = jnp.exp(bcum - mid)
    t["e2"] = jnp.exp(mid - bcum)
    t["eh"] = _chunk_rows([jnp.exp(0.5 * r) for r in ends])
    t["qi"] = q * t["e1"]
    t["ki"] = k * t["e2"]
    t["kd"] = t["ki"] * t["eh"]
    t["qe"] = t["qi"] * t["eh"]
    return t


def _hgrn_block_order(direction, nb):
    if direction == 0:
        return lambda j: j
    return lambda j: jnp.where(j == 0, 0, nb - j)


def _hgrn_fwd(p, lbg, carry=None, then=None):
    bs, rows, _ = p.shape
    nb = rows // ROW_BLOCK
    ncb = ROW_BLOCK // HGRN_CHUNK
    orders = [_hgrn_block_order(d, nb) for d in (0, 1)]
    dirs = (0, 1)

    def body(f0_ref, i0_ref, q0_ref, f1_ref, i1_ref, q1_ref, lbg_ref, tri0_ref, tri1_ref,
             o0_ref, s0_ref, o1_ref, s1_ref, st):
        @pl.when(pl.program_id(1) == 0)
        def _():
            st[...] = jnp.zeros_like(st)

        f_refs, i_refs, q_refs = (f0_ref, f1_ref), (i0_ref, i1_ref), (q0_ref, q1_ref)
        tri_refs, o_refs, s_refs = (tri0_ref, tri1_ref), (o0_ref, o1_ref), (s0_ref, s1_ref)
        chunk = lambda a, ci, h: a[ci * HGRN_CHUNK:(ci + 1) * HGRN_CHUNK, h * HEAD_DIM:(h + 1) * HEAD_DIM]
        masks = [_tri(HGRN_CHUNK, d == 1) for d in dirs]
        t = [_block_gates(f_refs[d][0], q_refs[d][0], _lower_bound(lbg_ref, d), tri_refs[d][...], d == 1) for d in dirs]
        v = [i_refs[d][0] for d in dirs]
        intra = [[[None] * N_HEADS for _ in range(ncb)] for _ in dirs]
        ds_loc = [[[None] * N_HEADS for _ in range(ncb)] for _ in dirs]
        for ci in range(ncb):
            for h in range(N_HEADS):
                for d in dirs:
                    a = jnp.where(masks[d], _dot_nt(chunk(t[d]["qi"], ci, h), chunk(t[d]["ki"], ci, h)), 0.0)
                    intra[d][ci][h] = _dot(a, chunk(v[d], ci, h))
                    ds_loc[d][ci][h] = _dot_tn(chunk(v[d], ci, h), chunk(t[d]["kd"], ci, h))
        for h in range(N_HEADS):
            ls = slice(h * HEAD_DIM, (h + 1) * HEAD_DIM)
            s = [st[d, h] for d in dirs]
            for step in range(ncb):
                for d in dirs:
                    ci = ncb - 1 - step if d == 1 else step
                    s_refs[d][0, 0, ci, h] = s[d]
                    o_refs[d][0, ci * HGRN_CHUNK:(ci + 1) * HGRN_CHUNK, ls] = (
                        intra[d][ci][h] + _dot_nt(chunk(t[d]["qe"], ci, h), s[d]))
                    s[d] = s[d] * t[d]["dec"][ci][:, ls] + ds_loc[d][ci][h]
            for d in dirs:
                st[d, h] = s[d]

    def col(d, cb):
        return pl.BlockSpec((1, ROW_BLOCK, KW), lambda b, j: (b, orders[d](j), cb))

    def outs(d):
        return [pl.BlockSpec((1, ROW_BLOCK, KW), lambda b, j: (b, orders[d](j), 0)),
                pl.BlockSpec((1, 1, ncb, N_HEADS, HEAD_DIM, HEAD_DIM), lambda b, j: (b, orders[d](j), 0, 0, 0, 0))]

    shapes = [SDS((bs, rows, KW), F32), SDS((bs, nb, ncb, N_HEADS, HEAD_DIM, HEAD_DIM), F32)]
    return _host_call(
        body, name="hgrn_fwd", grid=(bs, nb),
        in_specs=[col(0, 0), col(0, 2), col(0, 3), col(1, 1), col(1, 2), col(1, 3),
                  pl.BlockSpec((2, 2, KW), lambda b, j: (0, 0, 0)), TRI_SPEC, TRI_SPEC],
        out_specs=outs(0) + outs(1), out_shape=shapes * 2,
        scratch_shapes=[pltpu.VMEM((2, N_HEADS, HEAD_DIM, HEAD_DIM), F32)],
        args=(p, p, p, p, p, p, lbg, _block_tri(False), _block_tri(True)), carry=carry, then=then)


def _hgrn_bwd_pair(p, lbg, s_saved, do_raw, carry=None):
    bs, rows, _ = p.shape
    nb = rows // ROW_BLOCK
    ncb = ROW_BLOCK // HGRN_CHUNK
    dirs = (0, 1)
    fwd_orders = [_hgrn_block_order(d, nb) for d in dirs]
    orders = [lambda j, d=d: fwd_orders[d](nb - 1 - j) for d in dirs]
    pairs = [(ci, h) for ci in range(ncb) for h in range(N_HEADS)]

    def body(f0_ref, i0_ref, q0_ref, s0_ref, do0_ref, f1_ref, i1_ref, q1_ref, s1_ref, do1_ref,
             lbg_ref, tril_ref, triu_ref,
             df0_ref, dq0_ref, dv0_ref, dlb0_ref, df1_ref, dq1_ref, dv1_ref, dlb1_ref, dst, acc):
        b, j = pl.program_id(0), pl.program_id(1)
        f_refs, i_refs, q_refs = (f0_ref, f1_ref), (i0_ref, i1_ref), (q0_ref, q1_ref)
        s_refs, do_refs = (s0_ref, s1_ref), (do0_ref, do1_ref)
        df_refs, dq_refs, dv_refs, dlb_refs = (df0_ref, df1_ref), (dq0_ref, dq1_ref), (dv0_ref, dv1_ref), (dlb0_ref, dlb1_ref)
        tri_refs, trit_refs = (tril_ref, triu_ref), (triu_ref, tril_ref)

        @pl.when((b == 0) & (j == 0))
        def _():
            dlb0_ref[...] = jnp.zeros_like(dlb0_ref)
            dlb1_ref[...] = jnp.zeros_like(dlb1_ref)

        @pl.when(j == 0)
        def _():
            dst[...] = jnp.zeros_like(dst)

        chunk = lambda a, ci, h: a[ci * HGRN_CHUNK:(ci + 1) * HGRN_CHUNK, h * HEAD_DIM:(h + 1) * HEAD_DIM]
        rows_of = lambda ci: slice(ci * HGRN_CHUNK, (ci + 1) * HGRN_CHUNK)
        lanes_of = lambda h: slice(h * HEAD_DIM, (h + 1) * HEAD_DIM)
        grid3 = lambda: [[[None] * N_HEADS for _ in range(ncb)] for _ in dirs]
        lbs = [_lower_bound(lbg_ref, d) for d in dirs]
        masks = [_tri(HGRN_CHUNK, d == 1) for d in dirs]
        masks_t = [_tri(HGRN_CHUNK, d != 1) for d in dirs]
        t = [_block_gates(f_refs[d][0], q_refs[d][0], lbs[d], tri_refs[d][...], d == 1) for d in dirs]
        v = [i_refs[d][0] for d in dirs]
        do = [do_refs[d][0] for d in dirs]
        a_t, da, da_t, dv_in, ds_loc = (grid3() for _ in range(5))
        for ci, h in pairs:
            for d in dirs:
                a_t[d][ci][h] = _dot_nt(chunk(t[d]["ki"], ci, h), chunk(t[d]["qi"], ci, h))
        for ci, h in pairs:
            for d in dirs:
                da[d][ci][h] = _dot_nt(chunk(do[d], ci, h), chunk(v[d], ci, h))
        for ci, h in pairs:
            for d in dirs:
                da_t[d][ci][h] = _dot_nt(chunk(v[d], ci, h), chunk(do[d], ci, h))
        for ci, h in pairs:
            for d in dirs:
                acc[d, 3, rows_of(ci), lanes_of(h)] = _dot(chunk(do[d], ci, h), s_refs[d][0, 0, ci, h])
        for ci, h in pairs:
            for d in dirs:
                ds_loc[d][ci][h] = _dot_tn(chunk(do[d], ci, h), chunk(t[d]["qe"], ci, h))
        for ci, h in pairs:
            for d in dirs:
                acc[d, 0, rows_of(ci), lanes_of(h)] = _dot(jnp.where(masks[d], da[d][ci][h], 0.0),
                                                           chunk(t[d]["ki"], ci, h))
        for ci, h in pairs:
            for d in dirs:
                acc[d, 1, rows_of(ci), lanes_of(h)] = _dot(jnp.where(masks_t[d], da_t[d][ci][h], 0.0),
                                                           chunk(t[d]["qi"], ci, h))
        for ci, h in pairs:
            for d in dirs:
                dv_in[d][ci][h] = _dot(jnp.where(masks_t[d], a_t[d][ci][h], 0.0), chunk(do[d], ci, h))
        ddec = grid3()
        for h in range(N_HEADS):
            ls = lanes_of(h)
            ds = [dst[d, h] for d in dirs]
            for step in range(ncb):
                for d in dirs:
                    ci = step if d == 1 else ncb - 1 - step
                    acc[d, 2, rows_of(ci), ls] = _dot(chunk(v[d], ci, h), ds[d])
                    acc[d, 4, rows_of(ci), ls] = dv_in[d][ci][h] + _dot_nt(chunk(t[d]["kd"], ci, h), ds[d])
                    ddec[d][ci][h] = _colsum(ds[d] * s_refs[d][0, 0, ci, h])
                    ds[d] = ds[d] * t[d]["dec"][ci][:, ls] + ds_loc[d][ci][h]
            for d in dirs:
                dst[d, h] = ds[d]
        for d in dirs:
            td = t[d]
            dqi, dki, dkd, dqe = (acc[d, i] for i in range(4))
            dq_refs[d][0] = td["e1"] * (dqi + dqe * td["eh"])
            dv_refs[d][0] = acc[d, 4]
            dk = td["e2"] * (dki + dkd * td["eh"])
            dkd_kd = dkd * td["kd"]
            db = dqi * td["qi"] - dki * td["ki"] - dkd_kd + dqe * td["qe"]
            dbl = [_colsum(dkd_kd[rows_of(ci)]) + jnp.concatenate(ddec[d][ci], axis=1) * td["dec"][ci]
                   for ci in range(ncb)]
            dg = _tri_matmul_f32(trit_refs[d][...], db) + _chunk_rows(dbl)
            df = dg / td["f"] - dk
            sg = td["sg"]
            dlb_refs[d][...] += _colsum(df * (1.0 - sg))
            df_refs[d][0] = (df * (1.0 - lbs[d]) * sg * (1.0 - sg)).astype(BF16)

    def ins(d):
        col = lambda cb: pl.BlockSpec((1, ROW_BLOCK, KW), lambda b, j: (b, orders[d](j), cb))
        return [col(d), col(2), col(3),
                pl.BlockSpec((1, 1, ncb, N_HEADS, HEAD_DIM, HEAD_DIM), lambda b, j: (b, orders[d](j), 0, 0, 0, 0)),
                pl.BlockSpec((1, ROW_BLOCK, KW), lambda b, j: (b, orders[d](j), 0))]

    def outs(d):
        row = pl.BlockSpec((1, ROW_BLOCK, KW), lambda b, j: (b, orders[d](j), 0))
        return [row, row, row, pl.BlockSpec((1, KW), lambda b, j: (0, 0))]

    shapes = [SDS((bs, rows, KW), BF16), SDS((bs, rows, KW), F32), SDS((bs, rows, KW), F32), SDS((1, KW), F32)]
    return _host_call(
        body, name="hgrn_bwd", grid=(bs, nb),
        in_specs=ins(0) + ins(1) + [pl.BlockSpec((2, 2, KW), lambda b, j: (0, 0, 0)), TRI_SPEC, TRI_SPEC],
        out_specs=outs(0) + outs(1), out_shape=shapes * 2,
        scratch_shapes=[pltpu.VMEM((2, N_HEADS, HEAD_DIM, HEAD_DIM), F32), pltpu.VMEM((2, 5, ROW_BLOCK, KW), F32)],
        args=(p, p, p, s_saved[0], do_raw, p, p, p, s_saved[1], do_raw, lbg, _block_tri(False), _block_tri(True)),
        carry=carry)


def _mix_values(og, u, v, ga, gb, o_raw, gna, lng, lnb, ws_ref, bst, wpa, wpb, wo):
    t = {}
    sog = _sig(og)
    t["sog"], t["silu_og"] = sog, og * sog
    xh_l, r_l = [], []
    for h in range(N_HEADS):
        xh, r = _rms(o_raw[:, h * HEAD_DIM:(h + 1) * HEAD_DIM])
        xh_l.append(xh)
        r_l.append(r)
    t["xh"], t["r"] = jnp.concatenate(xh_l, axis=1), r_l
    gna4 = jnp.concatenate([gna] * N_HEADS, axis=1)
    t["gna4"] = gna4
    t["o_n"] = t["xh"] * gna4
    t["o_a"] = t["o_n"] * t["silu_og"]
    t["ya"] = _dot(t["o_a"], wpa)
    t["gu"], t["tu"] = _gelu(u)
    gv, t["tv"] = _gelu(v)
    mu = jnp.mean(gv, axis=-1, keepdims=True)
    cen = gv - mu
    t["rstd"] = lax.rsqrt(jnp.mean(cen * cen, axis=-1, keepdims=True) + EPS)
    t["xhat"] = cen * t["rstd"]
    vn = t["xhat"] * lng + lnb
    t["vn"] = vn
    chunks = []
    for n in range(ROW_BLOCK // SGU_CHUNK):
        rs = slice(n * SGU_CHUNK, (n + 1) * SGU_CHUNK)
        groups = []
        for g in range(N_HEADS):
            ls = slice(g * HEAD_DIM, (g + 1) * HEAD_DIM)
            groups.append(_dot(ws_ref[g], vn[rs, ls]) + bst[:, g:g + 1])
        chunks.append(jnp.concatenate(groups, axis=1))
    t["mixed"] = jnp.concatenate(chunks, axis=0)
    t["o_bm"] = t["gu"] * t["mixed"]
    t["yb"] = _dot(t["o_bm"], wpb)
    t["sa"], t["sb"] = _sig(ga), _sig(gb)
    t["merged"] = t["sa"] * t["ya"] + t["sb"] * t["yb"]
    t["mix"] = _dot(t["merged"], wo)
    return t


def _mix_in_specs(row_of):
    def col(cb):
        return pl.BlockSpec((1, ROW_BLOCK, KW), lambda b, j: (b, row_of(j), cb))
    return [col(cb) for cb in range(4, 11)]


def _mix_param_specs():
    full2 = lambda r, c: pl.BlockSpec((r, c), lambda b, j: (0, 0))
    return [full2(1, HEAD_DIM), full2(1, KW), full2(1, KW),
            pl.BlockSpec((N_HEADS, SGU_CHUNK, SGU_CHUNK), lambda b, j: (0, 0, 0)),
            full2(SGU_CHUNK, N_HEADS), full2(KW, D_MODEL), full2(KW, D_MODEL), full2(D_MODEL, D_MODEL)]


def _mix_fwd(p, o_f, o_b, x, mod_c, gna, lng, lnb, w_s, bst, wpa, wpb, wo):
    bs, seq, _ = x.shape
    nbl = seq // ROW_BLOCK

    def body(og_r, u_r, v_r, ga0_r, ga1_r, gb0_r, gb1_r, of_r, ob_r, x_r, mod_r,
             gna_r, lng_r, lnb_r, ws_r, bst_r, wpa_r, wpb_r, wo_r, x1_r):
        ga = jnp.concatenate([ga0_r[0], ga1_r[0]], axis=1)
        gb = jnp.concatenate([gb0_r[0], gb1_r[0]], axis=1)
        t = _mix_values(og_r[0], u_r[0], v_r[0], ga, gb, of_r[0] + ob_r[0], gna_r[...], lng_r[...], lnb_r[...],
                        ws_r, bst_r[...], wpa_r[...], wpb_r[...], wo_r[...])
        x1_r[0] = x_r[0] + mod_r[0, 0:1, :] * t["mix"]

    row = lambda w: pl.BlockSpec((1, ROW_BLOCK, w), lambda b, j: (b, j + 1, 0))
    lat = pl.BlockSpec((1, ROW_BLOCK, D_MODEL), lambda b, j: (b, j, 0))
    return pl.pallas_call(
        body, name="mix_fwd", grid=(bs, nbl),
        in_specs=_mix_in_specs(lambda j: j + 1) + [row(KW), row(KW), lat,
                                                    pl.BlockSpec((1, 8, D_MODEL), lambda b, j: (b, 0, 0))]
        + _mix_param_specs(),
        out_specs=lat, out_shape=SDS(x.shape, F32), compiler_params=_params(2),
    )(p, p, p, p, p, p, p, o_f, o_b, x, mod_c, gna, lng, lnb, w_s, bst, wpa, wpb, wo)


def _mix_bwd(p, o_f, o_b, dx1, mod_c, gna, lng, lnb, w_s, w_s_t, bst, wpa, wpb, wo, carry=None):
    bs, rows, _ = p.shape
    nb = rows // ROW_BLOCK

    def body(og_r, u_r, v_r, ga0_r, ga1_r, gb0_r, gb1_r, of_r, ob_r, dx1_r, mod_r,
             gna_r, lng_r, lnb_r, ws_r, bst_r, wpa_r, wpb_r, wo_r, wst_r,
             dor_r, dpc_r, dwpa_r, dwpb_r, dwo_r, dgna_r, dlng_r, dlnb_r, dws_r, dbst_r, dmod_r):
        b, j = pl.program_id(0), pl.program_id(1)

        @pl.when((b == 0) & (j == 0))
        def _():
            for r in (dwpa_r, dwpb_r, dwo_r, dgna_r, dlng_r, dlnb_r, dws_r, dbst_r):
                r[...] = jnp.zeros_like(r)

        @pl.when(j == 0)
        def _():
            dmod_r[...] = jnp.zeros_like(dmod_r)
            dor_r[...] = jnp.zeros_like(dor_r)
            dpc_r[...] = jnp.zeros_like(dpc_r)

        @pl.when(j > 0)
        def _():
            og, u, v = og_r[0], u_r[0], v_r[0]
            ga = jnp.concatenate([ga0_r[0], ga1_r[0]], axis=1)
            gb = jnp.concatenate([gb0_r[0], gb1_r[0]], axis=1)
            gna, lng = gna_r[...], lng_r[...]
            wpa, wpb, wo = wpa_r[...], wpb_r[...], wo_r[...]
            dx1 = dx1_r[0]
            dmix = mod_r[0, 0:1, :] * dx1
            dmerged = _dot_nt(dmix, wo)
            t = _mix_values(og, u, v, ga, gb, of_r[0] + ob_r[0], gna, lng, lnb_r[...],
                            ws_r, bst_r[...], wpa, wpb, wo)
            dmod_r[0, 0:1, :] += _colsum(dx1 * t["mix"])
            dwo_r[...] += _dot_tn(t["merged"], dmix)
            sa, sb = t["sa"], t["sb"]
            dya, dyb = sa * dmerged, sb * dmerged
            dga = dmerged * t["ya"] * sa * (1.0 - sa)
            dgb = dmerged * t["yb"] * sb * (1.0 - sb)
            do_a = _dot_nt(dya, wpa)
            dwpa_r[...] += _dot_tn(t["o_a"], dya)
            do_bm = _dot_nt(dyb, wpb)
            dwpb_r[...] += _dot_tn(t["o_bm"], dyb)
            sog = t["sog"]
            dog = do_a * t["o_n"] * (sog * (1.0 + og * (1.0 - sog)))
            do_n = do_a * t["silu_og"]
            dxh = do_n * t["gna4"]
            prod = do_n * t["xh"]
            dgna = jnp.zeros((1, HEAD_DIM), F32)
            dor_l = []
            for h in range(N_HEADS):
                ls = slice(h * HEAD_DIM, (h + 1) * HEAD_DIM)
                dgna = dgna + _colsum(prod[:, ls])
                dor_l.append(_rms_bwd(dxh[:, ls], t["xh"][:, ls], t["r"][h]))
            dgna_r[...] += dgna
            dor_r[0] = jnp.concatenate(dor_l, axis=1)
            du = do_bm * t["mixed"] * _dgelu(u, t["tu"])
            dmixed = do_bm * t["gu"]
            vn = t["vn"]
            dvn_chunks = []
            for n in range(ROW_BLOCK // SGU_CHUNK):
                rs = slice(n * SGU_CHUNK, (n + 1) * SGU_CHUNK)
                groups = []
                for g in range(N_HEADS):
                    ls = slice(g * HEAD_DIM, (g + 1) * HEAD_DIM)
                    dm = dmixed[rs, ls]
                    dws_r[g] += _dot_nt(dm, vn[rs, ls])
                    dbst_r[:, g:g + 1] += jnp.sum(dm, axis=1, keepdims=True)
                    groups.append(_dot(wst_r[g], dm))
                dvn_chunks.append(jnp.concatenate(groups, axis=1))
            dvn = jnp.concatenate(dvn_chunks, axis=0)
            xhat = t["xhat"]
            dlng_r[...] += _colsum(dvn * xhat)
            dlnb_r[...] += _colsum(dvn)
            dxhat = dvn * lng
            dgv = t["rstd"] * (dxhat - jnp.mean(dxhat, axis=-1, keepdims=True)
                               - xhat * jnp.mean(dxhat * xhat, axis=-1, keepdims=True))
            dv = dgv * _dgelu(v, t["tv"])
            dpc_r[0] = jnp.concatenate([dog, du, dv, dga, dgb], axis=1).astype(BF16)

    row = lambda w: pl.BlockSpec((1, ROW_BLOCK, w), lambda b, j: (b, j, 0))
    lat = pl.BlockSpec((1, ROW_BLOCK, D_MODEL), lambda b, j: (b, jnp.maximum(j - 1, 0), 0))
    full2 = lambda r, c: pl.BlockSpec((r, c), lambda b, j: (0, 0))
    ws_spec = pl.BlockSpec((N_HEADS, SGU_CHUNK, SGU_CHUNK), lambda b, j: (0, 0, 0))
    return _host_call(
        body, name="mix_bwd", grid=(bs, nb),
        in_specs=_mix_in_specs(lambda j: j) + [row(KW), row(KW), lat,
                                                pl.BlockSpec((1, 8, D_MODEL), lambda b, j: (b, 0, 0))]
        + _mix_param_specs() + [ws_spec],
        out_specs=[row(KW), row(7 * KW), full2(KW, D_MODEL), full2(KW, D_MODEL), full2(D_MODEL, D_MODEL),
                   full2(1, HEAD_DIM), full2(1, KW), full2(1, KW), ws_spec, full2(SGU_CHUNK, N_HEADS),
                   pl.BlockSpec((1, 8, D_MODEL), lambda b, j: (b, 0, 0))],
        out_shape=[SDS((bs, rows, KW), F32), SDS((bs, rows, 7 * KW), BF16), SDS((KW, D_MODEL), F32),
                   SDS((KW, D_MODEL), F32), SDS((D_MODEL, D_MODEL), F32), SDS((1, HEAD_DIM), F32),
                   SDS((1, KW), F32), SDS((1, KW), F32), SDS((N_HEADS, SGU_CHUNK, SGU_CHUNK), F32),
                   SDS((SGU_CHUNK, N_HEADS), F32), SDS((bs, 8, D_MODEL), F32)],
        args=(p, p, p, p, p, p, p, o_f, o_b, dx1, mod_c, gna, lng, lnb, w_s, bst, wpa, wpb, wo, w_s_t), carry=carry)


def _ffn(x1, target, mod_c, g_ffn, g_final, w_up, w_down):
    bs, seq, _ = x1.shape
    nbl = seq // ROW_BLOCK

    def body(x1_r, tg_r, mod_r, gf_r, gl_r, wu_r, wd_r,
             dx1_r, h2_r, dab_r, hid_r, dffn_r, loss_r, dgl_r, dgf_r, dmod_r):
        b, j = pl.program_id(0), pl.program_id(1)

        @pl.when((b == 0) & (j == 0))
        def _():
            for r in (loss_r, dgl_r, dgf_r):
                r[...] = jnp.zeros_like(r)

        @pl.when(j == 0)
        def _():
            dmod_r[...] = jnp.zeros_like(dmod_r)

        x1 = x1_r[0]
        shift, scale, gate = mod_r[0, 1:2, :], mod_r[0, 2:3, :], mod_r[0, 3:4, :]
        gf, gl = gf_r[...], gl_r[...]
        xn2, r2 = _rms(x1)
        hn2 = xn2 * gf
        h2 = (hn2 * (1.0 + scale) + shift).astype(BF16)
        h2_r[0] = h2
        ab = jnp.dot(h2, wu_r[...], preferred_element_type=F32)
        a, bb = ab[:, :D_FF], ab[:, D_FF:]
        sa = _sig(a)
        silu_a = a * sa
        hid = (silu_a * bb).astype(BF16)
        hid_r[0] = hid
        ffn = jnp.dot(hid, wd_r[...], preferred_element_type=F32)
        x2 = x1 + gate * ffn
        xn3, r3 = _rms(x2)
        err = xn3 * gl - tg_r[0]
        loss_r[...] += 0.5 * jnp.sum(jnp.mean(err * err, axis=-1, keepdims=True), axis=0, keepdims=True)
        dy = err * (1.0 / D_MODEL)
        dgl_r[...] += _colsum(dy * xn3)
        dx2 = _rms_bwd(dy * gl, xn3, r3)
        dmod_r[0, 3:4, :] += _colsum(dx2 * ffn)
        dffn = (gate * dx2).astype(BF16)
        dffn_r[0] = dffn
        dhid = lax.dot_general(dffn, wd_r[...], (((1,), (1,)), ((), ())), preferred_element_type=F32)
        da = dhid * bb * (sa * (1.0 + a * (1.0 - sa)))
        db = dhid * silu_a
        dab = jnp.concatenate([da, db], axis=1).astype(BF16)
        dab_r[0] = dab
        dh2 = lax.dot_general(dab, wu_r[...], (((1,), (1,)), ((), ())), preferred_element_type=F32)
        dmod_r[0, 1:2, :] += _colsum(dh2)
        dmod_r[0, 2:3, :] += _colsum(dh2 * hn2)
        dhn2 = dh2 * (1.0 + scale)
        dgf_r[...] += _colsum(dhn2 * xn2)
        dx1_r[0] = dx2 + _rms_bwd(dhn2 * gf, xn2, r2)

    lat = lambda w: pl.BlockSpec((1, ROW_BLOCK, w), lambda b, j: (b, j, 0))
    full2 = lambda r, c: pl.BlockSpec((r, c), lambda b, j: (0, 0))
    mod_spec = pl.BlockSpec((1, 8, D_MODEL), lambda b, j: (b, 0, 0))
    return pl.pallas_call(
        body, name="ffn", grid=(bs, nbl),
        in_specs=[lat(D_MODEL), lat(D_MODEL), mod_spec, full2(1, D_MODEL), full2(1, D_MODEL),
                  full2(D_MODEL, 2 * D_FF), full2(D_FF, D_MODEL)],
        out_specs=[lat(D_MODEL), lat(D_MODEL), lat(2 * D_FF), lat(D_FF), lat(D_MODEL),
                   full2(1, 1), full2(1, D_MODEL), full2(1, D_MODEL), mod_spec],
        out_shape=[SDS(x1.shape, F32), SDS(x1.shape, BF16), SDS((bs, seq, 2 * D_FF), BF16),
                   SDS((bs, seq, D_FF), BF16), SDS(x1.shape, BF16), SDS((1, 1), F32),
                   SDS((1, D_MODEL), F32), SDS((1, D_MODEL), F32), SDS((bs, 8, D_MODEL), F32)],
        compiler_params=_params(2),
    )(x1, target, mod_c, g_ffn, g_final, w_up, w_down)


def _row_tile(rows):
    return next(m * ROW_BLOCK for m in (4, 2, 1) if rows % (m * ROW_BLOCK) == 0)


def _matmul_tn(a, b, n_blocks, tk, name, carry=None):
    t, m = a.shape
    n = b.shape[1]
    tn = n // n_blocks

    def body(a_ref, b_ref, o_ref):
        @pl.when(pl.program_id(1) == 0)
        def _():
            o_ref[...] = jnp.zeros_like(o_ref)
        o_ref[0] += _dot_tn(a_ref[...], b_ref[...])

    (out,), carried = _host_call(
        body, name=name, grid=(n_blocks, t // tk),
        in_specs=[pl.BlockSpec((tk, m), lambda i, k: (k, 0)), pl.BlockSpec((tk, tn), lambda i, k: (k, i))],
        out_specs=[pl.BlockSpec((1, m, tn), lambda i, k: (i, 0, 0))],
        out_shape=[SDS((n_blocks, m, tn), F32)], args=(a, b), carry=carry)
    return out if carry is None else (out, carried)


SMALL_ROWS = 80
ROW_CCTX = 3


def _small_reduce(gathered, lbg):
    def body(g_ref, lbg_ref, s_ref, dgam_ref):
        tot = g_ref[0:SMALL_ROWS, :]
        for dev in range(1, N_DEV):
            tot = tot + g_ref[dev * SMALL_ROWS:(dev + 1) * SMALL_ROWS, :]
        s_ref[...] = tot
        cc = g_ref[ROW_CCTX:ROW_CCTX + 1, :]
        for dev in range(2, N_DEV, 2):
            cc = cc + g_ref[dev * SMALL_ROWS + ROW_CCTX:dev * SMALL_ROWS + ROW_CCTX + 1, :]
        s_ref[ROW_CCTX:ROW_CCTX + 1, :] = cc
        dlb = tot[7:8, :]
        for d in range(2):
            s0 = _sig(lbg_ref[0, d:d + 1, :] - lbg_ref[1, d:d + 1, :])
            dgam_ref[d:d + 1, :] = dlb[:, d * KW:(d + 1) * KW] * s0 * (1.0 - s0)

    return pl.pallas_call(
        body, name="small_reduce", out_shape=[SDS((SMALL_ROWS, D_MODEL), F32), SDS((2, KW), F32)],
        in_specs=[VMEM_SPEC] * 2, out_specs=[VMEM_SPEC] * 2,
    )(gathered, lbg)


def _pad_cols(a, width):
    return jnp.pad(a, ((0, 0), (0, width - a.shape[1])))


def kernel(x, c, ctx, c_ctx, w_mod, b_mod, g_mix, g_ffn, w_in, lb_gamma, g_norm_a, ln_v_g, ln_v_b, w_s, b_s, w_pa, w_pb, w_o, w_up, w_down, g_final, loss_target, m_c_ctx, m_w_mod, m_b_mod, m_g_mix, m_g_ffn, m_w_in, m_lb_gamma, m_g_norm_a, m_ln_v_g, m_ln_v_b, m_w_s, m_b_s, m_w_pa, m_w_pb, m_w_o, m_w_up, m_w_down, m_g_final, v_c_ctx, v_w_mod, v_b_mod, v_g_mix, v_g_ffn, v_w_in, v_lb_gamma, v_g_norm_a, v_ln_v_g, v_ln_v_b, v_w_s, v_b_s, v_w_pa, v_w_pb, v_w_o, v_w_up, v_w_down, v_g_final):
    ax, ay, ac = lax.axis_index("x"), lax.axis_index("y"), lax.axis_index("c")
    kc = 2 * ax + ay
    dev = 2 * kc + ac
    pos = jnp.stack([kc, ac]).astype(jnp.int32)
    bs, seq, _ = x.shape
    assert bs <= 4 and ctx.shape[1] == ROW_BLOCK and seq % ROW_BLOCK == 0
    mod_cols = w_mod.shape[2]

    lbg_row = _pad_cols(lb_gamma.reshape(1, -1), D_MODEL)
    pay1 = jnp.concatenate([c, jnp.zeros((4 - bs, D_MODEL), F32), c_ctx[None, :], lbg_row,
                            jnp.zeros((2, D_MODEL), F32)], axis=0)
    b_mod_s = lax.dynamic_slice(b_mod, (0, kc * mod_cols), (1, mod_cols))
    cond64, mod_g = _cond_and_mod(pay1, w_mod[0], b_mod_s)
    lbg_full = cond64.reshape(N_DEV, 8, D_MODEL)[0::2, 5, :KW].reshape(N_CHIPS, 2, 2, HEAD_DIM)
    lbg_full = jnp.transpose(lbg_full, (1, 2, 0, 3)).reshape(2, 2, KW)
    mod_g = mod_g.reshape(N_DEV, 64, mod_cols)[0::2]
    shards = [w_in[0], w_up[0], w_pa[0], w_pb[0], w_o[0], w_down[0]]
    bufs = _cast_bf16(pos, shards)
    mod_full = jnp.transpose(mod_g, (1, 0, 2)).reshape(64, N_CHIPS * mod_cols)
    mod_mine = lax.dynamic_slice(mod_full, (dev * 8, 0), (8, 6 * D_MODEL)).reshape(8, 6, D_MODEL)
    mod, mc = mod_mine[:bs], mod_mine[4]
    zeros4 = jnp.zeros((bs, 4, D_MODEL), F32)
    mod_a = jnp.concatenate([mod[:, 0:2], jnp.broadcast_to(mc[None, 0:2], (bs, 2, D_MODEL)), zeros4], axis=1)
    mod_c = jnp.concatenate([mod[:, 2:6], zeros4], axis=1)

    def cols_major(a):
        return jnp.transpose(a, (1, 0, 2)).reshape(a.shape[1], -1)

    gna, lng, lnb = g_norm_a, ln_v_g, ln_v_b
    ws3 = w_s[0]
    ws3_t = jnp.transpose(ws3, (0, 2, 1))
    bst = jnp.transpose(b_s[0])

    (p, h_all), (w_in_g,) = _in_fwd_own(pos, x, ctx, g_mix, mod_a, w_in[0], carry=_carry_gather_send(bufs[:1]),
                                        then=_carry_gather_forward(bufs[:1]))
    p, sent_up = _in_fwd_rest(pos, h_all, w_in_g, p, carry=_carry_gather_send(bufs[1:2]))
    w_in_f = cols_major(w_in_g)
    fwd_rest = _carry_gather_forward(bufs[2:])
    then_rest = _Carry([], [], {}, fwd_rest.sems, lambda i, o, s: fwd_rest.copies(i[1:], o[1:], s))
    (o_f, s_f, o_b, s_b), gathered = _hgrn_fwd(
        p, lbg_full, carry=_merge_carries(_carry_gather_forward(sent_up), _carry_gather_send(bufs[2:])),
        then=then_rest)
    w_up_f, w_pa_f, w_pb_f = (cols_major(a) for a in gathered[:3])
    w_o_f = gathered[3].reshape(-1, D_MODEL)
    w_down_f = gathered[4].reshape(-1, D_MODEL)
    x1 = _mix_fwd(p, o_f, o_b, x, mod_c, gna, lng, lnb, ws3, bst, w_pa_f, w_pb_f, w_o_f)
    dx1, h2, dab, hid, dffn, loss_part, dg_final, dg_ffn, dmod_ffn = _ffn(
        x1, loss_target, mod_c, g_ffn, g_final[None, :], w_up_f, w_down_f)
    rows_lat = bs * seq
    tk_lat = _row_tile(rows_lat)
    dw_up = _matmul_tn(h2.reshape(rows_lat, D_MODEL), dab.reshape(rows_lat, 2 * D_FF), N_CHIPS, tk_lat, "dw_up")
    dw_down = _matmul_tn(hid.reshape(rows_lat, D_FF), dffn.reshape(rows_lat, D_MODEL), 1, tk_lat, "dw_down")

    def shard_major(a):
        return jnp.transpose(a.reshape(a.shape[0], N_CHIPS, -1), (1, 0, 2))

    ffn_names, mix_names = ["w_up", "w_down"], ["w_pa", "w_pb", "w_o"]
    part_ffn = [dw_up, dw_down.reshape(N_CHIPS, -1, D_MODEL)]
    (do_raw, dpc, dw_pa, dw_pb, dw_o, dgna, dlng, dlnb, dws, dbst, dmod_mix), sib_ffn = _mix_bwd(
        p, o_f, o_b, dx1, mod_c, gna, lng, lnb, ws3, ws3_t, bst, w_pa_f, w_pb_f, w_o_f,
        carry=_carry_sibling_halves(part_ffn))
    cpbf_ffn = _rs_add_halves(pos, part_ffn, sib_ffn, "rs_add_ffn")
    part_mix = [shard_major(dw_pa), shard_major(dw_pb), dw_o.reshape(N_CHIPS, -1, D_MODEL)]
    (df_f, dq_f, dv_f, dlb0, df_b, dq_b, dv_b, dlb1), got = _hgrn_bwd_pair(
        p, lbg_full, (s_f, s_b), do_raw,
        carry=_merge_carries(_carry_to_owner(cpbf_ffn), _carry_sibling_halves(part_mix)))
    own_ffn, sib_mix = got[:2], got[2:]
    half_ffn = _rs_sum_owner(pos, part_ffn, sib_ffn, own_ffn, "rs_sum_ffn")
    cpbf_mix = _rs_add_halves(pos, part_mix, sib_mix, "rs_add_mix")
    (grad_x, dp, dg_mix, dmod_in), _ = _in_bwd(x, ctx, dx1, g_mix, mod_a, w_in_f, df_f, df_b, dv_f, dv_b, dq_f, dq_b,
                                               dpc)

    rows_all = dp.shape[0] * dp.shape[1]
    tk_all = _row_tile(rows_all)
    dw_in, got = _matmul_tn(h_all.reshape(rows_all, D_MODEL), dp.reshape(rows_all, IN_COLS), N_CHIPS, tk_all, "dw_in",
                            carry=_merge_carries(_carry_join_halves(half_ffn), _carry_to_owner(cpbf_mix)))
    g_ffn_w, own_mix = got[:2], got[2:]
    half_mix = _rs_sum_owner(pos, part_mix, sib_mix, own_mix, "rs_sum_mix")

    dmod_mine = jnp.concatenate([dmod_in[:, 0], dmod_in[:, 1], dmod_mix[:, 0], dmod_ffn[:, 1], dmod_ffn[:, 2],
                                 dmod_ffn[:, 3]], axis=1)
    dmc = jnp.concatenate([jnp.sum(dmod_in[:, 2], axis=0), jnp.sum(dmod_in[:, 3], axis=0),
                           jnp.zeros((4 * D_MODEL,), F32)])[None, :]
    pay3 = jnp.concatenate([dmod_mine, jnp.zeros((4 - bs, 6 * D_MODEL), F32), dmc,
                            jnp.zeros((3, 6 * D_MODEL), F32)], axis=0)
    dmod64, got = _all_gather8(pay3, "gather_dmod", carry=_merge_carries(_carry_sibling_halves([dw_in]),
                                                                          _carry_join_halves(half_mix)))
    sib_in, g_mix_w = got[:1], got[1:]
    cpbf_in = _rs_add_halves(pos, [dw_in], sib_in, "rs_add_w_in")
    dmod64_my = lax.dynamic_slice(dmod64, (0, kc * mod_cols), (64, mod_cols))
    g_w_mod, g_b_mod, g_cctx_part = _mod_bwd(cond64, dmod64, dmod64_my, w_mod[0], c_ctx[None, :])

    def row(*parts):
        return _pad_cols(jnp.concatenate([q.reshape(1, -1) for q in parts], axis=1), D_MODEL)

    small_rows = [dg_mix, dg_ffn, dg_final, g_cctx_part, row(dgna), row(dlng, dlnb), row(jnp.transpose(dbst)),
                  row(dlb0, dlb1), row(loss_part), jnp.zeros((7, D_MODEL), F32), dws.reshape(64, D_MODEL)]
    pay4 = jnp.concatenate(small_rows, axis=0)
    tot, dgam0 = _small_reduce(_all_gather8(pay4, "gather_small"), lbg_full)

    own_sems, own_src, own_land, own_token = _owner_send_start(cpbf_in[0], after=tot)

    rest_names = ["w_up", "w_pa", "w_pb", "w_o", "w_down", "w_mod"]
    rest_w = shards[1:] + [w_mod[0]]
    rest_g = [g_ffn_w[0], g_mix_w[0] + own_token[0, 0], g_mix_w[1], g_mix_w[2], g_ffn_w[1], g_w_mod]
    rest_m = [m_w_up[0], m_w_pa[0], m_w_pb[0], m_w_o[0], m_w_down[0], m_w_mod[0]]
    rest_v = [v_w_up[0], v_w_pa[0], v_w_pb[0], v_w_o[0], v_w_down[0], v_w_mod[0]]
    (ds_r, m2s_r, v2s_r), _ = _adamw_group(rest_w, rest_g, rest_m, rest_v, "adamw_rest")
    res = {}
    for name, g, d, m2, v2 in zip(rest_names, rest_g, ds_r, m2s_r, v2s_r):
        res[name] = (g[None], d[None], m2[None], v2[None])
    own_in = [_owner_send_wait(own_sems, own_src, own_land, after=(ds_r[0],))]
    g_in_w = _comm_call("rs_join_w_in",
                        _carry_join_halves(_rs_sum_owner(pos, [dw_in], sib_in, own_in, "rs_sum_w_in")))
    d, m2, v2 = _adamw_big(shards[0], g_in_w[0], m_w_in[0], v_w_in[0], "adamw_w_in")
    res["w_in"] = (g_in_w[0][None], d[None], m2[None], v2[None])

    loss = tot[8, 0]
    dgam_full = jnp.stack([dgam0, -dgam0])
    g_lbg = lax.dynamic_slice(dgam_full, (0, 0, kc * HEAD_DIM), (2, 2, HEAD_DIM))

    small = [
        ("c_ctx", c_ctx[None, :], tot[3:4], m_c_ctx, v_c_ctx),
        ("b_mod", b_mod, g_b_mod, m_b_mod, v_b_mod),
        ("g_mix", g_mix, tot[0:1], m_g_mix, v_g_mix),
        ("g_ffn", g_ffn, tot[1:2], m_g_ffn, v_g_ffn),
        ("lb_gamma", lb_gamma.reshape(4, HEAD_DIM), g_lbg.reshape(4, HEAD_DIM), m_lb_gamma, v_lb_gamma),
        ("g_norm_a", g_norm_a, tot[4:5, :HEAD_DIM], m_g_norm_a, v_g_norm_a),
        ("ln_v_g", ln_v_g, tot[5:6, :KW], m_ln_v_g, v_ln_v_g),
        ("ln_v_b", ln_v_b, tot[5:6, KW:], m_ln_v_b, v_ln_v_b),
        ("w_s", w_s.reshape(N_HEADS * SGU_CHUNK, SGU_CHUNK), tot[16:80].reshape(N_HEADS * SGU_CHUNK, SGU_CHUNK),
         m_w_s, v_w_s),
        ("b_s", b_s[0], tot[6:7, :KW].reshape(N_HEADS, SGU_CHUNK), m_b_s, v_b_s),
        ("g_final", g_final[None, :], tot[2:3], m_g_final, v_g_final),
    ]
    ws_, gs_ = [s[1] for s in small], [s[2] for s in small]
    ms_ = [s[3].reshape(s[1].shape) for s in small]
    vs_ = [s[4].reshape(s[1].shape) for s in small]
    ds_, m2s_, v2s_ = _adamw_small(ws_, gs_, ms_, vs_)
    for (name, _, g, m, _), d, m2, v2 in zip(small, ds_, m2s_, v2s_):
        res[name] = tuple(t.reshape(m.shape) for t in (g, d, m2, v2))

    order = ["c_ctx", "w_mod", "b_mod", "g_mix", "g_ffn", "w_in", "lb_gamma", "g_norm_a", "ln_v_g", "ln_v_b",
             "w_s", "b_s", "w_pa", "w_pb", "w_o", "w_up", "w_down", "g_final"]
    outs = [loss, grad_x]
    for part in range(4):
        outs += [res[n][part] for n in order]
    return tuple(outs)
```

```python
import functools
import math

import jax
import jax.numpy as jnp
import numpy as np
from jax import lax
from jax.experimental import pallas as pl
from jax.experimental.pallas import tpu as pltpu

F32 = jnp.float32
BF16 = jnp.bfloat16
SDS = jax.ShapeDtypeStruct
MESH = pl.DeviceIdType.MESH

EPS = 1e-6
D_MODEL = 1024
N_HEADS = 4
HEAD_DIM = 128
KW = N_HEADS * HEAD_DIM
IN_COLS = 11 * KW
D_FF = 2816
HGRN_CHUNK = 64
SGU_CHUNK = 128
ROW_BLOCK = 256
N_CHIPS = 4
N_DEV = 8
V7X_VMEM_BYTES = 64 * 1024 * 1024
VMEM_LIMIT = V7X_VMEM_BYTES - 6 * 1024 * 1024

ADAM_LR, ADAM_B1, ADAM_B2, ADAM_EPS, ADAM_WD, ADAM_STEP = 0.001, 0.9, 0.999, 1e-08, 0.01, 10
GELU_C0 = math.sqrt(2.0 / math.pi)
GELU_C1 = 0.044715

VMEM_SPEC = pl.BlockSpec(memory_space=pltpu.VMEM)
ANY_SPEC = pl.BlockSpec(memory_space=pl.ANY)


def _params(n_grid):
    return pltpu.CompilerParams(dimension_semantics=("arbitrary",) * n_grid, vmem_limit_bytes=VMEM_LIMIT)


def _sig(x):
    return 0.5 * jnp.tanh(0.5 * x) + 0.5


def _gelu(x):
    t = jnp.tanh(GELU_C0 * (x + GELU_C1 * x * x * x))
    return 0.5 * x * (1.0 + t), t


def _dgelu(x, t):
    return 0.5 * (1.0 + t) + 0.5 * x * (1.0 - t * t) * GELU_C0 * (1.0 + 3.0 * GELU_C1 * x * x)


def _dot(a, b):
    return jnp.dot(a.astype(BF16), b.astype(BF16), preferred_element_type=F32)


def _dot_nt(a, b):
    return lax.dot_general(a.astype(BF16), b.astype(BF16), (((1,), (1,)), ((), ())), preferred_element_type=F32)


def _dot_tn(a, b):
    return lax.dot_general(a.astype(BF16), b.astype(BF16), (((0,), (0,)), ((), ())), preferred_element_type=F32)


def _dot_f32(a, b, dims=(((1,), (0,)), ((), ()))):
    return lax.dot_general(a, b, dims, precision=lax.Precision.HIGHEST, preferred_element_type=F32)


def _rms(x):
    r = lax.rsqrt(jnp.mean(x * x, axis=-1, keepdims=True) + EPS)
    return x * r, r


def _rms_bwd(dxn, xn, r):
    return r * (dxn - xn * jnp.mean(dxn * xn, axis=-1, keepdims=True))


def _colsum(a):
    return jnp.sum(a, axis=0, keepdims=True)


def _tri(n, upper):
    t = lax.broadcasted_iota(jnp.int32, (n, n), 0)
    s = lax.broadcasted_iota(jnp.int32, (n, n), 1)
    return (s >= t) if upper else (s <= t)


def _all_gather8(x_shard, name, carry=None, then=None):
    m_per, n = x_shard.shape
    n_ci = len(carry.ins) if carry else 0
    n_co = len(carry.outs) if carry else 0
    n_cs = len(carry.sems) if carry else 0

    def body(*refs):
        x_ref, cins = refs[0], refs[1:1 + n_ci]
        out_ref, couts = refs[1 + n_ci], refs[2 + n_ci:2 + n_ci + n_co]
        send_sems, recv_sems, local_sem = refs[2 + n_ci + n_co:5 + n_ci + n_co]
        csems = refs[5 + n_ci + n_co:5 + n_ci + n_co + n_cs]
        tsems = refs[5 + n_ci + n_co + n_cs:]
        if carry:
            _start_all(carry.copies(cins, couts, csems)[0])
        _gather8_body(x_ref, out_ref, send_sems, recv_sems, local_sem, m_per)
        if carry:
            _wait_all(carry.copies(cins, couts, csems)[1])
        if then:
            _start_all(then.copies(couts, couts, tsems)[0])
            _wait_all(then.copies(couts, couts, tsems)[1])

    res = pl.pallas_call(
        body, name=name, out_shape=[SDS((N_DEV * m_per, n), x_shard.dtype)] + (carry.outs if carry else []),
        in_specs=[VMEM_SPEC] + [ANY_SPEC] * n_ci, out_specs=[VMEM_SPEC] + [ANY_SPEC] * n_co,
        input_output_aliases={1 + i: 1 + o for i, o in carry.alias.items()} if carry else {},
        scratch_shapes=[pltpu.SemaphoreType.DMA((7,)), pltpu.SemaphoreType.DMA((7,)), pltpu.SemaphoreType.DMA]
        + (carry.sems if carry else []) + (then.sems if then else []),
    )(x_shard, *(carry.ins if carry else []))
    return res[0] if carry is None else (res[0], list(res[1:]))


def _gather8_body(x_ref, out_ref, send_sems, recv_sems, local_sem, m_per):
    x, y, c = lax.axis_index("x"), lax.axis_index("y"), lax.axis_index("c")
    me, sibling = (x, y, c), (x, y, 1 - c)
    chips = [(1 - x, y), (x, 1 - y), (1 - x, 1 - y)]

    def rows(px, py, pc):
        return out_ref.at[pl.ds((4 * px + 2 * py + pc) * m_per, m_per), :]

    def copy(k, block, to, src=None):
        return pltpu.make_async_remote_copy(
            src_ref=rows(*block) if src is None else src, dst_ref=rows(*block),
            send_sem=send_sems.at[k], recv_sem=recv_sems.at[k], device_id=to, device_id_type=MESH)

    mine = pltpu.make_async_copy(x_ref, rows(*me), local_sem)
    mine.start()
    first = [copy(0, me, sibling, src=x_ref)]
    first += [copy(1 + j, me, (*chip, c), src=x_ref) for j, chip in enumerate(chips)]
    for cp in first:
        cp.start()
    passed = [copy(4 + j, (*chip, c), sibling) for j, chip in enumerate(chips)]
    for j, chip in enumerate(chips):
        copy(1 + j, (*chip, c), me).wait_recv()
        passed[j].start()
    copy(0, sibling, me).wait_recv()
    for j, chip in enumerate(chips):
        copy(4 + j, (*chip, 1 - c), me).wait_recv()
    for cp in first + passed:
        cp.wait_send()
    mine.wait()


def _mesh_pos():
    x, y, c = lax.axis_index("x"), lax.axis_index("y"), lax.axis_index("c")
    chips = [(1 - x, y), (x, 1 - y), (1 - x, 1 - y)]
    return x, y, c, 2 * x + y, (x, y, 1 - c), chips


def _half_rows(c, rh):
    return pl.ds(pl.multiple_of(c * rh, 16), rh)


class _Carry:
    def __init__(self, ins, outs, alias, sems, copies):
        self.ins, self.outs, self.alias, self.sems, self.copies = list(ins), list(outs), dict(alias), list(sems), copies


def _remote(src, dst, send, recv, to):
    return functools.partial(pltpu.make_async_remote_copy, src_ref=src, dst_ref=dst, send_sem=send, recv_sem=recv,
                             device_id=to, device_id_type=MESH)


def _carry_gather_send(bufs):
    n = len(bufs)

    def copies(ins, outs, sems):
        x, y, c, kc, sibling, chips = _mesh_pos()
        starts, waits = [], []
        for wi in range(n):
            rh = outs[wi].shape[1] // 2
            for jj, chip in enumerate(chips):
                mine = outs[wi].at[kc, _half_rows(c, rh), :]
                cp = _remote(mine, mine, sems[0].at[wi, jj], sems[1].at[wi, jj], (*chip, c))
                starts.append(cp)
                waits.append((cp, "send"))
                theirs = outs[wi].at[2 * chip[0] + chip[1], _half_rows(c, rh), :]
                waits.append((_remote(theirs, theirs, sems[0].at[wi, jj], sems[1].at[wi, jj], (*chip, c)), "recv"))
        return starts, waits

    return _Carry(bufs, [SDS(b.shape, b.dtype) for b in bufs], {i: i for i in range(n)},
                  [pltpu.SemaphoreType.DMA((n, 3)), pltpu.SemaphoreType.DMA((n, 3))], copies)


def _carry_gather_forward(bufs):
    n = len(bufs)

    def copies(ins, outs, sems):
        x, y, c, kc, sibling, chips = _mesh_pos()
        starts, waits = [], []
        for wi in range(n):
            rh = outs[wi].shape[1] // 2
            for jj, chip in enumerate(chips):
                got = outs[wi].at[2 * chip[0] + chip[1], _half_rows(c, rh), :]
                cp = _remote(got, got, sems[0].at[wi, jj], sems[1].at[wi, jj], sibling)
                starts.append(cp)
                waits.append((cp, "send"))
                other = outs[wi].at[2 * chip[0] + chip[1], _half_rows(1 - c, rh), :]
                waits.append((_remote(other, other, sems[0].at[wi, jj], sems[1].at[wi, jj], sibling), "recv"))
        return starts, waits

    return _Carry(bufs, [SDS(b.shape, b.dtype) for b in bufs], {i: i for i in range(n)},
                  [pltpu.SemaphoreType.DMA((n, 3)), pltpu.SemaphoreType.DMA((n, 3))], copies)


def _carry_sibling_halves(grads):
    n = len(grads)

    def copies(ins, outs, sems):
        x, y, c, kc, sibling, chips = _mesh_pos()
        cps = [_remote(ins[wi].at[:, _half_rows(1 - c, ins[wi].shape[1] // 2), :], outs[wi],
                       sems[0].at[wi], sems[1].at[wi], sibling) for wi in range(n)]
        return cps, [(cp, "both") for cp in cps]

    return _Carry(grads, [SDS((N_CHIPS, g.shape[1] // 2, g.shape[2]), F32) for g in grads], {},
                  [pltpu.SemaphoreType.DMA((n,)), pltpu.SemaphoreType.DMA((n,))], copies)


def _carry_to_owner(cpbfs):
    n = len(cpbfs)

    def copies(ins, outs, sems):
        x, y, c, kc, sibling, chips = _mesh_pos()
        starts, waits = [], []
        for wi in range(n):
            for jj, chip in enumerate(chips):
                cp = _remote(ins[wi].at[2 * chip[0] + chip[1]], outs[wi].at[kc],
                             sems[0].at[wi, jj], sems[1].at[wi, jj], (*chip, c))
                starts.append(cp)
                waits.append((cp, "send"))
                slot = outs[wi].at[2 * chip[0] + chip[1]]
                waits.append((_remote(slot, slot, sems[0].at[wi, jj], sems[1].at[wi, jj], (*chip, c)), "recv"))
        return starts, waits

    return _Carry(cpbfs, [SDS(g.shape, BF16) for g in cpbfs], {},
                  [pltpu.SemaphoreType.DMA((n, 3)), pltpu.SemaphoreType.DMA((n, 3))], copies)


def _carry_join_halves(bufs):
    n = len(bufs)

    def copies(ins, outs, sems):
        x, y, c, kc, sibling, chips = _mesh_pos()
        cps = []
        for wi in range(n):
            mine = outs[wi].at[_half_rows(c, outs[wi].shape[0] // 2), :]
            cps.append(_remote(mine, mine, sems[0].at[wi], sems[1].at[wi], sibling))
        return cps, [(cp, "both") for cp in cps]

    return _Carry(bufs, [SDS(b.shape, F32) for b in bufs], {i: i for i in range(n)},
                  [pltpu.SemaphoreType.DMA((n,)), pltpu.SemaphoreType.DMA((n,))], copies)


def _merge_carries(*carries):
    ins, outs, alias, sems, parts = [], [], {}, [], []
    for cy in carries:
        parts.append((len(ins), len(cy.ins), len(outs), len(cy.outs), len(sems), len(cy.sems), cy.copies))
        alias.update({len(ins) + i: len(outs) + o for i, o in cy.alias.items()})
        ins += cy.ins
        outs += cy.outs
        sems += cy.sems

    def copies(i, o, s):
        starts, waits = [], []
        for i0, ni, o0, no, s0, ns, fn in parts:
            st, wt = fn(i[i0:i0 + ni], o[o0:o0 + no], s[s0:s0 + ns])
            starts += st
            waits += wt
        return starts, waits

    return _Carry(ins, outs, alias, sems, copies)


def _start_all(starts):
    for cp in starts:
        cp().start()


def _wait_all(waits):
    for cp, which in waits:
        if which == "send":
            cp().wait_send()
        elif which == "recv":
            cp().wait_recv()
        else:
            cp().wait()


HBM_SPEC = pl.BlockSpec(memory_space=pltpu.HBM)
SEM_SPEC = pl.BlockSpec(memory_space=pltpu.SEMAPHORE)
SPLIT_COPY_EFFECT = pltpu.SideEffectType.DATAFLOW_SIDE_EFFECTING


def _owner_send_start(cpbf, after):
    land = lax.empty(cpbf.shape, cpbf.dtype)

    def body(src_ref, land_ref, after_ref, s0, s1, s2, r0, r1, r2, src_thru, land_thru, token):
        x, y, c, kc, sibling, chips = _mesh_pos()
        for jj, (chip, s_sem, r_sem) in enumerate(zip(chips, (s0, s1, s2), (r0, r1, r2))):
            pltpu.make_async_remote_copy(
                src_ref=src_ref.at[2 * chip[0] + chip[1]], dst_ref=land_ref.at[kc], send_sem=s_sem, recv_sem=r_sem,
                device_id=(*chip, c), device_id_type=MESH).start()
        token[...] = jnp.zeros_like(token)

    buf = pltpu.HBM(cpbf.shape, cpbf.dtype)
    outs = pl.pallas_call(
        body, name="rs_owner_w_in_start",
        out_shape=(pltpu.SemaphoreType.DMA(()),) * 6 + (buf, buf, SDS((8, 128), F32)),
        in_specs=(HBM_SPEC, HBM_SPEC, ANY_SPEC), out_specs=(SEM_SPEC,) * 6 + (HBM_SPEC, HBM_SPEC, VMEM_SPEC),
        input_output_aliases={0: 6, 1: 7},
        compiler_params=pltpu.CompilerParams(has_side_effects=SPLIT_COPY_EFFECT),
    )(pltpu.with_memory_space_constraint(cpbf, pltpu.HBM), pltpu.with_memory_space_constraint(land, pltpu.HBM), after)
    return outs[:6], outs[6], outs[7], outs[8]


def _owner_send_wait(sems, src_thru, land_thru, after):
    n_after = len(after)

    def body(*refs):
        src_ref, land_ref = refs[0], refs[1]
        sends, recvs = refs[2:5], refs[5:8]
        x, y, c, kc, sibling, chips = _mesh_pos()
        for jj, chip in enumerate(chips):
            slot = 2 * chip[0] + chip[1]
            cp = pltpu.make_async_remote_copy(
                src_ref=src_ref.at[slot], dst_ref=land_ref.at[slot], send_sem=sends[jj], recv_sem=recvs[jj],
                device_id=(*chip, c), device_id_type=MESH)
            cp.wait_send()
            cp.wait_recv()

    buf = pltpu.HBM(land_thru.shape, land_thru.dtype)
    return pl.pallas_call(
        body, name="rs_owner_w_in_wait", out_shape=(buf, buf),
        in_specs=(HBM_SPEC, HBM_SPEC) + (SEM_SPEC,) * 6 + (ANY_SPEC,) * n_after, out_specs=(HBM_SPEC, HBM_SPEC),
        input_output_aliases={0: 0, 1: 1},
        compiler_params=pltpu.CompilerParams(has_side_effects=SPLIT_COPY_EFFECT),
    )(src_thru, land_thru, *sems, *after)[1]


def _comm_call(name, carry):
    n_i, n_o = len(carry.ins), len(carry.outs)

    def body(*refs):
        ins, outs, sems = refs[:n_i], refs[n_i:n_i + n_o], refs[n_i + n_o:]
        _start_all(carry.copies(ins, outs, sems)[0])
        _wait_all(carry.copies(ins, outs, sems)[1])

    return pl.pallas_call(
        body, name=name, out_shape=carry.outs, in_specs=[ANY_SPEC] * n_i, out_specs=[ANY_SPEC] * n_o,
        input_output_aliases=carry.alias, scratch_shapes=carry.sems,
    )(*carry.ins)


def _host_call(body, *, name, grid, in_specs, out_specs, out_shape, args, scratch_shapes=(), carry=None, then=None,
               prefetch=None, aliases=None):
    n_in, n_out, n_scr = len(in_specs), len(out_specs), len(scratch_shapes)
    n_ci = len(carry.ins) if carry else 0
    n_co = len(carry.outs) if carry else 0
    n_cs = len(carry.sems) if carry else 0
    n_pf = 0 if prefetch is None else 1

    def wrapped(*refs):
        pf, refs = refs[:n_pf], refs[n_pf:]
        ins, cins = refs[:n_in], refs[n_in:n_in + n_ci]
        o0 = n_in + n_ci
        outs, couts = refs[o0:o0 + n_out], refs[o0 + n_out:o0 + n_out + n_co]
        s0 = o0 + n_out + n_co
        scr, sems, tsems = refs[s0:s0 + n_scr], refs[s0 + n_scr:s0 + n_scr + n_cs], refs[s0 + n_scr + n_cs:]
        idx = [pl.program_id(a) for a in range(len(grid))]
        first = functools.reduce(jnp.logical_and, [i == 0 for i in idx])
        last = functools.reduce(jnp.logical_and, [i == g - 1 for i, g in zip(idx, grid)])

        if carry:
            @pl.when(first)
            def _():
                _start_all(carry.copies(cins, couts, sems)[0])

        body(*pf, *ins, *outs, *scr)

        if carry:
            @pl.when(last)
            def _():
                _wait_all(carry.copies(cins, couts, sems)[1])
                if then:
                    _start_all(then.copies(couts, couts, tsems)[0])
                    _wait_all(then.copies(couts, couts, tsems)[1])

    all_in = list(in_specs) + [ANY_SPEC] * n_ci
    all_out = list(out_specs) + [ANY_SPEC] * n_co
    all_scr = list(scratch_shapes) + (carry.sems if carry else []) + (then.sems if then else [])
    alias = {n_pf + i: o for i, o in (aliases or {}).items()}
    if carry:
        alias.update({n_pf + n_in + i: n_out + o for i, o in carry.alias.items()})
    kwargs = dict(name=name, out_shape=list(out_shape) + (carry.outs if carry else []), input_output_aliases=alias,
                  compiler_params=_params(len(grid)))
    if prefetch is None:
        call = pl.pallas_call(wrapped, grid=grid, in_specs=all_in, out_specs=all_out, scratch_shapes=all_scr, **kwargs)
        res = call(*args, *(carry.ins if carry else []))
    else:
        call = pl.pallas_call(wrapped, grid_spec=pltpu.PrefetchScalarGridSpec(
            num_scalar_prefetch=1, grid=grid, in_specs=all_in, out_specs=all_out, scratch_shapes=all_scr), **kwargs)
        res = call(prefetch, *args, *(carry.ins if carry else []))
    return list(res[:n_out]), list(res[n_out:])


def _rs_add_halves(pos, grads, recvs, name):
    n = len(grads)

    def body(pos_ref, *refs):
        for i in range(n):
            refs[2 * n + i][...] = (refs[i][...] + refs[n + i][...]).astype(BF16)

    blks = [(1, g.shape[1] // 4, g.shape[2]) for g in grads]
    mine = [pl.BlockSpec(b, lambda k, i, p: (k, p[1] * 2 + i, 0)) for b in blks]
    half = [pl.BlockSpec(b, lambda k, i, p: (k, i, 0)) for b in blks]
    return pl.pallas_call(
        body, name=name,
        grid_spec=pltpu.PrefetchScalarGridSpec(
            num_scalar_prefetch=1, grid=(N_CHIPS, 2), in_specs=mine + half, out_specs=half),
        out_shape=[SDS((N_CHIPS, g.shape[1] // 2, g.shape[2]), BF16) for g in grads],
        compiler_params=_params(2),
    )(pos, *grads, *recvs)


def _rs_sum_owner(pos, grads, recvs, recv3s, name):
    n = len(grads)

    def body(pos_ref, *refs):
        for i in range(n):
            g, s, r1, r2, r3 = (refs[j * n + i] for j in range(5))
            own = g[0] + s[0]
            refs[5 * n + i][...] = ((own + r1[0].astype(F32)) + r2[0].astype(F32)) + r3[0].astype(F32)

    blks = [(1, g.shape[1] // 4, g.shape[2]) for g in grads]
    mine = [pl.BlockSpec(b, lambda i, p: (p[0], p[1] * 2 + i, 0)) for b in blks]

    def slot(d):
        return [pl.BlockSpec(b, lambda i, p: ((p[0] + d) % N_CHIPS, i, 0)) for b in blks]

    return pl.pallas_call(
        body, name=name,
        grid_spec=pltpu.PrefetchScalarGridSpec(
            num_scalar_prefetch=1, grid=(2,), in_specs=mine + slot(0) + slot(1) + slot(2) + slot(3),
            out_specs=[pl.BlockSpec(b[1:], lambda i, p: (p[1] * 2 + i, 0)) for b in blks]),
        out_shape=[SDS(g.shape[1:], F32) for g in grads],
        compiler_params=_params(1),
    )(pos, *grads, *recvs, *recv3s, *recv3s, *recv3s)


def _cast_bf16(pos, arrs):
    n = len(arrs)

    def body(pos_ref, *refs):
        for i in range(n):
            refs[n + i][0] = refs[i][...].astype(BF16)

    return pl.pallas_call(
        body, name="cast_bf16",
        grid_spec=pltpu.PrefetchScalarGridSpec(
            num_scalar_prefetch=1, grid=(2,),
            in_specs=[pl.BlockSpec((a.shape[0] // 2, a.shape[1]), lambda i, p: (i, 0)) for a in arrs],
            out_specs=[pl.BlockSpec((1, a.shape[0] // 2, a.shape[1]), lambda i, p: (p[0], i, 0)) for a in arrs]),
        out_shape=[SDS((N_CHIPS,) + a.shape, BF16) for a in arrs],
        compiler_params=_params(1),
    )(pos, *arrs)


def _adamw_vals(w, g, m, v):
    m2 = ADAM_B1 * m + (1.0 - ADAM_B1) * g
    v2 = ADAM_B2 * v + (1.0 - ADAM_B2) * (g * g)
    m_hat = m2 / (1.0 - ADAM_B1 ** ADAM_STEP)
    v_hat = v2 / (1.0 - ADAM_B2 ** ADAM_STEP)
    delta = -ADAM_LR * (m_hat / (jnp.sqrt(v_hat) + ADAM_EPS) + ADAM_WD * w)
    return delta, m2, v2


def _adamw_big(w, g, m, v, name):
    rows, cols = w.shape
    rb = rows // 4

    def body(w_ref, g_ref, m_ref, v_ref, d_ref, m2_ref, v2_ref):
        d, m2, v2 = _adamw_vals(w_ref[...], g_ref[...], m_ref[...], v_ref[...])
        d_ref[...] = d
        m2_ref[...] = m2
        v2_ref[...] = v2

    spec = pl.BlockSpec((rb, cols), lambda i: (i, 0))
    return pl.pallas_call(
        body, name=name, grid=(4,), in_specs=[spec] * 4, out_specs=[spec] * 3,
        out_shape=[SDS(w.shape, F32)] * 3, compiler_params=_params(1),
    )(w, g, m, v)


ADAMW_GROUP_STEPS = 8


def _adamw_group(ws, gs, ms, vs, name, carry=None):
    n = len(ws)

    def body(*refs):
        for i in range(n):
            d, m2, v2 = _adamw_vals(refs[i][...], refs[n + i][...], refs[2 * n + i][...], refs[3 * n + i][...])
            refs[4 * n + i][...] = d
            refs[5 * n + i][...] = m2
            refs[6 * n + i][...] = v2

    specs = [pl.BlockSpec((w.shape[0] // ADAMW_GROUP_STEPS, w.shape[1]), lambda i: (i, 0)) for w in ws]
    shapes = [SDS(w.shape, F32) for w in ws]
    outs, carried = _host_call(
        body, name=name, grid=(ADAMW_GROUP_STEPS,), in_specs=specs * 4, out_specs=specs * 3, out_shape=shapes * 3,
        args=(*ws, *gs, *ms, *vs), carry=carry)
    return (outs[:n], outs[n:2 * n], outs[2 * n:]), carried


def _adamw_small(ws, gs, ms, vs):
    n = len(ws)

    def body(*refs):
        for i in range(n):
            d, m2, v2 = _adamw_vals(refs[i][...], refs[n + i][...], refs[2 * n + i][...], refs[3 * n + i][...])
            refs[4 * n + i][...] = d
            refs[5 * n + i][...] = m2
            refs[6 * n + i][...] = v2

    shapes = [SDS(w.shape, F32) for w in ws]
    outs = pl.pallas_call(
        body, name="adamw_small", out_shape=shapes * 3,
        in_specs=[VMEM_SPEC] * (4 * n), out_specs=[VMEM_SPEC] * (3 * n),
    )(*ws, *gs, *ms, *vs)
    return outs[:n], outs[n:2 * n], outs[2 * n:]


def _cond_and_mod(pay, w_mod_s, b_mod_s):
    m_per, n_cols = pay.shape[0], w_mod_s.shape[1]
    rows = N_DEV * m_per

    def body(pay_ref, w_ref, b_ref, cond_ref, modg_ref, mod_s, sems_a, semr_a, loc_a, sems_b, semr_b, loc_b):
        _gather8_body(pay_ref, cond_ref, sems_a, semr_a, loc_a, m_per)
        cc = cond_ref[...]
        mod_s[...] = _dot_f32(cc * _sig(cc), w_ref[...]) + b_ref[...]
        _gather8_body(mod_s, modg_ref, sems_b, semr_b, loc_b, rows)

    dma7 = pltpu.SemaphoreType.DMA((7,))
    return pl.pallas_call(
        body, name="cond_and_mod",
        out_shape=[SDS((rows, pay.shape[1]), F32), SDS((N_DEV * rows, n_cols), F32)],
        in_specs=[VMEM_SPEC] * 3, out_specs=[VMEM_SPEC] * 2,
        scratch_shapes=[pltpu.VMEM((rows, n_cols), F32), dma7, dma7, pltpu.SemaphoreType.DMA,
                        dma7, dma7, pltpu.SemaphoreType.DMA],
        compiler_params=pltpu.CompilerParams(vmem_limit_bytes=VMEM_LIMIT),
    )(pay, w_mod_s, b_mod_s)


def _mod_bwd(cond64, dmod64, dmod64_my, w_mod_s, c_ctx):
    def body(c_ref, g_ref, gm_ref, w_ref, cc_ref, gw_ref, gb_ref, gcc_ref):
        cc = c_ref[...]
        act = cc * _sig(cc)
        gm = gm_ref[...]
        gw_ref[...] = _dot_f32(act, gm, (((0,), (0,)), ((), ())))
        gb_ref[...] = _colsum(g_ref[...])
        dact = _dot_f32(gm, w_ref[...], (((1,), (1,)), ((), ())))
        tot = dact[4:5, :]
        for dev in range(1, N_DEV):
            tot = tot + dact[8 * dev + 4:8 * dev + 5, :]
        c0 = cc_ref[...]
        s0 = _sig(c0)
        gcc_ref[...] = tot * (s0 * (1.0 + c0 * (1.0 - s0)))

    return pl.pallas_call(
        body, name="mod_bwd",
        out_shape=[SDS(w_mod_s.shape, F32), SDS((1, dmod64.shape[1]), F32), SDS((1, D_MODEL), F32)],
        in_specs=[VMEM_SPEC] * 5, out_specs=[VMEM_SPEC] * 3,
        compiler_params=pltpu.CompilerParams(vmem_limit_bytes=VMEM_LIMIT),
    )(cond64, dmod64, dmod64_my, w_mod_s, c_ctx)


SHARD_COLS = IN_COLS // N_CHIPS


def _in_fwd_own(pos, x, ctx, g_mix, mod_a, w_own, carry=None, then=None):
    bs, seq, _ = x.shape
    nb = seq // ROW_BLOCK + 1

    def body(pos_ref, x_ref, ctx_ref, g_ref, mod_ref, w_ref, p_ref, h_ref, w_bf):
        b, j = pl.program_id(0), pl.program_id(1)

        @pl.when((b == 0) & (j == 0))
        def _():
            w_bf[...] = w_ref[...].astype(BF16)

        is_ctx = j == 0
        xin = jnp.where(is_ctx, ctx_ref[0], x_ref[0])
        shift = jnp.where(is_ctx, mod_ref[0, 2:3, :], mod_ref[0, 0:1, :])
        scale = jnp.where(is_ctx, mod_ref[0, 3:4, :], mod_ref[0, 1:2, :])
        xn, _ = _rms(xin)
        hb = ((xn * g_ref[...]) * (1.0 + scale) + shift).astype(BF16)
        h_ref[0] = hb
        p_ref[0] = jnp.dot(hb, w_bf[...], preferred_element_type=F32)

    return _host_call(
        body, name="in_fwd_own", grid=(bs, nb), prefetch=pos,
        in_specs=[pl.BlockSpec((1, ROW_BLOCK, D_MODEL), lambda b, j, p: (b, jnp.maximum(j - 1, 0), 0)),
                  pl.BlockSpec((1, ROW_BLOCK, D_MODEL), lambda b, j, p: (b, 0, 0)),
                  pl.BlockSpec((1, D_MODEL), lambda b, j, p: (0, 0)),
                  pl.BlockSpec((1, 8, D_MODEL), lambda b, j, p: (b, 0, 0)),
                  pl.BlockSpec((D_MODEL, SHARD_COLS), lambda b, j, p: (0, 0))],
        out_specs=[pl.BlockSpec((1, ROW_BLOCK, SHARD_COLS), lambda b, j, p: (b, j, p[0])),
                   pl.BlockSpec((1, ROW_BLOCK, D_MODEL), lambda b, j, p: (b, j, 0))],
        out_shape=[SDS((bs, nb * ROW_BLOCK, IN_COLS), F32), SDS((bs, nb * ROW_BLOCK, D_MODEL), BF16)],
        scratch_shapes=[pltpu.VMEM((D_MODEL, SHARD_COLS), BF16)],
        args=(x, ctx, g_mix, mod_a, w_own), carry=carry, then=then)


def _in_fwd_rest(pos, h_all, w_in_g, p, carry=None):
    bs, rows, _ = h_all.shape
    rows_all = bs * rows
    tile = next(m * ROW_BLOCK for m in (9, 3, 1) if rows_all % (m * ROW_BLOCK) == 0)

    def body(pos_ref, h_ref, w_ref, p_in_ref, p_ref):
        p_ref[...] = jnp.dot(h_ref[...], w_ref[0], preferred_element_type=F32)

    shard = lambda n, p: (p[0] + 1 + n) % N_CHIPS
    (p2,), carried = _host_call(
        body, name="in_fwd_rest", grid=(N_CHIPS - 1, rows_all // tile), prefetch=pos,
        in_specs=[pl.BlockSpec((tile, D_MODEL), lambda n, t, p: (t, 0)),
                  pl.BlockSpec((1, D_MODEL, SHARD_COLS), lambda n, t, p: (shard(n, p), 0, 0)),
                  ANY_SPEC],
        out_specs=[pl.BlockSpec((tile, SHARD_COLS), lambda n, t, p: (t, shard(n, p)))],
        out_shape=[SDS((rows_all, IN_COLS), F32)], aliases={2: 0},
        args=(h_all.reshape(rows_all, D_MODEL), w_in_g, p.reshape(rows_all, IN_COLS)), carry=carry)
    return p2.reshape(bs, rows, IN_COLS), carried


def _in_bwd(x, ctx, dx1, g_mix, mod_a, w_in, df_f, df_b, dv_f, dv_b, dq_f, dq_b, dpc, carry=None):
    bs, seq, _ = x.shape
    nb = seq // ROW_BLOCK + 1

    def body(x_ref, ctx_ref, dx1_ref, g_ref, mod_ref, w_ref, dff_ref, dfb_ref, dvf_ref, dvb_ref, dqf_ref, dqb_ref,
             dpc_ref, gx_ref, dp_ref, dg_ref, dmod_ref):
        b, j = pl.program_id(0), pl.program_id(1)
        is_ctx = j == 0

        @pl.when((b == 0) & (j == 0))
        def _():
            dg_ref[...] = jnp.zeros_like(dg_ref)

        @pl.when(j == 0)
        def _():
            dmod_ref[...] = jnp.zeros_like(dmod_ref)

        di = (dvf_ref[0] + dvb_ref[0]).astype(BF16)
        dq = (dqf_ref[0] + dqb_ref[0]).astype(BF16)
        dp = jnp.concatenate([dff_ref[0], dfb_ref[0], di, dq, dpc_ref[0]], axis=1)
        dp_ref[0] = dp
        dh = lax.dot_general(dp, w_ref[...], (((1,), (1,)), ((), ())), preferred_element_type=F32)
        xin = jnp.where(is_ctx, ctx_ref[0], x_ref[0])
        scale = jnp.where(is_ctx, mod_ref[0, 3:4, :], mod_ref[0, 1:2, :])
        xn, r = _rms(xin)
        g = g_ref[...]
        hn = xn * g
        d_shift = _colsum(dh)
        d_scale = _colsum(dh * hn)
        dhn = dh * (1.0 + scale)
        dg_ref[...] += _colsum(dhn * xn)
        dx = _rms_bwd(dhn * g, xn, r)

        @pl.when(is_ctx)
        def _():
            dmod_ref[0, 2:3, :] += d_shift
            dmod_ref[0, 3:4, :] += d_scale

        @pl.when(jnp.logical_not(is_ctx))
        def _():
            dmod_ref[0, 0:1, :] += d_shift
            dmod_ref[0, 1:2, :] += d_scale
            gx_ref[0] = dx + dx1_ref[0]

    def rows(w):
        return pl.BlockSpec((1, ROW_BLOCK, w), lambda b, j: (b, j, 0))

    lat = pl.BlockSpec((1, ROW_BLOCK, D_MODEL), lambda b, j: (b, jnp.maximum(j - 1, 0), 0))
    return _host_call(
        body, name="in_bwd", grid=(bs, nb),
        in_specs=[lat, pl.BlockSpec((1, ROW_BLOCK, D_MODEL), lambda b, j: (b, 0, 0)), lat,
                  pl.BlockSpec((1, D_MODEL), lambda b, j: (0, 0)),
                  pl.BlockSpec((1, 8, D_MODEL), lambda b, j: (b, 0, 0)),
                  pl.BlockSpec((D_MODEL, IN_COLS), lambda b, j: (0, 0)),
                  rows(KW), rows(KW), rows(KW), rows(KW), rows(KW), rows(KW), rows(7 * KW)],
        out_specs=[lat, rows(IN_COLS), pl.BlockSpec((1, D_MODEL), lambda b, j: (0, 0)),
                   pl.BlockSpec((1, 8, D_MODEL), lambda b, j: (b, 0, 0))],
        out_shape=[SDS(x.shape, F32), SDS((bs, nb * ROW_BLOCK, IN_COLS), BF16), SDS((1, D_MODEL), F32),
                   SDS((bs, 8, D_MODEL), F32)],
        args=(x, ctx, dx1, g_mix, mod_a, w_in, df_f, df_b, dv_f, dv_b, dq_f, dq_b, dpc), carry=carry)


def _lower_bound(lbg_ref, direction):
    return _sig(lbg_ref[0, direction:direction + 1, :] - lbg_ref[1, direction:direction + 1, :])


def _block_tri(upper):
    t = np.arange(ROW_BLOCK)[:, None]
    s = np.arange(ROW_BLOCK)[None, :]
    same = (t // HGRN_CHUNK) == (s // HGRN_CHUNK)
    return jnp.asarray(same & ((s >= t) if upper else (s <= t)), dtype=BF16)


TRI_SPEC = pl.BlockSpec((ROW_BLOCK, ROW_BLOCK), lambda b, j: (0, 0))


def _tri_matmul_f32(tri, g):
    g0 = g.astype(BF16)
    r1 = g - g0.astype(F32)
    g1 = r1.astype(BF16)
    g2 = (r1 - g1.astype(F32)).astype(BF16)
    return (jnp.dot(tri, g2, preferred_element_type=F32) + jnp.dot(tri, g1, preferred_element_type=F32)) \
        + jnp.dot(tri, g0, preferred_element_type=F32)


def _chunk_rows(rows):
    return jnp.concatenate([jnp.broadcast_to(r, (HGRN_CHUNK, r.shape[1])) for r in rows], axis=0)


def _block_gates(fl, q, lb, tri, upper):
    t = {}
    t["sg"] = _sig(fl)
    t["f"] = lb + (1.0 - lb) * t["sg"]
    k = 1.0 - t["f"]
    bcum = _tri_matmul_f32(tri, jnp.log(t["f"]))
    ends = [bcum[ci * HGRN_CHUNK:ci * HGRN_CHUNK + 1] if upper else bcum[(ci + 1) * HGRN_CHUNK - 1:(ci + 1) * HGRN_CHUNK]
            for ci in range(fl.shape[0] // HGRN_CHUNK)]
    mid = _chunk_rows([0.5 * r for r in ends])
    t["dec"] = [jnp.exp(r) for r in ends]
    t["e1"] = jnp.exp(bcum - mid)
    t["e2"] = jnp.exp(mid - bcum)
    t["eh"] = _chunk_rows([jnp.exp(0.5 * r) for r in ends])
    t["qi"] = q * t["e1"]
    t["ki"] = k * t["e2"]
    t["kd"] = t["ki"] * t["eh"]
    t["qe"] = t["qi"] * t["eh"]
    return t


def _hgrn_block_order(direction, nb):
    if direction == 0:
        return lambda j: j
    return lambda j: jnp.where(j == 0, 0, nb - j)


def _hgrn_fwd(p, lbg, carry=None, then=None):
    bs, rows, _ = p.shape
    nb = rows // ROW_BLOCK
    ncb = ROW_BLOCK // HGRN_CHUNK
    orders = [_hgrn_block_order(d, nb) for d in (0, 1)]
    dirs = (0, 1)

    def body(f0_ref, i0_ref, q0_ref, f1_ref, i1_ref, q1_ref, lbg_ref, tri0_ref, tri1_ref,
             o0_ref, s0_ref, o1_ref, s1_ref, st):
        @pl.when(pl.program_id(1) == 0)
        def _():
            st[...] = jnp.zeros_like(st)

        f_refs, i_refs, q_refs = (f0_ref, f1_ref), (i0_ref, i1_ref), (q0_ref, q1_ref)
        tri_refs, o_refs, s_refs = (tri0_ref, tri1_ref), (o0_ref, o1_ref), (s0_ref, s1_ref)
        chunk = lambda a, ci, h: a[ci * HGRN_CHUNK:(ci + 1) * HGRN_CHUNK, h * HEAD_DIM:(h + 1) * HEAD_DIM]
        masks = [_tri(HGRN_CHUNK, d == 1) for d in dirs]
        t = [_block_gates(f_refs[d][0], q_refs[d][0], _lower_bound(lbg_ref, d), tri_refs[d][...], d == 1) for d in dirs]
        v = [i_refs[d][0] for d in dirs]
        intra = [[[None] * N_HEADS for _ in range(ncb)] for _ in dirs]
        ds_loc = [[[None] * N_HEADS for _ in range(ncb)] for _ in dirs]
        for ci in range(ncb):
            for h in range(N_HEADS):
                for d in dirs:
                    a = jnp.where(masks[d], _dot_nt(chunk(t[d]["qi"], ci, h), chunk(t[d]["ki"], ci, h)), 0.0)
                    intra[d][ci][h] = _dot(a, chunk(v[d], ci, h))
                    ds_loc[d][ci][h] = _dot_tn(chunk(v[d], ci, h), chunk(t[d]["kd"], ci, h))
        for h in range(N_HEADS):
            ls = slice(h * HEAD_DIM, (h + 1) * HEAD_DIM)
            s = [st[d, h] for d in dirs]
            for step in range(ncb):
                for d in dirs:
                    ci = ncb - 1 - step if d == 1 else step
                    s_refs[d][0, 0, ci, h] = s[d]
                    o_refs[d][0, ci * HGRN_CHUNK:(ci + 1) * HGRN_CHUNK, ls] = (
                        intra[d][ci][h] + _dot_nt(chunk(t[d]["qe"], ci, h), s[d]))
                    s[d] = s[d] * t[d]["dec"][ci][:, ls] + ds_loc[d][ci][h]
            for d in dirs:
                st[d, h] = s[d]

    def col(d, cb):
        return pl.BlockSpec((1, ROW_BLOCK, KW), lambda b, j: (b, orders[d](j), cb))

    def outs(d):
        return [pl.BlockSpec((1, ROW_BLOCK, KW), lambda b, j: (b, orders[d](j), 0)),
                pl.BlockSpec((1, 1, ncb, N_HEADS, HEAD_DIM, HEAD_DIM), lambda b, j: (b, orders[d](j), 0, 0, 0, 0))]

    shapes = [SDS((bs, rows, KW), F32), SDS((bs, nb, ncb, N_HEADS, HEAD_DIM, HEAD_DIM), F32)]
    return _host_call(
        body, name="hgrn_fwd", grid=(bs, nb),
        in_specs=[col(0, 0), col(0, 2), col(0, 3), col(1, 1), col(1, 2), col(1, 3),
                  pl.BlockSpec((2, 2, KW), lambda b, j: (0, 0, 0)), TRI_SPEC, TRI_SPEC],
        out_specs=outs(0) + outs(1), out_shape=shapes * 2,
        scratch_shapes=[pltpu.VMEM((2, N_HEADS, HEAD_DIM, HEAD_DIM), F32)],
        args=(p, p, p, p, p, p, lbg, _block_tri(False), _block_tri(True)), carry=carry, then=then)


def _hgrn_bwd_pair(p, lbg, s_saved, do_raw, carry=None):
    bs, rows, _ = p.shape
    nb = rows // ROW_BLOCK
    ncb = ROW_BLOCK // HGRN_CHUNK
    dirs = (0, 1)
    fwd_orders = [_hgrn_block_order(d, nb) for d in dirs]
    orders = [lambda j, d=d: fwd_orders[d](nb - 1 - j) for d in dirs]
    pairs = [(ci, h) for ci in range(ncb) for h in range(N_HEADS)]

    def body(f0_ref, i0_ref, q0_ref, s0_ref, do0_ref, f1_ref, i1_ref, q1_ref, s1_ref, do1_ref,
             lbg_ref, tril_ref, triu_ref,
             df0_ref, dq0_ref, dv0_ref, dlb0_ref, df1_ref, dq1_ref, dv1_ref, dlb1_ref, dst, acc):
        b, j = pl.program_id(0), pl.program_id(1)
        f_refs, i_refs, q_refs = (f0_ref, f1_ref), (i0_ref, i1_ref), (q0_ref, q1_ref)
        s_refs, do_refs = (s0_ref, s1_ref), (do0_ref, do1_ref)
        df_refs, dq_refs, dv_refs, dlb_refs = (df0_ref, df1_ref), (dq0_ref, dq1_ref), (dv0_ref, dv1_ref), (dlb0_ref, dlb1_ref)
        tri_refs, trit_refs = (tril_ref, triu_ref), (triu_ref, tril_ref)

        @pl.when((b == 0) & (j == 0))
        def _():
            dlb0_ref[...] = jnp.zeros_like(dlb0_ref)
            dlb1_ref[...] = jnp.zeros_like(dlb1_ref)

        @pl.when(j == 0)
        def _():
            dst[...] = jnp.zeros_like(dst)

        chunk = lambda a, ci, h: a[ci * HGRN_CHUNK:(ci + 1) * HGRN_CHUNK, h * HEAD_DIM:(h + 1) * HEAD_DIM]
        rows_of = lambda ci: slice(ci * HGRN_CHUNK, (ci + 1) * HGRN_CHUNK)
        lanes_of = lambda h: slice(h * HEAD_DIM, (h + 1) * HEAD_DIM)
        grid3 = lambda: [[[None] * N_HEADS for _ in range(ncb)] for _ in dirs]
        lbs = [_lower_bound(lbg_ref, d) for d in dirs]
        masks = [_tri(HGRN_CHUNK, d == 1) for d in dirs]
        masks_t = [_tri(HGRN_CHUNK, d != 1) for d in dirs]
        t = [_block_gates(f_refs[d][0], q_refs[d][0], lbs[d], tri_refs[d][...], d == 1) for d in dirs]
        v = [i_refs[d][0] for d in dirs]
        do = [do_refs[d][0] for d in dirs]
        a_t, da, da_t, dv_in, ds_loc = (grid3() for _ in range(5))
        for ci, h in pairs:
            for d in dirs:
                a_t[d][ci][h] = _dot_nt(chunk(t[d]["ki"], ci, h), chunk(t[d]["qi"], ci, h))
        for ci, h in pairs:
            for d in dirs:
                da[d][ci][h] = _dot_nt(chunk(do[d], ci, h), chunk(v[d], ci, h))
        for ci, h in pairs:
            for d in dirs:
                da_t[d][ci][h] = _dot_nt(chunk(v[d], ci, h), chunk(do[d], ci, h))
        for ci, h in pairs:
            for d in dirs:
                acc[d, 3, rows_of(ci), lanes_of(h)] = _dot(chunk(do[d], ci, h), s_refs[d][0, 0, ci, h])
        for ci, h in pairs:
            for d in dirs:
                ds_loc[d][ci][h] = _dot_tn(chunk(do[d], ci, h), chunk(t[d]["qe"], ci, h))
        for ci, h in pairs:
            for d in dirs:
                acc[d, 0, rows_of(ci), lanes_of(h)] = _dot(jnp.where(masks[d], da[d][ci][h], 0.0),
                                                           chunk(t[d]["ki"], ci, h))
        for ci, h in pairs:
            for d in dirs:
                acc[d, 1, rows_of(ci), lanes_of(h)] = _dot(jnp.where(masks_t[d], da_t[d][ci][h], 0.0),
                                                           chunk(t[d]["qi"], ci, h))
        for ci, h in pairs:
            for d in dirs:
                dv_in[d][ci][h] = _dot(jnp.where(masks_t[d], a_t[d][ci][h], 0.0), chunk(do[d], ci, h))
        ddec = grid3()
        for h in range(N_HEADS):
            ls = lanes_of(h)
            ds = [dst[d, h] for d in dirs]
            for step in range(ncb):
                for d in dirs:
                    ci = step if d == 1 else ncb - 1 - step
                    acc[d, 2, rows_of(ci), ls] = _dot(chunk(v[d], ci, h), ds[d])
                    acc[d, 4, rows_of(ci), ls] = dv_in[d][ci][h] + _dot_nt(chunk(t[d]["kd"], ci, h), ds[d])
                    ddec[d][ci][h] = _colsum(ds[d] * s_refs[d][0, 0, ci, h])
                    ds[d] = ds[d] * t[d]["dec"][ci][:, ls] + ds_loc[d][ci][h]
            for d in dirs:
                dst[d, h] = ds[d]
        for d in dirs:
            td = t[d]
            dqi, dki, dkd, dqe = (acc[d, i] for i in range(4))
            dq_refs[d][0] = td["e1"] * (dqi + dqe * td["eh"])
            dv_refs[d][0] = acc[d, 4]
            dk = td["e2"] * (dki + dkd * td["eh"])
            dkd_kd = dkd * td["kd"]
            db = dqi * td["qi"] - dki * td["ki"] - dkd_kd + dqe * td["qe"]
            dbl = [_colsum(dkd_kd[rows_of(ci)]) + jnp.concatenate(ddec[d][ci], axis=1) * td["dec"][ci]
                   for ci in range(ncb)]
            dg = _tri_matmul_f32(trit_refs[d][...], db) + _chunk_rows(dbl)
            df = dg / td["f"] - dk
            sg = td["sg"]
            dlb_refs[d][...] += _colsum(df * (1.0 - sg))
            df_refs[d][0] = (df * (1.0 - lbs[d]) * sg * (1.0 - sg)).astype(BF16)

    def ins(d):
        col = lambda cb: pl.BlockSpec((1, ROW_BLOCK, KW), lambda b, j: (b, orders[d](j), cb))
        return [col(d), col(2), col(3),
                pl.BlockSpec((1, 1, ncb, N_HEADS, HEAD_DIM, HEAD_DIM), lambda b, j: (b, orders[d](j), 0, 0, 0, 0)),
                pl.BlockSpec((1, ROW_BLOCK, KW), lambda b, j: (b, orders[d](j), 0))]

    def outs(d):
        row = pl.BlockSpec((1, ROW_BLOCK, KW), lambda b, j: (b, orders[d](j), 0))
        return [row, row, row, pl.BlockSpec((1, KW), lambda b, j: (0, 0))]

    shapes = [SDS((bs, rows, KW), BF16), SDS((bs, rows, KW), F32), SDS((bs, rows, KW), F32), SDS((1, KW), F32)]
    return _host_call(
        body, name="hgrn_bwd", grid=(bs, nb),
        in_specs=ins(0) + ins(1) + [pl.BlockSpec((2, 2, KW), lambda b, j: (0, 0, 0)), TRI_SPEC, TRI_SPEC],
        out_specs=outs(0) + outs(1), out_shape=shapes * 2,
        scratch_shapes=[pltpu.VMEM((2, N_HEADS, HEAD_DIM, HEAD_DIM), F32), pltpu.VMEM((2, 5, ROW_BLOCK, KW), F32)],
        args=(p, p, p, s_saved[0], do_raw, p, p, p, s_saved[1], do_raw, lbg, _block_tri(False), _block_tri(True)),
        carry=carry)


def _mix_values(og, u, v, ga, gb, o_raw, gna, lng, lnb, ws_ref, bst, wpa, wpb, wo):
    t = {}
    sog = _sig(og)
    t["sog"], t["silu_og"] = sog, og * sog
    xh_l, r_l = [], []
    for h in range(N_HEADS):
        xh, r = _rms(o_raw[:, h * HEAD_DIM:(h + 1) * HEAD_DIM])
        xh_l.append(xh)
        r_l.append(r)
    t["xh"], t["r"] = jnp.concatenate(xh_l, axis=1), r_l
    gna4 = jnp.concatenate([gna] * N_HEADS, axis=1)
    t["gna4"] = gna4
    t["o_n"] = t["xh"] * gna4
    t["o_a"] = t["o_n"] * t["silu_og"]
    t["ya"] = _dot(t["o_a"], wpa)
    t["gu"], t["tu"] = _gelu(u)
    gv, t["tv"] = _gelu(v)
    mu = jnp.mean(gv, axis=-1, keepdims=True)
    cen = gv - mu
    t["rstd"] = lax.rsqrt(jnp.mean(cen * cen, axis=-1, keepdims=True) + EPS)
    t["xhat"] = cen * t["rstd"]
    vn = t["xhat"] * lng + lnb
    t["vn"] = vn
    chunks = []
    for n in range(ROW_BLOCK // SGU_CHUNK):
        rs = slice(n * SGU_CHUNK, (n + 1) * SGU_CHUNK)
        groups = []
        for g in range(N_HEADS):
            ls = slice(g * HEAD_DIM, (g + 1) * HEAD_DIM)
            groups.append(_dot(ws_ref[g], vn[rs, ls]) + bst[:, g:g + 1])
        chunks.append(jnp.concatenate(groups, axis=1))
    t["mixed"] = jnp.concatenate(chunks, axis=0)
    t["o_bm"] = t["gu"] * t["mixed"]
    t["yb"] = _dot(t["o_bm"], wpb)
    t["sa"], t["sb"] = _sig(ga), _sig(gb)
    t["merged"] = t["sa"] * t["ya"] + t["sb"] * t["yb"]
    t["mix"] = _dot(t["merged"], wo)
    return t


def _mix_in_specs(row_of):
    def col(cb):
        return pl.BlockSpec((1, ROW_BLOCK, KW), lambda b, j: (b, row_of(j), cb))
    return [col(cb) for cb in range(4, 11)]


def _mix_param_specs():
    full2 = lambda r, c: pl.BlockSpec((r, c), lambda b, j: (0, 0))
    return [full2(1, HEAD_DIM), full2(1, KW), full2(1, KW),
            pl.BlockSpec((N_HEADS, SGU_CHUNK, SGU_CHUNK), lambda b, j: (0, 0, 0)),
            full2(SGU_CHUNK, N_HEADS), full2(KW, D_MODEL), full2(KW, D_MODEL), full2(D_MODEL, D_MODEL)]


def _mix_fwd(p, o_f, o_b, x, mod_c, gna, lng, lnb, w_s, bst, wpa, wpb, wo):
    bs, seq, _ = x.shape
    nbl = seq // ROW_BLOCK

    def body(og_r, u_r, v_r, ga0_r, ga1_r, gb0_r, gb1_r, of_r, ob_r, x_r, mod_r,
             gna_r, lng_r, lnb_r, ws_r, bst_r, wpa_r, wpb_r, wo_r, x1_r):
        ga = jnp.concatenate([ga0_r[0], ga1_r[0]], axis=1)
        gb = jnp.concatenate([gb0_r[0], gb1_r[0]], axis=1)
        t = _mix_values(og_r[0], u_r[0], v_r[0], ga, gb, of_r[0] + ob_r[0], gna_r[...], lng_r[...], lnb_r[...],
                        ws_r, bst_r[...], wpa_r[...], wpb_r[...], wo_r[...])
        x1_r[0] = x_r[0] + mod_r[0, 0:1, :] * t["mix"]

    row = lambda w: pl.BlockSpec((1, ROW_BLOCK, w), lambda b, j: (b, j + 1, 0))
    lat = pl.BlockSpec((1, ROW_BLOCK, D_MODEL), lambda b, j: (b, j, 0))
    return pl.pallas_call(
        body, name="mix_fwd", grid=(bs, nbl),
        in_specs=_mix_in_specs(lambda j: j + 1) + [row(KW), row(KW), lat,
                                                    pl.BlockSpec((1, 8, D_MODEL), lambda b, j: (b, 0, 0))]
        + _mix_param_specs(),
        out_specs=lat, out_shape=SDS(x.shape, F32), compiler_params=_params(2),
    )(p, p, p, p, p, p, p, o_f, o_b, x, mod_c, gna, lng, lnb, w_s, bst, wpa, wpb, wo)


def _mix_bwd(p, o_f, o_b, dx1, mod_c, gna, lng, lnb, w_s, w_s_t, bst, wpa, wpb, wo, carry=None):
    bs, rows, _ = p.shape
    nb = rows // ROW_BLOCK

    def body(og_r, u_r, v_r, ga0_r, ga1_r, gb0_r, gb1_r, of_r, ob_r, dx1_r, mod_r,
             gna_r, lng_r, lnb_r, ws_r, bst_r, wpa_r, wpb_r, wo_r, wst_r,
             dor_r, dpc_r, dwpa_r, dwpb_r, dwo_r, dgna_r, dlng_r, dlnb_r, dws_r, dbst_r, dmod_r):
        b, j = pl.program_id(0), pl.program_id(1)

        @pl.when((b == 0) & (j == 0))
        def _():
            for r in (dwpa_r, dwpb_r, dwo_r, dgna_r, dlng_r, dlnb_r, dws_r, dbst_r):
                r[...] = jnp.zeros_like(r)

        @pl.when(j == 0)
        def _():
            dmod_r[...] = jnp.zeros_like(dmod_r)
            dor_r[...] = jnp.zeros_like(dor_r)
            dpc_r[...] = jnp.zeros_like(dpc_r)

        @pl.when(j > 0)
        def _():
            og, u, v = og_r[0], u_r[0], v_r[0]
            ga = jnp.concatenate([ga0_r[0], ga1_r[0]], axis=1)
            gb = jnp.concatenate([gb0_r[0], gb1_r[0]], axis=1)
            gna, lng = gna_r[...], lng_r[...]
            wpa, wpb, wo = wpa_r[...], wpb_r[...], wo_r[...]
            dx1 = dx1_r[0]
            dmix = mod_r[0, 0:1, :] * dx1
            dmerged = _dot_nt(dmix, wo)
            t = _mix_values(og, u, v, ga, gb, of_r[0] + ob_r[0], gna, lng, lnb_r[...],
                            ws_r, bst_r[...], wpa, wpb, wo)
            dmod_r[0, 0:1, :] += _colsum(dx1 * t["mix"])
            dwo_r[...] += _dot_tn(t["merged"], dmix)
            sa, sb = t["sa"], t["sb"]
            dya, dyb = sa * dmerged, sb * dmerged
            dga = dmerged * t["ya"] * sa * (1.0 - sa)
            dgb = dmerged * t["yb"] * sb * (1.0 - sb)
            do_a = _dot_nt(dya, wpa)
            dwpa_r[...] += _dot_tn(t["o_a"], dya)
            do_bm = _dot_nt(dyb, wpb)
            dwpb_r[...] += _dot_tn(t["o_bm"], dyb)
            sog = t["sog"]
            dog = do_a * t["o_n"] * (sog * (1.0 + og * (1.0 - sog)))
            do_n = do_a * t["silu_og"]
            dxh = do_n * t["gna4"]
            prod = do_n * t["xh"]
            dgna = jnp.zeros((1, HEAD_DIM), F32)
            dor_l = []
            for h in range(N_HEADS):
                ls = slice(h * HEAD_DIM, (h + 1) * HEAD_DIM)
                dgna = dgna + _colsum(prod[:, ls])
                dor_l.append(_rms_bwd(dxh[:, ls], t["xh"][:, ls], t["r"][h]))
            dgna_r[...] += dgna
            dor_r[0] = jnp.concatenate(dor_l, axis=1)
            du = do_bm * t["mixed"] * _dgelu(u, t["tu"])
            dmixed = do_bm * t["gu"]
            vn = t["vn"]
            dvn_chunks = []
            for n in range(ROW_BLOCK // SGU_CHUNK):
                rs = slice(n * SGU_CHUNK, (n + 1) * SGU_CHUNK)
                groups = []
                for g in range(N_HEADS):
                    ls = slice(g * HEAD_DIM, (g + 1) * HEAD_DIM)
                    dm = dmixed[rs, ls]
                    dws_r[g] += _dot_nt(dm, vn[rs, ls])
                    dbst_r[:, g:g + 1] += jnp.sum(dm, axis=1, keepdims=True)
                    groups.append(_dot(wst_r[g], dm))
                dvn_chunks.append(jnp.concatenate(groups, axis=1))
            dvn = jnp.concatenate(dvn_chunks, axis=0)
            xhat = t["xhat"]
            dlng_r[...] += _colsum(dvn * xhat)
            dlnb_r[...] += _colsum(dvn)
            dxhat = dvn * lng
            dgv = t["rstd"] * (dxhat - jnp.mean(dxhat, axis=-1, keepdims=True)
                               - xhat * jnp.mean(dxhat * xhat, axis=-1, keepdims=True))
            dv = dgv * _dgelu(v, t["tv"])
            dpc_r[0] = jnp.concatenate([dog, du, dv, dga, dgb], axis=1).astype(BF16)

    row = lambda w: pl.BlockSpec((1, ROW_BLOCK, w), lambda b, j: (b, j, 0))
    lat = pl.BlockSpec((1, ROW_BLOCK, D_MODEL), lambda b, j: (b, jnp.maximum(j - 1, 0), 0))
    full2 = lambda r, c: pl.BlockSpec((r, c), lambda b, j: (0, 0))
    ws_spec = pl.BlockSpec((N_HEADS, SGU_CHUNK, SGU_CHUNK), lambda b, j: (0, 0, 0))
    return _host_call(
        body, name="mix_bwd", grid=(bs, nb),
        in_specs=_mix_in_specs(lambda j: j) + [row(KW), row(KW), lat,
                                                pl.BlockSpec((1, 8, D_MODEL), lambda b, j: (b, 0, 0))]
        + _mix_param_specs() + [ws_spec],
        out_specs=[row(KW), row(7 * KW), full2(KW, D_MODEL), full2(KW, D_MODEL), full2(D_MODEL, D_MODEL),
                   full2(1, HEAD_DIM), full2(1, KW), full2(1, KW), ws_spec, full2(SGU_CHUNK, N_HEADS),
                   pl.BlockSpec((1, 8, D_MODEL), lambda b, j: (b, 0, 0))],
        out_shape=[SDS((bs, rows, KW), F32), SDS((bs, rows, 7 * KW), BF16), SDS((KW, D_MODEL), F32),
                   SDS((KW, D_MODEL), F32), SDS((D_MODEL, D_MODEL), F32), SDS((1, HEAD_DIM), F32),
                   SDS((1, KW), F32), SDS((1, KW), F32), SDS((N_HEADS, SGU_CHUNK, SGU_CHUNK), F32),
                   SDS((SGU_CHUNK, N_HEADS), F32), SDS((bs, 8, D_MODEL), F32)],
        args=(p, p, p, p, p, p, p, o_f, o_b, dx1, mod_c, gna, lng, lnb, w_s, bst, wpa, wpb, wo, w_s_t), carry=carry)


def _ffn(x1, target, mod_c, g_ffn, g_final, w_up, w_down):
    bs, seq, _ = x1.shape
    nbl = seq // ROW_BLOCK

    def body(x1_r, tg_r, mod_r, gf_r, gl_r, wu_r, wd_r,
             dx1_r, h2_r, dab_r, hid_r, dffn_r, loss_r, dgl_r, dgf_r, dmod_r):
        b, j = pl.program_id(0), pl.program_id(1)

        @pl.when((b == 0) & (j == 0))
        def _():
            for r in (loss_r, dgl_r, dgf_r):
                r[...] = jnp.zeros_like(r)

        @pl.when(j == 0)
        def _():
            dmod_r[...] = jnp.zeros_like(dmod_r)

        x1 = x1_r[0]
        shift, scale, gate = mod_r[0, 1:2, :], mod_r[0, 2:3, :], mod_r[0, 3:4, :]
        gf, gl = gf_r[...], gl_r[...]
        xn2, r2 = _rms(x1)
        hn2 = xn2 * gf
        h2 = (hn2 * (1.0 + scale) + shift).astype(BF16)
        h2_r[0] = h2
        ab = jnp.dot(h2, wu_r[...], preferred_element_type=F32)
        a, bb = ab[:, :D_FF], ab[:, D_FF:]
        sa = _sig(a)
        silu_a = a * sa
        hid = (silu_a * bb).astype(BF16)
        hid_r[0] = hid
        ffn = jnp.dot(hid, wd_r[...], preferred_element_type=F32)
        x2 = x1 + gate * ffn
        xn3, r3 = _rms(x2)
        err = xn3 * gl - tg_r[0]
        loss_r[...] += 0.5 * jnp.sum(jnp.mean(err * err, axis=-1, keepdims=True), axis=0, keepdims=True)
        dy = err * (1.0 / D_MODEL)
        dgl_r[...] += _colsum(dy * xn3)
        dx2 = _rms_bwd(dy * gl, xn3, r3)
        dmod_r[0, 3:4, :] += _colsum(dx2 * ffn)
        dffn = (gate * dx2).astype(BF16)
        dffn_r[0] = dffn
        dhid = lax.dot_general(dffn, wd_r[...], (((1,), (1,)), ((), ())), preferred_element_type=F32)
        da = dhid * bb * (sa * (1.0 + a * (1.0 - sa)))
        db = dhid * silu_a
        dab = jnp.concatenate([da, db], axis=1).astype(BF16)
        dab_r[0] = dab
        dh2 = lax.dot_general(dab, wu_r[...], (((1,), (1,)), ((), ())), preferred_element_type=F32)
        dmod_r[0, 1:2, :] += _colsum(dh2)
        dmod_r[0, 2:3, :] += _colsum(dh2 * hn2)
        dhn2 = dh2 * (1.0 + scale)
        dgf_r[...] += _colsum(dhn2 * xn2)
        dx1_r[0] = dx2 + _rms_bwd(dhn2 * gf, xn2, r2)

    lat = lambda w: pl.BlockSpec((1, ROW_BLOCK, w), lambda b, j: (b, j, 0))
    full2 = lambda r, c: pl.BlockSpec((r, c), lambda b, j: (0, 0))
    mod_spec = pl.BlockSpec((1, 8, D_MODEL), lambda b, j: (b, 0, 0))
    return pl.pallas_call(
        body, name="ffn", grid=(bs, nbl),
        in_specs=[lat(D_MODEL), lat(D_MODEL), mod_spec, full2(1, D_MODEL), full2(1, D_MODEL),
                  full2(D_MODEL, 2 * D_FF), full2(D_FF, D_MODEL)],
        out_specs=[lat(D_MODEL), lat(D_MODEL), lat(2 * D_FF), lat(D_FF), lat(D_MODEL),
                   full2(1, 1), full2(1, D_MODEL), full2(1, D_MODEL), mod_spec],
        out_shape=[SDS(x1.shape, F32), SDS(x1.shape, BF16), SDS((bs, seq, 2 * D_FF), BF16),
                   SDS((bs, seq, D_FF), BF16), SDS(x1.shape, BF16), SDS((1, 1), F32),
                   SDS((1, D_MODEL), F32), SDS((1, D_MODEL), F32), SDS((bs, 8, D_MODEL), F32)],
        compiler_params=_params(2),
    )(x1, target, mod_c, g_ffn, g_final, w_up, w_down)


def _row_tile(rows):
    return next(m * ROW_BLOCK for m in (4, 2, 1) if rows % (m * ROW_BLOCK) == 0)


def _matmul_tn(a, b, n_blocks, tk, name, carry=None):
    t, m = a.shape
    n = b.shape[1]
    tn = n // n_blocks

    def body(a_ref, b_ref, o_ref):
        @pl.when(pl.program_id(1) == 0)
        def _():
            o_ref[...] = jnp.zeros_like(o_ref)
        o_ref[0] += _dot_tn(a_ref[...], b_ref[...])

    (out,), carried = _host_call(
        body, name=name, grid=(n_blocks, t // tk),
        in_specs=[pl.BlockSpec((tk, m), lambda i, k: (k, 0)), pl.BlockSpec((tk, tn), lambda i, k: (k, i))],
        out_specs=[pl.BlockSpec((1, m, tn), lambda i, k: (i, 0, 0))],
        out_shape=[SDS((n_blocks, m, tn), F32)], args=(a, b), carry=carry)
    return out if carry is None else (out, carried)


SMALL_ROWS = 80
ROW_CCTX = 3


def _small_reduce(gathered, lbg):
    def body(g_ref, lbg_ref, s_ref, dgam_ref):
        tot = g_ref[0:SMALL_ROWS, :]
        for dev in range(1, N_DEV):
            tot = tot + g_ref[dev * SMALL_ROWS:(dev + 1) * SMALL_ROWS, :]
        s_ref[...] = tot
        cc = g_ref[ROW_CCTX:ROW_CCTX + 1, :]
        for dev in range(2, N_DEV, 2):
            cc = cc + g_ref[dev * SMALL_ROWS + ROW_CCTX:dev * SMALL_ROWS + ROW_CCTX + 1, :]
        s_ref[ROW_CCTX:ROW_CCTX + 1, :] = cc
        dlb = tot[7:8, :]
        for d in range(2):
            s0 = _sig(lbg_ref[0, d:d + 1, :] - lbg_ref[1, d:d + 1, :])
            dgam_ref[d:d + 1, :] = dlb[:, d * KW:(d + 1) * KW] * s0 * (1.0 - s0)

    return pl.pallas_call(
        body, name="small_reduce", out_shape=[SDS((SMALL_ROWS, D_MODEL), F32), SDS((2, KW), F32)],
        in_specs=[VMEM_SPEC] * 2, out_specs=[VMEM_SPEC] * 2,
    )(gathered, lbg)


def _pad_cols(a, width):
    return jnp.pad(a, ((0, 0), (0, width - a.shape[1])))


def kernel(x, c, ctx, c_ctx, w_mod, b_mod, g_mix, g_ffn, w_in, lb_gamma, g_norm_a, ln_v_g, ln_v_b, w_s, b_s, w_pa, w_pb, w_o, w_up, w_down, g_final, loss_target, m_c_ctx, m_w_mod, m_b_mod, m_g_mix, m_g_ffn, m_w_in, m_lb_gamma, m_g_norm_a, m_ln_v_g, m_ln_v_b, m_w_s, m_b_s, m_w_pa, m_w_pb, m_w_o, m_w_up, m_w_down, m_g_final, v_c_ctx, v_w_mod, v_b_mod, v_g_mix, v_g_ffn, v_w_in, v_lb_gamma, v_g_norm_a, v_ln_v_g, v_ln_v_b, v_w_s, v_b_s, v_w_pa, v_w_pb, v_w_o, v_w_up, v_w_down, v_g_final):
    ax, ay, ac = lax.axis_index("x"), lax.axis_index("y"), lax.axis_index("c")
    kc = 2 * ax + ay
    dev = 2 * kc + ac
    pos = jnp.stack([kc, ac]).astype(jnp.int32)
    bs, seq, _ = x.shape
    assert bs <= 4 and ctx.shape[1] == ROW_BLOCK and seq % ROW_BLOCK == 0
    mod_cols = w_mod.shape[2]

    lbg_row = _pad_cols(lb_gamma.reshape(1, -1), D_MODEL)
    pay1 = jnp.concatenate([c, jnp.zeros((4 - bs, D_MODEL), F32), c_ctx[None, :], lbg_row,
                            jnp.zeros((2, D_MODEL), F32)], axis=0)
    b_mod_s = lax.dynamic_slice(b_mod, (0, kc * mod_cols), (1, mod_cols))
    cond64, mod_g = _cond_and_mod(pay1, w_mod[0], b_mod_s)
    lbg_full = cond64.reshape(N_DEV, 8, D_MODEL)[0::2, 5, :KW].reshape(N_CHIPS, 2, 2, HEAD_DIM)
    lbg_full = jnp.transpose(lbg_full, (1, 2, 0, 3)).reshape(2, 2, KW)
    mod_g = mod_g.reshape(N_DEV, 64, mod_cols)[0::2]
    shards = [w_in[0], w_up[0], w_pa[0], w_pb[0], w_o[0], w_down[0]]
    bufs = _cast_bf16(pos, shards)
    mod_full = jnp.transpose(mod_g, (1, 0, 2)).reshape(64, N_CHIPS * mod_cols)
    mod_mine = lax.dynamic_slice(mod_full, (dev * 8, 0), (8, 6 * D_MODEL)).reshape(8, 6, D_MODEL)
    mod, mc = mod_mine[:bs], mod_mine[4]
    zeros4 = jnp.zeros((bs, 4, D_MODEL), F32)
    mod_a = jnp.concatenate([mod[:, 0:2], jnp.broadcast_to(mc[None, 0:2], (bs, 2, D_MODEL)), zeros4], axis=1)
    mod_c = jnp.concatenate([mod[:, 2:6], zeros4], axis=1)

    def cols_major(a):
        return jnp.transpose(a, (1, 0, 2)).reshape(a.shape[1], -1)

    gna, lng, lnb = g_norm_a, ln_v_g, ln_v_b
    ws3 = w_s[0]
    ws3_t = jnp.transpose(ws3, (0, 2, 1))
    bst = jnp.transpose(b_s[0])

    (p, h_all), (w_in_g,) = _in_fwd_own(pos, x, ctx, g_mix, mod_a, w_in[0], carry=_carry_gather_send(bufs[:1]),
                                        then=_carry_gather_forward(bufs[:1]))
    p, sent_up = _in_fwd_rest(pos, h_all, w_in_g, p, carry=_carry_gather_send(bufs[1:2]))
    w_in_f = cols_major(w_in_g)
    fwd_rest = _carry_gather_forward(bufs[2:])
    then_rest = _Carry([], [], {}, fwd_rest.sems, lambda i, o, s: fwd_rest.copies(i[1:], o[1:], s))
    (o_f, s_f, o_b, s_b), gathered = _hgrn_fwd(
        p, lbg_full, carry=_merge_carries(_carry_gather_forward(sent_up), _carry_gather_send(bufs[2:])),
        then=then_rest)
    w_up_f, w_pa_f, w_pb_f = (cols_major(a) for a in gathered[:3])
    w_o_f = gathered[3].reshape(-1, D_MODEL)
    w_down_f = gathered[4].reshape(-1, D_MODEL)
    x1 = _mix_fwd(p, o_f, o_b, x, mod_c, gna, lng, lnb, ws3, bst, w_pa_f, w_pb_f, w_o_f)
    dx1, h2, dab, hid, dffn, loss_part, dg_final, dg_ffn, dmod_ffn = _ffn(
        x1, loss_target, mod_c, g_ffn, g_final[None, :], w_up_f, w_down_f)
    rows_lat = bs * seq
    tk_lat = _row_tile(rows_lat)
    dw_up = _matmul_tn(h2.reshape(rows_lat, D_MODEL), dab.reshape(rows_lat, 2 * D_FF), N_CHIPS, tk_lat, "dw_up")
    dw_down = _matmul_tn(hid.reshape(rows_lat, D_FF), dffn.reshape(rows_lat, D_MODEL), 1, tk_lat, "dw_down")

    def shard_major(a):
        return jnp.transpose(a.reshape(a.shape[0], N_CHIPS, -1), (1, 0, 2))

    part_ffn = [dw_up, dw_down.reshape(N_CHIPS, -1, D_MODEL)]
    (do_raw, dpc, dw_pa, dw_pb, dw_o, dgna, dlng, dlnb, dws, dbst, dmod_mix), sib_ffn = _mix_bwd(
        p, o_f, o_b, dx1, mod_c, gna, lng, lnb, ws3, ws3_t, bst, w_pa_f, w_pb_f, w_o_f,
        carry=_carry_sibling_halves(part_ffn))
    cpbf_ffn = _rs_add_halves(pos, part_ffn, sib_ffn, "rs_add_ffn")
    part_mix = [shard_major(dw_pa), shard_major(dw_pb), dw_o.reshape(N_CHIPS, -1, D_MODEL)]
    (df_f, dq_f, dv_f, dlb0, df_b, dq_b, dv_b, dlb1), got = _hgrn_bwd_pair(
        p, lbg_full, (s_f, s_b), do_raw,
        carry=_merge_carries(_carry_to_owner(cpbf_ffn), _carry_sibling_halves(part_mix)))
    own_ffn, sib_mix = got[:2], got[2:]
    half_ffn = _rs_sum_owner(pos, part_ffn, sib_ffn, own_ffn, "rs_sum_ffn")
    cpbf_mix = _rs_add_halves(pos, part_mix, sib_mix, "rs_add_mix")
    (grad_x, dp, dg_mix, dmod_in), _ = _in_bwd(x, ctx, dx1, g_mix, mod_a, w_in_f, df_f, df_b, dv_f, dv_b, dq_f, dq_b,
                                               dpc)

    rows_all = dp.shape[0] * dp.shape[1]
    tk_all = _row_tile(rows_all)
    dw_in, got = _matmul_tn(h_all.reshape(rows_all, D_MODEL), dp.reshape(rows_all, IN_COLS), N_CHIPS, tk_all, "dw_in",
                            carry=_merge_carries(_carry_join_halves(half_ffn), _carry_to_owner(cpbf_mix)))
    g_ffn_w, own_mix = got[:2], got[2:]
    half_mix = _rs_sum_owner(pos, part_mix, sib_mix, own_mix, "rs_sum_mix")

    dmod_mine = jnp.concatenate([dmod_in[:, 0], dmod_in[:, 1], dmod_mix[:, 0], dmod_ffn[:, 1], dmod_ffn[:, 2],
                                 dmod_ffn[:, 3]], axis=1)
    dmc = jnp.concatenate([jnp.sum(dmod_in[:, 2], axis=0), jnp.sum(dmod_in[:, 3], axis=0),
                           jnp.zeros((4 * D_MODEL,), F32)])[None, :]
    pay3 = jnp.concatenate([dmod_mine, jnp.zeros((4 - bs, 6 * D_MODEL), F32), dmc,
                            jnp.zeros((3, 6 * D_MODEL), F32)], axis=0)
    dmod64, got = _all_gather8(pay3, "gather_dmod", carry=_merge_carries(_carry_sibling_halves([dw_in]),
                                                                          _carry_join_halves(half_mix)))
    sib_in, g_mix_w = got[:1], got[1:]
    cpbf_in = _rs_add_halves(pos, [dw_in], sib_in, "rs_add_w_in")
    dmod64_my = lax.dynamic_slice(dmod64, (0, kc * mod_cols), (64, mod_cols))
    g_w_mod, g_b_mod, g_cctx_part = _mod_bwd(cond64, dmod64, dmod64_my, w_mod[0], c_ctx[None, :])

    def row(*parts):
        return _pad_cols(jnp.concatenate([q.reshape(1, -1) for q in parts], axis=1), D_MODEL)

    small_rows = [dg_mix, dg_ffn, dg_final, g_cctx_part, row(dgna), row(dlng, dlnb), row(jnp.transpose(dbst)),
                  row(dlb0, dlb1), row(loss_part), jnp.zeros((7, D_MODEL), F32), dws.reshape(64, D_MODEL)]
    pay4 = jnp.concatenate(small_rows, axis=0)
    tot, dgam0 = _small_reduce(_all_gather8(pay4, "gather_small"), lbg_full)

    own_sems, own_src, own_land, own_token = _owner_send_start(cpbf_in[0], after=tot)

    rest_names = ["w_up", "w_pa", "w_pb", "w_o", "w_down", "w_mod"]
    rest_w = shards[1:] + [w_mod[0]]
    rest_g = [g_ffn_w[0], g_mix_w[0] + own_token[0, 0], g_mix_w[1], g_mix_w[2], g_ffn_w[1], g_w_mod]
    rest_m = [m_w_up[0], m_w_pa[0], m_w_pb[0], m_w_o[0], m_w_down[0], m_w_mod[0]]
    rest_v = [v_w_up[0], v_w_pa[0], v_w_pb[0], v_w_o[0], v_w_down[0], v_w_mod[0]]
    (ds_r, m2s_r, v2s_r), _ = _adamw_group(rest_w, rest_g, rest_m, rest_v, "adamw_rest")
    res = {}
    for name, g, d, m2, v2 in zip(rest_names, rest_g, ds_r, m2s_r, v2s_r):
        res[name] = (g[None], d[None], m2[None], v2[None])
    own_in = [_owner_send_wait(own_sems, own_src, own_land, after=(ds_r[0],))]
    g_in_w = _comm_call("rs_join_w_in",
                        _carry_join_halves(_rs_sum_owner(pos, [dw_in], sib_in, own_in, "rs_sum_w_in")))
    d, m2, v2 = _adamw_big(shards[0], g_in_w[0], m_w_in[0], v_w_in[0], "adamw_w_in")
    res["w_in"] = (g_in_w[0][None], d[None], m2[None], v2[None])

    loss = tot[8, 0]
    dgam_full = jnp.stack([dgam0, -dgam0])
    g_lbg = lax.dynamic_slice(dgam_full, (0, 0, kc * HEAD_DIM), (2, 2, HEAD_DIM))

    small = [
        ("c_ctx", c_ctx[None, :], tot[3:4], m_c_ctx, v_c_ctx),
        ("b_mod", b_mod, g_b_mod, m_b_mod, v_b_mod),
        ("g_mix", g_mix, tot[0:1], m_g_mix, v_g_mix),
        ("g_ffn", g_ffn, tot[1:2], m_g_ffn, v_g_ffn),
        ("lb_gamma", lb_gamma.reshape(4, HEAD_DIM), g_lbg.reshape(4, HEAD_DIM), m_lb_gamma, v_lb_gamma),
        ("g_norm_a", g_norm_a, tot[4:5, :HEAD_DIM], m_g_norm_a, v_g_norm_a),
        ("ln_v_g", ln_v_g, tot[5:6, :KW], m_ln_v_g, v_ln_v_g),
        ("ln_v_b", ln_v_b, tot[5:6, KW:], m_ln_v_b, v_ln_v_b),
        ("w_s", w_s.reshape(N_HEADS * SGU_CHUNK, SGU_CHUNK), tot[16:80].reshape(N_HEADS * SGU_CHUNK, SGU_CHUNK),
         m_w_s, v_w_s),
        ("b_s", b_s[0], tot[6:7, :KW].reshape(N_HEADS, SGU_CHUNK), m_b_s, v_b_s),
        ("g_final", g_final[None, :], tot[2:3], m_g_final, v_g_final),
    ]
    ws_, gs_ = [s[1] for s in small], [s[2] for s in small]
    ms_ = [s[3].reshape(s[1].shape) for s in small]
    vs_ = [s[4].reshape(s[1].shape) for s in small]
    ds_, m2s_, v2s_ = _adamw_small(ws_, gs_, ms_, vs_)
    for (name, _, g, m, _), d, m2, v2 in zip(small, ds_, m2s_, v2s_):
        res[name] = tuple(t.reshape(m.shape) for t in (g, d, m2, v2))

    order = ["c_ctx", "w_mod", "b_mod", "g_mix", "g_ffn", "w_in", "lb_gamma", "g_norm_a", "ln_v_g", "ln_v_b",
             "w_s", "b_s", "w_pa", "w_pb", "w_o", "w_up", "w_down", "g_final"]
    outs = [loss, grad_x]
    for part in range(4):
        outs += [res[n][part] for n in order]
    return tuple(outs)
```

```python
import functools
import math

import jax
import jax.numpy as jnp
import numpy as np
from jax import lax
from jax.experimental import pallas as pl
from jax.experimental.pallas import tpu as pltpu

F32 = jnp.float32
BF16 = jnp.bfloat16
SDS = jax.ShapeDtypeStruct
MESH = pl.DeviceIdType.MESH

EPS = 1e-6
D_MODEL = 1024
N_HEADS = 4
HEAD_DIM = 128
KW = N_HEADS * HEAD_DIM
IN_COLS = 11 * KW
D_FF = 2816
HGRN_CHUNK = 64
SGU_CHUNK = 128
ROW_BLOCK = 256
N_CHIPS = 4
N_DEV = 8
V7X_VMEM_BYTES = 64 * 1024 * 1024
VMEM_LIMIT = V7X_VMEM_BYTES - 6 * 1024 * 1024

ADAM_LR, ADAM_B1, ADAM_B2, ADAM_EPS, ADAM_WD, ADAM_STEP = 0.001, 0.9, 0.999, 1e-08, 0.01, 10
GELU_C0 = math.sqrt(2.0 / math.pi)
GELU_C1 = 0.044715

VMEM_SPEC = pl.BlockSpec(memory_space=pltpu.VMEM)
ANY_SPEC = pl.BlockSpec(memory_space=pl.ANY)


def _params(n_grid):
    return pltpu.CompilerParams(dimension_semantics=("arbitrary",) * n_grid, vmem_limit_bytes=VMEM_LIMIT)


def _sig(x):
    return 0.5 * jnp.tanh(0.5 * x) + 0.5


def _gelu(x):
    t = jnp.tanh(GELU_C0 * (x + GELU_C1 * x * x * x))
    return 0.5 * x * (1.0 + t), t


def _dgelu(x, t):
    return 0.5 * (1.0 + t) + 0.5 * x * (1.0 - t * t) * GELU_C0 * (1.0 + 3.0 * GELU_C1 * x * x)


def _dot(a, b):
    return jnp.dot(a.astype(BF16), b.astype(BF16), preferred_element_type=F32)


def _dot_nt(a, b):
    return lax.dot_general(a.astype(BF16), b.astype(BF16), (((1,), (1,)), ((), ())), preferred_element_type=F32)


def _dot_tn(a, b):
    return lax.dot_general(a.astype(BF16), b.astype(BF16), (((0,), (0,)), ((), ())), preferred_element_type=F32)


def _dot_f32(a, b, dims=(((1,), (0,)), ((), ()))):
    return lax.dot_general(a, b, dims, precision=lax.Precision.HIGHEST, preferred_element_type=F32)


def _rms(x):
    r = lax.rsqrt(jnp.mean(x * x, axis=-1, keepdims=True) + EPS)
    return x * r, r


def _rms_bwd(dxn, xn, r):
    return r * (dxn - xn * jnp.mean(dxn * xn, axis=-1, keepdims=True))


def _colsum(a):
    return jnp.sum(a, axis=0, keepdims=True)


def _tri(n, upper):
    t = lax.broadcasted_iota(jnp.int32, (n, n), 0)
    s = lax.broadcasted_iota(jnp.int32, (n, n), 1)
    return (s >= t) if upper else (s <= t)


def _all_gather8(x_shard, name, carry=None, then=None):
    m_per, n = x_shard.shape
    n_ci = len(carry.ins) if carry else 0
    n_co = len(carry.outs) if carry else 0
    n_cs = len(carry.sems) if carry else 0

    def body(*refs):
        x_ref, cins = refs[0], refs[1:1 + n_ci]
        out_ref, couts = refs[1 + n_ci], refs[2 + n_ci:2 + n_ci + n_co]
        send_sems, recv_sems, local_sem = refs[2 + n_ci + n_co:5 + n_ci + n_co]
        csems = refs[5 + n_ci + n_co:5 + n_ci + n_co + n_cs]
        tsems = refs[5 + n_ci + n_co + n_cs:]
        if carry:
            _start_all(carry.copies(cins, couts, csems)[0])
        _gather8_body(x_ref, out_ref, send_sems, recv_sems, local_sem, m_per)
        if carry:
            _wait_all(carry.copies(cins, couts, csems)[1])
        if then:
            _start_all(then.copies(couts, couts, tsems)[0])
            _wait_all(then.copies(couts, couts, tsems)[1])

    res = pl.pallas_call(
        body, name=name, out_shape=[SDS((N_DEV * m_per, n), x_shard.dtype)] + (carry.outs if carry else []),
        in_specs=[VMEM_SPEC] + [ANY_SPEC] * n_ci, out_specs=[VMEM_SPEC] + [ANY_SPEC] * n_co,
        input_output_aliases={1 + i: 1 + o for i, o in carry.alias.items()} if carry else {},
        scratch_shapes=[pltpu.SemaphoreType.DMA((7,)), pltpu.SemaphoreType.DMA((7,)), pltpu.SemaphoreType.DMA]
        + (carry.sems if carry else []) + (then.sems if then else []),
    )(x_shard, *(carry.ins if carry else []))
    return res[0] if carry is None else (res[0], list(res[1:]))


def _gather8_body(x_ref, out_ref, send_sems, recv_sems, local_sem, m_per):
    x, y, c = lax.axis_index("x"), lax.axis_index("y"), lax.axis_index("c")
    me, sibling = (x, y, c), (x, y, 1 - c)
    chips = [(1 - x, y), (x, 1 - y), (1 - x, 1 - y)]

    def rows(px, py, pc):
        return out_ref.at[pl.ds((4 * px + 2 * py + pc) * m_per, m_per), :]

    def copy(k, block, to, src=None):
        return pltpu.make_async_remote_copy(
            src_ref=rows(*block) if src is None else src, dst_ref=rows(*block),
            send_sem=send_sems.at[k], recv_sem=recv_sems.at[k], device_id=to, device_id_type=MESH)

    mine = pltpu.make_async_copy(x_ref, rows(*me), local_sem)
    mine.start()
    first = [copy(0, me, sibling, src=x_ref)]
    first += [copy(1 + j, me, (*chip, c), src=x_ref) for j, chip in enumerate(chips)]
    for cp in first:
        cp.start()
    passed = [copy(4 + j, (*chip, c), sibling) for j, chip in enumerate(chips)]
    for j, chip in enumerate(chips):
        copy(1 + j, (*chip, c), me).wait_recv()
        passed[j].start()
    copy(0, sibling, me).wait_recv()
    for j, chip in enumerate(chips):
        copy(4 + j, (*chip, 1 - c), me).wait_recv()
    for cp in first + passed:
        cp.wait_send()
    mine.wait()


def _mesh_pos():
    x, y, c = lax.axis_index("x"), lax.axis_index("y"), lax.axis_index("c")
    chips = [(1 - x, y), (x, 1 - y), (1 - x, 1 - y)]
    return x, y, c, 2 * x + y, (x, y, 1 - c), chips


def _half_rows(c, rh):
    return pl.ds(pl.multiple_of(c * rh, 16), rh)


class _Carry:
    def __init__(self, ins, outs, alias, sems, copies):
        self.ins, self.outs, self.alias, self.sems, self.copies = list(ins), list(outs), dict(alias), list(sems), copies


def _remote(src, dst, send, recv, to):
    return functools.partial(pltpu.make_async_remote_copy, src_ref=src, dst_ref=dst, send_sem=send, recv_sem=recv,
                             device_id=to, device_id_type=MESH)


def _carry_gather_send(bufs):
    n = len(bufs)

    def copies(ins, outs, sems):
        x, y, c, kc, sibling, chips = _mesh_pos()
        starts, waits = [], []
        for wi in range(n):
            rh = outs[wi].shape[1] // 2
            for jj, chip in enumerate(chips):
                mine = outs[wi].at[kc, _half_rows(c, rh), :]
                cp = _remote(mine, mine, sems[0].at[wi, jj], sems[1].at[wi, jj], (*chip, c))
                starts.append(cp)
                waits.append((cp, "send"))
                theirs = outs[wi].at[2 * chip[0] + chip[1], _half_rows(c, rh), :]
                waits.append((_remote(theirs, theirs, sems[0].at[wi, jj], sems[1].at[wi, jj], (*chip, c)), "recv"))
        return starts, waits

    return _Carry(bufs, [SDS(b.shape, b.dtype) for b in bufs], {i: i for i in range(n)},
                  [pltpu.SemaphoreType.DMA((n, 3)), pltpu.SemaphoreType.DMA((n, 3))], copies)


def _carry_gather_forward(bufs):
    n = len(bufs)

    def copies(ins, outs, sems):
        x, y, c, kc, sibling, chips = _mesh_pos()
        starts, waits = [], []
        for wi in range(n):
            rh = outs[wi].shape[1] // 2
            for jj, chip in enumerate(chips):
                got = outs[wi].at[2 * chip[0] + chip[1], _half_rows(c, rh), :]
                cp = _remote(got, got, sems[0].at[wi, jj], sems[1].at[wi, jj], sibling)
                starts.append(cp)
                waits.append((cp, "send"))
                other = outs[wi].at[2 * chip[0] + chip[1], _half_rows(1 - c, rh), :]
                waits.append((_remote(other, other, sems[0].at[wi, jj], sems[1].at[wi, jj], sibling), "recv"))
        return starts, waits

    return _Carry(bufs, [SDS(b.shape, b.dtype) for b in bufs], {i: i for i in range(n)},
                  [pltpu.SemaphoreType.DMA((n, 3)), pltpu.SemaphoreType.DMA((n, 3))], copies)


def _carry_sibling_halves(grads):
    n = len(grads)

    def copies(ins, outs, sems):
        x, y, c, kc, sibling, chips = _mesh_pos()
        cps = [_remote(ins[wi].at[:, _half_rows(1 - c, ins[wi].shape[1] // 2), :], outs[wi],
                       sems[0].at[wi], sems[1].at[wi], sibling) for wi in range(n)]
        return cps, [(cp, "both") for cp in cps]

    return _Carry(grads, [SDS((N_CHIPS, g.shape[1] // 2, g.shape[2]), F32) for g in grads], {},
                  [pltpu.SemaphoreType.DMA((n,)), pltpu.SemaphoreType.DMA((n,))], copies)


def _carry_to_owner(cpbfs):
    n = len(cpbfs)

    def copies(ins, outs, sems):
        x, y, c, kc, sibling, chips = _mesh_pos()
        starts, waits = [], []
        for wi in range(n):
            for jj, chip in enumerate(chips):
                cp = _remote(ins[wi].at[2 * chip[0] + chip[1]], outs[wi].at[kc],
                             sems[0].at[wi, jj], sems[1].at[wi, jj], (*chip, c))
                starts.append(cp)
                waits.append((cp, "send"))
                slot = outs[wi].at[2 * chip[0] + chip[1]]
                waits.append((_remote(slot, slot, sems[0].at[wi, jj], sems[1].at[wi, jj], (*chip, c)), "recv"))
        return starts, waits

    return _Carry(cpbfs, [SDS(g.shape, BF16) for g in cpbfs], {},
                  [pltpu.SemaphoreType.DMA((n, 3)), pltpu.SemaphoreType.DMA((n, 3))], copies)


def _carry_join_halves(bufs):
    n = len(bufs)

    def copies(ins, outs, sems):
        x, y, c, kc, sibling, chips = _mesh_pos()
        cps = []
        for wi in range(n):
            mine = outs[wi].at[_half_rows(c, outs[wi].shape[0] // 2), :]
            cps.append(_remote(mine, mine, sems[0].at[wi], sems[1].at[wi], sibling))
        return cps, [(cp, "both") for cp in cps]

    return _Carry(bufs, [SDS(b.shape, F32) for b in bufs], {i: i for i in range(n)},
                  [pltpu.SemaphoreType.DMA((n,)), pltpu.SemaphoreType.DMA((n,))], copies)


def _merge_carries(*carries):
    ins, outs, alias, sems, parts = [], [], {}, [], []
    for cy in carries:
        parts.append((len(ins), len(cy.ins), len(outs), len(cy.outs), len(sems), len(cy.sems), cy.copies))
        alias.update({len(ins) + i: len(outs) + o for i, o in cy.alias.items()})
        ins += cy.ins
        outs += cy.outs
        sems += cy.sems

    def copies(i, o, s):
        starts, waits = [], []
        for i0, ni, o0, no, s0, ns, fn in parts:
            st, wt = fn(i[i0:i0 + ni], o[o0:o0 + no], s[s0:s0 + ns])
            starts += st
            waits += wt
        return starts, waits

    return _Carry(ins, outs, alias, sems, copies)


def _start_all(starts):
    for cp in starts:
        cp().start()


def _wait_all(waits):
    for cp, which in waits:
        if which == "send":
            cp().wait_send()
        elif which == "recv":
            cp().wait_recv()
        else:
            cp().wait()


HBM_SPEC = pl.BlockSpec(memory_space=pltpu.HBM)
SEM_SPEC = pl.BlockSpec(memory_space=pltpu.SEMAPHORE)
SPLIT_COPY_EFFECT = pltpu.SideEffectType.DATAFLOW_SIDE_EFFECTING


def _owner_send_start(cpbf, after):
    land = lax.empty(cpbf.shape, cpbf.dtype)

    def body(src_ref, land_ref, after_ref, s0, s1, s2, r0, r1, r2, src_thru, land_thru, token):
        x, y, c, kc, sibling, chips = _mesh_pos()
        for jj, (chip, s_sem, r_sem) in enumerate(zip(chips, (s0, s1, s2), (r0, r1, r2))):
            pltpu.make_async_remote_copy(
                src_ref=src_ref.at[2 * chip[0] + chip[1]], dst_ref=land_ref.at[kc], send_sem=s_sem, recv_sem=r_sem,
                device_id=(*chip, c), device_id_type=MESH).start()
        token[...] = jnp.zeros_like(token)

    buf = pltpu.HBM(cpbf.shape, cpbf.dtype)
    outs = pl.pallas_call(
        body, name="rs_owner_w_in_start",
        out_shape=(pltpu.SemaphoreType.DMA(()),) * 6 + (buf, buf, SDS((8, 128), F32)),
        in_specs=(HBM_SPEC, HBM_SPEC, ANY_SPEC), out_specs=(SEM_SPEC,) * 6 + (HBM_SPEC, HBM_SPEC, VMEM_SPEC),
        input_output_aliases={0: 6, 1: 7},
        compiler_params=pltpu.CompilerParams(has_side_effects=SPLIT_COPY_EFFECT),
    )(pltpu.with_memory_space_constraint(cpbf, pltpu.HBM), pltpu.with_memory_space_constraint(land, pltpu.HBM), after)
    return outs[:6], outs[6], outs[7], outs[8]


def _owner_send_wait(sems, src_thru, land_thru, after):
    n_after = len(after)

    def body(*refs):
        src_ref, land_ref = refs[0], refs[1]
        sends, recvs = refs[2:5], refs[5:8]
        x, y, c, kc, sibling, chips = _mesh_pos()
        for jj, chip in enumerate(chips):
            slot = 2 * chip[0] + chip[1]
            cp = pltpu.make_async_remote_copy(
                src_ref=src_ref.at[slot], dst_ref=land_ref.at[slot], send_sem=sends[jj], recv_sem=recvs[jj],
                device_id=(*chip, c), device_id_type=MESH)
            cp.wait_send()
            cp.wait_recv()

    buf = pltpu.HBM(land_thru.shape, land_thru.dtype)
    return pl.pallas_call(
        body, name="rs_owner_w_in_wait", out_shape=(buf, buf),
        in_specs=(HBM_SPEC, HBM_SPEC) + (SEM_SPEC,) * 6 + (ANY_SPEC,) * n_after, out_specs=(HBM_SPEC, HBM_SPEC),
        input_output_aliases={0: 0, 1: 1},
        compiler_params=pltpu.CompilerParams(has_side_effects=SPLIT_COPY_EFFECT),
    )(src_thru, land_thru, *sems, *after)[1]


def _comm_call(name, carry):
    n_i, n_o = len(carry.ins), len(carry.outs)

    def body(*refs):
        ins, outs, sems = refs[:n_i], refs[n_i:n_i + n_o], refs[n_i + n_o:]
        _start_all(carry.copies(ins, outs, sems)[0])
        _wait_all(carry.copies(ins, outs, sems)[1])

    return pl.pallas_call(
        body, name=name, out_shape=carry.outs, in_specs=[ANY_SPEC] * n_i, out_specs=[ANY_SPEC] * n_o,
        input_output_aliases=carry.alias, scratch_shapes=carry.sems,
    )(*carry.ins)


def _host_call(body, *, name, grid, in_specs, out_specs, out_shape, args, scratch_shapes=(), carry=None, then=None,
               prefetch=None, aliases=None):
    n_in, n_out, n_scr = len(in_specs), len(out_specs), len(scratch_shapes)
    n_ci = len(carry.ins) if carry else 0
    n_co = len(carry.outs) if carry else 0
    n_cs = len(carry.sems) if carry else 0
    n_pf = 0 if prefetch is None else 1

    def wrapped(*refs):
        pf, refs = refs[:n_pf], refs[n_pf:]
        ins, cins = refs[:n_in], refs[n_in:n_in + n_ci]
        o0 = n_in + n_ci
        outs, couts = refs[o0:o0 + n_out], refs[o0 + n_out:o0 + n_out + n_co]
        s0 = o0 + n_out + n_co
        scr, sems, tsems = refs[s0:s0 + n_scr], refs[s0 + n_scr:s0 + n_scr + n_cs], refs[s0 + n_scr + n_cs:]
        idx = [pl.program_id(a) for a in range(len(grid))]
        first = functools.reduce(jnp.logical_and, [i == 0 for i in idx])
        last = functools.reduce(jnp.logical_and, [i == g - 1 for i, g in zip(idx, grid)])

        if carry:
            @pl.when(first)
            def _():
                _start_all(carry.copies(cins, couts, sems)[0])

        body(*pf, *ins, *outs, *scr)

        if carry:
            @pl.when(last)
            def _():
                _wait_all(carry.copies(cins, couts, sems)[1])
                if then:
                    _start_all(then.copies(couts, couts, tsems)[0])
                    _wait_all(then.copies(couts, couts, tsems)[1])

    all_in = list(in_specs) + [ANY_SPEC] * n_ci
    all_out = list(out_specs) + [ANY_SPEC] * n_co
    all_scr = list(scratch_shapes) + (carry.sems if carry else []) + (then.sems if then else [])
    alias = {n_pf + i: o for i, o in (aliases or {}).items()}
    if carry:
        alias.update({n_pf + n_in + i: n_out + o for i, o in carry.alias.items()})
    kwargs = dict(name=name, out_shape=list(out_shape) + (carry.outs if carry else []), input_output_aliases=alias,
                  compiler_params=_params(len(grid)))
    if prefetch is None:
        call = pl.pallas_call(wrapped, grid=grid, in_specs=all_in, out_specs=all_out, scratch_shapes=all_scr, **kwargs)
        res = call(*args, *(carry.ins if carry else []))
    else:
        call = pl.pallas_call(wrapped, grid_spec=pltpu.PrefetchScalarGridSpec(
            num_scalar_prefetch=1, grid=grid, in_specs=all_in, out_specs=all_out, scratch_shapes=all_scr), **kwargs)
        res = call(prefetch, *args, *(carry.ins if carry else []))
    return list(res[:n_out]), list(res[n_out:])


def _rs_add_halves(pos, grads, recvs, name):
    n = len(grads)

    def body(pos_ref, *refs):
        for i in range(n):
            refs[2 * n + i][...] = (refs[i][...] + refs[n + i][...]).astype(BF16)

    blks = [(1, g.shape[1] // 4, g.shape[2]) for g in grads]
    mine = [pl.BlockSpec(b, lambda k, i, p: (k, p[1] * 2 + i, 0)) for b in blks]
    half = [pl.BlockSpec(b, lambda k, i, p: (k, i, 0)) for b in blks]
    return pl.pallas_call(
        body, name=name,
        grid_spec=pltpu.PrefetchScalarGridSpec(
            num_scalar_prefetch=1, grid=(N_CHIPS, 2), in_specs=mine + half, out_specs=half),
        out_shape=[SDS((N_CHIPS, g.shape[1] // 2, g.shape[2]), BF16) for g in grads],
        compiler_params=_params(2),
    )(pos, *grads, *recvs)


def _rs_sum_owner(pos, grads, recvs, recv3s, name):
    n = len(grads)

    def body(pos_ref, *refs):
        for i in range(n):
            g, s, r1, r2, r3 = (refs[j * n + i] for j in range(5))
            own = g[0] + s[0]
            refs[5 * n + i][...] = ((own + r1[0].astype(F32)) + r2[0].astype(F32)) + r3[0].astype(F32)

    blks = [(1, g.shape[1] // 4, g.shape[2]) for g in grads]
    mine = [pl.BlockSpec(b, lambda i, p: (p[0], p[1] * 2 + i, 0)) for b in blks]

    def slot(d):
        return [pl.BlockSpec(b, lambda i, p: ((p[0] + d) % N_CHIPS, i, 0)) for b in blks]

    return pl.pallas_call(
        body, name=name,
        grid_spec=pltpu.PrefetchScalarGridSpec(
            num_scalar_prefetch=1, grid=(2,), in_specs=mine + slot(0) + slot(1) + slot(2) + slot(3),
            out_specs=[pl.BlockSpec(b[1:], lambda i, p: (p[1] * 2 + i, 0)) for b in blks]),
        out_shape=[SDS(g.shape[1:], F32) for g in grads],
        compiler_params=_params(1),
    )(pos, *grads, *recvs, *recv3s, *recv3s, *recv3s)


def _cast_bf16(pos, arrs):
    n = len(arrs)

    def body(pos_ref, *refs):
        for i in range(n):
            refs[n + i][0] = refs[i][...].astype(BF16)

    return pl.pallas_call(
        body, name="cast_bf16",
        grid_spec=pltpu.PrefetchScalarGridSpec(
            num_scalar_prefetch=1, grid=(2,),
            in_specs=[pl.BlockSpec((a.shape[0] // 2, a.shape[1]), lambda i, p: (i, 0)) for a in arrs],
            out_specs=[pl.BlockSpec((1, a.shape[0] // 2, a.shape[1]), lambda i, p: (p[0], i, 0)) for a in arrs]),
        out_shape=[SDS((N_CHIPS,) + a.shape, BF16) for a in arrs],
        compiler_params=_params(1),
    )(pos, *arrs)


def _adamw_vals(w, g, m, v):
    m2 = ADAM_B1 * m + (1.0 - ADAM_B1) * g
    v2 = ADAM_B2 * v + (1.0 - ADAM_B2) * (g * g)
    m_hat = m2 / (1.0 - ADAM_B1 ** ADAM_STEP)
    v_hat = v2 / (1.0 - ADAM_B2 ** ADAM_STEP)
    delta = -ADAM_LR * (m_hat / (jnp.sqrt(v_hat) + ADAM_EPS) + ADAM_WD * w)
    return delta, m2, v2


def _adamw_big(w, g, m, v, name):
    rows, cols = w.shape
    rb = rows // 4

    def body(w_ref, g_ref, m_ref, v_ref, d_ref, m2_ref, v2_ref):
        d, m2, v2 = _adamw_vals(w_ref[...], g_ref[...], m_ref[...], v_ref[...])
        d_ref[...] = d
        m2_ref[...] = m2
        v2_ref[...] = v2

    spec = pl.BlockSpec((rb, cols), lambda i: (i, 0))
    return pl.pallas_call(
        body, name=name, grid=(4,), in_specs=[spec] * 4, out_specs=[spec] * 3,
        out_shape=[SDS(w.shape, F32)] * 3, compiler_params=_params(1),
    )(w, g, m, v)


ADAMW_GROUP_STEPS = 8


def _adamw_group(ws, gs, ms, vs, name, carry=None):
    n = len(ws)

    def body(*refs):
        for i in range(n):
            d, m2, v2 = _adamw_vals(refs[i][...], refs[n + i][...], refs[2 * n + i][...], refs[3 * n + i][...])
            refs[4 * n + i][...] = d
            refs[5 * n + i][...] = m2
            refs[6 * n + i][...] = v2

    specs = [pl.BlockSpec((w.shape[0] // ADAMW_GROUP_STEPS, w.shape[1]), lambda i: (i, 0)) for w in ws]
    shapes = [SDS(w.shape, F32) for w in ws]
    outs, carried = _host_call(
        body, name=name, grid=(ADAMW_GROUP_STEPS,), in_specs=specs * 4, out_specs=specs * 3, out_shape=shapes * 3,
        args=(*ws, *gs, *ms, *vs), carry=carry)
    return (outs[:n], outs[n:2 * n], outs[2 * n:]), carried


def _adamw_small(ws, gs, ms, vs):
    n = len(ws)

    def body(*refs):
        for i in range(n):
            d, m2, v2 = _adamw_vals(refs[i][...], refs[n + i][...], refs[2 * n + i][...], refs[3 * n + i][...])
            refs[4 * n + i][...] = d
            refs[5 * n + i][...] = m2
            refs[6 * n + i][...] = v2

    shapes = [SDS(w.shape, F32) for w in ws]
    outs = pl.pallas_call(
        body, name="adamw_small", out_shape=shapes * 3,
        in_specs=[VMEM_SPEC] * (4 * n), out_specs=[VMEM_SPEC] * (3 * n),
    )(*ws, *gs, *ms, *vs)
    return outs[:n], outs[n:2 * n], outs[2 * n:]


def _cond_and_mod(pay, w_mod_s, b_mod_s):
    m_per, n_cols = pay.shape[0], w_mod_s.shape[1]
    rows = N_DEV * m_per

    def body(pay_ref, w_ref, b_ref, cond_ref, modg_ref, mod_s, sems_a, semr_a, loc_a, sems_b, semr_b, loc_b):
        _gather8_body(pay_ref, cond_ref, sems_a, semr_a, loc_a, m_per)
        cc = cond_ref[...]
        mod_s[...] = _dot_f32(cc * _sig(cc), w_ref[...]) + b_ref[...]
        _gather8_body(mod_s, modg_ref, sems_b, semr_b, loc_b, rows)

    dma7 = pltpu.SemaphoreType.DMA((7,))
    return pl.pallas_call(
        body, name="cond_and_mod",
        out_shape=[SDS((rows, pay.shape[1]), F32), SDS((N_DEV * rows, n_cols), F32)],
        in_specs=[VMEM_SPEC] * 3, out_specs=[VMEM_SPEC] * 2,
        scratch_shapes=[pltpu.VMEM((rows, n_cols), F32), dma7, dma7, pltpu.SemaphoreType.DMA,
                        dma7, dma7, pltpu.SemaphoreType.DMA],
        compiler_params=pltpu.CompilerParams(vmem_limit_bytes=VMEM_LIMIT),
    )(pay, w_mod_s, b_mod_s)


def _mod_bwd(cond64, dmod64, dmod64_my, w_mod_s, c_ctx):
    def body(c_ref, g_ref, gm_ref, w_ref, cc_ref, gw_ref, gb_ref, gcc_ref):
        cc = c_ref[...]
        act = cc * _sig(cc)
        gm = gm_ref[...]
        gw_ref[...] = _dot_f32(act, gm, (((0,), (0,)), ((), ())))
        gb_ref[...] = _colsum(g_ref[...])
        dact = _dot_f32(gm, w_ref[...], (((1,), (1,)), ((), ())))
        tot = dact[4:5, :]
        for dev in range(1, N_DEV):
            tot = tot + dact[8 * dev + 4:8 * dev + 5, :]
        c0 = cc_ref[...]
        s0 = _sig(c0)
        gcc_ref[...] = tot * (s0 * (1.0 + c0 * (1.0 - s0)))

    return pl.pallas_call(
        body, name="mod_bwd",
        out_shape=[SDS(w_mod_s.shape, F32), SDS((1, dmod64.shape[1]), F32), SDS((1, D_MODEL), F32)],
        in_specs=[VMEM_SPEC] * 5, out_specs=[VMEM_SPEC] * 3,
        compiler_params=pltpu.CompilerParams(vmem_limit_bytes=VMEM_LIMIT),
    )(cond64, dmod64, dmod64_my, w_mod_s, c_ctx)


SHARD_COLS = IN_COLS // N_CHIPS


def _in_fwd_own(pos, x, ctx, g_mix, mod_a, w_own, carry=None, then=None):
    bs, seq, _ = x.shape
    nb = seq // ROW_BLOCK + 1

    def body(pos_ref, x_ref, ctx_ref, g_ref, mod_ref, w_ref, p_ref, h_ref, w_bf):
        b, j = pl.program_id(0), pl.program_id(1)

        @pl.when((b == 0) & (j == 0))
        def _():
            w_bf[...] = w_ref[...].astype(BF16)

        is_ctx = j == 0
        xin = jnp.where(is_ctx, ctx_ref[0], x_ref[0])
        shift = jnp.where(is_ctx, mod_ref[0, 2:3, :], mod_ref[0, 0:1, :])
        scale = jnp.where(is_ctx, mod_ref[0, 3:4, :], mod_ref[0, 1:2, :])
        xn, _ = _rms(xin)
        hb = ((xn * g_ref[...]) * (1.0 + scale) + shift).astype(BF16)
        h_ref[0] = hb
        p_ref[0] = jnp.dot(hb, w_bf[...], preferred_element_type=F32)

    return _host_call(
        body, name="in_fwd_own", grid=(bs, nb), prefetch=pos,
        in_specs=[pl.BlockSpec((1, ROW_BLOCK, D_MODEL), lambda b, j, p: (b, jnp.maximum(j - 1, 0), 0)),
                  pl.BlockSpec((1, ROW_BLOCK, D_MODEL), lambda b, j, p: (b, 0, 0)),
                  pl.BlockSpec((1, D_MODEL), lambda b, j, p: (0, 0)),
                  pl.BlockSpec((1, 8, D_MODEL), lambda b, j, p: (b, 0, 0)),
                  pl.BlockSpec((D_MODEL, SHARD_COLS), lambda b, j, p: (0, 0))],
        out_specs=[pl.BlockSpec((1, ROW_BLOCK, SHARD_COLS), lambda b, j, p: (b, j, p[0])),
                   pl.BlockSpec((1, ROW_BLOCK, D_MODEL), lambda b, j, p: (b, j, 0))],
        out_shape=[SDS((bs, nb * ROW_BLOCK, IN_COLS), F32), SDS((bs, nb * ROW_BLOCK, D_MODEL), BF16)],
        scratch_shapes=[pltpu.VMEM((D_MODEL, SHARD_COLS), BF16)],
        args=(x, ctx, g_mix, mod_a, w_own), carry=carry, then=then)


def _in_fwd_rest(pos, h_all, w_in_g, p, carry=None):
    bs, rows, _ = h_all.shape
    rows_all = bs * rows
    tile = next(m * ROW_BLOCK for m in (9, 3, 1) if rows_all % (m * ROW_BLOCK) == 0)

    def body(pos_ref, h_ref, w_ref, p_in_ref, p_ref):
        p_ref[...] = jnp.dot(h_ref[...], w_ref[0], preferred_element_type=F32)

    shard = lambda n, p: (p[0] + 1 + n) % N_CHIPS
    (p2,), carried = _host_call(
        body, name="in_fwd_rest", grid=(N_CHIPS - 1, rows_all // tile), prefetch=pos,
        in_specs=[pl.BlockSpec((tile, D_MODEL), lambda n, t, p: (t, 0)),
                  pl.BlockSpec((1, D_MODEL, SHARD_COLS), lambda n, t, p: (shard(n, p), 0, 0)),
                  ANY_SPEC],
        out_specs=[pl.BlockSpec((tile, SHARD_COLS), lambda n, t, p: (t, shard(n, p)))],
        out_shape=[SDS((rows_all, IN_COLS), F32)], aliases={2: 0},
        args=(h_all.reshape(rows_all, D_MODEL), w_in_g, p.reshape(rows_all, IN_COLS)), carry=carry)
    return p2.reshape(bs, rows, IN_COLS), carried


def _in_bwd(x, ctx, dx1, g_mix, mod_a, w_in, df_f, df_b, dv_f, dv_b, dq_f, dq_b, dpc, carry=None):
    bs, seq, _ = x.shape
    nb = seq // ROW_BLOCK + 1

    def body(x_ref, ctx_ref, dx1_ref, g_ref, mod_ref, w_ref, dff_ref, dfb_ref, dvf_ref, dvb_ref, dqf_ref, dqb_ref,
             dpc_ref, gx_ref, dp_ref, dg_ref, dmod_ref):
        b, j = pl.program_id(0), pl.program_id(1)
        is_ctx = j == 0

        @pl.when((b == 0) & (j == 0))
        def _():
            dg_ref[...] = jnp.zeros_like(dg_ref)

        @pl.when(j == 0)
        def _():
            dmod_ref[...] = jnp.zeros_like(dmod_ref)

        di = (dvf_ref[0] + dvb_ref[0]).astype(BF16)
        dq = (dqf_ref[0] + dqb_ref[0]).astype(BF16)
        dp = jnp.concatenate([dff_ref[0], dfb_ref[0], di, dq, dpc_ref[0]], axis=1)
        dp_ref[0] = dp
        dh = lax.dot_general(dp, w_ref[...], (((1,), (1,)), ((), ())), preferred_element_type=F32)
        xin = jnp.where(is_ctx, ctx_ref[0], x_ref[0])
        scale = jnp.where(is_ctx, mod_ref[0, 3:4, :], mod_ref[0, 1:2, :])
        xn, r = _rms(xin)
        g = g_ref[...]
        hn = xn * g
        d_shift = _colsum(dh)
        d_scale = _colsum(dh * hn)
        dhn = dh * (1.0 + scale)
        dg_ref[...] += _colsum(dhn * xn)
        dx = _rms_bwd(dhn * g, xn, r)

        @pl.when(is_ctx)
        def _():
            dmod_ref[0, 2:3, :] += d_shift
            dmod_ref[0, 3:4, :] += d_scale

        @pl.when(jnp.logical_not(is_ctx))
        def _():
            dmod_ref[0, 0:1, :] += d_shift
            dmod_ref[0, 1:2, :] += d_scale
            gx_ref[0] = dx + dx1_ref[0]

    def rows(w):
        return pl.BlockSpec((1, ROW_BLOCK, w), lambda b, j: (b, j, 0))

    lat = pl.BlockSpec((1, ROW_BLOCK, D_MODEL), lambda b, j: (b, jnp.maximum(j - 1, 0), 0))
    return _host_call(
        body, name="in_bwd", grid=(bs, nb),
        in_specs=[lat, pl.BlockSpec((1, ROW_BLOCK, D_MODEL), lambda b, j: (b, 0, 0)), lat,
                  pl.BlockSpec((1, D_MODEL), lambda b, j: (0, 0)),
                  pl.BlockSpec((1, 8, D_MODEL), lambda b, j: (b, 0, 0)),
                  pl.BlockSpec((D_MODEL, IN_COLS), lambda b, j: (0, 0)),
                  rows(KW), rows(KW), rows(KW), rows(KW), rows(KW), rows(KW), rows(7 * KW)],
        out_specs=[lat, rows(IN_COLS), pl.BlockSpec((1, D_MODEL), lambda b, j: (0, 0)),
                   pl.BlockSpec((1, 8, D_MODEL), lambda b, j: (b, 0, 0))],
        out_shape=[SDS(x.shape, F32), SDS((bs, nb * ROW_BLOCK, IN_COLS), BF16), SDS((1, D_MODEL), F32),
                   SDS((bs, 8, D_MODEL), F32)],
        args=(x, ctx, dx1, g_mix, mod_a, w_in, df_f, df_b, dv_f, dv_b, dq_f, dq_b, dpc), carry=carry)


def _lower_bound(lbg_ref, direction):
    return _sig(lbg_ref[0, direction:direction + 1, :] - lbg_ref[1, direction:direction + 1, :])


def _block_tri(upper):
    t = np.arange(ROW_BLOCK)[:, None]
    s = np.arange(ROW_BLOCK)[None, :]
    same = (t // HGRN_CHUNK) == (s // HGRN_CHUNK)
    return jnp.asarray(same & ((s >= t) if upper else (s <= t)), dtype=BF16)


TRI_SPEC = pl.BlockSpec((ROW_BLOCK, ROW_BLOCK), lambda b, j: (0, 0))


def _tri_matmul_f32(tri, g):
    g0 = g.astype(BF16)
    r1 = g - g0.astype(F32)
    g1 = r1.astype(BF16)
    g2 = (r1 - g1.astype(F32)).astype(BF16)
    return (jnp.dot(tri, g2, preferred_element_type=F32) + jnp.dot(tri, g1, preferred_element_type=F32)) \
        + jnp.dot(tri, g0, preferred_element_type=F32)


def _chunk_rows(rows):
    return jnp.concatenate([jnp.broadcast_to(r, (HGRN_CHUNK, r.shape[1])) for r in rows], axis=0)


def _block_gates(fl, q, lb, tri, upper):
    t = {}
    t["sg"] = _sig(fl)
    t["f"] = lb + (1.0 - lb) * t["sg"]
    k = 1.0 - t["f"]
    bcum = _tri_matmul_f32(tri, jnp.log(t["f"]))
    ends = [bcum[ci * HGRN_CHUNK:ci * HGRN_CHUNK + 1] if upper else bcum[(ci + 1) * HGRN_CHUNK - 1:(ci + 1) * HGRN_CHUNK]
            for ci in range(fl.shape[0] // HGRN_CHUNK)]
    mid = _chunk_rows([0.5 * r for r in ends])
    t["dec"] = [jnp.exp(r) for r in ends]
    t["e1"] = jnp.exp(bcum - mid)
    t["e2"] = jnp.exp(mid - bcum)
    t["eh"] = _chunk_rows([jnp.exp(0.5 * r) for r in ends])
    t["qi"] = q * t["e1"]
    t["ki"] = k * t["e2"]
    t["kd"] = t["ki"] * t["eh"]
    t["qe"] = t["qi"] * t["eh"]
    return t


def _hgrn_block_order(direction, nb):
    if direction == 0:
        return lambda j: j
    return lambda j: jnp.where(j == 0, 0, nb - j)


def _hgrn_fwd(p, lbg, carry=None, then=None):
    bs, rows, _ = p.shape
    nb = rows // ROW_BLOCK
    ncb = ROW_BLOCK // HGRN_CHUNK
    orders = [_hgrn_block_order(d, nb) for d in (0, 1)]
    dirs = (0, 1)

    def body(f0_ref, i0_ref, q0_ref, f1_ref, i1_ref, q1_ref, lbg_ref, tri0_ref, tri1_ref,
             o0_ref, s0_ref, o1_ref, s1_ref, st):
        @pl.when(pl.program_id(1) == 0)
        def _():
            st[...] = jnp.zeros_like(st)

        f_refs, i_refs, q_refs = (f0_ref, f1_ref), (i0_ref, i1_ref), (q0_ref, q1_ref)
        tri_refs, o_refs, s_refs = (tri0_ref, tri1_ref), (o0_ref, o1_ref), (s0_ref, s1_ref)
        chunk = lambda a, ci, h: a[ci * HGRN_CHUNK:(ci + 1) * HGRN_CHUNK, h * HEAD_DIM:(h + 1) * HEAD_DIM]
        masks = [_tri(HGRN_CHUNK, d == 1) for d in dirs]
        t = [_block_gates(f_refs[d][0], q_refs[d][0], _lower_bound(lbg_ref, d), tri_refs[d][...], d == 1) for d in dirs]
        v = [i_refs[d][0] for d in dirs]
        intra = [[[None] * N_HEADS for _ in range(ncb)] for _ in dirs]
        ds_loc = [[[None] * N_HEADS for _ in range(ncb)] for _ in dirs]
        for ci in range(ncb):
            for h in range(N_HEADS):
                for d in dirs:
                    a = jnp.where(masks[d], _dot_nt(chunk(t[d]["qi"], ci, h), chunk(t[d]["ki"], ci, h)), 0.0)
                    intra[d][ci][h] = _dot(a, chunk(v[d], ci, h))
                    ds_loc[d][ci][h] = _dot_tn(chunk(v[d], ci, h), chunk(t[d]["kd"], ci, h))
        for h in range(N_HEADS):
            ls = slice(h * HEAD_DIM, (h + 1) * HEAD_DIM)
            s = [st[d, h] for d in dirs]
            for step in range(ncb):
                for d in dirs:
                    ci = ncb - 1 - step if d == 1 else step
                    s_refs[d][0, 0, ci, h] = s[d]
                    o_refs[d][0, ci * HGRN_CHUNK:(ci + 1) * HGRN_CHUNK, ls] = (
                        intra[d][ci][h] + _dot_nt(chunk(t[d]["qe"], ci, h), s[d]))
                    s[d] = s[d] * t[d]["dec"][ci][:, ls] + ds_loc[d][ci][h]
            for d in dirs:
                st[d, h] = s[d]

    def col(d, cb):
        return pl.BlockSpec((1, ROW_BLOCK, KW), lambda b, j: (b, orders[d](j), cb))

    def outs(d):
        return [pl.BlockSpec((1, ROW_BLOCK, KW), lambda b, j: (b, orders[d](j), 0)),
                pl.BlockSpec((1, 1, ncb, N_HEADS, HEAD_DIM, HEAD_DIM), lambda b, j: (b, orders[d](j), 0, 0, 0, 0))]

    shapes = [SDS((bs, rows, KW), F32), SDS((bs, nb, ncb, N_HEADS, HEAD_DIM, HEAD_DIM), F32)]
    return _host_call(
        body, name="hgrn_fwd", grid=(bs, nb),
        in_specs=[col(0, 0), col(0, 2), col(0, 3), col(1, 1), col(1, 2), col(1, 3),
                  pl.BlockSpec((2, 2, KW), lambda b, j: (0, 0, 0)), TRI_SPEC, TRI_SPEC],
        out_specs=outs(0) + outs(1), out_shape=shapes * 2,
        scratch_shapes=[pltpu.VMEM((2, N_HEADS, HEAD_DIM, HEAD_DIM), F32)],
        args=(p, p, p, p, p, p, lbg, _block_tri(False), _block_tri(True)), carry=carry, then=then)


def _hgrn_bwd_pair(p, lbg, s_saved, do_raw, carry=None):
    bs, rows, _ = p.shape
    nb = rows // ROW_BLOCK
    ncb = ROW_BLOCK // HGRN_CHUNK
    dirs = (0, 1)
    fwd_orders = [_hgrn_block_order(d, nb) for d in dirs]
    orders = [lambda j, d=d: fwd_orders[d](nb - 1 - j) for d in dirs]
    pairs = [(ci, h) for ci in range(ncb) for h in range(N_HEADS)]

    def body(f0_ref, i0_ref, q0_ref, s0_ref, do0_ref, f1_ref, i1_ref, q1_ref, s1_ref, do1_ref,
             lbg_ref, tril_ref, triu_ref,
             df0_ref, dq0_ref, dv0_ref, dlb0_ref, df1_ref, dq1_ref, dv1_ref, dlb1_ref, dst, acc):
        b, j = pl.program_id(0), pl.program_id(1)
        f_refs, i_refs, q_refs = (f0_ref, f1_ref), (i0_ref, i1_ref), (q0_ref, q1_ref)
        s_refs, do_refs = (s0_ref, s1_ref), (do0_ref, do1_ref)
        df_refs, dq_refs, dv_refs, dlb_refs = (df0_ref, df1_ref), (dq0_ref, dq1_ref), (dv0_ref, dv1_ref), (dlb0_ref, dlb1_ref)
        tri_refs, trit_refs = (tril_ref, triu_ref), (triu_ref, tril_ref)

        @pl.when((b == 0) & (j == 0))
        def _():
            dlb0_ref[...] = jnp.zeros_like(dlb0_ref)
            dlb1_ref[...] = jnp.zeros_like(dlb1_ref)

        @pl.when(j == 0)
        def _():
            dst[...] = jnp.zeros_like(dst)

        chunk = lambda a, ci, h: a[ci * HGRN_CHUNK:(ci + 1) * HGRN_CHUNK, h * HEAD_DIM:(h + 1) * HEAD_DIM]
        rows_of = lambda ci: slice(ci * HGRN_CHUNK, (ci + 1) * HGRN_CHUNK)
        lanes_of = lambda h: slice(h * HEAD_DIM, (h + 1) * HEAD_DIM)
        grid3 = lambda: [[[None] * N_HEADS for _ in range(ncb)] for _ in dirs]
        lbs = [_lower_bound(lbg_ref, d) for d in dirs]
        masks = [_tri(HGRN_CHUNK, d == 1) for d in dirs]
        masks_t = [_tri(HGRN_CHUNK, d != 1) for d in dirs]
        t = [_block_gates(f_refs[d][0], q_refs[d][0], lbs[d], tri_refs[d][...], d == 1) for d in dirs]
        v = [i_refs[d][0] for d in dirs]
        do = [do_refs[d][0] for d in dirs]
        a_t, da, da_t, dv_in, ds_loc = (grid3() for _ in range(5))
        for ci, h in pairs:
            for d in dirs:
                a_t[d][ci][h] = _dot_nt(chunk(t[d]["ki"], ci, h), chunk(t[d]["qi"], ci, h))
        for ci, h in pairs:
            for d in dirs:
                da[d][ci][h] = _dot_nt(chunk(do[d], ci, h), chunk(v[d], ci, h))
        for ci, h in pairs:
            for d in dirs:
                da_t[d][ci][h] = _dot_nt(chunk(v[d], ci, h), chunk(do[d], ci, h))
        for ci, h in pairs:
            for d in dirs:
                acc[d, 3, rows_of(ci), lanes_of(h)] = _dot(chunk(do[d], ci, h), s_refs[d][0, 0, ci, h])
        for ci, h in pairs:
            for d in dirs:
                ds_loc[d][ci][h] = _dot_tn(chunk(do[d], ci, h), chunk(t[d]["qe"], ci, h))
        for ci, h in pairs:
            for d in dirs:
                acc[d, 0, rows_of(ci), lanes_of(h)] = _dot(jnp.where(masks[d], da[d][ci][h], 0.0),
                                                           chunk(t[d]["ki"], ci, h))
        for ci, h in pairs:
            for d in dirs:
                acc[d, 1, rows_of(ci), lanes_of(h)] = _dot(jnp.where(masks_t[d], da_t[d][ci][h], 0.0),
                                                           chunk(t[d]["qi"], ci, h))
        for ci, h in pairs:
            for d in dirs:
                dv_in[d][ci][h] = _dot(jnp.where(masks_t[d], a_t[d][ci][h], 0.0), chunk(do[d], ci, h))
        ddec = grid3()
        for h in range(N_HEADS):
            ls = lanes_of(h)
            ds = [dst[d, h] for d in dirs]
            for step in range(ncb):
                for d in dirs:
                    ci = step if d == 1 else ncb - 1 - step
                    acc[d, 2, rows_of(ci), ls] = _dot(chunk(v[d], ci, h), ds[d])
                    acc[d, 4, rows_of(ci), ls] = dv_in[d][ci][h] + _dot_nt(chunk(t[d]["kd"], ci, h), ds[d])
                    ddec[d][ci][h] = _colsum(ds[d] * s_refs[d][0, 0, ci, h])
                    ds[d] = ds[d] * t[d]["dec"][ci][:, ls] + ds_loc[d][ci][h]
            for d in dirs:
                dst[d, h] = ds[d]
        for d in dirs:
            td = t[d]
            dqi, dki, dkd, dqe = (acc[d, i] for i in range(4))
            dq_refs[d][0] = td["e1"] * (dqi + dqe * td["eh"])
            dv_refs[d][0] = acc[d, 4]
            dk = td["e2"] * (dki + dkd * td["eh"])
            dkd_kd = dkd * td["kd"]
            db = dqi * td["qi"] - dki * td["ki"] - dkd_kd + dqe * td["qe"]
            dbl = [_colsum(dkd_kd[rows_of(ci)]) + jnp.concatenate(ddec[d][ci], axis=1) * td["dec"][ci]
                   for ci in range(ncb)]
            dg = _tri_matmul_f32(trit_refs[d][...], db) + _chunk_rows(dbl)
            df = dg / td["f"] - dk
            sg = td["sg"]
            dlb_refs[d][...] += _colsum(df * (1.0 - sg))
            df_refs[d][0] = (df * (1.0 - lbs[d]) * sg * (1.0 - sg)).astype(BF16)

    def ins(d):
        col = lambda cb: pl.BlockSpec((1, ROW_BLOCK, KW), lambda b, j: (b, orders[d](j), cb))
        return [col(d), col(2), col(3),
                pl.BlockSpec((1, 1, ncb, N_HEADS, HEAD_DIM, HEAD_DIM), lambda b, j: (b, orders[d](j), 0, 0, 0, 0)),
                pl.BlockSpec((1, ROW_BLOCK, KW), lambda b, j: (b, orders[d](j), 0))]

    def outs(d):
        row = pl.BlockSpec((1, ROW_BLOCK, KW), lambda b, j: (b, orders[d](j), 0))
        return [row, row, row, pl.BlockSpec((1, KW), lambda b, j: (0, 0))]

    shapes = [SDS((bs, rows, KW), BF16), SDS((bs, rows, KW), F32), SDS((bs, rows, KW), F32), SDS((1, KW), F32)]
    return _host_call(
        body, name="hgrn_bwd", grid=(bs, nb),
        in_specs=ins(0) + ins(1) + [pl.BlockSpec((2, 2, KW), lambda b, j: (0, 0, 0)), TRI_SPEC, TRI_SPEC],
        out_specs=outs(0) + outs(1), out_shape=shapes * 2,
        scratch_shapes=[pltpu.VMEM((2, N_HEADS, HEAD_DIM, HEAD_DIM), F32), pltpu.VMEM((2, 5, ROW_BLOCK, KW), F32)],
        args=(p, p, p, s_saved[0], do_raw, p, p, p, s_saved[1], do_raw, lbg, _block_tri(False), _block_tri(True)),
        carry=carry)


def _mix_values(og, u, v, ga, gb, o_raw, gna, lng, lnb, ws_ref, bst, wpa, wpb, wo):
    t = {}
    sog = _sig(og)
    t["sog"], t["silu_og"] = sog, og * sog
    xh_l, r_l = [], []
    for h in range(N_HEADS):
        xh, r = _rms(o_raw[:, h * HEAD_DIM:(h + 1) * HEAD_DIM])
        xh_l.append(xh)
        r_l.append(r)
    t["xh"], t["r"] = jnp.concatenate(xh_l, axis=1), r_l
    gna4 = jnp.concatenate([gna] * N_HEADS, axis=1)
    t["gna4"] = gna4
    t["o_n"] = t["xh"] * gna4
    t["o_a"] = t["o_n"] * t["silu_og"]
    t["ya"] = _dot(t["o_a"], wpa)
    t["gu"], t["tu"] = _gelu(u)
    gv, t["tv"] = _gelu(v)
    mu = jnp.mean(gv, axis=-1, keepdims=True)
    cen = gv - mu
    t["rstd"] = lax.rsqrt(jnp.mean(cen * cen, axis=-1, keepdims=True) + EPS)
    t["xhat"] = cen * t["rstd"]
    vn = t["xhat"] * lng + lnb
    t["vn"] = vn
    chunks = []
    for n in range(ROW_BLOCK // SGU_CHUNK):
        rs = slice(n * SGU_CHUNK, (n + 1) * SGU_CHUNK)
        groups = []
        for g in range(N_HEADS):
            ls = slice(g * HEAD_DIM, (g + 1) * HEAD_DIM)
            groups.append(_dot(ws_ref[g], vn[rs, ls]) + bst[:, g:g + 1])
        chunks.append(jnp.concatenate(groups, axis=1))
    t["mixed"] = jnp.concatenate(chunks, axis=0)
    t["o_bm"] = t["gu"] * t["mixed"]
    t["yb"] = _dot(t["o_bm"], wpb)
    t["sa"], t["sb"] = _sig(ga), _sig(gb)
    t["merged"] = t["sa"] * t["ya"] + t["sb"] * t["yb"]
    t["mix"] = _dot(t["merged"], wo)
    return t


def _mix_in_specs(row_of):
    def col(cb):
        return pl.BlockSpec((1, ROW_BLOCK, KW), lambda b, j: (b, row_of(j), cb))
    return [col(cb) for cb in range(4, 11)]


def _mix_param_specs():
    full2 = lambda r, c: pl.BlockSpec((r, c), lambda b, j: (0, 0))
    return [full2(1, HEAD_DIM), full2(1, KW), full2(1, KW),
            pl.BlockSpec((N_HEADS, SGU_CHUNK, SGU_CHUNK), lambda b, j: (0, 0, 0)),
            full2(SGU_CHUNK, N_HEADS), full2(KW, D_MODEL), full2(KW, D_MODEL), full2(D_MODEL, D_MODEL)]


def _mix_fwd(p, o_f, o_b, x, mod_c, gna, lng, lnb, w_s, bst, wpa, wpb, wo):
    bs, seq, _ = x.shape
    nbl = seq // ROW_BLOCK

    def body(og_r, u_r, v_r, ga0_r, ga1_r, gb0_r, gb1_r, of_r, ob_r, x_r, mod_r,
             gna_r, lng_r, lnb_r, ws_r, bst_r, wpa_r, wpb_r, wo_r, x1_r):
        ga = jnp.concatenate([ga0_r[0], ga1_r[0]], axis=1)
        gb = jnp.concatenate([gb0_r[0], gb1_r[0]], axis=1)
        t = _mix_values(og_r[0], u_r[0], v_r[0], ga, gb, of_r[0] + ob_r[0], gna_r[...], lng_r[...], lnb_r[...],
                        ws_r, bst_r[...], wpa_r[...], wpb_r[...], wo_r[...])
        x1_r[0] = x_r[0] + mod_r[0, 0:1, :] * t["mix"]

    row = lambda w: pl.BlockSpec((1, ROW_BLOCK, w), lambda b, j: (b, j + 1, 0))
    lat = pl.BlockSpec((1, ROW_BLOCK, D_MODEL), lambda b, j: (b, j, 0))
    return pl.pallas_call(
        body, name="mix_fwd", grid=(bs, nbl),
        in_specs=_mix_in_specs(lambda j: j + 1) + [row(KW), row(KW), lat,
                                                    pl.BlockSpec((1, 8, D_MODEL), lambda b, j: (b, 0, 0))]
        + _mix_param_specs(),
        out_specs=lat, out_shape=SDS(x.shape, F32), compiler_params=_params(2),
    )(p, p, p, p, p, p, p, o_f, o_b, x, mod_c, gna, lng, lnb, w_s, bst, wpa, wpb, wo)


def _mix_bwd(p, o_f, o_b, dx1, mod_c, gna, lng, lnb, w_s, w_s_t, bst, wpa, wpb, wo, carry=None):
    bs, rows, _ = p.shape
    nb = rows // ROW_BLOCK

    def body(og_r, u_r, v_r, ga0_r, ga1_r, gb0_r, gb1_r, of_r, ob_r, dx1_r, mod_r,
             gna_r, lng_r, lnb_r, ws_r, bst_r, wpa_r, wpb_r, wo_r, wst_r,
             dor_r, dpc_r, dwpa_r, dwpb_r, dwo_r, dgna_r, dlng_r, dlnb_r, dws_r, dbst_r, dmod_r):
        b, j = pl.program_id(0), pl.program_id(1)

        @pl.when((b == 0) & (j == 0))
        def _():
            for r in (dwpa_r, dwpb_r, dwo_r, dgna_r, dlng_r, dlnb_r, dws_r, dbst_r):
                r[...] = jnp.zeros_like(r)

        @pl.when(j == 0)
        def _():
            dmod_r[...] = jnp.zeros_like(dmod_r)
            dor_r[...] = jnp.zeros_like(dor_r)
            dpc_r[...] = jnp.zeros_like(dpc_r)

        @pl.when(j > 0)
        def _():
            og, u, v = og_r[0], u_r[0], v_r[0]
            ga = jnp.concatenate([ga0_r[0], ga1_r[0]], axis=1)
            gb = jnp.concatenate([gb0_r[0], gb1_r[0]], axis=1)
            gna, lng = gna_r[...], lng_r[...]
            wpa, wpb, wo = wpa_r[...], wpb_r[...], wo_r[...]
            dx1 = dx1_r[0]
            dmix = mod_r[0, 0:1, :] * dx1
            dmerged = _dot_nt(dmix, wo)
            t = _mix_values(og, u, v, ga, gb, of_r[0] + ob_r[0], gna, lng, lnb_r[...],
                            ws_r, bst_r[...], wpa, wpb, wo)
            dmod_r[0, 0:1, :] += _colsum(dx1 * t["mix"])
            dwo_r[...] += _dot_tn(t["merged"], dmix)
            sa, sb = t["sa"], t["sb"]
            dya, dyb = sa * dmerged, sb * dmerged
            dga = dmerged * t["ya"] * sa * (1.0 - sa)
            dgb = dmerged * t["yb"] * sb * (1.0 - sb)
            do_a = _dot_nt(dya, wpa)
            dwpa_r[...] += _dot_tn(t["o_a"], dya)
            do_bm = _dot_nt(dyb, wpb)
            dwpb_r[...] += _dot_tn(t["o_bm"], dyb)
            sog = t["sog"]
            dog = do_a * t["o_n"] * (sog * (1.0 + og * (1.0 - sog)))
            do_n = do_a * t["silu_og"]
            dxh = do_n * t["gna4"]
            prod = do_n * t["xh"]
            dgna = jnp.zeros((1, HEAD_DIM), F32)
            dor_l = []
            for h in range(N_HEADS):
                ls = slice(h * HEAD_DIM, (h + 1) * HEAD_DIM)
                dgna = dgna + _colsum(prod[:, ls])
                dor_l.append(_rms_bwd(dxh[:, ls], t["xh"][:, ls], t["r"][h]))
            dgna_r[...] += dgna
            dor_r[0] = jnp.concatenate(dor_l, axis=1)
            du = do_bm * t["mixed"] * _dgelu(u, t["tu"])
            dmixed = do_bm * t["gu"]
            vn = t["vn"]
            dvn_chunks = []
            for n in range(ROW_BLOCK // SGU_CHUNK):
                rs = slice(n * SGU_CHUNK, (n + 1) * SGU_CHUNK)
                groups = []
                for g in range(N_HEADS):
                    ls = slice(g * HEAD_DIM, (g + 1) * HEAD_DIM)
                    dm = dmixed[rs, ls]
                    dws_r[g] += _dot_nt(dm, vn[rs, ls])
                    dbst_r[:, g:g + 1] += jnp.sum(dm, axis=1, keepdims=True)
                    groups.append(_dot(wst_r[g], dm))
                dvn_chunks.append(jnp.concatenate(groups, axis=1))
            dvn = jnp.concatenate(dvn_chunks, axis=0)
            xhat = t["xhat"]
            dlng_r[...] += _colsum(dvn * xhat)
            dlnb_r[...] += _colsum(dvn)
            dxhat = dvn * lng
            dgv = t["rstd"] * (dxhat - jnp.mean(dxhat, axis=-1, keepdims=True)
                               - xhat * jnp.mean(dxhat * xhat, axis=-1, keepdims=True))
            dv = dgv * _dgelu(v, t["tv"])
            dpc_r[0] = jnp.concatenate([dog, du, dv, dga, dgb], axis=1).astype(BF16)

    row = lambda w: pl.BlockSpec((1, ROW_BLOCK, w), lambda b, j: (b, j, 0))
    lat = pl.BlockSpec((1, ROW_BLOCK, D_MODEL), lambda b, j: (b, jnp.maximum(j - 1, 0), 0))
    full2 = lambda r, c: pl.BlockSpec((r, c), lambda b, j: (0, 0))
    ws_spec = pl.BlockSpec((N_HEADS, SGU_CHUNK, SGU_CHUNK), lambda b, j: (0, 0, 0))
    return _host_call(
        body, name="mix_bwd", grid=(bs, nb),
        in_specs=_mix_in_specs(lambda j: j) + [row(KW), row(KW), lat,
                                                pl.BlockSpec((1, 8, D_MODEL), lambda b, j: (b, 0, 0))]
        + _mix_param_specs() + [ws_spec],
        out_specs=[row(KW), row(7 * KW), full2(KW, D_MODEL), full2(KW, D_MODEL), full2(D_MODEL, D_MODEL),
                   full2(1, HEAD_DIM), full2(1, KW), full2(1, KW), ws_spec, full2(SGU_CHUNK, N_HEADS),
                   pl.BlockSpec((1, 8, D_MODEL), lambda b, j: (b, 0, 0))],
        out_shape=[SDS((bs, rows, KW), F32), SDS((bs, rows, 7 * KW), BF16), SDS((KW, D_MODEL), F32),
                   SDS((KW, D_MODEL), F32), SDS((D_MODEL, D_MODEL), F32), SDS((1, HEAD_DIM), F32),
                   SDS((1, KW), F32), SDS((1, KW), F32), SDS((N_HEADS, SGU_CHUNK, SGU_CHUNK), F32),
                   SDS((SGU_CHUNK, N_HEADS), F32), SDS((bs, 8, D_MODEL), F32)],
        args=(p, p, p, p, p, p, p, o_f, o_b, dx1, mod_c, gna, lng, lnb, w_s, bst, wpa, wpb, wo, w_s_t), carry=carry)


def _ffn(x1, target, mod_c, g_ffn, g_final, w_up, w_down):
    bs, seq, _ = x1.shape
    nbl = seq // ROW_BLOCK

    def body(x1_r, tg_r, mod_r, gf_r, gl_r, wu_r, wd_r,
             dx1_r, h2_r, dab_r, hid_r, dffn_r, loss_r, dgl_r, dgf_r, dmod_r):
        b, j = pl.program_id(0), pl.program_id(1)

        @pl.when((b == 0) & (j == 0))
        def _():
            for r in (loss_r, dgl_r, dgf_r):
                r[...] = jnp.zeros_like(r)

        @pl.when(j == 0)
        def _():
            dmod_r[...] = jnp.zeros_like(dmod_r)

        x1 = x1_r[0]
        shift, scale, gate = mod_r[0, 1:2, :], mod_r[0, 2:3, :], mod_r[0, 3:4, :]
        gf, gl = gf_r[...], gl_r[...]
        xn2, r2 = _rms(x1)
        hn2 = xn2 * gf
        h2 = (hn2 * (1.0 + scale) + shift).astype(BF16)
        h2_r[0] = h2
        ab = jnp.dot(h2, wu_r[...], preferred_element_type=F32)
        a, bb = ab[:, :D_FF], ab[:, D_FF:]
        sa = _sig(a)
        silu_a = a * sa
        hid = (silu_a * bb).astype(BF16)
        hid_r[0] = hid
        ffn = jnp.dot(hid, wd_r[...], preferred_element_type=F32)
        x2 = x1 + gate * ffn
        xn3, r3 = _rms(x2)
        err = xn3 * gl - tg_r[0]
        loss_r[...] += 0.5 * jnp.sum(jnp.mean(err * err, axis=-1, keepdims=True), axis=0, keepdims=True)
        dy = err * (1.0 / D_MODEL)
        dgl_r[...] += _colsum(dy * xn3)
        dx2 = _rms_bwd(dy * gl, xn3, r3)
        dmod_r[0, 3:4, :] += _colsum(dx2 * ffn)
        dffn = (gate * dx2).astype(BF16)
        dffn_r[0] = dffn
        dhid = lax.dot_general(dffn, wd_r[...], (((1,), (1,)), ((), ())), preferred_element_type=F32)
        da = dhid * bb * (sa * (1.0 + a * (1.0 - sa)))
        db = dhid * silu_a
        dab = jnp.concatenate([da, db], axis=1).astype(BF16)
        dab_r[0] = dab
        dh2 = lax.dot_general(dab, wu_r[...], (((1,), (1,)), ((), ())), preferred_element_type=F32)
        dmod_r[0, 1:2, :] += _colsum(dh2)
        dmod_r[0, 2:3, :] += _colsum(dh2 * hn2)
        dhn2 = dh2 * (1.0 + scale)
        dgf_r[...] += _colsum(dhn2 * xn2)
        dx1_r[0] = dx2 + _rms_bwd(dhn2 * gf, xn2, r2)

    lat = lambda w: pl.BlockSpec((1, ROW_BLOCK, w), lambda b, j: (b, j, 0))
    full2 = lambda r, c: pl.BlockSpec((r, c), lambda b, j: (0, 0))
    mod_spec = pl.BlockSpec((1, 8, D_MODEL), lambda b, j: (b, 0, 0))
    return pl.pallas_call(
        body, name="ffn", grid=(bs, nbl),
        in_specs=[lat(D_MODEL), lat(D_MODEL), mod_spec, full2(1, D_MODEL), full2(1, D_MODEL),
                  full2(D_MODEL, 2 * D_FF), full2(D_FF, D_MODEL)],
        out_specs=[lat(D_MODEL), lat(D_MODEL), lat(2 * D_FF), lat(D_FF), lat(D_MODEL),
                   full2(1, 1), full2(1, D_MODEL), full2(1, D_MODEL), mod_spec],
        out_shape=[SDS(x1.shape, F32), SDS(x1.shape, BF16), SDS((bs, seq, 2 * D_FF), BF16),
                   SDS((bs, seq, D_FF), BF16), SDS(x1.shape, BF16), SDS((1, 1), F32),
                   SDS((1, D_MODEL), F32), SDS((1, D_MODEL), F32), SDS((bs, 8, D_MODEL), F32)],
        compiler_params=_params(2),
    )(x1, target, mod_c, g_ffn, g_final, w_up, w_down)


def _row_tile(rows, most):
    return next(m * ROW_BLOCK for m in (9, 8, 4, 2, 1) if m <= most and rows % (m * ROW_BLOCK) == 0)


def _matmul_tn(a, b, n_blocks, tk, name, carry=None):
    t, m = a.shape
    n = b.shape[1]
    tn = n // n_blocks

    def body(a_ref, b_ref, o_ref):
        @pl.when(pl.program_id(1) == 0)
        def _():
            o_ref[...] = jnp.zeros_like(o_ref)
        o_ref[0] += _dot_tn(a_ref[...], b_ref[...])

    (out,), carried = _host_call(
        body, name=name, grid=(n_blocks, t // tk),
        in_specs=[pl.BlockSpec((tk, m), lambda i, k: (k, 0)), pl.BlockSpec((tk, tn), lambda i, k: (k, i))],
        out_specs=[pl.BlockSpec((1, m, tn), lambda i, k: (i, 0, 0))],
        out_shape=[SDS((n_blocks, m, tn), F32)], args=(a, b), carry=carry)
    return out if carry is None else (out, carried)


SMALL_ROWS = 80
ROW_CCTX = 3


def _small_reduce(gathered, lbg):
    def body(g_ref, lbg_ref, s_ref, dgam_ref):
        tot = g_ref[0:SMALL_ROWS, :]
        for dev in range(1, N_DEV):
            tot = tot + g_ref[dev * SMALL_ROWS:(dev + 1) * SMALL_ROWS, :]
        s_ref[...] = tot
        cc = g_ref[ROW_CCTX:ROW_CCTX + 1, :]
        for dev in range(2, N_DEV, 2):
            cc = cc + g_ref[dev * SMALL_ROWS + ROW_CCTX:dev * SMALL_ROWS + ROW_CCTX + 1, :]
        s_ref[ROW_CCTX:ROW_CCTX + 1, :] = cc
        dlb = tot[7:8, :]
        for d in range(2):
            s0 = _sig(lbg_ref[0, d:d + 1, :] - lbg_ref[1, d:d + 1, :])
            dgam_ref[d:d + 1, :] = dlb[:, d * KW:(d + 1) * KW] * s0 * (1.0 - s0)

    return pl.pallas_call(
        body, name="small_reduce", out_shape=[SDS((SMALL_ROWS, D_MODEL), F32), SDS((2, KW), F32)],
        in_specs=[VMEM_SPEC] * 2, out_specs=[VMEM_SPEC] * 2,
    )(gathered, lbg)


def _pad_cols(a, width):
    return jnp.pad(a, ((0, 0), (0, width - a.shape[1])))


def kernel(x, c, ctx, c_ctx, w_mod, b_mod, g_mix, g_ffn, w_in, lb_gamma, g_norm_a, ln_v_g, ln_v_b, w_s, b_s, w_pa, w_pb, w_o, w_up, w_down, g_final, loss_target, m_c_ctx, m_w_mod, m_b_mod, m_g_mix, m_g_ffn, m_w_in, m_lb_gamma, m_g_norm_a, m_ln_v_g, m_ln_v_b, m_w_s, m_b_s, m_w_pa, m_w_pb, m_w_o, m_w_up, m_w_down, m_g_final, v_c_ctx, v_w_mod, v_b_mod, v_g_mix, v_g_ffn, v_w_in, v_lb_gamma, v_g_norm_a, v_ln_v_g, v_ln_v_b, v_w_s, v_b_s, v_w_pa, v_w_pb, v_w_o, v_w_up, v_w_down, v_g_final):
    ax, ay, ac = lax.axis_index("x"), lax.axis_index("y"), lax.axis_index("c")
    kc = 2 * ax + ay
    dev = 2 * kc + ac
    pos = jnp.stack([kc, ac]).astype(jnp.int32)
    bs, seq, _ = x.shape
    assert bs <= 4 and ctx.shape[1] == ROW_BLOCK and seq % ROW_BLOCK == 0
    mod_cols = w_mod.shape[2]

    lbg_row = _pad_cols(lb_gamma.reshape(1, -1), D_MODEL)
    pay1 = jnp.concatenate([c, jnp.zeros((4 - bs, D_MODEL), F32), c_ctx[None, :], lbg_row,
                            jnp.zeros((2, D_MODEL), F32)], axis=0)
    b_mod_s = lax.dynamic_slice(b_mod, (0, kc * mod_cols), (1, mod_cols))
    cond64, mod_g = _cond_and_mod(pay1, w_mod[0], b_mod_s)
    lbg_full = cond64.reshape(N_DEV, 8, D_MODEL)[0::2, 5, :KW].reshape(N_CHIPS, 2, 2, HEAD_DIM)
    lbg_full = jnp.transpose(lbg_full, (1, 2, 0, 3)).reshape(2, 2, KW)
    mod_g = mod_g.reshape(N_DEV, 64, mod_cols)[0::2]
    shards = [w_in[0], w_up[0], w_pa[0], w_pb[0], w_o[0], w_down[0]]
    bufs = _cast_bf16(pos, shards)
    mod_full = jnp.transpose(mod_g, (1, 0, 2)).reshape(64, N_CHIPS * mod_cols)
    mod_mine = lax.dynamic_slice(mod_full, (dev * 8, 0), (8, 6 * D_MODEL)).reshape(8, 6, D_MODEL)
    mod, mc = mod_mine[:bs], mod_mine[4]
    zeros4 = jnp.zeros((bs, 4, D_MODEL), F32)
    mod_a = jnp.concatenate([mod[:, 0:2], jnp.broadcast_to(mc[None, 0:2], (bs, 2, D_MODEL)), zeros4], axis=1)
    mod_c = jnp.concatenate([mod[:, 2:6], zeros4], axis=1)

    def cols_major(a):
        return jnp.transpose(a, (1, 0, 2)).reshape(a.shape[1], -1)

    gna, lng, lnb = g_norm_a, ln_v_g, ln_v_b
    ws3 = w_s[0]
    ws3_t = jnp.transpose(ws3, (0, 2, 1))
    bst = jnp.transpose(b_s[0])

    (p, h_all), (w_in_g,) = _in_fwd_own(pos, x, ctx, g_mix, mod_a, w_in[0], carry=_carry_gather_send(bufs[:1]),
                                        then=_carry_gather_forward(bufs[:1]))
    p, sent_up = _in_fwd_rest(pos, h_all, w_in_g, p, carry=_carry_gather_send(bufs[1:2]))
    w_in_f = cols_major(w_in_g)
    fwd_rest = _carry_gather_forward(bufs[2:])
    then_rest = _Carry([], [], {}, fwd_rest.sems, lambda i, o, s: fwd_rest.copies(i[1:], o[1:], s))
    (o_f, s_f, o_b, s_b), gathered = _hgrn_fwd(
        p, lbg_full, carry=_merge_carries(_carry_gather_forward(sent_up), _carry_gather_send(bufs[2:])),
        then=then_rest)
    w_up_f, w_pa_f, w_pb_f = (cols_major(a) for a in gathered[:3])
    w_o_f = gathered[3].reshape(-1, D_MODEL)
    w_down_f = gathered[4].reshape(-1, D_MODEL)
    x1 = _mix_fwd(p, o_f, o_b, x, mod_c, gna, lng, lnb, ws3, bst, w_pa_f, w_pb_f, w_o_f)
    dx1, h2, dab, hid, dffn, loss_part, dg_final, dg_ffn, dmod_ffn = _ffn(
        x1, loss_target, mod_c, g_ffn, g_final[None, :], w_up_f, w_down_f)
    rows_lat = bs * seq
    dw_up = _matmul_tn(h2.reshape(rows_lat, D_MODEL), dab.reshape(rows_lat, 2 * D_FF), N_CHIPS,
                       _row_tile(rows_lat, 9), "dw_up")
    dw_down = _matmul_tn(hid.reshape(rows_lat, D_FF), dffn.reshape(rows_lat, D_MODEL), 1,
                         _row_tile(rows_lat, 4), "dw_down")

    def shard_major(a):
        return jnp.transpose(a.reshape(a.shape[0], N_CHIPS, -1), (1, 0, 2))

    part_ffn = [dw_up, dw_down.reshape(N_CHIPS, -1, D_MODEL)]
    (do_raw, dpc, dw_pa, dw_pb, dw_o, dgna, dlng, dlnb, dws, dbst, dmod_mix), sib_ffn = _mix_bwd(
        p, o_f, o_b, dx1, mod_c, gna, lng, lnb, ws3, ws3_t, bst, w_pa_f, w_pb_f, w_o_f,
        carry=_carry_sibling_halves(part_ffn))
    cpbf_ffn = _rs_add_halves(pos, part_ffn, sib_ffn, "rs_add_ffn")
    part_mix = [shard_major(dw_pa), shard_major(dw_pb), dw_o.reshape(N_CHIPS, -1, D_MODEL)]
    (df_f, dq_f, dv_f, dlb0, df_b, dq_b, dv_b, dlb1), got = _hgrn_bwd_pair(
        p, lbg_full, (s_f, s_b), do_raw,
        carry=_merge_carries(_carry_to_owner(cpbf_ffn), _carry_sibling_halves(part_mix)))
    own_ffn, sib_mix = got[:2], got[2:]
    half_ffn = _rs_sum_owner(pos, part_ffn, sib_ffn, own_ffn, "rs_sum_ffn")
    cpbf_mix = _rs_add_halves(pos, part_mix, sib_mix, "rs_add_mix")
    (grad_x, dp, dg_mix, dmod_in), _ = _in_bwd(x, ctx, dx1, g_mix, mod_a, w_in_f, df_f, df_b, dv_f, dv_b, dq_f, dq_b,
                                               dpc)

    rows_all = dp.shape[0] * dp.shape[1]
    tk_all = _row_tile(rows_all, 9)
    dw_in, got = _matmul_tn(h_all.reshape(rows_all, D_MODEL), dp.reshape(rows_all, IN_COLS), N_CHIPS, tk_all, "dw_in",
                            carry=_merge_carries(_carry_join_halves(half_ffn), _carry_to_owner(cpbf_mix)))
    g_ffn_w, own_mix = got[:2], got[2:]
    half_mix = _rs_sum_owner(pos, part_mix, sib_mix, own_mix, "rs_sum_mix")

    dmod_mine = jnp.concatenate([dmod_in[:, 0], dmod_in[:, 1], dmod_mix[:, 0], dmod_ffn[:, 1], dmod_ffn[:, 2],
                                 dmod_ffn[:, 3]], axis=1)
    dmc = jnp.concatenate([jnp.sum(dmod_in[:, 2], axis=0), jnp.sum(dmod_in[:, 3], axis=0),
                           jnp.zeros((4 * D_MODEL,), F32)])[None, :]
    pay3 = jnp.concatenate([dmod_mine, jnp.zeros((4 - bs, 6 * D_MODEL), F32), dmc,
                            jnp.zeros((3, 6 * D_MODEL), F32)], axis=0)
    dmod64, got = _all_gather8(pay3, "gather_dmod", carry=_merge_carries(_carry_sibling_halves([dw_in]),
                                                                          _carry_join_halves(half_mix)))
    sib_in, g_mix_w = got[:1], got[1:]
    cpbf_in = _rs_add_halves(pos, [dw_in], sib_in, "rs_add_w_in")
    dmod64_my = lax.dynamic_slice(dmod64, (0, kc * mod_cols), (64, mod_cols))
    g_w_mod, g_b_mod, g_cctx_part = _mod_bwd(cond64, dmod64, dmod64_my, w_mod[0], c_ctx[None, :])

    def row(*parts):
        return _pad_cols(jnp.concatenate([q.reshape(1, -1) for q in parts], axis=1), D_MODEL)

    small_rows = [dg_mix, dg_ffn, dg_final, g_cctx_part, row(dgna), row(dlng, dlnb), row(jnp.transpose(dbst)),
                  row(dlb0, dlb1), row(loss_part), jnp.zeros((7, D_MODEL), F32), dws.reshape(64, D_MODEL)]
    pay4 = jnp.concatenate(small_rows, axis=0)
    tot, dgam0 = _small_reduce(_all_gather8(pay4, "gather_small"), lbg_full)

    own_sems, own_src, own_land, own_token = _owner_send_start(cpbf_in[0], after=tot)

    rest_names = ["w_up", "w_pa", "w_pb", "w_o", "w_down", "w_mod"]
    rest_w = shards[1:] + [w_mod[0]]
    rest_g = [g_ffn_w[0], g_mix_w[0] + own_token[0, 0], g_mix_w[1], g_mix_w[2], g_ffn_w[1], g_w_mod]
    rest_m = [m_w_up[0], m_w_pa[0], m_w_pb[0], m_w_o[0], m_w_down[0], m_w_mod[0]]
    rest_v = [v_w_up[0], v_w_pa[0], v_w_pb[0], v_w_o[0], v_w_down[0], v_w_mod[0]]
    (ds_r, m2s_r, v2s_r), _ = _adamw_group(rest_w, rest_g, rest_m, rest_v, "adamw_rest")
    res = {}
    for name, g, d, m2, v2 in zip(rest_names, rest_g, ds_r, m2s_r, v2s_r):
        res[name] = (g[None], d[None], m2[None], v2[None])
    own_in = [_owner_send_wait(own_sems, own_src, own_land, after=(ds_r[0],))]
    g_in_w = _comm_call("rs_join_w_in",
                        _carry_join_halves(_rs_sum_owner(pos, [dw_in], sib_in, own_in, "rs_sum_w_in")))
    d, m2, v2 = _adamw_big(shards[0], g_in_w[0], m_w_in[0], v_w_in[0], "adamw_w_in")
    res["w_in"] = (g_in_w[0][None], d[None], m2[None], v2[None])

    loss = tot[8, 0]
    dgam_full = jnp.stack([dgam0, -dgam0])
    g_lbg = lax.dynamic_slice(dgam_full, (0, 0, kc * HEAD_DIM), (2, 2, HEAD_DIM))

    small = [
        ("c_ctx", c_ctx[None, :], tot[3:4], m_c_ctx, v_c_ctx),
        ("b_mod", b_mod, g_b_mod, m_b_mod, v_b_mod),
        ("g_mix", g_mix, tot[0:1], m_g_mix, v_g_mix),
        ("g_ffn", g_ffn, tot[1:2], m_g_ffn, v_g_ffn),
        ("lb_gamma", lb_gamma.reshape(4, HEAD_DIM), g_lbg.reshape(4, HEAD_DIM), m_lb_gamma, v_lb_gamma),
        ("g_norm_a", g_norm_a, tot[4:5, :HEAD_DIM], m_g_norm_a, v_g_norm_a),
        ("ln_v_g", ln_v_g, tot[5:6, :KW], m_ln_v_g, v_ln_v_g),
        ("ln_v_b", ln_v_b, tot[5:6, KW:], m_ln_v_b, v_ln_v_b),
        ("w_s", w_s.reshape(N_HEADS * SGU_CHUNK, SGU_CHUNK), tot[16:80].reshape(N_HEADS * SGU_CHUNK, SGU_CHUNK),
         m_w_s, v_w_s),
        ("b_s", b_s[0], tot[6:7, :KW].reshape(N_HEADS, SGU_CHUNK), m_b_s, v_b_s),
        ("g_final", g_final[None, :], tot[2:3], m_g_final, v_g_final),
    ]
    ws_, gs_ = [s[1] for s in small], [s[2] for s in small]
    ms_ = [s[3].reshape(s[1].shape) for s in small]
    vs_ = [s[4].reshape(s[1].shape) for s in small]
    ds_, m2s_, v2s_ = _adamw_small(ws_, gs_, ms_, vs_)
    for (name, _, g, m, _), d, m2, v2 in zip(small, ds_, m2s_, v2s_):
        res[name] = tuple(t.reshape(m.shape) for t in (g, d, m2, v2))

    order = ["c_ctx", "w_mod", "b_mod", "g_mix", "g_ffn", "w_in", "lb_gamma", "g_norm_a", "ln_v_g", "ln_v_b",
             "w_s", "b_s", "w_pa", "w_pb", "w_o", "w_up", "w_down", "g_final"]
    outs = [loss, grad_x]
    for part in range(4):
        outs += [res[n][part] for n in order]
    return tuple(outs)
```

```python
import functools
import math

import jax
import jax.numpy as jnp
import numpy as np
from jax import lax
from jax.experimental import pallas as pl
from jax.experimental.pallas import tpu as pltpu

F32 = jnp.float32
BF16 = jnp.bfloat16
SDS = jax.ShapeDtypeStruct
MESH = pl.DeviceIdType.MESH

EPS = 1e-6
D_MODEL = 1024
N_HEADS = 4
HEAD_DIM = 128
KW = N_HEADS * HEAD_DIM
IN_COLS = 11 * KW
D_FF = 2816
HGRN_CHUNK = 64
SGU_CHUNK = 128
ROW_BLOCK = 256
N_CHIPS = 4
N_DEV = 8
V7X_VMEM_BYTES = 64 * 1024 * 1024
VMEM_LIMIT = V7X_VMEM_BYTES - 6 * 1024 * 1024

ADAM_LR, ADAM_B1, ADAM_B2, ADAM_EPS, ADAM_WD, ADAM_STEP = 0.001, 0.9, 0.999, 1e-08, 0.01, 10
GELU_C0 = math.sqrt(2.0 / math.pi)
GELU_C1 = 0.044715

VMEM_SPEC = pl.BlockSpec(memory_space=pltpu.VMEM)
ANY_SPEC = pl.BlockSpec(memory_space=pl.ANY)


def _params(n_grid):
    return pltpu.CompilerParams(dimension_semantics=("arbitrary",) * n_grid, vmem_limit_bytes=VMEM_LIMIT)


def _sig(x):
    return 0.5 * jnp.tanh(0.5 * x) + 0.5


def _gelu(x):
    t = jnp.tanh(GELU_C0 * (x + GELU_C1 * x * x * x))
    return 0.5 * x * (1.0 + t), t


def _dgelu(x, t):
    return 0.5 * (1.0 + t) + 0.5 * x * (1.0 - t * t) * GELU_C0 * (1.0 + 3.0 * GELU_C1 * x * x)


def _dot(a, b):
    return jnp.dot(a.astype(BF16), b.astype(BF16), preferred_element_type=F32)


def _dot_nt(a, b):
    return lax.dot_general(a.astype(BF16), b.astype(BF16), (((1,), (1,)), ((), ())), preferred_element_type=F32)


def _dot_tn(a, b):
    return lax.dot_general(a.astype(BF16), b.astype(BF16), (((0,), (0,)), ((), ())), preferred_element_type=F32)


def _dot_f32(a, b, dims=(((1,), (0,)), ((), ()))):
    return lax.dot_general(a, b, dims, precision=lax.Precision.HIGHEST, preferred_element_type=F32)


def _rms(x):
    r = lax.rsqrt(jnp.mean(x * x, axis=-1, keepdims=True) + EPS)
    return x * r, r


def _rms_bwd(dxn, xn, r):
    return r * (dxn - xn * jnp.mean(dxn * xn, axis=-1, keepdims=True))


def _colsum(a):
    return jnp.sum(a, axis=0, keepdims=True)


def _tri(n, upper):
    t = lax.broadcasted_iota(jnp.int32, (n, n), 0)
    s = lax.broadcasted_iota(jnp.int32, (n, n), 1)
    return (s >= t) if upper else (s <= t)


def _all_gather8(x_shard, name, carry=None, then=None):
    m_per, n = x_shard.shape
    n_ci = len(carry.ins) if carry else 0
    n_co = len(carry.outs) if carry else 0
    n_cs = len(carry.sems) if carry else 0

    def body(*refs):
        x_ref, cins = refs[0], refs[1:1 + n_ci]
        out_ref, couts = refs[1 + n_ci], refs[2 + n_ci:2 + n_ci + n_co]
        send_sems, recv_sems, local_sem = refs[2 + n_ci + n_co:5 + n_ci + n_co]
        csems = refs[5 + n_ci + n_co:5 + n_ci + n_co + n_cs]
        tsems = refs[5 + n_ci + n_co + n_cs:]
        if carry:
            _start_all(carry.copies(cins, couts, csems)[0])
        _gather8_body(x_ref, out_ref, send_sems, recv_sems, local_sem, m_per)
        if carry:
            _wait_all(carry.copies(cins, couts, csems)[1])
        if then:
            _start_all(then.copies(couts, couts, tsems)[0])
            _wait_all(then.copies(couts, couts, tsems)[1])

    res = pl.pallas_call(
        body, name=name, out_shape=[SDS((N_DEV * m_per, n), x_shard.dtype)] + (carry.outs if carry else []),
        in_specs=[VMEM_SPEC] + [ANY_SPEC] * n_ci, out_specs=[VMEM_SPEC] + [ANY_SPEC] * n_co,
        input_output_aliases={1 + i: 1 + o for i, o in carry.alias.items()} if carry else {},
        scratch_shapes=[pltpu.SemaphoreType.DMA((7,)), pltpu.SemaphoreType.DMA((7,)), pltpu.SemaphoreType.DMA]
        + (carry.sems if carry else []) + (then.sems if then else []),
    )(x_shard, *(carry.ins if carry else []))
    return res[0] if carry is None else (res[0], list(res[1:]))


def _gather8_body(x_ref, out_ref, send_sems, recv_sems, local_sem, m_per):
    x, y, c = lax.axis_index("x"), lax.axis_index("y"), lax.axis_index("c")
    me, sibling = (x, y, c), (x, y, 1 - c)
    chips = [(1 - x, y), (x, 1 - y), (1 - x, 1 - y)]

    def rows(px, py, pc):
        return out_ref.at[pl.ds((4 * px + 2 * py + pc) * m_per, m_per), :]

    def copy(k, block, to, src=None):
        return pltpu.make_async_remote_copy(
            src_ref=rows(*block) if src is None else src, dst_ref=rows(*block),
            send_sem=send_sems.at[k], recv_sem=recv_sems.at[k], device_id=to, device_id_type=MESH)

    mine = pltpu.make_async_copy(x_ref, rows(*me), local_sem)
    mine.start()
    first = [copy(0, me, sibling, src=x_ref)]
    first += [copy(1 + j, me, (*chip, c), src=x_ref) for j, chip in enumerate(chips)]
    for cp in first:
        cp.start()
    passed = [copy(4 + j, (*chip, c), sibling) for j, chip in enumerate(chips)]
    for j, chip in enumerate(chips):
        copy(1 + j, (*chip, c), me).wait_recv()
        passed[j].start()
    copy(0, sibling, me).wait_recv()
    for j, chip in enumerate(chips):
        copy(4 + j, (*chip, 1 - c), me).wait_recv()
    for cp in first + passed:
        cp.wait_send()
    mine.wait()


def _mesh_pos():
    x, y, c = lax.axis_index("x"), lax.axis_index("y"), lax.axis_index("c")
    chips = [(1 - x, y), (x, 1 - y), (1 - x, 1 - y)]
    return x, y, c, 2 * x + y, (x, y, 1 - c), chips


def _half_rows(c, rh):
    return pl.ds(pl.multiple_of(c * rh, 16), rh)


class _Carry:
    def __init__(self, ins, outs, alias, sems, copies):
        self.ins, self.outs, self.alias, self.sems, self.copies = list(ins), list(outs), dict(alias), list(sems), copies


def _remote(src, dst, send, recv, to):
    return functools.partial(pltpu.make_async_remote_copy, src_ref=src, dst_ref=dst, send_sem=send, recv_sem=recv,
                             device_id=to, device_id_type=MESH)


def _carry_gather_send(bufs):
    n = len(bufs)

    def copies(ins, outs, sems):
        x, y, c, kc, sibling, chips = _mesh_pos()
        starts, waits = [], []
        for wi in range(n):
            rh = outs[wi].shape[1] // 2
            for jj, chip in enumerate(chips):
                mine = outs[wi].at[kc, _half_rows(c, rh), :]
                cp = _remote(mine, mine, sems[0].at[wi, jj], sems[1].at[wi, jj], (*chip, c))
                starts.append(cp)
                waits.append((cp, "send"))
                theirs = outs[wi].at[2 * chip[0] + chip[1], _half_rows(c, rh), :]
                waits.append((_remote(theirs, theirs, sems[0].at[wi, jj], sems[1].at[wi, jj], (*chip, c)), "recv"))
        return starts, waits

    return _Carry(bufs, [SDS(b.shape, b.dtype) for b in bufs], {i: i for i in range(n)},
                  [pltpu.SemaphoreType.DMA((n, 3)), pltpu.SemaphoreType.DMA((n, 3))], copies)


def _carry_gather_forward(bufs):
    n = len(bufs)

    def copies(ins, outs, sems):
        x, y, c, kc, sibling, chips = _mesh_pos()
        starts, waits = [], []
        for wi in range(n):
            rh = outs[wi].shape[1] // 2
            for jj, chip in enumerate(chips):
                got = outs[wi].at[2 * chip[0] + chip[1], _half_rows(c, rh), :]
                cp = _remote(got, got, sems[0].at[wi, jj], sems[1].at[wi, jj], sibling)
                starts.append(cp)
                waits.append((cp, "send"))
                other = outs[wi].at[2 * chip[0] + chip[1], _half_rows(1 - c, rh), :]
                waits.append((_remote(other, other, sems[0].at[wi, jj], sems[1].at[wi, jj], sibling), "recv"))
        return starts, waits

    return _Carry(bufs, [SDS(b.shape, b.dtype) for b in bufs], {i: i for i in range(n)},
                  [pltpu.SemaphoreType.DMA((n, 3)), pltpu.SemaphoreType.DMA((n, 3))], copies)


def _carry_sibling_halves(grads):
    n = len(grads)

    def copies(ins, outs, sems):
        x, y, c, kc, sibling, chips = _mesh_pos()
        cps = [_remote(ins[wi].at[:, _half_rows(1 - c, ins[wi].shape[1] // 2), :], outs[wi],
                       sems[0].at[wi], sems[1].at[wi], sibling) for wi in range(n)]
        return cps, [(cp, "both") for cp in cps]

    return _Carry(grads, [SDS((N_CHIPS, g.shape[1] // 2, g.shape[2]), F32) for g in grads], {},
                  [pltpu.SemaphoreType.DMA((n,)), pltpu.SemaphoreType.DMA((n,))], copies)


def _carry_to_owner(cpbfs):
    n = len(cpbfs)

    def copies(ins, outs, sems):
        x, y, c, kc, sibling, chips = _mesh_pos()
        starts, waits = [], []
        for wi in range(n):
            for jj, chip in enumerate(chips):
                cp = _remote(ins[wi].at[2 * chip[0] + chip[1]], outs[wi].at[kc],
                             sems[0].at[wi, jj], sems[1].at[wi, jj], (*chip, c))
                starts.append(cp)
                waits.append((cp, "send"))
                slot = outs[wi].at[2 * chip[0] + chip[1]]
                waits.append((_remote(slot, slot, sems[0].at[wi, jj], sems[1].at[wi, jj], (*chip, c)), "recv"))
        return starts, waits

    return _Carry(cpbfs, [SDS(g.shape, BF16) for g in cpbfs], {},
                  [pltpu.SemaphoreType.DMA((n, 3)), pltpu.SemaphoreType.DMA((n, 3))], copies)


def _carry_join_halves(bufs):
    n = len(bufs)

    def copies(ins, outs, sems):
        x, y, c, kc, sibling, chips = _mesh_pos()
        cps = []
        for wi in range(n):
            mine = outs[wi].at[_half_rows(c, outs[wi].shape[0] // 2), :]
            cps.append(_remote(mine, mine, sems[0].at[wi], sems[1].at[wi], sibling))
        return cps, [(cp, "both") for cp in cps]

    return _Carry(bufs, [SDS(b.shape, F32) for b in bufs], {i: i for i in range(n)},
                  [pltpu.SemaphoreType.DMA((n,)), pltpu.SemaphoreType.DMA((n,))], copies)


def _merge_carries(*carries):
    ins, outs, alias, sems, parts = [], [], {}, [], []
    for cy in carries:
        parts.append((len(ins), len(cy.ins), len(outs), len(cy.outs), len(sems), len(cy.sems), cy.copies))
        alias.update({len(ins) + i: len(outs) + o for i, o in cy.alias.items()})
        ins += cy.ins
        outs += cy.outs
        sems += cy.sems

    def copies(i, o, s):
        starts, waits = [], []
        for i0, ni, o0, no, s0, ns, fn in parts:
            st, wt = fn(i[i0:i0 + ni], o[o0:o0 + no], s[s0:s0 + ns])
            starts += st
            waits += wt
        return starts, waits

    return _Carry(ins, outs, alias, sems, copies)


def _start_all(starts):
    for cp in starts:
        cp().start()


def _wait_all(waits):
    for cp, which in waits:
        if which == "send":
            cp().wait_send()
        elif which == "recv":
            cp().wait_recv()
        else:
            cp().wait()


HBM_SPEC = pl.BlockSpec(memory_space=pltpu.HBM)
SEM_SPEC = pl.BlockSpec(memory_space=pltpu.SEMAPHORE)
SPLIT_COPY_EFFECT = pltpu.SideEffectType.DATAFLOW_SIDE_EFFECTING


def _owner_send_start(cpbf, after):
    land = lax.empty(cpbf.shape, cpbf.dtype)

    def body(src_ref, land_ref, after_ref, s0, s1, s2, r0, r1, r2, src_thru, land_thru, token):
        x, y, c, kc, sibling, chips = _mesh_pos()
        for jj, (chip, s_sem, r_sem) in enumerate(zip(chips, (s0, s1, s2), (r0, r1, r2))):
            pltpu.make_async_remote_copy(
                src_ref=src_ref.at[2 * chip[0] + chip[1]], dst_ref=land_ref.at[kc], send_sem=s_sem, recv_sem=r_sem,
                device_id=(*chip, c), device_id_type=MESH).start()
        token[...] = jnp.zeros_like(token)

    buf = pltpu.HBM(cpbf.shape, cpbf.dtype)
    outs = pl.pallas_call(
        body, name="rs_owner_w_in_start",
        out_shape=(pltpu.SemaphoreType.DMA(()),) * 6 + (buf, buf, SDS((8, 128), F32)),
        in_specs=(HBM_SPEC, HBM_SPEC, ANY_SPEC), out_specs=(SEM_SPEC,) * 6 + (HBM_SPEC, HBM_SPEC, VMEM_SPEC),
        input_output_aliases={0: 6, 1: 7},
        compiler_params=pltpu.CompilerParams(has_side_effects=SPLIT_COPY_EFFECT),
    )(pltpu.with_memory_space_constraint(cpbf, pltpu.HBM), pltpu.with_memory_space_constraint(land, pltpu.HBM), after)
    return outs[:6], outs[6], outs[7], outs[8]


def _owner_send_wait(sems, src_thru, land_thru, after):
    n_after = len(after)

    def body(*refs):
        src_ref, land_ref = refs[0], refs[1]
        sends, recvs = refs[2:5], refs[5:8]
        x, y, c, kc, sibling, chips = _mesh_pos()
        for jj, chip in enumerate(chips):
            slot = 2 * chip[0] + chip[1]
            cp = pltpu.make_async_remote_copy(
                src_ref=src_ref.at[slot], dst_ref=land_ref.at[slot], send_sem=sends[jj], recv_sem=recvs[jj],
                device_id=(*chip, c), device_id_type=MESH)
            cp.wait_send()
            cp.wait_recv()

    buf = pltpu.HBM(land_thru.shape, land_thru.dtype)
    return pl.pallas_call(
        body, name="rs_owner_w_in_wait", out_shape=(buf, buf),
        in_specs=(HBM_SPEC, HBM_SPEC) + (SEM_SPEC,) * 6 + (ANY_SPEC,) * n_after, out_specs=(HBM_SPEC, HBM_SPEC),
        input_output_aliases={0: 0, 1: 1},
        compiler_params=pltpu.CompilerParams(has_side_effects=SPLIT_COPY_EFFECT),
    )(src_thru, land_thru, *sems, *after)[1]


def _comm_call(name, carry):
    n_i, n_o = len(carry.ins), len(carry.outs)

    def body(*refs):
        ins, outs, sems = refs[:n_i], refs[n_i:n_i + n_o], refs[n_i + n_o:]
        _start_all(carry.copies(ins, outs, sems)[0])
        _wait_all(carry.copies(ins, outs, sems)[1])

    return pl.pallas_call(
        body, name=name, out_shape=carry.outs, in_specs=[ANY_SPEC] * n_i, out_specs=[ANY_SPEC] * n_o,
        input_output_aliases=carry.alias, scratch_shapes=carry.sems,
    )(*carry.ins)


def _host_call(body, *, name, grid, in_specs, out_specs, out_shape, args, scratch_shapes=(), carry=None, then=None,
               prefetch=None, aliases=None):
    n_in, n_out, n_scr = len(in_specs), len(out_specs), len(scratch_shapes)
    n_ci = len(carry.ins) if carry else 0
    n_co = len(carry.outs) if carry else 0
    n_cs = len(carry.sems) if carry else 0
    n_pf = 0 if prefetch is None else 1

    def wrapped(*refs):
        pf, refs = refs[:n_pf], refs[n_pf:]
        ins, cins = refs[:n_in], refs[n_in:n_in + n_ci]
        o0 = n_in + n_ci
        outs, couts = refs[o0:o0 + n_out], refs[o0 + n_out:o0 + n_out + n_co]
        s0 = o0 + n_out + n_co
        scr, sems, tsems = refs[s0:s0 + n_scr], refs[s0 + n_scr:s0 + n_scr + n_cs], refs[s0 + n_scr + n_cs:]
        idx = [pl.program_id(a) for a in range(len(grid))]
        first = functools.reduce(jnp.logical_and, [i == 0 for i in idx])
        last = functools.reduce(jnp.logical_and, [i == g - 1 for i, g in zip(idx, grid)])

        if carry:
            @pl.when(first)
            def _():
                _start_all(carry.copies(cins, couts, sems)[0])

        body(*pf, *ins, *outs, *scr)

        if carry:
            @pl.when(last)
            def _():
                _wait_all(carry.copies(cins, couts, sems)[1])
                if then:
                    _start_all(then.copies(couts, couts, tsems)[0])
                    _wait_all(then.copies(couts, couts, tsems)[1])

    all_in = list(in_specs) + [ANY_SPEC] * n_ci
    all_out = list(out_specs) + [ANY_SPEC] * n_co
    all_scr = list(scratch_shapes) + (carry.sems if carry else []) + (then.sems if then else [])
    alias = {n_pf + i: o for i, o in (aliases or {}).items()}
    if carry:
        alias.update({n_pf + n_in + i: n_out + o for i, o in carry.alias.items()})
    kwargs = dict(name=name, out_shape=list(out_shape) + (carry.outs if carry else []), input_output_aliases=alias,
                  compiler_params=_params(len(grid)))
    if prefetch is None:
        call = pl.pallas_call(wrapped, grid=grid, in_specs=all_in, out_specs=all_out, scratch_shapes=all_scr, **kwargs)
        res = call(*args, *(carry.ins if carry else []))
    else:
        call = pl.pallas_call(wrapped, grid_spec=pltpu.PrefetchScalarGridSpec(
            num_scalar_prefetch=1, grid=grid, in_specs=all_in, out_specs=all_out, scratch_shapes=all_scr), **kwargs)
        res = call(prefetch, *args, *(carry.ins if carry else []))
    return list(res[:n_out]), list(res[n_out:])


def _rs_add_halves(pos, grads, recvs, name):
    n = len(grads)

    def body(pos_ref, *refs):
        for i in range(n):
            refs[2 * n + i][...] = (refs[i][...] + refs[n + i][...]).astype(BF16)

    blks = [(1, g.shape[1] // 4, g.shape[2]) for g in grads]
    mine = [pl.BlockSpec(b, lambda k, i, p: (k, p[1] * 2 + i, 0)) for b in blks]
    half = [pl.BlockSpec(b, lambda k, i, p: (k, i, 0)) for b in blks]
    return pl.pallas_call(
        body, name=name,
        grid_spec=pltpu.PrefetchScalarGridSpec(
            num_scalar_prefetch=1, grid=(N_CHIPS, 2), in_specs=mine + half, out_specs=half),
        out_shape=[SDS((N_CHIPS, g.shape[1] // 2, g.shape[2]), BF16) for g in grads],
        compiler_params=_params(2),
    )(pos, *grads, *recvs)


def _rs_sum_owner(pos, grads, recvs, recv3s, name):
    n = len(grads)

    def body(pos_ref, *refs):
        for i in range(n):
            g, s, r1, r2, r3 = (refs[j * n + i] for j in range(5))
            own = g[0] + s[0]
            refs[5 * n + i][...] = ((own + r1[0].astype(F32)) + r2[0].astype(F32)) + r3[0].astype(F32)

    blks = [(1, g.shape[1] // 4, g.shape[2]) for g in grads]
    mine = [pl.BlockSpec(b, lambda i, p: (p[0], p[1] * 2 + i, 0)) for b in blks]

    def slot(d):
        return [pl.BlockSpec(b, lambda i, p: ((p[0] + d) % N_CHIPS, i, 0)) for b in blks]

    return pl.pallas_call(
        body, name=name,
        grid_spec=pltpu.PrefetchScalarGridSpec(
            num_scalar_prefetch=1, grid=(2,), in_specs=mine + slot(0) + slot(1) + slot(2) + slot(3),
            out_specs=[pl.BlockSpec(b[1:], lambda i, p: (p[1] * 2 + i, 0)) for b in blks]),
        out_shape=[SDS(g.shape[1:], F32) for g in grads],
        compiler_params=_params(1),
    )(pos, *grads, *recvs, *recv3s, *recv3s, *recv3s)


def _cast_bf16(pos, arrs):
    n = len(arrs)

    def body(pos_ref, *refs):
        for i in range(n):
            refs[n + i][0] = refs[i][...].astype(BF16)

    return pl.pallas_call(
        body, name="cast_bf16",
        grid_spec=pltpu.PrefetchScalarGridSpec(
            num_scalar_prefetch=1, grid=(2,),
            in_specs=[pl.BlockSpec((a.shape[0] // 2, a.shape[1]), lambda i, p: (i, 0)) for a in arrs],
            out_specs=[pl.BlockSpec((1, a.shape[0] // 2, a.shape[1]), lambda i, p: (p[0], i, 0)) for a in arrs]),
        out_shape=[SDS((N_CHIPS,) + a.shape, BF16) for a in arrs],
        compiler_params=_params(1),
    )(pos, *arrs)


def _adamw_vals(w, g, m, v):
    m2 = ADAM_B1 * m + (1.0 - ADAM_B1) * g
    v2 = ADAM_B2 * v + (1.0 - ADAM_B2) * (g * g)
    m_hat = m2 / (1.0 - ADAM_B1 ** ADAM_STEP)
    v_hat = v2 / (1.0 - ADAM_B2 ** ADAM_STEP)
    delta = -ADAM_LR * (m_hat / (jnp.sqrt(v_hat) + ADAM_EPS) + ADAM_WD * w)
    return delta, m2, v2


def _adamw_big(w, g, m, v, name):
    rows, cols = w.shape
    rb = rows // 4

    def body(w_ref, g_ref, m_ref, v_ref, d_ref, m2_ref, v2_ref):
        d, m2, v2 = _adamw_vals(w_ref[...], g_ref[...], m_ref[...], v_ref[...])
        d_ref[...] = d
        m2_ref[...] = m2
        v2_ref[...] = v2

    spec = pl.BlockSpec((rb, cols), lambda i: (i, 0))
    return pl.pallas_call(
        body, name=name, grid=(4,), in_specs=[spec] * 4, out_specs=[spec] * 3,
        out_shape=[SDS(w.shape, F32)] * 3, compiler_params=_params(1),
    )(w, g, m, v)


ADAMW_GROUP_STEPS = 8


def _adamw_group(ws, gs, ms, vs, name, carry=None):
    n = len(ws)

    def body(*refs):
        for i in range(n):
            d, m2, v2 = _adamw_vals(refs[i][...], refs[n + i][...], refs[2 * n + i][...], refs[3 * n + i][...])
            refs[4 * n + i][...] = d
            refs[5 * n + i][...] = m2
            refs[6 * n + i][...] = v2

    specs = [pl.BlockSpec((w.shape[0] // ADAMW_GROUP_STEPS, w.shape[1]), lambda i: (i, 0)) for w in ws]
    shapes = [SDS(w.shape, F32) for w in ws]
    outs, carried = _host_call(
        body, name=name, grid=(ADAMW_GROUP_STEPS,), in_specs=specs * 4, out_specs=specs * 3, out_shape=shapes * 3,
        args=(*ws, *gs, *ms, *vs), carry=carry)
    return (outs[:n], outs[n:2 * n], outs[2 * n:]), carried


def _adamw_small(ws, gs, ms, vs):
    n = len(ws)

    def body(*refs):
        for i in range(n):
            d, m2, v2 = _adamw_vals(refs[i][...], refs[n + i][...], refs[2 * n + i][...], refs[3 * n + i][...])
            refs[4 * n + i][...] = d
            refs[5 * n + i][...] = m2
            refs[6 * n + i][...] = v2

    shapes = [SDS(w.shape, F32) for w in ws]
    outs = pl.pallas_call(
        body, name="adamw_small", out_shape=shapes * 3,
        in_specs=[VMEM_SPEC] * (4 * n), out_specs=[VMEM_SPEC] * (3 * n),
    )(*ws, *gs, *ms, *vs)
    return outs[:n], outs[n:2 * n], outs[2 * n:]


def _cond_and_mod(pay, w_mod_s, b_mod_s):
    m_per, n_cols = pay.shape[0], w_mod_s.shape[1]
    rows = N_DEV * m_per

    def body(pay_ref, w_ref, b_ref, cond_ref, modg_ref, mod_s, sems_a, semr_a, loc_a, sems_b, semr_b, loc_b):
        _gather8_body(pay_ref, cond_ref, sems_a, semr_a, loc_a, m_per)
        cc = cond_ref[...]
        mod_s[...] = _dot_f32(cc * _sig(cc), w_ref[...]) + b_ref[...]
        _gather8_body(mod_s, modg_ref, sems_b, semr_b, loc_b, rows)

    dma7 = pltpu.SemaphoreType.DMA((7,))
    return pl.pallas_call(
        body, name="cond_and_mod",
        out_shape=[SDS((rows, pay.shape[1]), F32), SDS((N_DEV * rows, n_cols), F32)],
        in_specs=[VMEM_SPEC] * 3, out_specs=[VMEM_SPEC] * 2,
        scratch_shapes=[pltpu.VMEM((rows, n_cols), F32), dma7, dma7, pltpu.SemaphoreType.DMA,
                        dma7, dma7, pltpu.SemaphoreType.DMA],
        compiler_params=pltpu.CompilerParams(vmem_limit_bytes=VMEM_LIMIT),
    )(pay, w_mod_s, b_mod_s)


def _mod_bwd(cond64, dmod64, dmod64_my, w_mod_s, c_ctx):
    def body(c_ref, g_ref, gm_ref, w_ref, cc_ref, gw_ref, gb_ref, gcc_ref):
        cc = c_ref[...]
        act = cc * _sig(cc)
        gm = gm_ref[...]
        gw_ref[...] = _dot_f32(act, gm, (((0,), (0,)), ((), ())))
        gb_ref[...] = _colsum(g_ref[...])
        dact = _dot_f32(gm, w_ref[...], (((1,), (1,)), ((), ())))
        tot = dact[4:5, :]
        for dev in range(1, N_DEV):
            tot = tot + dact[8 * dev + 4:8 * dev + 5, :]
        c0 = cc_ref[...]
        s0 = _sig(c0)
        gcc_ref[...] = tot * (s0 * (1.0 + c0 * (1.0 - s0)))

    return pl.pallas_call(
        body, name="mod_bwd",
        out_shape=[SDS(w_mod_s.shape, F32), SDS((1, dmod64.shape[1]), F32), SDS((1, D_MODEL), F32)],
        in_specs=[VMEM_SPEC] * 5, out_specs=[VMEM_SPEC] * 3,
        compiler_params=pltpu.CompilerParams(vmem_limit_bytes=VMEM_LIMIT),
    )(cond64, dmod64, dmod64_my, w_mod_s, c_ctx)


SHARD_COLS = IN_COLS // N_CHIPS


def _in_fwd_own(pos, x, ctx, g_mix, mod_a, w_own, carry=None, then=None):
    bs, seq, _ = x.shape
    nb = seq // ROW_BLOCK + 1

    def body(pos_ref, x_ref, ctx_ref, g_ref, mod_ref, w_ref, p_ref, h_ref, w_bf):
        b, j = pl.program_id(0), pl.program_id(1)

        @pl.when((b == 0) & (j == 0))
        def _():
            w_bf[...] = w_ref[...].astype(BF16)

        is_ctx = j == 0
        xin = jnp.where(is_ctx, ctx_ref[0], x_ref[0])
        shift = jnp.where(is_ctx, mod_ref[0, 2:3, :], mod_ref[0, 0:1, :])
        scale = jnp.where(is_ctx, mod_ref[0, 3:4, :], mod_ref[0, 1:2, :])
        xn, _ = _rms(xin)
        hb = ((xn * g_ref[...]) * (1.0 + scale) + shift).astype(BF16)
        h_ref[0] = hb
        p_ref[0] = jnp.dot(hb, w_bf[...], preferred_element_type=F32)

    return _host_call(
        body, name="in_fwd_own", grid=(bs, nb), prefetch=pos,
        in_specs=[pl.BlockSpec((1, ROW_BLOCK, D_MODEL), lambda b, j, p: (b, jnp.maximum(j - 1, 0), 0)),
                  pl.BlockSpec((1, ROW_BLOCK, D_MODEL), lambda b, j, p: (b, 0, 0)),
                  pl.BlockSpec((1, D_MODEL), lambda b, j, p: (0, 0)),
                  pl.BlockSpec((1, 8, D_MODEL), lambda b, j, p: (b, 0, 0)),
                  pl.BlockSpec((D_MODEL, SHARD_COLS), lambda b, j, p: (0, 0))],
        out_specs=[pl.BlockSpec((1, ROW_BLOCK, SHARD_COLS), lambda b, j, p: (b, j, p[0])),
                   pl.BlockSpec((1, ROW_BLOCK, D_MODEL), lambda b, j, p: (b, j, 0))],
        out_shape=[SDS((bs, nb * ROW_BLOCK, IN_COLS), F32), SDS((bs, nb * ROW_BLOCK, D_MODEL), BF16)],
        scratch_shapes=[pltpu.VMEM((D_MODEL, SHARD_COLS), BF16)],
        args=(x, ctx, g_mix, mod_a, w_own), carry=carry, then=then)


def _in_fwd_rest(pos, h_all, w_in_g, p, carry=None):
    bs, rows, _ = h_all.shape
    rows_all = bs * rows
    tile = next(m * ROW_BLOCK for m in (9, 3, 1) if rows_all % (m * ROW_BLOCK) == 0)

    def body(pos_ref, h_ref, w_ref, p_in_ref, p_ref):
        p_ref[...] = jnp.dot(h_ref[...], w_ref[0], preferred_element_type=F32)

    shard = lambda n, p: (p[0] + 1 + n) % N_CHIPS
    (p2,), carried = _host_call(
        body, name="in_fwd_rest", grid=(N_CHIPS - 1, rows_all // tile), prefetch=pos,
        in_specs=[pl.BlockSpec((tile, D_MODEL), lambda n, t, p: (t, 0)),
                  pl.BlockSpec((1, D_MODEL, SHARD_COLS), lambda n, t, p: (shard(n, p), 0, 0)),
                  ANY_SPEC],
        out_specs=[pl.BlockSpec((tile, SHARD_COLS), lambda n, t, p: (t, shard(n, p)))],
        out_shape=[SDS((rows_all, IN_COLS), F32)], aliases={2: 0},
        args=(h_all.reshape(rows_all, D_MODEL), w_in_g, p.reshape(rows_all, IN_COLS)), carry=carry)
    return p2.reshape(bs, rows, IN_COLS), carried


def _in_bwd(x, ctx, dx1, g_mix, mod_a, w_in, df_f, df_b, dv_f, dv_b, dq_f, dq_b, dpc, carry=None):
    bs, seq, _ = x.shape
    nb = seq // ROW_BLOCK + 1

    ctx_cols = 3 * KW
    nt = (((1,), (1,)), ((), ()))

    def body(x_ref, ctx_ref, dx1_ref, g_ref, mod_ref, w_ref, dff_ref, dfb_ref, dvf_ref, dvb_ref, dqf_ref, dqb_ref,
             dpc_ref, gx_ref, dp_ref, dg_ref, dmod_ref, dh_scr):
        b, j = pl.program_id(0), pl.program_id(1)
        is_ctx = j == 0

        @pl.when((b == 0) & (j == 0))
        def _():
            dg_ref[...] = jnp.zeros_like(dg_ref)

        @pl.when(j == 0)
        def _():
            dmod_ref[...] = jnp.zeros_like(dmod_ref)

        di = (dvf_ref[0] + dvb_ref[0]).astype(BF16)
        dq = (dqf_ref[0] + dqb_ref[0]).astype(BF16)
        dp = jnp.concatenate([dff_ref[0], dfb_ref[0], di, dq, dpc_ref[0]], axis=1)
        dp_ref[0] = dp

        @pl.when(is_ctx)
        def _():
            dh_scr[...] = lax.dot_general(dp[:, :ctx_cols], w_ref[:, :ctx_cols], nt, preferred_element_type=F32)

        @pl.when(jnp.logical_not(is_ctx))
        def _():
            dh_scr[...] = lax.dot_general(dp, w_ref[...], nt, preferred_element_type=F32)

        dh = dh_scr[...]
        xin = jnp.where(is_ctx, ctx_ref[0], x_ref[0])
        scale = jnp.where(is_ctx, mod_ref[0, 3:4, :], mod_ref[0, 1:2, :])
        xn, r = _rms(xin)
        g = g_ref[...]
        hn = xn * g
        d_shift = _colsum(dh)
        d_scale = _colsum(dh * hn)
        dhn = dh * (1.0 + scale)
        dg_ref[...] += _colsum(dhn * xn)
        dx = _rms_bwd(dhn * g, xn, r)

        @pl.when(is_ctx)
        def _():
            dmod_ref[0, 2:3, :] += d_shift
            dmod_ref[0, 3:4, :] += d_scale

        @pl.when(jnp.logical_not(is_ctx))
        def _():
            dmod_ref[0, 0:1, :] += d_shift
            dmod_ref[0, 1:2, :] += d_scale
            gx_ref[0] = dx + dx1_ref[0]

    def rows(w):
        return pl.BlockSpec((1, ROW_BLOCK, w), lambda b, j: (b, j, 0))

    lat = pl.BlockSpec((1, ROW_BLOCK, D_MODEL), lambda b, j: (b, jnp.maximum(j - 1, 0), 0))
    return _host_call(
        body, name="in_bwd", grid=(bs, nb),
        in_specs=[lat, pl.BlockSpec((1, ROW_BLOCK, D_MODEL), lambda b, j: (b, 0, 0)), lat,
                  pl.BlockSpec((1, D_MODEL), lambda b, j: (0, 0)),
                  pl.BlockSpec((1, 8, D_MODEL), lambda b, j: (b, 0, 0)),
                  pl.BlockSpec((D_MODEL, IN_COLS), lambda b, j: (0, 0)),
                  rows(KW), rows(KW), rows(KW), rows(KW), rows(KW), rows(KW), rows(7 * KW)],
        out_specs=[lat, rows(IN_COLS), pl.BlockSpec((1, D_MODEL), lambda b, j: (0, 0)),
                   pl.BlockSpec((1, 8, D_MODEL), lambda b, j: (b, 0, 0))],
        out_shape=[SDS(x.shape, F32), SDS((bs, nb * ROW_BLOCK, IN_COLS), BF16), SDS((1, D_MODEL), F32),
                   SDS((bs, 8, D_MODEL), F32)],
        scratch_shapes=[pltpu.VMEM((ROW_BLOCK, D_MODEL), F32)],
        args=(x, ctx, dx1, g_mix, mod_a, w_in, df_f, df_b, dv_f, dv_b, dq_f, dq_b, dpc), carry=carry)


def _lower_bound(lbg_ref, direction):
    return _sig(lbg_ref[0, direction:direction + 1, :] - lbg_ref[1, direction:direction + 1, :])


def _block_tri(upper):
    t = np.arange(ROW_BLOCK)[:, None]
    s = np.arange(ROW_BLOCK)[None, :]
    same = (t // HGRN_CHUNK) == (s // HGRN_CHUNK)
    return jnp.asarray(same & ((s >= t) if upper else (s <= t)), dtype=BF16)


TRI_SPEC = pl.BlockSpec((ROW_BLOCK, ROW_BLOCK), lambda b, j: (0, 0))


def _tri_matmul_f32(tri, g):
    g0 = g.astype(BF16)
    r1 = g - g0.astype(F32)
    g1 = r1.astype(BF16)
    g2 = (r1 - g1.astype(F32)).astype(BF16)
    return (jnp.dot(tri, g2, preferred_element_type=F32) + jnp.dot(tri, g1, preferred_element_type=F32)) \
        + jnp.dot(tri, g0, preferred_element_type=F32)


def _chunk_rows(rows):
    return jnp.concatenate([jnp.broadcast_to(r, (HGRN_CHUNK, r.shape[1])) for r in rows], axis=0)


def _block_gates(fl, q, lb, tri, upper):
    t = {}
    t["sg"] = _sig(fl)
    t["f"] = lb + (1.0 - lb) * t["sg"]
    k = 1.0 - t["f"]
    bcum = _tri_matmul_f32(tri, jnp.log(t["f"]))
    ends = [bcum[ci * HGRN_CHUNK:ci * HGRN_CHUNK + 1] if upper else bcum[(ci + 1) * HGRN_CHUNK - 1:(ci + 1) * HGRN_CHUNK]
            for ci in range(fl.shape[0] // HGRN_CHUNK)]
    mid = _chunk_rows([0.5 * r for r in ends])
    t["dec"] = [jnp.exp(r) for r in ends]
    t["e1"] = jnp.exp(bcum - mid)
    t["e2"] = jnp.exp(mid - bcum)
    t["eh"] = _chunk_rows([jnp.exp(0.5 * r) for r in ends])
    t["qi"] = q * t["e1"]
    t["ki"] = k * t["e2"]
    t["kd"] = t["ki"] * t["eh"]
    t["qe"] = t["qi"] * t["eh"]
    return t


def _hgrn_block_order(direction, nb):
    if direction == 0:
        return lambda j: j
    return lambda j: jnp.where(j == 0, 0, nb - j)


def _hgrn_fwd(p, lbg, carry=None, then=None):
    bs, rows, _ = p.shape
    nb = rows // ROW_BLOCK
    ncb = ROW_BLOCK // HGRN_CHUNK
    orders = [_hgrn_block_order(d, nb) for d in (0, 1)]
    dirs = (0, 1)

    def body(f0_ref, i0_ref, q0_ref, f1_ref, i1_ref, q1_ref, lbg_ref, tri0_ref, tri1_ref,
             o0_ref, s0_ref, o1_ref, s1_ref, st):
        @pl.when(pl.program_id(1) == 0)
        def _():
            st[...] = jnp.zeros_like(st)

        f_refs, i_refs, q_refs = (f0_ref, f1_ref), (i0_ref, i1_ref), (q0_ref, q1_ref)
        tri_refs, o_refs, s_refs = (tri0_ref, tri1_ref), (o0_ref, o1_ref), (s0_ref, s1_ref)
        chunk = lambda a, ci, h: a[ci * HGRN_CHUNK:(ci + 1) * HGRN_CHUNK, h * HEAD_DIM:(h + 1) * HEAD_DIM]
        masks = [_tri(HGRN_CHUNK, d == 1) for d in dirs]
        t = [_block_gates(f_refs[d][0], q_refs[d][0], _lower_bound(lbg_ref, d), tri_refs[d][...], d == 1) for d in dirs]
        v = [i_refs[d][0] for d in dirs]
        intra = [[[None] * N_HEADS for _ in range(ncb)] for _ in dirs]
        ds_loc = [[[None] * N_HEADS for _ in range(ncb)] for _ in dirs]
        for ci in range(ncb):
            for h in range(N_HEADS):
                for d in dirs:
                    a = jnp.where(masks[d], _dot_nt(chunk(t[d]["qi"], ci, h), chunk(t[d]["ki"], ci, h)), 0.0)
                    intra[d][ci][h] = _dot(a, chunk(v[d], ci, h))
                    ds_loc[d][ci][h] = _dot_tn(chunk(v[d], ci, h), chunk(t[d]["kd"], ci, h))
        for h in range(N_HEADS):
            ls = slice(h * HEAD_DIM, (h + 1) * HEAD_DIM)
            s = [st[d, h] for d in dirs]
            for step in range(ncb):
                for d in dirs:
                    ci = ncb - 1 - step if d == 1 else step
                    s_refs[d][0, 0, ci, h] = s[d]
                    o_refs[d][0, ci * HGRN_CHUNK:(ci + 1) * HGRN_CHUNK, ls] = (
                        intra[d][ci][h] + _dot_nt(chunk(t[d]["qe"], ci, h), s[d]))
                    s[d] = s[d] * t[d]["dec"][ci][:, ls] + ds_loc[d][ci][h]
            for d in dirs:
                st[d, h] = s[d]

    def col(d, cb):
        return pl.BlockSpec((1, ROW_BLOCK, KW), lambda b, j: (b, orders[d](j), cb))

    def outs(d):
        return [pl.BlockSpec((1, ROW_BLOCK, KW), lambda b, j: (b, orders[d](j), 0)),
                pl.BlockSpec((1, 1, ncb, N_HEADS, HEAD_DIM, HEAD_DIM), lambda b, j: (b, orders[d](j), 0, 0, 0, 0))]

    shapes = [SDS((bs, rows, KW), F32), SDS((bs, nb, ncb, N_HEADS, HEAD_DIM, HEAD_DIM), F32)]
    return _host_call(
        body, name="hgrn_fwd", grid=(bs, nb),
        in_specs=[col(0, 0), col(0, 2), col(0, 3), col(1, 1), col(1, 2), col(1, 3),
                  pl.BlockSpec((2, 2, KW), lambda b, j: (0, 0, 0)), TRI_SPEC, TRI_SPEC],
        out_specs=outs(0) + outs(1), out_shape=shapes * 2,
        scratch_shapes=[pltpu.VMEM((2, N_HEADS, HEAD_DIM, HEAD_DIM), F32)],
        args=(p, p, p, p, p, p, lbg, _block_tri(False), _block_tri(True)), carry=carry, then=then)


def _hgrn_bwd_pair(p, lbg, s_saved, do_raw, carry=None):
    bs, rows, _ = p.shape
    nb = rows // ROW_BLOCK
    ncb = ROW_BLOCK // HGRN_CHUNK
    dirs = (0, 1)
    fwd_orders = [_hgrn_block_order(d, nb) for d in dirs]
    orders = [lambda j, d=d: fwd_orders[d](nb - 1 - j) for d in dirs]
    pairs = [(ci, h) for ci in range(ncb) for h in range(N_HEADS)]

    def body(f0_ref, i0_ref, q0_ref, s0_ref, do0_ref, f1_ref, i1_ref, q1_ref, s1_ref, do1_ref,
             lbg_ref, tril_ref, triu_ref,
             df0_ref, dq0_ref, dv0_ref, dlb0_ref, df1_ref, dq1_ref, dv1_ref, dlb1_ref, dst, acc):
        b, j = pl.program_id(0), pl.program_id(1)
        f_refs, i_refs, q_refs = (f0_ref, f1_ref), (i0_ref, i1_ref), (q0_ref, q1_ref)
        s_refs, do_refs = (s0_ref, s1_ref), (do0_ref, do1_ref)
        df_refs, dq_refs, dv_refs, dlb_refs = (df0_ref, df1_ref), (dq0_ref, dq1_ref), (dv0_ref, dv1_ref), (dlb0_ref, dlb1_ref)
        tri_refs, trit_refs = (tril_ref, triu_ref), (triu_ref, tril_ref)

        @pl.when((b == 0) & (j == 0))
        def _():
            dlb0_ref[...] = jnp.zeros_like(dlb0_ref)
            dlb1_ref[...] = jnp.zeros_like(dlb1_ref)

        @pl.when(j == 0)
        def _():
            dst[...] = jnp.zeros_like(dst)

        chunk = lambda a, ci, h: a[ci * HGRN_CHUNK:(ci + 1) * HGRN_CHUNK, h * HEAD_DIM:(h + 1) * HEAD_DIM]
        rows_of = lambda ci: slice(ci * HGRN_CHUNK, (ci + 1) * HGRN_CHUNK)
        lanes_of = lambda h: slice(h * HEAD_DIM, (h + 1) * HEAD_DIM)
        grid3 = lambda: [[[None] * N_HEADS for _ in range(ncb)] for _ in dirs]
        lbs = [_lower_bound(lbg_ref, d) for d in dirs]
        masks = [_tri(HGRN_CHUNK, d == 1) for d in dirs]
        masks_t = [_tri(HGRN_CHUNK, d != 1) for d in dirs]
        t = [_block_gates(f_refs[d][0], q_refs[d][0], lbs[d], tri_refs[d][...], d == 1) for d in dirs]
        v = [i_refs[d][0] for d in dirs]
        do = [do_refs[d][0] for d in dirs]
        a_t, da, da_t, dv_in, ds_loc = (grid3() for _ in range(5))
        for ci, h in pairs:
            for d in dirs:
                a_t[d][ci][h] = _dot_nt(chunk(t[d]["ki"], ci, h), chunk(t[d]["qi"], ci, h))
        for ci, h in pairs:
            for d in dirs:
                da[d][ci][h] = _dot_nt(chunk(do[d], ci, h), chunk(v[d], ci, h))
        for ci, h in pairs:
            for d in dirs:
                da_t[d][ci][h] = _dot_nt(chunk(v[d], ci, h), chunk(do[d], ci, h))
        for ci, h in pairs:
            for d in dirs:
                acc[d, 3, rows_of(ci), lanes_of(h)] = _dot(chunk(do[d], ci, h), s_refs[d][0, 0, ci, h])
        for ci, h in pairs:
            for d in dirs:
                ds_loc[d][ci][h] = _dot_tn(chunk(do[d], ci, h), chunk(t[d]["qe"], ci, h))
        for ci, h in pairs:
            for d in dirs:
                acc[d, 0, rows_of(ci), lanes_of(h)] = _dot(jnp.where(masks[d], da[d][ci][h], 0.0),
                                                           chunk(t[d]["ki"], ci, h))
        for ci, h in pairs:
            for d in dirs:
                acc[d, 1, rows_of(ci), lanes_of(h)] = _dot(jnp.where(masks_t[d], da_t[d][ci][h], 0.0),
                                                           chunk(t[d]["qi"], ci, h))
        for ci, h in pairs:
            for d in dirs:
                dv_in[d][ci][h] = _dot(jnp.where(masks_t[d], a_t[d][ci][h], 0.0), chunk(do[d], ci, h))
        ddec = grid3()
        for h in range(N_HEADS):
            ls = lanes_of(h)
            ds = [dst[d, h] for d in dirs]
            for step in range(ncb):
                for d in dirs:
                    ci = step if d == 1 else ncb - 1 - step
                    acc[d, 2, rows_of(ci), ls] = _dot(chunk(v[d], ci, h), ds[d])
                    acc[d, 4, rows_of(ci), ls] = dv_in[d][ci][h] + _dot_nt(chunk(t[d]["kd"], ci, h), ds[d])
                    ddec[d][ci][h] = _colsum(ds[d] * s_refs[d][0, 0, ci, h])
                    ds[d] = ds[d] * t[d]["dec"][ci][:, ls] + ds_loc[d][ci][h]
            for d in dirs:
                dst[d, h] = ds[d]
        for d in dirs:
            td = t[d]
            dqi, dki, dkd, dqe = (acc[d, i] for i in range(4))
            dq_refs[d][0] = td["e1"] * (dqi + dqe * td["eh"])
            dv_refs[d][0] = acc[d, 4]
            dk = td["e2"] * (dki + dkd * td["eh"])
            dkd_kd = dkd * td["kd"]
            db = dqi * td["qi"] - dki * td["ki"] - dkd_kd + dqe * td["qe"]
            dbl = [_colsum(dkd_kd[rows_of(ci)]) + jnp.concatenate(ddec[d][ci], axis=1) * td["dec"][ci]
                   for ci in range(ncb)]
            dg = _tri_matmul_f32(trit_refs[d][...], db) + _chunk_rows(dbl)
            df = dg / td["f"] - dk
            sg = td["sg"]
            dlb_refs[d][...] += _colsum(df * (1.0 - sg))
            df_refs[d][0] = (df * (1.0 - lbs[d]) * sg * (1.0 - sg)).astype(BF16)

    def ins(d):
        col = lambda cb: pl.BlockSpec((1, ROW_BLOCK, KW), lambda b, j: (b, orders[d](j), cb))
        return [col(d), col(2), col(3),
                pl.BlockSpec((1, 1, ncb, N_HEADS, HEAD_DIM, HEAD_DIM), lambda b, j: (b, orders[d](j), 0, 0, 0, 0)),
                pl.BlockSpec((1, ROW_BLOCK, KW), lambda b, j: (b, orders[d](j), 0))]

    def outs(d):
        row = pl.BlockSpec((1, ROW_BLOCK, KW), lambda b, j: (b, orders[d](j), 0))
        return [row, row, row, pl.BlockSpec((1, KW), lambda b, j: (0, 0))]

    shapes = [SDS((bs, rows, KW), BF16), SDS((bs, rows, KW), F32), SDS((bs, rows, KW), F32), SDS((1, KW), F32)]
    return _host_call(
        body, name="hgrn_bwd", grid=(bs, nb),
        in_specs=ins(0) + ins(1) + [pl.BlockSpec((2, 2, KW), lambda b, j: (0, 0, 0)), TRI_SPEC, TRI_SPEC],
        out_specs=outs(0) + outs(1), out_shape=shapes * 2,
        scratch_shapes=[pltpu.VMEM((2, N_HEADS, HEAD_DIM, HEAD_DIM), F32), pltpu.VMEM((2, 5, ROW_BLOCK, KW), F32)],
        args=(p, p, p, s_saved[0], do_raw, p, p, p, s_saved[1], do_raw, lbg, _block_tri(False), _block_tri(True)),
        carry=carry)


def _mix_values(og, u, v, ga, gb, o_raw, gna, lng, lnb, ws_ref, bst, wpa, wpb, wo):
    t = {}
    sog = _sig(og)
    t["sog"], t["silu_og"] = sog, og * sog
    xh_l, r_l = [], []
    for h in range(N_HEADS):
        xh, r = _rms(o_raw[:, h * HEAD_DIM:(h + 1) * HEAD_DIM])
        xh_l.append(xh)
        r_l.append(r)
    t["xh"], t["r"] = jnp.concatenate(xh_l, axis=1), r_l
    gna4 = jnp.concatenate([gna] * N_HEADS, axis=1)
    t["gna4"] = gna4
    t["o_n"] = t["xh"] * gna4
    t["o_a"] = t["o_n"] * t["silu_og"]
    t["ya"] = _dot(t["o_a"], wpa)
    t["gu"], t["tu"] = _gelu(u)
    gv, t["tv"] = _gelu(v)
    mu = jnp.mean(gv, axis=-1, keepdims=True)
    cen = gv - mu
    t["rstd"] = lax.rsqrt(jnp.mean(cen * cen, axis=-1, keepdims=True) + EPS)
    t["xhat"] = cen * t["rstd"]
    vn = t["xhat"] * lng + lnb
    t["vn"] = vn
    chunks = []
    for n in range(ROW_BLOCK // SGU_CHUNK):
        rs = slice(n * SGU_CHUNK, (n + 1) * SGU_CHUNK)
        groups = []
        for g in range(N_HEADS):
            ls = slice(g * HEAD_DIM, (g + 1) * HEAD_DIM)
            groups.append(_dot(ws_ref[g], vn[rs, ls]) + bst[:, g:g + 1])
        chunks.append(jnp.concatenate(groups, axis=1))
    t["mixed"] = jnp.concatenate(chunks, axis=0)
    t["o_bm"] = t["gu"] * t["mixed"]
    t["yb"] = _dot(t["o_bm"], wpb)
    t["sa"], t["sb"] = _sig(ga), _sig(gb)
    t["merged"] = t["sa"] * t["ya"] + t["sb"] * t["yb"]
    t["mix"] = _dot(t["merged"], wo)
    return t


def _mix_in_specs(row_of):
    def col(cb):
        return pl.BlockSpec((1, ROW_BLOCK, KW), lambda b, j: (b, row_of(j), cb))
    return [col(cb) for cb in range(4, 11)]


def _mix_param_specs():
    full2 = lambda r, c: pl.BlockSpec((r, c), lambda b, j: (0, 0))
    return [full2(1, HEAD_DIM), full2(1, KW), full2(1, KW),
            pl.BlockSpec((N_HEADS, SGU_CHUNK, SGU_CHUNK), lambda b, j: (0, 0, 0)),
            full2(SGU_CHUNK, N_HEADS), full2(KW, D_MODEL), full2(KW, D_MODEL), full2(D_MODEL, D_MODEL)]


def _mix_fwd(p, o_f, o_b, x, mod_c, gna, lng, lnb, w_s, bst, wpa, wpb, wo):
    bs, seq, _ = x.shape
    nbl = seq // ROW_BLOCK

    def body(og_r, u_r, v_r, ga0_r, ga1_r, gb0_r, gb1_r, of_r, ob_r, x_r, mod_r,
             gna_r, lng_r, lnb_r, ws_r, bst_r, wpa_r, wpb_r, wo_r, x1_r):
        ga = jnp.concatenate([ga0_r[0], ga1_r[0]], axis=1)
        gb = jnp.concatenate([gb0_r[0], gb1_r[0]], axis=1)
        t = _mix_values(og_r[0], u_r[0], v_r[0], ga, gb, of_r[0] + ob_r[0], gna_r[...], lng_r[...], lnb_r[...],
                        ws_r, bst_r[...], wpa_r[...], wpb_r[...], wo_r[...])
        x1_r[0] = x_r[0] + mod_r[0, 0:1, :] * t["mix"]

    row = lambda w: pl.BlockSpec((1, ROW_BLOCK, w), lambda b, j: (b, j + 1, 0))
    lat = pl.BlockSpec((1, ROW_BLOCK, D_MODEL), lambda b, j: (b, j, 0))
    return pl.pallas_call(
        body, name="mix_fwd", grid=(bs, nbl),
        in_specs=_mix_in_specs(lambda j: j + 1) + [row(KW), row(KW), lat,
                                                    pl.BlockSpec((1, 8, D_MODEL), lambda b, j: (b, 0, 0))]
        + _mix_param_specs(),
        out_specs=lat, out_shape=SDS(x.shape, F32), compiler_params=_params(2),
    )(p, p, p, p, p, p, p, o_f, o_b, x, mod_c, gna, lng, lnb, w_s, bst, wpa, wpb, wo)


def _mix_bwd(p, o_f, o_b, dx1, mod_c, gna, lng, lnb, w_s, w_s_t, bst, wpa, wpb, wo, carry=None):
    bs, rows, _ = p.shape
    nb = rows // ROW_BLOCK

    def body(og_r, u_r, v_r, ga0_r, ga1_r, gb0_r, gb1_r, of_r, ob_r, dx1_r, mod_r,
             gna_r, lng_r, lnb_r, ws_r, bst_r, wpa_r, wpb_r, wo_r, wst_r,
             dor_r, dpc_r, dwpa_r, dwpb_r, dwo_r, dgna_r, dlng_r, dlnb_r, dws_r, dbst_r, dmod_r):
        b, j = pl.program_id(0), pl.program_id(1)

        @pl.when((b == 0) & (j == 0))
        def _():
            for r in (dwpa_r, dwpb_r, dwo_r, dgna_r, dlng_r, dlnb_r, dws_r, dbst_r):
                r[...] = jnp.zeros_like(r)

        @pl.when(j == 0)
        def _():
            dmod_r[...] = jnp.zeros_like(dmod_r)
            dor_r[...] = jnp.zeros_like(dor_r)
            dpc_r[...] = jnp.zeros_like(dpc_r)

        @pl.when(j > 0)
        def _():
            og, u, v = og_r[0], u_r[0], v_r[0]
            ga = jnp.concatenate([ga0_r[0], ga1_r[0]], axis=1)
            gb = jnp.concatenate([gb0_r[0], gb1_r[0]], axis=1)
            gna, lng = gna_r[...], lng_r[...]
            wpa, wpb, wo = wpa_r[...], wpb_r[...], wo_r[...]
            dx1 = dx1_r[0]
            dmix = mod_r[0, 0:1, :] * dx1
            dmerged = _dot_nt(dmix, wo)
            t = _mix_values(og, u, v, ga, gb, of_r[0] + ob_r[0], gna, lng, lnb_r[...],
                            ws_r, bst_r[...], wpa, wpb, wo)
            dmod_r[0, 0:1, :] += _colsum(dx1 * t["mix"])
            dwo_r[...] += _dot_tn(t["merged"], dmix)
            sa, sb = t["sa"], t["sb"]
            dya, dyb = sa * dmerged, sb * dmerged
            dga = dmerged * t["ya"] * sa * (1.0 - sa)
            dgb = dmerged * t["yb"] * sb * (1.0 - sb)
            do_a = _dot_nt(dya, wpa)
            dwpa_r[...] += _dot_tn(t["o_a"], dya)
            do_bm = _dot_nt(dyb, wpb)
            dwpb_r[...] += _dot_tn(t["o_bm"], dyb)
            sog = t["sog"]
            dog = do_a * t["o_n"] * (sog * (1.0 + og * (1.0 - sog)))
            do_n = do_a * t["silu_og"]
            dxh = do_n * t["gna4"]
            prod = do_n * t["xh"]
            dgna = jnp.zeros((1, HEAD_DIM), F32)
            dor_l = []
            for h in range(N_HEADS):
                ls = slice(h * HEAD_DIM, (h + 1) * HEAD_DIM)
                dgna = dgna + _colsum(prod[:, ls])
                dor_l.append(_rms_bwd(dxh[:, ls], t["xh"][:, ls], t["r"][h]))
            dgna_r[...] += dgna
            dor_r[0] = jnp.concatenate(dor_l, axis=1)
            du = do_bm * t["mixed"] * _dgelu(u, t["tu"])
            dmixed = do_bm * t["gu"]
            vn = t["vn"]
            dvn_chunks = []
            for n in range(ROW_BLOCK // SGU_CHUNK):
                rs = slice(n * SGU_CHUNK, (n + 1) * SGU_CHUNK)
                groups = []
                for g in range(N_HEADS):
                    ls = slice(g * HEAD_DIM, (g + 1) * HEAD_DIM)
                    dm = dmixed[rs, ls]
                    dws_r[g] += _dot_nt(dm, vn[rs, ls])
                    dbst_r[:, g:g + 1] += jnp.sum(dm, axis=1, keepdims=True)
                    groups.append(_dot(wst_r[g], dm))
                dvn_chunks.append(jnp.concatenate(groups, axis=1))
            dvn = jnp.concatenate(dvn_chunks, axis=0)
            xhat = t["xhat"]
            dlng_r[...] += _colsum(dvn * xhat)
            dlnb_r[...] += _colsum(dvn)
            dxhat = dvn * lng
            dgv = t["rstd"] * (dxhat - jnp.mean(dxhat, axis=-1, keepdims=True)
                               - xhat * jnp.mean(dxhat * xhat, axis=-1, keepdims=True))
            dv = dgv * _dgelu(v, t["tv"])
            dpc_r[0] = jnp.concatenate([dog, du, dv, dga, dgb], axis=1).astype(BF16)

    row = lambda w: pl.BlockSpec((1, ROW_BLOCK, w), lambda b, j: (b, j, 0))
    lat = pl.BlockSpec((1, ROW_BLOCK, D_MODEL), lambda b, j: (b, jnp.maximum(j - 1, 0), 0))
    full2 = lambda r, c: pl.BlockSpec((r, c), lambda b, j: (0, 0))
    ws_spec = pl.BlockSpec((N_HEADS, SGU_CHUNK, SGU_CHUNK), lambda b, j: (0, 0, 0))
    return _host_call(
        body, name="mix_bwd", grid=(bs, nb),
        in_specs=_mix_in_specs(lambda j: j) + [row(KW), row(KW), lat,
                                                pl.BlockSpec((1, 8, D_MODEL), lambda b, j: (b, 0, 0))]
        + _mix_param_specs() + [ws_spec],
        out_specs=[row(KW), row(7 * KW), full2(KW, D_MODEL), full2(KW, D_MODEL), full2(D_MODEL, D_MODEL),
                   full2(1, HEAD_DIM), full2(1, KW), full2(1, KW), ws_spec, full2(SGU_CHUNK, N_HEADS),
                   pl.BlockSpec((1, 8, D_MODEL), lambda b, j: (b, 0, 0))],
        out_shape=[SDS((bs, rows, KW), F32), SDS((bs, rows, 7 * KW), BF16), SDS((KW, D_MODEL), F32),
                   SDS((KW, D_MODEL), F32), SDS((D_MODEL, D_MODEL), F32), SDS((1, HEAD_DIM), F32),
                   SDS((1, KW), F32), SDS((1, KW), F32), SDS((N_HEADS, SGU_CHUNK, SGU_CHUNK), F32),
                   SDS((SGU_CHUNK, N_HEADS), F32), SDS((bs, 8, D_MODEL), F32)],
        args=(p, p, p, p, p, p, p, o_f, o_b, dx1, mod_c, gna, lng, lnb, w_s, bst, wpa, wpb, wo, w_s_t), carry=carry)


def _ffn(x1, target, mod_c, g_ffn, g_final, w_up, w_down):
    bs, seq, _ = x1.shape
    nbl = seq // ROW_BLOCK

    def body(x1_r, tg_r, mod_r, gf_r, gl_r, wu_r, wd_r,
             dx1_r, h2_r, dab_r, hid_r, dffn_r, loss_r, dgl_r, dgf_r, dmod_r):
        b, j = pl.program_id(0), pl.program_id(1)

        @pl.when((b == 0) & (j == 0))
        def _():
            for r in (loss_r, dgl_r, dgf_r):
                r[...] = jnp.zeros_like(r)

        @pl.when(j == 0)
        def _():
            dmod_r[...] = jnp.zeros_like(dmod_r)

        x1 = x1_r[0]
        shift, scale, gate = mod_r[0, 1:2, :], mod_r[0, 2:3, :], mod_r[0, 3:4, :]
        gf, gl = gf_r[...], gl_r[...]
        xn2, r2 = _rms(x1)
        hn2 = xn2 * gf
        h2 = (hn2 * (1.0 + scale) + shift).astype(BF16)
        h2_r[0] = h2
        ab = jnp.dot(h2, wu_r[...], preferred_element_type=F32)
        a, bb = ab[:, :D_FF], ab[:, D_FF:]
        sa = _sig(a)
        silu_a = a * sa
        hid = (silu_a * bb).astype(BF16)
        hid_r[0] = hid
        ffn = jnp.dot(hid, wd_r[...], preferred_element_type=F32)
        x2 = x1 + gate * ffn
        xn3, r3 = _rms(x2)
        err = xn3 * gl - tg_r[0]
        loss_r[...] += 0.5 * jnp.sum(jnp.mean(err * err, axis=-1, keepdims=True), axis=0, keepdims=True)
        dy = err * (1.0 / D_MODEL)
        dgl_r[...] += _colsum(dy * xn3)
        dx2 = _rms_bwd(dy * gl, xn3, r3)
        dmod_r[0, 3:4, :] += _colsum(dx2 * ffn)
        dffn = (gate * dx2).astype(BF16)
        dffn_r[0] = dffn
        dhid = lax.dot_general(dffn, wd_r[...], (((1,), (1,)), ((), ())), preferred_element_type=F32)
        da = dhid * bb * (sa * (1.0 + a * (1.0 - sa)))
        db = dhid * silu_a
        dab = jnp.concatenate([da, db], axis=1).astype(BF16)
        dab_r[0] = dab
        dh2 = lax.dot_general(dab, wu_r[...], (((1,), (1,)), ((), ())), preferred_element_type=F32)
        dmod_r[0, 1:2, :] += _colsum(dh2)
        dmod_r[0, 2:3, :] += _colsum(dh2 * hn2)
        dhn2 = dh2 * (1.0 + scale)
        dgf_r[...] += _colsum(dhn2 * xn2)
        dx1_r[0] = dx2 + _rms_bwd(dhn2 * gf, xn2, r2)

    lat = lambda w: pl.BlockSpec((1, ROW_BLOCK, w), lambda b, j: (b, j, 0))
    full2 = lambda r, c: pl.BlockSpec((r, c), lambda b, j: (0, 0))
    mod_spec = pl.BlockSpec((1, 8, D_MODEL), lambda b, j: (b, 0, 0))
    return pl.pallas_call(
        body, name="ffn", grid=(bs, nbl),
        in_specs=[lat(D_MODEL), lat(D_MODEL), mod_spec, full2(1, D_MODEL), full2(1, D_MODEL),
                  full2(D_MODEL, 2 * D_FF), full2(D_FF, D_MODEL)],
        out_specs=[lat(D_MODEL), lat(D_MODEL), lat(2 * D_FF), lat(D_FF), lat(D_MODEL),
                   full2(1, 1), full2(1, D_MODEL), full2(1, D_MODEL), mod_spec],
        out_shape=[SDS(x1.shape, F32), SDS(x1.shape, BF16), SDS((bs, seq, 2 * D_FF), BF16),
                   SDS((bs, seq, D_FF), BF16), SDS(x1.shape, BF16), SDS((1, 1), F32),
                   SDS((1, D_MODEL), F32), SDS((1, D_MODEL), F32), SDS((bs, 8, D_MODEL), F32)],
        compiler_params=_params(2),
    )(x1, target, mod_c, g_ffn, g_final, w_up, w_down)


def _row_tile(rows, most):
    return next(m * ROW_BLOCK for m in (9, 8, 4, 2, 1) if m <= most and rows % (m * ROW_BLOCK) == 0)


def _matmul_tn(a, b, n_blocks, tk, name, carry=None):
    t, m = a.shape
    n = b.shape[1]
    tn = n // n_blocks

    def body(a_ref, b_ref, o_ref):
        @pl.when(pl.program_id(1) == 0)
        def _():
            o_ref[...] = jnp.zeros_like(o_ref)
        o_ref[0] += _dot_tn(a_ref[...], b_ref[...])

    (out,), carried = _host_call(
        body, name=name, grid=(n_blocks, t // tk),
        in_specs=[pl.BlockSpec((tk, m), lambda i, k: (k, 0)), pl.BlockSpec((tk, tn), lambda i, k: (k, i))],
        out_specs=[pl.BlockSpec((1, m, tn), lambda i, k: (i, 0, 0))],
        out_shape=[SDS((n_blocks, m, tn), F32)], args=(a, b), carry=carry)
    return out if carry is None else (out, carried)


SMALL_ROWS = 80
ROW_CCTX = 3


def _small_reduce(gathered, lbg):
    def body(g_ref, lbg_ref, s_ref, dgam_ref):
        tot = g_ref[0:SMALL_ROWS, :]
        for dev in range(1, N_DEV):
            tot = tot + g_ref[dev * SMALL_ROWS:(dev + 1) * SMALL_ROWS, :]
        s_ref[...] = tot
        cc = g_ref[ROW_CCTX:ROW_CCTX + 1, :]
        for dev in range(2, N_DEV, 2):
            cc = cc + g_ref[dev * SMALL_ROWS + ROW_CCTX:dev * SMALL_ROWS + ROW_CCTX + 1, :]
        s_ref[ROW_CCTX:ROW_CCTX + 1, :] = cc
        dlb = tot[7:8, :]
        for d in range(2):
            s0 = _sig(lbg_ref[0, d:d + 1, :] - lbg_ref[1, d:d + 1, :])
            dgam_ref[d:d + 1, :] = dlb[:, d * KW:(d + 1) * KW] * s0 * (1.0 - s0)

    return pl.pallas_call(
        body, name="small_reduce", out_shape=[SDS((SMALL_ROWS, D_MODEL), F32), SDS((2, KW), F32)],
        in_specs=[VMEM_SPEC] * 2, out_specs=[VMEM_SPEC] * 2,
    )(gathered, lbg)


def _pad_cols(a, width):
    return jnp.pad(a, ((0, 0), (0, width - a.shape[1])))


def kernel(x, c, ctx, c_ctx, w_mod, b_mod, g_mix, g_ffn, w_in, lb_gamma, g_norm_a, ln_v_g, ln_v_b, w_s, b_s, w_pa, w_pb, w_o, w_up, w_down, g_final, loss_target, m_c_ctx, m_w_mod, m_b_mod, m_g_mix, m_g_ffn, m_w_in, m_lb_gamma, m_g_norm_a, m_ln_v_g, m_ln_v_b, m_w_s, m_b_s, m_w_pa, m_w_pb, m_w_o, m_w_up, m_w_down, m_g_final, v_c_ctx, v_w_mod, v_b_mod, v_g_mix, v_g_ffn, v_w_in, v_lb_gamma, v_g_norm_a, v_ln_v_g, v_ln_v_b, v_w_s, v_b_s, v_w_pa, v_w_pb, v_w_o, v_w_up, v_w_down, v_g_final):
    ax, ay, ac = lax.axis_index("x"), lax.axis_index("y"), lax.axis_index("c")
    kc = 2 * ax + ay
    dev = 2 * kc + ac
    pos = jnp.stack([kc, ac]).astype(jnp.int32)
    bs, seq, _ = x.shape
    assert bs <= 4 and ctx.shape[1] == ROW_BLOCK and seq % ROW_BLOCK == 0
    mod_cols = w_mod.shape[2]

    lbg_row = _pad_cols(lb_gamma.reshape(1, -1), D_MODEL)
    pay1 = jnp.concatenate([c, jnp.zeros((4 - bs, D_MODEL), F32), c_ctx[None, :], lbg_row,
                            jnp.zeros((2, D_MODEL), F32)], axis=0)
    b_mod_s = lax.dynamic_slice(b_mod, (0, kc * mod_cols), (1, mod_cols))
    cond64, mod_g = _cond_and_mod(pay1, w_mod[0], b_mod_s)
    lbg_full = cond64.reshape(N_DEV, 8, D_MODEL)[0::2, 5, :KW].reshape(N_CHIPS, 2, 2, HEAD_DIM)
    lbg_full = jnp.transpose(lbg_full, (1, 2, 0, 3)).reshape(2, 2, KW)
    mod_g = mod_g.reshape(N_DEV, 64, mod_cols)[0::2]
    shards = [w_in[0], w_up[0], w_pa[0], w_pb[0], w_o[0], w_down[0]]
    bufs = _cast_bf16(pos, shards)
    mod_full = jnp.transpose(mod_g, (1, 0, 2)).reshape(64, N_CHIPS * mod_cols)
    mod_mine = lax.dynamic_slice(mod_full, (dev * 8, 0), (8, 6 * D_MODEL)).reshape(8, 6, D_MODEL)
    mod, mc = mod_mine[:bs], mod_mine[4]
    zeros4 = jnp.zeros((bs, 4, D_MODEL), F32)
    mod_a = jnp.concatenate([mod[:, 0:2], jnp.broadcast_to(mc[None, 0:2], (bs, 2, D_MODEL)), zeros4], axis=1)
    mod_c = jnp.concatenate([mod[:, 2:6], zeros4], axis=1)

    def cols_major(a):
        return jnp.transpose(a, (1, 0, 2)).reshape(a.shape[1], -1)

    gna, lng, lnb = g_norm_a, ln_v_g, ln_v_b
    ws3 = w_s[0]
    ws3_t = jnp.transpose(ws3, (0, 2, 1))
    bst = jnp.transpose(b_s[0])

    (p, h_all), (w_in_g,) = _in_fwd_own(pos, x, ctx, g_mix, mod_a, w_in[0], carry=_carry_gather_send(bufs[:1]),
                                        then=_carry_gather_forward(bufs[:1]))
    p, sent_up = _in_fwd_rest(pos, h_all, w_in_g, p, carry=_carry_gather_send(bufs[1:2]))
    w_in_f = cols_major(w_in_g)
    fwd_rest = _carry_gather_forward(bufs[2:])
    then_rest = _Carry([], [], {}, fwd_rest.sems, lambda i, o, s: fwd_rest.copies(i[1:], o[1:], s))
    (o_f, s_f, o_b, s_b), gathered = _hgrn_fwd(
        p, lbg_full, carry=_merge_carries(_carry_gather_forward(sent_up), _carry_gather_send(bufs[2:])),
        then=then_rest)
    w_up_f, w_pa_f, w_pb_f = (cols_major(a) for a in gathered[:3])
    w_o_f = gathered[3].reshape(-1, D_MODEL)
    w_down_f = gathered[4].reshape(-1, D_MODEL)
    x1 = _mix_fwd(p, o_f, o_b, x, mod_c, gna, lng, lnb, ws3, bst, w_pa_f, w_pb_f, w_o_f)
    dx1, h2, dab, hid, dffn, loss_part, dg_final, dg_ffn, dmod_ffn = _ffn(
        x1, loss_target, mod_c, g_ffn, g_final[None, :], w_up_f, w_down_f)
    rows_lat = bs * seq
    dw_up = _matmul_tn(h2.reshape(rows_lat, D_MODEL), dab.reshape(rows_lat, 2 * D_FF), N_CHIPS,
                       _row_tile(rows_lat, 9), "dw_up")
    dw_down = _matmul_tn(hid.reshape(rows_lat, D_FF), dffn.reshape(rows_lat, D_MODEL), 1,
                         _row_tile(rows_lat, 4), "dw_down")

    def shard_major(a):
        return jnp.transpose(a.reshape(a.shape[0], N_CHIPS, -1), (1, 0, 2))

    part_ffn = [dw_up, dw_down.reshape(N_CHIPS, -1, D_MODEL)]
    (do_raw, dpc, dw_pa, dw_pb, dw_o, dgna, dlng, dlnb, dws, dbst, dmod_mix), sib_ffn = _mix_bwd(
        p, o_f, o_b, dx1, mod_c, gna, lng, lnb, ws3, ws3_t, bst, w_pa_f, w_pb_f, w_o_f,
        carry=_carry_sibling_halves(part_ffn))
    cpbf_ffn = _rs_add_halves(pos, part_ffn, sib_ffn, "rs_add_ffn")
    part_mix = [shard_major(dw_pa), shard_major(dw_pb), dw_o.reshape(N_CHIPS, -1, D_MODEL)]
    (df_f, dq_f, dv_f, dlb0, df_b, dq_b, dv_b, dlb1), got = _hgrn_bwd_pair(
        p, lbg_full, (s_f, s_b), do_raw,
        carry=_merge_carries(_carry_to_owner(cpbf_ffn), _carry_sibling_halves(part_mix)))
    own_ffn, sib_mix = got[:2], got[2:]
    half_ffn = _rs_sum_owner(pos, part_ffn, sib_ffn, own_ffn, "rs_sum_ffn")
    cpbf_mix = _rs_add_halves(pos, part_mix, sib_mix, "rs_add_mix")
    (grad_x, dp, dg_mix, dmod_in), _ = _in_bwd(x, ctx, dx1, g_mix, mod_a, w_in_f, df_f, df_b, dv_f, dv_b, dq_f, dq_b,
                                               dpc)

    rows_all = dp.shape[0] * dp.shape[1]
    tk_all = _row_tile(rows_all, 9)
    dw_in, got = _matmul_tn(h_all.reshape(rows_all, D_MODEL), dp.reshape(rows_all, IN_COLS), N_CHIPS, tk_all, "dw_in",
                            carry=_merge_carries(_carry_join_halves(half_ffn), _carry_to_owner(cpbf_mix)))
    g_ffn_w, own_mix = got[:2], got[2:]
    half_mix = _rs_sum_owner(pos, part_mix, sib_mix, own_mix, "rs_sum_mix")

    dmod_mine = jnp.concatenate([dmod_in[:, 0], dmod_in[:, 1], dmod_mix[:, 0], dmod_ffn[:, 1], dmod_ffn[:, 2],
                                 dmod_ffn[:, 3]], axis=1)
    dmc = jnp.concatenate([jnp.sum(dmod_in[:, 2], axis=0), jnp.sum(dmod_in[:, 3], axis=0),
                           jnp.zeros((4 * D_MODEL,), F32)])[None, :]
    pay3 = jnp.concatenate([dmod_mine, jnp.zeros((4 - bs, 6 * D_MODEL), F32), dmc,
                            jnp.zeros((3, 6 * D_MODEL), F32)], axis=0)
    dmod64, got = _all_gather8(pay3, "gather_dmod", carry=_merge_carries(_carry_sibling_halves([dw_in]),
                                                                          _carry_join_halves(half_mix)))
    sib_in, g_mix_w = got[:1], got[1:]
    cpbf_in = _rs_add_halves(pos, [dw_in], sib_in, "rs_add_w_in")
    dmod64_my = lax.dynamic_slice(dmod64, (0, kc * mod_cols), (64, mod_cols))
    g_w_mod, g_b_mod, g_cctx_part = _mod_bwd(cond64, dmod64, dmod64_my, w_mod[0], c_ctx[None, :])

    def row(*parts):
        return _pad_cols(jnp.concatenate([q.reshape(1, -1) for q in parts], axis=1), D_MODEL)

    small_rows = [dg_mix, dg_ffn, dg_final, g_cctx_part, row(dgna), row(dlng, dlnb), row(jnp.transpose(dbst)),
                  row(dlb0, dlb1), row(loss_part), jnp.zeros((7, D_MODEL), F32), dws.reshape(64, D_MODEL)]
    pay4 = jnp.concatenate(small_rows, axis=0)
    tot, dgam0 = _small_reduce(_all_gather8(pay4, "gather_small"), lbg_full)

    own_sems, own_src, own_land, own_token = _owner_send_start(cpbf_in[0], after=tot)

    rest_names = ["w_up", "w_pa", "w_pb", "w_o", "w_down", "w_mod"]
    rest_w = shards[1:] + [w_mod[0]]
    rest_g = [g_ffn_w[0], g_mix_w[0] + own_token[0, 0], g_mix_w[1], g_mix_w[2], g_ffn_w[1], g_w_mod]
    rest_m = [m_w_up[0], m_w_pa[0], m_w_pb[0], m_w_o[0], m_w_down[0], m_w_mod[0]]
    rest_v = [v_w_up[0], v_w_pa[0], v_w_pb[0], v_w_o[0], v_w_down[0], v_w_mod[0]]
    (ds_r, m2s_r, v2s_r), _ = _adamw_group(rest_w, rest_g, rest_m, rest_v, "adamw_rest")
    res = {}
    for name, g, d, m2, v2 in zip(rest_names, rest_g, ds_r, m2s_r, v2s_r):
        res[name] = (g[None], d[None], m2[None], v2[None])
    own_in = [_owner_send_wait(own_sems, own_src, own_land, after=(ds_r[0],))]
    g_in_w = _comm_call("rs_join_w_in",
                        _carry_join_halves(_rs_sum_owner(pos, [dw_in], sib_in, own_in, "rs_sum_w_in")))
    d, m2, v2 = _adamw_big(shards[0], g_in_w[0], m_w_in[0], v_w_in[0], "adamw_w_in")
    res["w_in"] = (g_in_w[0][None], d[None], m2[None], v2[None])

    loss = tot[8, 0]
    dgam_full = jnp.stack([dgam0, -dgam0])
    g_lbg = lax.dynamic_slice(dgam_full, (0, 0, kc * HEAD_DIM), (2, 2, HEAD_DIM))

    small = [
        ("c_ctx", c_ctx[None, :], tot[3:4], m_c_ctx, v_c_ctx),
        ("b_mod", b_mod, g_b_mod, m_b_mod, v_b_mod),
        ("g_mix", g_mix, tot[0:1], m_g_mix, v_g_mix),
        ("g_ffn", g_ffn, tot[1:2], m_g_ffn, v_g_ffn),
        ("lb_gamma", lb_gamma.reshape(4, HEAD_DIM), g_lbg.reshape(4, HEAD_DIM), m_lb_gamma, v_lb_gamma),
        ("g_norm_a", g_norm_a, tot[4:5, :HEAD_DIM], m_g_norm_a, v_g_norm_a),
        ("ln_v_g", ln_v_g, tot[5:6, :KW], m_ln_v_g, v_ln_v_g),
        ("ln_v_b", ln_v_b, tot[5:6, KW:], m_ln_v_b, v_ln_v_b),
        ("w_s", w_s.reshape(N_HEADS * SGU_CHUNK, SGU_CHUNK), tot[16:80].reshape(N_HEADS * SGU_CHUNK, SGU_CHUNK),
         m_w_s, v_w_s),
        ("b_s", b_s[0], tot[6:7, :KW].reshape(N_HEADS, SGU_CHUNK), m_b_s, v_b_s),
        ("g_final", g_final[None, :], tot[2:3], m_g_final, v_g_final),
    ]
    ws_, gs_ = [s[1] for s in small], [s[2] for s in small]
    ms_ = [s[3].reshape(s[1].shape) for s in small]
    vs_ = [s[4].reshape(s[1].shape) for s in small]
    ds_, m2s_, v2s_ = _adamw_small(ws_, gs_, ms_, vs_)
    for (name, _, g, m, _), d, m2, v2 in zip(small, ds_, m2s_, v2s_):
        res[name] = tuple(t.reshape(m.shape) for t in (g, d, m2, v2))

    order = ["c_ctx", "w_mod", "b_mod", "g_mix", "g_ffn", "w_in", "lb_gamma", "g_norm_a", "ln_v_g", "ln_v_b",
             "w_s", "b_s", "w_pa", "w_pb", "w_o", "w_up", "w_down", "g_final"]
    outs = [loss, grad_x]
    for part in range(4):
        outs += [res[n][part] for n in order]
    return tuple(outs)
```

```python
import functools
import math

import jax
import jax.numpy as jnp
import numpy as np
from jax import lax
from jax.experimental import pallas as pl
from jax.experimental.pallas import tpu as pltpu

F32 = jnp.float32
BF16 = jnp.bfloat16
SDS = jax.ShapeDtypeStruct
MESH = pl.DeviceIdType.MESH

EPS = 1e-6
D_MODEL = 1024
N_HEADS = 4
HEAD_DIM = 128
KW = N_HEADS * HEAD_DIM
IN_COLS = 11 * KW
D_FF = 2816
HGRN_CHUNK = 64
SGU_CHUNK = 128
ROW_BLOCK = 256
N_CHIPS = 4
N_DEV = 8
V7X_VMEM_BYTES = 64 * 1024 * 1024
VMEM_LIMIT = V7X_VMEM_BYTES - 6 * 1024 * 1024

ADAM_LR, ADAM_B1, ADAM_B2, ADAM_EPS, ADAM_WD, ADAM_STEP = 0.001, 0.9, 0.999, 1e-08, 0.01, 10
GELU_C0 = math.sqrt(2.0 / math.pi)
GELU_C1 = 0.044715

VMEM_SPEC = pl.BlockSpec(memory_space=pltpu.VMEM)
ANY_SPEC = pl.BlockSpec(memory_space=pl.ANY)


def _params(n_grid):
    return pltpu.CompilerParams(dimension_semantics=("arbitrary",) * n_grid, vmem_limit_bytes=VMEM_LIMIT)


def _sig(x):
    return 0.5 * jnp.tanh(0.5 * x) + 0.5


def _gelu(x):
    t = jnp.tanh(GELU_C0 * (x + GELU_C1 * x * x * x))
    return 0.5 * x * (1.0 + t), t


def _dgelu(x, t):
    return 0.5 * (1.0 + t) + 0.5 * x * (1.0 - t * t) * GELU_C0 * (1.0 + 3.0 * GELU_C1 * x * x)


def _dot(a, b):
    return jnp.dot(a.astype(BF16), b.astype(BF16), preferred_element_type=F32)


def _dot_nt(a, b):
    return lax.dot_general(a.astype(BF16), b.astype(BF16), (((1,), (1,)), ((), ())), preferred_element_type=F32)


def _dot_tn(a, b):
    return lax.dot_general(a.astype(BF16), b.astype(BF16), (((0,), (0,)), ((), ())), preferred_element_type=F32)


def _dot_f32(a, b, dims=(((1,), (0,)), ((), ()))):
    return lax.dot_general(a, b, dims, precision=lax.Precision.HIGHEST, preferred_element_type=F32)


def _rms(x):
    r = lax.rsqrt(jnp.mean(x * x, axis=-1, keepdims=True) + EPS)
    return x * r, r


def _rms_bwd(dxn, xn, r):
    return r * (dxn - xn * jnp.mean(dxn * xn, axis=-1, keepdims=True))


def _colsum(a):
    return jnp.sum(a, axis=0, keepdims=True)


def _tri(n, upper):
    t = lax.broadcasted_iota(jnp.int32, (n, n), 0)
    s = lax.broadcasted_iota(jnp.int32, (n, n), 1)
    return (s >= t) if upper else (s <= t)


def _all_gather8(x_shard, name, carry=None, then=None):
    m_per, n = x_shard.shape
    n_ci = len(carry.ins) if carry else 0
    n_co = len(carry.outs) if carry else 0
    n_cs = len(carry.sems) if carry else 0

    def body(*refs):
        x_ref, cins = refs[0], refs[1:1 + n_ci]
        out_ref, couts = refs[1 + n_ci], refs[2 + n_ci:2 + n_ci + n_co]
        send_sems, recv_sems, local_sem = refs[2 + n_ci + n_co:5 + n_ci + n_co]
        csems = refs[5 + n_ci + n_co:5 + n_ci + n_co + n_cs]
        tsems = refs[5 + n_ci + n_co + n_cs:]
        if carry:
            _start_all(carry.copies(cins, couts, csems)[0])
        _gather8_body(x_ref, out_ref, send_sems, recv_sems, local_sem, m_per)
        if carry:
            _wait_all(carry.copies(cins, couts, csems)[1])
        if then:
            _start_all(then.copies(couts, couts, tsems)[0])
            _wait_all(then.copies(couts, couts, tsems)[1])

    res = pl.pallas_call(
        body, name=name, out_shape=[SDS((N_DEV * m_per, n), x_shard.dtype)] + (carry.outs if carry else []),
        in_specs=[VMEM_SPEC] + [ANY_SPEC] * n_ci, out_specs=[VMEM_SPEC] + [ANY_SPEC] * n_co,
        input_output_aliases={1 + i: 1 + o for i, o in carry.alias.items()} if carry else {},
        scratch_shapes=[pltpu.SemaphoreType.DMA((7,)), pltpu.SemaphoreType.DMA((7,)), pltpu.SemaphoreType.DMA]
        + (carry.sems if carry else []) + (then.sems if then else []),
    )(x_shard, *(carry.ins if carry else []))
    return res[0] if carry is None else (res[0], list(res[1:]))


def _gather8_body(x_ref, out_ref, send_sems, recv_sems, local_sem, m_per):
    x, y, c = lax.axis_index("x"), lax.axis_index("y"), lax.axis_index("c")
    me, sibling = (x, y, c), (x, y, 1 - c)
    chips = [(1 - x, y), (x, 1 - y), (1 - x, 1 - y)]

    def rows(px, py, pc):
        return out_ref.at[pl.ds((4 * px + 2 * py + pc) * m_per, m_per), :]

    def copy(k, block, to, src=None):
        return pltpu.make_async_remote_copy(
            src_ref=rows(*block) if src is None else src, dst_ref=rows(*block),
            send_sem=send_sems.at[k], recv_sem=recv_sems.at[k], device_id=to, device_id_type=MESH)

    mine = pltpu.make_async_copy(x_ref, rows(*me), local_sem)
    mine.start()
    first = [copy(0, me, sibling, src=x_ref)]
    first += [copy(1 + j, me, (*chip, c), src=x_ref) for j, chip in enumerate(chips)]
    for cp in first:
        cp.start()
    passed = [copy(4 + j, (*chip, c), sibling) for j, chip in enumerate(chips)]
    for j, chip in enumerate(chips):
        copy(1 + j, (*chip, c), me).wait_recv()
        passed[j].start()
    copy(0, sibling, me).wait_recv()
    for j, chip in enumerate(chips):
        copy(4 + j, (*chip, 1 - c), me).wait_recv()
    for cp in first + passed:
        cp.wait_send()
    mine.wait()


def _mesh_pos():
    x, y, c = lax.axis_index("x"), lax.axis_index("y"), lax.axis_index("c")
    chips = [(1 - x, y), (x, 1 - y), (1 - x, 1 - y)]
    return x, y, c, 2 * x + y, (x, y, 1 - c), chips


def _half_rows(c, rh):
    return pl.ds(pl.multiple_of(c * rh, 16), rh)


class _Carry:
    def __init__(self, ins, outs, alias, sems, copies):
        self.ins, self.outs, self.alias, self.sems, self.copies = list(ins), list(outs), dict(alias), list(sems), copies


def _remote(src, dst, send, recv, to):
    return functools.partial(pltpu.make_async_remote_copy, src_ref=src, dst_ref=dst, send_sem=send, recv_sem=recv,
                             device_id=to, device_id_type=MESH)


def _carry_gather_send(bufs):
    n = len(bufs)

    def copies(ins, outs, sems):
        x, y, c, kc, sibling, chips = _mesh_pos()
        starts, waits = [], []
        for wi in range(n):
            rh = outs[wi].shape[1] // 2
            for jj, chip in enumerate(chips):
                mine = outs[wi].at[kc, _half_rows(c, rh), :]
                cp = _remote(mine, mine, sems[0].at[wi, jj], sems[1].at[wi, jj], (*chip, c))
                starts.append(cp)
                waits.append((cp, "send"))
                theirs = outs[wi].at[2 * chip[0] + chip[1], _half_rows(c, rh), :]
                waits.append((_remote(theirs, theirs, sems[0].at[wi, jj], sems[1].at[wi, jj], (*chip, c)), "recv"))
        return starts, waits

    return _Carry(bufs, [SDS(b.shape, b.dtype) for b in bufs], {i: i for i in range(n)},
                  [pltpu.SemaphoreType.DMA((n, 3)), pltpu.SemaphoreType.DMA((n, 3))], copies)


def _carry_gather_forward(bufs):
    n = len(bufs)

    def copies(ins, outs, sems):
        x, y, c, kc, sibling, chips = _mesh_pos()
        starts, waits = [], []
        for wi in range(n):
            rh = outs[wi].shape[1] // 2
            for jj, chip in enumerate(chips):
                got = outs[wi].at[2 * chip[0] + chip[1], _half_rows(c, rh), :]
                cp = _remote(got, got, sems[0].at[wi, jj], sems[1].at[wi, jj], sibling)
                starts.append(cp)
                waits.append((cp, "send"))
                other = outs[wi].at[2 * chip[0] + chip[1], _half_rows(1 - c, rh), :]
                waits.append((_remote(other, other, sems[0].at[wi, jj], sems[1].at[wi, jj], sibling), "recv"))
        return starts, waits

    return _Carry(bufs, [SDS(b.shape, b.dtype) for b in bufs], {i: i for i in range(n)},
                  [pltpu.SemaphoreType.DMA((n, 3)), pltpu.SemaphoreType.DMA((n, 3))], copies)


def _carry_sibling_halves(grads):
    n = len(grads)

    def copies(ins, outs, sems):
        x, y, c, kc, sibling, chips = _mesh_pos()
        cps = [_remote(ins[wi].at[:, _half_rows(1 - c, ins[wi].shape[1] // 2), :], outs[wi],
                       sems[0].at[wi], sems[1].at[wi], sibling) for wi in range(n)]
        return cps, [(cp, "both") for cp in cps]

    return _Carry(grads, [SDS((N_CHIPS, g.shape[1] // 2, g.shape[2]), F32) for g in grads], {},
                  [pltpu.SemaphoreType.DMA((n,)), pltpu.SemaphoreType.DMA((n,))], copies)


def _carry_to_owner(cpbfs):
    n = len(cpbfs)

    def copies(ins, outs, sems):
        x, y, c, kc, sibling, chips = _mesh_pos()
        starts, waits = [], []
        for wi in range(n):
            for jj, chip in enumerate(chips):
                cp = _remote(ins[wi].at[2 * chip[0] + chip[1]], outs[wi].at[kc],
                             sems[0].at[wi, jj], sems[1].at[wi, jj], (*chip, c))
                starts.append(cp)
                waits.append((cp, "send"))
                slot = outs[wi].at[2 * chip[0] + chip[1]]
                waits.append((_remote(slot, slot, sems[0].at[wi, jj], sems[1].at[wi, jj], (*chip, c)), "recv"))
        return starts, waits

    return _Carry(cpbfs, [SDS(g.shape, BF16) for g in cpbfs], {},
                  [pltpu.SemaphoreType.DMA((n, 3)), pltpu.SemaphoreType.DMA((n, 3))], copies)


def _carry_join_halves(bufs):
    n = len(bufs)

    def copies(ins, outs, sems):
        x, y, c, kc, sibling, chips = _mesh_pos()
        cps = []
        for wi in range(n):
            mine = outs[wi].at[_half_rows(c, outs[wi].shape[0] // 2), :]
            cps.append(_remote(mine, mine, sems[0].at[wi], sems[1].at[wi], sibling))
        return cps, [(cp, "both") for cp in cps]

    return _Carry(bufs, [SDS(b.shape, F32) for b in bufs], {i: i for i in range(n)},
                  [pltpu.SemaphoreType.DMA((n,)), pltpu.SemaphoreType.DMA((n,))], copies)


def _merge_carries(*carries):
    ins, outs, alias, sems, parts = [], [], {}, [], []
    for cy in carries:
        parts.append((len(ins), len(cy.ins), len(outs), len(cy.outs), len(sems), len(cy.sems), cy.copies))
        alias.update({len(ins) + i: len(outs) + o for i, o in cy.alias.items()})
        ins += cy.ins
        outs += cy.outs
        sems += cy.sems

    def copies(i, o, s):
        starts, waits = [], []
        for i0, ni, o0, no, s0, ns, fn in parts:
            st, wt = fn(i[i0:i0 + ni], o[o0:o0 + no], s[s0:s0 + ns])
            starts += st
            waits += wt
        return starts, waits

    return _Carry(ins, outs, alias, sems, copies)


def _start_all(starts):
    for cp in starts:
        cp().start()


def _wait_all(waits):
    for cp, which in waits:
        if which == "send":
            cp().wait_send()
        elif which == "recv":
            cp().wait_recv()
        else:
            cp().wait()


HBM_SPEC = pl.BlockSpec(memory_space=pltpu.HBM)
SEM_SPEC = pl.BlockSpec(memory_space=pltpu.SEMAPHORE)
SPLIT_COPY_EFFECT = pltpu.SideEffectType.DATAFLOW_SIDE_EFFECTING


def _owner_send_start(cpbf, after):
    land = lax.empty(cpbf.shape, cpbf.dtype)

    def body(src_ref, land_ref, after_ref, s0, s1, s2, r0, r1, r2, src_thru, land_thru, token):
        x, y, c, kc, sibling, chips = _mesh_pos()
        for jj, (chip, s_sem, r_sem) in enumerate(zip(chips, (s0, s1, s2), (r0, r1, r2))):
            pltpu.make_async_remote_copy(
                src_ref=src_ref.at[2 * chip[0] + chip[1]], dst_ref=land_ref.at[kc], send_sem=s_sem, recv_sem=r_sem,
                device_id=(*chip, c), device_id_type=MESH).start()
        token[...] = jnp.zeros_like(token)

    buf = pltpu.HBM(cpbf.shape, cpbf.dtype)
    outs = pl.pallas_call(
        body, name="rs_owner_w_in_start",
        out_shape=(pltpu.SemaphoreType.DMA(()),) * 6 + (buf, buf, SDS((8, 128), F32)),
        in_specs=(HBM_SPEC, HBM_SPEC, ANY_SPEC), out_specs=(SEM_SPEC,) * 6 + (HBM_SPEC, HBM_SPEC, VMEM_SPEC),
        input_output_aliases={0: 6, 1: 7},
        compiler_params=pltpu.CompilerParams(has_side_effects=SPLIT_COPY_EFFECT),
    )(pltpu.with_memory_space_constraint(cpbf, pltpu.HBM), pltpu.with_memory_space_constraint(land, pltpu.HBM), after)
    return outs[:6], outs[6], outs[7], outs[8]


def _owner_send_wait(sems, src_thru, land_thru, after):
    n_after = len(after)

    def body(*refs):
        src_ref, land_ref = refs[0], refs[1]
        sends, recvs = refs[2:5], refs[5:8]
        x, y, c, kc, sibling, chips = _mesh_pos()
        for jj, chip in enumerate(chips):
            slot = 2 * chip[0] + chip[1]
            cp = pltpu.make_async_remote_copy(
                src_ref=src_ref.at[slot], dst_ref=land_ref.at[slot], send_sem=sends[jj], recv_sem=recvs[jj],
                device_id=(*chip, c), device_id_type=MESH)
            cp.wait_send()
            cp.wait_recv()

    buf = pltpu.HBM(land_thru.shape, land_thru.dtype)
    return pl.pallas_call(
        body, name="rs_owner_w_in_wait", out_shape=(buf, buf),
        in_specs=(HBM_SPEC, HBM_SPEC) + (SEM_SPEC,) * 6 + (ANY_SPEC,) * n_after, out_specs=(HBM_SPEC, HBM_SPEC),
        input_output_aliases={0: 0, 1: 1},
        compiler_params=pltpu.CompilerParams(has_side_effects=SPLIT_COPY_EFFECT),
    )(src_thru, land_thru, *sems, *after)[1]


def _comm_call(name, carry):
    n_i, n_o = len(carry.ins), len(carry.outs)

    def body(*refs):
        ins, outs, sems = refs[:n_i], refs[n_i:n_i + n_o], refs[n_i + n_o:]
        _start_all(carry.copies(ins, outs, sems)[0])
        _wait_all(carry.copies(ins, outs, sems)[1])

    return pl.pallas_call(
        body, name=name, out_shape=carry.outs, in_specs=[ANY_SPEC] * n_i, out_specs=[ANY_SPEC] * n_o,
        input_output_aliases=carry.alias, scratch_shapes=carry.sems,
    )(*carry.ins)


def _host_call(body, *, name, grid, in_specs, out_specs, out_shape, args, scratch_shapes=(), carry=None, then=None,
               prefetch=None, aliases=None):
    n_in, n_out, n_scr = len(in_specs), len(out_specs), len(scratch_shapes)
    n_ci = len(carry.ins) if carry else 0
    n_co = len(carry.outs) if carry else 0
    n_cs = len(carry.sems) if carry else 0
    n_pf = 0 if prefetch is None else 1

    def wrapped(*refs):
        pf, refs = refs[:n_pf], refs[n_pf:]
        ins, cins = refs[:n_in], refs[n_in:n_in + n_ci]
        o0 = n_in + n_ci
        outs, couts = refs[o0:o0 + n_out], refs[o0 + n_out:o0 + n_out + n_co]
        s0 = o0 + n_out + n_co
        scr, sems, tsems = refs[s0:s0 + n_scr], refs[s0 + n_scr:s0 + n_scr + n_cs], refs[s0 + n_scr + n_cs:]
        idx = [pl.program_id(a) for a in range(len(grid))]
        first = functools.reduce(jnp.logical_and, [i == 0 for i in idx])
        last = functools.reduce(jnp.logical_and, [i == g - 1 for i, g in zip(idx, grid)])

        if carry:
            @pl.when(first)
            def _():
                _start_all(carry.copies(cins, couts, sems)[0])

        body(*pf, *ins, *outs, *scr)

        if carry:
            @pl.when(last)
            def _():
                _wait_all(carry.copies(cins, couts, sems)[1])
                if then:
                    _start_all(then.copies(couts, couts, tsems)[0])
                    _wait_all(then.copies(couts, couts, tsems)[1])

    all_in = list(in_specs) + [ANY_SPEC] * n_ci
    all_out = list(out_specs) + [ANY_SPEC] * n_co
    all_scr = list(scratch_shapes) + (carry.sems if carry else []) + (then.sems if then else [])
    alias = {n_pf + i: o for i, o in (aliases or {}).items()}
    if carry:
        alias.update({n_pf + n_in + i: n_out + o for i, o in carry.alias.items()})
    kwargs = dict(name=name, out_shape=list(out_shape) + (carry.outs if carry else []), input_output_aliases=alias,
                  compiler_params=_params(len(grid)))
    if prefetch is None:
        call = pl.pallas_call(wrapped, grid=grid, in_specs=all_in, out_specs=all_out, scratch_shapes=all_scr, **kwargs)
        res = call(*args, *(carry.ins if carry else []))
    else:
        call = pl.pallas_call(wrapped, grid_spec=pltpu.PrefetchScalarGridSpec(
            num_scalar_prefetch=1, grid=grid, in_specs=all_in, out_specs=all_out, scratch_shapes=all_scr), **kwargs)
        res = call(prefetch, *args, *(carry.ins if carry else []))
    return list(res[:n_out]), list(res[n_out:])


def _rs_add_halves(pos, grads, recvs, name):
    n = len(grads)

    def body(pos_ref, *refs):
        for i in range(n):
            refs[2 * n + i][...] = (refs[i][...] + refs[n + i][...]).astype(BF16)

    blks = [(1, g.shape[1] // 4, g.shape[2]) for g in grads]
    mine = [pl.BlockSpec(b, lambda k, i, p: (k, p[1] * 2 + i, 0)) for b in blks]
    half = [pl.BlockSpec(b, lambda k, i, p: (k, i, 0)) for b in blks]
    return pl.pallas_call(
        body, name=name,
        grid_spec=pltpu.PrefetchScalarGridSpec(
            num_scalar_prefetch=1, grid=(N_CHIPS, 2), in_specs=mine + half, out_specs=half),
        out_shape=[SDS((N_CHIPS, g.shape[1] // 2, g.shape[2]), BF16) for g in grads],
        compiler_params=_params(2),
    )(pos, *grads, *recvs)


def _rs_sum_owner(pos, grads, recvs, recv3s, name):
    n = len(grads)

    def body(pos_ref, *refs):
        for i in range(n):
            g, s, r1, r2, r3 = (refs[j * n + i] for j in range(5))
            own = g[0] + s[0]
            refs[5 * n + i][...] = ((own + r1[0].astype(F32)) + r2[0].astype(F32)) + r3[0].astype(F32)

    blks = [(1, g.shape[1] // 4, g.shape[2]) for g in grads]
    mine = [pl.BlockSpec(b, lambda i, p: (p[0], p[1] * 2 + i, 0)) for b in blks]

    def slot(d):
        return [pl.BlockSpec(b, lambda i, p: ((p[0] + d) % N_CHIPS, i, 0)) for b in blks]

    return pl.pallas_call(
        body, name=name,
        grid_spec=pltpu.PrefetchScalarGridSpec(
            num_scalar_prefetch=1, grid=(2,), in_specs=mine + slot(0) + slot(1) + slot(2) + slot(3),
            out_specs=[pl.BlockSpec(b[1:], lambda i, p: (p[1] * 2 + i, 0)) for b in blks]),
        out_shape=[SDS(g.shape[1:], F32) for g in grads],
        compiler_params=_params(1),
    )(pos, *grads, *recvs, *recv3s, *recv3s, *recv3s)


def _cast_bf16(pos, arrs):
    n = len(arrs)

    def body(pos_ref, *refs):
        for i in range(n):
            refs[n + i][0] = refs[i][...].astype(BF16)

    return pl.pallas_call(
        body, name="cast_bf16",
        grid_spec=pltpu.PrefetchScalarGridSpec(
            num_scalar_prefetch=1, grid=(2,),
            in_specs=[pl.BlockSpec((a.shape[0] // 2, a.shape[1]), lambda i, p: (i, 0)) for a in arrs],
            out_specs=[pl.BlockSpec((1, a.shape[0] // 2, a.shape[1]), lambda i, p: (p[0], i, 0)) for a in arrs]),
        out_shape=[SDS((N_CHIPS,) + a.shape, BF16) for a in arrs],
        compiler_params=_params(1),
    )(pos, *arrs)


def _adamw_vals(w, g, m, v):
    m2 = ADAM_B1 * m + (1.0 - ADAM_B1) * g
    v2 = ADAM_B2 * v + (1.0 - ADAM_B2) * (g * g)
    m_hat = m2 / (1.0 - ADAM_B1 ** ADAM_STEP)
    v_hat = v2 / (1.0 - ADAM_B2 ** ADAM_STEP)
    delta = -ADAM_LR * (m_hat / (jnp.sqrt(v_hat) + ADAM_EPS) + ADAM_WD * w)
    return delta, m2, v2


def _adamw_big(w, g, m, v, name):
    rows, cols = w.shape
    rb = rows // 4

    def body(w_ref, g_ref, m_ref, v_ref, d_ref, m2_ref, v2_ref):
        d, m2, v2 = _adamw_vals(w_ref[...], g_ref[...], m_ref[...], v_ref[...])
        d_ref[...] = d
        m2_ref[...] = m2
        v2_ref[...] = v2

    spec = pl.BlockSpec((rb, cols), lambda i: (i, 0))
    return pl.pallas_call(
        body, name=name, grid=(4,), in_specs=[spec] * 4, out_specs=[spec] * 3,
        out_shape=[SDS(w.shape, F32)] * 3, compiler_params=_params(1),
    )(w, g, m, v)


ADAMW_GROUP_STEPS = 8


def _adamw_group(ws, gs, ms, vs, name, carry=None):
    n = len(ws)

    def body(*refs):
        for i in range(n):
            d, m2, v2 = _adamw_vals(refs[i][...], refs[n + i][...], refs[2 * n + i][...], refs[3 * n + i][...])
            refs[4 * n + i][...] = d
            refs[5 * n + i][...] = m2
            refs[6 * n + i][...] = v2

    specs = [pl.BlockSpec((w.shape[0] // ADAMW_GROUP_STEPS, w.shape[1]), lambda i: (i, 0)) for w in ws]
    shapes = [SDS(w.shape, F32) for w in ws]
    outs, carried = _host_call(
        body, name=name, grid=(ADAMW_GROUP_STEPS,), in_specs=specs * 4, out_specs=specs * 3, out_shape=shapes * 3,
        args=(*ws, *gs, *ms, *vs), carry=carry)
    return (outs[:n], outs[n:2 * n], outs[2 * n:]), carried


def _adamw_small(ws, gs, ms, vs):
    n = len(ws)

    def body(*refs):
        for i in range(n):
            d, m2, v2 = _adamw_vals(refs[i][...], refs[n + i][...], refs[2 * n + i][...], refs[3 * n + i][...])
            refs[4 * n + i][...] = d
            refs[5 * n + i][...] = m2
            refs[6 * n + i][...] = v2

    shapes = [SDS(w.shape, F32) for w in ws]
    outs = pl.pallas_call(
        body, name="adamw_small", out_shape=shapes * 3,
        in_specs=[VMEM_SPEC] * (4 * n), out_specs=[VMEM_SPEC] * (3 * n),
    )(*ws, *gs, *ms, *vs)
    return outs[:n], outs[n:2 * n], outs[2 * n:]


def _cond_and_mod(pay, w_mod_s, b_mod_s):
    m_per, n_cols = pay.shape[0], w_mod_s.shape[1]
    rows = N_DEV * m_per

    def body(pay_ref, w_ref, b_ref, cond_ref, modg_ref, mod_s, sems_a, semr_a, loc_a, sems_b, semr_b, loc_b):
        _gather8_body(pay_ref, cond_ref, sems_a, semr_a, loc_a, m_per)
        cc = cond_ref[...]
        mod_s[...] = _dot_f32(cc * _sig(cc), w_ref[...]) + b_ref[...]
        _gather8_body(mod_s, modg_ref, sems_b, semr_b, loc_b, rows)

    dma7 = pltpu.SemaphoreType.DMA((7,))
    return pl.pallas_call(
        body, name="cond_and_mod",
        out_shape=[SDS((rows, pay.shape[1]), F32), SDS((N_DEV * rows, n_cols), F32)],
        in_specs=[VMEM_SPEC] * 3, out_specs=[VMEM_SPEC] * 2,
        scratch_shapes=[pltpu.VMEM((rows, n_cols), F32), dma7, dma7, pltpu.SemaphoreType.DMA,
                        dma7, dma7, pltpu.SemaphoreType.DMA],
        compiler_params=pltpu.CompilerParams(vmem_limit_bytes=VMEM_LIMIT),
    )(pay, w_mod_s, b_mod_s)


def _mod_bwd(cond64, dmod64, dmod64_my, w_mod_s, c_ctx):
    def body(c_ref, g_ref, gm_ref, w_ref, cc_ref, gw_ref, gb_ref, gcc_ref):
        cc = c_ref[...]
        act = cc * _sig(cc)
        gm = gm_ref[...]
        gw_ref[...] = _dot_f32(act, gm, (((0,), (0,)), ((), ())))
        gb_ref[...] = _colsum(g_ref[...])
        dact = _dot_f32(gm, w_ref[...], (((1,), (1,)), ((), ())))
        tot = dact[4:5, :]
        for dev in range(1, N_DEV):
            tot = tot + dact[8 * dev + 4:8 * dev + 5, :]
        c0 = cc_ref[...]
        s0 = _sig(c0)
        gcc_ref[...] = tot * (s0 * (1.0 + c0 * (1.0 - s0)))

    return pl.pallas_call(
        body, name="mod_bwd",
        out_shape=[SDS(w_mod_s.shape, F32), SDS((1, dmod64.shape[1]), F32), SDS((1, D_MODEL), F32)],
        in_specs=[VMEM_SPEC] * 5, out_specs=[VMEM_SPEC] * 3,
        compiler_params=pltpu.CompilerParams(vmem_limit_bytes=VMEM_LIMIT),
    )(cond64, dmod64, dmod64_my, w_mod_s, c_ctx)


SHARD_COLS = IN_COLS // N_CHIPS


def _in_fwd_own(pos, x, ctx, g_mix, mod_a, w_own, carry=None, then=None):
    bs, seq, _ = x.shape
    nb = seq // ROW_BLOCK + 1

    def body(pos_ref, x_ref, ctx_ref, g_ref, mod_ref, w_ref, p_ref, h_ref, w_bf):
        b, j = pl.program_id(0), pl.program_id(1)

        @pl.when((b == 0) & (j == 0))
        def _():
            w_bf[...] = w_ref[...].astype(BF16)

        is_ctx = j == 0
        xin = jnp.where(is_ctx, ctx_ref[0], x_ref[0])
        shift = jnp.where(is_ctx, mod_ref[0, 2:3, :], mod_ref[0, 0:1, :])
        scale = jnp.where(is_ctx, mod_ref[0, 3:4, :], mod_ref[0, 1:2, :])
        xn, _ = _rms(xin)
        hb = ((xn * g_ref[...]) * (1.0 + scale) + shift).astype(BF16)
        h_ref[0] = hb
        p_ref[0] = jnp.dot(hb, w_bf[...], preferred_element_type=F32)

    return _host_call(
        body, name="in_fwd_own", grid=(bs, nb), prefetch=pos,
        in_specs=[pl.BlockSpec((1, ROW_BLOCK, D_MODEL), lambda b, j, p: (b, jnp.maximum(j - 1, 0), 0)),
                  pl.BlockSpec((1, ROW_BLOCK, D_MODEL), lambda b, j, p: (b, 0, 0)),
                  pl.BlockSpec((1, D_MODEL), lambda b, j, p: (0, 0)),
                  pl.BlockSpec((1, 8, D_MODEL), lambda b, j, p: (b, 0, 0)),
                  pl.BlockSpec((D_MODEL, SHARD_COLS), lambda b, j, p: (0, 0))],
        out_specs=[pl.BlockSpec((1, ROW_BLOCK, SHARD_COLS), lambda b, j, p: (b, j, p[0])),
                   pl.BlockSpec((1, ROW_BLOCK, D_MODEL), lambda b, j, p: (b, j, 0))],
        out_shape=[SDS((bs, nb * ROW_BLOCK, IN_COLS), F32), SDS((bs, nb * ROW_BLOCK, D_MODEL), BF16)],
        scratch_shapes=[pltpu.VMEM((D_MODEL, SHARD_COLS), BF16)],
        args=(x, ctx, g_mix, mod_a, w_own), carry=carry, then=then)


def _in_fwd_rest(pos, h_all, w_in_g, p, carry=None):
    bs, rows, _ = h_all.shape
    rows_all = bs * rows
    tile = next(m * ROW_BLOCK for m in (9, 3, 1) if rows_all % (m * ROW_BLOCK) == 0)

    def body(pos_ref, h_ref, w_ref, p_in_ref, p_ref):
        p_ref[...] = jnp.dot(h_ref[...], w_ref[0], preferred_element_type=F32)

    shard = lambda n, p: (p[0] + 1 + n) % N_CHIPS
    (p2,), carried = _host_call(
        body, name="in_fwd_rest", grid=(N_CHIPS - 1, rows_all // tile), prefetch=pos,
        in_specs=[pl.BlockSpec((tile, D_MODEL), lambda n, t, p: (t, 0)),
                  pl.BlockSpec((1, D_MODEL, SHARD_COLS), lambda n, t, p: (shard(n, p), 0, 0)),
                  ANY_SPEC],
        out_specs=[pl.BlockSpec((tile, SHARD_COLS), lambda n, t, p: (t, shard(n, p)))],
        out_shape=[SDS((rows_all, IN_COLS), F32)], aliases={2: 0},
        args=(h_all.reshape(rows_all, D_MODEL), w_in_g, p.reshape(rows_all, IN_COLS)), carry=carry)
    return p2.reshape(bs, rows, IN_COLS), carried


def _in_bwd(x, ctx, dx1, g_mix, mod_a, w_in, df_f, df_b, dv_f, dv_b, dq_f, dq_b, dpc, carry=None):
    bs, seq, _ = x.shape
    nb = seq // ROW_BLOCK + 1

    def body(x_ref, ctx_ref, dx1_ref, g_ref, mod_ref, w_ref, dff_ref, dfb_ref, dvf_ref, dvb_ref, dqf_ref, dqb_ref,
             dpc_ref, gx_ref, dp_ref, dg_ref, dmod_ref):
        b, j = pl.program_id(0), pl.program_id(1)
        is_ctx = j == 0

        @pl.when((b == 0) & (j == 0))
        def _():
            dg_ref[...] = jnp.zeros_like(dg_ref)

        @pl.when(j == 0)
        def _():
            dmod_ref[...] = jnp.zeros_like(dmod_ref)

        di = (dvf_ref[0] + dvb_ref[0]).astype(BF16)
        dq = (dqf_ref[0] + dqb_ref[0]).astype(BF16)
        dp = jnp.concatenate([dff_ref[0], dfb_ref[0], di, dq, dpc_ref[0]], axis=1)
        dp_ref[0] = dp
        dh = lax.dot_general(dp, w_ref[...], (((1,), (1,)), ((), ())), preferred_element_type=F32)
        xin = jnp.where(is_ctx, ctx_ref[0], x_ref[0])
        scale = jnp.where(is_ctx, mod_ref[0, 3:4, :], mod_ref[0, 1:2, :])
        xn, r = _rms(xin)
        g = g_ref[...]
        hn = xn * g
        d_shift = _colsum(dh)
        d_scale = _colsum(dh * hn)
        dhn = dh * (1.0 + scale)
        dg_ref[...] += _colsum(dhn * xn)
        dx = _rms_bwd(dhn * g, xn, r)

        @pl.when(is_ctx)
        def _():
            dmod_ref[0, 2:3, :] += d_shift
            dmod_ref[0, 3:4, :] += d_scale

        @pl.when(jnp.logical_not(is_ctx))
        def _():
            dmod_ref[0, 0:1, :] += d_shift
            dmod_ref[0, 1:2, :] += d_scale
            gx_ref[0] = dx + dx1_ref[0]

    def rows(w):
        return pl.BlockSpec((1, ROW_BLOCK, w), lambda b, j: (b, j, 0))

    lat = pl.BlockSpec((1, ROW_BLOCK, D_MODEL), lambda b, j: (b, jnp.maximum(j - 1, 0), 0))
    return _host_call(
        body, name="in_bwd", grid=(bs, nb),
        in_specs=[lat, pl.BlockSpec((1, ROW_BLOCK, D_MODEL), lambda b, j: (b, 0, 0)), lat,
                  pl.BlockSpec((1, D_MODEL), lambda b, j: (0, 0)),
                  pl.BlockSpec((1, 8, D_MODEL), lambda b, j: (b, 0, 0)),
                  pl.BlockSpec((D_MODEL, IN_COLS), lambda b, j: (0, 0)),
                  rows(KW), rows(KW), rows(KW), rows(KW), rows(KW), rows(KW), rows(7 * KW)],
        out_specs=[lat, rows(IN_COLS), pl.BlockSpec((1, D_MODEL), lambda b, j: (0, 0)),
                   pl.BlockSpec((1, 8, D_MODEL), lambda b, j: (b, 0, 0))],
        out_shape=[SDS(x.shape, F32), SDS((bs, nb * ROW_BLOCK, IN_COLS), BF16), SDS((1, D_MODEL), F32),
                   SDS((bs, 8, D_MODEL), F32)],
        args=(x, ctx, dx1, g_mix, mod_a, w_in, df_f, df_b, dv_f, dv_b, dq_f, dq_b, dpc), carry=carry)


def _lower_bound(lbg_ref, direction):
    return _sig(lbg_ref[0, direction:direction + 1, :] - lbg_ref[1, direction:direction + 1, :])


def _block_tri(upper):
    t = np.arange(ROW_BLOCK)[:, None]
    s = np.arange(ROW_BLOCK)[None, :]
    same = (t // HGRN_CHUNK) == (s // HGRN_CHUNK)
    return jnp.asarray(same & ((s >= t) if upper else (s <= t)), dtype=BF16)


TRI_SPEC = pl.BlockSpec((ROW_BLOCK, ROW_BLOCK), lambda b, j: (0, 0))


def _tri_matmul_f32(tri, g):
    g0 = g.astype(BF16)
    r1 = g - g0.astype(F32)
    g1 = r1.astype(BF16)
    g2 = (r1 - g1.astype(F32)).astype(BF16)
    return (jnp.dot(tri, g2, preferred_element_type=F32) + jnp.dot(tri, g1, preferred_element_type=F32)) \
        + jnp.dot(tri, g0, preferred_element_type=F32)


def _chunk_rows(rows):
    return jnp.concatenate([jnp.broadcast_to(r, (HGRN_CHUNK, r.shape[1])) for r in rows], axis=0)


def _block_gates(fl, q, lb, tri, upper):
    t = {}
    t["sg"] = _sig(fl)
    t["f"] = lb + (1.0 - lb) * t["sg"]
    k = 1.0 - t["f"]
    bcum = _tri_matmul_f32(tri, jnp.log(t["f"]))
    ends = [bcum[ci * HGRN_CHUNK:ci * HGRN_CHUNK + 1] if upper else bcum[(ci + 1) * HGRN_CHUNK - 1:(ci + 1) * HGRN_CHUNK]
            for ci in range(fl.shape[0] // HGRN_CHUNK)]
    mid = _chunk_rows([0.5 * r for r in ends])
    t["dec"] = [jnp.exp(r) for r in ends]
    t["e1"] = jnp.exp(bcum - mid)
    t["e2"] = jnp.exp(mid - bcum)
    t["eh"] = _chunk_rows([jnp.exp(0.5 * r) for r in ends])
    t["qi"] = q * t["e1"]
    t["ki"] = k * t["e2"]
    t["kd"] = t["ki"] * t["eh"]
    t["qe"] = t["qi"] * t["eh"]
    return t


def _hgrn_block_order(direction, nb):
    if direction == 0:
        return lambda j: j
    return lambda j: jnp.where(j == 0, 0, nb - j)


def _hgrn_fwd(p, lbg, carry=None, then=None):
    bs, rows, _ = p.shape
    nb = rows // ROW_BLOCK
    ncb = ROW_BLOCK // HGRN_CHUNK
    orders = [_hgrn_block_order(d, nb) for d in (0, 1)]
    dirs = (0, 1)

    def body(f0_ref, i0_ref, q0_ref, f1_ref, i1_ref, q1_ref, lbg_ref, tri0_ref, tri1_ref,
             o0_ref, s0_ref, o1_ref, s1_ref, st):
        @pl.when(pl.program_id(1) == 0)
        def _():
            st[...] = jnp.zeros_like(st)

        f_refs, i_refs, q_refs = (f0_ref, f1_ref), (i0_ref, i1_ref), (q0_ref, q1_ref)
        tri_refs, o_refs, s_refs = (tri0_ref, tri1_ref), (o0_ref, o1_ref), (s0_ref, s1_ref)
        chunk = lambda a, ci, h: a[ci * HGRN_CHUNK:(ci + 1) * HGRN_CHUNK, h * HEAD_DIM:(h + 1) * HEAD_DIM]
        masks = [_tri(HGRN_CHUNK, d == 1) for d in dirs]
        t = [_block_gates(f_refs[d][0], q_refs[d][0], _lower_bound(lbg_ref, d), tri_refs[d][...], d == 1) for d in dirs]
        v = [i_refs[d][0] for d in dirs]
        intra = [[[None] * N_HEADS for _ in range(ncb)] for _ in dirs]
        ds_loc = [[[None] * N_HEADS for _ in range(ncb)] for _ in dirs]
        for ci in range(ncb):
            for h in range(N_HEADS):
                for d in dirs:
                    a = jnp.where(masks[d], _dot_nt(chunk(t[d]["qi"], ci, h), chunk(t[d]["ki"], ci, h)), 0.0)
                    intra[d][ci][h] = _dot(a, chunk(v[d], ci, h))
                    ds_loc[d][ci][h] = _dot_tn(chunk(v[d], ci, h), chunk(t[d]["kd"], ci, h))
        for h in range(N_HEADS):
            ls = slice(h * HEAD_DIM, (h + 1) * HEAD_DIM)
            s = [st[d, h] for d in dirs]
            for step in range(ncb):
                for d in dirs:
                    ci = ncb - 1 - step if d == 1 else step
                    s_refs[d][0, 0, ci, h] = s[d]
                    o_refs[d][0, ci * HGRN_CHUNK:(ci + 1) * HGRN_CHUNK, ls] = (
                        intra[d][ci][h] + _dot_nt(chunk(t[d]["qe"], ci, h), s[d]))
                    s[d] = s[d] * t[d]["dec"][ci][:, ls] + ds_loc[d][ci][h]
            for d in dirs:
                st[d, h] = s[d]

    def col(d, cb):
        return pl.BlockSpec((1, ROW_BLOCK, KW), lambda b, j: (b, orders[d](j), cb))

    def outs(d):
        return [pl.BlockSpec((1, ROW_BLOCK, KW), lambda b, j: (b, orders[d](j), 0)),
                pl.BlockSpec((1, 1, ncb, N_HEADS, HEAD_DIM, HEAD_DIM), lambda b, j: (b, orders[d](j), 0, 0, 0, 0))]

    shapes = [SDS((bs, rows, KW), F32), SDS((bs, nb, ncb, N_HEADS, HEAD_DIM, HEAD_DIM), F32)]
    return _host_call(
        body, name="hgrn_fwd", grid=(bs, nb),
        in_specs=[col(0, 0), col(0, 2), col(0, 3), col(1, 1), col(1, 2), col(1, 3),
                  pl.BlockSpec((2, 2, KW), lambda b, j: (0, 0, 0)), TRI_SPEC, TRI_SPEC],
        out_specs=outs(0) + outs(1), out_shape=shapes * 2,
        scratch_shapes=[pltpu.VMEM((2, N_HEADS, HEAD_DIM, HEAD_DIM), F32)],
        args=(p, p, p, p, p, p, lbg, _block_tri(False), _block_tri(True)), carry=carry, then=then)


def _hgrn_bwd_pair(p, lbg, s_saved, do_raw, carry=None):
    bs, rows, _ = p.shape
    nb = rows // ROW_BLOCK
    ncb = ROW_BLOCK // HGRN_CHUNK
    dirs = (0, 1)
    fwd_orders = [_hgrn_block_order(d, nb) for d in dirs]
    orders = [lambda j, d=d: fwd_orders[d](nb - 1 - j) for d in dirs]
    pairs = [(ci, h) for ci in range(ncb) for h in range(N_HEADS)]

    def body(f0_ref, i0_ref, q0_ref, s0_ref, do0_ref, f1_ref, i1_ref, q1_ref, s1_ref, do1_ref,
             lbg_ref, tril_ref, triu_ref,
             df0_ref, dq0_ref, dv0_ref, dlb0_ref, df1_ref, dq1_ref, dv1_ref, dlb1_ref, dst, acc):
        b, j = pl.program_id(0), pl.program_id(1)
        f_refs, i_refs, q_refs = (f0_ref, f1_ref), (i0_ref, i1_ref), (q0_ref, q1_ref)
        s_refs, do_refs = (s0_ref, s1_ref), (do0_ref, do1_ref)
        df_refs, dq_refs, dv_refs, dlb_refs = (df0_ref, df1_ref), (dq0_ref, dq1_ref), (dv0_ref, dv1_ref), (dlb0_ref, dlb1_ref)
        tri_refs, trit_refs = (tril_ref, triu_ref), (triu_ref, tril_ref)

        @pl.when((b == 0) & (j == 0))
        def _():
            dlb0_ref[...] = jnp.zeros_like(dlb0_ref)
            dlb1_ref[...] = jnp.zeros_like(dlb1_ref)

        @pl.when(j == 0)
        def _():
            dst[...] = jnp.zeros_like(dst)

        chunk = lambda a, ci, h: a[ci * HGRN_CHUNK:(ci + 1) * HGRN_CHUNK, h * HEAD_DIM:(h + 1) * HEAD_DIM]
        rows_of = lambda ci: slice(ci * HGRN_CHUNK, (ci + 1) * HGRN_CHUNK)
        lanes_of = lambda h: slice(h * HEAD_DIM, (h + 1) * HEAD_DIM)
        grid3 = lambda: [[[None] * N_HEADS for _ in range(ncb)] for _ in dirs]
        lbs = [_lower_bound(lbg_ref, d) for d in dirs]
        masks = [_tri(HGRN_CHUNK, d == 1) for d in dirs]
        masks_t = [_tri(HGRN_CHUNK, d != 1) for d in dirs]
        t = [_block_gates(f_refs[d][0], q_refs[d][0], lbs[d], tri_refs[d][...], d == 1) for d in dirs]
        v = [i_refs[d][0] for d in dirs]
        do = [do_refs[d][0] for d in dirs]
        a_t, da, da_t, dv_in, ds_loc = (grid3() for _ in range(5))
        for ci, h in pairs:
            for d in dirs:
                a_t[d][ci][h] = _dot_nt(chunk(t[d]["ki"], ci, h), chunk(t[d]["qi"], ci, h))
        for ci, h in pairs:
            for d in dirs:
                da[d][ci][h] = _dot_nt(chunk(do[d], ci, h), chunk(v[d], ci, h))
        for ci, h in pairs:
            for d in dirs:
                da_t[d][ci][h] = _dot_nt(chunk(v[d], ci, h), chunk(do[d], ci, h))
        for ci, h in pairs:
            for d in dirs:
                acc[d, 3, rows_of(ci), lanes_of(h)] = _dot(chunk(do[d], ci, h), s_refs[d][0, 0, ci, h])
        for ci, h in pairs:
            for d in dirs:
                ds_loc[d][ci][h] = _dot_tn(chunk(do[d], ci, h), chunk(t[d]["qe"], ci, h))
        for ci, h in pairs:
            for d in dirs:
                acc[d, 0, rows_of(ci), lanes_of(h)] = _dot(jnp.where(masks[d], da[d][ci][h], 0.0),
                                                           chunk(t[d]["ki"], ci, h))
        for ci, h in pairs:
            for d in dirs:
                acc[d, 1, rows_of(ci), lanes_of(h)] = _dot(jnp.where(masks_t[d], da_t[d][ci][h], 0.0),
                                                           chunk(t[d]["qi"], ci, h))
        for ci, h in pairs:
            for d in dirs:
                dv_in[d][ci][h] = _dot(jnp.where(masks_t[d], a_t[d][ci][h], 0.0), chunk(do[d], ci, h))
        ddec = grid3()
        for h in range(N_HEADS):
            ls = lanes_of(h)
            ds = [dst[d, h] for d in dirs]
            for step in range(ncb):
                for d in dirs:
                    ci = step if d == 1 else ncb - 1 - step
                    acc[d, 2, rows_of(ci), ls] = _dot(chunk(v[d], ci, h), ds[d])
                    acc[d, 4, rows_of(ci), ls] = dv_in[d][ci][h] + _dot_nt(chunk(t[d]["kd"], ci, h), ds[d])
                    ddec[d][ci][h] = _colsum(ds[d] * s_refs[d][0, 0, ci, h])
                    ds[d] = ds[d] * t[d]["dec"][ci][:, ls] + ds_loc[d][ci][h]
            for d in dirs:
                dst[d, h] = ds[d]
        for d in dirs:
            td = t[d]
            dqi, dki, dkd, dqe = (acc[d, i] for i in range(4))
            dq_refs[d][0] = td["e1"] * (dqi + dqe * td["eh"])
            dv_refs[d][0] = acc[d, 4]
            dk = td["e2"] * (dki + dkd * td["eh"])
            dkd_kd = dkd * td["kd"]
            db = dqi * td["qi"] - dki * td["ki"] - dkd_kd + dqe * td["qe"]
            dbl = [_colsum(dkd_kd[rows_of(ci)]) + jnp.concatenate(ddec[d][ci], axis=1) * td["dec"][ci]
                   for ci in range(ncb)]
            dg = _tri_matmul_f32(trit_refs[d][...], db) + _chunk_rows(dbl)
            df = dg / td["f"] - dk
            sg = td["sg"]
            dlb_refs[d][...] += _colsum(df * (1.0 - sg))
            df_refs[d][0] = (df * (1.0 - lbs[d]) * sg * (1.0 - sg)).astype(BF16)

    def ins(d):
        col = lambda cb: pl.BlockSpec((1, ROW_BLOCK, KW), lambda b, j: (b, orders[d](j), cb))
        return [col(d), col(2), col(3),
                pl.BlockSpec((1, 1, ncb, N_HEADS, HEAD_DIM, HEAD_DIM), lambda b, j: (b, orders[d](j), 0, 0, 0, 0)),
                pl.BlockSpec((1, ROW_BLOCK, KW), lambda b, j: (b, orders[d](j), 0))]

    def outs(d):
        row = pl.BlockSpec((1, ROW_BLOCK, KW), lambda b, j: (b, orders[d](j), 0))
        return [row, row, row, pl.BlockSpec((1, KW), lambda b, j: (0, 0))]

    shapes = [SDS((bs, rows, KW), BF16), SDS((bs, rows, KW), F32), SDS((bs, rows, KW), F32), SDS((1, KW), F32)]
    return _host_call(
        body, name="hgrn_bwd", grid=(bs, nb),
        in_specs=ins(0) + ins(1) + [pl.BlockSpec((2, 2, KW), lambda b, j: (0, 0, 0)), TRI_SPEC, TRI_SPEC],
        out_specs=outs(0) + outs(1), out_shape=shapes * 2,
        scratch_shapes=[pltpu.VMEM((2, N_HEADS, HEAD_DIM, HEAD_DIM), F32), pltpu.VMEM((2, 5, ROW_BLOCK, KW), F32)],
        args=(p, p, p, s_saved[0], do_raw, p, p, p, s_saved[1], do_raw, lbg, _block_tri(False), _block_tri(True)),
        carry=carry)


PROJ_COLS = D_MODEL // N_CHIPS


PROJ_SPEC = pl.BlockSpec((N_CHIPS, KW, PROJ_COLS), lambda b, j: (0, 0, 0))


def _dot_shards(a, w_ref):
    return jnp.concatenate([_dot(a, w_ref[s]) for s in range(N_CHIPS)], axis=1)


def _dot_nt_shards(d, w_ref):
    parts = [_dot_nt(d[:, s * PROJ_COLS:(s + 1) * PROJ_COLS], w_ref[s]) for s in range(N_CHIPS)]
    return (parts[0] + parts[1]) + (parts[2] + parts[3])


def _mix_values(og, u, v, ga, gb, o_raw, gna, lng, lnb, ws_ref, bst, wpa, wpb, wo):
    t = {}
    sog = _sig(og)
    t["sog"], t["silu_og"] = sog, og * sog
    xh_l, r_l = [], []
    for h in range(N_HEADS):
        xh, r = _rms(o_raw[:, h * HEAD_DIM:(h + 1) * HEAD_DIM])
        xh_l.append(xh)
        r_l.append(r)
    t["xh"], t["r"] = jnp.concatenate(xh_l, axis=1), r_l
    gna4 = jnp.concatenate([gna] * N_HEADS, axis=1)
    t["gna4"] = gna4
    t["o_n"] = t["xh"] * gna4
    t["o_a"] = t["o_n"] * t["silu_og"]
    t["ya"] = _dot_shards(t["o_a"], wpa)
    t["gu"], t["tu"] = _gelu(u)
    gv, t["tv"] = _gelu(v)
    mu = jnp.mean(gv, axis=-1, keepdims=True)
    cen = gv - mu
    t["rstd"] = lax.rsqrt(jnp.mean(cen * cen, axis=-1, keepdims=True) + EPS)
    t["xhat"] = cen * t["rstd"]
    vn = t["xhat"] * lng + lnb
    t["vn"] = vn
    chunks = []
    for n in range(ROW_BLOCK // SGU_CHUNK):
        rs = slice(n * SGU_CHUNK, (n + 1) * SGU_CHUNK)
        groups = []
        for g in range(N_HEADS):
            ls = slice(g * HEAD_DIM, (g + 1) * HEAD_DIM)
            groups.append(_dot(ws_ref[g], vn[rs, ls]) + bst[:, g:g + 1])
        chunks.append(jnp.concatenate(groups, axis=1))
    t["mixed"] = jnp.concatenate(chunks, axis=0)
    t["o_bm"] = t["gu"] * t["mixed"]
    t["yb"] = _dot_shards(t["o_bm"], wpb)
    t["sa"], t["sb"] = _sig(ga), _sig(gb)
    t["merged"] = t["sa"] * t["ya"] + t["sb"] * t["yb"]
    t["mix"] = _dot(t["merged"], wo)
    return t


def _mix_in_specs(row_of):
    def col(cb):
        return pl.BlockSpec((1, ROW_BLOCK, KW), lambda b, j: (b, row_of(j), cb))
    return [col(cb) for cb in range(4, 11)]


def _mix_param_specs():
    full2 = lambda r, c: pl.BlockSpec((r, c), lambda b, j: (0, 0))
    return [full2(1, HEAD_DIM), full2(1, KW), full2(1, KW),
            pl.BlockSpec((N_HEADS, SGU_CHUNK, SGU_CHUNK), lambda b, j: (0, 0, 0)),
            full2(SGU_CHUNK, N_HEADS), PROJ_SPEC, PROJ_SPEC, full2(D_MODEL, D_MODEL)]


def _mix_fwd(p, o_f, o_b, x, mod_c, gna, lng, lnb, w_s, bst, wpa, wpb, wo):
    bs, seq, _ = x.shape
    nbl = seq // ROW_BLOCK

    def body(og_r, u_r, v_r, ga0_r, ga1_r, gb0_r, gb1_r, of_r, ob_r, x_r, mod_r,
             gna_r, lng_r, lnb_r, ws_r, bst_r, wpa_r, wpb_r, wo_r, x1_r):
        ga = jnp.concatenate([ga0_r[0], ga1_r[0]], axis=1)
        gb = jnp.concatenate([gb0_r[0], gb1_r[0]], axis=1)
        t = _mix_values(og_r[0], u_r[0], v_r[0], ga, gb, of_r[0] + ob_r[0], gna_r[...], lng_r[...], lnb_r[...],
                        ws_r, bst_r[...], wpa_r, wpb_r, wo_r[...])
        x1_r[0] = x_r[0] + mod_r[0, 0:1, :] * t["mix"]

    row = lambda w: pl.BlockSpec((1, ROW_BLOCK, w), lambda b, j: (b, j + 1, 0))
    lat = pl.BlockSpec((1, ROW_BLOCK, D_MODEL), lambda b, j: (b, j, 0))
    return pl.pallas_call(
        body, name="mix_fwd", grid=(bs, nbl),
        in_specs=_mix_in_specs(lambda j: j + 1) + [row(KW), row(KW), lat,
                                                    pl.BlockSpec((1, 8, D_MODEL), lambda b, j: (b, 0, 0))]
        + _mix_param_specs(),
        out_specs=lat, out_shape=SDS(x.shape, F32), compiler_params=_params(2),
    )(p, p, p, p, p, p, p, o_f, o_b, x, mod_c, gna, lng, lnb, w_s, bst, wpa, wpb, wo)


def _mix_bwd(p, o_f, o_b, dx1, mod_c, gna, lng, lnb, w_s, w_s_t, bst, wpa, wpb, wo, carry=None):
    bs, rows, _ = p.shape
    nb = rows // ROW_BLOCK

    def body(og_r, u_r, v_r, ga0_r, ga1_r, gb0_r, gb1_r, of_r, ob_r, dx1_r, mod_r,
             gna_r, lng_r, lnb_r, ws_r, bst_r, wpa_r, wpb_r, wo_r, wst_r,
             dor_r, dpc_r, dwpa_r, dwpb_r, dwo_r, dgna_r, dlng_r, dlnb_r, dws_r, dbst_r, dmod_r):
        b, j = pl.program_id(0), pl.program_id(1)

        @pl.when((b == 0) & (j == 0))
        def _():
            for r in (dwpa_r, dwpb_r, dwo_r, dgna_r, dlng_r, dlnb_r, dws_r, dbst_r):
                r[...] = jnp.zeros_like(r)

        @pl.when(j == 0)
        def _():
            dmod_r[...] = jnp.zeros_like(dmod_r)
            dor_r[...] = jnp.zeros_like(dor_r)
            dpc_r[...] = jnp.zeros_like(dpc_r)

        @pl.when(j > 0)
        def _():
            og, u, v = og_r[0], u_r[0], v_r[0]
            ga = jnp.concatenate([ga0_r[0], ga1_r[0]], axis=1)
            gb = jnp.concatenate([gb0_r[0], gb1_r[0]], axis=1)
            gna, lng = gna_r[...], lng_r[...]
            wpa, wpb, wo = wpa_r, wpb_r, wo_r[...]
            dx1 = dx1_r[0]
            dmix = mod_r[0, 0:1, :] * dx1
            dmerged = _dot_nt(dmix, wo)
            t = _mix_values(og, u, v, ga, gb, of_r[0] + ob_r[0], gna, lng, lnb_r[...],
                            ws_r, bst_r[...], wpa, wpb, wo)
            dmod_r[0, 0:1, :] += _colsum(dx1 * t["mix"])
            dwo_r[...] += _dot_tn(t["merged"], dmix)
            sa, sb = t["sa"], t["sb"]
            dya, dyb = sa * dmerged, sb * dmerged
            dga = dmerged * t["ya"] * sa * (1.0 - sa)
            dgb = dmerged * t["yb"] * sb * (1.0 - sb)
            do_a = _dot_nt_shards(dya, wpa)
            do_bm = _dot_nt_shards(dyb, wpb)
            for s in range(N_CHIPS):
                cs = slice(s * PROJ_COLS, (s + 1) * PROJ_COLS)
                dwpa_r[s] += _dot_tn(t["o_a"], dya[:, cs])
                dwpb_r[s] += _dot_tn(t["o_bm"], dyb[:, cs])
            sog = t["sog"]
            dog = do_a * t["o_n"] * (sog * (1.0 + og * (1.0 - sog)))
            do_n = do_a * t["silu_og"]
            dxh = do_n * t["gna4"]
            prod = do_n * t["xh"]
            dgna = jnp.zeros((1, HEAD_DIM), F32)
            dor_l = []
            for h in range(N_HEADS):
                ls = slice(h * HEAD_DIM, (h + 1) * HEAD_DIM)
                dgna = dgna + _colsum(prod[:, ls])
                dor_l.append(_rms_bwd(dxh[:, ls], t["xh"][:, ls], t["r"][h]))
            dgna_r[...] += dgna
            dor_r[0] = jnp.concatenate(dor_l, axis=1)
            du = do_bm * t["mixed"] * _dgelu(u, t["tu"])
            dmixed = do_bm * t["gu"]
            vn = t["vn"]
            dvn_chunks = []
            for n in range(ROW_BLOCK // SGU_CHUNK):
                rs = slice(n * SGU_CHUNK, (n + 1) * SGU_CHUNK)
                groups = []
                for g in range(N_HEADS):
                    ls = slice(g * HEAD_DIM, (g + 1) * HEAD_DIM)
                    dm = dmixed[rs, ls]
                    dws_r[g] += _dot_nt(dm, vn[rs, ls])
                    dbst_r[:, g:g + 1] += jnp.sum(dm, axis=1, keepdims=True)
                    groups.append(_dot(wst_r[g], dm))
                dvn_chunks.append(jnp.concatenate(groups, axis=1))
            dvn = jnp.concatenate(dvn_chunks, axis=0)
            xhat = t["xhat"]
            dlng_r[...] += _colsum(dvn * xhat)
            dlnb_r[...] += _colsum(dvn)
            dxhat = dvn * lng
            dgv = t["rstd"] * (dxhat - jnp.mean(dxhat, axis=-1, keepdims=True)
                               - xhat * jnp.mean(dxhat * xhat, axis=-1, keepdims=True))
            dv = dgv * _dgelu(v, t["tv"])
            dpc_r[0] = jnp.concatenate([dog, du, dv, dga, dgb], axis=1).astype(BF16)

    row = lambda w: pl.BlockSpec((1, ROW_BLOCK, w), lambda b, j: (b, j, 0))
    lat = pl.BlockSpec((1, ROW_BLOCK, D_MODEL), lambda b, j: (b, jnp.maximum(j - 1, 0), 0))
    full2 = lambda r, c: pl.BlockSpec((r, c), lambda b, j: (0, 0))
    ws_spec = pl.BlockSpec((N_HEADS, SGU_CHUNK, SGU_CHUNK), lambda b, j: (0, 0, 0))
    return _host_call(
        body, name="mix_bwd", grid=(bs, nb),
        in_specs=_mix_in_specs(lambda j: j) + [row(KW), row(KW), lat,
                                                pl.BlockSpec((1, 8, D_MODEL), lambda b, j: (b, 0, 0))]
        + _mix_param_specs() + [ws_spec],
        out_specs=[row(KW), row(7 * KW), PROJ_SPEC, PROJ_SPEC, full2(D_MODEL, D_MODEL),
                   full2(1, HEAD_DIM), full2(1, KW), full2(1, KW), ws_spec, full2(SGU_CHUNK, N_HEADS),
                   pl.BlockSpec((1, 8, D_MODEL), lambda b, j: (b, 0, 0))],
        out_shape=[SDS((bs, rows, KW), F32), SDS((bs, rows, 7 * KW), BF16), SDS((N_CHIPS, KW, PROJ_COLS), F32),
                   SDS((N_CHIPS, KW, PROJ_COLS), F32), SDS((D_MODEL, D_MODEL), F32), SDS((1, HEAD_DIM), F32),
                   SDS((1, KW), F32), SDS((1, KW), F32), SDS((N_HEADS, SGU_CHUNK, SGU_CHUNK), F32),
                   SDS((SGU_CHUNK, N_HEADS), F32), SDS((bs, 8, D_MODEL), F32)],
        args=(p, p, p, p, p, p, p, o_f, o_b, dx1, mod_c, gna, lng, lnb, w_s, bst, wpa, wpb, wo, w_s_t), carry=carry)


def _ffn(x1, target, mod_c, g_ffn, g_final, w_up, w_down):
    bs, seq, _ = x1.shape
    nbl = seq // ROW_BLOCK

    def body(x1_r, tg_r, mod_r, gf_r, gl_r, wu_r, wd_r,
             dx1_r, h2_r, dab_r, hid_r, dffn_r, loss_r, dgl_r, dgf_r, dmod_r):
        b, j = pl.program_id(0), pl.program_id(1)

        @pl.when((b == 0) & (j == 0))
        def _():
            for r in (loss_r, dgl_r, dgf_r):
                r[...] = jnp.zeros_like(r)

        @pl.when(j == 0)
        def _():
            dmod_r[...] = jnp.zeros_like(dmod_r)

        x1 = x1_r[0]
        shift, scale, gate = mod_r[0, 1:2, :], mod_r[0, 2:3, :], mod_r[0, 3:4, :]
        gf, gl = gf_r[...], gl_r[...]
        xn2, r2 = _rms(x1)
        hn2 = xn2 * gf
        h2 = (hn2 * (1.0 + scale) + shift).astype(BF16)
        h2_r[0] = h2
        ab = jnp.dot(h2, wu_r[...], preferred_element_type=F32)
        a, bb = ab[:, :D_FF], ab[:, D_FF:]
        sa = _sig(a)
        silu_a = a * sa
        hid = (silu_a * bb).astype(BF16)
        hid_r[0] = hid
        ffn = jnp.dot(hid, wd_r[...], preferred_element_type=F32)
        x2 = x1 + gate * ffn
        xn3, r3 = _rms(x2)
        err = xn3 * gl - tg_r[0]
        loss_r[...] += 0.5 * jnp.sum(jnp.mean(err * err, axis=-1, keepdims=True), axis=0, keepdims=True)
        dy = err * (1.0 / D_MODEL)
        dgl_r[...] += _colsum(dy * xn3)
        dx2 = _rms_bwd(dy * gl, xn3, r3)
        dmod_r[0, 3:4, :] += _colsum(dx2 * ffn)
        dffn = (gate * dx2).astype(BF16)
        dffn_r[0] = dffn
        dhid = lax.dot_general(dffn, wd_r[...], (((1,), (1,)), ((), ())), preferred_element_type=F32)
        da = dhid * bb * (sa * (1.0 + a * (1.0 - sa)))
        db = dhid * silu_a
        dab = jnp.concatenate([da, db], axis=1).astype(BF16)
        dab_r[0] = dab
        dh2 = lax.dot_general(dab, wu_r[...], (((1,), (1,)), ((), ())), preferred_element_type=F32)
        dmod_r[0, 1:2, :] += _colsum(dh2)
        dmod_r[0, 2:3, :] += _colsum(dh2 * hn2)
        dhn2 = dh2 * (1.0 + scale)
        dgf_r[...] += _colsum(dhn2 * xn2)
        dx1_r[0] = dx2 + _rms_bwd(dhn2 * gf, xn2, r2)

    lat = lambda w: pl.BlockSpec((1, ROW_BLOCK, w), lambda b, j: (b, j, 0))
    full2 = lambda r, c: pl.BlockSpec((r, c), lambda b, j: (0, 0))
    mod_spec = pl.BlockSpec((1, 8, D_MODEL), lambda b, j: (b, 0, 0))
    return pl.pallas_call(
        body, name="ffn", grid=(bs, nbl),
        in_specs=[lat(D_MODEL), lat(D_MODEL), mod_spec, full2(1, D_MODEL), full2(1, D_MODEL),
                  full2(D_MODEL, 2 * D_FF), full2(D_FF, D_MODEL)],
        out_specs=[lat(D_MODEL), lat(D_MODEL), lat(2 * D_FF), lat(D_FF), lat(D_MODEL),
                   full2(1, 1), full2(1, D_MODEL), full2(1, D_MODEL), mod_spec],
        out_shape=[SDS(x1.shape, F32), SDS(x1.shape, BF16), SDS((bs, seq, 2 * D_FF), BF16),
                   SDS((bs, seq, D_FF), BF16), SDS(x1.shape, BF16), SDS((1, 1), F32),
                   SDS((1, D_MODEL), F32), SDS((1, D_MODEL), F32), SDS((bs, 8, D_MODEL), F32)],
        compiler_params=_params(2),
    )(x1, target, mod_c, g_ffn, g_final, w_up, w_down)


def _row_tile(rows, most):
    return next(m * ROW_BLOCK for m in (9, 8, 4, 2, 1) if m <= most and rows % (m * ROW_BLOCK) == 0)


def _matmul_tn(a, b, n_blocks, tk, name, carry=None):
    t, m = a.shape
    n = b.shape[1]
    tn = n // n_blocks

    def body(a_ref, b_ref, o_ref):
        @pl.when(pl.program_id(1) == 0)
        def _():
            o_ref[...] = jnp.zeros_like(o_ref)
        o_ref[0] += _dot_tn(a_ref[...], b_ref[...])

    (out,), carried = _host_call(
        body, name=name, grid=(n_blocks, t // tk),
        in_specs=[pl.BlockSpec((tk, m), lambda i, k: (k, 0)), pl.BlockSpec((tk, tn), lambda i, k: (k, i))],
        out_specs=[pl.BlockSpec((1, m, tn), lambda i, k: (i, 0, 0))],
        out_shape=[SDS((n_blocks, m, tn), F32)], args=(a, b), carry=carry)
    return out if carry is None else (out, carried)


SMALL_ROWS = 80
ROW_CCTX = 3


def _small_reduce(gathered, lbg):
    def body(g_ref, lbg_ref, s_ref, dgam_ref):
        tot = g_ref[0:SMALL_ROWS, :]
        for dev in range(1, N_DEV):
            tot = tot + g_ref[dev * SMALL_ROWS:(dev + 1) * SMALL_ROWS, :]
        s_ref[...] = tot
        cc = g_ref[ROW_CCTX:ROW_CCTX + 1, :]
        for dev in range(2, N_DEV, 2):
            cc = cc + g_ref[dev * SMALL_ROWS + ROW_CCTX:dev * SMALL_ROWS + ROW_CCTX + 1, :]
        s_ref[ROW_CCTX:ROW_CCTX + 1, :] = cc
        dlb = tot[7:8, :]
        for d in range(2):
            s0 = _sig(lbg_ref[0, d:d + 1, :] - lbg_ref[1, d:d + 1, :])
            dgam_ref[d:d + 1, :] = dlb[:, d * KW:(d + 1) * KW] * s0 * (1.0 - s0)

    return pl.pallas_call(
        body, name="small_reduce", out_shape=[SDS((SMALL_ROWS, D_MODEL), F32), SDS((2, KW), F32)],
        in_specs=[VMEM_SPEC] * 2, out_specs=[VMEM_SPEC] * 2,
    )(gathered, lbg)


def _pad_cols(a, width):
    return jnp.pad(a, ((0, 0), (0, width - a.shape[1])))


def kernel(x, c, ctx, c_ctx, w_mod, b_mod, g_mix, g_ffn, w_in, lb_gamma, g_norm_a, ln_v_g, ln_v_b, w_s, b_s, w_pa, w_pb, w_o, w_up, w_down, g_final, loss_target, m_c_ctx, m_w_mod, m_b_mod, m_g_mix, m_g_ffn, m_w_in, m_lb_gamma, m_g_norm_a, m_ln_v_g, m_ln_v_b, m_w_s, m_b_s, m_w_pa, m_w_pb, m_w_o, m_w_up, m_w_down, m_g_final, v_c_ctx, v_w_mod, v_b_mod, v_g_mix, v_g_ffn, v_w_in, v_lb_gamma, v_g_norm_a, v_ln_v_g, v_ln_v_b, v_w_s, v_b_s, v_w_pa, v_w_pb, v_w_o, v_w_up, v_w_down, v_g_final):
    ax, ay, ac = lax.axis_index("x"), lax.axis_index("y"), lax.axis_index("c")
    kc = 2 * ax + ay
    dev = 2 * kc + ac
    pos = jnp.stack([kc, ac]).astype(jnp.int32)
    bs, seq, _ = x.shape
    assert bs <= 4 and ctx.shape[1] == ROW_BLOCK and seq % ROW_BLOCK == 0
    mod_cols = w_mod.shape[2]

    lbg_row = _pad_cols(lb_gamma.reshape(1, -1), D_MODEL)
    pay1 = jnp.concatenate([c, jnp.zeros((4 - bs, D_MODEL), F32), c_ctx[None, :], lbg_row,
                            jnp.zeros((2, D_MODEL), F32)], axis=0)
    b_mod_s = lax.dynamic_slice(b_mod, (0, kc * mod_cols), (1, mod_cols))
    cond64, mod_g = _cond_and_mod(pay1, w_mod[0], b_mod_s)
    lbg_full = cond64.reshape(N_DEV, 8, D_MODEL)[0::2, 5, :KW].reshape(N_CHIPS, 2, 2, HEAD_DIM)
    lbg_full = jnp.transpose(lbg_full, (1, 2, 0, 3)).reshape(2, 2, KW)
    mod_g = mod_g.reshape(N_DEV, 64, mod_cols)[0::2]
    shards = [w_in[0], w_up[0], w_pa[0], w_pb[0], w_o[0], w_down[0]]
    bufs = _cast_bf16(pos, shards)
    mod_full = jnp.transpose(mod_g, (1, 0, 2)).reshape(64, N_CHIPS * mod_cols)
    mod_mine = lax.dynamic_slice(mod_full, (dev * 8, 0), (8, 6 * D_MODEL)).reshape(8, 6, D_MODEL)
    mod, mc = mod_mine[:bs], mod_mine[4]
    zeros4 = jnp.zeros((bs, 4, D_MODEL), F32)
    mod_a = jnp.concatenate([mod[:, 0:2], jnp.broadcast_to(mc[None, 0:2], (bs, 2, D_MODEL)), zeros4], axis=1)
    mod_c = jnp.concatenate([mod[:, 2:6], zeros4], axis=1)

    def cols_major(a):
        return jnp.transpose(a, (1, 0, 2)).reshape(a.shape[1], -1)

    gna, lng, lnb = g_norm_a, ln_v_g, ln_v_b
    ws3 = w_s[0]
    ws3_t = jnp.transpose(ws3, (0, 2, 1))
    bst = jnp.transpose(b_s[0])

    (p, h_all), (w_in_g,) = _in_fwd_own(pos, x, ctx, g_mix, mod_a, w_in[0], carry=_carry_gather_send(bufs[:1]),
                                        then=_carry_gather_forward(bufs[:1]))
    p, sent_up = _in_fwd_rest(pos, h_all, w_in_g, p, carry=_carry_gather_send(bufs[1:2]))
    w_in_f = cols_major(w_in_g)
    fwd_rest = _carry_gather_forward(bufs[2:])
    then_rest = _Carry([], [], {}, fwd_rest.sems, lambda i, o, s: fwd_rest.copies(i[1:], o[1:], s))
    (o_f, s_f, o_b, s_b), gathered = _hgrn_fwd(
        p, lbg_full, carry=_merge_carries(_carry_gather_forward(sent_up), _carry_gather_send(bufs[2:])),
        then=then_rest)
    w_up_f, w_pa_f, w_pb_f = cols_major(gathered[0]), gathered[1], gathered[2]
    w_o_f = gathered[3].reshape(-1, D_MODEL)
    w_down_f = gathered[4].reshape(-1, D_MODEL)
    x1 = _mix_fwd(p, o_f, o_b, x, mod_c, gna, lng, lnb, ws3, bst, w_pa_f, w_pb_f, w_o_f)
    dx1, h2, dab, hid, dffn, loss_part, dg_final, dg_ffn, dmod_ffn = _ffn(
        x1, loss_target, mod_c, g_ffn, g_final[None, :], w_up_f, w_down_f)
    rows_lat = bs * seq
    dw_up = _matmul_tn(h2.reshape(rows_lat, D_MODEL), dab.reshape(rows_lat, 2 * D_FF), N_CHIPS,
                       _row_tile(rows_lat, 9), "dw_up")
    dw_down = _matmul_tn(hid.reshape(rows_lat, D_FF), dffn.reshape(rows_lat, D_MODEL), 1,
                         _row_tile(rows_lat, 4), "dw_down")

    part_ffn = [dw_up, dw_down.reshape(N_CHIPS, -1, D_MODEL)]
    (do_raw, dpc, dw_pa, dw_pb, dw_o, dgna, dlng, dlnb, dws, dbst, dmod_mix), sib_ffn = _mix_bwd(
        p, o_f, o_b, dx1, mod_c, gna, lng, lnb, ws3, ws3_t, bst, w_pa_f, w_pb_f, w_o_f,
        carry=_carry_sibling_halves(part_ffn))
    cpbf_ffn = _rs_add_halves(pos, part_ffn, sib_ffn, "rs_add_ffn")
    part_mix = [dw_pa, dw_pb, dw_o.reshape(N_CHIPS, -1, D_MODEL)]
    (df_f, dq_f, dv_f, dlb0, df_b, dq_b, dv_b, dlb1), got = _hgrn_bwd_pair(
        p, lbg_full, (s_f, s_b), do_raw,
        carry=_merge_carries(_carry_to_owner(cpbf_ffn), _carry_sibling_halves(part_mix)))
    own_ffn, sib_mix = got[:2], got[2:]
    half_ffn = _rs_sum_owner(pos, part_ffn, sib_ffn, own_ffn, "rs_sum_ffn")
    cpbf_mix = _rs_add_halves(pos, part_mix, sib_mix, "rs_add_mix")
    (grad_x, dp, dg_mix, dmod_in), _ = _in_bwd(x, ctx, dx1, g_mix, mod_a, w_in_f, df_f, df_b, dv_f, dv_b, dq_f, dq_b,
                                               dpc)

    rows_all = dp.shape[0] * dp.shape[1]
    tk_all = _row_tile(rows_all, 9)
    dw_in, got = _matmul_tn(h_all.reshape(rows_all, D_MODEL), dp.reshape(rows_all, IN_COLS), N_CHIPS, tk_all, "dw_in",
                            carry=_merge_carries(_carry_join_halves(half_ffn), _carry_to_owner(cpbf_mix)))
    g_ffn_w, own_mix = got[:2], got[2:]
    half_mix = _rs_sum_owner(pos, part_mix, sib_mix, own_mix, "rs_sum_mix")

    dmod_mine = jnp.concatenate([dmod_in[:, 0], dmod_in[:, 1], dmod_mix[:, 0], dmod_ffn[:, 1], dmod_ffn[:, 2],
                                 dmod_ffn[:, 3]], axis=1)
    dmc = jnp.concatenate([jnp.sum(dmod_in[:, 2], axis=0), jnp.sum(dmod_in[:, 3], axis=0),
                           jnp.zeros((4 * D_MODEL,), F32)])[None, :]
    pay3 = jnp.concatenate([dmod_mine, jnp.zeros((4 - bs, 6 * D_MODEL), F32), dmc,
                            jnp.zeros((3, 6 * D_MODEL), F32)], axis=0)
    dmod64, got = _all_gather8(pay3, "gather_dmod", carry=_merge_carries(_carry_sibling_halves([dw_in]),
                                                                          _carry_join_halves(half_mix)))
    sib_in, g_mix_w = got[:1], got[1:]
    cpbf_in = _rs_add_halves(pos, [dw_in], sib_in, "rs_add_w_in")
    dmod64_my = lax.dynamic_slice(dmod64, (0, kc * mod_cols), (64, mod_cols))
    g_w_mod, g_b_mod, g_cctx_part = _mod_bwd(cond64, dmod64, dmod64_my, w_mod[0], c_ctx[None, :])

    def row(*parts):
        return _pad_cols(jnp.concatenate([q.reshape(1, -1) for q in parts], axis=1), D_MODEL)

    small_rows = [dg_mix, dg_ffn, dg_final, g_cctx_part, row(dgna), row(dlng, dlnb), row(jnp.transpose(dbst)),
                  row(dlb0, dlb1), row(loss_part), jnp.zeros((7, D_MODEL), F32), dws.reshape(64, D_MODEL)]
    pay4 = jnp.concatenate(small_rows, axis=0)
    tot, dgam0 = _small_reduce(_all_gather8(pay4, "gather_small"), lbg_full)

    own_sems, own_src, own_land, own_token = _owner_send_start(cpbf_in[0], after=tot)

    rest_names = ["w_up", "w_pa", "w_pb", "w_o", "w_down", "w_mod"]
    rest_w = shards[1:] + [w_mod[0]]
    rest_g = [g_ffn_w[0], g_mix_w[0] + own_token[0, 0], g_mix_w[1], g_mix_w[2], g_ffn_w[1], g_w_mod]
    rest_m = [m_w_up[0], m_w_pa[0], m_w_pb[0], m_w_o[0], m_w_down[0], m_w_mod[0]]
    rest_v = [v_w_up[0], v_w_pa[0], v_w_pb[0], v_w_o[0], v_w_down[0], v_w_mod[0]]
    (ds_r, m2s_r, v2s_r), _ = _adamw_group(rest_w, rest_g, rest_m, rest_v, "adamw_rest")
    res = {}
    for name, g, d, m2, v2 in zip(rest_names, rest_g, ds_r, m2s_r, v2s_r):
        res[name] = (g[None], d[None], m2[None], v2[None])
    own_in = [_owner_send_wait(own_sems, own_src, own_land, after=(ds_r[0],))]
    g_in_w = _comm_call("rs_join_w_in",
                        _carry_join_halves(_rs_sum_owner(pos, [dw_in], sib_in, own_in, "rs_sum_w_in")))
    d, m2, v2 = _adamw_big(shards[0], g_in_w[0], m_w_in[0], v_w_in[0], "adamw_w_in")
    res["w_in"] = (g_in_w[0][None], d[None], m2[None], v2[None])

    loss = tot[8, 0]
    dgam_full = jnp.stack([dgam0, -dgam0])
    g_lbg = lax.dynamic_slice(dgam_full, (0, 0, kc * HEAD_DIM), (2, 2, HEAD_DIM))

    small = [
        ("c_ctx", c_ctx[None, :], tot[3:4], m_c_ctx, v_c_ctx),
        ("b_mod", b_mod, g_b_mod, m_b_mod, v_b_mod),
        ("g_mix", g_mix, tot[0:1], m_g_mix, v_g_mix),
        ("g_ffn", g_ffn, tot[1:2], m_g_ffn, v_g_ffn),
        ("lb_gamma", lb_gamma.reshape(4, HEAD_DIM), g_lbg.reshape(4, HEAD_DIM), m_lb_gamma, v_lb_gamma),
        ("g_norm_a", g_norm_a, tot[4:5, :HEAD_DIM], m_g_norm_a, v_g_norm_a),
        ("ln_v_g", ln_v_g, tot[5:6, :KW], m_ln_v_g, v_ln_v_g),
        ("ln_v_b", ln_v_b, tot[5:6, KW:], m_ln_v_b, v_ln_v_b),
        ("w_s", w_s.reshape(N_HEADS * SGU_CHUNK, SGU_CHUNK), tot[16:80].reshape(N_HEADS * SGU_CHUNK, SGU_CHUNK),
         m_w_s, v_w_s),
        ("b_s", b_s[0], tot[6:7, :KW].reshape(N_HEADS, SGU_CHUNK), m_b_s, v_b_s),
        ("g_final", g_final[None, :], tot[2:3], m_g_final, v_g_final),
    ]
    ws_, gs_ = [s[1] for s in small], [s[2] for s in small]
    ms_ = [s[3].reshape(s[1].shape) for s in small]
    vs_ = [s[4].reshape(s[1].shape) for s in small]
    ds_, m2s_, v2s_ = _adamw_small(ws_, gs_, ms_, vs_)
    for (name, _, g, m, _), d, m2, v2 in zip(small, ds_, m2s_, v2s_):
        res[name] = tuple(t.reshape(m.shape) for t in (g, d, m2, v2))

    order = ["c_ctx", "w_mod", "b_mod", "g_mix", "g_ffn", "w_in", "lb_gamma", "g_norm_a", "ln_v_g", "ln_v_b",
             "w_s", "b_s", "w_pa", "w_pb", "w_o", "w_up", "w_down", "g_final"]
    outs = [loss, grad_x]
    for part in range(4):
        outs += [res[n][part] for n in order]
    return tuple(outs)
```

```python
import functools
import math

import jax
import jax.numpy as jnp
import numpy as np
from jax import lax
from jax.experimental import pallas as pl
from jax.experimental.pallas import tpu as pltpu

F32 = jnp.float32
BF16 = jnp.bfloat16
SDS = jax.ShapeDtypeStruct
MESH = pl.DeviceIdType.MESH

EPS = 1e-6
D_MODEL = 1024
N_HEADS = 4
HEAD_DIM = 128
KW = N_HEADS * HEAD_DIM
IN_COLS = 11 * KW
D_FF = 2816
HGRN_CHUNK = 64
SGU_CHUNK = 128
ROW_BLOCK = 256
N_CHIPS = 4
N_DEV = 8
V7X_VMEM_BYTES = 64 * 1024 * 1024
VMEM_LIMIT = V7X_VMEM_BYTES - 6 * 1024 * 1024

ADAM_LR, ADAM_B1, ADAM_B2, ADAM_EPS, ADAM_WD, ADAM_STEP = 0.001, 0.9, 0.999, 1e-08, 0.01, 10
GELU_C0 = math.sqrt(2.0 / math.pi)
GELU_C1 = 0.044715

VMEM_SPEC = pl.BlockSpec(memory_space=pltpu.VMEM)
ANY_SPEC = pl.BlockSpec(memory_space=pl.ANY)


def _params(n_grid):
    return pltpu.CompilerParams(dimension_semantics=("arbitrary",) * n_grid, vmem_limit_bytes=VMEM_LIMIT)


def _sig(x):
    return 0.5 * jnp.tanh(0.5 * x) + 0.5


def _gelu(x):
    t = jnp.tanh(GELU_C0 * (x + GELU_C1 * x * x * x))
    return 0.5 * x * (1.0 + t), t


def _dgelu(x, t):
    return 0.5 * (1.0 + t) + 0.5 * x * (1.0 - t * t) * GELU_C0 * (1.0 + 3.0 * GELU_C1 * x * x)


def _dot(a, b):
    return jnp.dot(a.astype(BF16), b.astype(BF16), preferred_element_type=F32)


def _dot_nt(a, b):
    return lax.dot_general(a.astype(BF16), b.astype(BF16), (((1,), (1,)), ((), ())), preferred_element_type=F32)


def _dot_tn(a, b):
    return lax.dot_general(a.astype(BF16), b.astype(BF16), (((0,), (0,)), ((), ())), preferred_element_type=F32)


def _dot_f32(a, b, dims=(((1,), (0,)), ((), ()))):
    return lax.dot_general(a, b, dims, precision=lax.Precision.HIGHEST, preferred_element_type=F32)


def _rms(x):
    r = lax.rsqrt(jnp.mean(x * x, axis=-1, keepdims=True) + EPS)
    return x * r, r


def _rms_bwd(dxn, xn, r):
    return r * (dxn - xn * jnp.mean(dxn * xn, axis=-1, keepdims=True))


def _colsum(a):
    return jnp.sum(a, axis=0, keepdims=True)


def _tri(n, upper):
    t = lax.broadcasted_iota(jnp.int32, (n, n), 0)
    s = lax.broadcasted_iota(jnp.int32, (n, n), 1)
    return (s >= t) if upper else (s <= t)


def _all_gather8(x_shard, name, carry=None, then=None):
    m_per, n = x_shard.shape
    n_ci = len(carry.ins) if carry else 0
    n_co = len(carry.outs) if carry else 0
    n_cs = len(carry.sems) if carry else 0

    def body(*refs):
        x_ref, cins = refs[0], refs[1:1 + n_ci]
        out_ref, couts = refs[1 + n_ci], refs[2 + n_ci:2 + n_ci + n_co]
        send_sems, recv_sems, local_sem = refs[2 + n_ci + n_co:5 + n_ci + n_co]
        csems = refs[5 + n_ci + n_co:5 + n_ci + n_co + n_cs]
        tsems = refs[5 + n_ci + n_co + n_cs:]
        if carry:
            _start_all(carry.copies(cins, couts, csems)[0])
        _gather8_body(x_ref, out_ref, send_sems, recv_sems, local_sem, m_per)
        if carry:
            _wait_all(carry.copies(cins, couts, csems)[1])
        if then:
            _start_all(then.copies(couts, couts, tsems)[0])
            _wait_all(then.copies(couts, couts, tsems)[1])

    res = pl.pallas_call(
        body, name=name, out_shape=[SDS((N_DEV * m_per, n), x_shard.dtype)] + (carry.outs if carry else []),
        in_specs=[VMEM_SPEC] + [ANY_SPEC] * n_ci, out_specs=[VMEM_SPEC] + [ANY_SPEC] * n_co,
        input_output_aliases={1 + i: 1 + o for i, o in carry.alias.items()} if carry else {},
        scratch_shapes=[pltpu.SemaphoreType.DMA((7,)), pltpu.SemaphoreType.DMA((7,)), pltpu.SemaphoreType.DMA]
        + (carry.sems if carry else []) + (then.sems if then else []),
    )(x_shard, *(carry.ins if carry else []))
    return res[0] if carry is None else (res[0], list(res[1:]))


def _gather8_body(x_ref, out_ref, send_sems, recv_sems, local_sem, m_per):
    x, y, c = lax.axis_index("x"), lax.axis_index("y"), lax.axis_index("c")
    me, sibling = (x, y, c), (x, y, 1 - c)
    chips = [(1 - x, y), (x, 1 - y), (1 - x, 1 - y)]

    def rows(px, py, pc):
        return out_ref.at[pl.ds((4 * px + 2 * py + pc) * m_per, m_per), :]

    def copy(k, block, to, src=None):
        return pltpu.make_async_remote_copy(
            src_ref=rows(*block) if src is None else src, dst_ref=rows(*block),
            send_sem=send_sems.at[k], recv_sem=recv_sems.at[k], device_id=to, device_id_type=MESH)

    mine = pltpu.make_async_copy(x_ref, rows(*me), local_sem)
    mine.start()
    first = [copy(0, me, sibling, src=x_ref)]
    first += [copy(1 + j, me, (*chip, c), src=x_ref) for j, chip in enumerate(chips)]
    for cp in first:
        cp.start()
    passed = [copy(4 + j, (*chip, c), sibling) for j, chip in enumerate(chips)]
    for j, chip in enumerate(chips):
        copy(1 + j, (*chip, c), me).wait_recv()
        passed[j].start()
    copy(0, sibling, me).wait_recv()
    for j, chip in enumerate(chips):
        copy(4 + j, (*chip, 1 - c), me).wait_recv()
    for cp in first + passed:
        cp.wait_send()
    mine.wait()


def _mesh_pos():
    x, y, c = lax.axis_index("x"), lax.axis_index("y"), lax.axis_index("c")
    chips = [(1 - x, y), (x, 1 - y), (1 - x, 1 - y)]
    return x, y, c, 2 * x + y, (x, y, 1 - c), chips


def _half_rows(c, rh):
    return pl.ds(pl.multiple_of(c * rh, 16), rh)


class _Carry:
    def __init__(self, ins, outs, alias, sems, copies):
        self.ins, self.outs, self.alias, self.sems, self.copies = list(ins), list(outs), dict(alias), list(sems), copies


def _remote(src, dst, send, recv, to):
    return functools.partial(pltpu.make_async_remote_copy, src_ref=src, dst_ref=dst, send_sem=send, recv_sem=recv,
                             device_id=to, device_id_type=MESH)


def _carry_gather_send(bufs):
    n = len(bufs)

    def copies(ins, outs, sems):
        x, y, c, kc, sibling, chips = _mesh_pos()
        starts, waits = [], []
        for wi in range(n):
            rh = outs[wi].shape[1] // 2
            for jj, chip in enumerate(chips):
                mine = outs[wi].at[kc, _half_rows(c, rh), :]
                cp = _remote(mine, mine, sems[0].at[wi, jj], sems[1].at[wi, jj], (*chip, c))
                starts.append(cp)
                waits.append((cp, "send"))
                theirs = outs[wi].at[2 * chip[0] + chip[1], _half_rows(c, rh), :]
                waits.append((_remote(theirs, theirs, sems[0].at[wi, jj], sems[1].at[wi, jj], (*chip, c)), "recv"))
        return starts, waits

    return _Carry(bufs, [SDS(b.shape, b.dtype) for b in bufs], {i: i for i in range(n)},
                  [pltpu.SemaphoreType.DMA((n, 3)), pltpu.SemaphoreType.DMA((n, 3))], copies)


def _carry_gather_forward(bufs):
    n = len(bufs)

    def copies(ins, outs, sems):
        x, y, c, kc, sibling, chips = _mesh_pos()
        starts, waits = [], []
        for wi in range(n):
            rh = outs[wi].shape[1] // 2
            for jj, chip in enumerate(chips):
                got = outs[wi].at[2 * chip[0] + chip[1], _half_rows(c, rh), :]
                cp = _remote(got, got, sems[0].at[wi, jj], sems[1].at[wi, jj], sibling)
                starts.append(cp)
                waits.append((cp, "send"))
                other = outs[wi].at[2 * chip[0] + chip[1], _half_rows(1 - c, rh), :]
                waits.append((_remote(other, other, sems[0].at[wi, jj], sems[1].at[wi, jj], sibling), "recv"))
        return starts, waits

    return _Carry(bufs, [SDS(b.shape, b.dtype) for b in bufs], {i: i for i in range(n)},
                  [pltpu.SemaphoreType.DMA((n, 3)), pltpu.SemaphoreType.DMA((n, 3))], copies)


def _carry_sibling_halves(grads):
    n = len(grads)

    def copies(ins, outs, sems):
        x, y, c, kc, sibling, chips = _mesh_pos()
        cps = [_remote(ins[wi].at[:, _half_rows(1 - c, ins[wi].shape[1] // 2), :], outs[wi],
                       sems[0].at[wi], sems[1].at[wi], sibling) for wi in range(n)]
        return cps, [(cp, "both") for cp in cps]

    return _Carry(grads, [SDS((N_CHIPS, g.shape[1] // 2, g.shape[2]), F32) for g in grads], {},
                  [pltpu.SemaphoreType.DMA((n,)), pltpu.SemaphoreType.DMA((n,))], copies)


def _carry_to_owner(cpbfs):
    n = len(cpbfs)

    def copies(ins, outs, sems):
        x, y, c, kc, sibling, chips = _mesh_pos()
        starts, waits = [], []
        for wi in range(n):
            for jj, chip in enumerate(chips):
                cp = _remote(ins[wi].at[2 * chip[0] + chip[1]], outs[wi].at[kc],
                             sems[0].at[wi, jj], sems[1].at[wi, jj], (*chip, c))
                starts.append(cp)
                waits.append((cp, "send"))
                slot = outs[wi].at[2 * chip[0] + chip[1]]
                waits.append((_remote(slot, slot, sems[0].at[wi, jj], sems[1].at[wi, jj], (*chip, c)), "recv"))
        return starts, waits

    return _Carry(cpbfs, [SDS(g.shape, BF16) for g in cpbfs], {},
                  [pltpu.SemaphoreType.DMA((n, 3)), pltpu.SemaphoreType.DMA((n, 3))], copies)


def _carry_join_halves(bufs):
    n = len(bufs)

    def copies(ins, outs, sems):
        x, y, c, kc, sibling, chips = _mesh_pos()
        cps = []
        for wi in range(n):
            mine = outs[wi].at[_half_rows(c, outs[wi].shape[0] // 2), :]
            cps.append(_remote(mine, mine, sems[0].at[wi], sems[1].at[wi], sibling))
        return cps, [(cp, "both") for cp in cps]

    return _Carry(bufs, [SDS(b.shape, F32) for b in bufs], {i: i for i in range(n)},
                  [pltpu.SemaphoreType.DMA((n,)), pltpu.SemaphoreType.DMA((n,))], copies)


def _merge_carries(*carries):
    ins, outs, alias, sems, parts = [], [], {}, [], []
    for cy in carries:
        parts.append((len(ins), len(cy.ins), len(outs), len(cy.outs), len(sems), len(cy.sems), cy.copies))
        alias.update({len(ins) + i: len(outs) + o for i, o in cy.alias.items()})
        ins += cy.ins
        outs += cy.outs
        sems += cy.sems

    def copies(i, o, s):
        starts, waits = [], []
        for i0, ni, o0, no, s0, ns, fn in parts:
            st, wt = fn(i[i0:i0 + ni], o[o0:o0 + no], s[s0:s0 + ns])
            starts += st
            waits += wt
        return starts, waits

    return _Carry(ins, outs, alias, sems, copies)


def _start_all(starts):
    for cp in starts:
        cp().start()


def _wait_all(waits):
    for cp, which in waits:
        if which == "send":
            cp().wait_send()
        elif which == "recv":
            cp().wait_recv()
        else:
            cp().wait()


HBM_SPEC = pl.BlockSpec(memory_space=pltpu.HBM)
SEM_SPEC = pl.BlockSpec(memory_space=pltpu.SEMAPHORE)
SPLIT_COPY_EFFECT = pltpu.SideEffectType.DATAFLOW_SIDE_EFFECTING


def _owner_send_start(cpbf, after):
    land = lax.empty(cpbf.shape, cpbf.dtype)

    def body(src_ref, land_ref, after_ref, s0, s1, s2, r0, r1, r2, src_thru, land_thru, token):
        x, y, c, kc, sibling, chips = _mesh_pos()
        for jj, (chip, s_sem, r_sem) in enumerate(zip(chips, (s0, s1, s2), (r0, r1, r2))):
            pltpu.make_async_remote_copy(
                src_ref=src_ref.at[2 * chip[0] + chip[1]], dst_ref=land_ref.at[kc], send_sem=s_sem, recv_sem=r_sem,
                device_id=(*chip, c), device_id_type=MESH).start()
        token[...] = jnp.zeros_like(token)

    buf = pltpu.HBM(cpbf.shape, cpbf.dtype)
    outs = pl.pallas_call(
        body, name="rs_owner_w_in_start",
        out_shape=(pltpu.SemaphoreType.DMA(()),) * 6 + (buf, buf, SDS((8, 128), F32)),
        in_specs=(HBM_SPEC, HBM_SPEC, ANY_SPEC), out_specs=(SEM_SPEC,) * 6 + (HBM_SPEC, HBM_SPEC, VMEM_SPEC),
        input_output_aliases={0: 6, 1: 7},
        compiler_params=pltpu.CompilerParams(has_side_effects=SPLIT_COPY_EFFECT),
    )(pltpu.with_memory_space_constraint(cpbf, pltpu.HBM), pltpu.with_memory_space_constraint(land, pltpu.HBM), after)
    return outs[:6], outs[6], outs[7], outs[8]


def _owner_send_wait(sems, src_thru, land_thru, after):
    n_after = len(after)

    def body(*refs):
        src_ref, land_ref = refs[0], refs[1]
        sends, recvs = refs[2:5], refs[5:8]
        x, y, c, kc, sibling, chips = _mesh_pos()
        for jj, chip in enumerate(chips):
            slot = 2 * chip[0] + chip[1]
            cp = pltpu.make_async_remote_copy(
                src_ref=src_ref.at[slot], dst_ref=land_ref.at[slot], send_sem=sends[jj], recv_sem=recvs[jj],
                device_id=(*chip, c), device_id_type=MESH)
            cp.wait_send()
            cp.wait_recv()

    buf = pltpu.HBM(land_thru.shape, land_thru.dtype)
    return pl.pallas_call(
        body, name="rs_owner_w_in_wait", out_shape=(buf, buf),
        in_specs=(HBM_SPEC, HBM_SPEC) + (SEM_SPEC,) * 6 + (ANY_SPEC,) * n_after, out_specs=(HBM_SPEC, HBM_SPEC),
        input_output_aliases={0: 0, 1: 1},
        compiler_params=pltpu.CompilerParams(has_side_effects=SPLIT_COPY_EFFECT),
    )(src_thru, land_thru, *sems, *after)[1]


def _sibling_send_start(grad, after):
    rh = grad.shape[1] // 2
    land_shape = (N_CHIPS, rh, grad.shape[2])
    land = lax.empty(land_shape, F32)

    def body(src_ref, land_ref, after_ref, s_sem, r_sem, src_thru, land_thru, token):
        x, y, c, kc, sibling, chips = _mesh_pos()
        pltpu.make_async_remote_copy(
            src_ref=src_ref.at[:, _half_rows(1 - c, rh), :], dst_ref=land_ref, send_sem=s_sem, recv_sem=r_sem,
            device_id=sibling, device_id_type=MESH).start()
        token[...] = jnp.zeros_like(token)

    outs = pl.pallas_call(
        body, name="rs_sibling_w_in_start",
        out_shape=(pltpu.SemaphoreType.DMA(()),) * 2
        + (pltpu.HBM(grad.shape, F32), pltpu.HBM(land_shape, F32), SDS((8, 128), F32)),
        in_specs=(HBM_SPEC, HBM_SPEC, ANY_SPEC), out_specs=(SEM_SPEC,) * 2 + (HBM_SPEC, HBM_SPEC, VMEM_SPEC),
        input_output_aliases={0: 2, 1: 3},
        compiler_params=pltpu.CompilerParams(has_side_effects=SPLIT_COPY_EFFECT),
    )(pltpu.with_memory_space_constraint(grad, pltpu.HBM), pltpu.with_memory_space_constraint(land, pltpu.HBM), after)
    return outs[:2], outs[2], outs[3], outs[4]


def _sibling_send_wait(sems, src_thru, land_thru, after):
    rh = src_thru.shape[1] // 2

    def body(src_ref, land_ref, s_sem, r_sem, *rest):
        x, y, c, kc, sibling, chips = _mesh_pos()
        cp = pltpu.make_async_remote_copy(
            src_ref=src_ref.at[:, _half_rows(1 - c, rh), :], dst_ref=land_ref, send_sem=s_sem, recv_sem=r_sem,
            device_id=sibling, device_id_type=MESH)
        cp.wait_send()
        cp.wait_recv()

    return pl.pallas_call(
        body, name="rs_sibling_w_in_wait",
        out_shape=(pltpu.HBM(src_thru.shape, F32), pltpu.HBM(land_thru.shape, F32)),
        in_specs=(HBM_SPEC, HBM_SPEC, SEM_SPEC, SEM_SPEC) + (ANY_SPEC,) * len(after), out_specs=(HBM_SPEC, HBM_SPEC),
        input_output_aliases={0: 0, 1: 1},
        compiler_params=pltpu.CompilerParams(has_side_effects=SPLIT_COPY_EFFECT),
    )(src_thru, land_thru, *sems, *after)


def _comm_call(name, carry):
    n_i, n_o = len(carry.ins), len(carry.outs)

    def body(*refs):
        ins, outs, sems = refs[:n_i], refs[n_i:n_i + n_o], refs[n_i + n_o:]
        _start_all(carry.copies(ins, outs, sems)[0])
        _wait_all(carry.copies(ins, outs, sems)[1])

    return pl.pallas_call(
        body, name=name, out_shape=carry.outs, in_specs=[ANY_SPEC] * n_i, out_specs=[ANY_SPEC] * n_o,
        input_output_aliases=carry.alias, scratch_shapes=carry.sems,
    )(*carry.ins)


def _host_call(body, *, name, grid, in_specs, out_specs, out_shape, args, scratch_shapes=(), carry=None, then=None,
               prefetch=None, aliases=None):
    n_in, n_out, n_scr = len(in_specs), len(out_specs), len(scratch_shapes)
    n_ci = len(carry.ins) if carry else 0
    n_co = len(carry.outs) if carry else 0
    n_cs = len(carry.sems) if carry else 0
    n_pf = 0 if prefetch is None else 1

    def wrapped(*refs):
        pf, refs = refs[:n_pf], refs[n_pf:]
        ins, cins = refs[:n_in], refs[n_in:n_in + n_ci]
        o0 = n_in + n_ci
        outs, couts = refs[o0:o0 + n_out], refs[o0 + n_out:o0 + n_out + n_co]
        s0 = o0 + n_out + n_co
        scr, sems, tsems = refs[s0:s0 + n_scr], refs[s0 + n_scr:s0 + n_scr + n_cs], refs[s0 + n_scr + n_cs:]
        idx = [pl.program_id(a) for a in range(len(grid))]
        first = functools.reduce(jnp.logical_and, [i == 0 for i in idx])
        last = functools.reduce(jnp.logical_and, [i == g - 1 for i, g in zip(idx, grid)])

        if carry:
            @pl.when(first)
            def _():
                _start_all(carry.copies(cins, couts, sems)[0])

        body(*pf, *ins, *outs, *scr)

        if carry:
            @pl.when(last)
            def _():
                _wait_all(carry.copies(cins, couts, sems)[1])
                if then:
                    _start_all(then.copies(couts, couts, tsems)[0])
                    _wait_all(then.copies(couts, couts, tsems)[1])

    all_in = list(in_specs) + [ANY_SPEC] * n_ci
    all_out = list(out_specs) + [ANY_SPEC] * n_co
    all_scr = list(scratch_shapes) + (carry.sems if carry else []) + (then.sems if then else [])
    alias = {n_pf + i: o for i, o in (aliases or {}).items()}
    if carry:
        alias.update({n_pf + n_in + i: n_out + o for i, o in carry.alias.items()})
    kwargs = dict(name=name, out_shape=list(out_shape) + (carry.outs if carry else []), input_output_aliases=alias,
                  compiler_params=_params(len(grid)))
    if prefetch is None:
        call = pl.pallas_call(wrapped, grid=grid, in_specs=all_in, out_specs=all_out, scratch_shapes=all_scr, **kwargs)
        res = call(*args, *(carry.ins if carry else []))
    else:
        call = pl.pallas_call(wrapped, grid_spec=pltpu.PrefetchScalarGridSpec(
            num_scalar_prefetch=1, grid=grid, in_specs=all_in, out_specs=all_out, scratch_shapes=all_scr), **kwargs)
        res = call(prefetch, *args, *(carry.ins if carry else []))
    return list(res[:n_out]), list(res[n_out:])


def _rs_add_halves(pos, grads, recvs, name):
    n = len(grads)

    def body(pos_ref, *refs):
        for i in range(n):
            refs[2 * n + i][...] = (refs[i][...] + refs[n + i][...]).astype(BF16)

    blks = [(1, g.shape[1] // 4, g.shape[2]) for g in grads]
    mine = [pl.BlockSpec(b, lambda k, i, p: (k, p[1] * 2 + i, 0)) for b in blks]
    half = [pl.BlockSpec(b, lambda k, i, p: (k, i, 0)) for b in blks]
    return pl.pallas_call(
        body, name=name,
        grid_spec=pltpu.PrefetchScalarGridSpec(
            num_scalar_prefetch=1, grid=(N_CHIPS, 2), in_specs=mine + half, out_specs=half),
        out_shape=[SDS((N_CHIPS, g.shape[1] // 2, g.shape[2]), BF16) for g in grads],
        compiler_params=_params(2),
    )(pos, *grads, *recvs)


def _rs_sum_owner(pos, grads, recvs, recv3s, name):
    n = len(grads)

    def body(pos_ref, *refs):
        for i in range(n):
            g, s, r1, r2, r3 = (refs[j * n + i] for j in range(5))
            own = g[0] + s[0]
            refs[5 * n + i][...] = ((own + r1[0].astype(F32)) + r2[0].astype(F32)) + r3[0].astype(F32)

    blks = [(1, g.shape[1] // 4, g.shape[2]) for g in grads]
    mine = [pl.BlockSpec(b, lambda i, p: (p[0], p[1] * 2 + i, 0)) for b in blks]

    def slot(d):
        return [pl.BlockSpec(b, lambda i, p: ((p[0] + d) % N_CHIPS, i, 0)) for b in blks]

    return pl.pallas_call(
        body, name=name,
        grid_spec=pltpu.PrefetchScalarGridSpec(
            num_scalar_prefetch=1, grid=(2,), in_specs=mine + slot(0) + slot(1) + slot(2) + slot(3),
            out_specs=[pl.BlockSpec(b[1:], lambda i, p: (p[1] * 2 + i, 0)) for b in blks]),
        out_shape=[SDS(g.shape[1:], F32) for g in grads],
        compiler_params=_params(1),
    )(pos, *grads, *recvs, *recv3s, *recv3s, *recv3s)


def _cast_bf16(pos, arrs):
    n = len(arrs)

    def body(pos_ref, *refs):
        for i in range(n):
            refs[n + i][0] = refs[i][...].astype(BF16)

    return pl.pallas_call(
        body, name="cast_bf16",
        grid_spec=pltpu.PrefetchScalarGridSpec(
            num_scalar_prefetch=1, grid=(2,),
            in_specs=[pl.BlockSpec((a.shape[0] // 2, a.shape[1]), lambda i, p: (i, 0)) for a in arrs],
            out_specs=[pl.BlockSpec((1, a.shape[0] // 2, a.shape[1]), lambda i, p: (p[0], i, 0)) for a in arrs]),
        out_shape=[SDS((N_CHIPS,) + a.shape, BF16) for a in arrs],
        compiler_params=_params(1),
    )(pos, *arrs)


def _adamw_vals(w, g, m, v):
    m2 = ADAM_B1 * m + (1.0 - ADAM_B1) * g
    v2 = ADAM_B2 * v + (1.0 - ADAM_B2) * (g * g)
    m_hat = m2 / (1.0 - ADAM_B1 ** ADAM_STEP)
    v_hat = v2 / (1.0 - ADAM_B2 ** ADAM_STEP)
    delta = -ADAM_LR * (m_hat / (jnp.sqrt(v_hat) + ADAM_EPS) + ADAM_WD * w)
    return delta, m2, v2


def _adamw_big(w, g, m, v, name):
    rows, cols = w.shape
    rb = rows // 4

    def body(w_ref, g_ref, m_ref, v_ref, d_ref, m2_ref, v2_ref):
        d, m2, v2 = _adamw_vals(w_ref[...], g_ref[...], m_ref[...], v_ref[...])
        d_ref[...] = d
        m2_ref[...] = m2
        v2_ref[...] = v2

    spec = pl.BlockSpec((rb, cols), lambda i: (i, 0))
    return pl.pallas_call(
        body, name=name, grid=(4,), in_specs=[spec] * 4, out_specs=[spec] * 3,
        out_shape=[SDS(w.shape, F32)] * 3, compiler_params=_params(1),
    )(w, g, m, v)


ADAMW_GROUP_STEPS = 8


def _adamw_group(ws, gs, ms, vs, name, carry=None):
    n = len(ws)

    def body(*refs):
        for i in range(n):
            d, m2, v2 = _adamw_vals(refs[i][...], refs[n + i][...], refs[2 * n + i][...], refs[3 * n + i][...])
            refs[4 * n + i][...] = d
            refs[5 * n + i][...] = m2
            refs[6 * n + i][...] = v2

    specs = [pl.BlockSpec((w.shape[0] // ADAMW_GROUP_STEPS, w.shape[1]), lambda i: (i, 0)) for w in ws]
    shapes = [SDS(w.shape, F32) for w in ws]
    outs, carried = _host_call(
        body, name=name, grid=(ADAMW_GROUP_STEPS,), in_specs=specs * 4, out_specs=specs * 3, out_shape=shapes * 3,
        args=(*ws, *gs, *ms, *vs), carry=carry)
    return (outs[:n], outs[n:2 * n], outs[2 * n:]), carried


def _adamw_small(ws, gs, ms, vs):
    n = len(ws)

    def body(*refs):
        for i in range(n):
            d, m2, v2 = _adamw_vals(refs[i][...], refs[n + i][...], refs[2 * n + i][...], refs[3 * n + i][...])
            refs[4 * n + i][...] = d
            refs[5 * n + i][...] = m2
            refs[6 * n + i][...] = v2

    shapes = [SDS(w.shape, F32) for w in ws]
    outs = pl.pallas_call(
        body, name="adamw_small", out_shape=shapes * 3,
        in_specs=[VMEM_SPEC] * (4 * n), out_specs=[VMEM_SPEC] * (3 * n),
    )(*ws, *gs, *ms, *vs)
    return outs[:n], outs[n:2 * n], outs[2 * n:]


def _cond_and_mod(pay, w_mod_s, b_mod_s):
    m_per, n_cols = pay.shape[0], w_mod_s.shape[1]
    rows = N_DEV * m_per

    def body(pay_ref, w_ref, b_ref, cond_ref, modg_ref, mod_s, sems_a, semr_a, loc_a, sems_b, semr_b, loc_b):
        _gather8_body(pay_ref, cond_ref, sems_a, semr_a, loc_a, m_per)
        cc = cond_ref[...]
        mod_s[...] = _dot_f32(cc * _sig(cc), w_ref[...]) + b_ref[...]
        _gather8_body(mod_s, modg_ref, sems_b, semr_b, loc_b, rows)

    dma7 = pltpu.SemaphoreType.DMA((7,))
    return pl.pallas_call(
        body, name="cond_and_mod",
        out_shape=[SDS((rows, pay.shape[1]), F32), SDS((N_DEV * rows, n_cols), F32)],
        in_specs=[VMEM_SPEC] * 3, out_specs=[VMEM_SPEC] * 2,
        scratch_shapes=[pltpu.VMEM((rows, n_cols), F32), dma7, dma7, pltpu.SemaphoreType.DMA,
                        dma7, dma7, pltpu.SemaphoreType.DMA],
        compiler_params=pltpu.CompilerParams(vmem_limit_bytes=VMEM_LIMIT),
    )(pay, w_mod_s, b_mod_s)


def _mod_bwd(cond64, dmod64, dmod64_my, w_mod_s, c_ctx):
    def body(c_ref, g_ref, gm_ref, w_ref, cc_ref, gw_ref, gb_ref, gcc_ref):
        cc = c_ref[...]
        act = cc * _sig(cc)
        gm = gm_ref[...]
        gw_ref[...] = _dot_f32(act, gm, (((0,), (0,)), ((), ())))
        gb_ref[...] = _colsum(g_ref[...])
        dact = _dot_f32(gm, w_ref[...], (((1,), (1,)), ((), ())))
        tot = dact[4:5, :]
        for dev in range(1, N_DEV):
            tot = tot + dact[8 * dev + 4:8 * dev + 5, :]
        c0 = cc_ref[...]
        s0 = _sig(c0)
        gcc_ref[...] = tot * (s0 * (1.0 + c0 * (1.0 - s0)))

    return pl.pallas_call(
        body, name="mod_bwd",
        out_shape=[SDS(w_mod_s.shape, F32), SDS((1, dmod64.shape[1]), F32), SDS((1, D_MODEL), F32)],
        in_specs=[VMEM_SPEC] * 5, out_specs=[VMEM_SPEC] * 3,
        compiler_params=pltpu.CompilerParams(vmem_limit_bytes=VMEM_LIMIT),
    )(cond64, dmod64, dmod64_my, w_mod_s, c_ctx)


SHARD_COLS = IN_COLS // N_CHIPS


def _in_fwd_own(pos, x, ctx, g_mix, mod_a, w_own, carry=None, then=None):
    bs, seq, _ = x.shape
    nb = seq // ROW_BLOCK + 1

    def body(pos_ref, x_ref, ctx_ref, g_ref, mod_ref, w_ref, p_ref, h_ref, w_bf):
        b, j = pl.program_id(0), pl.program_id(1)

        @pl.when((b == 0) & (j == 0))
        def _():
            w_bf[...] = w_ref[...].astype(BF16)

        is_ctx = j == 0
        xin = jnp.where(is_ctx, ctx_ref[0], x_ref[0])
        shift = jnp.where(is_ctx, mod_ref[0, 2:3, :], mod_ref[0, 0:1, :])
        scale = jnp.where(is_ctx, mod_ref[0, 3:4, :], mod_ref[0, 1:2, :])
        xn, _ = _rms(xin)
        hb = ((xn * g_ref[...]) * (1.0 + scale) + shift).astype(BF16)
        h_ref[0] = hb
        p_ref[0] = jnp.dot(hb, w_bf[...], preferred_element_type=F32)

    return _host_call(
        body, name="in_fwd_own", grid=(bs, nb), prefetch=pos,
        in_specs=[pl.BlockSpec((1, ROW_BLOCK, D_MODEL), lambda b, j, p: (b, jnp.maximum(j - 1, 0), 0)),
                  pl.BlockSpec((1, ROW_BLOCK, D_MODEL), lambda b, j, p: (b, 0, 0)),
                  pl.BlockSpec((1, D_MODEL), lambda b, j, p: (0, 0)),
                  pl.BlockSpec((1, 8, D_MODEL), lambda b, j, p: (b, 0, 0)),
                  pl.BlockSpec((D_MODEL, SHARD_COLS), lambda b, j, p: (0, 0))],
        out_specs=[pl.BlockSpec((1, ROW_BLOCK, SHARD_COLS), lambda b, j, p: (b, j, p[0])),
                   pl.BlockSpec((1, ROW_BLOCK, D_MODEL), lambda b, j, p: (b, j, 0))],
        out_shape=[SDS((bs, nb * ROW_BLOCK, IN_COLS), F32), SDS((bs, nb * ROW_BLOCK, D_MODEL), BF16)],
        scratch_shapes=[pltpu.VMEM((D_MODEL, SHARD_COLS), BF16)],
        args=(x, ctx, g_mix, mod_a, w_own), carry=carry, then=then)


def _in_fwd_rest(pos, h_all, w_in_g, p, carry=None):
    bs, rows, _ = h_all.shape
    rows_all = bs * rows
    tile = next(m * ROW_BLOCK for m in (9, 3, 1) if rows_all % (m * ROW_BLOCK) == 0)

    def body(pos_ref, h_ref, w_ref, p_in_ref, p_ref):
        p_ref[...] = jnp.dot(h_ref[...], w_ref[0], preferred_element_type=F32)

    shard = lambda n, p: (p[0] + 1 + n) % N_CHIPS
    (p2,), carried = _host_call(
        body, name="in_fwd_rest", grid=(N_CHIPS - 1, rows_all // tile), prefetch=pos,
        in_specs=[pl.BlockSpec((tile, D_MODEL), lambda n, t, p: (t, 0)),
                  pl.BlockSpec((1, D_MODEL, SHARD_COLS), lambda n, t, p: (shard(n, p), 0, 0)),
                  ANY_SPEC],
        out_specs=[pl.BlockSpec((tile, SHARD_COLS), lambda n, t, p: (t, shard(n, p)))],
        out_shape=[SDS((rows_all, IN_COLS), F32)], aliases={2: 0},
        args=(h_all.reshape(rows_all, D_MODEL), w_in_g, p.reshape(rows_all, IN_COLS)), carry=carry)
    return p2.reshape(bs, rows, IN_COLS), carried


def _in_bwd(x, ctx, dx1, g_mix, mod_a, w_in, df_f, df_b, dv_f, dv_b, dq_f, dq_b, dpc, carry=None):
    bs, seq, _ = x.shape
    nb = seq // ROW_BLOCK + 1

    def body(x_ref, ctx_ref, dx1_ref, g_ref, mod_ref, w_ref, dff_ref, dfb_ref, dvf_ref, dvb_ref, dqf_ref, dqb_ref,
             dpc_ref, gx_ref, dp_ref, dg_ref, dmod_ref):
        b, j = pl.program_id(0), pl.program_id(1)
        is_ctx = j == 0

        @pl.when((b == 0) & (j == 0))
        def _():
            dg_ref[...] = jnp.zeros_like(dg_ref)

        @pl.when(j == 0)
        def _():
            dmod_ref[...] = jnp.zeros_like(dmod_ref)

        di = (dvf_ref[0] + dvb_ref[0]).astype(BF16)
        dq = (dqf_ref[0] + dqb_ref[0]).astype(BF16)
        dp = jnp.concatenate([dff_ref[0], dfb_ref[0], di, dq, dpc_ref[0]], axis=1)
        dp_ref[0] = dp
        dh = lax.dot_general(dp, w_ref[...], (((1,), (1,)), ((), ())), preferred_element_type=F32)
        xin = jnp.where(is_ctx, ctx_ref[0], x_ref[0])
        scale = jnp.where(is_ctx, mod_ref[0, 3:4, :], mod_ref[0, 1:2, :])
        xn, r = _rms(xin)
        g = g_ref[...]
        hn = xn * g
        d_shift = _colsum(dh)
        d_scale = _colsum(dh * hn)
        dhn = dh * (1.0 + scale)
        dg_ref[...] += _colsum(dhn * xn)
        dx = _rms_bwd(dhn * g, xn, r)

        @pl.when(is_ctx)
        def _():
            dmod_ref[0, 2:3, :] += d_shift
            dmod_ref[0, 3:4, :] += d_scale

        @pl.when(jnp.logical_not(is_ctx))
        def _():
            dmod_ref[0, 0:1, :] += d_shift
            dmod_ref[0, 1:2, :] += d_scale
            gx_ref[0] = dx + dx1_ref[0]

    def rows(w):
        return pl.BlockSpec((1, ROW_BLOCK, w), lambda b, j: (b, j, 0))

    lat = pl.BlockSpec((1, ROW_BLOCK, D_MODEL), lambda b, j: (b, jnp.maximum(j - 1, 0), 0))
    return _host_call(
        body, name="in_bwd", grid=(bs, nb),
        in_specs=[lat, pl.BlockSpec((1, ROW_BLOCK, D_MODEL), lambda b, j: (b, 0, 0)), lat,
                  pl.BlockSpec((1, D_MODEL), lambda b, j: (0, 0)),
                  pl.BlockSpec((1, 8, D_MODEL), lambda b, j: (b, 0, 0)),
                  pl.BlockSpec((D_MODEL, IN_COLS), lambda b, j: (0, 0)),
                  rows(KW), rows(KW), rows(KW), rows(KW), rows(KW), rows(KW), rows(7 * KW)],
        out_specs=[lat, rows(IN_COLS), pl.BlockSpec((1, D_MODEL), lambda b, j: (0, 0)),
                   pl.BlockSpec((1, 8, D_MODEL), lambda b, j: (b, 0, 0))],
        out_shape=[SDS(x.shape, F32), SDS((bs, nb * ROW_BLOCK, IN_COLS), BF16), SDS((1, D_MODEL), F32),
                   SDS((bs, 8, D_MODEL), F32)],
        args=(x, ctx, dx1, g_mix, mod_a, w_in, df_f, df_b, dv_f, dv_b, dq_f, dq_b, dpc), carry=carry)


def _lower_bound(lbg_ref, direction):
    return _sig(lbg_ref[0, direction:direction + 1, :] - lbg_ref[1, direction:direction + 1, :])


def _block_tri(upper):
    t = np.arange(ROW_BLOCK)[:, None]
    s = np.arange(ROW_BLOCK)[None, :]
    same = (t // HGRN_CHUNK) == (s // HGRN_CHUNK)
    return jnp.asarray(same & ((s >= t) if upper else (s <= t)), dtype=BF16)


TRI_SPEC = pl.BlockSpec((ROW_BLOCK, ROW_BLOCK), lambda b, j: (0, 0))


def _tri_matmul_f32(tri, g):
    g0 = g.astype(BF16)
    r1 = g - g0.astype(F32)
    g1 = r1.astype(BF16)
    g2 = (r1 - g1.astype(F32)).astype(BF16)
    return (jnp.dot(tri, g2, preferred_element_type=F32) + jnp.dot(tri, g1, preferred_element_type=F32)) \
        + jnp.dot(tri, g0, preferred_element_type=F32)


def _chunk_rows(rows):
    return jnp.concatenate([jnp.broadcast_to(r, (HGRN_CHUNK, r.shape[1])) for r in rows], axis=0)


def _block_gates(fl, q, lb, tri, upper):
    t = {}
    t["sg"] = _sig(fl)
    t["f"] = lb + (1.0 - lb) * t["sg"]
    k = 1.0 - t["f"]
    bcum = _tri_matmul_f32(tri, jnp.log(t["f"]))
    ends = [bcum[ci * HGRN_CHUNK:ci * HGRN_CHUNK + 1] if upper else bcum[(ci + 1) * HGRN_CHUNK - 1:(ci + 1) * HGRN_CHUNK]
            for ci in range(fl.shape[0] // HGRN_CHUNK)]
    mid = _chunk_rows([0.5 * r for r in ends])
    t["dec"] = [jnp.exp(r) for r in ends]
    t["e1"] = jnp.exp(bcum - mid)
    t["e2"] = jnp.exp(mid - bcum)
    t["eh"] = _chunk_rows([jnp.exp(0.5 * r) for r in ends])
    t["qi"] = q * t["e1"]
    t["ki"] = k * t["e2"]
    t["kd"] = t["ki"] * t["eh"]
    t["qe"] = t["qi"] * t["eh"]
    return t


def _hgrn_block_order(direction, nb):
    if direction == 0:
        return lambda j: j
    return lambda j: jnp.where(j == 0, 0, nb - j)


def _hgrn_fwd(p, lbg, carry=None, then=None):
    bs, rows, _ = p.shape
    nb = rows // ROW_BLOCK
    ncb = ROW_BLOCK // HGRN_CHUNK
    orders = [_hgrn_block_order(d, nb) for d in (0, 1)]
    dirs = (0, 1)

    def body(f0_ref, i0_ref, q0_ref, f1_ref, i1_ref, q1_ref, lbg_ref, tri0_ref, tri1_ref,
             o0_ref, s0_ref, o1_ref, s1_ref, st):
        @pl.when(pl.program_id(1) == 0)
        def _():
            st[...] = jnp.zeros_like(st)

        f_refs, i_refs, q_refs = (f0_ref, f1_ref), (i0_ref, i1_ref), (q0_ref, q1_ref)
        tri_refs, o_refs, s_refs = (tri0_ref, tri1_ref), (o0_ref, o1_ref), (s0_ref, s1_ref)
        chunk = lambda a, ci, h: a[ci * HGRN_CHUNK:(ci + 1) * HGRN_CHUNK, h * HEAD_DIM:(h + 1) * HEAD_DIM]
        masks = [_tri(HGRN_CHUNK, d == 1) for d in dirs]
        t = [_block_gates(f_refs[d][0], q_refs[d][0], _lower_bound(lbg_ref, d), tri_refs[d][...], d == 1) for d in dirs]
        v = [i_refs[d][0] for d in dirs]
        intra = [[[None] * N_HEADS for _ in range(ncb)] for _ in dirs]
        ds_loc = [[[None] * N_HEADS for _ in range(ncb)] for _ in dirs]
        for ci in range(ncb):
            for h in range(N_HEADS):
                for d in dirs:
                    a = jnp.where(masks[d], _dot_nt(chunk(t[d]["qi"], ci, h), chunk(t[d]["ki"], ci, h)), 0.0)
                    intra[d][ci][h] = _dot(a, chunk(v[d], ci, h))
                    ds_loc[d][ci][h] = _dot_tn(chunk(v[d], ci, h), chunk(t[d]["kd"], ci, h))
        for h in range(N_HEADS):
            ls = slice(h * HEAD_DIM, (h + 1) * HEAD_DIM)
            s = [st[d, h] for d in dirs]
            for step in range(ncb):
                for d in dirs:
                    ci = ncb - 1 - step if d == 1 else step
                    s_refs[d][0, 0, ci, h] = s[d]
                    o_refs[d][0, ci * HGRN_CHUNK:(ci + 1) * HGRN_CHUNK, ls] = (
                        intra[d][ci][h] + _dot_nt(chunk(t[d]["qe"], ci, h), s[d]))
                    s[d] = s[d] * t[d]["dec"][ci][:, ls] + ds_loc[d][ci][h]
            for d in dirs:
                st[d, h] = s[d]

    def col(d, cb):
        return pl.BlockSpec((1, ROW_BLOCK, KW), lambda b, j: (b, orders[d](j), cb))

    def outs(d):
        return [pl.BlockSpec((1, ROW_BLOCK, KW), lambda b, j: (b, orders[d](j), 0)),
                pl.BlockSpec((1, 1, ncb, N_HEADS, HEAD_DIM, HEAD_DIM), lambda b, j: (b, orders[d](j), 0, 0, 0, 0))]

    shapes = [SDS((bs, rows, KW), F32), SDS((bs, nb, ncb, N_HEADS, HEAD_DIM, HEAD_DIM), F32)]
    return _host_call(
        body, name="hgrn_fwd", grid=(bs, nb),
        in_specs=[col(0, 0), col(0, 2), col(0, 3), col(1, 1), col(1, 2), col(1, 3),
                  pl.BlockSpec((2, 2, KW), lambda b, j: (0, 0, 0)), TRI_SPEC, TRI_SPEC],
        out_specs=outs(0) + outs(1), out_shape=shapes * 2,
        scratch_shapes=[pltpu.VMEM((2, N_HEADS, HEAD_DIM, HEAD_DIM), F32)],
        args=(p, p, p, p, p, p, lbg, _block_tri(False), _block_tri(True)), carry=carry, then=then)


def _hgrn_bwd_pair(p, lbg, s_saved, do_raw, carry=None):
    bs, rows, _ = p.shape
    nb = rows // ROW_BLOCK
    ncb = ROW_BLOCK // HGRN_CHUNK
    dirs = (0, 1)
    fwd_orders = [_hgrn_block_order(d, nb) for d in dirs]
    orders = [lambda j, d=d: fwd_orders[d](nb - 1 - j) for d in dirs]
    pairs = [(ci, h) for ci in range(ncb) for h in range(N_HEADS)]

    def body(f0_ref, i0_ref, q0_ref, s0_ref, do0_ref, f1_ref, i1_ref, q1_ref, s1_ref, do1_ref,
             lbg_ref, tril_ref, triu_ref,
             df0_ref, dq0_ref, dv0_ref, dlb0_ref, df1_ref, dq1_ref, dv1_ref, dlb1_ref, dst, acc):
        b, j = pl.program_id(0), pl.program_id(1)
        f_refs, i_refs, q_refs = (f0_ref, f1_ref), (i0_ref, i1_ref), (q0_ref, q1_ref)
        s_refs, do_refs = (s0_ref, s1_ref), (do0_ref, do1_ref)
        df_refs, dq_refs, dv_refs, dlb_refs = (df0_ref, df1_ref), (dq0_ref, dq1_ref), (dv0_ref, dv1_ref), (dlb0_ref, dlb1_ref)
        tri_refs, trit_refs = (tril_ref, triu_ref), (triu_ref, tril_ref)

        @pl.when((b == 0) & (j == 0))
        def _():
            dlb0_ref[...] = jnp.zeros_like(dlb0_ref)
            dlb1_ref[...] = jnp.zeros_like(dlb1_ref)

        @pl.when(j == 0)
        def _():
            dst[...] = jnp.zeros_like(dst)

        chunk = lambda a, ci, h: a[ci * HGRN_CHUNK:(ci + 1) * HGRN_CHUNK, h * HEAD_DIM:(h + 1) * HEAD_DIM]
        rows_of = lambda ci: slice(ci * HGRN_CHUNK, (ci + 1) * HGRN_CHUNK)
        lanes_of = lambda h: slice(h * HEAD_DIM, (h + 1) * HEAD_DIM)
        grid3 = lambda: [[[None] * N_HEADS for _ in range(ncb)] for _ in dirs]
        lbs = [_lower_bound(lbg_ref, d) for d in dirs]
        masks = [_tri(HGRN_CHUNK, d == 1) for d in dirs]
        masks_t = [_tri(HGRN_CHUNK, d != 1) for d in dirs]
        t = [_block_gates(f_refs[d][0], q_refs[d][0], lbs[d], tri_refs[d][...], d == 1) for d in dirs]
        v = [i_refs[d][0] for d in dirs]
        do = [do_refs[d][0] for d in dirs]
        a_t, da, da_t, dv_in, ds_loc = (grid3() for _ in range(5))
        for ci, h in pairs:
            for d in dirs:
                a_t[d][ci][h] = _dot_nt(chunk(t[d]["ki"], ci, h), chunk(t[d]["qi"], ci, h))
        for ci, h in pairs:
            for d in dirs:
                da[d][ci][h] = _dot_nt(chunk(do[d], ci, h), chunk(v[d], ci, h))
        for ci, h in pairs:
            for d in dirs:
                da_t[d][ci][h] = _dot_nt(chunk(v[d], ci, h), chunk(do[d], ci, h))
        for ci, h in pairs:
            for d in dirs:
                acc[d, 3, rows_of(ci), lanes_of(h)] = _dot(chunk(do[d], ci, h), s_refs[d][0, 0, ci, h])
        for ci, h in pairs:
            for d in dirs:
                ds_loc[d][ci][h] = _dot_tn(chunk(do[d], ci, h), chunk(t[d]["qe"], ci, h))
        for ci, h in pairs:
            for d in dirs:
                acc[d, 0, rows_of(ci), lanes_of(h)] = _dot(jnp.where(masks[d], da[d][ci][h], 0.0),
                                                           chunk(t[d]["ki"], ci, h))
        for ci, h in pairs:
            for d in dirs:
                acc[d, 1, rows_of(ci), lanes_of(h)] = _dot(jnp.where(masks_t[d], da_t[d][ci][h], 0.0),
                                                           chunk(t[d]["qi"], ci, h))
        for ci, h in pairs:
            for d in dirs:
                dv_in[d][ci][h] = _dot(jnp.where(masks_t[d], a_t[d][ci][h], 0.0), chunk(do[d], ci, h))
        ddec = grid3()
        for h in range(N_HEADS):
            ls = lanes_of(h)
            ds = [dst[d, h] for d in dirs]
            for step in range(ncb):
                for d in dirs:
                    ci = step if d == 1 else ncb - 1 - step
                    acc[d, 2, rows_of(ci), ls] = _dot(chunk(v[d], ci, h), ds[d])
                    acc[d, 4, rows_of(ci), ls] = dv_in[d][ci][h] + _dot_nt(chunk(t[d]["kd"], ci, h), ds[d])
                    ddec[d][ci][h] = _colsum(ds[d] * s_refs[d][0, 0, ci, h])
                    ds[d] = ds[d] * t[d]["dec"][ci][:, ls] + ds_loc[d][ci][h]
            for d in dirs:
                dst[d, h] = ds[d]
        for d in dirs:
            td = t[d]
            dqi, dki, dkd, dqe = (acc[d, i] for i in range(4))
            dq_refs[d][0] = td["e1"] * (dqi + dqe * td["eh"])
            dv_refs[d][0] = acc[d, 4]
            dk = td["e2"] * (dki + dkd * td["eh"])
            dkd_kd = dkd * td["kd"]
            db = dqi * td["qi"] - dki * td["ki"] - dkd_kd + dqe * td["qe"]
            dbl = [_colsum(dkd_kd[rows_of(ci)]) + jnp.concatenate(ddec[d][ci], axis=1) * td["dec"][ci]
                   for ci in range(ncb)]
            dg = _tri_matmul_f32(trit_refs[d][...], db) + _chunk_rows(dbl)
            df = dg / td["f"] - dk
            sg = td["sg"]
            dlb_refs[d][...] += _colsum(df * (1.0 - sg))
            df_refs[d][0] = (df * (1.0 - lbs[d]) * sg * (1.0 - sg)).astype(BF16)

    def ins(d):
        col = lambda cb: pl.BlockSpec((1, ROW_BLOCK, KW), lambda b, j: (b, orders[d](j), cb))
        return [col(d), col(2), col(3),
                pl.BlockSpec((1, 1, ncb, N_HEADS, HEAD_DIM, HEAD_DIM), lambda b, j: (b, orders[d](j), 0, 0, 0, 0)),
                pl.BlockSpec((1, ROW_BLOCK, KW), lambda b, j: (b, orders[d](j), 0))]

    def outs(d):
        row = pl.BlockSpec((1, ROW_BLOCK, KW), lambda b, j: (b, orders[d](j), 0))
        return [row, row, row, pl.BlockSpec((1, KW), lambda b, j: (0, 0))]

    shapes = [SDS((bs, rows, KW), BF16), SDS((bs, rows, KW), F32), SDS((bs, rows, KW), F32), SDS((1, KW), F32)]
    return _host_call(
        body, name="hgrn_bwd", grid=(bs, nb),
        in_specs=ins(0) + ins(1) + [pl.BlockSpec((2, 2, KW), lambda b, j: (0, 0, 0)), TRI_SPEC, TRI_SPEC],
        out_specs=outs(0) + outs(1), out_shape=shapes * 2,
        scratch_shapes=[pltpu.VMEM((2, N_HEADS, HEAD_DIM, HEAD_DIM), F32), pltpu.VMEM((2, 5, ROW_BLOCK, KW), F32)],
        args=(p, p, p, s_saved[0], do_raw, p, p, p, s_saved[1], do_raw, lbg, _block_tri(False), _block_tri(True)),
        carry=carry)


PROJ_COLS = D_MODEL // N_CHIPS


PROJ_SPEC = pl.BlockSpec((N_CHIPS, KW, PROJ_COLS), lambda b, j: (0, 0, 0))


def _dot_shards(a, w_ref):
    return jnp.concatenate([_dot(a, w_ref[s]) for s in range(N_CHIPS)], axis=1)


def _dot_nt_shards(d, w_ref):
    parts = [_dot_nt(d[:, s * PROJ_COLS:(s + 1) * PROJ_COLS], w_ref[s]) for s in range(N_CHIPS)]
    return (parts[0] + parts[1]) + (parts[2] + parts[3])


def _mix_values(og, u, v, ga, gb, o_raw, gna, lng, lnb, ws_ref, bst, wpa, wpb, wo):
    t = {}
    sog = _sig(og)
    t["sog"], t["silu_og"] = sog, og * sog
    xh_l, r_l = [], []
    for h in range(N_HEADS):
        xh, r = _rms(o_raw[:, h * HEAD_DIM:(h + 1) * HEAD_DIM])
        xh_l.append(xh)
        r_l.append(r)
    t["xh"], t["r"] = jnp.concatenate(xh_l, axis=1), r_l
    gna4 = jnp.concatenate([gna] * N_HEADS, axis=1)
    t["gna4"] = gna4
    t["o_n"] = t["xh"] * gna4
    t["o_a"] = t["o_n"] * t["silu_og"]
    t["ya"] = _dot_shards(t["o_a"], wpa)
    t["gu"], t["tu"] = _gelu(u)
    gv, t["tv"] = _gelu(v)
    mu = jnp.mean(gv, axis=-1, keepdims=True)
    cen = gv - mu
    t["rstd"] = lax.rsqrt(jnp.mean(cen * cen, axis=-1, keepdims=True) + EPS)
    t["xhat"] = cen * t["rstd"]
    vn = t["xhat"] * lng + lnb
    t["vn"] = vn
    chunks = []
    for n in range(ROW_BLOCK // SGU_CHUNK):
        rs = slice(n * SGU_CHUNK, (n + 1) * SGU_CHUNK)
        groups = []
        for g in range(N_HEADS):
            ls = slice(g * HEAD_DIM, (g + 1) * HEAD_DIM)
            groups.append(_dot(ws_ref[g], vn[rs, ls]) + bst[:, g:g + 1])
        chunks.append(jnp.concatenate(groups, axis=1))
    t["mixed"] = jnp.concatenate(chunks, axis=0)
    t["o_bm"] = t["gu"] * t["mixed"]
    t["yb"] = _dot_shards(t["o_bm"], wpb)
    t["sa"], t["sb"] = _sig(ga), _sig(gb)
    t["merged"] = t["sa"] * t["ya"] + t["sb"] * t["yb"]
    t["mix"] = _dot(t["merged"], wo)
    return t


def _mix_in_specs(row_of):
    def col(cb):
        return pl.BlockSpec((1, ROW_BLOCK, KW), lambda b, j: (b, row_of(j), cb))
    return [col(cb) for cb in range(4, 11)]


def _mix_param_specs():
    full2 = lambda r, c: pl.BlockSpec((r, c), lambda b, j: (0, 0))
    return [full2(1, HEAD_DIM), full2(1, KW), full2(1, KW),
            pl.BlockSpec((N_HEADS, SGU_CHUNK, SGU_CHUNK), lambda b, j: (0, 0, 0)),
            full2(SGU_CHUNK, N_HEADS), PROJ_SPEC, PROJ_SPEC, full2(D_MODEL, D_MODEL)]


def _mix_fwd(p, o_f, o_b, x, mod_c, gna, lng, lnb, w_s, bst, wpa, wpb, wo):
    bs, seq, _ = x.shape
    nbl = seq // ROW_BLOCK

    def body(og_r, u_r, v_r, ga0_r, ga1_r, gb0_r, gb1_r, of_r, ob_r, x_r, mod_r,
             gna_r, lng_r, lnb_r, ws_r, bst_r, wpa_r, wpb_r, wo_r, x1_r):
        ga = jnp.concatenate([ga0_r[0], ga1_r[0]], axis=1)
        gb = jnp.concatenate([gb0_r[0], gb1_r[0]], axis=1)
        t = _mix_values(og_r[0], u_r[0], v_r[0], ga, gb, of_r[0] + ob_r[0], gna_r[...], lng_r[...], lnb_r[...],
                        ws_r, bst_r[...], wpa_r, wpb_r, wo_r[...])
        x1_r[0] = x_r[0] + mod_r[0, 0:1, :] * t["mix"]

    row = lambda w: pl.BlockSpec((1, ROW_BLOCK, w), lambda b, j: (b, j + 1, 0))
    lat = pl.BlockSpec((1, ROW_BLOCK, D_MODEL), lambda b, j: (b, j, 0))
    return pl.pallas_call(
        body, name="mix_fwd", grid=(bs, nbl),
        in_specs=_mix_in_specs(lambda j: j + 1) + [row(KW), row(KW), lat,
                                                    pl.BlockSpec((1, 8, D_MODEL), lambda b, j: (b, 0, 0))]
        + _mix_param_specs(),
        out_specs=lat, out_shape=SDS(x.shape, F32), compiler_params=_params(2),
    )(p, p, p, p, p, p, p, o_f, o_b, x, mod_c, gna, lng, lnb, w_s, bst, wpa, wpb, wo)


def _mix_bwd(p, o_f, o_b, dx1, mod_c, gna, lng, lnb, w_s, w_s_t, bst, wpa, wpb, wo, carry=None):
    bs, rows, _ = p.shape
    nb = rows // ROW_BLOCK

    def body(og_r, u_r, v_r, ga0_r, ga1_r, gb0_r, gb1_r, of_r, ob_r, dx1_r, mod_r,
             gna_r, lng_r, lnb_r, ws_r, bst_r, wpa_r, wpb_r, wo_r, wst_r,
             dor_r, dpc_r, dwpa_r, dwpb_r, dwo_r, dgna_r, dlng_r, dlnb_r, dws_r, dbst_r, dmod_r):
        b, j = pl.program_id(0), pl.program_id(1)

        @pl.when((b == 0) & (j == 0))
        def _():
            for r in (dwpa_r, dwpb_r, dwo_r, dgna_r, dlng_r, dlnb_r, dws_r, dbst_r):
                r[...] = jnp.zeros_like(r)

        @pl.when(j == 0)
        def _():
            dmod_r[...] = jnp.zeros_like(dmod_r)
            dor_r[...] = jnp.zeros_like(dor_r)
            dpc_r[...] = jnp.zeros_like(dpc_r)

        @pl.when(j > 0)
        def _():
            og, u, v = og_r[0], u_r[0], v_r[0]
            ga = jnp.concatenate([ga0_r[0], ga1_r[0]], axis=1)
            gb = jnp.concatenate([gb0_r[0], gb1_r[0]], axis=1)
            gna, lng = gna_r[...], lng_r[...]
            wpa, wpb, wo = wpa_r, wpb_r, wo_r[...]
            dx1 = dx1_r[0]
            dmix = mod_r[0, 0:1, :] * dx1
            dmerged = _dot_nt(dmix, wo)
            t = _mix_values(og, u, v, ga, gb, of_r[0] + ob_r[0], gna, lng, lnb_r[...],
                            ws_r, bst_r[...], wpa, wpb, wo)
            dmod_r[0, 0:1, :] += _colsum(dx1 * t["mix"])
            dwo_r[...] += _dot_tn(t["merged"], dmix)
            sa, sb = t["sa"], t["sb"]
            dya, dyb = sa * dmerged, sb * dmerged
            dga = dmerged * t["ya"] * sa * (1.0 - sa)
            dgb = dmerged * t["yb"] * sb * (1.0 - sb)
            do_a = _dot_nt_shards(dya, wpa)
            do_bm = _dot_nt_shards(dyb, wpb)
            for s in range(N_CHIPS):
                cs = slice(s * PROJ_COLS, (s + 1) * PROJ_COLS)
                dwpa_r[s] += _dot_tn(t["o_a"], dya[:, cs])
                dwpb_r[s] += _dot_tn(t["o_bm"], dyb[:, cs])
            sog = t["sog"]
            dog = do_a * t["o_n"] * (sog * (1.0 + og * (1.0 - sog)))
            do_n = do_a * t["silu_og"]
            dxh = do_n * t["gna4"]
            prod = do_n * t["xh"]
            dgna = jnp.zeros((1, HEAD_DIM), F32)
            dor_l = []
            for h in range(N_HEADS):
                ls = slice(h * HEAD_DIM, (h + 1) * HEAD_DIM)
                dgna = dgna + _colsum(prod[:, ls])
                dor_l.append(_rms_bwd(dxh[:, ls], t["xh"][:, ls], t["r"][h]))
            dgna_r[...] += dgna
            dor_r[0] = jnp.concatenate(dor_l, axis=1)
            du = do_bm * t["mixed"] * _dgelu(u, t["tu"])
            dmixed = do_bm * t["gu"]
            vn = t["vn"]
            dvn_chunks = []
            for n in range(ROW_BLOCK // SGU_CHUNK):
                rs = slice(n * SGU_CHUNK, (n + 1) * SGU_CHUNK)
                groups = []
                for g in range(N_HEADS):
                    ls = slice(g * HEAD_DIM, (g + 1) * HEAD_DIM)
                    dm = dmixed[rs, ls]
                    dws_r[g] += _dot_nt(dm, vn[rs, ls])
                    dbst_r[:, g:g + 1] += jnp.sum(dm, axis=1, keepdims=True)
                    groups.append(_dot(wst_r[g], dm))
                dvn_chunks.append(jnp.concatenate(groups, axis=1))
            dvn = jnp.concatenate(dvn_chunks, axis=0)
            xhat = t["xhat"]
            dlng_r[...] += _colsum(dvn * xhat)
            dlnb_r[...] += _colsum(dvn)
            dxhat = dvn * lng
            dgv = t["rstd"] * (dxhat - jnp.mean(dxhat, axis=-1, keepdims=True)
                               - xhat * jnp.mean(dxhat * xhat, axis=-1, keepdims=True))
            dv = dgv * _dgelu(v, t["tv"])
            dpc_r[0] = jnp.concatenate([dog, du, dv, dga, dgb], axis=1).astype(BF16)

    row = lambda w: pl.BlockSpec((1, ROW_BLOCK, w), lambda b, j: (b, j, 0))
    lat = pl.BlockSpec((1, ROW_BLOCK, D_MODEL), lambda b, j: (b, jnp.maximum(j - 1, 0), 0))
    full2 = lambda r, c: pl.BlockSpec((r, c), lambda b, j: (0, 0))
    ws_spec = pl.BlockSpec((N_HEADS, SGU_CHUNK, SGU_CHUNK), lambda b, j: (0, 0, 0))
    return _host_call(
        body, name="mix_bwd", grid=(bs, nb),
        in_specs=_mix_in_specs(lambda j: j) + [row(KW), row(KW), lat,
                                                pl.BlockSpec((1, 8, D_MODEL), lambda b, j: (b, 0, 0))]
        + _mix_param_specs() + [ws_spec],
        out_specs=[row(KW), row(7 * KW), PROJ_SPEC, PROJ_SPEC, full2(D_MODEL, D_MODEL),
                   full2(1, HEAD_DIM), full2(1, KW), full2(1, KW), ws_spec, full2(SGU_CHUNK, N_HEADS),
                   pl.BlockSpec((1, 8, D_MODEL), lambda b, j: (b, 0, 0))],
        out_shape=[SDS((bs, rows, KW), F32), SDS((bs, rows, 7 * KW), BF16), SDS((N_CHIPS, KW, PROJ_COLS), F32),
                   SDS((N_CHIPS, KW, PROJ_COLS), F32), SDS((D_MODEL, D_MODEL), F32), SDS((1, HEAD_DIM), F32),
                   SDS((1, KW), F32), SDS((1, KW), F32), SDS((N_HEADS, SGU_CHUNK, SGU_CHUNK), F32),
                   SDS((SGU_CHUNK, N_HEADS), F32), SDS((bs, 8, D_MODEL), F32)],
        args=(p, p, p, p, p, p, p, o_f, o_b, dx1, mod_c, gna, lng, lnb, w_s, bst, wpa, wpb, wo, w_s_t), carry=carry)


def _ffn(x1, target, mod_c, g_ffn, g_final, w_up, w_down):
    bs, seq, _ = x1.shape
    nbl = seq // ROW_BLOCK

    def body(x1_r, tg_r, mod_r, gf_r, gl_r, wu_r, wd_r,
             dx1_r, h2_r, dab_r, hid_r, dffn_r, loss_r, dgl_r, dgf_r, dmod_r):
        b, j = pl.program_id(0), pl.program_id(1)

        @pl.when((b == 0) & (j == 0))
        def _():
            for r in (loss_r, dgl_r, dgf_r):
                r[...] = jnp.zeros_like(r)

        @pl.when(j == 0)
        def _():
            dmod_r[...] = jnp.zeros_like(dmod_r)

        x1 = x1_r[0]
        shift, scale, gate = mod_r[0, 1:2, :], mod_r[0, 2:3, :], mod_r[0, 3:4, :]
        gf, gl = gf_r[...], gl_r[...]
        xn2, r2 = _rms(x1)
        hn2 = xn2 * gf
        h2 = (hn2 * (1.0 + scale) + shift).astype(BF16)
        h2_r[0] = h2
        ab = jnp.dot(h2, wu_r[...], preferred_element_type=F32)
        a, bb = ab[:, :D_FF], ab[:, D_FF:]
        sa = _sig(a)
        silu_a = a * sa
        hid = (silu_a * bb).astype(BF16)
        hid_r[0] = hid
        ffn = jnp.dot(hid, wd_r[...], preferred_element_type=F32)
        x2 = x1 + gate * ffn
        xn3, r3 = _rms(x2)
        err = xn3 * gl - tg_r[0]
        loss_r[...] += 0.5 * jnp.sum(jnp.mean(err * err, axis=-1, keepdims=True), axis=0, keepdims=True)
        dy = err * (1.0 / D_MODEL)
        dgl_r[...] += _colsum(dy * xn3)
        dx2 = _rms_bwd(dy * gl, xn3, r3)
        dmod_r[0, 3:4, :] += _colsum(dx2 * ffn)
        dffn = (gate * dx2).astype(BF16)
        dffn_r[0] = dffn
        dhid = lax.dot_general(dffn, wd_r[...], (((1,), (1,)), ((), ())), preferred_element_type=F32)
        da = dhid * bb * (sa * (1.0 + a * (1.0 - sa)))
        db = dhid * silu_a
        dab = jnp.concatenate([da, db], axis=1).astype(BF16)
        dab_r[0] = dab
        dh2 = lax.dot_general(dab, wu_r[...], (((1,), (1,)), ((), ())), preferred_element_type=F32)
        dmod_r[0, 1:2, :] += _colsum(dh2)
        dmod_r[0, 2:3, :] += _colsum(dh2 * hn2)
        dhn2 = dh2 * (1.0 + scale)
        dgf_r[...] += _colsum(dhn2 * xn2)
        dx1_r[0] = dx2 + _rms_bwd(dhn2 * gf, xn2, r2)

    lat = lambda w: pl.BlockSpec((1, ROW_BLOCK, w), lambda b, j: (b, j, 0))
    full2 = lambda r, c: pl.BlockSpec((r, c), lambda b, j: (0, 0))
    mod_spec = pl.BlockSpec((1, 8, D_MODEL), lambda b, j: (b, 0, 0))
    return pl.pallas_call(
        body, name="ffn", grid=(bs, nbl),
        in_specs=[lat(D_MODEL), lat(D_MODEL), mod_spec, full2(1, D_MODEL), full2(1, D_MODEL),
                  full2(D_MODEL, 2 * D_FF), full2(D_FF, D_MODEL)],
        out_specs=[lat(D_MODEL), lat(D_MODEL), lat(2 * D_FF), lat(D_FF), lat(D_MODEL),
                   full2(1, 1), full2(1, D_MODEL), full2(1, D_MODEL), mod_spec],
        out_shape=[SDS(x1.shape, F32), SDS(x1.shape, BF16), SDS((bs, seq, 2 * D_FF), BF16),
                   SDS((bs, seq, D_FF), BF16), SDS(x1.shape, BF16), SDS((1, 1), F32),
                   SDS((1, D_MODEL), F32), SDS((1, D_MODEL), F32), SDS((bs, 8, D_MODEL), F32)],
        compiler_params=_params(2),
    )(x1, target, mod_c, g_ffn, g_final, w_up, w_down)


def _row_tile(rows, most):
    return next(m * ROW_BLOCK for m in (9, 8, 4, 2, 1) if m <= most and rows % (m * ROW_BLOCK) == 0)


def _matmul_tn(a, b, n_blocks, tk, name, carry=None):
    t, m = a.shape
    n = b.shape[1]
    tn = n // n_blocks

    def body(a_ref, b_ref, o_ref):
        @pl.when(pl.program_id(1) == 0)
        def _():
            o_ref[...] = jnp.zeros_like(o_ref)
        o_ref[0] += _dot_tn(a_ref[...], b_ref[...])

    (out,), carried = _host_call(
        body, name=name, grid=(n_blocks, t // tk),
        in_specs=[pl.BlockSpec((tk, m), lambda i, k: (k, 0)), pl.BlockSpec((tk, tn), lambda i, k: (k, i))],
        out_specs=[pl.BlockSpec((1, m, tn), lambda i, k: (i, 0, 0))],
        out_shape=[SDS((n_blocks, m, tn), F32)], args=(a, b), carry=carry)
    return out if carry is None else (out, carried)


SMALL_ROWS = 80
ROW_CCTX = 3


def _small_reduce(gathered, lbg):
    def body(g_ref, lbg_ref, s_ref, dgam_ref):
        tot = g_ref[0:SMALL_ROWS, :]
        for dev in range(1, N_DEV):
            tot = tot + g_ref[dev * SMALL_ROWS:(dev + 1) * SMALL_ROWS, :]
        s_ref[...] = tot
        cc = g_ref[ROW_CCTX:ROW_CCTX + 1, :]
        for dev in range(2, N_DEV, 2):
            cc = cc + g_ref[dev * SMALL_ROWS + ROW_CCTX:dev * SMALL_ROWS + ROW_CCTX + 1, :]
        s_ref[ROW_CCTX:ROW_CCTX + 1, :] = cc
        dlb = tot[7:8, :]
        for d in range(2):
            s0 = _sig(lbg_ref[0, d:d + 1, :] - lbg_ref[1, d:d + 1, :])
            dgam_ref[d:d + 1, :] = dlb[:, d * KW:(d + 1) * KW] * s0 * (1.0 - s0)

    return pl.pallas_call(
        body, name="small_reduce", out_shape=[SDS((SMALL_ROWS, D_MODEL), F32), SDS((2, KW), F32)],
        in_specs=[VMEM_SPEC] * 2, out_specs=[VMEM_SPEC] * 2,
    )(gathered, lbg)


def _pad_cols(a, width):
    return jnp.pad(a, ((0, 0), (0, width - a.shape[1])))


def kernel(x, c, ctx, c_ctx, w_mod, b_mod, g_mix, g_ffn, w_in, lb_gamma, g_norm_a, ln_v_g, ln_v_b, w_s, b_s, w_pa, w_pb, w_o, w_up, w_down, g_final, loss_target, m_c_ctx, m_w_mod, m_b_mod, m_g_mix, m_g_ffn, m_w_in, m_lb_gamma, m_g_norm_a, m_ln_v_g, m_ln_v_b, m_w_s, m_b_s, m_w_pa, m_w_pb, m_w_o, m_w_up, m_w_down, m_g_final, v_c_ctx, v_w_mod, v_b_mod, v_g_mix, v_g_ffn, v_w_in, v_lb_gamma, v_g_norm_a, v_ln_v_g, v_ln_v_b, v_w_s, v_b_s, v_w_pa, v_w_pb, v_w_o, v_w_up, v_w_down, v_g_final):
    ax, ay, ac = lax.axis_index("x"), lax.axis_index("y"), lax.axis_index("c")
    kc = 2 * ax + ay
    dev = 2 * kc + ac
    pos = jnp.stack([kc, ac]).astype(jnp.int32)
    bs, seq, _ = x.shape
    assert bs <= 4 and ctx.shape[1] == ROW_BLOCK and seq % ROW_BLOCK == 0
    mod_cols = w_mod.shape[2]

    lbg_row = _pad_cols(lb_gamma.reshape(1, -1), D_MODEL)
    pay1 = jnp.concatenate([c, jnp.zeros((4 - bs, D_MODEL), F32), c_ctx[None, :], lbg_row,
                            jnp.zeros((2, D_MODEL), F32)], axis=0)
    b_mod_s = lax.dynamic_slice(b_mod, (0, kc * mod_cols), (1, mod_cols))
    cond64, mod_g = _cond_and_mod(pay1, w_mod[0], b_mod_s)
    lbg_full = cond64.reshape(N_DEV, 8, D_MODEL)[0::2, 5, :KW].reshape(N_CHIPS, 2, 2, HEAD_DIM)
    lbg_full = jnp.transpose(lbg_full, (1, 2, 0, 3)).reshape(2, 2, KW)
    mod_g = mod_g.reshape(N_DEV, 64, mod_cols)[0::2]
    shards = [w_in[0], w_up[0], w_pa[0], w_pb[0], w_o[0], w_down[0]]
    bufs = _cast_bf16(pos, shards)
    mod_full = jnp.transpose(mod_g, (1, 0, 2)).reshape(64, N_CHIPS * mod_cols)
    mod_mine = lax.dynamic_slice(mod_full, (dev * 8, 0), (8, 6 * D_MODEL)).reshape(8, 6, D_MODEL)
    mod, mc = mod_mine[:bs], mod_mine[4]
    zeros4 = jnp.zeros((bs, 4, D_MODEL), F32)
    mod_a = jnp.concatenate([mod[:, 0:2], jnp.broadcast_to(mc[None, 0:2], (bs, 2, D_MODEL)), zeros4], axis=1)
    mod_c = jnp.concatenate([mod[:, 2:6], zeros4], axis=1)

    def cols_major(a):
        return jnp.transpose(a, (1, 0, 2)).reshape(a.shape[1], -1)

    gna, lng, lnb = g_norm_a, ln_v_g, ln_v_b
    ws3 = w_s[0]
    ws3_t = jnp.transpose(ws3, (0, 2, 1))
    bst = jnp.transpose(b_s[0])

    (p, h_all), (w_in_g,) = _in_fwd_own(pos, x, ctx, g_mix, mod_a, w_in[0], carry=_carry_gather_send(bufs[:1]),
                                        then=_carry_gather_forward(bufs[:1]))
    p, sent_up = _in_fwd_rest(pos, h_all, w_in_g, p, carry=_carry_gather_send(bufs[1:2]))
    w_in_f = cols_major(w_in_g)
    fwd_rest = _carry_gather_forward(bufs[2:])
    then_rest = _Carry([], [], {}, fwd_rest.sems, lambda i, o, s: fwd_rest.copies(i[1:], o[1:], s))
    (o_f, s_f, o_b, s_b), gathered = _hgrn_fwd(
        p, lbg_full, carry=_merge_carries(_carry_gather_forward(sent_up), _carry_gather_send(bufs[2:])),
        then=then_rest)
    w_up_f, w_pa_f, w_pb_f = cols_major(gathered[0]), gathered[1], gathered[2]
    w_o_f = gathered[3].reshape(-1, D_MODEL)
    w_down_f = gathered[4].reshape(-1, D_MODEL)
    x1 = _mix_fwd(p, o_f, o_b, x, mod_c, gna, lng, lnb, ws3, bst, w_pa_f, w_pb_f, w_o_f)
    dx1, h2, dab, hid, dffn, loss_part, dg_final, dg_ffn, dmod_ffn = _ffn(
        x1, loss_target, mod_c, g_ffn, g_final[None, :], w_up_f, w_down_f)
    rows_lat = bs * seq
    dw_up = _matmul_tn(h2.reshape(rows_lat, D_MODEL), dab.reshape(rows_lat, 2 * D_FF), N_CHIPS,
                       _row_tile(rows_lat, 9), "dw_up")
    dw_down = _matmul_tn(hid.reshape(rows_lat, D_FF), dffn.reshape(rows_lat, D_MODEL), 1,
                         _row_tile(rows_lat, 4), "dw_down")

    part_ffn = [dw_up, dw_down.reshape(N_CHIPS, -1, D_MODEL)]
    (do_raw, dpc, dw_pa, dw_pb, dw_o, dgna, dlng, dlnb, dws, dbst, dmod_mix), sib_ffn = _mix_bwd(
        p, o_f, o_b, dx1, mod_c, gna, lng, lnb, ws3, ws3_t, bst, w_pa_f, w_pb_f, w_o_f,
        carry=_carry_sibling_halves(part_ffn))
    cpbf_ffn = _rs_add_halves(pos, part_ffn, sib_ffn, "rs_add_ffn")
    part_mix = [dw_pa, dw_pb, dw_o.reshape(N_CHIPS, -1, D_MODEL)]
    (df_f, dq_f, dv_f, dlb0, df_b, dq_b, dv_b, dlb1), got = _hgrn_bwd_pair(
        p, lbg_full, (s_f, s_b), do_raw,
        carry=_merge_carries(_carry_to_owner(cpbf_ffn), _carry_sibling_halves(part_mix)))
    own_ffn, sib_mix = got[:2], got[2:]
    half_ffn = _rs_sum_owner(pos, part_ffn, sib_ffn, own_ffn, "rs_sum_ffn")
    cpbf_mix = _rs_add_halves(pos, part_mix, sib_mix, "rs_add_mix")
    (grad_x, dp, dg_mix, dmod_in), _ = _in_bwd(x, ctx, dx1, g_mix, mod_a, w_in_f, df_f, df_b, dv_f, dv_b, dq_f, dq_b,
                                               dpc)

    rows_all = dp.shape[0] * dp.shape[1]
    tk_all = _row_tile(rows_all, 9)
    dw_in, got = _matmul_tn(h_all.reshape(rows_all, D_MODEL), dp.reshape(rows_all, IN_COLS), N_CHIPS, tk_all, "dw_in",
                            carry=_merge_carries(_carry_join_halves(half_ffn), _carry_to_owner(cpbf_mix)))
    g_ffn_w, own_mix = got[:2], got[2:]
    half_mix = _rs_sum_owner(pos, part_mix, sib_mix, own_mix, "rs_sum_mix")

    dmod_mine = jnp.concatenate([dmod_in[:, 0], dmod_in[:, 1], dmod_mix[:, 0], dmod_ffn[:, 1], dmod_ffn[:, 2],
                                 dmod_ffn[:, 3]], axis=1)
    dmc = jnp.concatenate([jnp.sum(dmod_in[:, 2], axis=0), jnp.sum(dmod_in[:, 3], axis=0),
                           jnp.zeros((4 * D_MODEL,), F32)])[None, :]
    sib_sems, dw_in_thru, sib_land, sib_token = _sibling_send_start(dw_in, after=dg_mix)
    pay3 = jnp.concatenate([dmod_mine, jnp.zeros((4 - bs, 6 * D_MODEL), F32), dmc + sib_token[0, 0],
                            jnp.zeros((3, 6 * D_MODEL), F32)], axis=0)
    dmod64, g_mix_w = _all_gather8(pay3, "gather_dmod", carry=_carry_join_halves(half_mix))
    dmod64_my = lax.dynamic_slice(dmod64, (0, kc * mod_cols), (64, mod_cols))
    g_w_mod, g_b_mod, g_cctx_part = _mod_bwd(cond64, dmod64, dmod64_my, w_mod[0], c_ctx[None, :])

    def row(*parts):
        return _pad_cols(jnp.concatenate([q.reshape(1, -1) for q in parts], axis=1), D_MODEL)

    small_rows = [dg_mix, dg_ffn, dg_final, g_cctx_part, row(dgna), row(dlng, dlnb), row(jnp.transpose(dbst)),
                  row(dlb0, dlb1), row(loss_part), jnp.zeros((7, D_MODEL), F32), dws.reshape(64, D_MODEL)]
    pay4 = jnp.concatenate(small_rows, axis=0)
    tot, dgam0 = _small_reduce(_all_gather8(pay4, "gather_small"), lbg_full)
    dw_in, sib_one = _sibling_send_wait(sib_sems, dw_in_thru, sib_land, after=(tot,))
    sib_in = [sib_one]
    cpbf_in = _rs_add_halves(pos, [dw_in], sib_in, "rs_add_w_in")

    own_sems, own_src, own_land, own_token = _owner_send_start(cpbf_in[0], after=tot)

    rest_names = ["w_up", "w_pa", "w_pb", "w_o", "w_down", "w_mod"]
    rest_w = shards[1:] + [w_mod[0]]
    rest_g = [g_ffn_w[0], g_mix_w[0] + own_token[0, 0], g_mix_w[1], g_mix_w[2], g_ffn_w[1], g_w_mod]
    rest_m = [m_w_up[0], m_w_pa[0], m_w_pb[0], m_w_o[0], m_w_down[0], m_w_mod[0]]
    rest_v = [v_w_up[0], v_w_pa[0], v_w_pb[0], v_w_o[0], v_w_down[0], v_w_mod[0]]
    (ds_r, m2s_r, v2s_r), _ = _adamw_group(rest_w, rest_g, rest_m, rest_v, "adamw_rest")
    res = {}
    for name, g, d, m2, v2 in zip(rest_names, rest_g, ds_r, m2s_r, v2s_r):
        res[name] = (g[None], d[None], m2[None], v2[None])
    own_in = [_owner_send_wait(own_sems, own_src, own_land, after=(ds_r[0],))]
    g_in_w = _comm_call("rs_join_w_in",
                        _carry_join_halves(_rs_sum_owner(pos, [dw_in], sib_in, own_in, "rs_sum_w_in")))
    d, m2, v2 = _adamw_big(shards[0], g_in_w[0], m_w_in[0], v_w_in[0], "adamw_w_in")
    res["w_in"] = (g_in_w[0][None], d[None], m2[None], v2[None])

    loss = tot[8, 0]
    dgam_full = jnp.stack([dgam0, -dgam0])
    g_lbg = lax.dynamic_slice(dgam_full, (0, 0, kc * HEAD_DIM), (2, 2, HEAD_DIM))

    small = [
        ("c_ctx", c_ctx[None, :], tot[3:4], m_c_ctx, v_c_ctx),
        ("b_mod", b_mod, g_b_mod, m_b_mod, v_b_mod),
        ("g_mix", g_mix, tot[0:1], m_g_mix, v_g_mix),
        ("g_ffn", g_ffn, tot[1:2], m_g_ffn, v_g_ffn),
        ("lb_gamma", lb_gamma.reshape(4, HEAD_DIM), g_lbg.reshape(4, HEAD_DIM), m_lb_gamma, v_lb_gamma),
        ("g_norm_a", g_norm_a, tot[4:5, :HEAD_DIM], m_g_norm_a, v_g_norm_a),
        ("ln_v_g", ln_v_g, tot[5:6, :KW], m_ln_v_g, v_ln_v_g),
        ("ln_v_b", ln_v_b, tot[5:6, KW:], m_ln_v_b, v_ln_v_b),
        ("w_s", w_s.reshape(N_HEADS * SGU_CHUNK, SGU_CHUNK), tot[16:80].reshape(N_HEADS * SGU_CHUNK, SGU_CHUNK),
         m_w_s, v_w_s),
        ("b_s", b_s[0], tot[6:7, :KW].reshape(N_HEADS, SGU_CHUNK), m_b_s, v_b_s),
        ("g_final", g_final[None, :], tot[2:3], m_g_final, v_g_final),
    ]
    ws_, gs_ = [s[1] for s in small], [s[2] for s in small]
    ms_ = [s[3].reshape(s[1].shape) for s in small]
    vs_ = [s[4].reshape(s[1].shape) for s in small]
    ds_, m2s_, v2s_ = _adamw_small(ws_, gs_, ms_, vs_)
    for (name, _, g, m, _), d, m2, v2 in zip(small, ds_, m2s_, v2s_):
        res[name] = tuple(t.reshape(m.shape) for t in (g, d, m2, v2))

    order = ["c_ctx", "w_mod", "b_mod", "g_mix", "g_ffn", "w_in", "lb_gamma", "g_norm_a", "ln_v_g", "ln_v_b",
             "w_s", "b_s", "w_pa", "w_pb", "w_o", "w_up", "w_down", "g_final"]
    outs = [loss, grad_x]
    for part in range(4):
        outs += [res[n][part] for n in order]
    return tuple(outs)
```

```python
import functools
import math

import jax
import jax.numpy as jnp
import numpy as np
from jax import lax
from jax.experimental import pallas as pl
from jax.experimental.pallas import tpu as pltpu

F32 = jnp.float32
BF16 = jnp.bfloat16
SDS = jax.ShapeDtypeStruct
MESH = pl.DeviceIdType.MESH

EPS = 1e-6
D_MODEL = 1024
N_HEADS = 4
HEAD_DIM = 128
KW = N_HEADS * HEAD_DIM
IN_COLS = 11 * KW
D_FF = 2816
HGRN_CHUNK = 64
SGU_CHUNK = 128
ROW_BLOCK = 256
N_CHIPS = 4
N_DEV = 8
V7X_VMEM_BYTES = 64 * 1024 * 1024
VMEM_LIMIT = V7X_VMEM_BYTES - 6 * 1024 * 1024

ADAM_LR, ADAM_B1, ADAM_B2, ADAM_EPS, ADAM_WD, ADAM_STEP = 0.001, 0.9, 0.999, 1e-08, 0.01, 10
GELU_C0 = math.sqrt(2.0 / math.pi)
GELU_C1 = 0.044715

VMEM_SPEC = pl.BlockSpec(memory_space=pltpu.VMEM)
ANY_SPEC = pl.BlockSpec(memory_space=pl.ANY)


def _params(n_grid):
    return pltpu.CompilerParams(dimension_semantics=("arbitrary",) * n_grid, vmem_limit_bytes=VMEM_LIMIT)


def _sig(x):
    return 0.5 * jnp.tanh(0.5 * x) + 0.5


def _gelu(x):
    t = jnp.tanh(GELU_C0 * (x + GELU_C1 * x * x * x))
    return 0.5 * x * (1.0 + t), t


def _dgelu(x, t):
    return 0.5 * (1.0 + t) + 0.5 * x * (1.0 - t * t) * GELU_C0 * (1.0 + 3.0 * GELU_C1 * x * x)


def _dot(a, b):
    return jnp.dot(a.astype(BF16), b.astype(BF16), preferred_element_type=F32)


def _dot_nt(a, b):
    return lax.dot_general(a.astype(BF16), b.astype(BF16), (((1,), (1,)), ((), ())), preferred_element_type=F32)


def _dot_tn(a, b):
    return lax.dot_general(a.astype(BF16), b.astype(BF16), (((0,), (0,)), ((), ())), preferred_element_type=F32)


def _dot_f32(a, b, dims=(((1,), (0,)), ((), ()))):
    return lax.dot_general(a, b, dims, precision=lax.Precision.HIGHEST, preferred_element_type=F32)


def _rms(x):
    r = lax.rsqrt(jnp.mean(x * x, axis=-1, keepdims=True) + EPS)
    return x * r, r


def _rms_bwd(dxn, xn, r):
    return r * (dxn - xn * jnp.mean(dxn * xn, axis=-1, keepdims=True))


def _colsum(a):
    return jnp.sum(a, axis=0, keepdims=True)


def _tri(n, upper):
    t = lax.broadcasted_iota(jnp.int32, (n, n), 0)
    s = lax.broadcasted_iota(jnp.int32, (n, n), 1)
    return (s >= t) if upper else (s <= t)


def _all_gather8(x_shard, name, carry=None, then=None):
    m_per, n = x_shard.shape
    n_ci = len(carry.ins) if carry else 0
    n_co = len(carry.outs) if carry else 0
    n_cs = len(carry.sems) if carry else 0

    def body(*refs):
        x_ref, cins = refs[0], refs[1:1 + n_ci]
        out_ref, couts = refs[1 + n_ci], refs[2 + n_ci:2 + n_ci + n_co]
        send_sems, recv_sems, local_sem = refs[2 + n_ci + n_co:5 + n_ci + n_co]
        csems = refs[5 + n_ci + n_co:5 + n_ci + n_co + n_cs]
        tsems = refs[5 + n_ci + n_co + n_cs:]
        if carry:
            _start_all(carry.copies(cins, couts, csems)[0])
        _gather8_body(x_ref, out_ref, send_sems, recv_sems, local_sem, m_per)
        if carry:
            _wait_all(carry.copies(cins, couts, csems)[1])
        if then:
            _start_all(then.copies(couts, couts, tsems)[0])
            _wait_all(then.copies(couts, couts, tsems)[1])

    res = pl.pallas_call(
        body, name=name, out_shape=[SDS((N_DEV * m_per, n), x_shard.dtype)] + (carry.outs if carry else []),
        in_specs=[VMEM_SPEC] + [ANY_SPEC] * n_ci, out_specs=[VMEM_SPEC] + [ANY_SPEC] * n_co,
        input_output_aliases={1 + i: 1 + o for i, o in carry.alias.items()} if carry else {},
        scratch_shapes=[pltpu.SemaphoreType.DMA((7,)), pltpu.SemaphoreType.DMA((7,)), pltpu.SemaphoreType.DMA]
        + (carry.sems if carry else []) + (then.sems if then else []),
    )(x_shard, *(carry.ins if carry else []))
    return res[0] if carry is None else (res[0], list(res[1:]))


def _gather8_body(x_ref, out_ref, send_sems, recv_sems, local_sem, m_per):
    x, y, c = lax.axis_index("x"), lax.axis_index("y"), lax.axis_index("c")
    me, sibling = (x, y, c), (x, y, 1 - c)
    chips = [(1 - x, y), (x, 1 - y), (1 - x, 1 - y)]

    def rows(px, py, pc):
        return out_ref.at[pl.ds((4 * px + 2 * py + pc) * m_per, m_per), :]

    def copy(k, block, to, src=None):
        return pltpu.make_async_remote_copy(
            src_ref=rows(*block) if src is None else src, dst_ref=rows(*block),
            send_sem=send_sems.at[k], recv_sem=recv_sems.at[k], device_id=to, device_id_type=MESH)

    mine = pltpu.make_async_copy(x_ref, rows(*me), local_sem)
    mine.start()
    first = [copy(0, me, sibling, src=x_ref)]
    first += [copy(1 + j, me, (*chip, c), src=x_ref) for j, chip in enumerate(chips)]
    for cp in first:
        cp.start()
    passed = [copy(4 + j, (*chip, c), sibling) for j, chip in enumerate(chips)]
    for j, chip in enumerate(chips):
        copy(1 + j, (*chip, c), me).wait_recv()
        passed[j].start()
    copy(0, sibling, me).wait_recv()
    for j, chip in enumerate(chips):
        copy(4 + j, (*chip, 1 - c), me).wait_recv()
    for cp in first + passed:
        cp.wait_send()
    mine.wait()


def _mesh_pos():
    x, y, c = lax.axis_index("x"), lax.axis_index("y"), lax.axis_index("c")
    chips = [(1 - x, y), (x, 1 - y), (1 - x, 1 - y)]
    return x, y, c, 2 * x + y, (x, y, 1 - c), chips


def _half_rows(c, rh):
    return pl.ds(pl.multiple_of(c * rh, 16), rh)


class _Carry:
    def __init__(self, ins, outs, alias, sems, copies):
        self.ins, self.outs, self.alias, self.sems, self.copies = list(ins), list(outs), dict(alias), list(sems), copies


def _remote(src, dst, send, recv, to):
    return functools.partial(pltpu.make_async_remote_copy, src_ref=src, dst_ref=dst, send_sem=send, recv_sem=recv,
                             device_id=to, device_id_type=MESH)


def _carry_gather_send(bufs):
    n = len(bufs)

    def copies(ins, outs, sems):
        x, y, c, kc, sibling, chips = _mesh_pos()
        starts, waits = [], []
        for wi in range(n):
            rh = outs[wi].shape[1] // 2
            for jj, chip in enumerate(chips):
                mine = outs[wi].at[kc, _half_rows(c, rh), :]
                cp = _remote(mine, mine, sems[0].at[wi, jj], sems[1].at[wi, jj], (*chip, c))
                starts.append(cp)
                waits.append((cp, "send"))
                theirs = outs[wi].at[2 * chip[0] + chip[1], _half_rows(c, rh), :]
                waits.append((_remote(theirs, theirs, sems[0].at[wi, jj], sems[1].at[wi, jj], (*chip, c)), "recv"))
        return starts, waits

    return _Carry(bufs, [SDS(b.shape, b.dtype) for b in bufs], {i: i for i in range(n)},
                  [pltpu.SemaphoreType.DMA((n, 3)), pltpu.SemaphoreType.DMA((n, 3))], copies)


def _carry_gather_forward(bufs):
    n = len(bufs)

    def copies(ins, outs, sems):
        x, y, c, kc, sibling, chips = _mesh_pos()
        starts, waits = [], []
        for wi in range(n):
            rh = outs[wi].shape[1] // 2
            for jj, chip in enumerate(chips):
                got = outs[wi].at[2 * chip[0] + chip[1], _half_rows(c, rh), :]
                cp = _remote(got, got, sems[0].at[wi, jj], sems[1].at[wi, jj], sibling)
                starts.append(cp)
                waits.append((cp, "send"))
                other = outs[wi].at[2 * chip[0] + chip[1], _half_rows(1 - c, rh), :]
                waits.append((_remote(other, other, sems[0].at[wi, jj], sems[1].at[wi, jj], sibling), "recv"))
        return starts, waits

    return _Carry(bufs, [SDS(b.shape, b.dtype) for b in bufs], {i: i for i in range(n)},
                  [pltpu.SemaphoreType.DMA((n, 3)), pltpu.SemaphoreType.DMA((n, 3))], copies)


def _carry_sibling_halves(grads):
    n = len(grads)

    def copies(ins, outs, sems):
        x, y, c, kc, sibling, chips = _mesh_pos()
        cps = [_remote(ins[wi].at[:, _half_rows(1 - c, ins[wi].shape[1] // 2), :], outs[wi],
                       sems[0].at[wi], sems[1].at[wi], sibling) for wi in range(n)]
        return cps, [(cp, "both") for cp in cps]

    return _Carry(grads, [SDS((N_CHIPS, g.shape[1] // 2, g.shape[2]), F32) for g in grads], {},
                  [pltpu.SemaphoreType.DMA((n,)), pltpu.SemaphoreType.DMA((n,))], copies)


def _carry_to_owner(cpbfs):
    n = len(cpbfs)

    def copies(ins, outs, sems):
        x, y, c, kc, sibling, chips = _mesh_pos()
        starts, waits = [], []
        for wi in range(n):
            for jj, chip in enumerate(chips):
                cp = _remote(ins[wi].at[2 * chip[0] + chip[1]], outs[wi].at[kc],
                             sems[0].at[wi, jj], sems[1].at[wi, jj], (*chip, c))
                starts.append(cp)
                waits.append((cp, "send"))
                slot = outs[wi].at[2 * chip[0] + chip[1]]
                waits.append((_remote(slot, slot, sems[0].at[wi, jj], sems[1].at[wi, jj], (*chip, c)), "recv"))
        return starts, waits

    return _Carry(cpbfs, [SDS(g.shape, BF16) for g in cpbfs], {},
                  [pltpu.SemaphoreType.DMA((n, 3)), pltpu.SemaphoreType.DMA((n, 3))], copies)


def _carry_join_halves(bufs):
    n = len(bufs)

    def copies(ins, outs, sems):
        x, y, c, kc, sibling, chips = _mesh_pos()
        cps = []
        for wi in range(n):
            mine = outs[wi].at[_half_rows(c, outs[wi].shape[0] // 2), :]
            cps.append(_remote(mine, mine, sems[0].at[wi], sems[1].at[wi], sibling))
        return cps, [(cp, "both") for cp in cps]

    return _Carry(bufs, [SDS(b.shape, F32) for b in bufs], {i: i for i in range(n)},
                  [pltpu.SemaphoreType.DMA((n,)), pltpu.SemaphoreType.DMA((n,))], copies)


def _merge_carries(*carries):
    ins, outs, alias, sems, parts = [], [], {}, [], []
    for cy in carries:
        parts.append((len(ins), len(cy.ins), len(outs), len(cy.outs), len(sems), len(cy.sems), cy.copies))
        alias.update({len(ins) + i: len(outs) + o for i, o in cy.alias.items()})
        ins += cy.ins
        outs += cy.outs
        sems += cy.sems

    def copies(i, o, s):
        starts, waits = [], []
        for i0, ni, o0, no, s0, ns, fn in parts:
            st, wt = fn(i[i0:i0 + ni], o[o0:o0 + no], s[s0:s0 + ns])
            starts += st
            waits += wt
        return starts, waits

    return _Carry(ins, outs, alias, sems, copies)


def _start_all(starts):
    for cp in starts:
        cp().start()


def _wait_all(waits):
    for cp, which in waits:
        if which == "send":
            cp().wait_send()
        elif which == "recv":
            cp().wait_recv()
        else:
            cp().wait()


HBM_SPEC = pl.BlockSpec(memory_space=pltpu.HBM)
SEM_SPEC = pl.BlockSpec(memory_space=pltpu.SEMAPHORE)
SPLIT_COPY_EFFECT = pltpu.SideEffectType.DATAFLOW_SIDE_EFFECTING


def _owner_send_start(cpbf, after):
    land = lax.empty(cpbf.shape, cpbf.dtype)

    def body(src_ref, land_ref, after_ref, s0, s1, s2, r0, r1, r2, src_thru, land_thru, token):
        x, y, c, kc, sibling, chips = _mesh_pos()
        for jj, (chip, s_sem, r_sem) in enumerate(zip(chips, (s0, s1, s2), (r0, r1, r2))):
            pltpu.make_async_remote_copy(
                src_ref=src_ref.at[2 * chip[0] + chip[1]], dst_ref=land_ref.at[kc], send_sem=s_sem, recv_sem=r_sem,
                device_id=(*chip, c), device_id_type=MESH).start()
        token[...] = jnp.zeros_like(token)

    buf = pltpu.HBM(cpbf.shape, cpbf.dtype)
    outs = pl.pallas_call(
        body, name="rs_owner_w_in_start",
        out_shape=(pltpu.SemaphoreType.DMA(()),) * 6 + (buf, buf, SDS((8, 128), F32)),
        in_specs=(HBM_SPEC, HBM_SPEC, ANY_SPEC), out_specs=(SEM_SPEC,) * 6 + (HBM_SPEC, HBM_SPEC, VMEM_SPEC),
        input_output_aliases={0: 6, 1: 7},
        compiler_params=pltpu.CompilerParams(has_side_effects=SPLIT_COPY_EFFECT),
    )(pltpu.with_memory_space_constraint(cpbf, pltpu.HBM), pltpu.with_memory_space_constraint(land, pltpu.HBM), after)
    return outs[:6], outs[6], outs[7], outs[8]


def _owner_send_wait(sems, src_thru, land_thru, after):
    n_after = len(after)

    def body(*refs):
        src_ref, land_ref = refs[0], refs[1]
        sends, recvs = refs[2:5], refs[5:8]
        x, y, c, kc, sibling, chips = _mesh_pos()
        for jj, chip in enumerate(chips):
            slot = 2 * chip[0] + chip[1]
            cp = pltpu.make_async_remote_copy(
                src_ref=src_ref.at[slot], dst_ref=land_ref.at[slot], send_sem=sends[jj], recv_sem=recvs[jj],
                device_id=(*chip, c), device_id_type=MESH)
            cp.wait_send()
            cp.wait_recv()

    buf = pltpu.HBM(land_thru.shape, land_thru.dtype)
    return pl.pallas_call(
        body, name="rs_owner_w_in_wait", out_shape=(buf, buf),
        in_specs=(HBM_SPEC, HBM_SPEC) + (SEM_SPEC,) * 6 + (ANY_SPEC,) * n_after, out_specs=(HBM_SPEC, HBM_SPEC),
        input_output_aliases={0: 0, 1: 1},
        compiler_params=pltpu.CompilerParams(has_side_effects=SPLIT_COPY_EFFECT),
    )(src_thru, land_thru, *sems, *after)[1]


def _comm_call(name, carry):
    n_i, n_o = len(carry.ins), len(carry.outs)

    def body(*refs):
        ins, outs, sems = refs[:n_i], refs[n_i:n_i + n_o], refs[n_i + n_o:]
        _start_all(carry.copies(ins, outs, sems)[0])
        _wait_all(carry.copies(ins, outs, sems)[1])

    return pl.pallas_call(
        body, name=name, out_shape=carry.outs, in_specs=[ANY_SPEC] * n_i, out_specs=[ANY_SPEC] * n_o,
        input_output_aliases=carry.alias, scratch_shapes=carry.sems,
    )(*carry.ins)


def _host_call(body, *, name, grid, in_specs, out_specs, out_shape, args, scratch_shapes=(), carry=None, then=None,
               prefetch=None, aliases=None):
    n_in, n_out, n_scr = len(in_specs), len(out_specs), len(scratch_shapes)
    n_ci = len(carry.ins) if carry else 0
    n_co = len(carry.outs) if carry else 0
    n_cs = len(carry.sems) if carry else 0
    n_pf = 0 if prefetch is None else 1

    def wrapped(*refs):
        pf, refs = refs[:n_pf], refs[n_pf:]
        ins, cins = refs[:n_in], refs[n_in:n_in + n_ci]
        o0 = n_in + n_ci
        outs, couts = refs[o0:o0 + n_out], refs[o0 + n_out:o0 + n_out + n_co]
        s0 = o0 + n_out + n_co
        scr, sems, tsems = refs[s0:s0 + n_scr], refs[s0 + n_scr:s0 + n_scr + n_cs], refs[s0 + n_scr + n_cs:]
        idx = [pl.program_id(a) for a in range(len(grid))]
        first = functools.reduce(jnp.logical_and, [i == 0 for i in idx])
        last = functools.reduce(jnp.logical_and, [i == g - 1 for i, g in zip(idx, grid)])

        if carry:
            @pl.when(first)
            def _():
                _start_all(carry.copies(cins, couts, sems)[0])

        body(*pf, *ins, *outs, *scr)

        if carry:
            @pl.when(last)
            def _():
                _wait_all(carry.copies(cins, couts, sems)[1])
                if then:
                    _start_all(then.copies(couts, couts, tsems)[0])
                    _wait_all(then.copies(couts, couts, tsems)[1])

    all_in = list(in_specs) + [ANY_SPEC] * n_ci
    all_out = list(out_specs) + [ANY_SPEC] * n_co
    all_scr = list(scratch_shapes) + (carry.sems if carry else []) + (then.sems if then else [])
    alias = {n_pf + i: o for i, o in (aliases or {}).items()}
    if carry:
        alias.update({n_pf + n_in + i: n_out + o for i, o in carry.alias.items()})
    kwargs = dict(name=name, out_shape=list(out_shape) + (carry.outs if carry else []), input_output_aliases=alias,
                  compiler_params=_params(len(grid)))
    if prefetch is None:
        call = pl.pallas_call(wrapped, grid=grid, in_specs=all_in, out_specs=all_out, scratch_shapes=all_scr, **kwargs)
        res = call(*args, *(carry.ins if carry else []))
    else:
        call = pl.pallas_call(wrapped, grid_spec=pltpu.PrefetchScalarGridSpec(
            num_scalar_prefetch=1, grid=grid, in_specs=all_in, out_specs=all_out, scratch_shapes=all_scr), **kwargs)
        res = call(prefetch, *args, *(carry.ins if carry else []))
    return list(res[:n_out]), list(res[n_out:])


def _rs_add_halves(pos, grads, recvs, name):
    n = len(grads)

    def body(pos_ref, *refs):
        for i in range(n):
            refs[2 * n + i][...] = (refs[i][...] + refs[n + i][...]).astype(BF16)

    blks = [(1, g.shape[1] // 4, g.shape[2]) for g in grads]
    mine = [pl.BlockSpec(b, lambda k, i, p: (k, p[1] * 2 + i, 0)) for b in blks]
    half = [pl.BlockSpec(b, lambda k, i, p: (k, i, 0)) for b in blks]
    return pl.pallas_call(
        body, name=name,
        grid_spec=pltpu.PrefetchScalarGridSpec(
            num_scalar_prefetch=1, grid=(N_CHIPS, 2), in_specs=mine + half, out_specs=half),
        out_shape=[SDS((N_CHIPS, g.shape[1] // 2, g.shape[2]), BF16) for g in grads],
        compiler_params=_params(2),
    )(pos, *grads, *recvs)


def _rs_sum_owner(pos, grads, recvs, recv3s, name):
    n = len(grads)

    def body(pos_ref, *refs):
        for i in range(n):
            g, s, r1, r2, r3 = (refs[j * n + i] for j in range(5))
            own = g[0] + s[0]
            refs[5 * n + i][...] = ((own + r1[0].astype(F32)) + r2[0].astype(F32)) + r3[0].astype(F32)

    blks = [(1, g.shape[1] // 4, g.shape[2]) for g in grads]
    mine = [pl.BlockSpec(b, lambda i, p: (p[0], p[1] * 2 + i, 0)) for b in blks]

    def slot(d):
        return [pl.BlockSpec(b, lambda i, p: ((p[0] + d) % N_CHIPS, i, 0)) for b in blks]

    return pl.pallas_call(
        body, name=name,
        grid_spec=pltpu.PrefetchScalarGridSpec(
            num_scalar_prefetch=1, grid=(2,), in_specs=mine + slot(0) + slot(1) + slot(2) + slot(3),
            out_specs=[pl.BlockSpec(b[1:], lambda i, p: (p[1] * 2 + i, 0)) for b in blks]),
        out_shape=[SDS(g.shape[1:], F32) for g in grads],
        compiler_params=_params(1),
    )(pos, *grads, *recvs, *recv3s, *recv3s, *recv3s)


def _cast_bf16(pos, arrs):
    n = len(arrs)

    def body(pos_ref, *refs):
        for i in range(n):
            refs[n + i][0] = refs[i][...].astype(BF16)

    return pl.pallas_call(
        body, name="cast_bf16",
        grid_spec=pltpu.PrefetchScalarGridSpec(
            num_scalar_prefetch=1, grid=(2,),
            in_specs=[pl.BlockSpec((a.shape[0] // 2, a.shape[1]), lambda i, p: (i, 0)) for a in arrs],
            out_specs=[pl.BlockSpec((1, a.shape[0] // 2, a.shape[1]), lambda i, p: (p[0], i, 0)) for a in arrs]),
        out_shape=[SDS((N_CHIPS,) + a.shape, BF16) for a in arrs],
        compiler_params=_params(1),
    )(pos, *arrs)


def _adamw_vals(w, g, m, v):
    m2 = ADAM_B1 * m + (1.0 - ADAM_B1) * g
    v2 = ADAM_B2 * v + (1.0 - ADAM_B2) * (g * g)
    m_hat = m2 / (1.0 - ADAM_B1 ** ADAM_STEP)
    v_hat = v2 / (1.0 - ADAM_B2 ** ADAM_STEP)
    delta = -ADAM_LR * (m_hat / (jnp.sqrt(v_hat) + ADAM_EPS) + ADAM_WD * w)
    return delta, m2, v2


def _adamw_big(w, g, m, v, name):
    rows, cols = w.shape
    rb = rows // 4

    def body(w_ref, g_ref, m_ref, v_ref, d_ref, m2_ref, v2_ref, g_out_ref):
        g = g_ref[...]
        d, m2, v2 = _adamw_vals(w_ref[...], g, m_ref[...], v_ref[...])
        d_ref[...] = d
        m2_ref[...] = m2
        v2_ref[...] = v2
        g_out_ref[...] = g

    spec = pl.BlockSpec((rb, cols), lambda i: (i, 0))
    return pl.pallas_call(
        body, name=name, grid=(4,), in_specs=[spec] * 4, out_specs=[spec] * 4,
        out_shape=[SDS(w.shape, F32)] * 4, compiler_params=_params(1),
    )(w, g, m, v)


ADAMW_GROUP_STEPS = 8


def _adamw_group(ws, gs, ms, vs, name, carry=None):
    n = len(ws)

    def body(*refs):
        for i in range(n):
            g = refs[n + i][...]
            d, m2, v2 = _adamw_vals(refs[i][...], g, refs[2 * n + i][...], refs[3 * n + i][...])
            refs[4 * n + i][...] = d
            refs[5 * n + i][...] = m2
            refs[6 * n + i][...] = v2
            refs[7 * n + i][...] = g

    specs = [pl.BlockSpec((w.shape[0] // ADAMW_GROUP_STEPS, w.shape[1]), lambda i: (i, 0)) for w in ws]
    shapes = [SDS(w.shape, F32) for w in ws]
    outs, carried = _host_call(
        body, name=name, grid=(ADAMW_GROUP_STEPS,), in_specs=specs * 4, out_specs=specs * 4, out_shape=shapes * 4,
        args=(*ws, *gs, *ms, *vs), carry=carry)
    return (outs[:n], outs[n:2 * n], outs[2 * n:3 * n], outs[3 * n:]), carried


def _adamw_small(ws, gs, ms, vs):
    n = len(ws)

    def body(*refs):
        for i in range(n):
            d, m2, v2 = _adamw_vals(refs[i][...], refs[n + i][...], refs[2 * n + i][...], refs[3 * n + i][...])
            refs[4 * n + i][...] = d
            refs[5 * n + i][...] = m2
            refs[6 * n + i][...] = v2

    shapes = [SDS(w.shape, F32) for w in ws]
    outs = pl.pallas_call(
        body, name="adamw_small", out_shape=shapes * 3,
        in_specs=[VMEM_SPEC] * (4 * n), out_specs=[VMEM_SPEC] * (3 * n),
    )(*ws, *gs, *ms, *vs)
    return outs[:n], outs[n:2 * n], outs[2 * n:]


def _cond_and_mod(pay, w_mod_s, b_mod_s):
    m_per, n_cols = pay.shape[0], w_mod_s.shape[1]
    rows = N_DEV * m_per

    def body(pay_ref, w_ref, b_ref, cond_ref, modg_ref, mod_s, sems_a, semr_a, loc_a, sems_b, semr_b, loc_b):
        _gather8_body(pay_ref, cond_ref, sems_a, semr_a, loc_a, m_per)
        cc = cond_ref[...]
        mod_s[...] = _dot_f32(cc * _sig(cc), w_ref[...]) + b_ref[...]
        _gather8_body(mod_s, modg_ref, sems_b, semr_b, loc_b, rows)

    dma7 = pltpu.SemaphoreType.DMA((7,))
    return pl.pallas_call(
        body, name="cond_and_mod",
        out_shape=[SDS((rows, pay.shape[1]), F32), SDS((N_DEV * rows, n_cols), F32)],
        in_specs=[VMEM_SPEC] * 3, out_specs=[VMEM_SPEC] * 2,
        scratch_shapes=[pltpu.VMEM((rows, n_cols), F32), dma7, dma7, pltpu.SemaphoreType.DMA,
                        dma7, dma7, pltpu.SemaphoreType.DMA],
        compiler_params=pltpu.CompilerParams(vmem_limit_bytes=VMEM_LIMIT),
    )(pay, w_mod_s, b_mod_s)


def _mod_bwd(cond64, dmod64, dmod64_my, w_mod_s, c_ctx):
    def body(c_ref, g_ref, gm_ref, w_ref, cc_ref, gw_ref, gb_ref, gcc_ref):
        cc = c_ref[...]
        act = cc * _sig(cc)
        gm = gm_ref[...]
        gw_ref[...] = _dot_f32(act, gm, (((0,), (0,)), ((), ())))
        gb_ref[...] = _colsum(g_ref[...])
        dact = _dot_f32(gm, w_ref[...], (((1,), (1,)), ((), ())))
        tot = dact[4:5, :]
        for dev in range(1, N_DEV):
            tot = tot + dact[8 * dev + 4:8 * dev + 5, :]
        c0 = cc_ref[...]
        s0 = _sig(c0)
        gcc_ref[...] = tot * (s0 * (1.0 + c0 * (1.0 - s0)))

    return pl.pallas_call(
        body, name="mod_bwd",
        out_shape=[SDS(w_mod_s.shape, F32), SDS((1, dmod64.shape[1]), F32), SDS((1, D_MODEL), F32)],
        in_specs=[VMEM_SPEC] * 5, out_specs=[VMEM_SPEC] * 3,
        compiler_params=pltpu.CompilerParams(vmem_limit_bytes=VMEM_LIMIT),
    )(cond64, dmod64, dmod64_my, w_mod_s, c_ctx)


SHARD_COLS = IN_COLS // N_CHIPS


def _in_fwd_own(pos, x, ctx, g_mix, mod_a, w_own, carry=None, then=None):
    bs, seq, _ = x.shape
    nb = seq // ROW_BLOCK + 1

    def body(pos_ref, x_ref, ctx_ref, g_ref, mod_ref, w_ref, p_ref, h_ref, w_bf):
        b, j = pl.program_id(0), pl.program_id(1)

        @pl.when((b == 0) & (j == 0))
        def _():
            w_bf[...] = w_ref[...].astype(BF16)

        is_ctx = j == 0
        xin = jnp.where(is_ctx, ctx_ref[0], x_ref[0])
        shift = jnp.where(is_ctx, mod_ref[0, 2:3, :], mod_ref[0, 0:1, :])
        scale = jnp.where(is_ctx, mod_ref[0, 3:4, :], mod_ref[0, 1:2, :])
        xn, _ = _rms(xin)
        hb = ((xn * g_ref[...]) * (1.0 + scale) + shift).astype(BF16)
        h_ref[0] = hb
        p_ref[0] = jnp.dot(hb, w_bf[...], preferred_element_type=F32)

    return _host_call(
        body, name="in_fwd_own", grid=(bs, nb), prefetch=pos,
        in_specs=[pl.BlockSpec((1, ROW_BLOCK, D_MODEL), lambda b, j, p: (b, jnp.maximum(j - 1, 0), 0)),
                  pl.BlockSpec((1, ROW_BLOCK, D_MODEL), lambda b, j, p: (b, 0, 0)),
                  pl.BlockSpec((1, D_MODEL), lambda b, j, p: (0, 0)),
                  pl.BlockSpec((1, 8, D_MODEL), lambda b, j, p: (b, 0, 0)),
                  pl.BlockSpec((D_MODEL, SHARD_COLS), lambda b, j, p: (0, 0))],
        out_specs=[pl.BlockSpec((1, ROW_BLOCK, SHARD_COLS), lambda b, j, p: (b, j, p[0])),
                   pl.BlockSpec((1, ROW_BLOCK, D_MODEL), lambda b, j, p: (b, j, 0))],
        out_shape=[SDS((bs, nb * ROW_BLOCK, IN_COLS), F32), SDS((bs, nb * ROW_BLOCK, D_MODEL), BF16)],
        scratch_shapes=[pltpu.VMEM((D_MODEL, SHARD_COLS), BF16)],
        args=(x, ctx, g_mix, mod_a, w_own), carry=carry, then=then)


def _in_fwd_rest(pos, h_all, w_in_g, p, carry=None):
    bs, rows, _ = h_all.shape
    rows_all = bs * rows
    tile = next(m * ROW_BLOCK for m in (9, 3, 1) if rows_all % (m * ROW_BLOCK) == 0)

    def body(pos_ref, h_ref, w_ref, p_in_ref, p_ref):
        p_ref[...] = jnp.dot(h_ref[...], w_ref[0], preferred_element_type=F32)

    shard = lambda n, p: (p[0] + 1 + n) % N_CHIPS
    (p2,), carried = _host_call(
        body, name="in_fwd_rest", grid=(N_CHIPS - 1, rows_all // tile), prefetch=pos,
        in_specs=[pl.BlockSpec((tile, D_MODEL), lambda n, t, p: (t, 0)),
                  pl.BlockSpec((1, D_MODEL, SHARD_COLS), lambda n, t, p: (shard(n, p), 0, 0)),
                  ANY_SPEC],
        out_specs=[pl.BlockSpec((tile, SHARD_COLS), lambda n, t, p: (t, shard(n, p)))],
        out_shape=[SDS((rows_all, IN_COLS), F32)], aliases={2: 0},
        args=(h_all.reshape(rows_all, D_MODEL), w_in_g, p.reshape(rows_all, IN_COLS)), carry=carry)
    return p2.reshape(bs, rows, IN_COLS), carried


def _in_bwd(x, ctx, dx1, g_mix, mod_a, w_in, df_f, df_b, dv_f, dv_b, dq_f, dq_b, dpc, carry=None):
    bs, seq, _ = x.shape
    nb = seq // ROW_BLOCK + 1

    def body(x_ref, ctx_ref, dx1_ref, g_ref, mod_ref, w_ref, dff_ref, dfb_ref, dvf_ref, dvb_ref, dqf_ref, dqb_ref,
             dpc_ref, gx_ref, dp_ref, dg_ref, dmod_ref):
        b, j = pl.program_id(0), pl.program_id(1)
        is_ctx = j == 0

        @pl.when((b == 0) & (j == 0))
        def _():
            dg_ref[...] = jnp.zeros_like(dg_ref)

        @pl.when(j == 0)
        def _():
            dmod_ref[...] = jnp.zeros_like(dmod_ref)

        di = (dvf_ref[0] + dvb_ref[0]).astype(BF16)
        dq = (dqf_ref[0] + dqb_ref[0]).astype(BF16)
        dp = jnp.concatenate([dff_ref[0], dfb_ref[0], di, dq, dpc_ref[0]], axis=1)
        dp_ref[0] = dp
        dh = lax.dot_general(dp, w_ref[...], (((1,), (1,)), ((), ())), preferred_element_type=F32)
        xin = jnp.where(is_ctx, ctx_ref[0], x_ref[0])
        scale = jnp.where(is_ctx, mod_ref[0, 3:4, :], mod_ref[0, 1:2, :])
        xn, r = _rms(xin)
        g = g_ref[...]
        hn = xn * g
        d_shift = _colsum(dh)
        d_scale = _colsum(dh * hn)
        dhn = dh * (1.0 + scale)
        dg_ref[...] += _colsum(dhn * xn)
        dx = _rms_bwd(dhn * g, xn, r)

        @pl.when(is_ctx)
        def _():
            dmod_ref[0, 2:3, :] += d_shift
            dmod_ref[0, 3:4, :] += d_scale

        @pl.when(jnp.logical_not(is_ctx))
        def _():
            dmod_ref[0, 0:1, :] += d_shift
            dmod_ref[0, 1:2, :] += d_scale
            gx_ref[0] = dx + dx1_ref[0]

    def rows(w):
        return pl.BlockSpec((1, ROW_BLOCK, w), lambda b, j: (b, j, 0))

    lat = pl.BlockSpec((1, ROW_BLOCK, D_MODEL), lambda b, j: (b, jnp.maximum(j - 1, 0), 0))
    return _host_call(
        body, name="in_bwd", grid=(bs, nb),
        in_specs=[lat, pl.BlockSpec((1, ROW_BLOCK, D_MODEL), lambda b, j: (b, 0, 0)), lat,
                  pl.BlockSpec((1, D_MODEL), lambda b, j: (0, 0)),
                  pl.BlockSpec((1, 8, D_MODEL), lambda b, j: (b, 0, 0)),
                  pl.BlockSpec((D_MODEL, IN_COLS), lambda b, j: (0, 0)),
                  rows(KW), rows(KW), rows(KW), rows(KW), rows(KW), rows(KW), rows(7 * KW)],
        out_specs=[lat, rows(IN_COLS), pl.BlockSpec((1, D_MODEL), lambda b, j: (0, 0)),
                   pl.BlockSpec((1, 8, D_MODEL), lambda b, j: (b, 0, 0))],
        out_shape=[SDS(x.shape, F32), SDS((bs, nb * ROW_BLOCK, IN_COLS), BF16), SDS((1, D_MODEL), F32),
                   SDS((bs, 8, D_MODEL), F32)],
        args=(x, ctx, dx1, g_mix, mod_a, w_in, df_f, df_b, dv_f, dv_b, dq_f, dq_b, dpc), carry=carry)


def _lower_bound(lbg_ref, direction):
    return _sig(lbg_ref[0, direction:direction + 1, :] - lbg_ref[1, direction:direction + 1, :])


def _block_tri(upper):
    t = np.arange(ROW_BLOCK)[:, None]
    s = np.arange(ROW_BLOCK)[None, :]
    same = (t // HGRN_CHUNK) == (s // HGRN_CHUNK)
    return jnp.asarray(same & ((s >= t) if upper else (s <= t)), dtype=BF16)


TRI_SPEC = pl.BlockSpec((ROW_BLOCK, ROW_BLOCK), lambda b, j: (0, 0))


def _tri_matmul_f32(tri, g):
    g0 = g.astype(BF16)
    r1 = g - g0.astype(F32)
    g1 = r1.astype(BF16)
    g2 = (r1 - g1.astype(F32)).astype(BF16)
    return (jnp.dot(tri, g2, preferred_element_type=F32) + jnp.dot(tri, g1, preferred_element_type=F32)) \
        + jnp.dot(tri, g0, preferred_element_type=F32)


def _chunk_rows(rows):
    return jnp.concatenate([jnp.broadcast_to(r, (HGRN_CHUNK, r.shape[1])) for r in rows], axis=0)


def _block_gates(fl, q, lb, tri, upper):
    t = {}
    t["sg"] = _sig(fl)
    t["f"] = lb + (1.0 - lb) * t["sg"]
    k = 1.0 - t["f"]
    bcum = _tri_matmul_f32(tri, jnp.log(t["f"]))
    ends = [bcum[ci * HGRN_CHUNK:ci * HGRN_CHUNK + 1] if upper else bcum[(ci + 1) * HGRN_CHUNK - 1:(ci + 1) * HGRN_CHUNK]
            for ci in range(fl.shape[0] // HGRN_CHUNK)]
    mid = _chunk_rows([0.5 * r for r in ends])
    t["dec"] = [jnp.exp(r) for r in ends]
    t["e1"] = jnp.exp(bcum - mid)
    t["e2"] = jnp.exp(mid - bcum)
    t["eh"] = _chunk_rows([jnp.exp(0.5 * r) for r in ends])
    t["qi"] = q * t["e1"]
    t["ki"] = k * t["e2"]
    t["kd"] = t["ki"] * t["eh"]
    t["qe"] = t["qi"] * t["eh"]
    return t


def _hgrn_block_order(direction, nb):
    if direction == 0:
        return lambda j: j
    return lambda j: jnp.where(j == 0, 0, nb - j)


def _hgrn_fwd(p, lbg, carry=None, then=None):
    bs, rows, _ = p.shape
    nb = rows // ROW_BLOCK
    ncb = ROW_BLOCK // HGRN_CHUNK
    orders = [_hgrn_block_order(d, nb) for d in (0, 1)]
    dirs = (0, 1)

    def body(f0_ref, i0_ref, q0_ref, f1_ref, i1_ref, q1_ref, lbg_ref, tri0_ref, tri1_ref,
             o0_ref, s0_ref, o1_ref, s1_ref, st):
        @pl.when(pl.program_id(1) == 0)
        def _():
            st[...] = jnp.zeros_like(st)

        f_refs, i_refs, q_refs = (f0_ref, f1_ref), (i0_ref, i1_ref), (q0_ref, q1_ref)
        tri_refs, o_refs, s_refs = (tri0_ref, tri1_ref), (o0_ref, o1_ref), (s0_ref, s1_ref)
        chunk = lambda a, ci, h: a[ci * HGRN_CHUNK:(ci + 1) * HGRN_CHUNK, h * HEAD_DIM:(h + 1) * HEAD_DIM]
        masks = [_tri(HGRN_CHUNK, d == 1) for d in dirs]
        t = [_block_gates(f_refs[d][0], q_refs[d][0], _lower_bound(lbg_ref, d), tri_refs[d][...], d == 1) for d in dirs]
        v = [i_refs[d][0] for d in dirs]
        intra = [[[None] * N_HEADS for _ in range(ncb)] for _ in dirs]
        ds_loc = [[[None] * N_HEADS for _ in range(ncb)] for _ in dirs]
        for ci in range(ncb):
            for h in range(N_HEADS):
                for d in dirs:
                    a = jnp.where(masks[d], _dot_nt(chunk(t[d]["qi"], ci, h), chunk(t[d]["ki"], ci, h)), 0.0)
                    intra[d][ci][h] = _dot(a, chunk(v[d], ci, h))
                    ds_loc[d][ci][h] = _dot_tn(chunk(v[d], ci, h), chunk(t[d]["kd"], ci, h))
        for h in range(N_HEADS):
            ls = slice(h * HEAD_DIM, (h + 1) * HEAD_DIM)
            s = [st[d, h] for d in dirs]
            for step in range(ncb):
                for d in dirs:
                    ci = ncb - 1 - step if d == 1 else step
                    s_refs[d][0, 0, ci, h] = s[d]
                    o_refs[d][0, ci * HGRN_CHUNK:(ci + 1) * HGRN_CHUNK, ls] = (
                        intra[d][ci][h] + _dot_nt(chunk(t[d]["qe"], ci, h), s[d]))
                    s[d] = s[d] * t[d]["dec"][ci][:, ls] + ds_loc[d][ci][h]
            for d in dirs:
                st[d, h] = s[d]

    def col(d, cb):
        return pl.BlockSpec((1, ROW_BLOCK, KW), lambda b, j: (b, orders[d](j), cb))

    def outs(d):
        return [pl.BlockSpec((1, ROW_BLOCK, KW), lambda b, j: (b, orders[d](j), 0)),
                pl.BlockSpec((1, 1, ncb, N_HEADS, HEAD_DIM, HEAD_DIM), lambda b, j: (b, orders[d](j), 0, 0, 0, 0))]

    shapes = [SDS((bs, rows, KW), F32), SDS((bs, nb, ncb, N_HEADS, HEAD_DIM, HEAD_DIM), F32)]
    return _host_call(
        body, name="hgrn_fwd", grid=(bs, nb),
        in_specs=[col(0, 0), col(0, 2), col(0, 3), col(1, 1), col(1, 2), col(1, 3),
                  pl.BlockSpec((2, 2, KW), lambda b, j: (0, 0, 0)), TRI_SPEC, TRI_SPEC],
        out_specs=outs(0) + outs(1), out_shape=shapes * 2,
        scratch_shapes=[pltpu.VMEM((2, N_HEADS, HEAD_DIM, HEAD_DIM), F32)],
        args=(p, p, p, p, p, p, lbg, _block_tri(False), _block_tri(True)), carry=carry, then=then)


def _hgrn_bwd_pair(p, lbg, s_saved, do_raw, carry=None):
    bs, rows, _ = p.shape
    nb = rows // ROW_BLOCK
    ncb = ROW_BLOCK // HGRN_CHUNK
    dirs = (0, 1)
    fwd_orders = [_hgrn_block_order(d, nb) for d in dirs]
    orders = [lambda j, d=d: fwd_orders[d](nb - 1 - j) for d in dirs]
    pairs = [(ci, h) for ci in range(ncb) for h in range(N_HEADS)]

    def body(f0_ref, i0_ref, q0_ref, s0_ref, do0_ref, f1_ref, i1_ref, q1_ref, s1_ref, do1_ref,
             lbg_ref, tril_ref, triu_ref,
             df0_ref, dq0_ref, dv0_ref, dlb0_ref, df1_ref, dq1_ref, dv1_ref, dlb1_ref, dst, acc):
        b, j = pl.program_id(0), pl.program_id(1)
        f_refs, i_refs, q_refs = (f0_ref, f1_ref), (i0_ref, i1_ref), (q0_ref, q1_ref)
        s_refs, do_refs = (s0_ref, s1_ref), (do0_ref, do1_ref)
        df_refs, dq_refs, dv_refs, dlb_refs = (df0_ref, df1_ref), (dq0_ref, dq1_ref), (dv0_ref, dv1_ref), (dlb0_ref, dlb1_ref)
        tri_refs, trit_refs = (tril_ref, triu_ref), (triu_ref, tril_ref)

        @pl.when((b == 0) & (j == 0))
        def _():
            dlb0_ref[...] = jnp.zeros_like(dlb0_ref)
            dlb1_ref[...] = jnp.zeros_like(dlb1_ref)

        @pl.when(j == 0)
        def _():
            dst[...] = jnp.zeros_like(dst)

        chunk = lambda a, ci, h: a[ci * HGRN_CHUNK:(ci + 1) * HGRN_CHUNK, h * HEAD_DIM:(h + 1) * HEAD_DIM]
        rows_of = lambda ci: slice(ci * HGRN_CHUNK, (ci + 1) * HGRN_CHUNK)
        lanes_of = lambda h: slice(h * HEAD_DIM, (h + 1) * HEAD_DIM)
        grid3 = lambda: [[[None] * N_HEADS for _ in range(ncb)] for _ in dirs]
        lbs = [_lower_bound(lbg_ref, d) for d in dirs]
        masks = [_tri(HGRN_CHUNK, d == 1) for d in dirs]
        masks_t = [_tri(HGRN_CHUNK, d != 1) for d in dirs]
        t = [_block_gates(f_refs[d][0], q_refs[d][0], lbs[d], tri_refs[d][...], d == 1) for d in dirs]
        v = [i_refs[d][0] for d in dirs]
        do = [do_refs[d][0] for d in dirs]
        a_t, da, da_t, dv_in, ds_loc = (grid3() for _ in range(5))
        for ci, h in pairs:
            for d in dirs:
                a_t[d][ci][h] = _dot_nt(chunk(t[d]["ki"], ci, h), chunk(t[d]["qi"], ci, h))
        for ci, h in pairs:
            for d in dirs:
                da[d][ci][h] = _dot_nt(chunk(do[d], ci, h), chunk(v[d], ci, h))
        for ci, h in pairs:
            for d in dirs:
                da_t[d][ci][h] = _dot_nt(chunk(v[d], ci, h), chunk(do[d], ci, h))
        for ci, h in pairs:
            for d in dirs:
                acc[d, 3, rows_of(ci), lanes_of(h)] = _dot(chunk(do[d], ci, h), s_refs[d][0, 0, ci, h])
        for ci, h in pairs:
            for d in dirs:
                ds_loc[d][ci][h] = _dot_tn(chunk(do[d], ci, h), chunk(t[d]["qe"], ci, h))
        for ci, h in pairs:
            for d in dirs:
                acc[d, 0, rows_of(ci), lanes_of(h)] = _dot(jnp.where(masks[d], da[d][ci][h], 0.0),
                                                           chunk(t[d]["ki"], ci, h))
        for ci, h in pairs:
            for d in dirs:
                acc[d, 1, rows_of(ci), lanes_of(h)] = _dot(jnp.where(masks_t[d], da_t[d][ci][h], 0.0),
                                                           chunk(t[d]["qi"], ci, h))
        for ci, h in pairs:
            for d in dirs:
                dv_in[d][ci][h] = _dot(jnp.where(masks_t[d], a_t[d][ci][h], 0.0), chunk(do[d], ci, h))
        ddec = grid3()
        for h in range(N_HEADS):
            ls = lanes_of(h)
            ds = [dst[d, h] for d in dirs]
            for step in range(ncb):
                for d in dirs:
                    ci = step if d == 1 else ncb - 1 - step
                    acc[d, 2, rows_of(ci), ls] = _dot(chunk(v[d], ci, h), ds[d])
                    acc[d, 4, rows_of(ci), ls] = dv_in[d][ci][h] + _dot_nt(chunk(t[d]["kd"], ci, h), ds[d])
                    ddec[d][ci][h] = _colsum(ds[d] * s_refs[d][0, 0, ci, h])
                    ds[d] = ds[d] * t[d]["dec"][ci][:, ls] + ds_loc[d][ci][h]
            for d in dirs:
                dst[d, h] = ds[d]
        for d in dirs:
            td = t[d]
            dqi, dki, dkd, dqe = (acc[d, i] for i in range(4))
            dq_refs[d][0] = td["e1"] * (dqi + dqe * td["eh"])
            dv_refs[d][0] = acc[d, 4]
            dk = td["e2"] * (dki + dkd * td["eh"])
            dkd_kd = dkd * td["kd"]
            db = dqi * td["qi"] - dki * td["ki"] - dkd_kd + dqe * td["qe"]
            dbl = [_colsum(dkd_kd[rows_of(ci)]) + jnp.concatenate(ddec[d][ci], axis=1) * td["dec"][ci]
                   for ci in range(ncb)]
            dg = _tri_matmul_f32(trit_refs[d][...], db) + _chunk_rows(dbl)
            df = dg / td["f"] - dk
            sg = td["sg"]
            dlb_refs[d][...] += _colsum(df * (1.0 - sg))
            df_refs[d][0] = (df * (1.0 - lbs[d]) * sg * (1.0 - sg)).astype(BF16)

    def ins(d):
        col = lambda cb: pl.BlockSpec((1, ROW_BLOCK, KW), lambda b, j: (b, orders[d](j), cb))
        return [col(d), col(2), col(3),
                pl.BlockSpec((1, 1, ncb, N_HEADS, HEAD_DIM, HEAD_DIM), lambda b, j: (b, orders[d](j), 0, 0, 0, 0)),
                pl.BlockSpec((1, ROW_BLOCK, KW), lambda b, j: (b, orders[d](j), 0))]

    def outs(d):
        row = pl.BlockSpec((1, ROW_BLOCK, KW), lambda b, j: (b, orders[d](j), 0))
        return [row, row, row, pl.BlockSpec((1, KW), lambda b, j: (0, 0))]

    shapes = [SDS((bs, rows, KW), BF16), SDS((bs, rows, KW), F32), SDS((bs, rows, KW), F32), SDS((1, KW), F32)]
    return _host_call(
        body, name="hgrn_bwd", grid=(bs, nb),
        in_specs=ins(0) + ins(1) + [pl.BlockSpec((2, 2, KW), lambda b, j: (0, 0, 0)), TRI_SPEC, TRI_SPEC],
        out_specs=outs(0) + outs(1), out_shape=shapes * 2,
        scratch_shapes=[pltpu.VMEM((2, N_HEADS, HEAD_DIM, HEAD_DIM), F32), pltpu.VMEM((2, 5, ROW_BLOCK, KW), F32)],
        args=(p, p, p, s_saved[0], do_raw, p, p, p, s_saved[1], do_raw, lbg, _block_tri(False), _block_tri(True)),
        carry=carry)


PROJ_COLS = D_MODEL // N_CHIPS


PROJ_SPEC = pl.BlockSpec((N_CHIPS, KW, PROJ_COLS), lambda b, j: (0, 0, 0))


def _dot_shards(a, w_ref):
    return jnp.concatenate([_dot(a, w_ref[s]) for s in range(N_CHIPS)], axis=1)


def _dot_nt_shards(d, w_ref):
    parts = [_dot_nt(d[:, s * PROJ_COLS:(s + 1) * PROJ_COLS], w_ref[s]) for s in range(N_CHIPS)]
    return (parts[0] + parts[1]) + (parts[2] + parts[3])


def _mix_values(og, u, v, ga, gb, o_raw, gna, lng, lnb, ws_ref, bst, wpa, wpb, wo):
    t = {}
    sog = _sig(og)
    t["sog"], t["silu_og"] = sog, og * sog
    xh_l, r_l = [], []
    for h in range(N_HEADS):
        xh, r = _rms(o_raw[:, h * HEAD_DIM:(h + 1) * HEAD_DIM])
        xh_l.append(xh)
        r_l.append(r)
    t["xh"], t["r"] = jnp.concatenate(xh_l, axis=1), r_l
    gna4 = jnp.concatenate([gna] * N_HEADS, axis=1)
    t["gna4"] = gna4
    t["o_n"] = t["xh"] * gna4
    t["o_a"] = t["o_n"] * t["silu_og"]
    t["ya"] = _dot_shards(t["o_a"], wpa)
    t["gu"], t["tu"] = _gelu(u)
    gv, t["tv"] = _gelu(v)
    mu = jnp.mean(gv, axis=-1, keepdims=True)
    cen = gv - mu
    t["rstd"] = lax.rsqrt(jnp.mean(cen * cen, axis=-1, keepdims=True) + EPS)
    t["xhat"] = cen * t["rstd"]
    vn = t["xhat"] * lng + lnb
    t["vn"] = vn
    chunks = []
    for n in range(ROW_BLOCK // SGU_CHUNK):
        rs = slice(n * SGU_CHUNK, (n + 1) * SGU_CHUNK)
        groups = []
        for g in range(N_HEADS):
            ls = slice(g * HEAD_DIM, (g + 1) * HEAD_DIM)
            groups.append(_dot(ws_ref[g], vn[rs, ls]) + bst[:, g:g + 1])
        chunks.append(jnp.concatenate(groups, axis=1))
    t["mixed"] = jnp.concatenate(chunks, axis=0)
    t["o_bm"] = t["gu"] * t["mixed"]
    t["yb"] = _dot_shards(t["o_bm"], wpb)
    t["sa"], t["sb"] = _sig(ga), _sig(gb)
    t["merged"] = t["sa"] * t["ya"] + t["sb"] * t["yb"]
    t["mix"] = _dot(t["merged"], wo)
    return t


def _mix_in_specs(row_of):
    def col(cb):
        return pl.BlockSpec((1, ROW_BLOCK, KW), lambda b, j: (b, row_of(j), cb))
    return [col(cb) for cb in range(4, 11)]


def _mix_param_specs():
    full2 = lambda r, c: pl.BlockSpec((r, c), lambda b, j: (0, 0))
    return [full2(1, HEAD_DIM), full2(1, KW), full2(1, KW),
            pl.BlockSpec((N_HEADS, SGU_CHUNK, SGU_CHUNK), lambda b, j: (0, 0, 0)),
            full2(SGU_CHUNK, N_HEADS), PROJ_SPEC, PROJ_SPEC, full2(D_MODEL, D_MODEL)]


def _mix_fwd(p, o_f, o_b, x, mod_c, gna, lng, lnb, w_s, bst, wpa, wpb, wo):
    bs, seq, _ = x.shape
    nbl = seq // ROW_BLOCK

    def body(og_r, u_r, v_r, ga0_r, ga1_r, gb0_r, gb1_r, of_r, ob_r, x_r, mod_r,
             gna_r, lng_r, lnb_r, ws_r, bst_r, wpa_r, wpb_r, wo_r, x1_r):
        ga = jnp.concatenate([ga0_r[0], ga1_r[0]], axis=1)
        gb = jnp.concatenate([gb0_r[0], gb1_r[0]], axis=1)
        t = _mix_values(og_r[0], u_r[0], v_r[0], ga, gb, of_r[0] + ob_r[0], gna_r[...], lng_r[...], lnb_r[...],
                        ws_r, bst_r[...], wpa_r, wpb_r, wo_r[...])
        x1_r[0] = x_r[0] + mod_r[0, 0:1, :] * t["mix"]

    row = lambda w: pl.BlockSpec((1, ROW_BLOCK, w), lambda b, j: (b, j + 1, 0))
    lat = pl.BlockSpec((1, ROW_BLOCK, D_MODEL), lambda b, j: (b, j, 0))
    return pl.pallas_call(
        body, name="mix_fwd", grid=(bs, nbl),
        in_specs=_mix_in_specs(lambda j: j + 1) + [row(KW), row(KW), lat,
                                                    pl.BlockSpec((1, 8, D_MODEL), lambda b, j: (b, 0, 0))]
        + _mix_param_specs(),
        out_specs=lat, out_shape=SDS(x.shape, F32), compiler_params=_params(2),
    )(p, p, p, p, p, p, p, o_f, o_b, x, mod_c, gna, lng, lnb, w_s, bst, wpa, wpb, wo)


def _mix_bwd(p, o_f, o_b, dx1, mod_c, gna, lng, lnb, w_s, w_s_t, bst, wpa, wpb, wo, carry=None):
    bs, rows, _ = p.shape
    nb = rows // ROW_BLOCK

    def body(og_r, u_r, v_r, ga0_r, ga1_r, gb0_r, gb1_r, of_r, ob_r, dx1_r, mod_r,
             gna_r, lng_r, lnb_r, ws_r, bst_r, wpa_r, wpb_r, wo_r, wst_r,
             dor_r, dpc_r, dwpa_r, dwpb_r, dwo_r, dgna_r, dlng_r, dlnb_r, dws_r, dbst_r, dmod_r):
        b, j = pl.program_id(0), pl.program_id(1)

        @pl.when((b == 0) & (j == 0))
        def _():
            for r in (dwpa_r, dwpb_r, dwo_r, dgna_r, dlng_r, dlnb_r, dws_r, dbst_r):
                r[...] = jnp.zeros_like(r)

        @pl.when(j == 0)
        def _():
            dmod_r[...] = jnp.zeros_like(dmod_r)
            dor_r[...] = jnp.zeros_like(dor_r)
            dpc_r[...] = jnp.zeros_like(dpc_r)

        @pl.when(j > 0)
        def _():
            og, u, v = og_r[0], u_r[0], v_r[0]
            ga = jnp.concatenate([ga0_r[0], ga1_r[0]], axis=1)
            gb = jnp.concatenate([gb0_r[0], gb1_r[0]], axis=1)
            gna, lng = gna_r[...], lng_r[...]
            wpa, wpb, wo = wpa_r, wpb_r, wo_r[...]
            dx1 = dx1_r[0]
            dmix = mod_r[0, 0:1, :] * dx1
            dmerged = _dot_nt(dmix, wo)
            t = _mix_values(og, u, v, ga, gb, of_r[0] + ob_r[0], gna, lng, lnb_r[...],
                            ws_r, bst_r[...], wpa, wpb, wo)
            dmod_r[0, 0:1, :] += _colsum(dx1 * t["mix"])
            dwo_r[...] += _dot_tn(t["merged"], dmix)
            sa, sb = t["sa"], t["sb"]
            dya, dyb = sa * dmerged, sb * dmerged
            dga = dmerged * t["ya"] * sa * (1.0 - sa)
            dgb = dmerged * t["yb"] * sb * (1.0 - sb)
            do_a = _dot_nt_shards(dya, wpa)
            do_bm = _dot_nt_shards(dyb, wpb)
            for s in range(N_CHIPS):
                cs = slice(s * PROJ_COLS, (s + 1) * PROJ_COLS)
                dwpa_r[s] += _dot_tn(t["o_a"], dya[:, cs])
                dwpb_r[s] += _dot_tn(t["o_bm"], dyb[:, cs])
            sog = t["sog"]
            dog = do_a * t["o_n"] * (sog * (1.0 + og * (1.0 - sog)))
            do_n = do_a * t["silu_og"]
            dxh = do_n * t["gna4"]
            prod = do_n * t["xh"]
            dgna = jnp.zeros((1, HEAD_DIM), F32)
            dor_l = []
            for h in range(N_HEADS):
                ls = slice(h * HEAD_DIM, (h + 1) * HEAD_DIM)
                dgna = dgna + _colsum(prod[:, ls])
                dor_l.append(_rms_bwd(dxh[:, ls], t["xh"][:, ls], t["r"][h]))
            dgna_r[...] += dgna
            dor_r[0] = jnp.concatenate(dor_l, axis=1)
            du = do_bm * t["mixed"] * _dgelu(u, t["tu"])
            dmixed = do_bm * t["gu"]
            vn = t["vn"]
            dvn_chunks = []
            for n in range(ROW_BLOCK // SGU_CHUNK):
                rs = slice(n * SGU_CHUNK, (n + 1) * SGU_CHUNK)
                groups = []
                for g in range(N_HEADS):
                    ls = slice(g * HEAD_DIM, (g + 1) * HEAD_DIM)
                    dm = dmixed[rs, ls]
                    dws_r[g] += _dot_nt(dm, vn[rs, ls])
                    dbst_r[:, g:g + 1] += jnp.sum(dm, axis=1, keepdims=True)
                    groups.append(_dot(wst_r[g], dm))
                dvn_chunks.append(jnp.concatenate(groups, axis=1))
            dvn = jnp.concatenate(dvn_chunks, axis=0)
            xhat = t["xhat"]
            dlng_r[...] += _colsum(dvn * xhat)
            dlnb_r[...] += _colsum(dvn)
            dxhat = dvn * lng
            dgv = t["rstd"] * (dxhat - jnp.mean(dxhat, axis=-1, keepdims=True)
                               - xhat * jnp.mean(dxhat * xhat, axis=-1, keepdims=True))
            dv = dgv * _dgelu(v, t["tv"])
            dpc_r[0] = jnp.concatenate([dog, du, dv, dga, dgb], axis=1).astype(BF16)

    row = lambda w: pl.BlockSpec((1, ROW_BLOCK, w), lambda b, j: (b, j, 0))
    lat = pl.BlockSpec((1, ROW_BLOCK, D_MODEL), lambda b, j: (b, jnp.maximum(j - 1, 0), 0))
    full2 = lambda r, c: pl.BlockSpec((r, c), lambda b, j: (0, 0))
    ws_spec = pl.BlockSpec((N_HEADS, SGU_CHUNK, SGU_CHUNK), lambda b, j: (0, 0, 0))
    return _host_call(
        body, name="mix_bwd", grid=(bs, nb),
        in_specs=_mix_in_specs(lambda j: j) + [row(KW), row(KW), lat,
                                                pl.BlockSpec((1, 8, D_MODEL), lambda b, j: (b, 0, 0))]
        + _mix_param_specs() + [ws_spec],
        out_specs=[row(KW), row(7 * KW), PROJ_SPEC, PROJ_SPEC, full2(D_MODEL, D_MODEL),
                   full2(1, HEAD_DIM), full2(1, KW), full2(1, KW), ws_spec, full2(SGU_CHUNK, N_HEADS),
                   pl.BlockSpec((1, 8, D_MODEL), lambda b, j: (b, 0, 0))],
        out_shape=[SDS((bs, rows, KW), F32), SDS((bs, rows, 7 * KW), BF16), SDS((N_CHIPS, KW, PROJ_COLS), F32),
                   SDS((N_CHIPS, KW, PROJ_COLS), F32), SDS((D_MODEL, D_MODEL), F32), SDS((1, HEAD_DIM), F32),
                   SDS((1, KW), F32), SDS((1, KW), F32), SDS((N_HEADS, SGU_CHUNK, SGU_CHUNK), F32),
                   SDS((SGU_CHUNK, N_HEADS), F32), SDS((bs, 8, D_MODEL), F32)],
        args=(p, p, p, p, p, p, p, o_f, o_b, dx1, mod_c, gna, lng, lnb, w_s, bst, wpa, wpb, wo, w_s_t), carry=carry)


def _ffn(x1, target, mod_c, g_ffn, g_final, w_up, w_down):
    bs, seq, _ = x1.shape
    nbl = seq // ROW_BLOCK

    def body(x1_r, tg_r, mod_r, gf_r, gl_r, wu_r, wd_r,
             dx1_r, h2_r, dab_r, hid_r, dffn_r, loss_r, dgl_r, dgf_r, dmod_r):
        b, j = pl.program_id(0), pl.program_id(1)

        @pl.when((b == 0) & (j == 0))
        def _():
            for r in (loss_r, dgl_r, dgf_r):
                r[...] = jnp.zeros_like(r)

        @pl.when(j == 0)
        def _():
            dmod_r[...] = jnp.zeros_like(dmod_r)

        x1 = x1_r[0]
        shift, scale, gate = mod_r[0, 1:2, :], mod_r[0, 2:3, :], mod_r[0, 3:4, :]
        gf, gl = gf_r[...], gl_r[...]
        xn2, r2 = _rms(x1)
        hn2 = xn2 * gf
        h2 = (hn2 * (1.0 + scale) + shift).astype(BF16)
        h2_r[0] = h2
        ab = jnp.dot(h2, wu_r[...], preferred_element_type=F32)
        a, bb = ab[:, :D_FF], ab[:, D_FF:]
        sa = _sig(a)
        silu_a = a * sa
        hid = (silu_a * bb).astype(BF16)
        hid_r[0] = hid
        ffn = jnp.dot(hid, wd_r[...], preferred_element_type=F32)
        x2 = x1 + gate * ffn
        xn3, r3 = _rms(x2)
        err = xn3 * gl - tg_r[0]
        loss_r[...] += 0.5 * jnp.sum(jnp.mean(err * err, axis=-1, keepdims=True), axis=0, keepdims=True)
        dy = err * (1.0 / D_MODEL)
        dgl_r[...] += _colsum(dy * xn3)
        dx2 = _rms_bwd(dy * gl, xn3, r3)
        dmod_r[0, 3:4, :] += _colsum(dx2 * ffn)
        dffn = (gate * dx2).astype(BF16)
        dffn_r[0] = dffn
        dhid = lax.dot_general(dffn, wd_r[...], (((1,), (1,)), ((), ())), preferred_element_type=F32)
        da = dhid * bb * (sa * (1.0 + a * (1.0 - sa)))
        db = dhid * silu_a
        dab = jnp.concatenate([da, db], axis=1).astype(BF16)
        dab_r[0] = dab
        dh2 = lax.dot_general(dab, wu_r[...], (((1,), (1,)), ((), ())), preferred_element_type=F32)
        dmod_r[0, 1:2, :] += _colsum(dh2)
        dmod_r[0, 2:3, :] += _colsum(dh2 * hn2)
        dhn2 = dh2 * (1.0 + scale)
        dgf_r[...] += _colsum(dhn2 * xn2)
        dx1_r[0] = dx2 + _rms_bwd(dhn2 * gf, xn2, r2)

    lat = lambda w: pl.BlockSpec((1, ROW_BLOCK, w), lambda b, j: (b, j, 0))
    full2 = lambda r, c: pl.BlockSpec((r, c), lambda b, j: (0, 0))
    mod_spec = pl.BlockSpec((1, 8, D_MODEL), lambda b, j: (b, 0, 0))
    return pl.pallas_call(
        body, name="ffn", grid=(bs, nbl),
        in_specs=[lat(D_MODEL), lat(D_MODEL), mod_spec, full2(1, D_MODEL), full2(1, D_MODEL),
                  full2(D_MODEL, 2 * D_FF), full2(D_FF, D_MODEL)],
        out_specs=[lat(D_MODEL), lat(D_MODEL), lat(2 * D_FF), lat(D_FF), lat(D_MODEL),
                   full2(1, 1), full2(1, D_MODEL), full2(1, D_MODEL), mod_spec],
        out_shape=[SDS(x1.shape, F32), SDS(x1.shape, BF16), SDS((bs, seq, 2 * D_FF), BF16),
                   SDS((bs, seq, D_FF), BF16), SDS(x1.shape, BF16), SDS((1, 1), F32),
                   SDS((1, D_MODEL), F32), SDS((1, D_MODEL), F32), SDS((bs, 8, D_MODEL), F32)],
        compiler_params=_params(2),
    )(x1, target, mod_c, g_ffn, g_final, w_up, w_down)


def _row_tile(rows, most):
    return next(m * ROW_BLOCK for m in (9, 8, 4, 2, 1) if m <= most and rows % (m * ROW_BLOCK) == 0)


def _matmul_tn(a, b, n_blocks, tk, name, carry=None):
    t, m = a.shape
    n = b.shape[1]
    tn = n // n_blocks

    def body(a_ref, b_ref, o_ref):
        @pl.when(pl.program_id(1) == 0)
        def _():
            o_ref[...] = jnp.zeros_like(o_ref)
        o_ref[0] += _dot_tn(a_ref[...], b_ref[...])

    (out,), carried = _host_call(
        body, name=name, grid=(n_blocks, t // tk),
        in_specs=[pl.BlockSpec((tk, m), lambda i, k: (k, 0)), pl.BlockSpec((tk, tn), lambda i, k: (k, i))],
        out_specs=[pl.BlockSpec((1, m, tn), lambda i, k: (i, 0, 0))],
        out_shape=[SDS((n_blocks, m, tn), F32)], args=(a, b), carry=carry)
    return out if carry is None else (out, carried)


SMALL_ROWS = 80
ROW_CCTX = 3


def _small_reduce(gathered, lbg):
    def body(g_ref, lbg_ref, s_ref, dgam_ref):
        tot = g_ref[0:SMALL_ROWS, :]
        for dev in range(1, N_DEV):
            tot = tot + g_ref[dev * SMALL_ROWS:(dev + 1) * SMALL_ROWS, :]
        s_ref[...] = tot
        cc = g_ref[ROW_CCTX:ROW_CCTX + 1, :]
        for dev in range(2, N_DEV, 2):
            cc = cc + g_ref[dev * SMALL_ROWS + ROW_CCTX:dev * SMALL_ROWS + ROW_CCTX + 1, :]
        s_ref[ROW_CCTX:ROW_CCTX + 1, :] = cc
        dlb = tot[7:8, :]
        for d in range(2):
            s0 = _sig(lbg_ref[0, d:d + 1, :] - lbg_ref[1, d:d + 1, :])
            dgam_ref[d:d + 1, :] = dlb[:, d * KW:(d + 1) * KW] * s0 * (1.0 - s0)

    return pl.pallas_call(
        body, name="small_reduce", out_shape=[SDS((SMALL_ROWS, D_MODEL), F32), SDS((2, KW), F32)],
        in_specs=[VMEM_SPEC] * 2, out_specs=[VMEM_SPEC] * 2,
    )(gathered, lbg)


def _pad_cols(a, width):
    return jnp.pad(a, ((0, 0), (0, width - a.shape[1])))


def kernel(x, c, ctx, c_ctx, w_mod, b_mod, g_mix, g_ffn, w_in, lb_gamma, g_norm_a, ln_v_g, ln_v_b, w_s, b_s, w_pa, w_pb, w_o, w_up, w_down, g_final, loss_target, m_c_ctx, m_w_mod, m_b_mod, m_g_mix, m_g_ffn, m_w_in, m_lb_gamma, m_g_norm_a, m_ln_v_g, m_ln_v_b, m_w_s, m_b_s, m_w_pa, m_w_pb, m_w_o, m_w_up, m_w_down, m_g_final, v_c_ctx, v_w_mod, v_b_mod, v_g_mix, v_g_ffn, v_w_in, v_lb_gamma, v_g_norm_a, v_ln_v_g, v_ln_v_b, v_w_s, v_b_s, v_w_pa, v_w_pb, v_w_o, v_w_up, v_w_down, v_g_final):
    ax, ay, ac = lax.axis_index("x"), lax.axis_index("y"), lax.axis_index("c")
    kc = 2 * ax + ay
    dev = 2 * kc + ac
    pos = jnp.stack([kc, ac]).astype(jnp.int32)
    bs, seq, _ = x.shape
    assert bs <= 4 and ctx.shape[1] == ROW_BLOCK and seq % ROW_BLOCK == 0
    mod_cols = w_mod.shape[2]

    lbg_row = _pad_cols(lb_gamma.reshape(1, -1), D_MODEL)
    pay1 = jnp.concatenate([c, jnp.zeros((4 - bs, D_MODEL), F32), c_ctx[None, :], lbg_row,
                            jnp.zeros((2, D_MODEL), F32)], axis=0)
    b_mod_s = lax.dynamic_slice(b_mod, (0, kc * mod_cols), (1, mod_cols))
    cond64, mod_g = _cond_and_mod(pay1, w_mod[0], b_mod_s)
    lbg_full = cond64.reshape(N_DEV, 8, D_MODEL)[0::2, 5, :KW].reshape(N_CHIPS, 2, 2, HEAD_DIM)
    lbg_full = jnp.transpose(lbg_full, (1, 2, 0, 3)).reshape(2, 2, KW)
    mod_g = mod_g.reshape(N_DEV, 64, mod_cols)[0::2]
    shards = [w_in[0], w_up[0], w_pa[0], w_pb[0], w_o[0], w_down[0]]
    bufs = _cast_bf16(pos, shards)
    mod_full = jnp.transpose(mod_g, (1, 0, 2)).reshape(64, N_CHIPS * mod_cols)
    mod_mine = lax.dynamic_slice(mod_full, (dev * 8, 0), (8, 6 * D_MODEL)).reshape(8, 6, D_MODEL)
    mod, mc = mod_mine[:bs], mod_mine[4]
    zeros4 = jnp.zeros((bs, 4, D_MODEL), F32)
    mod_a = jnp.concatenate([mod[:, 0:2], jnp.broadcast_to(mc[None, 0:2], (bs, 2, D_MODEL)), zeros4], axis=1)
    mod_c = jnp.concatenate([mod[:, 2:6], zeros4], axis=1)

    def cols_major(a):
        return jnp.transpose(a, (1, 0, 2)).reshape(a.shape[1], -1)

    gna, lng, lnb = g_norm_a, ln_v_g, ln_v_b
    ws3 = w_s[0]
    ws3_t = jnp.transpose(ws3, (0, 2, 1))
    bst = jnp.transpose(b_s[0])

    (p, h_all), (w_in_g,) = _in_fwd_own(pos, x, ctx, g_mix, mod_a, w_in[0], carry=_carry_gather_send(bufs[:1]),
                                        then=_carry_gather_forward(bufs[:1]))
    p, sent_up = _in_fwd_rest(pos, h_all, w_in_g, p, carry=_carry_gather_send(bufs[1:2]))
    w_in_f = cols_major(w_in_g)
    fwd_rest = _carry_gather_forward(bufs[2:])
    then_rest = _Carry([], [], {}, fwd_rest.sems, lambda i, o, s: fwd_rest.copies(i[1:], o[1:], s))
    (o_f, s_f, o_b, s_b), gathered = _hgrn_fwd(
        p, lbg_full, carry=_merge_carries(_carry_gather_forward(sent_up), _carry_gather_send(bufs[2:])),
        then=then_rest)
    w_up_f, w_pa_f, w_pb_f = cols_major(gathered[0]), gathered[1], gathered[2]
    w_o_f = gathered[3].reshape(-1, D_MODEL)
    w_down_f = gathered[4].reshape(-1, D_MODEL)
    x1 = _mix_fwd(p, o_f, o_b, x, mod_c, gna, lng, lnb, ws3, bst, w_pa_f, w_pb_f, w_o_f)
    dx1, h2, dab, hid, dffn, loss_part, dg_final, dg_ffn, dmod_ffn = _ffn(
        x1, loss_target, mod_c, g_ffn, g_final[None, :], w_up_f, w_down_f)
    rows_lat = bs * seq
    dw_up = _matmul_tn(h2.reshape(rows_lat, D_MODEL), dab.reshape(rows_lat, 2 * D_FF), N_CHIPS,
                       _row_tile(rows_lat, 9), "dw_up")
    dw_down = _matmul_tn(hid.reshape(rows_lat, D_FF), dffn.reshape(rows_lat, D_MODEL), 1,
                         _row_tile(rows_lat, 4), "dw_down")

    part_ffn = [dw_up, dw_down.reshape(N_CHIPS, -1, D_MODEL)]
    (do_raw, dpc, dw_pa, dw_pb, dw_o, dgna, dlng, dlnb, dws, dbst, dmod_mix), sib_ffn = _mix_bwd(
        p, o_f, o_b, dx1, mod_c, gna, lng, lnb, ws3, ws3_t, bst, w_pa_f, w_pb_f, w_o_f,
        carry=_carry_sibling_halves(part_ffn))
    cpbf_ffn = _rs_add_halves(pos, part_ffn, sib_ffn, "rs_add_ffn")
    part_mix = [dw_pa, dw_pb, dw_o.reshape(N_CHIPS, -1, D_MODEL)]
    (df_f, dq_f, dv_f, dlb0, df_b, dq_b, dv_b, dlb1), got = _hgrn_bwd_pair(
        p, lbg_full, (s_f, s_b), do_raw,
        carry=_merge_carries(_carry_to_owner(cpbf_ffn), _carry_sibling_halves(part_mix)))
    own_ffn, sib_mix = got[:2], got[2:]
    half_ffn = _rs_sum_owner(pos, part_ffn, sib_ffn, own_ffn, "rs_sum_ffn")
    cpbf_mix = _rs_add_halves(pos, part_mix, sib_mix, "rs_add_mix")
    (grad_x, dp, dg_mix, dmod_in), _ = _in_bwd(x, ctx, dx1, g_mix, mod_a, w_in_f, df_f, df_b, dv_f, dv_b, dq_f, dq_b,
                                               dpc)

    rows_all = dp.shape[0] * dp.shape[1]
    tk_all = _row_tile(rows_all, 9)
    dw_in, got = _matmul_tn(h_all.reshape(rows_all, D_MODEL), dp.reshape(rows_all, IN_COLS), N_CHIPS, tk_all, "dw_in",
                            carry=_merge_carries(_carry_join_halves(half_ffn), _carry_to_owner(cpbf_mix)))
    g_ffn_w, own_mix = got[:2], got[2:]
    half_mix = _rs_sum_owner(pos, part_mix, sib_mix, own_mix, "rs_sum_mix")

    dmod_mine = jnp.concatenate([dmod_in[:, 0], dmod_in[:, 1], dmod_mix[:, 0], dmod_ffn[:, 1], dmod_ffn[:, 2],
                                 dmod_ffn[:, 3]], axis=1)
    dmc = jnp.concatenate([jnp.sum(dmod_in[:, 2], axis=0), jnp.sum(dmod_in[:, 3], axis=0),
                           jnp.zeros((4 * D_MODEL,), F32)])[None, :]
    pay3 = jnp.concatenate([dmod_mine, jnp.zeros((4 - bs, 6 * D_MODEL), F32), dmc,
                            jnp.zeros((3, 6 * D_MODEL), F32)], axis=0)
    dmod64, got = _all_gather8(pay3, "gather_dmod", carry=_merge_carries(_carry_sibling_halves([dw_in]),
                                                                          _carry_join_halves(half_mix)))
    sib_in, g_mix_w = got[:1], got[1:]
    cpbf_in = _rs_add_halves(pos, [dw_in], sib_in, "rs_add_w_in")
    dmod64_my = lax.dynamic_slice(dmod64, (0, kc * mod_cols), (64, mod_cols))
    g_w_mod, g_b_mod, g_cctx_part = _mod_bwd(cond64, dmod64, dmod64_my, w_mod[0], c_ctx[None, :])

    def row(*parts):
        return _pad_cols(jnp.concatenate([q.reshape(1, -1) for q in parts], axis=1), D_MODEL)

    small_rows = [dg_mix, dg_ffn, dg_final, g_cctx_part, row(dgna), row(dlng, dlnb), row(jnp.transpose(dbst)),
                  row(dlb0, dlb1), row(loss_part), jnp.zeros((7, D_MODEL), F32), dws.reshape(64, D_MODEL)]
    pay4 = jnp.concatenate(small_rows, axis=0)
    tot, dgam0 = _small_reduce(_all_gather8(pay4, "gather_small"), lbg_full)

    own_sems, own_src, own_land, own_token = _owner_send_start(cpbf_in[0], after=tot)

    rest_names = ["w_up", "w_pa", "w_pb", "w_o", "w_down", "w_mod"]
    rest_w = shards[1:] + [w_mod[0]]
    rest_g = [g_ffn_w[0], g_mix_w[0] + own_token[0, 0], g_mix_w[1], g_mix_w[2], g_ffn_w[1], g_w_mod]
    rest_m = [m_w_up[0], m_w_pa[0], m_w_pb[0], m_w_o[0], m_w_down[0], m_w_mod[0]]
    rest_v = [v_w_up[0], v_w_pa[0], v_w_pb[0], v_w_o[0], v_w_down[0], v_w_mod[0]]
    (ds_r, m2s_r, v2s_r, gs_r), _ = _adamw_group(rest_w, rest_g, rest_m, rest_v, "adamw_rest")
    res = {}
    for name, g, d, m2, v2 in zip(rest_names, gs_r, ds_r, m2s_r, v2s_r):
        res[name] = (g[None], d[None], m2[None], v2[None])
    own_in = [_owner_send_wait(own_sems, own_src, own_land, after=(ds_r[0],))]
    g_in_w = _comm_call("rs_join_w_in",
                        _carry_join_halves(_rs_sum_owner(pos, [dw_in], sib_in, own_in, "rs_sum_w_in")))
    d, m2, v2, g_in_out = _adamw_big(shards[0], g_in_w[0], m_w_in[0], v_w_in[0], "adamw_w_in")
    res["w_in"] = (g_in_out[None], d[None], m2[None], v2[None])

    loss = tot[8, 0]
    dgam_full = jnp.stack([dgam0, -dgam0])
    g_lbg = lax.dynamic_slice(dgam_full, (0, 0, kc * HEAD_DIM), (2, 2, HEAD_DIM))

    small = [
        ("c_ctx", c_ctx[None, :], tot[3:4], m_c_ctx, v_c_ctx),
        ("b_mod", b_mod, g_b_mod, m_b_mod, v_b_mod),
        ("g_mix", g_mix, tot[0:1], m_g_mix, v_g_mix),
        ("g_ffn", g_ffn, tot[1:2], m_g_ffn, v_g_ffn),
        ("lb_gamma", lb_gamma.reshape(4, HEAD_DIM), g_lbg.reshape(4, HEAD_DIM), m_lb_gamma, v_lb_gamma),
        ("g_norm_a", g_norm_a, tot[4:5, :HEAD_DIM], m_g_norm_a, v_g_norm_a),
        ("ln_v_g", ln_v_g, tot[5:6, :KW], m_ln_v_g, v_ln_v_g),
        ("ln_v_b", ln_v_b, tot[5:6, KW:], m_ln_v_b, v_ln_v_b),
        ("w_s", w_s.reshape(N_HEADS * SGU_CHUNK, SGU_CHUNK), tot[16:80].reshape(N_HEADS * SGU_CHUNK, SGU_CHUNK),
         m_w_s, v_w_s),
        ("b_s", b_s[0], tot[6:7, :KW].reshape(N_HEADS, SGU_CHUNK), m_b_s, v_b_s),
        ("g_final", g_final[None, :], tot[2:3], m_g_final, v_g_final),
    ]
    ws_, gs_ = [s[1] for s in small], [s[2] for s in small]
    ms_ = [s[3].reshape(s[1].shape) for s in small]
    vs_ = [s[4].reshape(s[1].shape) for s in small]
    ds_, m2s_, v2s_ = _adamw_small(ws_, gs_, ms_, vs_)
    for (name, _, g, m, _), d, m2, v2 in zip(small, ds_, m2s_, v2s_):
        res[name] = tuple(t.reshape(m.shape) for t in (g, d, m2, v2))

    order = ["c_ctx", "w_mod", "b_mod", "g_mix", "g_ffn", "w_in", "lb_gamma", "g_norm_a", "ln_v_g", "ln_v_b",
             "w_s", "b_s", "w_pa", "w_pb", "w_o", "w_up", "w_down", "g_final"]
    outs = [loss, grad_x]
    for part in range(4):
        outs += [res[n][part] for n in order]
    return tuple(outs)
```
